```python
import jax, jax.numpy as jnp
from jax import lax
import numpy as np

D_MODEL = 2048
BATCH = 8
SEQ = 4096
DEPTH = 1

N_META = 16
D_LRU = D_MODEL // 2
N_LRU_HEADS = 16
LRU_BLOCK = D_LRU // N_LRU_HEADS
LRU_CONV_WIDTH = 4
LRU_C = 8.0
D_SCONV = D_MODEL - D_LRU
N_SCONV_GROUPS = 16
SCONV_BLOCK = D_SCONV // N_SCONV_GROUPS
SCONV_WIDTH = 3
D_FF = ((8 * D_MODEL // 3 + 127) // 128) * 128
IN_COLS = 2 * D_LRU + 3 * D_SCONV
EPS = 1e-6

kernel_name = "hymba_lru_shortconv_macaron"


def rmsnorm(x, g):
    xf = x.astype(jnp.float32)
    y = xf * lax.rsqrt(jnp.mean(xf * xf, axis=-1, keepdims=True) + EPS)
    return (y * g.astype(jnp.float32)).astype(x.dtype)


def group_rmsnorm(x, g, n_groups):
    b, t, c = x.shape
    xf = x.astype(jnp.float32).reshape(b, t, n_groups, c // n_groups)
    y = xf * lax.rsqrt(jnp.mean(xf * xf, axis=-1, keepdims=True) + EPS)
    return (y.reshape(b, t, c) * g.astype(jnp.float32)).astype(x.dtype)


def causal_depthwise_conv(x, w):
    k = w.shape[0]
    return lax.conv_general_dilated(
        x, w[:, None, :].astype(x.dtype), window_strides=(1,),
        padding=[(k - 1, 0)], dimension_numbers=("NWC", "WIO", "NWC"),
        feature_group_count=x.shape[-1])


def swiglu(x, w_gate, w_up, w_down):
    return (jax.nn.silu(x @ w_gate) * (x @ w_up)) @ w_down


def rg_lru(x, w_a, b_a, w_x, b_x, lam):
    bn, t, c = x.shape
    xh = x.reshape(bn, t, N_LRU_HEADS, LRU_BLOCK)
    gate_a = jax.nn.sigmoid(jnp.einsum("bthi,hij->bthj", xh, w_a).reshape(bn, t, c) + b_a)
    gate_x = jax.nn.sigmoid(jnp.einsum("bthi,hij->bthj", xh, w_x).reshape(bn, t, c) + b_x)
    log_a = -LRU_C * gate_a.astype(jnp.float32) * jax.nn.softplus(-lam.astype(jnp.float32))
    a = jnp.exp(log_a)
    mult = jnp.sqrt(-jnp.expm1(2.0 * log_a))
    u = mult * (gate_x * x).astype(jnp.float32)

    def combine(left, right):
        a_l, b_l = left
        a_r, b_r = right
        return a_r * a_l, a_r * b_l + b_r

    _, h = lax.associative_scan(combine, (a, u), axis=1)
    return h.astype(x.dtype)


def _fwd_setup_inputs(seed: int = 0) -> dict:
    key = jax.random.key(seed)
    ks = iter(jax.random.split(key, 40))
    f32 = jnp.float32
    L = DEPTH

    def nrm(shape, fan_in):
        return jax.random.normal(next(ks), shape, f32) * (fan_in ** -0.5)

    def gain(shape):
        return 1.0 + 0.02 * jax.random.normal(next(ks), shape, f32)

    def bias(shape):
        return 0.01 * jax.random.normal(next(ks), shape, f32)

    x = jax.random.normal(next(ks), (BATCH, SEQ, D_MODEL), f32)
    meta_tokens = jax.random.normal(next(ks), (N_META, D_MODEL), f32)

    a_c = jax.random.uniform(next(ks), (L, D_LRU), f32, 0.9, 0.999)
    s = a_c ** (1.0 / LRU_C)
    lru_lambda = jnp.log(s) - jnp.log1p(-s)

    return {
        "x": x,
        "meta_tokens": meta_tokens,
        "ffn1_pre_g": gain((L, D_MODEL)),
        "ffn1_w_gate": nrm((L, D_MODEL, D_FF), D_MODEL),
        "ffn1_w_up": nrm((L, D_MODEL, D_FF), D_MODEL),
        "ffn1_w_down": nrm((L, D_FF, D_MODEL), D_FF),
        "ffn1_post_g": gain((L, D_MODEL)),
        "mix_pre_g": gain((L, D_MODEL)),
        "w_in": nrm((L, D_MODEL, IN_COLS), D_MODEL),
        "lru_conv_w": nrm((L, LRU_CONV_WIDTH, D_LRU), LRU_CONV_WIDTH),
        "lru_conv_b": bias((L, D_LRU)),
        "lru_w_a": nrm((L, N_LRU_HEADS, LRU_BLOCK, LRU_BLOCK), LRU_BLOCK),
        "lru_b_a": bias((L, D_LRU)),
        "lru_w_x": nrm((L, N_LRU_HEADS, LRU_BLOCK, LRU_BLOCK), LRU_BLOCK),
        "lru_b_x": bias((L, D_LRU)),
        "lru_lambda": lru_lambda,
        "sconv_w": nrm((L, SCONV_WIDTH, D_SCONV), SCONV_WIDTH),
        "lru_out_g": gain((L, D_LRU)),
        "sconv_out_g": gain((L, D_SCONV)),
        "w_out": nrm((L, D_MODEL, D_MODEL), D_MODEL),
        "mix_post_g": gain((L, D_MODEL)),
        "ffn2_pre_g": gain((L, D_MODEL)),
        "ffn2_w_gate": nrm((L, D_MODEL, D_FF), D_MODEL),
        "ffn2_w_up": nrm((L, D_MODEL, D_FF), D_MODEL),
        "ffn2_w_down": nrm((L, D_FF, D_MODEL), D_FF),
        "ffn2_post_g": gain((L, D_MODEL)),
    }


def _fwd_reference(x, meta_tokens, ffn1_pre_g, ffn1_w_gate, ffn1_w_up, ffn1_w_down, ffn1_post_g,
              mix_pre_g, w_in, lru_conv_w, lru_conv_b, lru_w_a, lru_b_a, lru_w_x, lru_b_x,
              lru_lambda, sconv_w, lru_out_g, sconv_out_g, w_out, mix_post_g,
              ffn2_pre_g, ffn2_w_gate, ffn2_w_up, ffn2_w_down, ffn2_post_g):
    bn = x.shape[0]
    meta = jnp.broadcast_to(meta_tokens.astype(x.dtype)[None], (bn, N_META, x.shape[-1]))
    h = jnp.concatenate([meta, x], axis=1)
    splits = [D_LRU, 2 * D_LRU, 2 * D_LRU + D_SCONV, 2 * D_LRU + 2 * D_SCONV]

    for l in range(DEPTH):
        f = swiglu(rmsnorm(h, ffn1_pre_g[l]), ffn1_w_gate[l], ffn1_w_up[l], ffn1_w_down[l])
        h = h + 0.5 * rmsnorm(f, ffn1_post_g[l])

        u = rmsnorm(h, mix_pre_g[l])
        z = u @ w_in[l]
        y_lru, x_lru, b_sc, c_sc, v_sc = jnp.split(z, splits, axis=-1)

        x_lru = causal_depthwise_conv(x_lru, lru_conv_w[l]) + lru_conv_b[l]
        lru_out = rg_lru(x_lru, lru_w_a[l], lru_b_a[l], lru_w_x[l], lru_b_x[l], lru_lambda[l])
        lru_out = lru_out * jax.nn.gelu(y_lru, approximate=True)

        sc_out = b_sc * causal_depthwise_conv(c_sc * v_sc, sconv_w[l])

        mixed = jnp.concatenate([
            group_rmsnorm(lru_out, lru_out_g[l], N_LRU_HEADS),
            group_rmsnorm(sc_out, sconv_out_g[l], N_SCONV_GROUPS)], axis=-1)
        h = h + rmsnorm(mixed @ w_out[l], mix_post_g[l])

        f = swiglu(rmsnorm(h, ffn2_pre_g[l]), ffn2_w_gate[l], ffn2_w_up[l], ffn2_w_down[l])
        h = h + 0.5 * rmsnorm(f, ffn2_post_g[l])

    return h[:, N_META:]


import jax as _jax
import jax.numpy as _jnp

TWIN_FORMAT = 'train_step'
FWD_PARAMS = ['x', 'meta_tokens', 'ffn1_pre_g', 'ffn1_w_gate', 'ffn1_w_up', 'ffn1_w_down', 'ffn1_post_g', 'mix_pre_g', 'w_in', 'lru_conv_w', 'lru_conv_b', 'lru_w_a', 'lru_b_a', 'lru_w_x', 'lru_b_x', 'lru_lambda', 'sconv_w', 'lru_out_g', 'sconv_out_g', 'w_out', 'mix_post_g', 'ffn2_pre_g', 'ffn2_w_gate', 'ffn2_w_up', 'ffn2_w_down', 'ffn2_post_g']
TWIN_WEIGHTS = ['meta_tokens', 'ffn1_pre_g', 'ffn1_w_gate', 'ffn1_w_up', 'ffn1_w_down', 'ffn1_post_g', 'mix_pre_g', 'w_in', 'lru_conv_w', 'lru_conv_b', 'lru_w_a', 'lru_b_a', 'lru_w_x', 'lru_b_x', 'lru_lambda', 'sconv_w', 'lru_out_g', 'sconv_out_g', 'w_out', 'mix_post_g', 'ffn2_pre_g', 'ffn2_w_gate', 'ffn2_w_up', 'ffn2_w_down', 'ffn2_post_g']
TWIN_DIFF_INPUT = 'x'
TWIN_INPUTS = ['x', 'meta_tokens', 'ffn1_pre_g', 'ffn1_w_gate', 'ffn1_w_up', 'ffn1_w_down', 'ffn1_post_g', 'mix_pre_g', 'w_in', 'lru_conv_w', 'lru_conv_b', 'lru_w_a', 'lru_b_a', 'lru_w_x', 'lru_b_x', 'lru_lambda', 'sconv_w', 'lru_out_g', 'sconv_out_g', 'w_out', 'mix_post_g', 'ffn2_pre_g', 'ffn2_w_gate', 'ffn2_w_up', 'ffn2_w_down', 'ffn2_post_g', 'loss_target', 'm_meta_tokens', 'm_ffn1_pre_g', 'm_ffn1_w_gate', 'm_ffn1_w_up', 'm_ffn1_w_down', 'm_ffn1_post_g', 'm_mix_pre_g', 'm_w_in', 'm_lru_conv_w', 'm_lru_conv_b', 'm_lru_w_a', 'm_lru_b_a', 'm_lru_w_x', 'm_lru_b_x', 'm_lru_lambda', 'm_sconv_w', 'm_lru_out_g', 'm_sconv_out_g', 'm_w_out', 'm_mix_post_g', 'm_ffn2_pre_g', 'm_ffn2_w_gate', 'm_ffn2_w_up', 'm_ffn2_w_down', 'm_ffn2_post_g', 'v_meta_tokens', 'v_ffn1_pre_g', 'v_ffn1_w_gate', 'v_ffn1_w_up', 'v_ffn1_w_down', 'v_ffn1_post_g', 'v_mix_pre_g', 'v_w_in', 'v_lru_conv_w', 'v_lru_conv_b', 'v_lru_w_a', 'v_lru_b_a', 'v_lru_w_x', 'v_lru_b_x', 'v_lru_lambda', 'v_sconv_w', 'v_lru_out_g', 'v_sconv_out_g', 'v_w_out', 'v_mix_post_g', 'v_ffn2_pre_g', 'v_ffn2_w_gate', 'v_ffn2_w_up', 'v_ffn2_w_down', 'v_ffn2_post_g']
TWIN_OUTPUTS = ['loss', 'grad_x', 'grad_meta_tokens', 'grad_ffn1_pre_g', 'grad_ffn1_w_gate', 'grad_ffn1_w_up', 'grad_ffn1_w_down', 'grad_ffn1_post_g', 'grad_mix_pre_g', 'grad_w_in', 'grad_lru_conv_w', 'grad_lru_conv_b', 'grad_lru_w_a', 'grad_lru_b_a', 'grad_lru_w_x', 'grad_lru_b_x', 'grad_lru_lambda', 'grad_sconv_w', 'grad_lru_out_g', 'grad_sconv_out_g', 'grad_w_out', 'grad_mix_post_g', 'grad_ffn2_pre_g', 'grad_ffn2_w_gate', 'grad_ffn2_w_up', 'grad_ffn2_w_down', 'grad_ffn2_post_g', 'delta_meta_tokens', 'delta_ffn1_pre_g', 'delta_ffn1_w_gate', 'delta_ffn1_w_up', 'delta_ffn1_w_down', 'delta_ffn1_post_g', 'delta_mix_pre_g', 'delta_w_in', 'delta_lru_conv_w', 'delta_lru_conv_b', 'delta_lru_w_a', 'delta_lru_b_a', 'delta_lru_w_x', 'delta_lru_b_x', 'delta_lru_lambda', 'delta_sconv_w', 'delta_lru_out_g', 'delta_sconv_out_g', 'delta_w_out', 'delta_mix_post_g', 'delta_ffn2_pre_g', 'delta_ffn2_w_gate', 'delta_ffn2_w_up', 'delta_ffn2_w_down', 'delta_ffn2_post_g', 'new_m_meta_tokens', 'new_m_ffn1_pre_g', 'new_m_ffn1_w_gate', 'new_m_ffn1_w_up', 'new_m_ffn1_w_down', 'new_m_ffn1_post_g', 'new_m_mix_pre_g', 'new_m_w_in', 'new_m_lru_conv_w', 'new_m_lru_conv_b', 'new_m_lru_w_a', 'new_m_lru_b_a', 'new_m_lru_w_x', 'new_m_lru_b_x', 'new_m_lru_lambda', 'new_m_sconv_w', 'new_m_lru_out_g', 'new_m_sconv_out_g', 'new_m_w_out', 'new_m_mix_post_g', 'new_m_ffn2_pre_g', 'new_m_ffn2_w_gate', 'new_m_ffn2_w_up', 'new_m_ffn2_w_down', 'new_m_ffn2_post_g', 'new_v_meta_tokens', 'new_v_ffn1_pre_g', 'new_v_ffn1_w_gate', 'new_v_ffn1_w_up', 'new_v_ffn1_w_down', 'new_v_ffn1_post_g', 'new_v_mix_pre_g', 'new_v_w_in', 'new_v_lru_conv_w', 'new_v_lru_conv_b', 'new_v_lru_w_a', 'new_v_lru_b_a', 'new_v_lru_w_x', 'new_v_lru_b_x', 'new_v_lru_lambda', 'new_v_sconv_w', 'new_v_lru_out_g', 'new_v_sconv_out_g', 'new_v_w_out', 'new_v_mix_post_g', 'new_v_ffn2_pre_g', 'new_v_ffn2_w_gate', 'new_v_ffn2_w_up', 'new_v_ffn2_w_down', 'new_v_ffn2_post_g']
TWIN_LEAF_KINDS = {'loss': 'loss', 'grad_x': 'grad_x', 'grad_meta_tokens': 'grad_w', 'grad_ffn1_pre_g': 'grad_w', 'grad_ffn1_w_gate': 'grad_w', 'grad_ffn1_w_up': 'grad_w', 'grad_ffn1_w_down': 'grad_w', 'grad_ffn1_post_g': 'grad_w', 'grad_mix_pre_g': 'grad_w', 'grad_w_in': 'grad_w', 'grad_lru_conv_w': 'grad_w', 'grad_lru_conv_b': 'grad_w', 'grad_lru_w_a': 'grad_w', 'grad_lru_b_a': 'grad_w', 'grad_lru_w_x': 'grad_w', 'grad_lru_b_x': 'grad_w', 'grad_lru_lambda': 'grad_w', 'grad_sconv_w': 'grad_w', 'grad_lru_out_g': 'grad_w', 'grad_sconv_out_g': 'grad_w', 'grad_w_out': 'grad_w', 'grad_mix_post_g': 'grad_w', 'grad_ffn2_pre_g': 'grad_w', 'grad_ffn2_w_gate': 'grad_w', 'grad_ffn2_w_up': 'grad_w', 'grad_ffn2_w_down': 'grad_w', 'grad_ffn2_post_g': 'grad_w', 'delta_meta_tokens': 'delta_w', 'delta_ffn1_pre_g': 'delta_w', 'delta_ffn1_w_gate': 'delta_w', 'delta_ffn1_w_up': 'delta_w', 'delta_ffn1_w_down': 'delta_w', 'delta_ffn1_post_g': 'delta_w', 'delta_mix_pre_g': 'delta_w', 'delta_w_in': 'delta_w', 'delta_lru_conv_w': 'delta_w', 'delta_lru_conv_b': 'delta_w', 'delta_lru_w_a': 'delta_w', 'delta_lru_b_a': 'delta_w', 'delta_lru_w_x': 'delta_w', 'delta_lru_b_x': 'delta_w', 'delta_lru_lambda': 'delta_w', 'delta_sconv_w': 'delta_w', 'delta_lru_out_g': 'delta_w', 'delta_sconv_out_g': 'delta_w', 'delta_w_out': 'delta_w', 'delta_mix_post_g': 'delta_w', 'delta_ffn2_pre_g': 'delta_w', 'delta_ffn2_w_gate': 'delta_w', 'delta_ffn2_w_up': 'delta_w', 'delta_ffn2_w_down': 'delta_w', 'delta_ffn2_post_g': 'delta_w', 'new_m_meta_tokens': 'new_m', 'new_m_ffn1_pre_g': 'new_m', 'new_m_ffn1_w_gate': 'new_m', 'new_m_ffn1_w_up': 'new_m', 'new_m_ffn1_w_down': 'new_m', 'new_m_ffn1_post_g': 'new_m', 'new_m_mix_pre_g': 'new_m', 'new_m_w_in': 'new_m', 'new_m_lru_conv_w': 'new_m', 'new_m_lru_conv_b': 'new_m', 'new_m_lru_w_a': 'new_m', 'new_m_lru_b_a': 'new_m', 'new_m_lru_w_x': 'new_m', 'new_m_lru_b_x': 'new_m', 'new_m_lru_lambda': 'new_m', 'new_m_sconv_w': 'new_m', 'new_m_lru_out_g': 'new_m', 'new_m_sconv_out_g': 'new_m', 'new_m_w_out': 'new_m', 'new_m_mix_post_g': 'new_m', 'new_m_ffn2_pre_g': 'new_m', 'new_m_ffn2_w_gate': 'new_m', 'new_m_ffn2_w_up': 'new_m', 'new_m_ffn2_w_down': 'new_m', 'new_m_ffn2_post_g': 'new_m', 'new_v_meta_tokens': 'new_v', 'new_v_ffn1_pre_g': 'new_v', 'new_v_ffn1_w_gate': 'new_v', 'new_v_ffn1_w_up': 'new_v', 'new_v_ffn1_w_down': 'new_v', 'new_v_ffn1_post_g': 'new_v', 'new_v_mix_pre_g': 'new_v', 'new_v_w_in': 'new_v', 'new_v_lru_conv_w': 'new_v', 'new_v_lru_conv_b': 'new_v', 'new_v_lru_w_a': 'new_v', 'new_v_lru_b_a': 'new_v', 'new_v_lru_w_x': 'new_v', 'new_v_lru_b_x': 'new_v', 'new_v_lru_lambda': 'new_v', 'new_v_sconv_w': 'new_v', 'new_v_lru_out_g': 'new_v', 'new_v_sconv_out_g': 'new_v', 'new_v_w_out': 'new_v', 'new_v_mix_post_g': 'new_v', 'new_v_ffn2_pre_g': 'new_v', 'new_v_ffn2_w_gate': 'new_v', 'new_v_ffn2_w_up': 'new_v', 'new_v_ffn2_w_down': 'new_v', 'new_v_ffn2_post_g': 'new_v'}


def _forward(args):
    return _fwd_reference(*[args[k] for k in FWD_PARAMS])


def _output_shape():
    def fwd():
        inp = _fwd_setup_inputs(0)
        return _fwd_reference(*[inp[k] for k in FWD_PARAMS])
    out = _jax.eval_shape(fwd)
    return out.shape, out.dtype

N_MICROBATCH = 1
ADAM_LR = 0.001
ADAM_B1 = 0.9
ADAM_B2 = 0.999
ADAM_EPS = 1e-08
ADAM_WD = 0.01
ADAM_STEP = 10
PER_EXAMPLE_BATCH_AXIS = {'x': 0, 'loss_target': 0}
SHARED_INPUTS = []
_WEIGHT_DTYPES = {'meta_tokens': _jnp.float32, 'ffn1_pre_g': _jnp.float32, 'ffn1_w_gate': _jnp.float32, 'ffn1_w_up': _jnp.float32, 'ffn1_w_down': _jnp.float32, 'ffn1_post_g': _jnp.float32, 'mix_pre_g': _jnp.float32, 'w_in': _jnp.float32, 'lru_conv_w': _jnp.float32, 'lru_conv_b': _jnp.float32, 'lru_w_a': _jnp.float32, 'lru_b_a': _jnp.float32, 'lru_w_x': _jnp.float32, 'lru_b_x': _jnp.float32, 'lru_lambda': _jnp.float32, 'sconv_w': _jnp.float32, 'lru_out_g': _jnp.float32, 'sconv_out_g': _jnp.float32, 'w_out': _jnp.float32, 'mix_post_g': _jnp.float32, 'ffn2_pre_g': _jnp.float32, 'ffn2_w_gate': _jnp.float32, 'ffn2_w_up': _jnp.float32, 'ffn2_w_down': _jnp.float32, 'ffn2_post_g': _jnp.float32}
MOMENT_SCALE = {'meta_tokens': 1.026312e-02, 'ffn1_pre_g': 2.227085e-01, 'ffn1_w_gate': 9.424258e-02, 'ffn1_w_up': 9.519821e-02, 'ffn1_w_down': 1.567369e-01, 'ffn1_post_g': 3.982038e+00, 'mix_pre_g': 2.684642e-01, 'w_in': 1.754366e-01, 'lru_conv_w': 2.120942e-01, 'lru_conv_b': 3.689456e+00, 'lru_w_a': 1.027551e-01, 'lru_b_a': 6.131692e-02, 'lru_w_x': 1.911208e-01, 'lru_b_x': 5.606226e-02, 'lru_lambda': 9.956102e-02, 'sconv_w': 1.739369e-01, 'lru_out_g': 2.349918e-01, 'sconv_out_g': 1.890252e-01, 'w_out': 2.005242e-01, 'mix_post_g': 1.604039e+01, 'ffn2_pre_g': 1.565798e-01, 'ffn2_w_gate': 5.478775e-02, 'ffn2_w_up': 7.540545e-02, 'ffn2_w_down': 1.241020e-01, 'ffn2_post_g': 3.984181e+00}


def _to_microbatches(a, axis):
    t = _jnp.moveaxis(a, axis, 0)
    t = t.reshape((N_MICROBATCH, t.shape[0] // N_MICROBATCH) + t.shape[1:])
    return _jnp.moveaxis(t, 1, axis + 1)


def setup_inputs(seed: int = 0) -> dict:
    inp = _fwd_setup_inputs(seed)
    key = _jax.random.fold_in(_jax.random.key(seed), 7919)
    shape, _ = _output_shape()
    out = dict(inp)
    out["loss_target"] = _jax.random.normal(_jax.random.fold_in(key, 0), shape, _jnp.float32)
    for i, name in enumerate(TWIN_WEIGHTS):
        w = inp[name].astype(_jnp.float32)
        if MOMENT_SCALE is None:
            s = _jnp.sqrt(_jnp.mean(_jnp.square(w)) + 1e-30)
        else:
            s = MOMENT_SCALE[name]
        km, kv = _jax.random.split(_jax.random.fold_in(key, i + 1))
        out[name] = w
        out["m_" + name] = s * _jax.random.normal(km, w.shape, _jnp.float32)
        out["v_" + name] = (s * s) * _jax.random.uniform(kv, w.shape, _jnp.float32, 0.5, 1.5)
    if N_MICROBATCH > 1:
        for name, axis in PER_EXAMPLE_BATCH_AXIS.items():
            out[name] = _to_microbatches(out[name], axis)
    return {'x': out['x'], 'meta_tokens': out['meta_tokens'], 'ffn1_pre_g': out['ffn1_pre_g'], 'ffn1_w_gate': out['ffn1_w_gate'], 'ffn1_w_up': out['ffn1_w_up'], 'ffn1_w_down': out['ffn1_w_down'], 'ffn1_post_g': out['ffn1_post_g'], 'mix_pre_g': out['mix_pre_g'], 'w_in': out['w_in'], 'lru_conv_w': out['lru_conv_w'], 'lru_conv_b': out['lru_conv_b'], 'lru_w_a': out['lru_w_a'], 'lru_b_a': out['lru_b_a'], 'lru_w_x': out['lru_w_x'], 'lru_b_x': out['lru_b_x'], 'lru_lambda': out['lru_lambda'], 'sconv_w': out['sconv_w'], 'lru_out_g': out['lru_out_g'], 'sconv_out_g': out['sconv_out_g'], 'w_out': out['w_out'], 'mix_post_g': out['mix_post_g'], 'ffn2_pre_g': out['ffn2_pre_g'], 'ffn2_w_gate': out['ffn2_w_gate'], 'ffn2_w_up': out['ffn2_w_up'], 'ffn2_w_down': out['ffn2_w_down'], 'ffn2_post_g': out['ffn2_post_g'], 'loss_target': out['loss_target'], 'm_meta_tokens': out['m_meta_tokens'], 'm_ffn1_pre_g': out['m_ffn1_pre_g'], 'm_ffn1_w_gate': out['m_ffn1_w_gate'], 'm_ffn1_w_up': out['m_ffn1_w_up'], 'm_ffn1_w_down': out['m_ffn1_w_down'], 'm_ffn1_post_g': out['m_ffn1_post_g'], 'm_mix_pre_g': out['m_mix_pre_g'], 'm_w_in': out['m_w_in'], 'm_lru_conv_w': out['m_lru_conv_w'], 'm_lru_conv_b': out['m_lru_conv_b'], 'm_lru_w_a': out['m_lru_w_a'], 'm_lru_b_a': out['m_lru_b_a'], 'm_lru_w_x': out['m_lru_w_x'], 'm_lru_b_x': out['m_lru_b_x'], 'm_lru_lambda': out['m_lru_lambda'], 'm_sconv_w': out['m_sconv_w'], 'm_lru_out_g': out['m_lru_out_g'], 'm_sconv_out_g': out['m_sconv_out_g'], 'm_w_out': out['m_w_out'], 'm_mix_post_g': out['m_mix_post_g'], 'm_ffn2_pre_g': out['m_ffn2_pre_g'], 'm_ffn2_w_gate': out['m_ffn2_w_gate'], 'm_ffn2_w_up': out['m_ffn2_w_up'], 'm_ffn2_w_down': out['m_ffn2_w_down'], 'm_ffn2_post_g': out['m_ffn2_post_g'], 'v_meta_tokens': out['v_meta_tokens'], 'v_ffn1_pre_g': out['v_ffn1_pre_g'], 'v_ffn1_w_gate': out['v_ffn1_w_gate'], 'v_ffn1_w_up': out['v_ffn1_w_up'], 'v_ffn1_w_down': out['v_ffn1_w_down'], 'v_ffn1_post_g': out['v_ffn1_post_g'], 'v_mix_pre_g': out['v_mix_pre_g'], 'v_w_in': out['v_w_in'], 'v_lru_conv_w': out['v_lru_conv_w'], 'v_lru_conv_b': out['v_lru_conv_b'], 'v_lru_w_a': out['v_lru_w_a'], 'v_lru_b_a': out['v_lru_b_a'], 'v_lru_w_x': out['v_lru_w_x'], 'v_lru_b_x': out['v_lru_b_x'], 'v_lru_lambda': out['v_lru_lambda'], 'v_sconv_w': out['v_sconv_w'], 'v_lru_out_g': out['v_lru_out_g'], 'v_sconv_out_g': out['v_sconv_out_g'], 'v_w_out': out['v_w_out'], 'v_mix_post_g': out['v_mix_post_g'], 'v_ffn2_pre_g': out['v_ffn2_pre_g'], 'v_ffn2_w_gate': out['v_ffn2_w_gate'], 'v_ffn2_w_up': out['v_ffn2_w_up'], 'v_ffn2_w_down': out['v_ffn2_w_down'], 'v_ffn2_post_g': out['v_ffn2_post_g']}


def _loss(weights, diff, rest, loss_target):
    with _jax.named_scope("forward"):
        args = {**rest, TWIN_DIFF_INPUT: diff, **{k: w.astype(_WEIGHT_DTYPES[k]) for k, w in weights.items()}}
        y = _forward(args)
    with _jax.named_scope("loss_head"):
        err = _jnp.square(y.astype(_jnp.float32) - loss_target)
        return 0.5 * _jnp.sum(_jnp.mean(err, axis=-1)) if err.ndim else 0.5 * err


def _adamw(w, g, m, v):
    m = ADAM_B1 * m + (1.0 - ADAM_B1) * g
    v = ADAM_B2 * v + (1.0 - ADAM_B2) * _jnp.square(g)
    m_hat = m / (1.0 - ADAM_B1 ** ADAM_STEP)
    v_hat = v / (1.0 - ADAM_B2 ** ADAM_STEP)
    delta = -ADAM_LR * (m_hat / (_jnp.sqrt(v_hat) + ADAM_EPS) + ADAM_WD * w)
    return delta, m, v


def reference(x, meta_tokens, ffn1_pre_g, ffn1_w_gate, ffn1_w_up, ffn1_w_down, ffn1_post_g, mix_pre_g, w_in, lru_conv_w, lru_conv_b, lru_w_a, lru_b_a, lru_w_x, lru_b_x, lru_lambda, sconv_w, lru_out_g, sconv_out_g, w_out, mix_post_g, ffn2_pre_g, ffn2_w_gate, ffn2_w_up, ffn2_w_down, ffn2_post_g, loss_target, m_meta_tokens, m_ffn1_pre_g, m_ffn1_w_gate, m_ffn1_w_up, m_ffn1_w_down, m_ffn1_post_g, m_mix_pre_g, m_w_in, m_lru_conv_w, m_lru_conv_b, m_lru_w_a, m_lru_b_a, m_lru_w_x, m_lru_b_x, m_lru_lambda, m_sconv_w, m_lru_out_g, m_sconv_out_g, m_w_out, m_mix_post_g, m_ffn2_pre_g, m_ffn2_w_gate, m_ffn2_w_up, m_ffn2_w_down, m_ffn2_post_g, v_meta_tokens, v_ffn1_pre_g, v_ffn1_w_gate, v_ffn1_w_up, v_ffn1_w_down, v_ffn1_post_g, v_mix_pre_g, v_w_in, v_lru_conv_w, v_lru_conv_b, v_lru_w_a, v_lru_b_a, v_lru_w_x, v_lru_b_x, v_lru_lambda, v_sconv_w, v_lru_out_g, v_sconv_out_g, v_w_out, v_mix_post_g, v_ffn2_pre_g, v_ffn2_w_gate, v_ffn2_w_up, v_ffn2_w_down, v_ffn2_post_g):
    given = dict(x=x, meta_tokens=meta_tokens, ffn1_pre_g=ffn1_pre_g, ffn1_w_gate=ffn1_w_gate, ffn1_w_up=ffn1_w_up, ffn1_w_down=ffn1_w_down, ffn1_post_g=ffn1_post_g, mix_pre_g=mix_pre_g, w_in=w_in, lru_conv_w=lru_conv_w, lru_conv_b=lru_conv_b, lru_w_a=lru_w_a, lru_b_a=lru_b_a, lru_w_x=lru_w_x, lru_b_x=lru_b_x, lru_lambda=lru_lambda, sconv_w=sconv_w, lru_out_g=lru_out_g, sconv_out_g=sconv_out_g, w_out=w_out, mix_post_g=mix_post_g, ffn2_pre_g=ffn2_pre_g, ffn2_w_gate=ffn2_w_gate, ffn2_w_up=ffn2_w_up, ffn2_w_down=ffn2_w_down, ffn2_post_g=ffn2_post_g, loss_target=loss_target, m_meta_tokens=m_meta_tokens, m_ffn1_pre_g=m_ffn1_pre_g, m_ffn1_w_gate=m_ffn1_w_gate, m_ffn1_w_up=m_ffn1_w_up, m_ffn1_w_down=m_ffn1_w_down, m_ffn1_post_g=m_ffn1_post_g, m_mix_pre_g=m_mix_pre_g, m_w_in=m_w_in, m_lru_conv_w=m_lru_conv_w, m_lru_conv_b=m_lru_conv_b, m_lru_w_a=m_lru_w_a, m_lru_b_a=m_lru_b_a, m_lru_w_x=m_lru_w_x, m_lru_b_x=m_lru_b_x, m_lru_lambda=m_lru_lambda, m_sconv_w=m_sconv_w, m_lru_out_g=m_lru_out_g, m_sconv_out_g=m_sconv_out_g, m_w_out=m_w_out, m_mix_post_g=m_mix_post_g, m_ffn2_pre_g=m_ffn2_pre_g, m_ffn2_w_gate=m_ffn2_w_gate, m_ffn2_w_up=m_ffn2_w_up, m_ffn2_w_down=m_ffn2_w_down, m_ffn2_post_g=m_ffn2_post_g, v_meta_tokens=v_meta_tokens, v_ffn1_pre_g=v_ffn1_pre_g, v_ffn1_w_gate=v_ffn1_w_gate, v_ffn1_w_up=v_ffn1_w_up, v_ffn1_w_down=v_ffn1_w_down, v_ffn1_post_g=v_ffn1_post_g, v_mix_pre_g=v_mix_pre_g, v_w_in=v_w_in, v_lru_conv_w=v_lru_conv_w, v_lru_conv_b=v_lru_conv_b, v_lru_w_a=v_lru_w_a, v_lru_b_a=v_lru_b_a, v_lru_w_x=v_lru_w_x, v_lru_b_x=v_lru_b_x, v_lru_lambda=v_lru_lambda, v_sconv_w=v_sconv_w, v_lru_out_g=v_lru_out_g, v_sconv_out_g=v_sconv_out_g, v_w_out=v_w_out, v_mix_post_g=v_mix_post_g, v_ffn2_pre_g=v_ffn2_pre_g, v_ffn2_w_gate=v_ffn2_w_gate, v_ffn2_w_up=v_ffn2_w_up, v_ffn2_w_down=v_ffn2_w_down, v_ffn2_post_g=v_ffn2_post_g)
    weights = {n: given[n] for n in TWIN_WEIGHTS}
    shared = {n: given[n] for n in SHARED_INPUTS}
    per_example = {n: given[n] for n in ['x']}
    grad_fn = _jax.value_and_grad(_loss, argnums=(0, 1))

    def one_microbatch(ex, loss_target):
        ex = dict(ex)
        diff = ex.pop(TWIN_DIFF_INPUT)
        return grad_fn(weights, diff, {**shared, **ex}, loss_target)

    if N_MICROBATCH == 1:
        loss, (grad_w, grad_x) = one_microbatch(per_example, given["loss_target"])
    else:
        def body(carry, xs):
            loss_sum, grad_sum = carry
            l_k, (gw_k, gx_k) = one_microbatch(xs[0], xs[1])
            with _jax.named_scope("update"):
                return (loss_sum + l_k, _jax.tree.map(_jnp.add, grad_sum, gw_k)), gx_k

        init = (_jnp.zeros((), _jnp.float32), _jax.tree.map(_jnp.zeros_like, weights))
        (loss, grad_w), grad_x = _jax.lax.scan(body, init, (per_example, given["loss_target"]))
    with _jax.named_scope("update"):
        delta_w, new_m, new_v = {}, {}, {}
        for n in TWIN_WEIGHTS:
            delta_w[n], new_m[n], new_v[n] = _adamw(weights[n], grad_w[n], given["m_" + n], given["v_" + n])
    return (loss, grad_x, *[grad_w[n] for n in TWIN_WEIGHTS], *[delta_w[n] for n in TWIN_WEIGHTS],
            *[new_m[n] for n in TWIN_WEIGHTS], *[new_v[n] for n in TWIN_WEIGHTS])
```

```python
import functools

import jax
import jax.numpy as jnp
from jax import lax
from jax.experimental import pallas as pl
from jax.experimental.pallas import tpu as pltpu

F32 = jnp.float32
MXU_DTYPE = jnp.bfloat16
WIRE_DTYPE = jnp.bfloat16
MESH = pl.DeviceIdType.MESH

EPS = 1e-6
LRU_C = 8.0
N_GROUPS = 16
ADAM_LR = 0.001
ADAM_B1 = 0.9
ADAM_B2 = 0.999
ADAM_EPS = 1e-08
ADAM_WD = 0.01
ADAM_STEP = 10

N_DEV = 8
LANE = 128
SUBLANE_BF16 = 16
ROW_ALIGN = 128
F_ALIGN = 512
BD = 256
MIX_ROWS = 128
VMEM_LIMIT_MB = 56

WEIGHT_NAMES = ['meta_tokens', 'ffn1_pre_g', 'ffn1_w_gate', 'ffn1_w_up', 'ffn1_w_down', 'ffn1_post_g',
                'mix_pre_g', 'w_in', 'lru_conv_w', 'lru_conv_b', 'lru_w_a', 'lru_b_a', 'lru_w_x', 'lru_b_x',
                'lru_lambda', 'sconv_w', 'lru_out_g', 'sconv_out_g', 'w_out', 'mix_post_g', 'ffn2_pre_g',
                'ffn2_w_gate', 'ffn2_w_up', 'ffn2_w_down', 'ffn2_post_g']


def _round_up(n, q):
    return (n + q - 1) // q * q


def _tile(n, target, q):
    best = None
    t = q
    while t <= min(n, target):
        if n % t == 0:
            best = t
        t += q
    assert best is not None, (n, target, q)
    return best


def _params(**kw):
    return pltpu.CompilerParams(vmem_limit_bytes=VMEM_LIMIT_MB << 20, **kw)


def _rmsnorm(h, g, name):
    m, d = h.shape
    tm = _tile(m, 528, SUBLANE_BF16)

    def body(h_ref, g_ref, o_ref):
        x = h_ref[...]
        r = lax.rsqrt(jnp.mean(x * x, axis=-1, keepdims=True) + EPS)
        o_ref[...] = (x * r * g_ref[...]).astype(o_ref.dtype)

    return pl.pallas_call(
        body, grid=(m // tm,),
        in_specs=[pl.BlockSpec((tm, d), lambda i: (i, 0)), pl.BlockSpec((1, d), lambda i: (0, 0))],
        out_specs=pl.BlockSpec((tm, d), lambda i: (i, 0)),
        out_shape=jax.ShapeDtypeStruct((m, d), MXU_DTYPE), name=name, compiler_params=_params())(h, g)


def _rmsnorm_bwd_rows(x, g, dy):
    r = lax.rsqrt(jnp.mean(x * x, axis=-1, keepdims=True) + EPS)
    xh = x * r
    dyh = dy * g
    dx = r * (dyh - xh * jnp.mean(dyh * xh, axis=-1, keepdims=True))
    return dx, dy * xh


def _norm_bwd(x, g, dy, scale, name):
    m, d = x.shape
    tm = _tile(m, 528, SUBLANE_BF16)

    def body(x_ref, g_ref, dy_ref, dx_ref, dg_ref):
        @pl.when(pl.program_id(0) == 0)
        def _():
            dg_ref[...] = jnp.zeros_like(dg_ref)

        dx, dgr = _rmsnorm_bwd_rows(x_ref[...], g_ref[...], scale * dy_ref[...])
        dx_ref[...] = dx.astype(dx_ref.dtype)
        dg_ref[...] += jnp.sum(dgr, axis=0, keepdims=True)

    return pl.pallas_call(
        body, grid=(m // tm,),
        in_specs=[pl.BlockSpec((tm, d), lambda i: (i, 0)), pl.BlockSpec((1, d), lambda i: (0, 0)),
                  pl.BlockSpec((tm, d), lambda i: (i, 0))],
        out_specs=[pl.BlockSpec((tm, d), lambda i: (i, 0)), pl.BlockSpec((1, d), lambda i: (0, 0))],
        out_shape=[jax.ShapeDtypeStruct((m, d), MXU_DTYPE), jax.ShapeDtypeStruct((1, d), F32)],
        name=name, compiler_params=_params())(x, g, dy)


def _loss_grad(h, target, lead, name):
    m, d = h.shape
    tl = ROW_ALIGN
    assert lead % tl == 0 and target.shape[0] == m - lead
    lead_blocks = lead // tl

    def body(h_ref, t_ref, dh_ref, l_ref):
        i = pl.program_id(0)

        @pl.when(i == 0)
        def _():
            l_ref[...] = jnp.zeros_like(l_ref)

        @pl.when(i < lead_blocks)
        def _():
            dh_ref[...] = jnp.zeros_like(dh_ref)

        @pl.when(i >= lead_blocks)
        def _():
            e = h_ref[...] - t_ref[...]
            dh_ref[...] = e * (1.0 / d)
            row = jnp.sum(e * e, axis=-1, keepdims=True) * (1.0 / d)
            l_ref[...] += 0.5 * jnp.sum(row, axis=0, keepdims=True)

    return pl.pallas_call(
        body, grid=(m // tl,),
        in_specs=[pl.BlockSpec((tl, d), lambda i: (i, 0)),
                  pl.BlockSpec((tl, d), lambda i: (jnp.maximum(i - lead_blocks, 0), 0))],
        out_specs=[pl.BlockSpec((tl, d), lambda i: (i, 0)), pl.BlockSpec((1, 1), lambda i: (0, 0))],
        out_shape=[jax.ShapeDtypeStruct((m, d), F32), jax.ShapeDtypeStruct((1, 1), F32)],
        name=name, compiler_params=_params())(h, target)


def _dot_nt(a, b):
    return lax.dot_general(a, b, (((1,), (1,)), ((), ())), preferred_element_type=F32)


def _dot_tn(a, b):
    return lax.dot_general(a, b, (((0,), (0,)), ((), ())), preferred_element_type=F32)


def _mm_nt(a, w, name):
    m, k = a.shape
    n = w.shape[0]
    tm = _tile(m, 1056, SUBLANE_BF16)
    tn = _tile(n, 512, LANE)

    def body(a_ref, w_ref, o_ref):
        o_ref[...] = _dot_nt(a_ref[...], w_ref[...])

    return pl.pallas_call(
        body, grid=(m // tm, n // tn),
        in_specs=[pl.BlockSpec((tm, k), lambda i, j: (i, 0)), pl.BlockSpec((tn, k), lambda i, j: (j, 0))],
        out_specs=pl.BlockSpec((tm, tn), lambda i, j: (i, j)),
        out_shape=jax.ShapeDtypeStruct((m, n), F32), name=name, compiler_params=_params())(a, w)


def _ffn_gate_up(n_act, wg_t, wu_t, name):
    m, d = n_act.shape
    fp = wg_t.shape[0]
    tm = _tile(m, 1056, SUBLANE_BF16)
    tn = _tile(fp, 512, LANE)

    def body(n_ref, wg_ref, wu_ref, g_ref, u_ref, a_ref):
        n = n_ref[...]
        g = _dot_nt(n, wg_ref[...])
        u = _dot_nt(n, wu_ref[...])
        g_ref[...] = g.astype(g_ref.dtype)
        u_ref[...] = u.astype(u_ref.dtype)
        a_ref[...] = (g * jax.nn.sigmoid(g) * u).astype(a_ref.dtype)

    act = pl.BlockSpec((tm, tn), lambda i, j: (i, j))
    wsp = pl.BlockSpec((tn, d), lambda i, j: (j, 0))
    return pl.pallas_call(
        body, grid=(m // tm, fp // tn),
        in_specs=[pl.BlockSpec((tm, d), lambda i, j: (i, 0)), wsp, wsp],
        out_specs=[act, act, act],
        out_shape=[jax.ShapeDtypeStruct((m, fp), MXU_DTYPE)] * 3, name=name, compiler_params=_params())(n_act, wg_t, wu_t)


def _ffn_hidden_bwd(dfo, wd, g_act, u_act, name):
    m, d = dfo.shape
    fp = wd.shape[0]
    tm = _tile(m, 1056, SUBLANE_BF16)
    tn = _tile(fp, 512, LANE)

    def body(df_ref, wd_ref, g_ref, u_ref, dg_ref, du_ref):
        da = _dot_nt(df_ref[...], wd_ref[...])
        g = g_ref[...].astype(F32)
        u = u_ref[...].astype(F32)
        s = jax.nn.sigmoid(g)
        du_ref[...] = (da * (g * s)).astype(du_ref.dtype)
        dg_ref[...] = (da * u * (s * (1.0 + g * (1.0 - s)))).astype(dg_ref.dtype)

    act = pl.BlockSpec((tm, tn), lambda i, j: (i, j))
    return pl.pallas_call(
        body, grid=(m // tm, fp // tn),
        in_specs=[pl.BlockSpec((tm, d), lambda i, j: (i, 0)), pl.BlockSpec((tn, d), lambda i, j: (j, 0)), act, act],
        out_specs=[act, act],
        out_shape=[jax.ShapeDtypeStruct((m, fp), MXU_DTYPE)] * 2, name=name, compiler_params=_params())(dfo, wd, g_act, u_act)


def _mm_residual_norm(a, w, h, g, scale, name):
    m, k = a.shape
    d = w.shape[1]
    tm = _tile(m, 528, SUBLANE_BF16)
    tk = _tile(k, 512, LANE)
    nk = k // tk

    def body(a_ref, w_ref, h_ref, g_ref, fo_ref, hn_ref, acc_ref):
        kk = pl.program_id(1)

        @pl.when(kk == 0)
        def _():
            acc_ref[...] = jnp.zeros_like(acc_ref)

        acc_ref[...] += jnp.dot(a_ref[...], w_ref[...], preferred_element_type=F32)

        @pl.when(kk == nk - 1)
        def _():
            fo = acc_ref[...]
            fo_ref[...] = fo
            r = lax.rsqrt(jnp.mean(fo * fo, axis=-1, keepdims=True) + EPS)
            hn_ref[...] = h_ref[...] + scale * (fo * r * g_ref[...])

    row = pl.BlockSpec((tm, d), lambda i, kk: (i, 0))
    return pl.pallas_call(
        body, grid=(m // tm, nk),
        in_specs=[pl.BlockSpec((tm, tk), lambda i, kk: (i, kk)), pl.BlockSpec((tk, d), lambda i, kk: (kk, 0)),
                  row, pl.BlockSpec((1, d), lambda i, kk: (0, 0))],
        out_specs=[row, row],
        out_shape=[jax.ShapeDtypeStruct((m, d), F32)] * 2,
        scratch_shapes=[pltpu.VMEM((tm, d), F32)], name=name, compiler_params=_params())(a, w, h, g)


def _mm_norm_bwd(pairs, h, g, dh_up, name):
    n_pairs = len(pairs)
    m, k = pairs[0][0].shape
    d = h.shape[1]
    tm = _tile(m, 528, SUBLANE_BF16)
    tk = _tile(k, 512, LANE)
    nk = k // tk

    def body(*refs):
        ops = refs[:2 * n_pairs]
        h_ref, g_ref, up_ref, dh_ref, dg_ref, acc_ref = refs[2 * n_pairs:]
        i = pl.program_id(0)
        kk = pl.program_id(1)

        @pl.when(jnp.logical_and(i == 0, kk == 0))
        def _():
            dg_ref[...] = jnp.zeros_like(dg_ref)

        @pl.when(kk == 0)
        def _():
            acc_ref[...] = jnp.zeros_like(acc_ref)

        for p in range(n_pairs):
            acc_ref[...] += jnp.dot(ops[2 * p][...], ops[2 * p + 1][...], preferred_element_type=F32)

        @pl.when(kk == nk - 1)
        def _():
            dx, dgr = _rmsnorm_bwd_rows(h_ref[...], g_ref[...], acc_ref[...])
            dh_ref[...] = up_ref[...] + dx
            dg_ref[...] += jnp.sum(dgr, axis=0, keepdims=True)

    row = pl.BlockSpec((tm, d), lambda i, kk: (i, 0))
    vec = pl.BlockSpec((1, d), lambda i, kk: (0, 0))
    in_specs = []
    args = []
    for a, w in pairs:
        in_specs += [pl.BlockSpec((tm, tk), lambda i, kk: (i, kk)), pl.BlockSpec((tk, d), lambda i, kk: (kk, 0))]
        args += [a, w]
    return pl.pallas_call(
        body, grid=(m // tm, nk),
        in_specs=in_specs + [row, vec, row], out_specs=[row, vec],
        out_shape=[jax.ShapeDtypeStruct((m, d), F32), jax.ShapeDtypeStruct((1, d), F32)],
        scratch_shapes=[pltpu.VMEM((tm, d), F32)], name=name, compiler_params=_params())(*args, h, g, dh_up)


def _mm_tn(a, b, name):
    m, ka = a.shape
    d = b.shape[1]
    tf = _tile(ka, 512, LANE)
    tm = _tile(m, 1056, SUBLANE_BF16)
    nm = m // tm

    def body(a_ref, b_ref, o_ref, acc_ref):
        mm = pl.program_id(1)

        @pl.when(mm == 0)
        def _():
            acc_ref[...] = jnp.zeros_like(acc_ref)

        acc_ref[...] += _dot_tn(a_ref[...], b_ref[...])

        @pl.when(mm == nm - 1)
        def _():
            o_ref[...] = acc_ref[...].astype(o_ref.dtype)

    return pl.pallas_call(
        body, grid=(ka // tf, nm),
        in_specs=[pl.BlockSpec((tm, tf), lambda j, mm: (mm, j)), pl.BlockSpec((tm, d), lambda j, mm: (mm, 0))],
        out_specs=pl.BlockSpec((tf, d), lambda j, mm: (j, 0)),
        out_shape=jax.ShapeDtypeStruct((ka, d), WIRE_DTYPE),
        scratch_shapes=[pltpu.VMEM((tf, d), F32)], name=name, compiler_params=_params())(a, b)


GELU_K = 0.7978845608028654
GELU_C = 0.044715


def _expm1(x):
    series = x * (1.0 + x * (1.0 / 2 + x * (1.0 / 6 + x * (1.0 / 24 + x * (1.0 / 120 + x * (1.0 / 720 + x * (1.0 / 5040)))))))
    return jnp.where(jnp.abs(x) < 0.3, series, jnp.exp(x) - 1.0)


def _softplus(x):
    return jnp.maximum(x, 0.0) + jnp.log1p(jnp.exp(-jnp.abs(x)))


def _block_mm(v, w_ref, transposed):
    nbk = w_ref.shape[0]
    outs = []
    for j in range(nbk):
        vj = v[:, j * BD:(j + 1) * BD]
        outs.append(_dot_nt(vj, w_ref[j]) if transposed else jnp.dot(vj, w_ref[j], preferred_element_type=F32))
    return outs[0] if nbk == 1 else jnp.concatenate(outs, axis=1)


def _group_mean(q, gm_ref):
    hi = q.astype(MXU_DTYPE)
    lo = (q - hi.astype(F32)).astype(MXU_DTYPE)
    nbk = q.shape[1] // BD
    gm = gm_ref[...]
    outs = []
    for j in range(nbk):
        sl = slice(j * BD, (j + 1) * BD)
        outs.append(jnp.dot(hi[:, sl], gm, preferred_element_type=F32) + jnp.dot(lo[:, sl], gm, preferred_element_type=F32))
    return outs[0] if nbk == 1 else jnp.concatenate(outs, axis=1)


class _RowReader:
    def __init__(self, ref):
        self.ref = ref

    def __getitem__(self, rows):
        return self.ref[rows, :]


def _shifted(ext_ref, cur, before8, after8, downs=(), ups=()):
    r = cur.shape[0]
    if downs:
        ext_ref[0:8, :] = before8
    ext_ref[8:8 + r, :] = cur
    if ups:
        ext_ref[8 + r:16 + r, :] = after8
    return [ext_ref[pl.ds(8 - j, r), :] for j in downs] + [ext_ref[pl.ds(8 + j, r), :] for j in ups]


def _lru_gates(xc, pv, wa_ref, wx_ref):
    xcb = xc.astype(MXU_DTYPE)
    ga = jax.nn.sigmoid(_block_mm(xcb, wa_ref, False) + pv[5:6])
    gx = jax.nn.sigmoid(_block_mm(xcb, wx_ref, False) + pv[6:7])
    sp = _softplus(-pv[7:8])
    log_a = -LRU_C * ga * sp
    a = jnp.exp(log_a)
    e2 = _expm1(2.0 * log_a)
    mult = jnp.sqrt(-e2)
    return xcb, ga, gx, sp, a, e2, mult


def _gelu_parts(y):
    th = jnp.tanh(GELU_K * (y + GELU_C * y * y * y))
    return 0.5 * y * (1.0 + th), th


def _mixer_fwd(z, pv, wa, wx, gm, pad, name):
    m = z.shape[0]
    c = pv.shape[1]
    r = MIX_ROWS
    nb = m // r

    def body(z_ref, pv_ref, wa_ref, wx_ref, gm_ref, mixed_ref, hs_ref, ext_ref, tailx_ref, tailc_ref, carry_ref):
        b = pl.program_id(0)

        @pl.when(b == 0)
        def _():
            tailx_ref[...] = jnp.zeros_like(tailx_ref)
            tailc_ref[...] = jnp.zeros_like(tailc_ref)
            carry_ref[...] = jnp.zeros_like(carry_ref)

        pv = _RowReader(pv_ref)
        row = b * r + lax.broadcasted_iota(jnp.int32, (r, 1), 0)
        lrow = lax.broadcasted_iota(jnp.int32, (r, c), 0)
        maskf = (row >= pad).astype(F32)
        y = z_ref[:, 0:c]
        xl = z_ref[:, c:2 * c]
        bs = z_ref[:, 2 * c:3 * c]
        cv = z_ref[:, 3 * c:4 * c] * z_ref[:, 4 * c:5 * c]

        x1, x2, x3 = _shifted(ext_ref, xl, tailx_ref[...], None, downs=(1, 2, 3))
        tailx_ref[...] = z_ref[pl.ds(r - 8, 8), c:2 * c]
        xc = pv[4:5] + pv[3:4] * xl + pv[2:3] * x1 + pv[1:2] * x2 + pv[0:1] * x3
        _, _, gx, _, a, _, mult = _lru_gates(xc, pv, wa_ref, wx_ref)
        uu = mult * (gx * xc) * maskf

        acc_a = a
        acc_h = uu
        dlt = 1
        while dlt < r:
            keep = lrow >= dlt
            sh_a = pltpu.roll(acc_a, dlt, axis=0)
            sh_h = pltpu.roll(acc_h, dlt, axis=0)
            acc_h = acc_h + acc_a * jnp.where(keep, sh_h, 0.0)
            acc_a = acc_a * jnp.where(keep, sh_a, 1.0)
            dlt *= 2
        hs = acc_h + acc_a * carry_ref[...]
        hs_ref[...] = hs
        carry_ref[...] = hs_ref[pl.ds(r - 1, 1), :]

        gelu_y, _ = _gelu_parts(y)
        lru_out = hs * gelu_y
        c1, c2 = _shifted(ext_ref, cv, tailc_ref[...], None, downs=(1, 2))
        tailc_ref[...] = cv[r - 8:r]
        sc_out = bs * (pv[10:11] * cv + pv[9:10] * c1 + pv[8:9] * c2)

        rl = lax.rsqrt(_group_mean(lru_out * lru_out, gm_ref) + EPS)
        rs = lax.rsqrt(_group_mean(sc_out * sc_out, gm_ref) + EPS)
        mixed_ref[:, 0:c] = (lru_out * rl * pv[11:12]).astype(mixed_ref.dtype)
        mixed_ref[:, c:2 * c] = (sc_out * rs * pv[12:13]).astype(mixed_ref.dtype)

    full = lambda shape: pl.BlockSpec(shape, lambda b: (0,) * len(shape))
    return pl.pallas_call(
        body, grid=(nb,),
        in_specs=[pl.BlockSpec((r, 5 * c), lambda b: (b, 0)), full(pv.shape), full(wa.shape), full(wx.shape), full(gm.shape)],
        out_specs=[pl.BlockSpec((r, 2 * c), lambda b: (b, 0)), pl.BlockSpec((r, c), lambda b: (b, 0))],
        out_shape=[jax.ShapeDtypeStruct((m, 2 * c), MXU_DTYPE), jax.ShapeDtypeStruct((m, c), F32)],
        scratch_shapes=[pltpu.VMEM((r + 16, c), F32), pltpu.VMEM((8, c), F32), pltpu.VMEM((8, c), F32),
                        pltpu.VMEM((1, c), F32)],
        name=name, compiler_params=_params())(z, pv, wa, wx, gm)


def _mixer_bwd(z, hs, dmixed, pv, wa, wx, gm, pad, name):
    m = z.shape[0]
    c = pv.shape[1]
    r = MIX_ROWS
    nb = m // r
    r8 = r // 8

    def body(z_ref, zp_ref, hs_ref, hsp_ref, dm_ref, pv_ref, wa_ref, wx_ref, gm_ref,
             dz_ref, dpv_ref, dwa_ref, dwx_ref, ext_ref, hxc_ref, hsc_ref, hp_ref):
        i = pl.program_id(0)
        b = nb - 1 - i

        @pl.when(i == 0)
        def _():
            hxc_ref[...] = jnp.zeros_like(hxc_ref)
            hsc_ref[...] = jnp.zeros_like(hsc_ref)
            hp_ref[...] = jnp.zeros_like(hp_ref)
            dpv_ref[...] = jnp.zeros_like(dpv_ref)
            dwa_ref[...] = jnp.zeros_like(dwa_ref)
            dwx_ref[...] = jnp.zeros_like(dwx_ref)

        pv = _RowReader(pv_ref)
        row = b * r + lax.broadcasted_iota(jnp.int32, (r, 1), 0)
        lrow = lax.broadcasted_iota(jnp.int32, (r, c), 0)
        maskf = (row >= pad).astype(F32)
        has_prev = (b > 0).astype(F32)
        y = z_ref[:, 0:c]
        xl = z_ref[:, c:2 * c]
        bs = z_ref[:, 2 * c:3 * c]
        cs = z_ref[:, 3 * c:4 * c]
        vs = z_ref[:, 4 * c:5 * c]
        cv = cs * vs
        xl_prev = zp_ref[:, c:2 * c] * has_prev
        cv_prev = zp_ref[:, 3 * c:4 * c] * zp_ref[:, 4 * c:5 * c] * has_prev
        hs = hs_ref[...]

        x1, x2, x3 = _shifted(ext_ref, xl, xl_prev, None, downs=(1, 2, 3))
        xc = pv[4:5] + pv[3:4] * xl + pv[2:3] * x1 + pv[1:2] * x2 + pv[0:1] * x3
        xcb, ga, gx, sp, a, e2, mult = _lru_gates(xc, pv, wa_ref, wx_ref)
        gxx = gx * xc
        gelu_y, th = _gelu_parts(y)
        lru_out = hs * gelu_y
        c1, c2 = _shifted(ext_ref, cv, cv_prev, None, downs=(1, 2))
        sc = pv[10:11] * cv + pv[9:10] * c1 + pv[8:9] * c2
        sc_out = bs * sc

        def group_norm_bwd(v, dm, gain):
            rr = lax.rsqrt(_group_mean(v * v, gm_ref) + EPS)
            vh = v * rr
            dvh = dm * gain
            dv = rr * (dvh - vh * _group_mean(dvh * vh, gm_ref))
            return dv, jnp.sum(dm * vh, axis=0, keepdims=True)

        d_lru_out, d_og = group_norm_bwd(lru_out, dm_ref[:, 0:c], pv[11:12])
        d_sc_out, d_sg = group_norm_bwd(sc_out, dm_ref[:, c:2 * c], pv[12:13])
        dpv_ref[11:12, :] += d_og
        dpv_ref[12:13, :] += d_sg

        dhs = d_lru_out * gelu_y
        dgelu = 0.5 * (1.0 + th) + 0.5 * y * (1.0 - th * th) * GELU_K * (1.0 + 3.0 * GELU_C * y * y)
        dy = d_lru_out * hs * dgelu

        acc_a = a
        acc_p = a * dhs
        dlt = 1
        while dlt < r:
            keep = lrow < r - dlt
            sh_a = pltpu.roll(acc_a, r - dlt, axis=0)
            sh_p = pltpu.roll(acc_p, r - dlt, axis=0)
            acc_p = acc_p + acc_a * jnp.where(keep, sh_p, 0.0)
            acc_a = acc_a * jnp.where(keep, sh_a, 1.0)
            dlt *= 2
        p_all = acc_p + acc_a * hp_ref[0:1, :]
        (p_next,) = _shifted(ext_ref, p_all, None, hp_ref[...], ups=(1,))
        hp_ref[...] = p_all[0:8]
        q = dhs + p_next
        (hs_prev,) = _shifted(ext_ref, hs, hsp_ref[...] * has_prev, None, downs=(1,))
        duu = q * maskf
        da = q * hs_prev

        dmult = duu * gxx
        dgxx = duu * mult
        dgx = dgxx * xc
        dxc = dgxx * gx
        dlog_a = da * a - dmult * ((1.0 + e2) / mult)
        dga = dlog_a * (-LRU_C * sp)
        dsp = jnp.sum(dlog_a * (-LRU_C * ga), axis=0, keepdims=True)
        dpv_ref[7:8, :] += dsp * (-jax.nn.sigmoid(-pv[7:8]))
        dga_pre = dga * ga * (1.0 - ga)
        dgx_pre = dgx * gx * (1.0 - gx)
        dpv_ref[5:6, :] += jnp.sum(dga_pre, axis=0, keepdims=True)
        dpv_ref[6:7, :] += jnp.sum(dgx_pre, axis=0, keepdims=True)
        dga_b = dga_pre.astype(MXU_DTYPE)
        dgx_b = dgx_pre.astype(MXU_DTYPE)
        dxc = dxc + _block_mm(dga_b, wa_ref, True) + _block_mm(dgx_b, wx_ref, True)
        for j in range(c // BD):
            sl = slice(j * BD, (j + 1) * BD)
            dwa_ref[j] += _dot_tn(xcb[:, sl], dga_b[:, sl])
            dwx_ref[j] += _dot_tn(xcb[:, sl], dgx_b[:, sl])

        dpv_ref[4:5, :] += jnp.sum(dxc, axis=0, keepdims=True)
        dpv_ref[3:4, :] += jnp.sum(dxc * xl, axis=0, keepdims=True)
        dpv_ref[2:3, :] += jnp.sum(dxc * x1, axis=0, keepdims=True)
        dpv_ref[1:2, :] += jnp.sum(dxc * x2, axis=0, keepdims=True)
        dpv_ref[0:1, :] += jnp.sum(dxc * x3, axis=0, keepdims=True)
        u1, u2, u3 = _shifted(ext_ref, dxc, None, hxc_ref[...], ups=(1, 2, 3))
        hxc_ref[...] = dxc[0:8]
        dxl = pv[3:4] * dxc + pv[2:3] * u1 + pv[1:2] * u2 + pv[0:1] * u3

        dbs = d_sc_out * sc
        dsc = d_sc_out * bs
        dpv_ref[10:11, :] += jnp.sum(dsc * cv, axis=0, keepdims=True)
        dpv_ref[9:10, :] += jnp.sum(dsc * c1, axis=0, keepdims=True)
        dpv_ref[8:9, :] += jnp.sum(dsc * c2, axis=0, keepdims=True)
        s1, s2 = _shifted(ext_ref, dsc, None, hsc_ref[...], ups=(1, 2))
        hsc_ref[...] = dsc[0:8]
        dcv = pv[10:11] * dsc + pv[9:10] * s1 + pv[8:9] * s2

        dz_ref[:, 0:c] = (dy * maskf).astype(dz_ref.dtype)
        dz_ref[:, c:2 * c] = (dxl * maskf).astype(dz_ref.dtype)
        dz_ref[:, 2 * c:3 * c] = (dbs * maskf).astype(dz_ref.dtype)
        dz_ref[:, 3 * c:4 * c] = (dcv * vs * maskf).astype(dz_ref.dtype)
        dz_ref[:, 4 * c:5 * c] = (dcv * cs * maskf).astype(dz_ref.dtype)

    full = lambda shape: pl.BlockSpec(shape, lambda i: (0,) * len(shape))
    cur = lambda width: pl.BlockSpec((r, width), lambda i: (nb - 1 - i, 0))
    prev8 = lambda width: pl.BlockSpec((8, width), lambda i: (jnp.maximum((nb - 1 - i) * r8 - 1, 0), 0))
    return pl.pallas_call(
        body, grid=(nb,),
        in_specs=[cur(5 * c), prev8(5 * c), cur(c), prev8(c), cur(2 * c),
                  full(pv.shape), full(wa.shape), full(wx.shape), full(gm.shape)],
        out_specs=[cur(5 * c), full(pv.shape), full(wa.shape), full(wx.shape)],
        out_shape=[jax.ShapeDtypeStruct((m, 5 * c), MXU_DTYPE), jax.ShapeDtypeStruct(pv.shape, F32),
                   jax.ShapeDtypeStruct(wa.shape, F32), jax.ShapeDtypeStruct(wx.shape, F32)],
        scratch_shapes=[pltpu.VMEM((r + 16, c), F32), pltpu.VMEM((8, c), F32), pltpu.VMEM((8, c), F32),
                        pltpu.VMEM((8, c), F32)],
        name=name, compiler_params=_params())(z, z, hs, hs, dmixed, pv, wa, wx, gm)


def _position():
    return lax.axis_index("x"), lax.axis_index("y"), lax.axis_index("c")


def _block_of(px, py, pc):
    return 4 * px + 2 * py + pc


def _two_level_gather(x, y, c, n_arrays, rows_of, src_of, send_sems, recv_sems):
    me, sibling = (x, y, c), (x, y, 1 - c)
    chips = [(1 - x, y), (x, 1 - y), (1 - x, 1 - y)]

    def copy(i, k, block, to, src=None):
        return pltpu.make_async_remote_copy(
            src_ref=rows_of(i, *block) if src is None else src, dst_ref=rows_of(i, *block),
            send_sem=send_sems.at[7 * i + k], recv_sem=recv_sems.at[7 * i + k], device_id=to, device_id_type=MESH)

    started = []
    for i in range(n_arrays):
        first = [copy(i, 0, me, sibling, src=src_of(i))]
        first += [copy(i, 1 + j, me, (*chip, c), src=src_of(i)) for j, chip in enumerate(chips)]
        for cp in first:
            cp.start()
        started += first
    for i in range(n_arrays):
        for j, chip in enumerate(chips):
            copy(i, 1 + j, (*chip, c), me).wait_recv()
            passed = copy(i, 4 + j, (*chip, c), sibling)
            passed.start()
            started.append(passed)
    for i in range(n_arrays):
        copy(i, 0, sibling, me).wait_recv()
        for j, chip in enumerate(chips):
            copy(i, 4 + j, (*chip, 1 - c), me).wait_recv()
    for cp in started:
        cp.wait_send()


def _gather_weights(shards, padded_rows, name):
    n = len(shards)
    d = shards[0].shape[1]
    pads = [p - N_DEV * s.shape[0] for s, p in zip(shards, padded_rows)]
    max_pad = max(max(pads), SUBLANE_BF16)

    def body(*refs):
        ins, outs = refs[:n], refs[n:2 * n]
        send_sems, recv_sems, local_sems, zero_sems, zeros_ref = refs[2 * n:]
        x, y, c = _position()

        def rows_of(i, px, py, pc):
            s = shards[i].shape[0]
            return outs[i].at[pl.ds(pl.multiple_of(_block_of(px, py, pc) * s, SUBLANE_BF16), s), :]

        zeros_ref[...] = jnp.zeros_like(zeros_ref)
        local = []
        for i in range(n):
            own = pltpu.make_async_copy(ins[i], rows_of(i, x, y, c), local_sems.at[i])
            own.start()
            local.append(own)
            if pads[i]:
                zc = pltpu.make_async_copy(zeros_ref.at[pl.ds(0, pads[i]), :],
                                           outs[i].at[pl.ds(N_DEV * shards[i].shape[0], pads[i]), :], zero_sems.at[i])
                zc.start()
                local.append(zc)
        _two_level_gather(x, y, c, n, rows_of, lambda i: ins[i], send_sems, recv_sems)
        for cp in local:
            cp.wait()

    hbm = pl.BlockSpec(memory_space=pl.ANY)
    return pl.pallas_call(
        body, in_specs=[hbm] * n, out_specs=[hbm] * n,
        out_shape=[jax.ShapeDtypeStruct((p, d), s.dtype) for s, p in zip(shards, padded_rows)],
        scratch_shapes=[pltpu.SemaphoreType.DMA((7 * n,)), pltpu.SemaphoreType.DMA((7 * n,)),
                        pltpu.SemaphoreType.DMA((n,)), pltpu.SemaphoreType.DMA((n,)),
                        pltpu.VMEM((max_pad, d), shards[0].dtype)],
        name=name, compiler_params=_params())(*shards)


def _gather_small(block, reduce, name):
    rr, nn = block.shape

    def body(x_ref, out_ref, *rest):
        if reduce:
            stack_ref, send_sems, recv_sems, local_sem = rest
        else:
            send_sems, recv_sems, local_sem = rest
            stack_ref = out_ref
        x, y, c = _position()

        def rows_of(i, px, py, pc):
            return stack_ref.at[pl.ds(pl.multiple_of(_block_of(px, py, pc) * rr, 8), rr), :]

        own = pltpu.make_async_copy(x_ref, rows_of(0, x, y, c), local_sem)
        own.start()
        _two_level_gather(x, y, c, 1, rows_of, lambda i: x_ref, send_sems, recv_sems)
        own.wait()
        if reduce:
            acc = stack_ref[0:rr, :]
            for k in range(1, N_DEV):
                acc = acc + stack_ref[k * rr:(k + 1) * rr, :]
            out_ref[...] = acc

    vmem = pl.BlockSpec(memory_space=pltpu.VMEM)
    scratch = [pltpu.SemaphoreType.DMA((7,)), pltpu.SemaphoreType.DMA((7,)), pltpu.SemaphoreType.DMA]
    if reduce:
        scratch = [pltpu.VMEM((N_DEV * rr, nn), F32)] + scratch
    out_rows = rr if reduce else N_DEV * rr
    return pl.pallas_call(
        body, in_specs=[vmem], out_specs=vmem, out_shape=jax.ShapeDtypeStruct((out_rows, nn), F32),
        scratch_shapes=scratch, name=name, compiler_params=_params())(block)


def _exchange_grads(grads, shard_rows, name):
    n = len(grads)
    d = grads[0].shape[1]

    def body(*refs):
        ins, outs = refs[:n], refs[n:2 * n]
        send_sems, recv_sems = refs[2 * n:]
        x, y, c = _position()
        copies = []
        for i in range(n):
            s = shard_rows[i]
            for rel in range(1, N_DEV):
                px, py, pc = x ^ ((rel >> 2) & 1), y ^ ((rel >> 1) & 1), c ^ (rel & 1)
                src = ins[i].at[pl.ds(pl.multiple_of(_block_of(px, py, pc) * s, SUBLANE_BF16), s), :]
                cp = pltpu.make_async_remote_copy(
                    src_ref=src, dst_ref=outs[i].at[rel - 1],
                    send_sem=send_sems.at[7 * i + rel - 1], recv_sem=recv_sems.at[7 * i + rel - 1],
                    device_id=(px, py, pc), device_id_type=MESH)
                cp.start()
                copies.append(cp)
        for cp in copies:
            cp.wait()

    hbm = pl.BlockSpec(memory_space=pl.ANY)
    return pl.pallas_call(
        body, in_specs=[hbm] * n, out_specs=[hbm] * n,
        out_shape=[jax.ShapeDtypeStruct((N_DEV - 1, s, d), g.dtype) for g, s in zip(grads, shard_rows)],
        scratch_shapes=[pltpu.SemaphoreType.DMA((7 * n,)), pltpu.SemaphoreType.DMA((7 * n,))],
        name=name, compiler_params=_params())(*grads)


def _sum_shard(own_block, grad, received, name):
    s, d = received.shape[1], received.shape[2]
    tc = _tile(d, 512, LANE)

    def body(blk_ref, g_ref, r_ref, o_ref):
        acc = g_ref[...].astype(F32)
        for k in range(N_DEV - 1):
            acc = acc + r_ref[k].astype(F32)
        o_ref[...] = acc

    grid_spec = pltpu.PrefetchScalarGridSpec(
        num_scalar_prefetch=1, grid=(d // tc,),
        in_specs=[pl.BlockSpec((s, tc), lambda j, blk: (blk[0], j)),
                  pl.BlockSpec((N_DEV - 1, s, tc), lambda j, blk: (0, 0, j))],
        out_specs=pl.BlockSpec((s, tc), lambda j, blk: (0, j)))
    return pl.pallas_call(body, grid_spec=grid_spec, out_shape=jax.ShapeDtypeStruct((s, d), F32),
                          name=name, compiler_params=_params())(own_block, grad, received)


def _adamw(w, g, m, v, name):
    rows, cols = w.shape
    tr = _tile(rows, 256, 8)

    def body(w_ref, g_ref, m_ref, v_ref, d_ref, nm_ref, nv_ref):
        g = g_ref[...]
        nm = ADAM_B1 * m_ref[...] + (1.0 - ADAM_B1) * g
        nv = ADAM_B2 * v_ref[...] + (1.0 - ADAM_B2) * (g * g)
        m_hat = nm / (1.0 - ADAM_B1 ** ADAM_STEP)
        v_hat = nv / (1.0 - ADAM_B2 ** ADAM_STEP)
        d_ref[...] = -ADAM_LR * (m_hat / (jnp.sqrt(v_hat) + ADAM_EPS) + ADAM_WD * w_ref[...])
        nm_ref[...] = nm
        nv_ref[...] = nv

    spec = pl.BlockSpec((tr, cols), lambda i: (i, 0))
    return pl.pallas_call(
        body, grid=(rows // tr,), in_specs=[spec] * 4, out_specs=[spec] * 3,
        out_shape=[jax.ShapeDtypeStruct((rows, cols), F32)] * 3, name=name, compiler_params=_params())(w, g, m, v)


def _pack_rows(arrays, width):
    flat = jnp.concatenate([a.reshape(-1) for a in arrays])
    total = _round_up(flat.shape[0], 8 * width)
    flat = jnp.pad(flat, (0, total - flat.shape[0]))
    return flat.reshape(-1, width)


def _unpack_rows(packed, shapes):
    flat = packed.reshape(-1)
    out = []
    off = 0
    for shp in shapes:
        size = 1
        for s in shp:
            size *= s
        out.append(flat[off:off + size].reshape(shp))
        off += size
    return out


def _block_diag(w):
    h, hb, _ = w.shape
    per = BD // hb
    w4 = w.reshape(h // per, per, hb, hb)
    eye = jnp.eye(per, dtype=w.dtype)
    return jnp.einsum('npij,pq->npiqj', w4, eye).reshape(h // per, BD, BD)


def _block_diag_extract(bd, hb):
    nbk = bd.shape[0]
    per = BD // hb
    b5 = bd.reshape(nbk, per, hb, per, hb)
    eye = jnp.eye(per, dtype=bd.dtype)
    return jnp.einsum('npiqj,pq->npij', b5, eye).reshape(nbk * per, hb, hb)


def kernel(x, meta_tokens, ffn1_pre_g, ffn1_w_gate, ffn1_w_up, ffn1_w_down, ffn1_post_g, mix_pre_g, w_in, lru_conv_w, lru_conv_b, lru_w_a, lru_b_a, lru_w_x, lru_b_x, lru_lambda, sconv_w, lru_out_g, sconv_out_g, w_out, mix_post_g, ffn2_pre_g, ffn2_w_gate, ffn2_w_up, ffn2_w_down, ffn2_post_g, loss_target, m_meta_tokens, m_ffn1_pre_g, m_ffn1_w_gate, m_ffn1_w_up, m_ffn1_w_down, m_ffn1_post_g, m_mix_pre_g, m_w_in, m_lru_conv_w, m_lru_conv_b, m_lru_w_a, m_lru_b_a, m_lru_w_x, m_lru_b_x, m_lru_lambda, m_sconv_w, m_lru_out_g, m_sconv_out_g, m_w_out, m_mix_post_g, m_ffn2_pre_g, m_ffn2_w_gate, m_ffn2_w_up, m_ffn2_w_down, m_ffn2_post_g, v_meta_tokens, v_ffn1_pre_g, v_ffn1_w_gate, v_ffn1_w_up, v_ffn1_w_down, v_ffn1_post_g, v_mix_pre_g, v_w_in, v_lru_conv_w, v_lru_conv_b, v_lru_w_a, v_lru_b_a, v_lru_w_x, v_lru_b_x, v_lru_lambda, v_sconv_w, v_lru_out_g, v_sconv_out_g, v_w_out, v_mix_post_g, v_ffn2_pre_g, v_ffn2_w_gate, v_ffn2_w_up, v_ffn2_w_down, v_ffn2_post_g):
    given = dict(locals())
    wts = {n: given[n] for n in WEIGHT_NAMES}
    mom = {n: given["m_" + n] for n in WEIGHT_NAMES}
    var = {n: given["v_" + n] for n in WEIGHT_NAMES}

    xi, yi, ci = _position()
    me = _block_of(xi, yi, ci)
    x2 = x[0]
    seq, d = x2.shape
    n_meta = meta_tokens.shape[0]
    m_rows = _round_up(n_meta + seq, ROW_ALIGN)
    pad = m_rows - n_meta - seq
    lead = pad + n_meta
    c = lru_conv_b.shape[1]
    hb = lru_w_a.shape[-1]
    dm = meta_tokens.shape[1]
    cs_ = lru_conv_w.shape[2]
    kw4, kw3 = lru_conv_w.shape[1], sconv_w.shape[1]
    assert d == 2 * c and c % BD == 0 and BD % hb == 0 and cs_ <= dm and kw4 == 4 and kw3 == 3

    small = jnp.zeros((_round_up(n_meta + kw4 + kw3, 8), dm), F32)
    small = small.at[0:n_meta].set(meta_tokens)
    small = small.at[n_meta:n_meta + kw4, 0:cs_].set(lru_conv_w[0])
    small = small.at[n_meta + kw4:n_meta + kw4 + kw3, 0:cs_].set(sconv_w[0])
    sr = small.shape[0]
    small_all = _gather_small(small, False, "gather_small").reshape(N_DEV, sr, dm)
    meta_full = small_all[:, 0:n_meta, :].transpose(1, 0, 2).reshape(n_meta, d)
    conv_w_full = small_all[:, n_meta:n_meta + kw4, 0:cs_].transpose(1, 0, 2).reshape(kw4, c)
    sconv_w_full = small_all[:, n_meta + kw4:n_meta + kw4 + kw3, 0:cs_].transpose(1, 0, 2).reshape(kw3, c)

    big = ['ffn1_w_gate', 'ffn1_w_up', 'ffn1_w_down', 'w_in', 'w_out', 'ffn2_w_gate', 'ffn2_w_up', 'ffn2_w_down']
    col_sharded = {'ffn1_w_gate', 'ffn1_w_up', 'w_in', 'ffn2_w_gate', 'ffn2_w_up'}
    shards = []
    for nme in big:
        w = wts[nme][0].astype(WIRE_DTYPE)
        shards.append(w.T if nme in col_sharded else w)
    shard_rows = [s.shape[0] for s in shards]
    padded_rows = [_round_up(N_DEV * s.shape[0], LANE if nme in ('w_in', 'w_out') else F_ALIGN) for nme, s in zip(big, shards)]
    wg1, wu1, wd1, win_t, wout, wg2, wu2, wd2 = _gather_weights(shards, padded_rows, "gather_weights")

    pv = jnp.zeros((16, c), F32)
    pv = pv.at[0:4].set(conv_w_full).at[4].set(lru_conv_b[0]).at[5].set(lru_b_a[0]).at[6].set(lru_b_x[0])
    pv = pv.at[7].set(lru_lambda[0]).at[8:11].set(sconv_w_full).at[11].set(lru_out_g[0]).at[12].set(sconv_out_g[0])
    wa_bd = _block_diag(lru_w_a[0]).astype(MXU_DTYPE)
    wx_bd = _block_diag(lru_w_x[0]).astype(MXU_DTYPE)
    gs = c // N_GROUPS
    gidx = jnp.arange(BD) // gs
    gm = jnp.where(gidx[:, None] == gidx[None, :], 1.0 / gs, 0.0).astype(MXU_DTYPE)

    h0 = jnp.concatenate([jnp.zeros((pad, d), F32), meta_full, x2], axis=0)
    n1 = _rmsnorm(h0, ffn1_pre_g, "ffn1_prenorm")
    g1, u1, a1 = _ffn_gate_up(n1, wg1, wu1, "ffn1_gate_up")
    fo1, h1 = _mm_residual_norm(a1, wd1, h0, ffn1_post_g, 0.5, "ffn1_down")
    un = _rmsnorm(h1, mix_pre_g, "mix_prenorm")
    z = _mm_nt(un, win_t, "mix_in_proj")
    mixed, hs = _mixer_fwd(z, pv, wa_bd, wx_bd, gm, pad, "mixer_fwd")
    o_mix, h2 = _mm_residual_norm(mixed, wout, h1, mix_post_g, 1.0, "mix_out_proj")
    n2 = _rmsnorm(h2, ffn2_pre_g, "ffn2_prenorm")
    g2, u2, a2 = _ffn_gate_up(n2, wg2, wu2, "ffn2_gate_up")
    fo2, h3 = _mm_residual_norm(a2, wd2, h2, ffn2_post_g, 0.5, "ffn2_down")
    dh3, loss_part = _loss_grad(h3, loss_target[0], lead, "loss_grad")
    loss = lax.psum(loss_part[0, 0], ("x", "y", "c"))

    def ffn_bwd(tag, dh_out, h_in, n_act, g_act, u_act, a_act, fo, wg, wu, wd, pre_g, post_g):
        dfo, d_post = _norm_bwd(fo, post_g, dh_out, 0.5, tag + "_postnorm_bwd")
        dg_act, du_act = _ffn_hidden_bwd(dfo, wd, g_act, u_act, tag + "_hidden_bwd")
        gwd = _mm_tn(a_act, dfo, tag + "_dw_down")
        dh_in, d_pre = _mm_norm_bwd([(dg_act, wg), (du_act, wu)], h_in, pre_g, dh_out, tag + "_dx")
        gwg = _mm_tn(dg_act, n_act, tag + "_dw_gate")
        gwu = _mm_tn(du_act, n_act, tag + "_dw_up")
        return dh_in, d_pre, d_post, gwg, gwu, gwd

    dh2, d_pre2, d_post2, gwg2, gwu2, gwd2 = ffn_bwd("ffn2", dh3, h2, n2, g2, u2, a2, fo2, wg2, wu2, wd2, ffn2_pre_g, ffn2_post_g)
    do_mix, d_mix_post = _norm_bwd(o_mix, mix_post_g, dh2, 1.0, "mix_postnorm_bwd")
    dmixed = _mm_nt(do_mix, wout, "mix_out_proj_bwd")
    gwout = _mm_tn(mixed, do_mix, "mix_dw_out")
    dz, dpv, dwa_bd, dwx_bd = _mixer_bwd(z, hs, dmixed, pv, wa_bd, wx_bd, gm, pad, "mixer_bwd")
    dh1, d_mix_pre = _mm_norm_bwd([(dz, win_t)], h1, mix_pre_g, dh2, "mix_dx")
    gwin = _mm_tn(dz, un, "mix_dw_in")
    dh0, d_pre1, d_post1, gwg1, gwu1, gwd1 = ffn_bwd("ffn1", dh1, h0, n1, g1, u1, a1, fo1, wg1, wu1, wd1, ffn1_pre_g, ffn1_post_g)
    grad_x = dh0[lead:][None]

    received = _exchange_grads([gwg1, gwu1, gwd1, gwin, gwout, gwg2, gwu2, gwd2], shard_rows, "exchange_grads")
    own_block = jnp.reshape(me, (1,)).astype(jnp.int32)
    grads = {}
    for nme, gfull, rec in zip(big, [gwg1, gwu1, gwd1, gwin, gwout, gwg2, gwu2, gwd2], received):
        gsum = _sum_shard(own_block, gfull, rec, "sum_" + nme)
        grads[nme] = (gsum.T if nme in col_sharded else gsum)[None]

    small_names = ['ffn1_pre_g', 'ffn1_post_g', 'mix_pre_g', 'mix_post_g', 'ffn2_pre_g', 'ffn2_post_g',
                   'lru_conv_b', 'lru_b_a', 'lru_b_x', 'lru_lambda', 'lru_out_g', 'sconv_out_g',
                   'lru_conv_w', 'sconv_w', 'lru_w_a', 'lru_w_x', 'meta_tokens']
    small_parts = [d_pre1, d_post1, d_mix_pre, d_mix_post, d_pre2, d_post2,
                   dpv[4:5], dpv[5:6], dpv[6:7], dpv[7:8], dpv[11:12], dpv[12:13],
                   dpv[0:4], dpv[8:11], _block_diag_extract(dwa_bd, hb), _block_diag_extract(dwx_bd, hb),
                   dh0[pad:lead]]
    small_shapes = [p.shape for p in small_parts]
    small_sum = _gather_small(_pack_rows(small_parts, d), True, "reduce_small")
    for nme, gsm in zip(small_names, _unpack_rows(small_sum, small_shapes)):
        if nme == 'meta_tokens':
            grads[nme] = lax.dynamic_slice_in_dim(gsm, me * dm, dm, axis=1)
        elif nme in ('lru_conv_w', 'sconv_w'):
            grads[nme] = lax.dynamic_slice_in_dim(gsm, me * cs_, cs_, axis=1)[None]
        else:
            grads[nme] = gsm.reshape(wts[nme].shape)

    delta, new_m, new_v = {}, {}, {}
    for nme in big:
        shp = wts[nme].shape
        dl, nm_, nv_ = _adamw(wts[nme][0], grads[nme][0], mom[nme][0], var[nme][0], "adamw_" + nme)
        delta[nme], new_m[nme], new_v[nme] = dl.reshape(shp), nm_.reshape(shp), nv_.reshape(shp)
    rest = [n for n in WEIGHT_NAMES if n not in big]
    rest_shapes = [wts[n].shape for n in rest]
    packed = [_pack_rows([src[n] for n in rest], LANE) for src in (wts, grads, mom, var)]
    for out, packed_out in zip((delta, new_m, new_v), _adamw(*packed, "adamw_small")):
        for nme, arr in zip(rest, _unpack_rows(packed_out, rest_shapes)):
            out[nme] = arr

    return (loss, grad_x, *[grads[n] for n in WEIGHT_NAMES], *[delta[n] for n in WEIGHT_NAMES],
            *[new_m[n] for n in WEIGHT_NAMES], *[new_v[n] for n in WEIGHT_NAMES])
```

```python
import functools

import jax
import jax.numpy as jnp
from jax import lax
from jax.experimental import pallas as pl
from jax.experimental.pallas import tpu as pltpu

F32 = jnp.float32
MXU_DTYPE = jnp.bfloat16
WIRE_DTYPE = jnp.bfloat16
MESH = pl.DeviceIdType.MESH

EPS = 1e-6
LRU_C = 8.0
N_GROUPS = 16
ADAM_LR = 0.001
ADAM_B1 = 0.9
ADAM_B2 = 0.999
ADAM_EPS = 1e-08
ADAM_WD = 0.01
ADAM_STEP = 10

N_DEV = 8
LANE = 128
SUBLANE_BF16 = 16
ROW_ALIGN = 128
F_ALIGN = 512
BD = 256
MIX_ROWS = 128
VMEM_LIMIT_MB = 56

WEIGHT_NAMES = ['meta_tokens', 'ffn1_pre_g', 'ffn1_w_gate', 'ffn1_w_up', 'ffn1_w_down', 'ffn1_post_g',
                'mix_pre_g', 'w_in', 'lru_conv_w', 'lru_conv_b', 'lru_w_a', 'lru_b_a', 'lru_w_x', 'lru_b_x',
                'lru_lambda', 'sconv_w', 'lru_out_g', 'sconv_out_g', 'w_out', 'mix_post_g', 'ffn2_pre_g',
                'ffn2_w_gate', 'ffn2_w_up', 'ffn2_w_down', 'ffn2_post_g']


def _round_up(n, q):
    return (n + q - 1) // q * q


def _tile(n, target, q):
    best = None
    t = q
    while t <= min(n, target):
        if n % t == 0:
            best = t
        t += q
    assert best is not None, (n, target, q)
    return best


def _params(**kw):
    return pltpu.CompilerParams(vmem_limit_bytes=VMEM_LIMIT_MB << 20, **kw)


def _call(body, *, grid, in_specs, out_specs, out_shape, name, args, scratch_shapes=(), carried=(), prefetch=()):
    carried = list(carried)
    n_pf = len(prefetch)

    def launch(fn, in_specs_, out_specs_, out_shape_, scratch_, operands, aliases_):
        if n_pf:
            spec = pltpu.PrefetchScalarGridSpec(num_scalar_prefetch=n_pf, grid=grid, in_specs=in_specs_,
                                                out_specs=out_specs_, scratch_shapes=scratch_)
            return pl.pallas_call(fn, grid_spec=spec, out_shape=out_shape_, input_output_aliases=aliases_,
                                  name=name, compiler_params=_params())(*prefetch, *operands)
        return pl.pallas_call(fn, grid=grid, in_specs=in_specs_, out_specs=out_specs_, out_shape=out_shape_,
                              scratch_shapes=scratch_, input_output_aliases=aliases_, name=name,
                              compiler_params=_params())(*operands)

    if not carried:
        return launch(body, in_specs, out_specs, out_shape, list(scratch_shapes), args, {})
    single = not isinstance(out_shape, (list, tuple))
    out_specs_l = [out_specs] if single else list(out_specs)
    out_shape_l = [out_shape] if single else list(out_shape)
    n_in, n_out, n_scr = len(in_specs), len(out_specs_l), len(scratch_shapes)
    hbm = pl.BlockSpec(memory_space=pl.ANY)
    c_in = [a for cm in carried for a in cm.arrays]
    c_out = [s for cm in carried for s in cm.out_shapes]
    c_scr = []
    aliases = {}
    in_off, out_off = n_pf + n_in, n_out
    for cm in carried:
        c_scr += [pltpu.SemaphoreType.DMA((cm.n_remote,)), pltpu.SemaphoreType.DMA((cm.n_remote,)),
                  pltpu.SemaphoreType.DMA((max(cm.n_local, 1),))]
        for k, v in cm.aliases.items():
            aliases[in_off + k] = out_off + v
        in_off += len(cm.arrays)
        out_off += len(cm.out_shapes)
    steps = 1
    for g in grid:
        steps *= g
    forward_step = min((3 * steps) // 4, steps - 1)

    def wrapped(*refs):
        pf = refs[:n_pf]
        p = n_pf
        ins = refs[p:p + n_in]
        p += n_in
        cins = refs[p:p + len(c_in)]
        p += len(c_in)
        outs = refs[p:p + n_out]
        p += n_out
        couts = refs[p:p + len(c_out)]
        p += len(c_out)
        scr = refs[p:p + n_scr]
        csem = refs[p + n_scr:]
        lin = 0
        for axis, g in enumerate(grid):
            lin = lin * g + pl.program_id(axis)
        views = []
        io = oo = 0
        for j, cm in enumerate(carried):
            views.append((cins[io:io + len(cm.arrays)], couts[oo:oo + len(cm.out_shapes)],
                          csem[3 * j], csem[3 * j + 1], csem[3 * j + 2]))
            io += len(cm.arrays)
            oo += len(cm.out_shapes)

        @pl.when(lin == 0)
        def _():
            for cm, v in zip(carried, views):
                cm.start(*v)

        body(*pf, *ins, *outs, *scr)

        @pl.when(lin == forward_step)
        def _():
            for cm, v in zip(carried, views):
                cm.forward(*v)

        @pl.when(lin == steps - 1)
        def _():
            for cm, v in zip(carried, views):
                cm.finish(*v)

    res = launch(wrapped, list(in_specs) + [hbm] * len(c_in), out_specs_l + [hbm] * len(c_out),
                 out_shape_l + c_out, list(scratch_shapes) + c_scr, (*args, *c_in), aliases)
    oo = n_out
    for cm in carried:
        cm.results = list(res[oo:oo + len(cm.out_shapes)])
        oo += len(cm.out_shapes)
    return res[0] if single else list(res[:n_out])


def _rmsnorm(h, g, name, carried=()):
    m, d = h.shape
    tm = _tile(m, 528, SUBLANE_BF16)

    def body(h_ref, g_ref, o_ref):
        x = h_ref[...]
        r = lax.rsqrt(jnp.mean(x * x, axis=-1, keepdims=True) + EPS)
        o_ref[...] = (x * r * g_ref[...]).astype(o_ref.dtype)

    return _call(
        body, grid=(m // tm,),
        in_specs=[pl.BlockSpec((tm, d), lambda i: (i, 0)), pl.BlockSpec((1, d), lambda i: (0, 0))],
        out_specs=pl.BlockSpec((tm, d), lambda i: (i, 0)),
        out_shape=jax.ShapeDtypeStruct((m, d), MXU_DTYPE), name=name, args=(h, g), carried=carried)


def _rmsnorm_bwd_rows(x, g, dy):
    r = lax.rsqrt(jnp.mean(x * x, axis=-1, keepdims=True) + EPS)
    xh = x * r
    dyh = dy * g
    dx = r * (dyh - xh * jnp.mean(dyh * xh, axis=-1, keepdims=True))
    return dx, dy * xh


def _norm_bwd(x, g, dy, scale, name, carried=()):
    m, d = x.shape
    tm = _tile(m, 528, SUBLANE_BF16)

    def body(x_ref, g_ref, dy_ref, dx_ref, dg_ref):
        @pl.when(pl.program_id(0) == 0)
        def _():
            dg_ref[...] = jnp.zeros_like(dg_ref)

        dx, dgr = _rmsnorm_bwd_rows(x_ref[...], g_ref[...], scale * dy_ref[...])
        dx_ref[...] = dx.astype(dx_ref.dtype)
        dg_ref[...] += jnp.sum(dgr, axis=0, keepdims=True)

    return _call(
        body, grid=(m // tm,),
        in_specs=[pl.BlockSpec((tm, d), lambda i: (i, 0)), pl.BlockSpec((1, d), lambda i: (0, 0)),
                  pl.BlockSpec((tm, d), lambda i: (i, 0))],
        out_specs=[pl.BlockSpec((tm, d), lambda i: (i, 0)), pl.BlockSpec((1, d), lambda i: (0, 0))],
        out_shape=[jax.ShapeDtypeStruct((m, d), MXU_DTYPE), jax.ShapeDtypeStruct((1, d), F32)],
        name=name, args=(x, g, dy), carried=carried)


def _loss_grad(h, target, lead, name):
    m, d = h.shape
    tl = ROW_ALIGN
    assert lead % tl == 0 and target.shape[0] == m - lead
    lead_blocks = lead // tl

    def body(h_ref, t_ref, dh_ref, l_ref):
        i = pl.program_id(0)

        @pl.when(i == 0)
        def _():
            l_ref[...] = jnp.zeros_like(l_ref)

        @pl.when(i < lead_blocks)
        def _():
            dh_ref[...] = jnp.zeros_like(dh_ref)

        @pl.when(i >= lead_blocks)
        def _():
            e = h_ref[...] - t_ref[...]
            dh_ref[...] = e * (1.0 / d)
            row = jnp.sum(e * e, axis=-1, keepdims=True) * (1.0 / d)
            l_ref[...] += 0.5 * jnp.sum(row, axis=0, keepdims=True)

    return pl.pallas_call(
        body, grid=(m // tl,),
        in_specs=[pl.BlockSpec((tl, d), lambda i: (i, 0)),
                  pl.BlockSpec((tl, d), lambda i: (jnp.maximum(i - lead_blocks, 0), 0))],
        out_specs=[pl.BlockSpec((tl, d), lambda i: (i, 0)), pl.BlockSpec((1, 1), lambda i: (0, 0))],
        out_shape=[jax.ShapeDtypeStruct((m, d), F32), jax.ShapeDtypeStruct((1, 1), F32)],
        name=name, compiler_params=_params())(h, target)


def _dot_nt(a, b):
    return lax.dot_general(a, b, (((1,), (1,)), ((), ())), preferred_element_type=F32)


def _dot_tn(a, b):
    return lax.dot_general(a, b, (((0,), (0,)), ((), ())), preferred_element_type=F32)


def _mm_nt(a, w, name, carried=()):
    m, k = a.shape
    n = w.shape[0]
    tm = _tile(m, 1056, SUBLANE_BF16)
    tn = _tile(n, 512, LANE)

    def body(a_ref, w_ref, o_ref):
        o_ref[...] = _dot_nt(a_ref[...], w_ref[...])

    return _call(
        body, grid=(m // tm, n // tn),
        in_specs=[pl.BlockSpec((tm, k), lambda i, j: (i, 0)), pl.BlockSpec((tn, k), lambda i, j: (j, 0))],
        out_specs=pl.BlockSpec((tm, tn), lambda i, j: (i, j)),
        out_shape=jax.ShapeDtypeStruct((m, n), F32), name=name, args=(a, w), carried=carried)


def _ffn_gate_up(n_act, wg_t, wu_t, name, carried=()):
    m, d = n_act.shape
    fp = wg_t.shape[0]
    tm = _tile(m, 1056, SUBLANE_BF16)
    tn = _tile(fp, 512, LANE)

    def body(n_ref, wg_ref, wu_ref, g_ref, u_ref, a_ref):
        n = n_ref[...]
        g = _dot_nt(n, wg_ref[...])
        u = _dot_nt(n, wu_ref[...])
        g_ref[...] = g.astype(g_ref.dtype)
        u_ref[...] = u.astype(u_ref.dtype)
        a_ref[...] = (g * jax.nn.sigmoid(g) * u).astype(a_ref.dtype)

    act = pl.BlockSpec((tm, tn), lambda i, j: (i, j))
    wsp = pl.BlockSpec((tn, d), lambda i, j: (j, 0))
    return _call(
        body, grid=(m // tm, fp // tn),
        in_specs=[pl.BlockSpec((tm, d), lambda i, j: (i, 0)), wsp, wsp],
        out_specs=[act, act, act],
        out_shape=[jax.ShapeDtypeStruct((m, fp), MXU_DTYPE)] * 3, name=name, args=(n_act, wg_t, wu_t), carried=carried)


def _ffn_hidden_bwd(dfo, wd, g_act, u_act, name, carried=()):
    m, d = dfo.shape
    fp = wd.shape[0]
    tm = _tile(m, 1056, SUBLANE_BF16)
    tn = _tile(fp, 512, LANE)

    def body(df_ref, wd_ref, g_ref, u_ref, dg_ref, du_ref):
        da = _dot_nt(df_ref[...], wd_ref[...])
        g = g_ref[...].astype(F32)
        u = u_ref[...].astype(F32)
        s = jax.nn.sigmoid(g)
        du_ref[...] = (da * (g * s)).astype(du_ref.dtype)
        dg_ref[...] = (da * u * (s * (1.0 + g * (1.0 - s)))).astype(dg_ref.dtype)

    act = pl.BlockSpec((tm, tn), lambda i, j: (i, j))
    return _call(
        body, grid=(m // tm, fp // tn),
        in_specs=[pl.BlockSpec((tm, d), lambda i, j: (i, 0)), pl.BlockSpec((tn, d), lambda i, j: (j, 0)), act, act],
        out_specs=[act, act],
        out_shape=[jax.ShapeDtypeStruct((m, fp), MXU_DTYPE)] * 2, name=name, args=(dfo, wd, g_act, u_act),
        carried=carried)


def _mm_residual_norm(a, w, h, g, scale, name, carried=()):
    m, k = a.shape
    d = w.shape[1]
    tm = _tile(m, 528, SUBLANE_BF16)
    tk = _tile(k, 512, LANE)
    nk = k // tk

    def body(a_ref, w_ref, h_ref, g_ref, fo_ref, hn_ref, acc_ref):
        kk = pl.program_id(1)

        @pl.when(kk == 0)
        def _():
            acc_ref[...] = jnp.zeros_like(acc_ref)

        acc_ref[...] += jnp.dot(a_ref[...], w_ref[...], preferred_element_type=F32)

        @pl.when(kk == nk - 1)
        def _():
            fo = acc_ref[...]
            fo_ref[...] = fo
            r = lax.rsqrt(jnp.mean(fo * fo, axis=-1, keepdims=True) + EPS)
            hn_ref[...] = h_ref[...] + scale * (fo * r * g_ref[...])

    row = pl.BlockSpec((tm, d), lambda i, kk: (i, 0))
    return _call(
        body, grid=(m // tm, nk),
        in_specs=[pl.BlockSpec((tm, tk), lambda i, kk: (i, kk)), pl.BlockSpec((tk, d), lambda i, kk: (kk, 0)),
                  row, pl.BlockSpec((1, d), lambda i, kk: (0, 0))],
        out_specs=[row, row],
        out_shape=[jax.ShapeDtypeStruct((m, d), F32)] * 2,
        scratch_shapes=[pltpu.VMEM((tm, d), F32)], name=name, args=(a, w, h, g), carried=carried)


def _mm_norm_bwd(pairs, h, g, dh_up, name, carried=()):
    n_pairs = len(pairs)
    m, k = pairs[0][0].shape
    d = h.shape[1]
    tm = _tile(m, 528, SUBLANE_BF16)
    tk = _tile(k, 512, LANE)
    nk = k // tk

    def body(*refs):
        ops = refs[:2 * n_pairs]
        h_ref, g_ref, up_ref, dh_ref, dg_ref, acc_ref = refs[2 * n_pairs:]
        i = pl.program_id(0)
        kk = pl.program_id(1)

        @pl.when(jnp.logical_and(i == 0, kk == 0))
        def _():
            dg_ref[...] = jnp.zeros_like(dg_ref)

        @pl.when(kk == 0)
        def _():
            acc_ref[...] = jnp.zeros_like(acc_ref)

        for p in range(n_pairs):
            acc_ref[...] += jnp.dot(ops[2 * p][...], ops[2 * p + 1][...], preferred_element_type=F32)

        @pl.when(kk == nk - 1)
        def _():
            dx, dgr = _rmsnorm_bwd_rows(h_ref[...], g_ref[...], acc_ref[...])
            dh_ref[...] = up_ref[...] + dx
            dg_ref[...] += jnp.sum(dgr, axis=0, keepdims=True)

    row = pl.BlockSpec((tm, d), lambda i, kk: (i, 0))
    vec = pl.BlockSpec((1, d), lambda i, kk: (0, 0))
    in_specs = []
    args = []
    for a, w in pairs:
        in_specs += [pl.BlockSpec((tm, tk), lambda i, kk: (i, kk)), pl.BlockSpec((tk, d), lambda i, kk: (kk, 0))]
        args += [a, w]
    return _call(
        body, grid=(m // tm, nk),
        in_specs=in_specs + [row, vec, row], out_specs=[row, vec],
        out_shape=[jax.ShapeDtypeStruct((m, d), F32), jax.ShapeDtypeStruct((1, d), F32)],
        scratch_shapes=[pltpu.VMEM((tm, d), F32)], name=name, args=(*args, h, g, dh_up), carried=carried)


def _mm_tn(a, b, name, carried=()):
    m, ka = a.shape
    d = b.shape[1]
    tf = _tile(ka, 512, LANE)
    tm = _tile(m, 1056, SUBLANE_BF16)
    nm = m // tm

    def body(a_ref, b_ref, o_ref, acc_ref):
        mm = pl.program_id(1)

        @pl.when(mm == 0)
        def _():
            acc_ref[...] = jnp.zeros_like(acc_ref)

        acc_ref[...] += _dot_tn(a_ref[...], b_ref[...])

        @pl.when(mm == nm - 1)
        def _():
            o_ref[...] = acc_ref[...].astype(o_ref.dtype)

    return _call(
        body, grid=(ka // tf, nm),
        in_specs=[pl.BlockSpec((tm, tf), lambda j, mm: (mm, j)), pl.BlockSpec((tm, d), lambda j, mm: (mm, 0))],
        out_specs=pl.BlockSpec((tf, d), lambda j, mm: (j, 0)),
        out_shape=jax.ShapeDtypeStruct((ka, d), WIRE_DTYPE),
        scratch_shapes=[pltpu.VMEM((tf, d), F32)], name=name, args=(a, b), carried=carried)


GELU_K = 0.7978845608028654
GELU_C = 0.044715


def _expm1(x):
    series = x * (1.0 + x * (1.0 / 2 + x * (1.0 / 6 + x * (1.0 / 24 + x * (1.0 / 120 + x * (1.0 / 720 + x * (1.0 / 5040)))))))
    return jnp.where(jnp.abs(x) < 0.3, series, jnp.exp(x) - 1.0)


def _softplus(x):
    return jnp.maximum(x, 0.0) + jnp.log1p(jnp.exp(-jnp.abs(x)))


def _block_mm(v, w_ref, transposed):
    nbk = w_ref.shape[0]
    outs = []
    for j in range(nbk):
        vj = v[:, j * BD:(j + 1) * BD]
        outs.append(_dot_nt(vj, w_ref[j]) if transposed else jnp.dot(vj, w_ref[j], preferred_element_type=F32))
    return outs[0] if nbk == 1 else jnp.concatenate(outs, axis=1)


def _group_mean(q, gm_ref):
    hi = q.astype(MXU_DTYPE)
    lo = (q - hi.astype(F32)).astype(MXU_DTYPE)
    nbk = q.shape[1] // BD
    gm = gm_ref[...]
    outs = []
    for j in range(nbk):
        sl = slice(j * BD, (j + 1) * BD)
        outs.append(jnp.dot(hi[:, sl], gm, preferred_element_type=F32) + jnp.dot(lo[:, sl], gm, preferred_element_type=F32))
    return outs[0] if nbk == 1 else jnp.concatenate(outs, axis=1)


class _RowReader:
    def __init__(self, ref):
        self.ref = ref

    def __getitem__(self, rows):
        return self.ref[rows, :]


def _shifted(ext_ref, cur, before8, after8, downs=(), ups=()):
    r = cur.shape[0]
    if downs:
        ext_ref[0:8, :] = before8
    ext_ref[8:8 + r, :] = cur
    if ups:
        ext_ref[8 + r:16 + r, :] = after8
    return [ext_ref[pl.ds(8 - j, r), :] for j in downs] + [ext_ref[pl.ds(8 + j, r), :] for j in ups]


def _lru_gates(xc, pv, wa_ref, wx_ref):
    xcb = xc.astype(MXU_DTYPE)
    ga = jax.nn.sigmoid(_block_mm(xcb, wa_ref, False) + pv[5:6])
    gx = jax.nn.sigmoid(_block_mm(xcb, wx_ref, False) + pv[6:7])
    sp = _softplus(-pv[7:8])
    log_a = -LRU_C * ga * sp
    a = jnp.exp(log_a)
    e2 = _expm1(2.0 * log_a)
    mult = jnp.sqrt(-e2)
    return xcb, ga, gx, sp, a, e2, mult


def _gelu_parts(y):
    th = jnp.tanh(GELU_K * (y + GELU_C * y * y * y))
    return 0.5 * y * (1.0 + th), th


def _mixer_fwd(z, pv, wa, wx, gm, pad, name, carried=()):
    m = z.shape[0]
    c = pv.shape[1]
    r = MIX_ROWS
    nb = m // r

    def body(z_ref, pv_ref, wa_ref, wx_ref, gm_ref, mixed_ref, hs_ref, ext_ref, tailx_ref, tailc_ref, carry_ref):
        b = pl.program_id(0)

        @pl.when(b == 0)
        def _():
            tailx_ref[...] = jnp.zeros_like(tailx_ref)
            tailc_ref[...] = jnp.zeros_like(tailc_ref)
            carry_ref[...] = jnp.zeros_like(carry_ref)

        pv = _RowReader(pv_ref)
        row = b * r + lax.broadcasted_iota(jnp.int32, (r, 1), 0)
        lrow = lax.broadcasted_iota(jnp.int32, (r, c), 0)
        maskf = (row >= pad).astype(F32)
        y = z_ref[:, 0:c]
        xl = z_ref[:, c:2 * c]
        bs = z_ref[:, 2 * c:3 * c]
        cv = z_ref[:, 3 * c:4 * c] * z_ref[:, 4 * c:5 * c]

        x1, x2, x3 = _shifted(ext_ref, xl, tailx_ref[...], None, downs=(1, 2, 3))
        tailx_ref[...] = z_ref[pl.ds(r - 8, 8), c:2 * c]
        xc = pv[4:5] + pv[3:4] * xl + pv[2:3] * x1 + pv[1:2] * x2 + pv[0:1] * x3
        _, _, gx, _, a, _, mult = _lru_gates(xc, pv, wa_ref, wx_ref)
        uu = mult * (gx * xc) * maskf

        acc_a = a
        acc_h = uu
        dlt = 1
        while dlt < r:
            keep = lrow >= dlt
            sh_a = pltpu.roll(acc_a, dlt, axis=0)
            sh_h = pltpu.roll(acc_h, dlt, axis=0)
            acc_h = acc_h + acc_a * jnp.where(keep, sh_h, 0.0)
            acc_a = acc_a * jnp.where(keep, sh_a, 1.0)
            dlt *= 2
        hs = acc_h + acc_a * carry_ref[...]
        hs_ref[...] = hs
        carry_ref[...] = hs_ref[pl.ds(r - 1, 1), :]

        gelu_y, _ = _gelu_parts(y)
        lru_out = hs * gelu_y
        c1, c2 = _shifted(ext_ref, cv, tailc_ref[...], None, downs=(1, 2))
        tailc_ref[...] = cv[r - 8:r]
        sc_out = bs * (pv[10:11] * cv + pv[9:10] * c1 + pv[8:9] * c2)

        rl = lax.rsqrt(_group_mean(lru_out * lru_out, gm_ref) + EPS)
        rs = lax.rsqrt(_group_mean(sc_out * sc_out, gm_ref) + EPS)
        mixed_ref[:, 0:c] = (lru_out * rl * pv[11:12]).astype(mixed_ref.dtype)
        mixed_ref[:, c:2 * c] = (sc_out * rs * pv[12:13]).astype(mixed_ref.dtype)

    full = lambda shape: pl.BlockSpec(shape, lambda b: (0,) * len(shape))
    return _call(
        body, grid=(nb,),
        in_specs=[pl.BlockSpec((r, 5 * c), lambda b: (b, 0)), full(pv.shape), full(wa.shape), full(wx.shape), full(gm.shape)],
        out_specs=[pl.BlockSpec((r, 2 * c), lambda b: (b, 0)), pl.BlockSpec((r, c), lambda b: (b, 0))],
        out_shape=[jax.ShapeDtypeStruct((m, 2 * c), MXU_DTYPE), jax.ShapeDtypeStruct((m, c), F32)],
        scratch_shapes=[pltpu.VMEM((r + 16, c), F32), pltpu.VMEM((8, c), F32), pltpu.VMEM((8, c), F32),
                        pltpu.VMEM((1, c), F32)],
        name=name, args=(z, pv, wa, wx, gm), carried=carried)


def _mixer_bwd(z, hs, dmixed, pv, wa, wx, gm, pad, name, carried=()):
    m = z.shape[0]
    c = pv.shape[1]
    r = MIX_ROWS
    nb = m // r
    r8 = r // 8

    def body(z_ref, zp_ref, hs_ref, hsp_ref, dm_ref, pv_ref, wa_ref, wx_ref, gm_ref,
             dz_ref, dpv_ref, dwa_ref, dwx_ref, ext_ref, hxc_ref, hsc_ref, hp_ref):
        i = pl.program_id(0)
        b = nb - 1 - i

        @pl.when(i == 0)
        def _():
            hxc_ref[...] = jnp.zeros_like(hxc_ref)
            hsc_ref[...] = jnp.zeros_like(hsc_ref)
            hp_ref[...] = jnp.zeros_like(hp_ref)
            dpv_ref[...] = jnp.zeros_like(dpv_ref)
            dwa_ref[...] = jnp.zeros_like(dwa_ref)
            dwx_ref[...] = jnp.zeros_like(dwx_ref)

        pv = _RowReader(pv_ref)
        row = b * r + lax.broadcasted_iota(jnp.int32, (r, 1), 0)
        lrow = lax.broadcasted_iota(jnp.int32, (r, c), 0)
        maskf = (row >= pad).astype(F32)
        has_prev = (b > 0).astype(F32)
        y = z_ref[:, 0:c]
        xl = z_ref[:, c:2 * c]
        bs = z_ref[:, 2 * c:3 * c]
        cs = z_ref[:, 3 * c:4 * c]
        vs = z_ref[:, 4 * c:5 * c]
        cv = cs * vs
        xl_prev = zp_ref[:, c:2 * c] * has_prev
        cv_prev = zp_ref[:, 3 * c:4 * c] * zp_ref[:, 4 * c:5 * c] * has_prev
        hs = hs_ref[...]

        x1, x2, x3 = _shifted(ext_ref, xl, xl_prev, None, downs=(1, 2, 3))
        xc = pv[4:5] + pv[3:4] * xl + pv[2:3] * x1 + pv[1:2] * x2 + pv[0:1] * x3
        xcb, ga, gx, sp, a, e2, mult = _lru_gates(xc, pv, wa_ref, wx_ref)
        gxx = gx * xc
        gelu_y, th = _gelu_parts(y)
        lru_out = hs * gelu_y
        c1, c2 = _shifted(ext_ref, cv, cv_prev, None, downs=(1, 2))
        sc = pv[10:11] * cv + pv[9:10] * c1 + pv[8:9] * c2
        sc_out = bs * sc

        def group_norm_bwd(v, dm, gain):
            rr = lax.rsqrt(_group_mean(v * v, gm_ref) + EPS)
            vh = v * rr
            dvh = dm * gain
            dv = rr * (dvh - vh * _group_mean(dvh * vh, gm_ref))
            return dv, jnp.sum(dm * vh, axis=0, keepdims=True)

        d_lru_out, d_og = group_norm_bwd(lru_out, dm_ref[:, 0:c], pv[11:12])
        d_sc_out, d_sg = group_norm_bwd(sc_out, dm_ref[:, c:2 * c], pv[12:13])
        dpv_ref[11:12, :] += d_og
        dpv_ref[12:13, :] += d_sg

        dhs = d_lru_out * gelu_y
        dgelu = 0.5 * (1.0 + th) + 0.5 * y * (1.0 - th * th) * GELU_K * (1.0 + 3.0 * GELU_C * y * y)
        dy = d_lru_out * hs * dgelu

        acc_a = a
        acc_p = a * dhs
        dlt = 1
        while dlt < r:
            keep = lrow < r - dlt
            sh_a = pltpu.roll(acc_a, r - dlt, axis=0)
            sh_p = pltpu.roll(acc_p, r - dlt, axis=0)
            acc_p = acc_p + acc_a * jnp.where(keep, sh_p, 0.0)
            acc_a = acc_a * jnp.where(keep, sh_a, 1.0)
            dlt *= 2
        p_all = acc_p + acc_a * hp_ref[0:1, :]
        (p_next,) = _shifted(ext_ref, p_all, None, hp_ref[...], ups=(1,))
        hp_ref[...] = p_all[0:8]
        q = dhs + p_next
        (hs_prev,) = _shifted(ext_ref, hs, hsp_ref[...] * has_prev, None, downs=(1,))
        duu = q * maskf
        da = q * hs_prev

        dmult = duu * gxx
        dgxx = duu * mult
        dgx = dgxx * xc
        dxc = dgxx * gx
        dlog_a = da * a - dmult * ((1.0 + e2) / mult)
        dga = dlog_a * (-LRU_C * sp)
        dsp = jnp.sum(dlog_a * (-LRU_C * ga), axis=0, keepdims=True)
        dpv_ref[7:8, :] += dsp * (-jax.nn.sigmoid(-pv[7:8]))
        dga_pre = dga * ga * (1.0 - ga)
        dgx_pre = dgx * gx * (1.0 - gx)
        dpv_ref[5:6, :] += jnp.sum(dga_pre, axis=0, keepdims=True)
        dpv_ref[6:7, :] += jnp.sum(dgx_pre, axis=0, keepdims=True)
        dga_b = dga_pre.astype(MXU_DTYPE)
        dgx_b = dgx_pre.astype(MXU_DTYPE)
        dxc = dxc + _block_mm(dga_b, wa_ref, True) + _block_mm(dgx_b, wx_ref, True)
        for j in range(c // BD):
            sl = slice(j * BD, (j + 1) * BD)
            dwa_ref[j] += _dot_tn(xcb[:, sl], dga_b[:, sl])
            dwx_ref[j] += _dot_tn(xcb[:, sl], dgx_b[:, sl])

        dpv_ref[4:5, :] += jnp.sum(dxc, axis=0, keepdims=True)
        dpv_ref[3:4, :] += jnp.sum(dxc * xl, axis=0, keepdims=True)
        dpv_ref[2:3, :] += jnp.sum(dxc * x1, axis=0, keepdims=True)
        dpv_ref[1:2, :] += jnp.sum(dxc * x2, axis=0, keepdims=True)
        dpv_ref[0:1, :] += jnp.sum(dxc * x3, axis=0, keepdims=True)
        u1, u2, u3 = _shifted(ext_ref, dxc, None, hxc_ref[...], ups=(1, 2, 3))
        hxc_ref[...] = dxc[0:8]
        dxl = pv[3:4] * dxc + pv[2:3] * u1 + pv[1:2] * u2 + pv[0:1] * u3

        dbs = d_sc_out * sc
        dsc = d_sc_out * bs
        dpv_ref[10:11, :] += jnp.sum(dsc * cv, axis=0, keepdims=True)
        dpv_ref[9:10, :] += jnp.sum(dsc * c1, axis=0, keepdims=True)
        dpv_ref[8:9, :] += jnp.sum(dsc * c2, axis=0, keepdims=True)
        s1, s2 = _shifted(ext_ref, dsc, None, hsc_ref[...], ups=(1, 2))
        hsc_ref[...] = dsc[0:8]
        dcv = pv[10:11] * dsc + pv[9:10] * s1 + pv[8:9] * s2

        dz_ref[:, 0:c] = (dy * maskf).astype(dz_ref.dtype)
        dz_ref[:, c:2 * c] = (dxl * maskf).astype(dz_ref.dtype)
        dz_ref[:, 2 * c:3 * c] = (dbs * maskf).astype(dz_ref.dtype)
        dz_ref[:, 3 * c:4 * c] = (dcv * vs * maskf).astype(dz_ref.dtype)
        dz_ref[:, 4 * c:5 * c] = (dcv * cs * maskf).astype(dz_ref.dtype)

    full = lambda shape: pl.BlockSpec(shape, lambda i: (0,) * len(shape))
    cur = lambda width: pl.BlockSpec((r, width), lambda i: (nb - 1 - i, 0))
    prev8 = lambda width: pl.BlockSpec((8, width), lambda i: (jnp.maximum((nb - 1 - i) * r8 - 1, 0), 0))
    return _call(
        body, grid=(nb,),
        in_specs=[cur(5 * c), prev8(5 * c), cur(c), prev8(c), cur(2 * c),
                  full(pv.shape), full(wa.shape), full(wx.shape), full(gm.shape)],
        out_specs=[cur(5 * c), full(pv.shape), full(wa.shape), full(wx.shape)],
        out_shape=[jax.ShapeDtypeStruct((m, 5 * c), MXU_DTYPE), jax.ShapeDtypeStruct(pv.shape, F32),
                   jax.ShapeDtypeStruct(wa.shape, F32), jax.ShapeDtypeStruct(wx.shape, F32)],
        scratch_shapes=[pltpu.VMEM((r + 16, c), F32), pltpu.VMEM((8, c), F32), pltpu.VMEM((8, c), F32),
                        pltpu.VMEM((8, c), F32)],
        name=name, args=(z, z, hs, hs, dmixed, pv, wa, wx, gm), carried=carried)


def _position():
    return lax.axis_index("x"), lax.axis_index("y"), lax.axis_index("c")


def _block_of(px, py, pc):
    return 4 * px + 2 * py + pc


class _TwoLevelGather:
    def __init__(self, n_arrays, rows_of, src_of, send_sems, recv_sems):
        x, y, c = _position()
        self.n, self.rows_of, self.src_of = n_arrays, rows_of, src_of
        self.send_sems, self.recv_sems = send_sems, recv_sems
        self.c, self.me, self.sibling = c, (x, y, c), (x, y, 1 - c)
        self.chips = [(1 - x, y), (x, 1 - y), (1 - x, 1 - y)]

    def _copy(self, i, k, block, to, src=None):
        return pltpu.make_async_remote_copy(
            src_ref=self.rows_of(i, *block) if src is None else src, dst_ref=self.rows_of(i, *block),
            send_sem=self.send_sems.at[7 * i + k], recv_sem=self.recv_sems.at[7 * i + k],
            device_id=to, device_id_type=MESH)

    def _first(self, i):
        own = [self._copy(i, 0, self.me, self.sibling, src=self.src_of(i))]
        return own + [self._copy(i, 1 + j, self.me, (*chip, self.c), src=self.src_of(i))
                      for j, chip in enumerate(self.chips)]

    def _passed(self, i, j):
        return self._copy(i, 4 + j, (*self.chips[j], self.c), self.sibling)

    def start(self):
        for i in range(self.n):
            for cp in self._first(i):
                cp.start()

    def forward(self):
        for i in range(self.n):
            for j, chip in enumerate(self.chips):
                self._copy(i, 1 + j, (*chip, self.c), self.me).wait_recv()
                self._passed(i, j).start()

    def drain(self):
        for i in range(self.n):
            self._copy(i, 0, self.sibling, self.me).wait_recv()
            for j, chip in enumerate(self.chips):
                self._copy(i, 4 + j, (*chip, 1 - self.c), self.me).wait_recv()
        for i in range(self.n):
            for cp in self._first(i) + [self._passed(i, j) for j in range(3)]:
                cp.wait_send()


class _CarriedGather:
    def __init__(self, shards, padded_rows, zeros):
        d = shards[0].shape[1]
        self.n = len(shards)
        self.rows = [s.shape[0] for s in shards]
        self.pads = [p - N_DEV * r for r, p in zip(self.rows, padded_rows)]
        assert max(self.pads) <= zeros.shape[0] and zeros.shape[1] == d
        self.arrays = list(shards) + [zeros]
        self.out_shapes = [jax.ShapeDtypeStruct((p, d), s.dtype) for s, p in zip(shards, padded_rows)]
        self.aliases = {}
        self.n_remote, self.n_local = 7 * self.n, 2 * self.n
        self.results = None

    def _rows_of(self, outs):
        def rows_of(i, px, py, pc):
            s = self.rows[i]
            return outs[i].at[pl.ds(pl.multiple_of(_block_of(px, py, pc) * s, SUBLANE_BF16), s), :]
        return rows_of

    def _gather(self, ins, outs, send_sems, recv_sems):
        return _TwoLevelGather(self.n, self._rows_of(outs), lambda i: ins[i], send_sems, recv_sems)

    def _local(self, ins, outs, local_sems):
        x, y, c = _position()
        rows_of = self._rows_of(outs)
        cps = []
        for i in range(self.n):
            cps.append(pltpu.make_async_copy(ins[i], rows_of(i, x, y, c), local_sems.at[2 * i]))
            if self.pads[i]:
                cps.append(pltpu.make_async_copy(ins[self.n].at[pl.ds(0, self.pads[i]), :],
                                                 outs[i].at[pl.ds(N_DEV * self.rows[i], self.pads[i]), :],
                                                 local_sems.at[2 * i + 1]))
        return cps

    def start(self, ins, outs, send_sems, recv_sems, local_sems):
        for cp in self._local(ins, outs, local_sems):
            cp.start()
        self._gather(ins, outs, send_sems, recv_sems).start()

    def forward(self, ins, outs, send_sems, recv_sems, local_sems):
        self._gather(ins, outs, send_sems, recv_sems).forward()

    def finish(self, ins, outs, send_sems, recv_sems, local_sems):
        self._gather(ins, outs, send_sems, recv_sems).drain()
        for cp in self._local(ins, outs, local_sems):
            cp.wait()


class _CarriedSwap:
    def __init__(self, grads, shard_rows):
        d = grads[0].shape[1]
        self.n, self.rows = len(grads), list(shard_rows)
        self.arrays = list(grads)
        self.out_shapes = [jax.ShapeDtypeStruct((4, s, d), g.dtype) for g, s in zip(grads, shard_rows)]
        self.aliases = {}
        self.n_remote, self.n_local = 4 * self.n, 0
        self.results = None

    def _copies(self, ins, outs, send_sems, recv_sems):
        x, y, c = _position()
        cps = []
        for i in range(self.n):
            s = self.rows[i]
            for k in range(4):
                blk = _block_of(k >> 1, k & 1, 1 - c)
                cps.append(pltpu.make_async_remote_copy(
                    src_ref=ins[i].at[pl.ds(pl.multiple_of(blk * s, SUBLANE_BF16), s), :], dst_ref=outs[i].at[k],
                    send_sem=send_sems.at[4 * i + k], recv_sem=recv_sems.at[4 * i + k],
                    device_id=(x, y, 1 - c), device_id_type=MESH))
        return cps

    def start(self, ins, outs, send_sems, recv_sems, local_sems):
        for cp in self._copies(ins, outs, send_sems, recv_sems):
            cp.start()

    def forward(self, *_):
        pass

    def finish(self, ins, outs, send_sems, recv_sems, local_sems):
        for cp in self._copies(ins, outs, send_sems, recv_sems):
            cp.wait()


class _CarriedChipExchange:
    def __init__(self, presums):
        self.n = len(presums)
        self.arrays = list(presums)
        self.out_shapes = [jax.ShapeDtypeStruct(p.shape, p.dtype) for p in presums]
        self.aliases = {}
        self.n_remote, self.n_local = 3 * self.n, 0
        self.results = None

    def _copies(self, ins, outs, send_sems, recv_sems):
        x, y, c = _position()
        cps = []
        for i in range(self.n):
            for r in range(1, 4):
                cps.append(pltpu.make_async_remote_copy(
                    src_ref=ins[i].at[r - 1], dst_ref=outs[i].at[r - 1],
                    send_sem=send_sems.at[3 * i + r - 1], recv_sem=recv_sems.at[3 * i + r - 1],
                    device_id=(x ^ (r >> 1), y ^ (r & 1), c), device_id_type=MESH))
        return cps

    def start(self, ins, outs, send_sems, recv_sems, local_sems):
        for cp in self._copies(ins, outs, send_sems, recv_sems):
            cp.start()

    def forward(self, *_):
        pass

    def finish(self, ins, outs, send_sems, recv_sems, local_sems):
        for cp in self._copies(ins, outs, send_sems, recv_sems):
            cp.wait()


def _gather_small(block, reduce, name):
    rr, nn = block.shape

    def body(x_ref, out_ref, *rest):
        if reduce:
            stack_ref, send_sems, recv_sems, local_sem = rest
        else:
            send_sems, recv_sems, local_sem = rest
            stack_ref = out_ref
        x, y, c = _position()

        def rows_of(i, px, py, pc):
            return stack_ref.at[pl.ds(pl.multiple_of(_block_of(px, py, pc) * rr, 8), rr), :]

        own = pltpu.make_async_copy(x_ref, rows_of(0, x, y, c), local_sem)
        own.start()
        gather = _TwoLevelGather(1, rows_of, lambda i: x_ref, send_sems, recv_sems)
        gather.start()
        gather.forward()
        gather.drain()
        own.wait()
        if reduce:
            acc = stack_ref[0:rr, :]
            for k in range(1, N_DEV):
                acc = acc + stack_ref[k * rr:(k + 1) * rr, :]
            out_ref[...] = acc

    vmem = pl.BlockSpec(memory_space=pltpu.VMEM)
    scratch = [pltpu.SemaphoreType.DMA((7,)), pltpu.SemaphoreType.DMA((7,)), pltpu.SemaphoreType.DMA]
    if reduce:
        scratch = [pltpu.VMEM((N_DEV * rr, nn), F32)] + scratch
    out_rows = rr if reduce else N_DEV * rr
    return pl.pallas_call(
        body, in_specs=[vmem], out_specs=vmem, out_shape=jax.ShapeDtypeStruct((out_rows, nn), F32),
        scratch_shapes=scratch, name=name, compiler_params=_params())(block)


def _presum(where, grad, swapped, name):
    s, d = swapped.shape[1], swapped.shape[2]
    tc = _tile(d, 512, LANE)

    def body(where_ref, g_ref, sw_ref, o_ref):
        o_ref[0] = (g_ref[...].astype(F32) + sw_ref[0].astype(F32)).astype(o_ref.dtype)

    return _call(
        body, grid=(3, d // tc),
        in_specs=[pl.BlockSpec((s, tc), lambda r, j, where: (where[1 + r], j)),
                  pl.BlockSpec((1, s, tc), lambda r, j, where: (where[5 + r], 0, j))],
        out_specs=pl.BlockSpec((1, s, tc), lambda r, j, where: (r, 0, j)),
        out_shape=jax.ShapeDtypeStruct((3, s, d), WIRE_DTYPE), name=name, args=(grad, swapped), prefetch=(where,))


def _final_sum(where, grad, swapped, received, name, carried=()):
    s, d = swapped.shape[1], swapped.shape[2]
    tc = _tile(d, 512, LANE)

    def body(where_ref, g_ref, sw_ref, r_ref, o_ref):
        acc = g_ref[...].astype(F32) + sw_ref[0].astype(F32)
        for k in range(3):
            acc = acc + r_ref[k].astype(F32)
        o_ref[...] = acc

    return _call(
        body, grid=(d // tc,),
        in_specs=[pl.BlockSpec((s, tc), lambda j, where: (where[0], j)),
                  pl.BlockSpec((1, s, tc), lambda j, where: (where[4], 0, j)),
                  pl.BlockSpec((3, s, tc), lambda j, where: (0, 0, j))],
        out_specs=pl.BlockSpec((s, tc), lambda j, where: (0, j)),
        out_shape=jax.ShapeDtypeStruct((s, d), F32), name=name, args=(grad, swapped, received),
        prefetch=(where,), carried=carried)


class _GradReduction:
    def __init__(self, key, grad, shard_rows, where):
        self.key, self.grad, self.rows, self.where = key, grad, shard_rows, where

    def swap(self):
        self._swap = _CarriedSwap([self.grad], [self.rows])
        return self._swap

    def exchange(self):
        presum = _presum(self.where, self.grad, self._swap.results[0], "presum_" + self.key)
        self._exchange = _CarriedChipExchange([presum])
        return self._exchange

    def total(self, carried=()):
        return _final_sum(self.where, self.grad, self._swap.results[0], self._exchange.results[0],
                          "sum_" + self.key, carried)


def _adamw(w, g, m, v, name):
    rows, cols = w.shape
    tr = _tile(rows, 256, 8)

    def body(w_ref, g_ref, m_ref, v_ref, d_ref, nm_ref, nv_ref):
        g = g_ref[...]
        nm = ADAM_B1 * m_ref[...] + (1.0 - ADAM_B1) * g
        nv = ADAM_B2 * v_ref[...] + (1.0 - ADAM_B2) * (g * g)
        m_hat = nm / (1.0 - ADAM_B1 ** ADAM_STEP)
        v_hat = nv / (1.0 - ADAM_B2 ** ADAM_STEP)
        d_ref[...] = -ADAM_LR * (m_hat / (jnp.sqrt(v_hat) + ADAM_EPS) + ADAM_WD * w_ref[...])
        nm_ref[...] = nm
        nv_ref[...] = nv

    spec = pl.BlockSpec((tr, cols), lambda i: (i, 0))
    return pl.pallas_call(
        body, grid=(rows // tr,), in_specs=[spec] * 4, out_specs=[spec] * 3,
        out_shape=[jax.ShapeDtypeStruct((rows, cols), F32)] * 3, name=name, compiler_params=_params())(w, g, m, v)


def _pack_rows(arrays, width, row_quantum=8):
    flat = jnp.concatenate([a.reshape(-1) for a in arrays])
    total = _round_up(flat.shape[0], row_quantum * width)
    flat = jnp.pad(flat, (0, total - flat.shape[0]))
    return flat.reshape(-1, width)


def _unpack_rows(packed, shapes):
    flat = packed.reshape(-1)
    out = []
    off = 0
    for shp in shapes:
        size = 1
        for s in shp:
            size *= s
        out.append(flat[off:off + size].reshape(shp))
        off += size
    return out


def _block_diag(w):
    h, hb, _ = w.shape
    per = BD // hb
    w4 = w.reshape(h // per, per, hb, hb)
    eye = jnp.eye(per, dtype=w.dtype)
    return jnp.einsum('npij,pq->npiqj', w4, eye).reshape(h // per, BD, BD)


def _block_diag_extract(bd, hb):
    nbk = bd.shape[0]
    per = BD // hb
    b5 = bd.reshape(nbk, per, hb, per, hb)
    eye = jnp.eye(per, dtype=bd.dtype)
    return jnp.einsum('npiqj,pq->npij', b5, eye).reshape(nbk * per, hb, hb)


def kernel(x, meta_tokens, ffn1_pre_g, ffn1_w_gate, ffn1_w_up, ffn1_w_down, ffn1_post_g, mix_pre_g, w_in, lru_conv_w, lru_conv_b, lru_w_a, lru_b_a, lru_w_x, lru_b_x, lru_lambda, sconv_w, lru_out_g, sconv_out_g, w_out, mix_post_g, ffn2_pre_g, ffn2_w_gate, ffn2_w_up, ffn2_w_down, ffn2_post_g, loss_target, m_meta_tokens, m_ffn1_pre_g, m_ffn1_w_gate, m_ffn1_w_up, m_ffn1_w_down, m_ffn1_post_g, m_mix_pre_g, m_w_in, m_lru_conv_w, m_lru_conv_b, m_lru_w_a, m_lru_b_a, m_lru_w_x, m_lru_b_x, m_lru_lambda, m_sconv_w, m_lru_out_g, m_sconv_out_g, m_w_out, m_mix_post_g, m_ffn2_pre_g, m_ffn2_w_gate, m_ffn2_w_up, m_ffn2_w_down, m_ffn2_post_g, v_meta_tokens, v_ffn1_pre_g, v_ffn1_w_gate, v_ffn1_w_up, v_ffn1_w_down, v_ffn1_post_g, v_mix_pre_g, v_w_in, v_lru_conv_w, v_lru_conv_b, v_lru_w_a, v_lru_b_a, v_lru_w_x, v_lru_b_x, v_lru_lambda, v_sconv_w, v_lru_out_g, v_sconv_out_g, v_w_out, v_mix_post_g, v_ffn2_pre_g, v_ffn2_w_gate, v_ffn2_w_up, v_ffn2_w_down, v_ffn2_post_g):
    given = dict(locals())
    wts = {n: given[n] for n in WEIGHT_NAMES}
    mom = {n: given["m_" + n] for n in WEIGHT_NAMES}
    var = {n: given["v_" + n] for n in WEIGHT_NAMES}

    xi, yi, ci = _position()
    me = _block_of(xi, yi, ci)
    x2 = x[0]
    seq, d = x2.shape
    n_meta = meta_tokens.shape[0]
    m_rows = _round_up(n_meta + seq, ROW_ALIGN)
    pad = m_rows - n_meta - seq
    lead = pad + n_meta
    c = lru_conv_b.shape[1]
    hb = lru_w_a.shape[-1]
    dm = meta_tokens.shape[1]
    cs_ = lru_conv_w.shape[2]
    kw4, kw3 = lru_conv_w.shape[1], sconv_w.shape[1]
    assert d == 2 * c and c % BD == 0 and BD % hb == 0 and cs_ <= dm and kw4 == 4 and kw3 == 3

    small = jnp.zeros((_round_up(n_meta + kw4 + kw3, 8), dm), F32)
    small = small.at[0:n_meta].set(meta_tokens)
    small = small.at[n_meta:n_meta + kw4, 0:cs_].set(lru_conv_w[0])
    small = small.at[n_meta + kw4:n_meta + kw4 + kw3, 0:cs_].set(sconv_w[0])
    sr = small.shape[0]
    small_all = _gather_small(small, False, "gather_small").reshape(N_DEV, sr, dm)
    meta_full = small_all[:, 0:n_meta, :].transpose(1, 0, 2).reshape(n_meta, d)
    conv_w_full = small_all[:, n_meta:n_meta + kw4, 0:cs_].transpose(1, 0, 2).reshape(kw4, c)
    sconv_w_full = small_all[:, n_meta + kw4:n_meta + kw4 + kw3, 0:cs_].transpose(1, 0, 2).reshape(kw3, c)

    big = ['ffn1_w_gate', 'ffn1_w_up', 'ffn1_w_down', 'w_in', 'w_out', 'ffn2_w_gate', 'ffn2_w_up', 'ffn2_w_down']
    col_sharded = {'ffn1_w_gate', 'ffn1_w_up', 'w_in', 'ffn2_w_gate', 'ffn2_w_up'}
    shards = []
    for nme in big:
        w = wts[nme][0].astype(WIRE_DTYPE)
        shards.append(w.T if nme in col_sharded else w)
    shard_rows = dict(zip(big, [s.shape[0] for s in shards]))
    zeros = jnp.zeros((F_ALIGN, d), WIRE_DTYPE)

    def gather(*names):
        sel = [shards[big.index(nme)] for nme in names]
        padded = [_round_up(N_DEV * shard_rows[nme], LANE if nme in ('w_in', 'w_out') else F_ALIGN) for nme in names]
        return _CarriedGather(sel, padded, zeros)

    pv = jnp.zeros((16, c), F32)
    pv = pv.at[0:4].set(conv_w_full).at[4].set(lru_conv_b[0]).at[5].set(lru_b_a[0]).at[6].set(lru_b_x[0])
    pv = pv.at[7].set(lru_lambda[0]).at[8:11].set(sconv_w_full).at[11].set(lru_out_g[0]).at[12].set(sconv_out_g[0])
    wa_bd = _block_diag(lru_w_a[0]).astype(MXU_DTYPE)
    wx_bd = _block_diag(lru_w_x[0]).astype(MXU_DTYPE)
    gs = c // N_GROUPS
    gidx = jnp.arange(BD) // gs
    gm = jnp.where(gidx[:, None] == gidx[None, :], 1.0 / gs, 0.0).astype(MXU_DTYPE)

    h0 = jnp.concatenate([jnp.zeros((pad, d), F32), meta_full, x2], axis=0)
    ride = gather('ffn1_w_gate', 'ffn1_w_up')
    n1 = _rmsnorm(h0, ffn1_pre_g, "ffn1_prenorm", carried=[ride])
    wg1, wu1 = ride.results
    ride = gather('ffn1_w_down')
    g1, u1, a1 = _ffn_gate_up(n1, wg1, wu1, "ffn1_gate_up", carried=[ride])
    (wd1,) = ride.results
    ride = gather('w_in', 'w_out')
    fo1, h1 = _mm_residual_norm(a1, wd1, h0, ffn1_post_g, 0.5, "ffn1_down", carried=[ride])
    win_t, wout = ride.results
    un = _rmsnorm(h1, mix_pre_g, "mix_prenorm")
    ride = gather('ffn2_w_gate')
    z = _mm_nt(un, win_t, "mix_in_proj", carried=[ride])
    (wg2,) = ride.results
    ride = gather('ffn2_w_up')
    mixed, hs = _mixer_fwd(z, pv, wa_bd, wx_bd, gm, pad, "mixer_fwd", carried=[ride])
    (wu2,) = ride.results
    o_mix, h2 = _mm_residual_norm(mixed, wout, h1, mix_post_g, 1.0, "mix_out_proj")
    n2 = _rmsnorm(h2, ffn2_pre_g, "ffn2_prenorm")
    ride = gather('ffn2_w_down')
    g2, u2, a2 = _ffn_gate_up(n2, wg2, wu2, "ffn2_gate_up", carried=[ride])
    (wd2,) = ride.results
    fo2, h3 = _mm_residual_norm(a2, wd2, h2, ffn2_post_g, 0.5, "ffn2_down")
    dh3, loss_part = _loss_grad(h3, loss_target[0], lead, "loss_grad")
    loss = lax.psum(loss_part[0, 0], ("x", "y", "c"))

    chip_rel = [2 * (xi ^ (r >> 1)) + (yi ^ (r & 1)) for r in range(4)]
    where = jnp.stack([2 * k + ci for k in chip_rel] + chip_rel).astype(jnp.int32)
    red = {}

    def reduction(nme, grad):
        red[nme] = _GradReduction(nme, grad, shard_rows[nme], where)
        return red[nme]

    dfo2, d_post2 = _norm_bwd(fo2, ffn2_post_g, dh3, 0.5, "ffn2_postnorm_bwd")
    dg2, du2 = _ffn_hidden_bwd(dfo2, wd2, g2, u2, "ffn2_hidden_bwd")
    r_wd2 = reduction('ffn2_w_down', _mm_tn(a2, dfo2, "ffn2_dw_down"))
    r_wg2 = reduction('ffn2_w_gate', _mm_tn(dg2, n2, "ffn2_dw_gate", carried=[r_wd2.swap()]))
    r_wu2 = reduction('ffn2_w_up', _mm_tn(du2, n2, "ffn2_dw_up", carried=[r_wd2.exchange(), r_wg2.swap()]))
    dh2, d_pre2 = _mm_norm_bwd([(dg2, wg2), (du2, wu2)], h2, ffn2_pre_g, dh3, "ffn2_dx",
                               carried=[r_wg2.exchange(), r_wu2.swap()])
    do_mix, d_mix_post = _norm_bwd(o_mix, mix_post_g, dh2, 1.0, "mix_postnorm_bwd")
    dmixed = _mm_nt(do_mix, wout, "mix_out_proj_bwd")
    r_wout = reduction('w_out', _mm_tn(mixed, do_mix, "mix_dw_out"))
    dz, dpv, dwa_bd, dwx_bd = _mixer_bwd(z, hs, dmixed, pv, wa_bd, wx_bd, gm, pad, "mixer_bwd",
                                         carried=[r_wu2.exchange(), r_wout.swap()])
    dh1, d_mix_pre = _mm_norm_bwd([(dz, win_t)], h1, mix_pre_g, dh2, "mix_dx", carried=[r_wout.exchange()])
    r_win = reduction('w_in', _mm_tn(dz, un, "mix_dw_in"))
    dfo1, d_post1 = _norm_bwd(fo1, ffn1_post_g, dh1, 0.5, "ffn1_postnorm_bwd")
    dg1, du1 = _ffn_hidden_bwd(dfo1, wd1, g1, u1, "ffn1_hidden_bwd", carried=[r_win.swap()])
    r_wd1 = reduction('ffn1_w_down', _mm_tn(a1, dfo1, "ffn1_dw_down", carried=[r_win.exchange()]))
    r_wg1 = reduction('ffn1_w_gate', _mm_tn(dg1, n1, "ffn1_dw_gate", carried=[r_wd1.swap()]))
    r_wu1 = reduction('ffn1_w_up', _mm_tn(du1, n1, "ffn1_dw_up", carried=[r_wd1.exchange(), r_wg1.swap()]))
    dh0, d_pre1 = _mm_norm_bwd([(dg1, wg1), (du1, wu1)], h0, ffn1_pre_g, dh1, "ffn1_dx",
                               carried=[r_wg1.exchange(), r_wu1.swap()])
    grad_x = dh0[lead:][None]

    grads = {}
    last_exchange = [r_wu1.exchange()]
    for nme in ['ffn2_w_down', 'ffn2_w_gate', 'ffn2_w_up', 'w_out', 'w_in', 'ffn1_w_down', 'ffn1_w_gate', 'ffn1_w_up']:
        gsum = red[nme].total(carried=last_exchange)
        last_exchange = []
        grads[nme] = (gsum.T if nme in col_sharded else gsum)[None]

    small_names = ['ffn1_pre_g', 'ffn1_post_g', 'mix_pre_g', 'mix_post_g', 'ffn2_pre_g', 'ffn2_post_g',
                   'lru_conv_b', 'lru_b_a', 'lru_b_x', 'lru_lambda', 'lru_out_g', 'sconv_out_g',
                   'lru_conv_w', 'sconv_w', 'lru_w_a', 'lru_w_x', 'meta_tokens']
    small_parts = [d_pre1, d_post1, d_mix_pre, d_mix_post, d_pre2, d_post2,
                   dpv[4:5], dpv[5:6], dpv[6:7], dpv[7:8], dpv[11:12], dpv[12:13],
                   dpv[0:4], dpv[8:11], _block_diag_extract(dwa_bd, hb), _block_diag_extract(dwx_bd, hb),
                   dh0[pad:lead]]
    small_shapes = [p.shape for p in small_parts]
    small_sum = _gather_small(_pack_rows(small_parts, d), True, "reduce_small")
    for nme, gsm in zip(small_names, _unpack_rows(small_sum, small_shapes)):
        if nme == 'meta_tokens':
            grads[nme] = lax.dynamic_slice_in_dim(gsm, me * dm, dm, axis=1)
        elif nme in ('lru_conv_w', 'sconv_w'):
            grads[nme] = lax.dynamic_slice_in_dim(gsm, me * cs_, cs_, axis=1)[None]
        else:
            grads[nme] = gsm.reshape(wts[nme].shape)

    delta, new_m, new_v = {}, {}, {}
    for nme in big:
        shp = wts[nme].shape
        dl, nm_, nv_ = _adamw(wts[nme][0], grads[nme][0], mom[nme][0], var[nme][0], "adamw_" + nme)
        delta[nme], new_m[nme], new_v[nme] = dl.reshape(shp), nm_.reshape(shp), nv_.reshape(shp)
    rest = [n for n in WEIGHT_NAMES if n not in big]
    rest_shapes = [wts[n].shape for n in rest]
    packed = [_pack_rows([src[n] for n in rest], LANE, 256) for src in (wts, grads, mom, var)]
    for out, packed_out in zip((delta, new_m, new_v), _adamw(*packed, "adamw_small")):
        for nme, arr in zip(rest, _unpack_rows(packed_out, rest_shapes)):
            out[nme] = arr

    return (loss, grad_x, *[grads[n] for n in WEIGHT_NAMES], *[delta[n] for n in WEIGHT_NAMES],
            *[new_m[n] for n in WEIGHT_NAMES], *[new_v[n] for n in WEIGHT_NAMES])
```

```python
import functools

import jax
import jax.numpy as jnp
from jax import lax
from jax.experimental import pallas as pl
from jax.experimental.pallas import tpu as pltpu

F32 = jnp.float32
MXU_DTYPE = jnp.bfloat16
WIRE_DTYPE = jnp.bfloat16
MESH = pl.DeviceIdType.MESH

EPS = 1e-6
LRU_C = 8.0
N_GROUPS = 16
ADAM_LR = 0.001
ADAM_B1 = 0.9
ADAM_B2 = 0.999
ADAM_EPS = 1e-08
ADAM_WD = 0.01
ADAM_STEP = 10

N_DEV = 8
LANE = 128
SUBLANE_BF16 = 16
ROW_ALIGN = 128
F_ALIGN = 512
BD = 256
K_TILE = 1408
MIX_ROWS = 128
VMEM_LIMIT_MB = 56

WEIGHT_NAMES = ['meta_tokens', 'ffn1_pre_g', 'ffn1_w_gate', 'ffn1_w_up', 'ffn1_w_down', 'ffn1_post_g',
                'mix_pre_g', 'w_in', 'lru_conv_w', 'lru_conv_b', 'lru_w_a', 'lru_b_a', 'lru_w_x', 'lru_b_x',
                'lru_lambda', 'sconv_w', 'lru_out_g', 'sconv_out_g', 'w_out', 'mix_post_g', 'ffn2_pre_g',
                'ffn2_w_gate', 'ffn2_w_up', 'ffn2_w_down', 'ffn2_post_g']


def _round_up(n, q):
    return (n + q - 1) // q * q


def _tile(n, target, q):
    best = None
    t = q
    while t <= min(n, target):
        if n % t == 0:
            best = t
        t += q
    assert best is not None, (n, target, q)
    return best


def _params(**kw):
    return pltpu.CompilerParams(vmem_limit_bytes=VMEM_LIMIT_MB << 20, **kw)


def _call(body, *, grid, in_specs, out_specs, out_shape, name, args, scratch_shapes=(), carried=(), prefetch=()):
    carried = list(carried)
    n_pf = len(prefetch)

    def launch(fn, in_specs_, out_specs_, out_shape_, scratch_, operands, aliases_):
        if n_pf:
            spec = pltpu.PrefetchScalarGridSpec(num_scalar_prefetch=n_pf, grid=grid, in_specs=in_specs_,
                                                out_specs=out_specs_, scratch_shapes=scratch_)
            return pl.pallas_call(fn, grid_spec=spec, out_shape=out_shape_, input_output_aliases=aliases_,
                                  name=name, compiler_params=_params())(*prefetch, *operands)
        return pl.pallas_call(fn, grid=grid, in_specs=in_specs_, out_specs=out_specs_, out_shape=out_shape_,
                              scratch_shapes=scratch_, input_output_aliases=aliases_, name=name,
                              compiler_params=_params())(*operands)

    if not carried:
        return launch(body, in_specs, out_specs, out_shape, list(scratch_shapes), args, {})
    single = not isinstance(out_shape, (list, tuple))
    out_specs_l = [out_specs] if single else list(out_specs)
    out_shape_l = [out_shape] if single else list(out_shape)
    n_in, n_out, n_scr = len(in_specs), len(out_specs_l), len(scratch_shapes)
    hbm = pl.BlockSpec(memory_space=pl.ANY)
    c_in = [a for cm in carried for a in cm.arrays]
    c_out = [s for cm in carried for s in cm.out_shapes]
    c_scr = []
    aliases = {}
    in_off, out_off = n_pf + n_in, n_out
    for cm in carried:
        c_scr += [pltpu.SemaphoreType.DMA((cm.n_remote,)), pltpu.SemaphoreType.DMA((cm.n_remote,)),
                  pltpu.SemaphoreType.DMA((max(cm.n_local, 1),))]
        for k, v in cm.aliases.items():
            aliases[in_off + k] = out_off + v
        in_off += len(cm.arrays)
        out_off += len(cm.out_shapes)
    steps = 1
    for g in grid:
        steps *= g
    forward_step = min((3 * steps) // 4, steps - 1)

    def wrapped(*refs):
        pf = refs[:n_pf]
        p = n_pf
        ins = refs[p:p + n_in]
        p += n_in
        cins = refs[p:p + len(c_in)]
        p += len(c_in)
        outs = refs[p:p + n_out]
        p += n_out
        couts = refs[p:p + len(c_out)]
        p += len(c_out)
        scr = refs[p:p + n_scr]
        csem = refs[p + n_scr:]
        lin = 0
        for axis, g in enumerate(grid):
            lin = lin * g + pl.program_id(axis)
        views = []
        io = oo = 0
        for j, cm in enumerate(carried):
            views.append((cins[io:io + len(cm.arrays)], couts[oo:oo + len(cm.out_shapes)],
                          csem[3 * j], csem[3 * j + 1], csem[3 * j + 2]))
            io += len(cm.arrays)
            oo += len(cm.out_shapes)

        @pl.when(lin == 0)
        def _():
            for cm, v in zip(carried, views):
                cm.start(*v)

        body(*pf, *ins, *outs, *scr)

        @pl.when(lin == forward_step)
        def _():
            for cm, v in zip(carried, views):
                cm.forward(*v)

        @pl.when(lin == steps - 1)
        def _():
            for cm, v in zip(carried, views):
                cm.finish(*v)

    res = launch(wrapped, list(in_specs) + [hbm] * len(c_in), out_specs_l + [hbm] * len(c_out),
                 out_shape_l + c_out, list(scratch_shapes) + c_scr, (*args, *c_in), aliases)
    oo = n_out
    for cm in carried:
        cm.results = list(res[oo:oo + len(cm.out_shapes)])
        oo += len(cm.out_shapes)
    return res[0] if single else list(res[:n_out])


def _rmsnorm(h, g, name, carried=()):
    m, d = h.shape
    tm = _tile(m, 528, SUBLANE_BF16)

    def body(h_ref, g_ref, o_ref):
        x = h_ref[...]
        r = lax.rsqrt(jnp.mean(x * x, axis=-1, keepdims=True) + EPS)
        o_ref[...] = (x * r * g_ref[...]).astype(o_ref.dtype)

    return _call(
        body, grid=(m // tm,),
        in_specs=[pl.BlockSpec((tm, d), lambda i: (i, 0)), pl.BlockSpec((1, d), lambda i: (0, 0))],
        out_specs=pl.BlockSpec((tm, d), lambda i: (i, 0)),
        out_shape=jax.ShapeDtypeStruct((m, d), MXU_DTYPE), name=name, args=(h, g), carried=carried)


def _rmsnorm_bwd_rows(x, g, dy):
    r = lax.rsqrt(jnp.mean(x * x, axis=-1, keepdims=True) + EPS)
    xh = x * r
    dyh = dy * g
    dx = r * (dyh - xh * jnp.mean(dyh * xh, axis=-1, keepdims=True))
    return dx, dy * xh


def _norm_bwd(x, g, dy, scale, name, carried=()):
    m, d = x.shape
    tm = _tile(m, 528, SUBLANE_BF16)

    def body(x_ref, g_ref, dy_ref, dx_ref, dg_ref):
        @pl.when(pl.program_id(0) == 0)
        def _():
            dg_ref[...] = jnp.zeros_like(dg_ref)

        dx, dgr = _rmsnorm_bwd_rows(x_ref[...], g_ref[...], scale * dy_ref[...])
        dx_ref[...] = dx.astype(dx_ref.dtype)
        dg_ref[...] += jnp.sum(dgr, axis=0, keepdims=True)

    return _call(
        body, grid=(m // tm,),
        in_specs=[pl.BlockSpec((tm, d), lambda i: (i, 0)), pl.BlockSpec((1, d), lambda i: (0, 0)),
                  pl.BlockSpec((tm, d), lambda i: (i, 0))],
        out_specs=[pl.BlockSpec((tm, d), lambda i: (i, 0)), pl.BlockSpec((1, d), lambda i: (0, 0))],
        out_shape=[jax.ShapeDtypeStruct((m, d), MXU_DTYPE), jax.ShapeDtypeStruct((1, d), F32)],
        name=name, args=(x, g, dy), carried=carried)


def _loss_grad(h, target, lead, name):
    m, d = h.shape
    tl = ROW_ALIGN
    assert lead % tl == 0 and target.shape[0] == m - lead
    lead_blocks = lead // tl

    def body(h_ref, t_ref, dh_ref, l_ref):
        i = pl.program_id(0)

        @pl.when(i == 0)
        def _():
            l_ref[...] = jnp.zeros_like(l_ref)

        @pl.when(i < lead_blocks)
        def _():
            dh_ref[...] = jnp.zeros_like(dh_ref)

        @pl.when(i >= lead_blocks)
        def _():
            e = h_ref[...] - t_ref[...]
            dh_ref[...] = e * (1.0 / d)
            row = jnp.sum(e * e, axis=-1, keepdims=True) * (1.0 / d)
            l_ref[...] += 0.5 * jnp.sum(row, axis=0, keepdims=True)

    return pl.pallas_call(
        body, grid=(m // tl,),
        in_specs=[pl.BlockSpec((tl, d), lambda i: (i, 0)),
                  pl.BlockSpec((tl, d), lambda i: (jnp.maximum(i - lead_blocks, 0), 0))],
        out_specs=[pl.BlockSpec((tl, d), lambda i: (i, 0)), pl.BlockSpec((1, 1), lambda i: (0, 0))],
        out_shape=[jax.ShapeDtypeStruct((m, d), F32), jax.ShapeDtypeStruct((1, 1), F32)],
        name=name, compiler_params=_params())(h, target)


def _dot_nt(a, b):
    return lax.dot_general(a, b, (((1,), (1,)), ((), ())), preferred_element_type=F32)


def _dot_tn(a, b):
    return lax.dot_general(a, b, (((0,), (0,)), ((), ())), preferred_element_type=F32)


def _mm_nt(a, w, name, carried=()):
    m, k = a.shape
    n = w.shape[0]
    tm = _tile(m, 1056, SUBLANE_BF16)
    tn = _tile(n, 512, LANE)

    def body(a_ref, w_ref, o_ref):
        o_ref[...] = _dot_nt(a_ref[...], w_ref[...])

    return _call(
        body, grid=(m // tm, n // tn),
        in_specs=[pl.BlockSpec((tm, k), lambda i, j: (i, 0)), pl.BlockSpec((tn, k), lambda i, j: (j, 0))],
        out_specs=pl.BlockSpec((tm, tn), lambda i, j: (i, j)),
        out_shape=jax.ShapeDtypeStruct((m, n), F32), name=name, args=(a, w), carried=carried)


def _ffn_gate_up(n_act, wg_t, wu_t, name, carried=()):
    m, d = n_act.shape
    fp = wg_t.shape[0]
    tm = _tile(m, 1056, SUBLANE_BF16)
    tn = _tile(fp, 512, LANE)

    def body(n_ref, wg_ref, wu_ref, g_ref, u_ref, a_ref):
        n = n_ref[...]
        g = _dot_nt(n, wg_ref[...])
        u = _dot_nt(n, wu_ref[...])
        g_ref[...] = g.astype(g_ref.dtype)
        u_ref[...] = u.astype(u_ref.dtype)
        a_ref[...] = (g * jax.nn.sigmoid(g) * u).astype(a_ref.dtype)

    act = pl.BlockSpec((tm, tn), lambda i, j: (i, j))
    wsp = pl.BlockSpec((tn, d), lambda i, j: (j, 0))
    return _call(
        body, grid=(m // tm, fp // tn),
        in_specs=[pl.BlockSpec((tm, d), lambda i, j: (i, 0)), wsp, wsp],
        out_specs=[act, act, act],
        out_shape=[jax.ShapeDtypeStruct((m, fp), MXU_DTYPE)] * 3, name=name, args=(n_act, wg_t, wu_t), carried=carried)


def _ffn_hidden_bwd(dfo, wd, g_act, u_act, name, carried=()):
    m, d = dfo.shape
    fp = wd.shape[0]
    tm = _tile(m, 1056, SUBLANE_BF16)
    tn = _tile(fp, 512, LANE)

    def body(df_ref, wd_ref, g_ref, u_ref, dg_ref, du_ref):
        da = _dot_nt(df_ref[...], wd_ref[...])
        g = g_ref[...].astype(F32)
        u = u_ref[...].astype(F32)
        s = jax.nn.sigmoid(g)
        du_ref[...] = (da * (g * s)).astype(du_ref.dtype)
        dg_ref[...] = (da * u * (s * (1.0 + g * (1.0 - s)))).astype(dg_ref.dtype)

    act = pl.BlockSpec((tm, tn), lambda i, j: (i, j))
    return _call(
        body, grid=(m // tm, fp // tn),
        in_specs=[pl.BlockSpec((tm, d), lambda i, j: (i, 0)), pl.BlockSpec((tn, d), lambda i, j: (j, 0)), act, act],
        out_specs=[act, act],
        out_shape=[jax.ShapeDtypeStruct((m, fp), MXU_DTYPE)] * 2, name=name, args=(dfo, wd, g_act, u_act),
        carried=carried)


def _mm_residual_norm(a, w, h, g, scale, name, carried=()):
    m, k = a.shape
    d = w.shape[1]
    tm = _tile(m, 528, SUBLANE_BF16)
    tk = _tile(k, K_TILE, LANE)
    nk = k // tk

    def body(a_ref, w_ref, h_ref, g_ref, fo_ref, hn_ref, acc_ref):
        kk = pl.program_id(1)

        @pl.when(kk == 0)
        def _():
            acc_ref[...] = jnp.zeros_like(acc_ref)

        acc_ref[...] += jnp.dot(a_ref[...], w_ref[...], preferred_element_type=F32)

        @pl.when(kk == nk - 1)
        def _():
            fo = acc_ref[...]
            fo_ref[...] = fo
            r = lax.rsqrt(jnp.mean(fo * fo, axis=-1, keepdims=True) + EPS)
            hn_ref[...] = h_ref[...] + scale * (fo * r * g_ref[...])

    row = pl.BlockSpec((tm, d), lambda i, kk: (i, 0))
    row_once = pl.BlockSpec((tm, d), lambda i, kk: (i, 0), pipeline_mode=pl.Buffered(1))
    return _call(
        body, grid=(m // tm, nk),
        in_specs=[pl.BlockSpec((tm, tk), lambda i, kk: (i, kk)), pl.BlockSpec((tk, d), lambda i, kk: (kk, 0)),
                  row_once, pl.BlockSpec((1, d), lambda i, kk: (0, 0))],
        out_specs=[row, row],
        out_shape=[jax.ShapeDtypeStruct((m, d), F32)] * 2,
        scratch_shapes=[pltpu.VMEM((tm, d), F32)], name=name, args=(a, w, h, g), carried=carried)


def _norm_bwd_row_tile(m):
    return _tile(m, 384, SUBLANE_BF16)


def _mm_norm_bwd(pairs, h, g, dh_up, name, carried=(), row_tiles=None, dg_init=None):
    n_pairs = len(pairs)
    m, k = pairs[0][0].shape
    d = h.shape[1]
    tm = _norm_bwd_row_tile(m)
    tk = _tile(k, K_TILE, LANE)
    nk = k // tk
    t0, nt = row_tiles if row_tiles is not None else (0, m // tm)
    if dg_init is None:
        dg_init = jnp.zeros((1, d), F32)

    def body(*refs):
        ops = refs[:2 * n_pairs]
        h_ref, g_ref, up_ref, init_ref, dh_ref, dg_ref, acc_ref = refs[2 * n_pairs:]
        i = pl.program_id(0)
        kk = pl.program_id(1)

        @pl.when(jnp.logical_and(i == 0, kk == 0))
        def _():
            dg_ref[...] = init_ref[...]

        @pl.when(kk == 0)
        def _():
            acc_ref[...] = jnp.zeros_like(acc_ref)

        for p in range(n_pairs):
            acc_ref[...] += jnp.dot(ops[2 * p][...], ops[2 * p + 1][...], preferred_element_type=F32)

        @pl.when(kk == nk - 1)
        def _():
            dx, dgr = _rmsnorm_bwd_rows(h_ref[...], g_ref[...], acc_ref[...])
            dh_ref[...] = up_ref[...] + dx
            dg_ref[...] += jnp.sum(dgr, axis=0, keepdims=True)

    row_in = pl.BlockSpec((tm, d), lambda i, kk: (t0 + i, 0), pipeline_mode=pl.Buffered(1))
    vec = pl.BlockSpec((1, d), lambda i, kk: (0, 0))
    in_specs = []
    args = []
    for a, w in pairs:
        in_specs += [pl.BlockSpec((tm, tk), lambda i, kk: (t0 + i, kk)), pl.BlockSpec((tk, d), lambda i, kk: (kk, 0))]
        args += [a, w]
    return _call(
        body, grid=(nt, nk),
        in_specs=in_specs + [row_in, vec, row_in, vec], out_specs=[pl.BlockSpec((tm, d), lambda i, kk: (i, 0)), vec],
        out_shape=[jax.ShapeDtypeStruct((nt * tm, d), F32), jax.ShapeDtypeStruct((1, d), F32)],
        scratch_shapes=[pltpu.VMEM((tm, d), F32)], name=name, args=(*args, h, g, dh_up, dg_init), carried=carried)


def _mm_tn(a, b, name, carried=()):
    m, ka = a.shape
    d = b.shape[1]
    tf = _tile(ka, 512, LANE)

    def body(a_ref, b_ref, o_ref):
        o_ref[...] = _dot_tn(a_ref[...], b_ref[...]).astype(o_ref.dtype)

    return _call(
        body, grid=(ka // tf,),
        in_specs=[pl.BlockSpec((m, tf), lambda j: (0, j)),
                  pl.BlockSpec((m, d), lambda j: (0, 0), pipeline_mode=pl.Buffered(1))],
        out_specs=pl.BlockSpec((tf, d), lambda j: (j, 0)),
        out_shape=jax.ShapeDtypeStruct((ka, d), WIRE_DTYPE), name=name, args=(a, b), carried=carried)


GELU_K = 0.7978845608028654
GELU_C = 0.044715


def _expm1(x):
    series = x * (1.0 + x * (1.0 / 2 + x * (1.0 / 6 + x * (1.0 / 24 + x * (1.0 / 120 + x * (1.0 / 720 + x * (1.0 / 5040)))))))
    return jnp.where(jnp.abs(x) < 0.3, series, jnp.exp(x) - 1.0)


def _softplus(x):
    return jnp.maximum(x, 0.0) + jnp.log1p(jnp.exp(-jnp.abs(x)))


def _block_mm(v, w_ref, transposed):
    nbk = w_ref.shape[0]
    outs = []
    for j in range(nbk):
        vj = v[:, j * BD:(j + 1) * BD]
        outs.append(_dot_nt(vj, w_ref[j]) if transposed else jnp.dot(vj, w_ref[j], preferred_element_type=F32))
    return outs[0] if nbk == 1 else jnp.concatenate(outs, axis=1)


def _group_mean(q, gm_ref):
    hi = q.astype(MXU_DTYPE)
    lo = (q - hi.astype(F32)).astype(MXU_DTYPE)
    nbk = q.shape[1] // BD
    gm = gm_ref[...]
    outs = []
    for j in range(nbk):
        sl = slice(j * BD, (j + 1) * BD)
        outs.append(jnp.dot(hi[:, sl], gm, preferred_element_type=F32) + jnp.dot(lo[:, sl], gm, preferred_element_type=F32))
    return outs[0] if nbk == 1 else jnp.concatenate(outs, axis=1)


class _RowReader:
    def __init__(self, ref):
        self.ref = ref

    def __getitem__(self, rows):
        return self.ref[rows, :]


def _shifted(ext_ref, cur, before8, after8, downs=(), ups=()):
    r = cur.shape[0]
    if downs:
        ext_ref[0:8, :] = before8
    ext_ref[8:8 + r, :] = cur
    if ups:
        ext_ref[8 + r:16 + r, :] = after8
    return [ext_ref[pl.ds(8 - j, r), :] for j in downs] + [ext_ref[pl.ds(8 + j, r), :] for j in ups]


def _lru_gates(xc, pv, wa_ref, wx_ref):
    xcb = xc.astype(MXU_DTYPE)
    ga = jax.nn.sigmoid(_block_mm(xcb, wa_ref, False) + pv[5:6])
    gx = jax.nn.sigmoid(_block_mm(xcb, wx_ref, False) + pv[6:7])
    sp = _softplus(-pv[7:8])
    log_a = -LRU_C * ga * sp
    a = jnp.exp(log_a)
    e2 = _expm1(2.0 * log_a)
    mult = jnp.sqrt(-e2)
    return xcb, ga, gx, sp, a, e2, mult


def _gelu_parts(y):
    th = jnp.tanh(GELU_K * (y + GELU_C * y * y * y))
    return 0.5 * y * (1.0 + th), th


def _mixer_fwd(z, pv, wa, wx, gm, pad, name, carried=()):
    m = z.shape[0]
    c = pv.shape[1]
    r = MIX_ROWS
    nb = m // r

    def body(z_ref, pv_ref, wa_ref, wx_ref, gm_ref, mixed_ref, hs_ref, ext_ref, tailx_ref, tailc_ref, carry_ref):
        b = pl.program_id(0)

        @pl.when(b == 0)
        def _():
            tailx_ref[...] = jnp.zeros_like(tailx_ref)
            tailc_ref[...] = jnp.zeros_like(tailc_ref)
            carry_ref[...] = jnp.zeros_like(carry_ref)

        pv = _RowReader(pv_ref)
        row = b * r + lax.broadcasted_iota(jnp.int32, (r, 1), 0)
        lrow = lax.broadcasted_iota(jnp.int32, (r, c), 0)
        maskf = (row >= pad).astype(F32)
        y = z_ref[:, 0:c]
        xl = z_ref[:, c:2 * c]
        bs = z_ref[:, 2 * c:3 * c]
        cv = z_ref[:, 3 * c:4 * c] * z_ref[:, 4 * c:5 * c]

        x1, x2, x3 = _shifted(ext_ref, xl, tailx_ref[...], None, downs=(1, 2, 3))
        tailx_ref[...] = z_ref[pl.ds(r - 8, 8), c:2 * c]
        xc = pv[4:5] + pv[3:4] * xl + pv[2:3] * x1 + pv[1:2] * x2 + pv[0:1] * x3
        _, _, gx, _, a, _, mult = _lru_gates(xc, pv, wa_ref, wx_ref)
        uu = mult * (gx * xc) * maskf

        acc_a = a
        acc_h = uu
        dlt = 1
        while dlt < r:
            keep = lrow >= dlt
            sh_a = pltpu.roll(acc_a, dlt, axis=0)
            sh_h = pltpu.roll(acc_h, dlt, axis=0)
            acc_h = acc_h + acc_a * jnp.where(keep, sh_h, 0.0)
            acc_a = acc_a * jnp.where(keep, sh_a, 1.0)
            dlt *= 2
        hs = acc_h + acc_a * carry_ref[...]
        hs_ref[...] = hs
        carry_ref[...] = hs_ref[pl.ds(r - 1, 1), :]

        gelu_y, _ = _gelu_parts(y)
        lru_out = hs * gelu_y
        c1, c2 = _shifted(ext_ref, cv, tailc_ref[...], None, downs=(1, 2))
        tailc_ref[...] = cv[r - 8:r]
        sc_out = bs * (pv[10:11] * cv + pv[9:10] * c1 + pv[8:9] * c2)

        rl = lax.rsqrt(_group_mean(lru_out * lru_out, gm_ref) + EPS)
        rs = lax.rsqrt(_group_mean(sc_out * sc_out, gm_ref) + EPS)
        mixed_ref[:, 0:c] = (lru_out * rl * pv[11:12]).astype(mixed_ref.dtype)
        mixed_ref[:, c:2 * c] = (sc_out * rs * pv[12:13]).astype(mixed_ref.dtype)

    full = lambda shape: pl.BlockSpec(shape, lambda b: (0,) * len(shape))
    return _call(
        body, grid=(nb,),
        in_specs=[pl.BlockSpec((r, 5 * c), lambda b: (b, 0)), full(pv.shape), full(wa.shape), full(wx.shape), full(gm.shape)],
        out_specs=[pl.BlockSpec((r, 2 * c), lambda b: (b, 0)), pl.BlockSpec((r, c), lambda b: (b, 0))],
        out_shape=[jax.ShapeDtypeStruct((m, 2 * c), MXU_DTYPE), jax.ShapeDtypeStruct((m, c), F32)],
        scratch_shapes=[pltpu.VMEM((r + 16, c), F32), pltpu.VMEM((8, c), F32), pltpu.VMEM((8, c), F32),
                        pltpu.VMEM((1, c), F32)],
        name=name, args=(z, pv, wa, wx, gm), carried=carried)


def _mixer_bwd(z, hs, dmixed, pv, wa, wx, gm, pad, name, carried=()):
    m = z.shape[0]
    c = pv.shape[1]
    r = MIX_ROWS
    nb = m // r
    r8 = r // 8

    def body(z_ref, zp_ref, hs_ref, hsp_ref, dm_ref, pv_ref, wa_ref, wx_ref, gm_ref,
             dz_ref, dpv_ref, dwa_ref, dwx_ref, ext_ref, hxc_ref, hsc_ref, hp_ref):
        i = pl.program_id(0)
        b = nb - 1 - i

        @pl.when(i == 0)
        def _():
            hxc_ref[...] = jnp.zeros_like(hxc_ref)
            hsc_ref[...] = jnp.zeros_like(hsc_ref)
            hp_ref[...] = jnp.zeros_like(hp_ref)
            dpv_ref[...] = jnp.zeros_like(dpv_ref)
            dwa_ref[...] = jnp.zeros_like(dwa_ref)
            dwx_ref[...] = jnp.zeros_like(dwx_ref)

        pv = _RowReader(pv_ref)
        row = b * r + lax.broadcasted_iota(jnp.int32, (r, 1), 0)
        lrow = lax.broadcasted_iota(jnp.int32, (r, c), 0)
        maskf = (row >= pad).astype(F32)
        has_prev = (b > 0).astype(F32)
        y = z_ref[:, 0:c]
        xl = z_ref[:, c:2 * c]
        bs = z_ref[:, 2 * c:3 * c]
        cs = z_ref[:, 3 * c:4 * c]
        vs = z_ref[:, 4 * c:5 * c]
        cv = cs * vs
        xl_prev = zp_ref[:, c:2 * c] * has_prev
        cv_prev = zp_ref[:, 3 * c:4 * c] * zp_ref[:, 4 * c:5 * c] * has_prev
        hs = hs_ref[...]

        x1, x2, x3 = _shifted(ext_ref, xl, xl_prev, None, downs=(1, 2, 3))
        xc = pv[4:5] + pv[3:4] * xl + pv[2:3] * x1 + pv[1:2] * x2 + pv[0:1] * x3
        xcb, ga, gx, sp, a, e2, mult = _lru_gates(xc, pv, wa_ref, wx_ref)
        gxx = gx * xc
        gelu_y, th = _gelu_parts(y)
        lru_out = hs * gelu_y
        c1, c2 = _shifted(ext_ref, cv, cv_prev, None, downs=(1, 2))
        sc = pv[10:11] * cv + pv[9:10] * c1 + pv[8:9] * c2
        sc_out = bs * sc

        def group_norm_bwd(v, dm, gain):
            rr = lax.rsqrt(_group_mean(v * v, gm_ref) + EPS)
            vh = v * rr
            dvh = dm * gain
            dv = rr * (dvh - vh * _group_mean(dvh * vh, gm_ref))
            return dv, jnp.sum(dm * vh, axis=0, keepdims=True)

        d_lru_out, d_og = group_norm_bwd(lru_out, dm_ref[:, 0:c], pv[11:12])
        d_sc_out, d_sg = group_norm_bwd(sc_out, dm_ref[:, c:2 * c], pv[12:13])
        dpv_ref[11:12, :] += d_og
        dpv_ref[12:13, :] += d_sg

        dhs = d_lru_out * gelu_y
        dgelu = 0.5 * (1.0 + th) + 0.5 * y * (1.0 - th * th) * GELU_K * (1.0 + 3.0 * GELU_C * y * y)
        dy = d_lru_out * hs * dgelu

        acc_a = a
        acc_p = a * dhs
        dlt = 1
        while dlt < r:
            keep = lrow < r - dlt
            sh_a = pltpu.roll(acc_a, r - dlt, axis=0)
            sh_p = pltpu.roll(acc_p, r - dlt, axis=0)
            acc_p = acc_p + acc_a * jnp.where(keep, sh_p, 0.0)
            acc_a = acc_a * jnp.where(keep, sh_a, 1.0)
            dlt *= 2
        p_all = acc_p + acc_a * hp_ref[0:1, :]
        (p_next,) = _shifted(ext_ref, p_all, None, hp_ref[...], ups=(1,))
        hp_ref[...] = p_all[0:8]
        q = dhs + p_next
        (hs_prev,) = _shifted(ext_ref, hs, hsp_ref[...] * has_prev, None, downs=(1,))
        duu = q * maskf
        da = q * hs_prev

        dmult = duu * gxx
        dgxx = duu * mult
        dgx = dgxx * xc
        dxc = dgxx * gx
        dlog_a = da * a - dmult * ((1.0 + e2) / mult)
        dga = dlog_a * (-LRU_C * sp)
        dsp = jnp.sum(dlog_a * (-LRU_C * ga), axis=0, keepdims=True)
        dpv_ref[7:8, :] += dsp * (-jax.nn.sigmoid(-pv[7:8]))
        dga_pre = dga * ga * (1.0 - ga)
        dgx_pre = dgx * gx * (1.0 - gx)
        dpv_ref[5:6, :] += jnp.sum(dga_pre, axis=0, keepdims=True)
        dpv_ref[6:7, :] += jnp.sum(dgx_pre, axis=0, keepdims=True)
        dga_b = dga_pre.astype(MXU_DTYPE)
        dgx_b = dgx_pre.astype(MXU_DTYPE)
        dxc = dxc + _block_mm(dga_b, wa_ref, True) + _block_mm(dgx_b, wx_ref, True)
        for j in range(c // BD):
            sl = slice(j * BD, (j + 1) * BD)
            dwa_ref[j] += _dot_tn(xcb[:, sl], dga_b[:, sl])
            dwx_ref[j] += _dot_tn(xcb[:, sl], dgx_b[:, sl])

        dpv_ref[4:5, :] += jnp.sum(dxc, axis=0, keepdims=True)
        dpv_ref[3:4, :] += jnp.sum(dxc * xl, axis=0, keepdims=True)
        dpv_ref[2:3, :] += jnp.sum(dxc * x1, axis=0, keepdims=True)
        dpv_ref[1:2, :] += jnp.sum(dxc * x2, axis=0, keepdims=True)
        dpv_ref[0:1, :] += jnp.sum(dxc * x3, axis=0, keepdims=True)
        u1, u2, u3 = _shifted(ext_ref, dxc, None, hxc_ref[...], ups=(1, 2, 3))
        hxc_ref[...] = dxc[0:8]
        dxl = pv[3:4] * dxc + pv[2:3] * u1 + pv[1:2] * u2 + pv[0:1] * u3

        dbs = d_sc_out * sc
        dsc = d_sc_out * bs
        dpv_ref[10:11, :] += jnp.sum(dsc * cv, axis=0, keepdims=True)
        dpv_ref[9:10, :] += jnp.sum(dsc * c1, axis=0, keepdims=True)
        dpv_ref[8:9, :] += jnp.sum(dsc * c2, axis=0, keepdims=True)
        s1, s2 = _shifted(ext_ref, dsc, None, hsc_ref[...], ups=(1, 2))
        hsc_ref[...] = dsc[0:8]
        dcv = pv[10:11] * dsc + pv[9:10] * s1 + pv[8:9] * s2

        dz_ref[:, 0:c] = (dy * maskf).astype(dz_ref.dtype)
        dz_ref[:, c:2 * c] = (dxl * maskf).astype(dz_ref.dtype)
        dz_ref[:, 2 * c:3 * c] = (dbs * maskf).astype(dz_ref.dtype)
        dz_ref[:, 3 * c:4 * c] = (dcv * vs * maskf).astype(dz_ref.dtype)
        dz_ref[:, 4 * c:5 * c] = (dcv * cs * maskf).astype(dz_ref.dtype)

    full = lambda shape: pl.BlockSpec(shape, lambda i: (0,) * len(shape))
    cur = lambda width: pl.BlockSpec((r, width), lambda i: (nb - 1 - i, 0))
    prev8 = lambda width: pl.BlockSpec((8, width), lambda i: (jnp.maximum((nb - 1 - i) * r8 - 1, 0), 0))
    return _call(
        body, grid=(nb,),
        in_specs=[cur(5 * c), prev8(5 * c), cur(c), prev8(c), cur(2 * c),
                  full(pv.shape), full(wa.shape), full(wx.shape), full(gm.shape)],
        out_specs=[cur(5 * c), full(pv.shape), full(wa.shape), full(wx.shape)],
        out_shape=[jax.ShapeDtypeStruct((m, 5 * c), MXU_DTYPE), jax.ShapeDtypeStruct(pv.shape, F32),
                   jax.ShapeDtypeStruct(wa.shape, F32), jax.ShapeDtypeStruct(wx.shape, F32)],
        scratch_shapes=[pltpu.VMEM((r + 16, c), F32), pltpu.VMEM((8, c), F32), pltpu.VMEM((8, c), F32),
                        pltpu.VMEM((8, c), F32)],
        name=name, args=(z, z, hs, hs, dmixed, pv, wa, wx, gm), carried=carried)


def _position():
    return lax.axis_index("x"), lax.axis_index("y"), lax.axis_index("c")


def _block_of(px, py, pc):
    return 4 * px + 2 * py + pc


class _TwoLevelGather:
    def __init__(self, n_arrays, rows_of, src_of, send_sems, recv_sems):
        x, y, c = _position()
        self.n, self.rows_of, self.src_of = n_arrays, rows_of, src_of
        self.send_sems, self.recv_sems = send_sems, recv_sems
        self.c, self.me, self.sibling = c, (x, y, c), (x, y, 1 - c)
        self.chips = [(1 - x, y), (x, 1 - y), (1 - x, 1 - y)]

    def _copy(self, i, k, block, to, src=None):
        return pltpu.make_async_remote_copy(
            src_ref=self.rows_of(i, *block) if src is None else src, dst_ref=self.rows_of(i, *block),
            send_sem=self.send_sems.at[7 * i + k], recv_sem=self.recv_sems.at[7 * i + k],
            device_id=to, device_id_type=MESH)

    def _first(self, i):
        own = [self._copy(i, 0, self.me, self.sibling, src=self.src_of(i))]
        return own + [self._copy(i, 1 + j, self.me, (*chip, self.c), src=self.src_of(i))
                      for j, chip in enumerate(self.chips)]

    def _passed(self, i, j):
        return self._copy(i, 4 + j, (*self.chips[j], self.c), self.sibling)

    def start(self):
        for i in range(self.n):
            for cp in self._first(i):
                cp.start()

    def forward(self):
        for i in range(self.n):
            for j, chip in enumerate(self.chips):
                self._copy(i, 1 + j, (*chip, self.c), self.me).wait_recv()
                self._passed(i, j).start()

    def drain(self):
        for i in range(self.n):
            self._copy(i, 0, self.sibling, self.me).wait_recv()
            for j, chip in enumerate(self.chips):
                self._copy(i, 4 + j, (*chip, 1 - self.c), self.me).wait_recv()
        for i in range(self.n):
            for cp in self._first(i) + [self._passed(i, j) for j in range(3)]:
                cp.wait_send()


class _CarriedGather:
    def __init__(self, shards, padded_rows, zeros):
        d = shards[0].shape[1]
        self.n = len(shards)
        self.rows = [s.shape[0] for s in shards]
        self.pads = [p - N_DEV * r for r, p in zip(self.rows, padded_rows)]
        assert max(self.pads) <= zeros.shape[0] and zeros.shape[1] == d
        self.arrays = list(shards) + [zeros]
        self.out_shapes = [jax.ShapeDtypeStruct((p, d), s.dtype) for s, p in zip(shards, padded_rows)]
        self.aliases = {}
        self.n_remote, self.n_local = 7 * self.n, 2 * self.n
        self.results = None

    def _rows_of(self, outs):
        def rows_of(i, px, py, pc):
            s = self.rows[i]
            return outs[i].at[pl.ds(pl.multiple_of(_block_of(px, py, pc) * s, SUBLANE_BF16), s), :]
        return rows_of

    def _gather(self, ins, outs, send_sems, recv_sems):
        return _TwoLevelGather(self.n, self._rows_of(outs), lambda i: ins[i], send_sems, recv_sems)

    def _local(self, ins, outs, local_sems):
        x, y, c = _position()
        rows_of = self._rows_of(outs)
        cps = []
        for i in range(self.n):
            cps.append(pltpu.make_async_copy(ins[i], rows_of(i, x, y, c), local_sems.at[2 * i]))
            if self.pads[i]:
                cps.append(pltpu.make_async_copy(ins[self.n].at[pl.ds(0, self.pads[i]), :],
                                                 outs[i].at[pl.ds(N_DEV * self.rows[i], self.pads[i]), :],
                                                 local_sems.at[2 * i + 1]))
        return cps

    def start(self, ins, outs, send_sems, recv_sems, local_sems):
        for cp in self._local(ins, outs, local_sems):
            cp.start()
        self._gather(ins, outs, send_sems, recv_sems).start()

    def forward(self, ins, outs, send_sems, recv_sems, local_sems):
        self._gather(ins, outs, send_sems, recv_sems).forward()

    def finish(self, ins, outs, send_sems, recv_sems, local_sems):
        self._gather(ins, outs, send_sems, recv_sems).drain()
        for cp in self._local(ins, outs, local_sems):
            cp.wait()


class _CarriedSwap:
    def __init__(self, grads, shard_rows):
        d = grads[0].shape[1]
        self.n, self.rows = len(grads), list(shard_rows)
        self.arrays = list(grads)
        self.out_shapes = [jax.ShapeDtypeStruct((4, s, d), g.dtype) for g, s in zip(grads, shard_rows)]
        self.aliases = {}
        self.n_remote, self.n_local = 4 * self.n, 0
        self.results = None

    def _copies(self, ins, outs, send_sems, recv_sems):
        x, y, c = _position()
        cps = []
        for i in range(self.n):
            s = self.rows[i]
            for k in range(4):
                blk = _block_of(k >> 1, k & 1, 1 - c)
                cps.append(pltpu.make_async_remote_copy(
                    src_ref=ins[i].at[pl.ds(pl.multiple_of(blk * s, SUBLANE_BF16), s), :], dst_ref=outs[i].at[k],
                    send_sem=send_sems.at[4 * i + k], recv_sem=recv_sems.at[4 * i + k],
                    device_id=(x, y, 1 - c), device_id_type=MESH))
        return cps

    def start(self, ins, outs, send_sems, recv_sems, local_sems):
        for cp in self._copies(ins, outs, send_sems, recv_sems):
            cp.start()

    def forward(self, *_):
        pass

    def finish(self, ins, outs, send_sems, recv_sems, local_sems):
        for cp in self._copies(ins, outs, send_sems, recv_sems):
            cp.wait()


class _CarriedChipExchange:
    def __init__(self, presums):
        self.n = len(presums)
        self.arrays = list(presums)
        self.out_shapes = [jax.ShapeDtypeStruct(p.shape, p.dtype) for p in presums]
        self.aliases = {}
        self.n_remote, self.n_local = 3 * self.n, 0
        self.results = None

    def _copies(self, ins, outs, send_sems, recv_sems):
        x, y, c = _position()
        cps = []
        for i in range(self.n):
            for r in range(1, 4):
                cps.append(pltpu.make_async_remote_copy(
                    src_ref=ins[i].at[r - 1], dst_ref=outs[i].at[r - 1],
                    send_sem=send_sems.at[3 * i + r - 1], recv_sem=recv_sems.at[3 * i + r - 1],
                    device_id=(x ^ (r >> 1), y ^ (r & 1), c), device_id_type=MESH))
        return cps

    def start(self, ins, outs, send_sems, recv_sems, local_sems):
        for cp in self._copies(ins, outs, send_sems, recv_sems):
            cp.start()

    def forward(self, *_):
        pass

    def finish(self, ins, outs, send_sems, recv_sems, local_sems):
        for cp in self._copies(ins, outs, send_sems, recv_sems):
            cp.wait()


def _gather_small(block, reduce, name):
    rr, nn = block.shape

    def body(x_ref, out_ref, *rest):
        if reduce:
            stack_ref, send_sems, recv_sems, local_sem = rest
        else:
            send_sems, recv_sems, local_sem = rest
            stack_ref = out_ref
        x, y, c = _position()

        def rows_of(i, px, py, pc):
            return stack_ref.at[pl.ds(pl.multiple_of(_block_of(px, py, pc) * rr, 8), rr), :]

        own = pltpu.make_async_copy(x_ref, rows_of(0, x, y, c), local_sem)
        own.start()
        gather = _TwoLevelGather(1, rows_of, lambda i: x_ref, send_sems, recv_sems)
        gather.start()
        gather.forward()
        gather.drain()
        own.wait()
        if reduce:
            acc = stack_ref[0:rr, :]
            for k in range(1, N_DEV):
                acc = acc + stack_ref[k * rr:(k + 1) * rr, :]
            out_ref[...] = acc

    vmem = pl.BlockSpec(memory_space=pltpu.VMEM)
    scratch = [pltpu.SemaphoreType.DMA((7,)), pltpu.SemaphoreType.DMA((7,)), pltpu.SemaphoreType.DMA]
    if reduce:
        scratch = [pltpu.VMEM((N_DEV * rr, nn), F32)] + scratch
    out_rows = rr if reduce else N_DEV * rr
    return pl.pallas_call(
        body, in_specs=[vmem], out_specs=vmem, out_shape=jax.ShapeDtypeStruct((out_rows, nn), F32),
        scratch_shapes=scratch, name=name, compiler_params=_params())(block)


def _presum(where, grad, swapped, name):
    s, d = swapped.shape[1], swapped.shape[2]
    tc = _tile(d, 512, LANE)

    def body(where_ref, g_ref, sw_ref, o_ref):
        o_ref[0] = (g_ref[...].astype(F32) + sw_ref[0].astype(F32)).astype(o_ref.dtype)

    return _call(
        body, grid=(3, d // tc),
        in_specs=[pl.BlockSpec((s, tc), lambda r, j, where: (where[1 + r], j)),
                  pl.BlockSpec((1, s, tc), lambda r, j, where: (where[5 + r], 0, j))],
        out_specs=pl.BlockSpec((1, s, tc), lambda r, j, where: (r, 0, j)),
        out_shape=jax.ShapeDtypeStruct((3, s, d), WIRE_DTYPE), name=name, args=(grad, swapped), prefetch=(where,))


def _final_sum(where, grad, swapped, received, name, carried=()):
    s, d = swapped.shape[1], swapped.shape[2]
    tc = _tile(d, 512, LANE)

    def body(where_ref, g_ref, sw_ref, r_ref, o_ref):
        acc = g_ref[...].astype(F32) + sw_ref[0].astype(F32)
        for k in range(3):
            acc = acc + r_ref[k].astype(F32)
        o_ref[...] = acc

    return _call(
        body, grid=(d // tc,),
        in_specs=[pl.BlockSpec((s, tc), lambda j, where: (where[0], j)),
                  pl.BlockSpec((1, s, tc), lambda j, where: (where[4], 0, j)),
                  pl.BlockSpec((3, s, tc), lambda j, where: (0, 0, j))],
        out_specs=pl.BlockSpec((s, tc), lambda j, where: (0, j)),
        out_shape=jax.ShapeDtypeStruct((s, d), F32), name=name, args=(grad, swapped, received),
        prefetch=(where,), carried=carried)


class _GradReduction:
    def __init__(self, key, grad, shard_rows, where):
        self.key, self.grad, self.rows, self.where = key, grad, shard_rows, where

    def swap(self):
        self._swap = _CarriedSwap([self.grad], [self.rows])
        return self._swap

    def exchange(self):
        presum = _presum(self.where, self.grad, self._swap.results[0], "presum_" + self.key)
        self._exchange = _CarriedChipExchange([presum])
        return self._exchange

    def total(self, carried=()):
        return _final_sum(self.where, self.grad, self._swap.results[0], self._exchange.results[0],
                          "sum_" + self.key, carried)


def _adamw(w, g, m, v, name):
    rows, cols = w.shape
    tr = _tile(rows, 256, 8)

    def body(w_ref, g_ref, m_ref, v_ref, d_ref, nm_ref, nv_ref):
        g = g_ref[...]
        nm = ADAM_B1 * m_ref[...] + (1.0 - ADAM_B1) * g
        nv = ADAM_B2 * v_ref[...] + (1.0 - ADAM_B2) * (g * g)
        m_hat = nm / (1.0 - ADAM_B1 ** ADAM_STEP)
        v_hat = nv / (1.0 - ADAM_B2 ** ADAM_STEP)
        d_ref[...] = -ADAM_LR * (m_hat / (jnp.sqrt(v_hat) + ADAM_EPS) + ADAM_WD * w_ref[...])
        nm_ref[...] = nm
        nv_ref[...] = nv

    spec = pl.BlockSpec((tr, cols), lambda i: (i, 0))
    return pl.pallas_call(
        body, grid=(rows // tr,), in_specs=[spec] * 4, out_specs=[spec] * 3,
        out_shape=[jax.ShapeDtypeStruct((rows, cols), F32)] * 3, name=name, compiler_params=_params())(w, g, m, v)


def _pack_rows(arrays, width, row_quantum=8):
    flat = jnp.concatenate([a.reshape(-1) for a in arrays])
    total = _round_up(flat.shape[0], row_quantum * width)
    flat = jnp.pad(flat, (0, total - flat.shape[0]))
    return flat.reshape(-1, width)


def _unpack_rows(packed, shapes):
    flat = packed.reshape(-1)
    out = []
    off = 0
    for shp in shapes:
        size = 1
        for s in shp:
            size *= s
        out.append(flat[off:off + size].reshape(shp))
        off += size
    return out


def _block_diag(w):
    h, hb, _ = w.shape
    per = BD // hb
    w4 = w.reshape(h // per, per, hb, hb)
    eye = jnp.eye(per, dtype=w.dtype)
    return jnp.einsum('npij,pq->npiqj', w4, eye).reshape(h // per, BD, BD)


def _block_diag_extract(bd, hb):
    nbk = bd.shape[0]
    per = BD // hb
    b5 = bd.reshape(nbk, per, hb, per, hb)
    eye = jnp.eye(per, dtype=bd.dtype)
    return jnp.einsum('npiqj,pq->npij', b5, eye).reshape(nbk * per, hb, hb)


def kernel(x, meta_tokens, ffn1_pre_g, ffn1_w_gate, ffn1_w_up, ffn1_w_down, ffn1_post_g, mix_pre_g, w_in, lru_conv_w, lru_conv_b, lru_w_a, lru_b_a, lru_w_x, lru_b_x, lru_lambda, sconv_w, lru_out_g, sconv_out_g, w_out, mix_post_g, ffn2_pre_g, ffn2_w_gate, ffn2_w_up, ffn2_w_down, ffn2_post_g, loss_target, m_meta_tokens, m_ffn1_pre_g, m_ffn1_w_gate, m_ffn1_w_up, m_ffn1_w_down, m_ffn1_post_g, m_mix_pre_g, m_w_in, m_lru_conv_w, m_lru_conv_b, m_lru_w_a, m_lru_b_a, m_lru_w_x, m_lru_b_x, m_lru_lambda, m_sconv_w, m_lru_out_g, m_sconv_out_g, m_w_out, m_mix_post_g, m_ffn2_pre_g, m_ffn2_w_gate, m_ffn2_w_up, m_ffn2_w_down, m_ffn2_post_g, v_meta_tokens, v_ffn1_pre_g, v_ffn1_w_gate, v_ffn1_w_up, v_ffn1_w_down, v_ffn1_post_g, v_mix_pre_g, v_w_in, v_lru_conv_w, v_lru_conv_b, v_lru_w_a, v_lru_b_a, v_lru_w_x, v_lru_b_x, v_lru_lambda, v_sconv_w, v_lru_out_g, v_sconv_out_g, v_w_out, v_mix_post_g, v_ffn2_pre_g, v_ffn2_w_gate, v_ffn2_w_up, v_ffn2_w_down, v_ffn2_post_g):
    given = dict(locals())
    wts = {n: given[n] for n in WEIGHT_NAMES}
    mom = {n: given["m_" + n] for n in WEIGHT_NAMES}
    var = {n: given["v_" + n] for n in WEIGHT_NAMES}

    xi, yi, ci = _position()
    me = _block_of(xi, yi, ci)
    x2 = x[0]
    seq, d = x2.shape
    n_meta = meta_tokens.shape[0]
    m_rows = _round_up(n_meta + seq, ROW_ALIGN)
    pad = m_rows - n_meta - seq
    lead = pad + n_meta
    c = lru_conv_b.shape[1]
    hb = lru_w_a.shape[-1]
    dm = meta_tokens.shape[1]
    cs_ = lru_conv_w.shape[2]
    kw4, kw3 = lru_conv_w.shape[1], sconv_w.shape[1]
    assert d == 2 * c and c % BD == 0 and BD % hb == 0 and cs_ <= dm and kw4 == 4 and kw3 == 3

    small = jnp.zeros((_round_up(n_meta + kw4 + kw3, 8), dm), F32)
    small = small.at[0:n_meta].set(meta_tokens)
    small = small.at[n_meta:n_meta + kw4, 0:cs_].set(lru_conv_w[0])
    small = small.at[n_meta + kw4:n_meta + kw4 + kw3, 0:cs_].set(sconv_w[0])
    sr = small.shape[0]
    small_all = _gather_small(small, False, "gather_small").reshape(N_DEV, sr, dm)
    meta_full = small_all[:, 0:n_meta, :].transpose(1, 0, 2).reshape(n_meta, d)
    conv_w_full = small_all[:, n_meta:n_meta + kw4, 0:cs_].transpose(1, 0, 2).reshape(kw4, c)
    sconv_w_full = small_all[:, n_meta + kw4:n_meta + kw4 + kw3, 0:cs_].transpose(1, 0, 2).reshape(kw3, c)

    big = ['ffn1_w_gate', 'ffn1_w_up', 'ffn1_w_down', 'w_in', 'w_out', 'ffn2_w_gate', 'ffn2_w_up', 'ffn2_w_down']
    col_sharded = {'ffn1_w_gate', 'ffn1_w_up', 'w_in', 'ffn2_w_gate', 'ffn2_w_up'}
    shards = []
    for nme in big:
        w = wts[nme][0].astype(WIRE_DTYPE)
        shards.append(w.T if nme in col_sharded else w)
    shard_rows = dict(zip(big, [s.shape[0] for s in shards]))
    zeros = jnp.zeros((F_ALIGN, d), WIRE_DTYPE)

    def gather(*names):
        sel = [shards[big.index(nme)] for nme in names]
        padded = [_round_up(N_DEV * shard_rows[nme], LANE if nme in ('w_in', 'w_out') else F_ALIGN) for nme in names]
        return _CarriedGather(sel, padded, zeros)

    pv = jnp.zeros((16, c), F32)
    pv = pv.at[0:4].set(conv_w_full).at[4].set(lru_conv_b[0]).at[5].set(lru_b_a[0]).at[6].set(lru_b_x[0])
    pv = pv.at[7].set(lru_lambda[0]).at[8:11].set(sconv_w_full).at[11].set(lru_out_g[0]).at[12].set(sconv_out_g[0])
    wa_bd = _block_diag(lru_w_a[0]).astype(MXU_DTYPE)
    wx_bd = _block_diag(lru_w_x[0]).astype(MXU_DTYPE)
    gs = c // N_GROUPS
    gidx = jnp.arange(BD) // gs
    gm = jnp.where(gidx[:, None] == gidx[None, :], 1.0 / gs, 0.0).astype(MXU_DTYPE)

    h0 = jnp.concatenate([jnp.zeros((pad, d), F32), meta_full, x2], axis=0)
    ride = gather('ffn1_w_gate', 'ffn1_w_up')
    n1 = _rmsnorm(h0, ffn1_pre_g, "ffn1_prenorm", carried=[ride])
    wg1, wu1 = ride.results
    ride = gather('ffn1_w_down', 'w_in')
    g1, u1, a1 = _ffn_gate_up(n1, wg1, wu1, "ffn1_gate_up", carried=[ride])
    wd1, win_t = ride.results
    ride = gather('w_out', 'ffn2_w_gate')
    fo1, h1 = _mm_residual_norm(a1, wd1, h0, ffn1_post_g, 0.5, "ffn1_down", carried=[ride])
    wout, wg2 = ride.results
    un = _rmsnorm(h1, mix_pre_g, "mix_prenorm")
    ride = gather('ffn2_w_up')
    z = _mm_nt(un, win_t, "mix_in_proj", carried=[ride])
    (wu2,) = ride.results
    mixed, hs = _mixer_fwd(z, pv, wa_bd, wx_bd, gm, pad, "mixer_fwd")
    o_mix, h2 = _mm_residual_norm(mixed, wout, h1, mix_post_g, 1.0, "mix_out_proj")
    n2 = _rmsnorm(h2, ffn2_pre_g, "ffn2_prenorm")
    ride = gather('ffn2_w_down')
    g2, u2, a2 = _ffn_gate_up(n2, wg2, wu2, "ffn2_gate_up", carried=[ride])
    (wd2,) = ride.results
    fo2, h3 = _mm_residual_norm(a2, wd2, h2, ffn2_post_g, 0.5, "ffn2_down")
    dh3, loss_part = _loss_grad(h3, loss_target[0], lead, "loss_grad")
    loss = lax.psum(loss_part[0, 0], ("x", "y", "c"))

    chip_rel = [2 * (xi ^ (r >> 1)) + (yi ^ (r & 1)) for r in range(4)]
    where = jnp.stack([2 * k + ci for k in chip_rel] + chip_rel).astype(jnp.int32)
    red = {}

    def reduction(nme, grad):
        red[nme] = _GradReduction(nme, grad, shard_rows[nme], where)
        return red[nme]

    dfo2, d_post2 = _norm_bwd(fo2, ffn2_post_g, dh3, 0.5, "ffn2_postnorm_bwd")
    dg2, du2 = _ffn_hidden_bwd(dfo2, wd2, g2, u2, "ffn2_hidden_bwd")
    r_wd2 = reduction('ffn2_w_down', _mm_tn(a2, dfo2, "ffn2_dw_down"))
    r_wg2 = reduction('ffn2_w_gate', _mm_tn(dg2, n2, "ffn2_dw_gate", carried=[r_wd2.swap()]))
    r_wu2 = reduction('ffn2_w_up', _mm_tn(du2, n2, "ffn2_dw_up", carried=[r_wd2.exchange(), r_wg2.swap()]))
    dh2, d_pre2 = _mm_norm_bwd([(dg2, wg2), (du2, wu2)], h2, ffn2_pre_g, dh3, "ffn2_dx",
                               carried=[r_wg2.exchange(), r_wu2.swap()])
    do_mix, d_mix_post = _norm_bwd(o_mix, mix_post_g, dh2, 1.0, "mix_postnorm_bwd")
    dmixed = _mm_nt(do_mix, wout, "mix_out_proj_bwd")
    r_wout = reduction('w_out', _mm_tn(mixed, do_mix, "mix_dw_out"))
    dz, dpv, dwa_bd, dwx_bd = _mixer_bwd(z, hs, dmixed, pv, wa_bd, wx_bd, gm, pad, "mixer_bwd",
                                         carried=[r_wu2.exchange(), r_wout.swap()])
    dh1, d_mix_pre = _mm_norm_bwd([(dz, win_t)], h1, mix_pre_g, dh2, "mix_dx", carried=[r_wout.exchange()])
    r_win = reduction('w_in', _mm_tn(dz, un, "mix_dw_in"))
    dfo1, d_post1 = _norm_bwd(fo1, ffn1_post_g, dh1, 0.5, "ffn1_postnorm_bwd")
    dg1, du1 = _ffn_hidden_bwd(dfo1, wd1, g1, u1, "ffn1_hidden_bwd", carried=[r_win.swap()])
    r_wd1 = reduction('ffn1_w_down', _mm_tn(a1, dfo1, "ffn1_dw_down", carried=[r_win.exchange()]))
    r_wg1 = reduction('ffn1_w_gate', _mm_tn(dg1, n1, "ffn1_dw_gate", carried=[r_wd1.swap()]))
    r_wu1 = reduction('ffn1_w_up', _mm_tn(du1, n1, "ffn1_dw_up", carried=[r_wd1.exchange(), r_wg1.swap()]))
    row_tile = _norm_bwd_row_tile(m_rows)
    n_tiles = m_rows // row_tile
    half = n_tiles // 2
    assert half >= 1 and half * row_tile >= lead
    dh0_a, d_pre1_a = _mm_norm_bwd([(dg1, wg1), (du1, wu1)], h0, ffn1_pre_g, dh1, "ffn1_dx_a",
                                   carried=[r_wg1.exchange(), r_wu1.swap()], row_tiles=(0, half))
    dh0_b, d_pre1 = _mm_norm_bwd([(dg1, wg1), (du1, wu1)], h0, ffn1_pre_g, dh1, "ffn1_dx_b",
                                 carried=[r_wu1.exchange()], row_tiles=(half, n_tiles - half), dg_init=d_pre1_a)
    grad_x = jnp.concatenate([dh0_a[lead:], dh0_b], axis=0)[None]
    d_meta = dh0_a[pad:lead]

    grads = {}
    for nme in big:
        gsum = red[nme].total()
        grads[nme] = (gsum.T if nme in col_sharded else gsum)[None]

    small_names = ['ffn1_pre_g', 'ffn1_post_g', 'mix_pre_g', 'mix_post_g', 'ffn2_pre_g', 'ffn2_post_g',
                   'lru_conv_b', 'lru_b_a', 'lru_b_x', 'lru_lambda', 'lru_out_g', 'sconv_out_g',
                   'lru_conv_w', 'sconv_w', 'lru_w_a', 'lru_w_x', 'meta_tokens']
    small_parts = [d_pre1, d_post1, d_mix_pre, d_mix_post, d_pre2, d_post2,
                   dpv[4:5], dpv[5:6], dpv[6:7], dpv[7:8], dpv[11:12], dpv[12:13],
                   dpv[0:4], dpv[8:11], _block_diag_extract(dwa_bd, hb), _block_diag_extract(dwx_bd, hb),
                   d_meta]
    small_shapes = [p.shape for p in small_parts]
    small_sum = _gather_small(_pack_rows(small_parts, d), True, "reduce_small")
    for nme, gsm in zip(small_names, _unpack_rows(small_sum, small_shapes)):
        if nme == 'meta_tokens':
            grads[nme] = lax.dynamic_slice_in_dim(gsm, me * dm, dm, axis=1)
        elif nme in ('lru_conv_w', 'sconv_w'):
            grads[nme] = lax.dynamic_slice_in_dim(gsm, me * cs_, cs_, axis=1)[None]
        else:
            grads[nme] = gsm.reshape(wts[nme].shape)

    delta, new_m, new_v = {}, {}, {}
    for nme in big:
        shp = wts[nme].shape
        dl, nm_, nv_ = _adamw(wts[nme][0], grads[nme][0], mom[nme][0], var[nme][0], "adamw_" + nme)
        delta[nme], new_m[nme], new_v[nme] = dl.reshape(shp), nm_.reshape(shp), nv_.reshape(shp)
    rest = [n for n in WEIGHT_NAMES if n not in big]
    rest_shapes = [wts[n].shape for n in rest]
    packed = [_pack_rows([src[n] for n in rest], LANE, 256) for src in (wts, grads, mom, var)]
    for out, packed_out in zip((delta, new_m, new_v), _adamw(*packed, "adamw_small")):
        for nme, arr in zip(rest, _unpack_rows(packed_out, rest_shapes)):
            out[nme] = arr

    return (loss, grad_x, *[grads[n] for n in WEIGHT_NAMES], *[delta[n] for n in WEIGHT_NAMES],
            *[new_m[n] for n in WEIGHT_NAMES], *[new_v[n] for n in WEIGHT_NAMES])
```

```python
import functools

import jax
import jax.numpy as jnp
from jax import lax
from jax.experimental import pallas as pl
from jax.experimental.pallas import tpu as pltpu

F32 = jnp.float32
MXU_DTYPE = jnp.bfloat16
WIRE_DTYPE = jnp.bfloat16
MESH = pl.DeviceIdType.MESH

EPS = 1e-6
LRU_C = 8.0
N_GROUPS = 16
ADAM_LR = 0.001
ADAM_B1 = 0.9
ADAM_B2 = 0.999
ADAM_EPS = 1e-08
ADAM_WD = 0.01
ADAM_STEP = 10

N_DEV = 8
LANE = 128
SUBLANE_BF16 = 16
ROW_ALIGN = 128
F_ALIGN = 512
BD = 256
K_TILE = 512
ACC_ROWS = 704
ACC_ROWS_2 = 528
MIX_ROWS = 128
VMEM_LIMIT_MB = 56

WEIGHT_NAMES = ['meta_tokens', 'ffn1_pre_g', 'ffn1_w_gate', 'ffn1_w_up', 'ffn1_w_down', 'ffn1_post_g',
                'mix_pre_g', 'w_in', 'lru_conv_w', 'lru_conv_b', 'lru_w_a', 'lru_b_a', 'lru_w_x', 'lru_b_x',
                'lru_lambda', 'sconv_w', 'lru_out_g', 'sconv_out_g', 'w_out', 'mix_post_g', 'ffn2_pre_g',
                'ffn2_w_gate', 'ffn2_w_up', 'ffn2_w_down', 'ffn2_post_g']


def _round_up(n, q):
    return (n + q - 1) // q * q


def _tile(n, target, q):
    best = None
    t = q
    while t <= min(n, target):
        if n % t == 0:
            best = t
        t += q
    assert best is not None, (n, target, q)
    return best


def _params(**kw):
    return pltpu.CompilerParams(vmem_limit_bytes=VMEM_LIMIT_MB << 20, **kw)


def _call(body, *, grid, in_specs, out_specs, out_shape, name, args, scratch_shapes=(), carried=(), prefetch=()):
    carried = list(carried)
    n_pf = len(prefetch)

    def launch(fn, in_specs_, out_specs_, out_shape_, scratch_, operands, aliases_):
        if n_pf:
            spec = pltpu.PrefetchScalarGridSpec(num_scalar_prefetch=n_pf, grid=grid, in_specs=in_specs_,
                                                out_specs=out_specs_, scratch_shapes=scratch_)
            return pl.pallas_call(fn, grid_spec=spec, out_shape=out_shape_, input_output_aliases=aliases_,
                                  name=name, compiler_params=_params())(*prefetch, *operands)
        return pl.pallas_call(fn, grid=grid, in_specs=in_specs_, out_specs=out_specs_, out_shape=out_shape_,
                              scratch_shapes=scratch_, input_output_aliases=aliases_, name=name,
                              compiler_params=_params())(*operands)

    if not carried:
        return launch(body, in_specs, out_specs, out_shape, list(scratch_shapes), args, {})
    single = not isinstance(out_shape, (list, tuple))
    out_specs_l = [out_specs] if single else list(out_specs)
    out_shape_l = [out_shape] if single else list(out_shape)
    n_in, n_out, n_scr = len(in_specs), len(out_specs_l), len(scratch_shapes)
    hbm = pl.BlockSpec(memory_space=pl.ANY)
    c_in = [a for cm in carried for a in cm.arrays]
    c_out = [s for cm in carried for s in cm.out_shapes]
    c_scr = []
    aliases = {}
    in_off, out_off = n_pf + n_in, n_out
    for cm in carried:
        c_scr += [pltpu.SemaphoreType.DMA((cm.n_remote,)), pltpu.SemaphoreType.DMA((cm.n_remote,)),
                  pltpu.SemaphoreType.DMA((max(cm.n_local, 1),))]
        for k, v in cm.aliases.items():
            aliases[in_off + k] = out_off + v
        in_off += len(cm.arrays)
        out_off += len(cm.out_shapes)
    steps = 1
    for g in grid:
        steps *= g
    forward_steps = [min(int(cm.forward_at * steps), steps - 1) for cm in carried]

    def wrapped(*refs):
        pf = refs[:n_pf]
        p = n_pf
        ins = refs[p:p + n_in]
        p += n_in
        cins = refs[p:p + len(c_in)]
        p += len(c_in)
        outs = refs[p:p + n_out]
        p += n_out
        couts = refs[p:p + len(c_out)]
        p += len(c_out)
        scr = refs[p:p + n_scr]
        csem = refs[p + n_scr:]
        lin = 0
        for axis, g in enumerate(grid):
            lin = lin * g + pl.program_id(axis)
        views = []
        io = oo = 0
        for j, cm in enumerate(carried):
            views.append((cins[io:io + len(cm.arrays)], couts[oo:oo + len(cm.out_shapes)],
                          csem[3 * j], csem[3 * j + 1], csem[3 * j + 2]))
            io += len(cm.arrays)
            oo += len(cm.out_shapes)

        @pl.when(lin == 0)
        def _():
            for cm, v in zip(carried, views):
                cm.start(*v)

        body(*pf, *ins, *outs, *scr)

        for cm, v, step in zip(carried, views, forward_steps):
            pl.when(lin == step)(functools.partial(cm.forward, *v))

        @pl.when(lin == steps - 1)
        def _():
            for cm, v in zip(carried, views):
                cm.finish(*v)

    res = launch(wrapped, list(in_specs) + [hbm] * len(c_in), out_specs_l + [hbm] * len(c_out),
                 out_shape_l + c_out, list(scratch_shapes) + c_scr, (*args, *c_in), aliases)
    oo = n_out
    for cm in carried:
        cm.results = list(res[oo:oo + len(cm.out_shapes)])
        oo += len(cm.out_shapes)
    return res[0] if single else list(res[:n_out])


def _rmsnorm(h, g, name, carried=()):
    m, d = h.shape
    tm = _tile(m, 528, SUBLANE_BF16)

    def body(h_ref, g_ref, o_ref):
        x = h_ref[...]
        r = lax.rsqrt(jnp.mean(x * x, axis=-1, keepdims=True) + EPS)
        o_ref[...] = (x * r * g_ref[...]).astype(o_ref.dtype)

    return _call(
        body, grid=(m // tm,),
        in_specs=[pl.BlockSpec((tm, d), lambda i: (i, 0)), pl.BlockSpec((1, d), lambda i: (0, 0))],
        out_specs=pl.BlockSpec((tm, d), lambda i: (i, 0)),
        out_shape=jax.ShapeDtypeStruct((m, d), MXU_DTYPE), name=name, args=(h, g), carried=carried)


def _rmsnorm_bwd_rows(x, g, dy):
    r = lax.rsqrt(jnp.mean(x * x, axis=-1, keepdims=True) + EPS)
    xh = x * r
    dyh = dy * g
    dx = r * (dyh - xh * jnp.mean(dyh * xh, axis=-1, keepdims=True))
    return dx, dy * xh


def _norm_bwd(x, g, dy, scale, name, carried=()):
    m, d = x.shape
    tm = _tile(m, 528, SUBLANE_BF16)

    def body(x_ref, g_ref, dy_ref, dx_ref, dg_ref):
        @pl.when(pl.program_id(0) == 0)
        def _():
            dg_ref[...] = jnp.zeros_like(dg_ref)

        dx, dgr = _rmsnorm_bwd_rows(x_ref[...], g_ref[...], scale * dy_ref[...])
        dx_ref[...] = dx.astype(dx_ref.dtype)
        dg_ref[...] += jnp.sum(dgr, axis=0, keepdims=True)

    return _call(
        body, grid=(m // tm,),
        in_specs=[pl.BlockSpec((tm, d), lambda i: (i, 0)), pl.BlockSpec((1, d), lambda i: (0, 0)),
                  pl.BlockSpec((tm, d), lambda i: (i, 0))],
        out_specs=[pl.BlockSpec((tm, d), lambda i: (i, 0)), pl.BlockSpec((1, d), lambda i: (0, 0))],
        out_shape=[jax.ShapeDtypeStruct((m, d), MXU_DTYPE), jax.ShapeDtypeStruct((1, d), F32)],
        name=name, args=(x, g, dy), carried=carried)


def _loss_grad(h, target, lead, name):
    m, d = h.shape
    tl = ROW_ALIGN
    assert lead % tl == 0 and target.shape[0] == m - lead
    lead_blocks = lead // tl

    def body(h_ref, t_ref, dh_ref, l_ref):
        i = pl.program_id(0)

        @pl.when(i == 0)
        def _():
            l_ref[...] = jnp.zeros_like(l_ref)

        @pl.when(i < lead_blocks)
        def _():
            dh_ref[...] = jnp.zeros_like(dh_ref)

        @pl.when(i >= lead_blocks)
        def _():
            e = h_ref[...] - t_ref[...]
            dh_ref[...] = e * (1.0 / d)
            row = jnp.sum(e * e, axis=-1, keepdims=True) * (1.0 / d)
            l_ref[...] += 0.5 * jnp.sum(row, axis=0, keepdims=True)

    return pl.pallas_call(
        body, grid=(m // tl,),
        in_specs=[pl.BlockSpec((tl, d), lambda i: (i, 0)),
                  pl.BlockSpec((tl, d), lambda i: (jnp.maximum(i - lead_blocks, 0), 0))],
        out_specs=[pl.BlockSpec((tl, d), lambda i: (i, 0)), pl.BlockSpec((1, 1), lambda i: (0, 0))],
        out_shape=[jax.ShapeDtypeStruct((m, d), F32), jax.ShapeDtypeStruct((1, 1), F32)],
        name=name, compiler_params=_params())(h, target)


def _dot_nt(a, b):
    return lax.dot_general(a, b, (((1,), (1,)), ((), ())), preferred_element_type=F32)


def _dot_tn(a, b):
    return lax.dot_general(a, b, (((0,), (0,)), ((), ())), preferred_element_type=F32)


def _mm_nt(a, w, name, carried=()):
    m, k = a.shape
    n = w.shape[0]
    tm = _tile(m, 1056, SUBLANE_BF16)
    tn = _tile(n, 512, LANE)

    def body(a_ref, w_ref, o_ref):
        o_ref[...] = _dot_nt(a_ref[...], w_ref[...])

    return _call(
        body, grid=(m // tm, n // tn),
        in_specs=[pl.BlockSpec((tm, k), lambda i, j: (i, 0)), pl.BlockSpec((tn, k), lambda i, j: (j, 0))],
        out_specs=pl.BlockSpec((tm, tn), lambda i, j: (i, j)),
        out_shape=jax.ShapeDtypeStruct((m, n), F32), name=name, args=(a, w), carried=carried)


def _ffn_gate_up(n_act, wg_t, wu_t, name, carried=()):
    m, d = n_act.shape
    fp = wg_t.shape[0]
    tm = _tile(m, 1056, SUBLANE_BF16)
    tn = _tile(fp, 512, LANE)

    def body(n_ref, wg_ref, wu_ref, g_ref, u_ref, a_ref):
        n = n_ref[...]
        g = _dot_nt(n, wg_ref[...])
        u = _dot_nt(n, wu_ref[...])
        g_ref[...] = g.astype(g_ref.dtype)
        u_ref[...] = u.astype(u_ref.dtype)
        a_ref[...] = (g * jax.nn.sigmoid(g) * u).astype(a_ref.dtype)

    act = pl.BlockSpec((tm, tn), lambda i, j: (i, j))
    wsp = pl.BlockSpec((tn, d), lambda i, j: (j, 0))
    return _call(
        body, grid=(m // tm, fp // tn),
        in_specs=[pl.BlockSpec((tm, d), lambda i, j: (i, 0)), wsp, wsp],
        out_specs=[act, act, act],
        out_shape=[jax.ShapeDtypeStruct((m, fp), MXU_DTYPE)] * 3, name=name, args=(n_act, wg_t, wu_t), carried=carried)


def _ffn_hidden_bwd(dfo, wd, g_act, u_act, name, carried=()):
    m, d = dfo.shape
    fp = wd.shape[0]
    tm = _tile(m, 1056, SUBLANE_BF16)
    tn = _tile(fp, 512, LANE)

    def body(df_ref, wd_ref, g_ref, u_ref, dg_ref, du_ref):
        da = _dot_nt(df_ref[...], wd_ref[...])
        g = g_ref[...].astype(F32)
        u = u_ref[...].astype(F32)
        s = jax.nn.sigmoid(g)
        du_ref[...] = (da * (g * s)).astype(du_ref.dtype)
        dg_ref[...] = (da * u * (s * (1.0 + g * (1.0 - s)))).astype(dg_ref.dtype)

    act = pl.BlockSpec((tm, tn), lambda i, j: (i, j))
    return _call(
        body, grid=(m // tm, fp // tn),
        in_specs=[pl.BlockSpec((tm, d), lambda i, j: (i, 0)), pl.BlockSpec((tn, d), lambda i, j: (j, 0)), act, act],
        out_specs=[act, act],
        out_shape=[jax.ShapeDtypeStruct((m, fp), MXU_DTYPE)] * 2, name=name, args=(dfo, wd, g_act, u_act),
        carried=carried)


def _mm_residual_norm(a, w, h, g, scale, name, carried=()):
    m, k = a.shape
    d = w.shape[1]
    tm = _tile(m, ACC_ROWS, SUBLANE_BF16)
    tk = _tile(k, K_TILE, LANE)
    nk = k // tk

    def body(a_ref, w_ref, h_ref, g_ref, fo_ref, hn_ref, acc_ref):
        kk = pl.program_id(1)
        prod = jnp.dot(a_ref[...], w_ref[...], preferred_element_type=F32)

        @pl.when(kk == 0)
        def _():
            acc_ref[...] = prod

        @pl.when(kk > 0)
        def _():
            acc_ref[...] += prod

        @pl.when(kk == nk - 1)
        def _():
            fo = acc_ref[...]
            fo_ref[...] = fo
            r = lax.rsqrt(jnp.mean(fo * fo, axis=-1, keepdims=True) + EPS)
            hn_ref[...] = h_ref[...] + scale * (fo * r * g_ref[...])

    row = pl.BlockSpec((tm, d), lambda i, kk: (i, 0))
    row_once = pl.BlockSpec((tm, d), lambda i, kk: (i, 0), pipeline_mode=pl.Buffered(1))
    return _call(
        body, grid=(m // tm, nk),
        in_specs=[pl.BlockSpec((tm, tk), lambda i, kk: (i, kk)), pl.BlockSpec((tk, d), lambda i, kk: (kk, 0)),
                  row_once, pl.BlockSpec((1, d), lambda i, kk: (0, 0))],
        out_specs=[row, row],
        out_shape=[jax.ShapeDtypeStruct((m, d), F32)] * 2,
        scratch_shapes=[pltpu.VMEM((tm, d), F32)], name=name, args=(a, w, h, g), carried=carried)


def _norm_bwd_row_tile(m, n_pairs):
    return _tile(m, ACC_ROWS if n_pairs == 1 else ACC_ROWS_2, SUBLANE_BF16)


def _mm_norm_bwd(pairs, h, g, dh_up, name, carried=(), row_tiles=None, dg_init=None):
    n_pairs = len(pairs)
    m, k = pairs[0][0].shape
    d = h.shape[1]
    tm = _norm_bwd_row_tile(m, n_pairs)
    tk = _tile(k, K_TILE, LANE)
    nk = k // tk
    t0, nt = row_tiles if row_tiles is not None else (0, m // tm)
    if dg_init is None:
        dg_init = jnp.zeros((1, d), F32)

    def body(*refs):
        ops = refs[:2 * n_pairs]
        h_ref, g_ref, up_ref, init_ref, dh_ref, dg_ref, acc_ref = refs[2 * n_pairs:]
        i = pl.program_id(0)
        kk = pl.program_id(1)

        @pl.when(jnp.logical_and(i == 0, kk == 0))
        def _():
            dg_ref[...] = init_ref[...]

        prod = jnp.dot(ops[0][...], ops[1][...], preferred_element_type=F32)
        for p in range(1, n_pairs):
            prod = prod + jnp.dot(ops[2 * p][...], ops[2 * p + 1][...], preferred_element_type=F32)

        @pl.when(kk == 0)
        def _():
            acc_ref[...] = prod

        @pl.when(kk > 0)
        def _():
            acc_ref[...] += prod

        @pl.when(kk == nk - 1)
        def _():
            dx, dgr = _rmsnorm_bwd_rows(h_ref[...], g_ref[...], acc_ref[...])
            dh_ref[...] = up_ref[...] + dx
            dg_ref[...] += jnp.sum(dgr, axis=0, keepdims=True)

    row_in = pl.BlockSpec((tm, d), lambda i, kk: (t0 + i, 0), pipeline_mode=pl.Buffered(1))
    vec = pl.BlockSpec((1, d), lambda i, kk: (0, 0))
    in_specs = []
    args = []
    for a, w in pairs:
        in_specs += [pl.BlockSpec((tm, tk), lambda i, kk: (t0 + i, kk)), pl.BlockSpec((tk, d), lambda i, kk: (kk, 0))]
        args += [a, w]
    return _call(
        body, grid=(nt, nk),
        in_specs=in_specs + [row_in, vec, row_in, vec], out_specs=[pl.BlockSpec((tm, d), lambda i, kk: (i, 0)), vec],
        out_shape=[jax.ShapeDtypeStruct((nt * tm, d), F32), jax.ShapeDtypeStruct((1, d), F32)],
        scratch_shapes=[pltpu.VMEM((tm, d), F32)], name=name, args=(*args, h, g, dh_up, dg_init), carried=carried)


def _mm_tn(a, b, name, carried=()):
    m, ka = a.shape
    d = b.shape[1]
    tf = _tile(ka, 512, LANE)

    def body(a_ref, b_ref, o_ref):
        o_ref[...] = _dot_tn(a_ref[...], b_ref[...]).astype(o_ref.dtype)

    return _call(
        body, grid=(ka // tf,),
        in_specs=[pl.BlockSpec((m, tf), lambda j: (0, j)),
                  pl.BlockSpec((m, d), lambda j: (0, 0), pipeline_mode=pl.Buffered(1))],
        out_specs=pl.BlockSpec((tf, d), lambda j: (j, 0)),
        out_shape=jax.ShapeDtypeStruct((ka, d), WIRE_DTYPE), name=name, args=(a, b), carried=carried)


GELU_K = 0.7978845608028654
GELU_C = 0.044715


def _expm1(x):
    series = x * (1.0 + x * (1.0 / 2 + x * (1.0 / 6 + x * (1.0 / 24 + x * (1.0 / 120 + x * (1.0 / 720 + x * (1.0 / 5040)))))))
    return jnp.where(jnp.abs(x) < 0.3, series, jnp.exp(x) - 1.0)


def _softplus(x):
    return jnp.maximum(x, 0.0) + jnp.log1p(jnp.exp(-jnp.abs(x)))


def _block_mm(v, w_ref, transposed):
    nbk = w_ref.shape[0]
    outs = []
    for j in range(nbk):
        vj = v[:, j * BD:(j + 1) * BD]
        outs.append(_dot_nt(vj, w_ref[j]) if transposed else jnp.dot(vj, w_ref[j], preferred_element_type=F32))
    return outs[0] if nbk == 1 else jnp.concatenate(outs, axis=1)


def _group_mean(q, gm_ref):
    hi = q.astype(MXU_DTYPE)
    lo = (q - hi.astype(F32)).astype(MXU_DTYPE)
    nbk = q.shape[1] // BD
    gm = gm_ref[...]
    outs = []
    for j in range(nbk):
        sl = slice(j * BD, (j + 1) * BD)
        outs.append(jnp.dot(hi[:, sl], gm, preferred_element_type=F32) + jnp.dot(lo[:, sl], gm, preferred_element_type=F32))
    return outs[0] if nbk == 1 else jnp.concatenate(outs, axis=1)


class _RowReader:
    def __init__(self, ref):
        self.ref = ref

    def __getitem__(self, rows):
        return self.ref[rows, :]


def _shifted(ext_ref, cur, before8, after8, downs=(), ups=()):
    r = cur.shape[0]
    if downs:
        ext_ref[0:8, :] = before8
    ext_ref[8:8 + r, :] = cur
    if ups:
        ext_ref[8 + r:16 + r, :] = after8
    return [ext_ref[pl.ds(8 - j, r), :] for j in downs] + [ext_ref[pl.ds(8 + j, r), :] for j in ups]


def _lru_gates(xc, pv, wa_ref, wx_ref):
    xcb = xc.astype(MXU_DTYPE)
    ga = jax.nn.sigmoid(_block_mm(xcb, wa_ref, False) + pv[5:6])
    gx = jax.nn.sigmoid(_block_mm(xcb, wx_ref, False) + pv[6:7])
    sp = _softplus(-pv[7:8])
    log_a = -LRU_C * ga * sp
    a = jnp.exp(log_a)
    e2 = _expm1(2.0 * log_a)
    mult = jnp.sqrt(-e2)
    return xcb, ga, gx, sp, a, e2, mult


def _gelu_parts(y):
    th = jnp.tanh(GELU_K * (y + GELU_C * y * y * y))
    return 0.5 * y * (1.0 + th), th


def _mixer_fwd(z, pv, wa, wx, gm, pad, name, carried=()):
    m = z.shape[0]
    c = pv.shape[1]
    r = MIX_ROWS
    nb = m // r

    def body(z_ref, pv_ref, wa_ref, wx_ref, gm_ref, mixed_ref, hs_ref, ext_ref, tailx_ref, tailc_ref, carry_ref):
        b = pl.program_id(0)

        @pl.when(b == 0)
        def _():
            tailx_ref[...] = jnp.zeros_like(tailx_ref)
            tailc_ref[...] = jnp.zeros_like(tailc_ref)
            carry_ref[...] = jnp.zeros_like(carry_ref)

        pv = _RowReader(pv_ref)
        row = b * r + lax.broadcasted_iota(jnp.int32, (r, 1), 0)
        lrow = lax.broadcasted_iota(jnp.int32, (r, c), 0)
        maskf = (row >= pad).astype(F32)
        y = z_ref[:, 0:c]
        xl = z_ref[:, c:2 * c]
        bs = z_ref[:, 2 * c:3 * c]
        cv = z_ref[:, 3 * c:4 * c] * z_ref[:, 4 * c:5 * c]

        x1, x2, x3 = _shifted(ext_ref, xl, tailx_ref[...], None, downs=(1, 2, 3))
        tailx_ref[...] = z_ref[pl.ds(r - 8, 8), c:2 * c]
        xc = pv[4:5] + pv[3:4] * xl + pv[2:3] * x1 + pv[1:2] * x2 + pv[0:1] * x3
        _, _, gx, _, a, _, mult = _lru_gates(xc, pv, wa_ref, wx_ref)
        uu = mult * (gx * xc) * maskf

        acc_a = a
        acc_h = uu
        dlt = 1
        while dlt < r:
            keep = lrow >= dlt
            sh_a = pltpu.roll(acc_a, dlt, axis=0)
            sh_h = pltpu.roll(acc_h, dlt, axis=0)
            acc_h = acc_h + acc_a * jnp.where(keep, sh_h, 0.0)
            acc_a = acc_a * jnp.where(keep, sh_a, 1.0)
            dlt *= 2
        hs = acc_h + acc_a * carry_ref[...]
        hs_ref[...] = hs
        carry_ref[...] = hs_ref[pl.ds(r - 1, 1), :]

        gelu_y, _ = _gelu_parts(y)
        lru_out = hs * gelu_y
        c1, c2 = _shifted(ext_ref, cv, tailc_ref[...], None, downs=(1, 2))
        tailc_ref[...] = cv[r - 8:r]
        sc_out = bs * (pv[10:11] * cv + pv[9:10] * c1 + pv[8:9] * c2)

        rl = lax.rsqrt(_group_mean(lru_out * lru_out, gm_ref) + EPS)
        rs = lax.rsqrt(_group_mean(sc_out * sc_out, gm_ref) + EPS)
        mixed_ref[:, 0:c] = (lru_out * rl * pv[11:12]).astype(mixed_ref.dtype)
        mixed_ref[:, c:2 * c] = (sc_out * rs * pv[12:13]).astype(mixed_ref.dtype)

    full = lambda shape: pl.BlockSpec(shape, lambda b: (0,) * len(shape))
    return _call(
        body, grid=(nb,),
        in_specs=[pl.BlockSpec((r, 5 * c), lambda b: (b, 0)), full(pv.shape), full(wa.shape), full(wx.shape), full(gm.shape)],
        out_specs=[pl.BlockSpec((r, 2 * c), lambda b: (b, 0)), pl.BlockSpec((r, c), lambda b: (b, 0))],
        out_shape=[jax.ShapeDtypeStruct((m, 2 * c), MXU_DTYPE), jax.ShapeDtypeStruct((m, c), F32)],
        scratch_shapes=[pltpu.VMEM((r + 16, c), F32), pltpu.VMEM((8, c), F32), pltpu.VMEM((8, c), F32),
                        pltpu.VMEM((1, c), F32)],
        name=name, args=(z, pv, wa, wx, gm), carried=carried)


def _mixer_bwd(z, hs, dmixed, pv, wa, wx, gm, pad, name, carried=()):
    m = z.shape[0]
    c = pv.shape[1]
    r = MIX_ROWS
    nb = m // r
    r8 = r // 8

    def body(z_ref, zp_ref, hs_ref, hsp_ref, dm_ref, pv_ref, wa_ref, wx_ref, gm_ref,
             dz_ref, dpv_ref, dwa_ref, dwx_ref, ext_ref, hxc_ref, hsc_ref, hp_ref):
        i = pl.program_id(0)
        b = nb - 1 - i

        @pl.when(i == 0)
        def _():
            hxc_ref[...] = jnp.zeros_like(hxc_ref)
            hsc_ref[...] = jnp.zeros_like(hsc_ref)
            hp_ref[...] = jnp.zeros_like(hp_ref)
            dpv_ref[...] = jnp.zeros_like(dpv_ref)
            dwa_ref[...] = jnp.zeros_like(dwa_ref)
            dwx_ref[...] = jnp.zeros_like(dwx_ref)

        pv = _RowReader(pv_ref)
        row = b * r + lax.broadcasted_iota(jnp.int32, (r, 1), 0)
        lrow = lax.broadcasted_iota(jnp.int32, (r, c), 0)
        maskf = (row >= pad).astype(F32)
        has_prev = (b > 0).astype(F32)
        y = z_ref[:, 0:c]
        xl = z_ref[:, c:2 * c]
        bs = z_ref[:, 2 * c:3 * c]
        cs = z_ref[:, 3 * c:4 * c]
        vs = z_ref[:, 4 * c:5 * c]
        cv = cs * vs
        xl_prev = zp_ref[:, c:2 * c] * has_prev
        cv_prev = zp_ref[:, 3 * c:4 * c] * zp_ref[:, 4 * c:5 * c] * has_prev
        hs = hs_ref[...]

        x1, x2, x3 = _shifted(ext_ref, xl, xl_prev, None, downs=(1, 2, 3))
        xc = pv[4:5] + pv[3:4] * xl + pv[2:3] * x1 + pv[1:2] * x2 + pv[0:1] * x3
        xcb, ga, gx, sp, a, e2, mult = _lru_gates(xc, pv, wa_ref, wx_ref)
        gxx = gx * xc
        gelu_y, th = _gelu_parts(y)
        lru_out = hs * gelu_y
        c1, c2 = _shifted(ext_ref, cv, cv_prev, None, downs=(1, 2))
        sc = pv[10:11] * cv + pv[9:10] * c1 + pv[8:9] * c2
        sc_out = bs * sc

        def group_norm_bwd(v, dm, gain):
            rr = lax.rsqrt(_group_mean(v * v, gm_ref) + EPS)
            vh = v * rr
            dvh = dm * gain
            dv = rr * (dvh - vh * _group_mean(dvh * vh, gm_ref))
            return dv, jnp.sum(dm * vh, axis=0, keepdims=True)

        d_lru_out, d_og = group_norm_bwd(lru_out, dm_ref[:, 0:c], pv[11:12])
        d_sc_out, d_sg = group_norm_bwd(sc_out, dm_ref[:, c:2 * c], pv[12:13])
        dpv_ref[11:12, :] += d_og
        dpv_ref[12:13, :] += d_sg

        dhs = d_lru_out * gelu_y
        dgelu = 0.5 * (1.0 + th) + 0.5 * y * (1.0 - th * th) * GELU_K * (1.0 + 3.0 * GELU_C * y * y)
        dy = d_lru_out * hs * dgelu

        acc_a = a
        acc_p = a * dhs
        dlt = 1
        while dlt < r:
            keep = lrow < r - dlt
            sh_a = pltpu.roll(acc_a, r - dlt, axis=0)
            sh_p = pltpu.roll(acc_p, r - dlt, axis=0)
            acc_p = acc_p + acc_a * jnp.where(keep, sh_p, 0.0)
            acc_a = acc_a * jnp.where(keep, sh_a, 1.0)
            dlt *= 2
        p_all = acc_p + acc_a * hp_ref[0:1, :]
        (p_next,) = _shifted(ext_ref, p_all, None, hp_ref[...], ups=(1,))
        hp_ref[...] = p_all[0:8]
        q = dhs + p_next
        (hs_prev,) = _shifted(ext_ref, hs, hsp_ref[...] * has_prev, None, downs=(1,))
        duu = q * maskf
        da = q * hs_prev

        dmult = duu * gxx
        dgxx = duu * mult
        dgx = dgxx * xc
        dxc = dgxx * gx
        dlog_a = da * a - dmult * ((1.0 + e2) / mult)
        dga = dlog_a * (-LRU_C * sp)
        dsp = jnp.sum(dlog_a * (-LRU_C * ga), axis=0, keepdims=True)
        dpv_ref[7:8, :] += dsp * (-jax.nn.sigmoid(-pv[7:8]))
        dga_pre = dga * ga * (1.0 - ga)
        dgx_pre = dgx * gx * (1.0 - gx)
        dpv_ref[5:6, :] += jnp.sum(dga_pre, axis=0, keepdims=True)
        dpv_ref[6:7, :] += jnp.sum(dgx_pre, axis=0, keepdims=True)
        dga_b = dga_pre.astype(MXU_DTYPE)
        dgx_b = dgx_pre.astype(MXU_DTYPE)
        dxc = dxc + _block_mm(dga_b, wa_ref, True) + _block_mm(dgx_b, wx_ref, True)
        for j in range(c // BD):
            sl = slice(j * BD, (j + 1) * BD)
            dwa_ref[j] += _dot_tn(xcb[:, sl], dga_b[:, sl])
            dwx_ref[j] += _dot_tn(xcb[:, sl], dgx_b[:, sl])

        dpv_ref[4:5, :] += jnp.sum(dxc, axis=0, keepdims=True)
        dpv_ref[3:4, :] += jnp.sum(dxc * xl, axis=0, keepdims=True)
        dpv_ref[2:3, :] += jnp.sum(dxc * x1, axis=0, keepdims=True)
        dpv_ref[1:2, :] += jnp.sum(dxc * x2, axis=0, keepdims=True)
        dpv_ref[0:1, :] += jnp.sum(dxc * x3, axis=0, keepdims=True)
        u1, u2, u3 = _shifted(ext_ref, dxc, None, hxc_ref[...], ups=(1, 2, 3))
        hxc_ref[...] = dxc[0:8]
        dxl = pv[3:4] * dxc + pv[2:3] * u1 + pv[1:2] * u2 + pv[0:1] * u3

        dbs = d_sc_out * sc
        dsc = d_sc_out * bs
        dpv_ref[10:11, :] += jnp.sum(dsc * cv, axis=0, keepdims=True)
        dpv_ref[9:10, :] += jnp.sum(dsc * c1, axis=0, keepdims=True)
        dpv_ref[8:9, :] += jnp.sum(dsc * c2, axis=0, keepdims=True)
        s1, s2 = _shifted(ext_ref, dsc, None, hsc_ref[...], ups=(1, 2))
        hsc_ref[...] = dsc[0:8]
        dcv = pv[10:11] * dsc + pv[9:10] * s1 + pv[8:9] * s2

        dz_ref[:, 0:c] = (dy * maskf).astype(dz_ref.dtype)
        dz_ref[:, c:2 * c] = (dxl * maskf).astype(dz_ref.dtype)
        dz_ref[:, 2 * c:3 * c] = (dbs * maskf).astype(dz_ref.dtype)
        dz_ref[:, 3 * c:4 * c] = (dcv * vs * maskf).astype(dz_ref.dtype)
        dz_ref[:, 4 * c:5 * c] = (dcv * cs * maskf).astype(dz_ref.dtype)

    full = lambda shape: pl.BlockSpec(shape, lambda i: (0,) * len(shape))
    cur = lambda width: pl.BlockSpec((r, width), lambda i: (nb - 1 - i, 0))
    prev8 = lambda width: pl.BlockSpec((8, width), lambda i: (jnp.maximum((nb - 1 - i) * r8 - 1, 0), 0))
    return _call(
        body, grid=(nb,),
        in_specs=[cur(5 * c), prev8(5 * c), cur(c), prev8(c), cur(2 * c),
                  full(pv.shape), full(wa.shape), full(wx.shape), full(gm.shape)],
        out_specs=[cur(5 * c), full(pv.shape), full(wa.shape), full(wx.shape)],
        out_shape=[jax.ShapeDtypeStruct((m, 5 * c), MXU_DTYPE), jax.ShapeDtypeStruct(pv.shape, F32),
                   jax.ShapeDtypeStruct(wa.shape, F32), jax.ShapeDtypeStruct(wx.shape, F32)],
        scratch_shapes=[pltpu.VMEM((r + 16, c), F32), pltpu.VMEM((8, c), F32), pltpu.VMEM((8, c), F32),
                        pltpu.VMEM((8, c), F32)],
        name=name, args=(z, z, hs, hs, dmixed, pv, wa, wx, gm), carried=carried)


def _position():
    return lax.axis_index("x"), lax.axis_index("y"), lax.axis_index("c")


def _block_of(px, py, pc):
    return 4 * px + 2 * py + pc


class _TwoLevelGather:
    def __init__(self, n_arrays, rows_of, src_of, send_sems, recv_sems):
        x, y, c = _position()
        self.n, self.rows_of, self.src_of = n_arrays, rows_of, src_of
        self.send_sems, self.recv_sems = send_sems, recv_sems
        self.c, self.me, self.sibling = c, (x, y, c), (x, y, 1 - c)
        self.chips = [(1 - x, y), (x, 1 - y), (1 - x, 1 - y)]

    def _copy(self, i, k, block, to, src=None):
        return pltpu.make_async_remote_copy(
            src_ref=self.rows_of(i, *block) if src is None else src, dst_ref=self.rows_of(i, *block),
            send_sem=self.send_sems.at[7 * i + k], recv_sem=self.recv_sems.at[7 * i + k],
            device_id=to, device_id_type=MESH)

    def _first(self, i):
        own = [self._copy(i, 0, self.me, self.sibling, src=self.src_of(i))]
        return own + [self._copy(i, 1 + j, self.me, (*chip, self.c), src=self.src_of(i))
                      for j, chip in enumerate(self.chips)]

    def _passed(self, i, j):
        return self._copy(i, 4 + j, (*self.chips[j], self.c), self.sibling)

    def start(self):
        for i in range(self.n):
            for cp in self._first(i):
                cp.start()

    def forward(self):
        for i in range(self.n):
            for j, chip in enumerate(self.chips):
                self._copy(i, 1 + j, (*chip, self.c), self.me).wait_recv()
                self._passed(i, j).start()

    def drain(self):
        for i in range(self.n):
            self._copy(i, 0, self.sibling, self.me).wait_recv()
            for j, chip in enumerate(self.chips):
                self._copy(i, 4 + j, (*chip, 1 - self.c), self.me).wait_recv()
        for i in range(self.n):
            for cp in self._first(i) + [self._passed(i, j) for j in range(3)]:
                cp.wait_send()


class _CarriedGather:
    def __init__(self, shards, padded_rows, zeros, forward_at):
        d = shards[0].shape[1]
        self.forward_at = forward_at
        self.n = len(shards)
        self.rows = [s.shape[0] for s in shards]
        self.pads = [p - N_DEV * r for r, p in zip(self.rows, padded_rows)]
        assert max(self.pads) <= zeros.shape[0] and zeros.shape[1] == d
        self.arrays = list(shards) + [zeros]
        self.out_shapes = [jax.ShapeDtypeStruct((p, d), s.dtype) for s, p in zip(shards, padded_rows)]
        self.aliases = {}
        self.n_remote, self.n_local = 7 * self.n, 2 * self.n
        self.results = None

    def _rows_of(self, outs):
        def rows_of(i, px, py, pc):
            s = self.rows[i]
            return outs[i].at[pl.ds(pl.multiple_of(_block_of(px, py, pc) * s, SUBLANE_BF16), s), :]
        return rows_of

    def _gather(self, ins, outs, send_sems, recv_sems):
        return _TwoLevelGather(self.n, self._rows_of(outs), lambda i: ins[i], send_sems, recv_sems)

    def _local(self, ins, outs, local_sems):
        x, y, c = _position()
        rows_of = self._rows_of(outs)
        cps = []
        for i in range(self.n):
            cps.append(pltpu.make_async_copy(ins[i], rows_of(i, x, y, c), local_sems.at[2 * i]))
            if self.pads[i]:
                cps.append(pltpu.make_async_copy(ins[self.n].at[pl.ds(0, self.pads[i]), :],
                                                 outs[i].at[pl.ds(N_DEV * self.rows[i], self.pads[i]), :],
                                                 local_sems.at[2 * i + 1]))
        return cps

    def start(self, ins, outs, send_sems, recv_sems, local_sems):
        for cp in self._local(ins, outs, local_sems):
            cp.start()
        self._gather(ins, outs, send_sems, recv_sems).start()

    def forward(self, ins, outs, send_sems, recv_sems, local_sems):
        self._gather(ins, outs, send_sems, recv_sems).forward()

    def finish(self, ins, outs, send_sems, recv_sems, local_sems):
        self._gather(ins, outs, send_sems, recv_sems).drain()
        for cp in self._local(ins, outs, local_sems):
            cp.wait()


class _CarriedSwap:
    def __init__(self, grads, shard_rows):
        d = grads[0].shape[1]
        self.n, self.rows = len(grads), list(shard_rows)
        self.arrays = list(grads)
        self.out_shapes = [jax.ShapeDtypeStruct((4, s, d), g.dtype) for g, s in zip(grads, shard_rows)]
        self.aliases = {}
        self.n_remote, self.n_local = 4 * self.n, 0
        self.forward_at = 1.0
        self.results = None

    def _copies(self, ins, outs, send_sems, recv_sems):
        x, y, c = _position()
        cps = []
        for i in range(self.n):
            s = self.rows[i]
            for k in range(4):
                blk = _block_of(k >> 1, k & 1, 1 - c)
                cps.append(pltpu.make_async_remote_copy(
                    src_ref=ins[i].at[pl.ds(pl.multiple_of(blk * s, SUBLANE_BF16), s), :], dst_ref=outs[i].at[k],
                    send_sem=send_sems.at[4 * i + k], recv_sem=recv_sems.at[4 * i + k],
                    device_id=(x, y, 1 - c), device_id_type=MESH))
        return cps

    def start(self, ins, outs, send_sems, recv_sems, local_sems):
        for cp in self._copies(ins, outs, send_sems, recv_sems):
            cp.start()

    def forward(self, *_):
        pass

    def finish(self, ins, outs, send_sems, recv_sems, local_sems):
        for cp in self._copies(ins, outs, send_sems, recv_sems):
            cp.wait()


class _CarriedChipExchange:
    def __init__(self, presums):
        self.n = len(presums)
        self.arrays = list(presums)
        self.out_shapes = [jax.ShapeDtypeStruct(p.shape, p.dtype) for p in presums]
        self.aliases = {}
        self.n_remote, self.n_local = 3 * self.n, 0
        self.forward_at = 1.0
        self.results = None

    def _copies(self, ins, outs, send_sems, recv_sems):
        x, y, c = _position()
        cps = []
        for i in range(self.n):
            for r in range(1, 4):
                cps.append(pltpu.make_async_remote_copy(
                    src_ref=ins[i].at[r - 1], dst_ref=outs[i].at[r - 1],
                    send_sem=send_sems.at[3 * i + r - 1], recv_sem=recv_sems.at[3 * i + r - 1],
                    device_id=(x ^ (r >> 1), y ^ (r & 1), c), device_id_type=MESH))
        return cps

    def start(self, ins, outs, send_sems, recv_sems, local_sems):
        for cp in self._copies(ins, outs, send_sems, recv_sems):
            cp.start()

    def forward(self, *_):
        pass

    def finish(self, ins, outs, send_sems, recv_sems, local_sems):
        for cp in self._copies(ins, outs, send_sems, recv_sems):
            cp.wait()


def _gather_small(block, reduce, name):
    rr, nn = block.shape

    def body(x_ref, out_ref, *rest):
        if reduce:
            stack_ref, send_sems, recv_sems, local_sem = rest
        else:
            send_sems, recv_sems, local_sem = rest
            stack_ref = out_ref
        x, y, c = _position()

        def rows_of(i, px, py, pc):
            return stack_ref.at[pl.ds(pl.multiple_of(_block_of(px, py, pc) * rr, 8), rr), :]

        own = pltpu.make_async_copy(x_ref, rows_of(0, x, y, c), local_sem)
        own.start()
        gather = _TwoLevelGather(1, rows_of, lambda i: x_ref, send_sems, recv_sems)
        gather.start()
        gather.forward()
        gather.drain()
        own.wait()
        if reduce:
            acc = stack_ref[0:rr, :]
            for k in range(1, N_DEV):
                acc = acc + stack_ref[k * rr:(k + 1) * rr, :]
            out_ref[...] = acc

    vmem = pl.BlockSpec(memory_space=pltpu.VMEM)
    scratch = [pltpu.SemaphoreType.DMA((7,)), pltpu.SemaphoreType.DMA((7,)), pltpu.SemaphoreType.DMA]
    if reduce:
        scratch = [pltpu.VMEM((N_DEV * rr, nn), F32)] + scratch
    out_rows = rr if reduce else N_DEV * rr
    return pl.pallas_call(
        body, in_specs=[vmem], out_specs=vmem, out_shape=jax.ShapeDtypeStruct((out_rows, nn), F32),
        scratch_shapes=scratch, name=name, compiler_params=_params())(block)


def _presum(where, grad, swapped, name):
    s, d = swapped.shape[1], swapped.shape[2]
    tc = _tile(d, 512, LANE)

    def body(where_ref, g_ref, sw_ref, o_ref):
        o_ref[0] = (g_ref[...].astype(F32) + sw_ref[0].astype(F32)).astype(o_ref.dtype)

    return _call(
        body, grid=(3, d // tc),
        in_specs=[pl.BlockSpec((s, tc), lambda r, j, where: (where[1 + r], j)),
                  pl.BlockSpec((1, s, tc), lambda r, j, where: (where[5 + r], 0, j))],
        out_specs=pl.BlockSpec((1, s, tc), lambda r, j, where: (r, 0, j)),
        out_shape=jax.ShapeDtypeStruct((3, s, d), WIRE_DTYPE), name=name, args=(grad, swapped), prefetch=(where,))


def _final_sum(where, grad, swapped, received, name, carried=()):
    s, d = swapped.shape[1], swapped.shape[2]
    tc = _tile(d, 512, LANE)

    def body(where_ref, g_ref, sw_ref, r_ref, o_ref):
        acc = g_ref[...].astype(F32) + sw_ref[0].astype(F32)
        for k in range(3):
            acc = acc + r_ref[k].astype(F32)
        o_ref[...] = acc

    return _call(
        body, grid=(d // tc,),
        in_specs=[pl.BlockSpec((s, tc), lambda j, where: (where[0], j)),
                  pl.BlockSpec((1, s, tc), lambda j, where: (where[4], 0, j)),
                  pl.BlockSpec((3, s, tc), lambda j, where: (0, 0, j))],
        out_specs=pl.BlockSpec((s, tc), lambda j, where: (0, j)),
        out_shape=jax.ShapeDtypeStruct((s, d), F32), name=name, args=(grad, swapped, received),
        prefetch=(where,), carried=carried)


class _GradReduction:
    def __init__(self, key, grad, shard_rows, where):
        self.key, self.grad, self.rows, self.where = key, grad, shard_rows, where

    def swap(self):
        self._swap = _CarriedSwap([self.grad], [self.rows])
        return self._swap

    def exchange(self):
        presum = _presum(self.where, self.grad, self._swap.results[0], "presum_" + self.key)
        self._exchange = _CarriedChipExchange([presum])
        return self._exchange

    def total(self, carried=()):
        return _final_sum(self.where, self.grad, self._swap.results[0], self._exchange.results[0],
                          "sum_" + self.key, carried)


def _adamw(w, g, m, v, name):
    rows, cols = w.shape
    tr = _tile(rows, 256, 8)

    def body(w_ref, g_ref, m_ref, v_ref, d_ref, nm_ref, nv_ref):
        g = g_ref[...]
        nm = ADAM_B1 * m_ref[...] + (1.0 - ADAM_B1) * g
        nv = ADAM_B2 * v_ref[...] + (1.0 - ADAM_B2) * (g * g)
        m_hat = nm / (1.0 - ADAM_B1 ** ADAM_STEP)
        v_hat = nv / (1.0 - ADAM_B2 ** ADAM_STEP)
        d_ref[...] = -ADAM_LR * (m_hat / (jnp.sqrt(v_hat) + ADAM_EPS) + ADAM_WD * w_ref[...])
        nm_ref[...] = nm
        nv_ref[...] = nv

    spec = pl.BlockSpec((tr, cols), lambda i: (i, 0))
    return pl.pallas_call(
        body, grid=(rows // tr,), in_specs=[spec] * 4, out_specs=[spec] * 3,
        out_shape=[jax.ShapeDtypeStruct((rows, cols), F32)] * 3, name=name, compiler_params=_params())(w, g, m, v)


def _pack_rows(arrays, width, row_quantum=8):
    flat = jnp.concatenate([a.reshape(-1) for a in arrays])
    total = _round_up(flat.shape[0], row_quantum * width)
    flat = jnp.pad(flat, (0, total - flat.shape[0]))
    return flat.reshape(-1, width)


def _unpack_rows(packed, shapes):
    flat = packed.reshape(-1)
    out = []
    off = 0
    for shp in shapes:
        size = 1
        for s in shp:
            size *= s
        out.append(flat[off:off + size].reshape(shp))
        off += size
    return out


def _block_diag(w):
    h, hb, _ = w.shape
    per = BD // hb
    w4 = w.reshape(h // per, per, hb, hb)
    eye = jnp.eye(per, dtype=w.dtype)
    return jnp.einsum('npij,pq->npiqj', w4, eye).reshape(h // per, BD, BD)


def _block_diag_extract(bd, hb):
    nbk = bd.shape[0]
    per = BD // hb
    b5 = bd.reshape(nbk, per, hb, per, hb)
    eye = jnp.eye(per, dtype=bd.dtype)
    return jnp.einsum('npiqj,pq->npij', b5, eye).reshape(nbk * per, hb, hb)


def kernel(x, meta_tokens, ffn1_pre_g, ffn1_w_gate, ffn1_w_up, ffn1_w_down, ffn1_post_g, mix_pre_g, w_in, lru_conv_w, lru_conv_b, lru_w_a, lru_b_a, lru_w_x, lru_b_x, lru_lambda, sconv_w, lru_out_g, sconv_out_g, w_out, mix_post_g, ffn2_pre_g, ffn2_w_gate, ffn2_w_up, ffn2_w_down, ffn2_post_g, loss_target, m_meta_tokens, m_ffn1_pre_g, m_ffn1_w_gate, m_ffn1_w_up, m_ffn1_w_down, m_ffn1_post_g, m_mix_pre_g, m_w_in, m_lru_conv_w, m_lru_conv_b, m_lru_w_a, m_lru_b_a, m_lru_w_x, m_lru_b_x, m_lru_lambda, m_sconv_w, m_lru_out_g, m_sconv_out_g, m_w_out, m_mix_post_g, m_ffn2_pre_g, m_ffn2_w_gate, m_ffn2_w_up, m_ffn2_w_down, m_ffn2_post_g, v_meta_tokens, v_ffn1_pre_g, v_ffn1_w_gate, v_ffn1_w_up, v_ffn1_w_down, v_ffn1_post_g, v_mix_pre_g, v_w_in, v_lru_conv_w, v_lru_conv_b, v_lru_w_a, v_lru_b_a, v_lru_w_x, v_lru_b_x, v_lru_lambda, v_sconv_w, v_lru_out_g, v_sconv_out_g, v_w_out, v_mix_post_g, v_ffn2_pre_g, v_ffn2_w_gate, v_ffn2_w_up, v_ffn2_w_down, v_ffn2_post_g):
    given = dict(locals())
    wts = {n: given[n] for n in WEIGHT_NAMES}
    mom = {n: given["m_" + n] for n in WEIGHT_NAMES}
    var = {n: given["v_" + n] for n in WEIGHT_NAMES}

    xi, yi, ci = _position()
    me = _block_of(xi, yi, ci)
    x2 = x[0]
    seq, d = x2.shape
    n_meta = meta_tokens.shape[0]
    m_rows = _round_up(n_meta + seq, ROW_ALIGN)
    pad = m_rows - n_meta - seq
    lead = pad + n_meta
    c = lru_conv_b.shape[1]
    hb = lru_w_a.shape[-1]
    dm = meta_tokens.shape[1]
    cs_ = lru_conv_w.shape[2]
    kw4, kw3 = lru_conv_w.shape[1], sconv_w.shape[1]
    assert d == 2 * c and c % BD == 0 and BD % hb == 0 and cs_ <= dm and kw4 == 4 and kw3 == 3

    small = jnp.zeros((_round_up(n_meta + kw4 + kw3, 8), dm), F32)
    small = small.at[0:n_meta].set(meta_tokens)
    small = small.at[n_meta:n_meta + kw4, 0:cs_].set(lru_conv_w[0])
    small = small.at[n_meta + kw4:n_meta + kw4 + kw3, 0:cs_].set(sconv_w[0])
    sr = small.shape[0]
    small_all = _gather_small(small, False, "gather_small").reshape(N_DEV, sr, dm)
    meta_full = small_all[:, 0:n_meta, :].transpose(1, 0, 2).reshape(n_meta, d)
    conv_w_full = small_all[:, n_meta:n_meta + kw4, 0:cs_].transpose(1, 0, 2).reshape(kw4, c)
    sconv_w_full = small_all[:, n_meta + kw4:n_meta + kw4 + kw3, 0:cs_].transpose(1, 0, 2).reshape(kw3, c)

    big = ['ffn1_w_gate', 'ffn1_w_up', 'ffn1_w_down', 'w_in', 'w_out', 'ffn2_w_gate', 'ffn2_w_up', 'ffn2_w_down']
    col_sharded = {'ffn1_w_gate', 'ffn1_w_up', 'w_in', 'ffn2_w_gate', 'ffn2_w_up'}
    shards = []
    for nme in big:
        w = wts[nme][0].astype(WIRE_DTYPE)
        shards.append(w.T if nme in col_sharded else w)
    shard_rows = dict(zip(big, [s.shape[0] for s in shards]))
    zeros = jnp.zeros((F_ALIGN, d), WIRE_DTYPE)

    def gather(forward_at, *names):
        sel = [shards[big.index(nme)] for nme in names]
        padded = [_round_up(N_DEV * shard_rows[nme], LANE if nme in ('w_in', 'w_out') else F_ALIGN) for nme in names]
        return _CarriedGather(sel, padded, zeros, forward_at)

    pv = jnp.zeros((16, c), F32)
    pv = pv.at[0:4].set(conv_w_full).at[4].set(lru_conv_b[0]).at[5].set(lru_b_a[0]).at[6].set(lru_b_x[0])
    pv = pv.at[7].set(lru_lambda[0]).at[8:11].set(sconv_w_full).at[11].set(lru_out_g[0]).at[12].set(sconv_out_g[0])
    wa_bd = _block_diag(lru_w_a[0]).astype(MXU_DTYPE)
    wx_bd = _block_diag(lru_w_x[0]).astype(MXU_DTYPE)
    gs = c // N_GROUPS
    gidx = jnp.arange(BD) // gs
    gm = jnp.where(gidx[:, None] == gidx[None, :], 1.0 / gs, 0.0).astype(MXU_DTYPE)

    h0 = jnp.concatenate([jnp.zeros((pad, d), F32), meta_full, x2], axis=0)
    ride = gather(1.0, 'ffn1_w_gate', 'ffn1_w_up')
    n1 = _rmsnorm(h0, ffn1_pre_g, "ffn1_prenorm", carried=[ride])
    wg1, wu1 = ride.results
    ride = gather(0.85, 'ffn1_w_down', 'w_out')
    g1, u1, a1 = _ffn_gate_up(n1, wg1, wu1, "ffn1_gate_up", carried=[ride])
    wd1, wout = ride.results
    ride = gather(0.8, 'w_in')
    fo1, h1 = _mm_residual_norm(a1, wd1, h0, ffn1_post_g, 0.5, "ffn1_down", carried=[ride])
    (win_t,) = ride.results
    un = _rmsnorm(h1, mix_pre_g, "mix_prenorm")
    ride = gather(1.0, 'ffn2_w_gate')
    z = _mm_nt(un, win_t, "mix_in_proj", carried=[ride])
    (wg2,) = ride.results
    ride = gather(1.0, 'ffn2_w_up')
    mixed, hs = _mixer_fwd(z, pv, wa_bd, wx_bd, gm, pad, "mixer_fwd", carried=[ride])
    (wu2,) = ride.results
    o_mix, h2 = _mm_residual_norm(mixed, wout, h1, mix_post_g, 1.0, "mix_out_proj")
    n2 = _rmsnorm(h2, ffn2_pre_g, "ffn2_prenorm")
    ride = gather(0.75, 'ffn2_w_down')
    g2, u2, a2 = _ffn_gate_up(n2, wg2, wu2, "ffn2_gate_up", carried=[ride])
    (wd2,) = ride.results
    fo2, h3 = _mm_residual_norm(a2, wd2, h2, ffn2_post_g, 0.5, "ffn2_down")
    dh3, loss_part = _loss_grad(h3, loss_target[0], lead, "loss_grad")
    loss = lax.psum(loss_part[0, 0], ("x", "y", "c"))

    chip_rel = [2 * (xi ^ (r >> 1)) + (yi ^ (r & 1)) for r in range(4)]
    where = jnp.stack([2 * k + ci for k in chip_rel] + chip_rel).astype(jnp.int32)
    red = {}

    def reduction(nme, grad):
        red[nme] = _GradReduction(nme, grad, shard_rows[nme], where)
        return red[nme]

    dfo2, d_post2 = _norm_bwd(fo2, ffn2_post_g, dh3, 0.5, "ffn2_postnorm_bwd")
    r_wd2 = reduction('ffn2_w_down', _mm_tn(a2, dfo2, "ffn2_dw_down"))
    dg2, du2 = _ffn_hidden_bwd(dfo2, wd2, g2, u2, "ffn2_hidden_bwd", carried=[r_wd2.swap()])
    r_wg2 = reduction('ffn2_w_gate', _mm_tn(dg2, n2, "ffn2_dw_gate", carried=[r_wd2.exchange()]))
    r_wu2 = reduction('ffn2_w_up', _mm_tn(du2, n2, "ffn2_dw_up", carried=[r_wg2.swap()]))
    dh2, d_pre2 = _mm_norm_bwd([(dg2, wg2), (du2, wu2)], h2, ffn2_pre_g, dh3, "ffn2_dx",
                               carried=[r_wg2.exchange(), r_wu2.swap()])
    do_mix, d_mix_post = _norm_bwd(o_mix, mix_post_g, dh2, 1.0, "mix_postnorm_bwd")
    dmixed = _mm_nt(do_mix, wout, "mix_out_proj_bwd")
    r_wout = reduction('w_out', _mm_tn(mixed, do_mix, "mix_dw_out"))
    dz, dpv, dwa_bd, dwx_bd = _mixer_bwd(z, hs, dmixed, pv, wa_bd, wx_bd, gm, pad, "mixer_bwd",
                                         carried=[r_wu2.exchange(), r_wout.swap()])
    r_win = reduction('w_in', _mm_tn(dz, un, "mix_dw_in", carried=[r_wout.exchange()]))
    dh1, d_mix_pre = _mm_norm_bwd([(dz, win_t)], h1, mix_pre_g, dh2, "mix_dx", carried=[r_win.swap()])
    dfo1, d_post1 = _norm_bwd(fo1, ffn1_post_g, dh1, 0.5, "ffn1_postnorm_bwd")
    r_wd1 = reduction('ffn1_w_down', _mm_tn(a1, dfo1, "ffn1_dw_down", carried=[r_win.exchange()]))
    dg1, du1 = _ffn_hidden_bwd(dfo1, wd1, g1, u1, "ffn1_hidden_bwd", carried=[r_wd1.swap()])
    r_wg1 = reduction('ffn1_w_gate', _mm_tn(dg1, n1, "ffn1_dw_gate", carried=[r_wd1.exchange()]))
    r_wu1 = reduction('ffn1_w_up', _mm_tn(du1, n1, "ffn1_dw_up", carried=[r_wg1.swap()]))
    row_tile = _norm_bwd_row_tile(m_rows, 2)
    n_tiles = m_rows // row_tile
    half = n_tiles // 2
    assert half >= 1 and half * row_tile >= lead
    dh0_a, d_pre1_a = _mm_norm_bwd([(dg1, wg1), (du1, wu1)], h0, ffn1_pre_g, dh1, "ffn1_dx_a",
                                   carried=[r_wg1.exchange(), r_wu1.swap()], row_tiles=(0, half))
    dh0_b, d_pre1 = _mm_norm_bwd([(dg1, wg1), (du1, wu1)], h0, ffn1_pre_g, dh1, "ffn1_dx_b",
                                 carried=[r_wu1.exchange()], row_tiles=(half, n_tiles - half), dg_init=d_pre1_a)
    grad_x = jnp.concatenate([dh0_a[lead:], dh0_b], axis=0)[None]
    d_meta = dh0_a[pad:lead]

    grads = {}
    for nme in big:
        gsum = red[nme].total()
        grads[nme] = (gsum.T if nme in col_sharded else gsum)[None]

    small_names = ['ffn1_pre_g', 'ffn1_post_g', 'mix_pre_g', 'mix_post_g', 'ffn2_pre_g', 'ffn2_post_g',
                   'lru_conv_b', 'lru_b_a', 'lru_b_x', 'lru_lambda', 'lru_out_g', 'sconv_out_g',
                   'lru_conv_w', 'sconv_w', 'lru_w_a', 'lru_w_x', 'meta_tokens']
    small_parts = [d_pre1, d_post1, d_mix_pre, d_mix_post, d_pre2, d_post2,
                   dpv[4:5], dpv[5:6], dpv[6:7], dpv[7:8], dpv[11:12], dpv[12:13],
                   dpv[0:4], dpv[8:11], _block_diag_extract(dwa_bd, hb), _block_diag_extract(dwx_bd, hb),
                   d_meta]
    small_shapes = [p.shape for p in small_parts]
    small_sum = _gather_small(_pack_rows(small_parts, d), True, "reduce_small")
    for nme, gsm in zip(small_names, _unpack_rows(small_sum, small_shapes)):
        if nme == 'meta_tokens':
            grads[nme] = lax.dynamic_slice_in_dim(gsm, me * dm, dm, axis=1)
        elif nme in ('lru_conv_w', 'sconv_w'):
            grads[nme] = lax.dynamic_slice_in_dim(gsm, me * cs_, cs_, axis=1)[None]
        else:
            grads[nme] = gsm.reshape(wts[nme].shape)

    delta, new_m, new_v = {}, {}, {}
    for nme in big:
        shp = wts[nme].shape
        dl, nm_, nv_ = _adamw(wts[nme][0], grads[nme][0], mom[nme][0], var[nme][0], "adamw_" + nme)
        delta[nme], new_m[nme], new_v[nme] = dl.reshape(shp), nm_.reshape(shp), nv_.reshape(shp)
    rest = [n for n in WEIGHT_NAMES if n not in big]
    rest_shapes = [wts[n].shape for n in rest]
    packed = [_pack_rows([src[n] for n in rest], LANE, 256) for src in (wts, grads, mom, var)]
    for out, packed_out in zip((delta, new_m, new_v), _adamw(*packed, "adamw_small")):
        for nme, arr in zip(rest, _unpack_rows(packed_out, rest_shapes)):
            out[nme] = arr

    return (loss, grad_x, *[grads[n] for n in WEIGHT_NAMES], *[delta[n] for n in WEIGHT_NAMES],
            *[new_m[n] for n in WEIGHT_NAMES], *[new_v[n] for n in WEIGHT_NAMES])
```

```python
import functools

import jax
import jax.numpy as jnp
from jax import lax
from jax.experimental import pallas as pl
from jax.experimental.pallas import tpu as pltpu

F32 = jnp.float32
MXU_DTYPE = jnp.bfloat16
WIRE_DTYPE = jnp.bfloat16
MESH = pl.DeviceIdType.MESH

EPS = 1e-6
LRU_C = 8.0
N_GROUPS = 16
ADAM_LR = 0.001
ADAM_B1 = 0.9
ADAM_B2 = 0.999
ADAM_EPS = 1e-08
ADAM_WD = 0.01
ADAM_STEP = 10

N_DEV = 8
LANE = 128
SUBLANE_BF16 = 16
ROW_ALIGN = 128
F_ALIGN = 512
BD = 256
K_TILE = 512
ACC_ROWS = 528
MIX_ROWS = 128
VMEM_LIMIT_MB = 56

WEIGHT_NAMES = ['meta_tokens', 'ffn1_pre_g', 'ffn1_w_gate', 'ffn1_w_up', 'ffn1_w_down', 'ffn1_post_g',
                'mix_pre_g', 'w_in', 'lru_conv_w', 'lru_conv_b', 'lru_w_a', 'lru_b_a', 'lru_w_x', 'lru_b_x',
                'lru_lambda', 'sconv_w', 'lru_out_g', 'sconv_out_g', 'w_out', 'mix_post_g', 'ffn2_pre_g',
                'ffn2_w_gate', 'ffn2_w_up', 'ffn2_w_down', 'ffn2_post_g']


def _round_up(n, q):
    return (n + q - 1) // q * q


def _tile(n, target, q):
    best = None
    t = q
    while t <= min(n, target):
        if n % t == 0:
            best = t
        t += q
    assert best is not None, (n, target, q)
    return best


def _params(**kw):
    return pltpu.CompilerParams(vmem_limit_bytes=VMEM_LIMIT_MB << 20, **kw)


def _call(body, *, grid, in_specs, out_specs, out_shape, name, args, scratch_shapes=(), carried=(), prefetch=()):
    carried = list(carried)
    n_pf = len(prefetch)

    def launch(fn, in_specs_, out_specs_, out_shape_, scratch_, operands, aliases_):
        if n_pf:
            spec = pltpu.PrefetchScalarGridSpec(num_scalar_prefetch=n_pf, grid=grid, in_specs=in_specs_,
                                                out_specs=out_specs_, scratch_shapes=scratch_)
            return pl.pallas_call(fn, grid_spec=spec, out_shape=out_shape_, input_output_aliases=aliases_,
                                  name=name, compiler_params=_params())(*prefetch, *operands)
        return pl.pallas_call(fn, grid=grid, in_specs=in_specs_, out_specs=out_specs_, out_shape=out_shape_,
                              scratch_shapes=scratch_, input_output_aliases=aliases_, name=name,
                              compiler_params=_params())(*operands)

    if not carried:
        return launch(body, in_specs, out_specs, out_shape, list(scratch_shapes), args, {})
    single = not isinstance(out_shape, (list, tuple))
    out_specs_l = [out_specs] if single else list(out_specs)
    out_shape_l = [out_shape] if single else list(out_shape)
    n_in, n_out, n_scr = len(in_specs), len(out_specs_l), len(scratch_shapes)
    hbm = pl.BlockSpec(memory_space=pl.ANY)
    c_in = [a for cm in carried for a in cm.arrays]
    c_out = [s for cm in carried for s in cm.out_shapes]
    c_scr = []
    aliases = {}
    in_off, out_off = n_pf + n_in, n_out
    for cm in carried:
        c_scr += [pltpu.SemaphoreType.DMA((cm.n_remote,)), pltpu.SemaphoreType.DMA((cm.n_remote,)),
                  pltpu.SemaphoreType.DMA((max(cm.n_local, 1),))]
        for k, v in cm.aliases.items():
            aliases[in_off + k] = out_off + v
        in_off += len(cm.arrays)
        out_off += len(cm.out_shapes)
    steps = 1
    for g in grid:
        steps *= g
    forward_steps = [min(int(cm.forward_at * steps), steps - 1) for cm in carried]

    def wrapped(*refs):
        pf = refs[:n_pf]
        p = n_pf
        ins = refs[p:p + n_in]
        p += n_in
        cins = refs[p:p + len(c_in)]
        p += len(c_in)
        outs = refs[p:p + n_out]
        p += n_out
        couts = refs[p:p + len(c_out)]
        p += len(c_out)
        scr = refs[p:p + n_scr]
        csem = refs[p + n_scr:]
        lin = 0
        for axis, g in enumerate(grid):
            lin = lin * g + pl.program_id(axis)
        views = []
        io = oo = 0
        for j, cm in enumerate(carried):
            views.append((cins[io:io + len(cm.arrays)], couts[oo:oo + len(cm.out_shapes)],
                          csem[3 * j], csem[3 * j + 1], csem[3 * j + 2]))
            io += len(cm.arrays)
            oo += len(cm.out_shapes)

        @pl.when(lin == 0)
        def _():
            for cm, v in zip(carried, views):
                cm.start(*v)

        body(*pf, *ins, *outs, *scr)

        for cm, v, step in zip(carried, views, forward_steps):
            pl.when(lin == step)(functools.partial(cm.forward, *v))

        @pl.when(lin == steps - 1)
        def _():
            for cm, v in zip(carried, views):
                cm.finish(*v)

    res = launch(wrapped, list(in_specs) + [hbm] * len(c_in), out_specs_l + [hbm] * len(c_out),
                 out_shape_l + c_out, list(scratch_shapes) + c_scr, (*args, *c_in), aliases)
    oo = n_out
    for cm in carried:
        cm.results = list(res[oo:oo + len(cm.out_shapes)])
        oo += len(cm.out_shapes)
    return res[0] if single else list(res[:n_out])


def _rmsnorm(h, g, name, carried=()):
    m, d = h.shape
    tm = _tile(m, 528, SUBLANE_BF16)

    def body(h_ref, g_ref, o_ref):
        x = h_ref[...]
        r = lax.rsqrt(jnp.mean(x * x, axis=-1, keepdims=True) + EPS)
        o_ref[...] = (x * r * g_ref[...]).astype(o_ref.dtype)

    return _call(
        body, grid=(m // tm,),
        in_specs=[pl.BlockSpec((tm, d), lambda i: (i, 0)), pl.BlockSpec((1, d), lambda i: (0, 0))],
        out_specs=pl.BlockSpec((tm, d), lambda i: (i, 0)),
        out_shape=jax.ShapeDtypeStruct((m, d), MXU_DTYPE), name=name, args=(h, g), carried=carried)


def _rmsnorm_bwd_rows(x, g, dy):
    r = lax.rsqrt(jnp.mean(x * x, axis=-1, keepdims=True) + EPS)
    xh = x * r
    dyh = dy * g
    dx = r * (dyh - xh * jnp.mean(dyh * xh, axis=-1, keepdims=True))
    return dx, dy * xh


def _norm_bwd(x, g, dy, scale, name, carried=()):
    m, d = x.shape
    tm = _tile(m, 528, SUBLANE_BF16)

    def body(x_ref, g_ref, dy_ref, dx_ref, dg_ref):
        @pl.when(pl.program_id(0) == 0)
        def _():
            dg_ref[...] = jnp.zeros_like(dg_ref)

        dx, dgr = _rmsnorm_bwd_rows(x_ref[...], g_ref[...], scale * dy_ref[...])
        dx_ref[...] = dx.astype(dx_ref.dtype)
        dg_ref[...] += jnp.sum(dgr, axis=0, keepdims=True)

    return _call(
        body, grid=(m // tm,),
        in_specs=[pl.BlockSpec((tm, d), lambda i: (i, 0)), pl.BlockSpec((1, d), lambda i: (0, 0)),
                  pl.BlockSpec((tm, d), lambda i: (i, 0))],
        out_specs=[pl.BlockSpec((tm, d), lambda i: (i, 0)), pl.BlockSpec((1, d), lambda i: (0, 0))],
        out_shape=[jax.ShapeDtypeStruct((m, d), MXU_DTYPE), jax.ShapeDtypeStruct((1, d), F32)],
        name=name, args=(x, g, dy), carried=carried)


def _dot_nt(a, b):
    return lax.dot_general(a, b, (((1,), (1,)), ((), ())), preferred_element_type=F32)


def _dot_tn(a, b):
    return lax.dot_general(a, b, (((0,), (0,)), ((), ())), preferred_element_type=F32)


def _mm_nt(a, w, name, carried=()):
    m, k = a.shape
    n = w.shape[0]
    tm = _tile(m, 1056, SUBLANE_BF16)
    tn = _tile(n, 512, LANE)

    def body(a_ref, w_ref, o_ref):
        o_ref[...] = _dot_nt(a_ref[...], w_ref[...])

    return _call(
        body, grid=(m // tm, n // tn),
        in_specs=[pl.BlockSpec((tm, k), lambda i, j: (i, 0)), pl.BlockSpec((tn, k), lambda i, j: (j, 0))],
        out_specs=pl.BlockSpec((tm, tn), lambda i, j: (i, j)),
        out_shape=jax.ShapeDtypeStruct((m, n), F32), name=name, args=(a, w), carried=carried)


def _ffn_gate_up(n_act, wg_t, wu_t, name, carried=()):
    m, d = n_act.shape
    fp = wg_t.shape[0]
    tm = _tile(m, 1056, SUBLANE_BF16)
    tn = _tile(fp, 512, LANE)

    def body(n_ref, wg_ref, wu_ref, g_ref, u_ref, a_ref):
        n = n_ref[...]
        g = _dot_nt(n, wg_ref[...])
        u = _dot_nt(n, wu_ref[...])
        g_ref[...] = g.astype(g_ref.dtype)
        u_ref[...] = u.astype(u_ref.dtype)
        a_ref[...] = (g * jax.nn.sigmoid(g) * u).astype(a_ref.dtype)

    act = pl.BlockSpec((tm, tn), lambda i, j: (i, j))
    wsp = pl.BlockSpec((tn, d), lambda i, j: (j, 0))
    return _call(
        body, grid=(m // tm, fp // tn),
        in_specs=[pl.BlockSpec((tm, d), lambda i, j: (i, 0)), wsp, wsp],
        out_specs=[act, act, act],
        out_shape=[jax.ShapeDtypeStruct((m, fp), MXU_DTYPE)] * 3, name=name, args=(n_act, wg_t, wu_t), carried=carried)


def _ffn_hidden_bwd(dfo, wd, g_act, u_act, name, carried=()):
    m, d = dfo.shape
    fp = wd.shape[0]
    tm = _tile(m, 1056, SUBLANE_BF16)
    tn = _tile(fp, 512, LANE)

    def body(df_ref, wd_ref, g_ref, u_ref, dg_ref, du_ref):
        da = _dot_nt(df_ref[...], wd_ref[...])
        g = g_ref[...].astype(F32)
        u = u_ref[...].astype(F32)
        s = jax.nn.sigmoid(g)
        du_ref[...] = (da * (g * s)).astype(du_ref.dtype)
        dg_ref[...] = (da * u * (s * (1.0 + g * (1.0 - s)))).astype(dg_ref.dtype)

    act = pl.BlockSpec((tm, tn), lambda i, j: (i, j))
    return _call(
        body, grid=(m // tm, fp // tn),
        in_specs=[pl.BlockSpec((tm, d), lambda i, j: (i, 0)), pl.BlockSpec((tn, d), lambda i, j: (j, 0)), act, act],
        out_specs=[act, act],
        out_shape=[jax.ShapeDtypeStruct((m, fp), MXU_DTYPE)] * 2, name=name, args=(dfo, wd, g_act, u_act),
        carried=carried)


def _mm_residual_norm(a, w, h, g, scale, next_g, name, carried=()):
    m, k = a.shape
    d = w.shape[1]
    tm = _tile(m, ACC_ROWS, SUBLANE_BF16)
    tk = _tile(k, K_TILE, LANE)
    nk = k // tk

    def body(a_ref, w_ref, h_ref, g_ref, ng_ref, fo_ref, hn_ref, nn_ref, acc_ref):
        kk = pl.program_id(1)

        @pl.when(kk == 0)
        def _():
            acc_ref[...] = jnp.zeros_like(acc_ref)

        acc_ref[...] += jnp.dot(a_ref[...], w_ref[...], preferred_element_type=F32)

        @pl.when(kk == nk - 1)
        def _():
            fo = acc_ref[...]
            fo_ref[...] = fo
            r = lax.rsqrt(jnp.mean(fo * fo, axis=-1, keepdims=True) + EPS)
            hn = h_ref[...] + scale * (fo * r * g_ref[...])
            hn_ref[...] = hn
            rn = lax.rsqrt(jnp.mean(hn * hn, axis=-1, keepdims=True) + EPS)
            nn_ref[...] = (hn * rn * ng_ref[...]).astype(nn_ref.dtype)

    row = pl.BlockSpec((tm, d), lambda i, kk: (i, 0))
    row_once = pl.BlockSpec((tm, d), lambda i, kk: (i, 0), pipeline_mode=pl.Buffered(1))
    vec = pl.BlockSpec((1, d), lambda i, kk: (0, 0))
    return _call(
        body, grid=(m // tm, nk),
        in_specs=[pl.BlockSpec((tm, tk), lambda i, kk: (i, kk)), pl.BlockSpec((tk, d), lambda i, kk: (kk, 0)),
                  row_once, vec, vec],
        out_specs=[row, row, row],
        out_shape=[jax.ShapeDtypeStruct((m, d), F32)] * 2 + [jax.ShapeDtypeStruct((m, d), MXU_DTYPE)],
        scratch_shapes=[pltpu.VMEM((tm, d), F32)], name=name, args=(a, w, h, g, next_g), carried=carried)


def _mm_residual_loss(a, w, h, g, scale, target, lead, name, carried=()):
    m, k = a.shape
    d = w.shape[1]
    tm = _tile(m, ACC_ROWS, SUBLANE_BF16)
    tk = _tile(k, K_TILE, LANE)
    nk = k // tk

    def body(a_ref, w_ref, h_ref, g_ref, t_ref, dy_ref, dfo_ref, dg_ref, l_ref, acc_ref):
        i = pl.program_id(0)
        kk = pl.program_id(1)

        @pl.when(jnp.logical_and(i == 0, kk == 0))
        def _():
            dg_ref[...] = jnp.zeros_like(dg_ref)
            l_ref[...] = jnp.zeros_like(l_ref)

        @pl.when(kk == 0)
        def _():
            acc_ref[...] = jnp.zeros_like(acc_ref)

        acc_ref[...] += jnp.dot(a_ref[...], w_ref[...], preferred_element_type=F32)

        @pl.when(kk == nk - 1)
        def _():
            fo = acc_ref[...]
            gain = g_ref[...]
            r = lax.rsqrt(jnp.mean(fo * fo, axis=-1, keepdims=True) + EPS)
            xh = fo * r
            y = h_ref[...] + scale * (xh * gain)
            row = i * tm + lax.broadcasted_iota(jnp.int32, (tm, 1), 0)
            e = jnp.where(row >= lead, y - t_ref[...], 0.0)
            dy = e * (1.0 / d)
            dy_ref[...] = dy
            l_ref[...] += 0.5 * jnp.sum(jnp.sum(e * e, axis=-1, keepdims=True) * (1.0 / d), axis=0, keepdims=True)
            dn = scale * dy
            dyh = dn * gain
            dfo_ref[...] = (r * (dyh - xh * jnp.mean(dyh * xh, axis=-1, keepdims=True))).astype(dfo_ref.dtype)
            dg_ref[...] += jnp.sum(dn * xh, axis=0, keepdims=True)

    row = pl.BlockSpec((tm, d), lambda i, kk: (i, 0))
    row_once = pl.BlockSpec((tm, d), lambda i, kk: (i, 0), pipeline_mode=pl.Buffered(1))
    vec = pl.BlockSpec((1, d), lambda i, kk: (0, 0))
    return _call(
        body, grid=(m // tm, nk),
        in_specs=[pl.BlockSpec((tm, tk), lambda i, kk: (i, kk)), pl.BlockSpec((tk, d), lambda i, kk: (kk, 0)),
                  row_once, vec, row_once],
        out_specs=[row, row, vec, pl.BlockSpec((1, 1), lambda i, kk: (0, 0))],
        out_shape=[jax.ShapeDtypeStruct((m, d), F32), jax.ShapeDtypeStruct((m, d), MXU_DTYPE),
                   jax.ShapeDtypeStruct((1, d), F32), jax.ShapeDtypeStruct((1, 1), F32)],
        scratch_shapes=[pltpu.VMEM((tm, d), F32)], name=name, args=(a, w, h, g, target), carried=carried)


def _norm_bwd_row_tile(m):
    return _tile(m, ACC_ROWS, SUBLANE_BF16)


def _mm_norm_bwd(pairs, h, g, dh_up, name, carried=(), row_tiles=None, dg_init=None):
    n_pairs = len(pairs)
    m, k = pairs[0][0].shape
    d = h.shape[1]
    tm = _norm_bwd_row_tile(m)
    tk = _tile(k, K_TILE, LANE)
    nk = k // tk
    t0, nt = row_tiles if row_tiles is not None else (0, m // tm)
    if dg_init is None:
        dg_init = jnp.zeros((1, d), F32)

    def body(*refs):
        ops = refs[:2 * n_pairs]
        h_ref, g_ref, up_ref, init_ref, dh_ref, dg_ref, acc_ref = refs[2 * n_pairs:]
        i = pl.program_id(0)
        kk = pl.program_id(1)

        @pl.when(jnp.logical_and(i == 0, kk == 0))
        def _():
            dg_ref[...] = init_ref[...]

        @pl.when(kk == 0)
        def _():
            acc_ref[...] = jnp.zeros_like(acc_ref)

        for p in range(n_pairs):
            acc_ref[...] += jnp.dot(ops[2 * p][...], ops[2 * p + 1][...], preferred_element_type=F32)

        @pl.when(kk == nk - 1)
        def _():
            dx, dgr = _rmsnorm_bwd_rows(h_ref[...], g_ref[...], acc_ref[...])
            dh_ref[...] = up_ref[...] + dx
            dg_ref[...] += jnp.sum(dgr, axis=0, keepdims=True)

    row_in = pl.BlockSpec((tm, d), lambda i, kk: (t0 + i, 0))
    vec = pl.BlockSpec((1, d), lambda i, kk: (0, 0))
    in_specs = []
    args = []
    for a, w in pairs:
        in_specs += [pl.BlockSpec((tm, tk), lambda i, kk: (t0 + i, kk)), pl.BlockSpec((tk, d), lambda i, kk: (kk, 0))]
        args += [a, w]
    return _call(
        body, grid=(nt, nk),
        in_specs=in_specs + [row_in, vec, row_in, vec], out_specs=[pl.BlockSpec((tm, d), lambda i, kk: (i, 0)), vec],
        out_shape=[jax.ShapeDtypeStruct((nt * tm, d), F32), jax.ShapeDtypeStruct((1, d), F32)],
        scratch_shapes=[pltpu.VMEM((tm, d), F32)], name=name, args=(*args, h, g, dh_up, dg_init), carried=carried)


def _mm_tn(a, b, name, carried=()):
    m, ka = a.shape
    d = b.shape[1]
    tf = _tile(ka, 512, LANE)

    def body(a_ref, b_ref, o_ref):
        o_ref[...] = _dot_tn(a_ref[...], b_ref[...]).astype(o_ref.dtype)

    return _call(
        body, grid=(ka // tf,),
        in_specs=[pl.BlockSpec((m, tf), lambda j: (0, j)),
                  pl.BlockSpec((m, d), lambda j: (0, 0), pipeline_mode=pl.Buffered(1))],
        out_specs=pl.BlockSpec((tf, d), lambda j: (j, 0)),
        out_shape=jax.ShapeDtypeStruct((ka, d), WIRE_DTYPE), name=name, args=(a, b), carried=carried)


GELU_K = 0.7978845608028654
GELU_C = 0.044715


def _expm1(x):
    series = x * (1.0 + x * (1.0 / 2 + x * (1.0 / 6 + x * (1.0 / 24 + x * (1.0 / 120 + x * (1.0 / 720 + x * (1.0 / 5040)))))))
    return jnp.where(jnp.abs(x) < 0.3, series, jnp.exp(x) - 1.0)


def _softplus(x):
    return jnp.maximum(x, 0.0) + jnp.log1p(jnp.exp(-jnp.abs(x)))


def _block_mm(v, w_ref, transposed):
    nbk = w_ref.shape[0]
    outs = []
    for j in range(nbk):
        vj = v[:, j * BD:(j + 1) * BD]
        outs.append(_dot_nt(vj, w_ref[j]) if transposed else jnp.dot(vj, w_ref[j], preferred_element_type=F32))
    return outs[0] if nbk == 1 else jnp.concatenate(outs, axis=1)


def _group_mean(q, gm_ref):
    hi = q.astype(MXU_DTYPE)
    lo = (q - hi.astype(F32)).astype(MXU_DTYPE)
    nbk = q.shape[1] // BD
    gm = gm_ref[...]
    outs = []
    for j in range(nbk):
        sl = slice(j * BD, (j + 1) * BD)
        outs.append(jnp.dot(hi[:, sl], gm, preferred_element_type=F32) + jnp.dot(lo[:, sl], gm, preferred_element_type=F32))
    return outs[0] if nbk == 1 else jnp.concatenate(outs, axis=1)


class _RowReader:
    def __init__(self, ref):
        self.ref = ref

    def __getitem__(self, rows):
        return self.ref[rows, :]


def _shifted(ext_ref, cur, before8, after8, downs=(), ups=()):
    r = cur.shape[0]
    if downs:
        ext_ref[0:8, :] = before8
    ext_ref[8:8 + r, :] = cur
    if ups:
        ext_ref[8 + r:16 + r, :] = after8
    return [ext_ref[pl.ds(8 - j, r), :] for j in downs] + [ext_ref[pl.ds(8 + j, r), :] for j in ups]


def _lru_gates(xc, pv, wa_ref, wx_ref):
    xcb = xc.astype(MXU_DTYPE)
    ga = jax.nn.sigmoid(_block_mm(xcb, wa_ref, False) + pv[5:6])
    gx = jax.nn.sigmoid(_block_mm(xcb, wx_ref, False) + pv[6:7])
    sp = _softplus(-pv[7:8])
    log_a = -LRU_C * ga * sp
    a = jnp.exp(log_a)
    e2 = _expm1(2.0 * log_a)
    mult = jnp.sqrt(-e2)
    return xcb, ga, gx, sp, a, e2, mult


def _gelu_parts(y):
    th = jnp.tanh(GELU_K * (y + GELU_C * y * y * y))
    return 0.5 * y * (1.0 + th), th


def _mixer_fwd(z, pv, wa, wx, gm, pad, name, carried=()):
    m = z.shape[0]
    c = pv.shape[1]
    r = MIX_ROWS
    nb = m // r

    def body(z_ref, pv_ref, wa_ref, wx_ref, gm_ref, mixed_ref, hs_ref, ext_ref, tailx_ref, tailc_ref, carry_ref):
        b = pl.program_id(0)

        @pl.when(b == 0)
        def _():
            tailx_ref[...] = jnp.zeros_like(tailx_ref)
            tailc_ref[...] = jnp.zeros_like(tailc_ref)
            carry_ref[...] = jnp.zeros_like(carry_ref)

        pv = _RowReader(pv_ref)
        row = b * r + lax.broadcasted_iota(jnp.int32, (r, 1), 0)
        lrow = lax.broadcasted_iota(jnp.int32, (r, c), 0)
        maskf = (row >= pad).astype(F32)
        y = z_ref[:, 0:c]
        xl = z_ref[:, c:2 * c]
        bs = z_ref[:, 2 * c:3 * c]
        cv = z_ref[:, 3 * c:4 * c] * z_ref[:, 4 * c:5 * c]

        x1, x2, x3 = _shifted(ext_ref, xl, tailx_ref[...], None, downs=(1, 2, 3))
        tailx_ref[...] = z_ref[pl.ds(r - 8, 8), c:2 * c]
        xc = pv[4:5] + pv[3:4] * xl + pv[2:3] * x1 + pv[1:2] * x2 + pv[0:1] * x3
        _, _, gx, _, a, _, mult = _lru_gates(xc, pv, wa_ref, wx_ref)
        uu = mult * (gx * xc) * maskf

        acc_a = a
        acc_h = uu
        dlt = 1
        while dlt < r:
            keep = lrow >= dlt
            sh_a = pltpu.roll(acc_a, dlt, axis=0)
            sh_h = pltpu.roll(acc_h, dlt, axis=0)
            acc_h = acc_h + acc_a * jnp.where(keep, sh_h, 0.0)
            acc_a = acc_a * jnp.where(keep, sh_a, 1.0)
            dlt *= 2
        hs = acc_h + acc_a * carry_ref[...]
        hs_ref[...] = hs
        carry_ref[...] = hs_ref[pl.ds(r - 1, 1), :]

        gelu_y, _ = _gelu_parts(y)
        lru_out = hs * gelu_y
        c1, c2 = _shifted(ext_ref, cv, tailc_ref[...], None, downs=(1, 2))
        tailc_ref[...] = cv[r - 8:r]
        sc_out = bs * (pv[10:11] * cv + pv[9:10] * c1 + pv[8:9] * c2)

        rl = lax.rsqrt(_group_mean(lru_out * lru_out, gm_ref) + EPS)
        rs = lax.rsqrt(_group_mean(sc_out * sc_out, gm_ref) + EPS)
        mixed_ref[:, 0:c] = (lru_out * rl * pv[11:12]).astype(mixed_ref.dtype)
        mixed_ref[:, c:2 * c] = (sc_out * rs * pv[12:13]).astype(mixed_ref.dtype)

    full = lambda shape: pl.BlockSpec(shape, lambda b: (0,) * len(shape))
    return _call(
        body, grid=(nb,),
        in_specs=[pl.BlockSpec((r, 5 * c), lambda b: (b, 0)), full(pv.shape), full(wa.shape), full(wx.shape), full(gm.shape)],
        out_specs=[pl.BlockSpec((r, 2 * c), lambda b: (b, 0)), pl.BlockSpec((r, c), lambda b: (b, 0))],
        out_shape=[jax.ShapeDtypeStruct((m, 2 * c), MXU_DTYPE), jax.ShapeDtypeStruct((m, c), F32)],
        scratch_shapes=[pltpu.VMEM((r + 16, c), F32), pltpu.VMEM((8, c), F32), pltpu.VMEM((8, c), F32),
                        pltpu.VMEM((1, c), F32)],
        name=name, args=(z, pv, wa, wx, gm), carried=carried)


def _mixer_bwd(z, hs, dmixed, pv, wa, wx, gm, pad, name, carried=()):
    m = z.shape[0]
    c = pv.shape[1]
    r = MIX_ROWS
    nb = m // r
    r8 = r // 8

    def body(z_ref, zp_ref, hs_ref, hsp_ref, dm_ref, pv_ref, wa_ref, wx_ref, gm_ref,
             dz_ref, dpv_ref, dwa_ref, dwx_ref, ext_ref, hxc_ref, hsc_ref, hp_ref):
        i = pl.program_id(0)
        b = nb - 1 - i

        @pl.when(i == 0)
        def _():
            hxc_ref[...] = jnp.zeros_like(hxc_ref)
            hsc_ref[...] = jnp.zeros_like(hsc_ref)
            hp_ref[...] = jnp.zeros_like(hp_ref)
            dpv_ref[...] = jnp.zeros_like(dpv_ref)
            dwa_ref[...] = jnp.zeros_like(dwa_ref)
            dwx_ref[...] = jnp.zeros_like(dwx_ref)

        pv = _RowReader(pv_ref)
        row = b * r + lax.broadcasted_iota(jnp.int32, (r, 1), 0)
        lrow = lax.broadcasted_iota(jnp.int32, (r, c), 0)
        maskf = (row >= pad).astype(F32)
        has_prev = (b > 0).astype(F32)
        y = z_ref[:, 0:c]
        xl = z_ref[:, c:2 * c]
        bs = z_ref[:, 2 * c:3 * c]
        cs = z_ref[:, 3 * c:4 * c]
        vs = z_ref[:, 4 * c:5 * c]
        cv = cs * vs
        xl_prev = zp_ref[:, c:2 * c] * has_prev
        cv_prev = zp_ref[:, 3 * c:4 * c] * zp_ref[:, 4 * c:5 * c] * has_prev
        hs = hs_ref[...]

        x1, x2, x3 = _shifted(ext_ref, xl, xl_prev, None, downs=(1, 2, 3))
        xc = pv[4:5] + pv[3:4] * xl + pv[2:3] * x1 + pv[1:2] * x2 + pv[0:1] * x3
        xcb, ga, gx, sp, a, e2, mult = _lru_gates(xc, pv, wa_ref, wx_ref)
        gxx = gx * xc
        gelu_y, th = _gelu_parts(y)
        lru_out = hs * gelu_y
        c1, c2 = _shifted(ext_ref, cv, cv_prev, None, downs=(1, 2))
        sc = pv[10:11] * cv + pv[9:10] * c1 + pv[8:9] * c2
        sc_out = bs * sc

        def group_norm_bwd(v, dm, gain):
            rr = lax.rsqrt(_group_mean(v * v, gm_ref) + EPS)
            vh = v * rr
            dvh = dm * gain
            dv = rr * (dvh - vh * _group_mean(dvh * vh, gm_ref))
            return dv, jnp.sum(dm * vh, axis=0, keepdims=True)

        d_lru_out, d_og = group_norm_bwd(lru_out, dm_ref[:, 0:c], pv[11:12])
        d_sc_out, d_sg = group_norm_bwd(sc_out, dm_ref[:, c:2 * c], pv[12:13])
        dpv_ref[11:12, :] += d_og
        dpv_ref[12:13, :] += d_sg

        dhs = d_lru_out * gelu_y
        dgelu = 0.5 * (1.0 + th) + 0.5 * y * (1.0 - th * th) * GELU_K * (1.0 + 3.0 * GELU_C * y * y)
        dy = d_lru_out * hs * dgelu

        acc_a = a
        acc_p = a * dhs
        dlt = 1
        while dlt < r:
            keep = lrow < r - dlt
            sh_a = pltpu.roll(acc_a, r - dlt, axis=0)
            sh_p = pltpu.roll(acc_p, r - dlt, axis=0)
            acc_p = acc_p + acc_a * jnp.where(keep, sh_p, 0.0)
            acc_a = acc_a * jnp.where(keep, sh_a, 1.0)
            dlt *= 2
        p_all = acc_p + acc_a * hp_ref[0:1, :]
        (p_next,) = _shifted(ext_ref, p_all, None, hp_ref[...], ups=(1,))
        hp_ref[...] = p_all[0:8]
        q = dhs + p_next
        (hs_prev,) = _shifted(ext_ref, hs, hsp_ref[...] * has_prev, None, downs=(1,))
        duu = q * maskf
        da = q * hs_prev

        dmult = duu * gxx
        dgxx = duu * mult
        dgx = dgxx * xc
        dxc = dgxx * gx
        dlog_a = da * a - dmult * ((1.0 + e2) / mult)
        dga = dlog_a * (-LRU_C * sp)
        dsp = jnp.sum(dlog_a * (-LRU_C * ga), axis=0, keepdims=True)
        dpv_ref[7:8, :] += dsp * (-jax.nn.sigmoid(-pv[7:8]))
        dga_pre = dga * ga * (1.0 - ga)
        dgx_pre = dgx * gx * (1.0 - gx)
        dpv_ref[5:6, :] += jnp.sum(dga_pre, axis=0, keepdims=True)
        dpv_ref[6:7, :] += jnp.sum(dgx_pre, axis=0, keepdims=True)
        dga_b = dga_pre.astype(MXU_DTYPE)
        dgx_b = dgx_pre.astype(MXU_DTYPE)
        dxc = dxc + _block_mm(dga_b, wa_ref, True) + _block_mm(dgx_b, wx_ref, True)
        for j in range(c // BD):
            sl = slice(j * BD, (j + 1) * BD)
            dwa_ref[j] += _dot_tn(xcb[:, sl], dga_b[:, sl])
            dwx_ref[j] += _dot_tn(xcb[:, sl], dgx_b[:, sl])

        dpv_ref[4:5, :] += jnp.sum(dxc, axis=0, keepdims=True)
        dpv_ref[3:4, :] += jnp.sum(dxc * xl, axis=0, keepdims=True)
        dpv_ref[2:3, :] += jnp.sum(dxc * x1, axis=0, keepdims=True)
        dpv_ref[1:2, :] += jnp.sum(dxc * x2, axis=0, keepdims=True)
        dpv_ref[0:1, :] += jnp.sum(dxc * x3, axis=0, keepdims=True)
        u1, u2, u3 = _shifted(ext_ref, dxc, None, hxc_ref[...], ups=(1, 2, 3))
        hxc_ref[...] = dxc[0:8]
        dxl = pv[3:4] * dxc + pv[2:3] * u1 + pv[1:2] * u2 + pv[0:1] * u3

        dbs = d_sc_out * sc
        dsc = d_sc_out * bs
        dpv_ref[10:11, :] += jnp.sum(dsc * cv, axis=0, keepdims=True)
        dpv_ref[9:10, :] += jnp.sum(dsc * c1, axis=0, keepdims=True)
        dpv_ref[8:9, :] += jnp.sum(dsc * c2, axis=0, keepdims=True)
        s1, s2 = _shifted(ext_ref, dsc, None, hsc_ref[...], ups=(1, 2))
        hsc_ref[...] = dsc[0:8]
        dcv = pv[10:11] * dsc + pv[9:10] * s1 + pv[8:9] * s2

        dz_ref[:, 0:c] = (dy * maskf).astype(dz_ref.dtype)
        dz_ref[:, c:2 * c] = (dxl * maskf).astype(dz_ref.dtype)
        dz_ref[:, 2 * c:3 * c] = (dbs * maskf).astype(dz_ref.dtype)
        dz_ref[:, 3 * c:4 * c] = (dcv * vs * maskf).astype(dz_ref.dtype)
        dz_ref[:, 4 * c:5 * c] = (dcv * cs * maskf).astype(dz_ref.dtype)

    full = lambda shape: pl.BlockSpec(shape, lambda i: (0,) * len(shape))
    cur = lambda width: pl.BlockSpec((r, width), lambda i: (nb - 1 - i, 0))
    prev8 = lambda width: pl.BlockSpec((8, width), lambda i: (jnp.maximum((nb - 1 - i) * r8 - 1, 0), 0))
    return _call(
        body, grid=(nb,),
        in_specs=[cur(5 * c), prev8(5 * c), cur(c), prev8(c), cur(2 * c),
                  full(pv.shape), full(wa.shape), full(wx.shape), full(gm.shape)],
        out_specs=[cur(5 * c), full(pv.shape), full(wa.shape), full(wx.shape)],
        out_shape=[jax.ShapeDtypeStruct((m, 5 * c), MXU_DTYPE), jax.ShapeDtypeStruct(pv.shape, F32),
                   jax.ShapeDtypeStruct(wa.shape, F32), jax.ShapeDtypeStruct(wx.shape, F32)],
        scratch_shapes=[pltpu.VMEM((r + 16, c), F32), pltpu.VMEM((8, c), F32), pltpu.VMEM((8, c), F32),
                        pltpu.VMEM((8, c), F32)],
        name=name, args=(z, z, hs, hs, dmixed, pv, wa, wx, gm), carried=carried)


def _position():
    return lax.axis_index("x"), lax.axis_index("y"), lax.axis_index("c")


def _block_of(px, py, pc):
    return 4 * px + 2 * py + pc


class _TwoLevelGather:
    def __init__(self, n_arrays, rows_of, src_of, send_sems, recv_sems):
        x, y, c = _position()
        self.n, self.rows_of, self.src_of = n_arrays, rows_of, src_of
        self.send_sems, self.recv_sems = send_sems, recv_sems
        self.c, self.me, self.sibling = c, (x, y, c), (x, y, 1 - c)
        self.chips = [(1 - x, y), (x, 1 - y), (1 - x, 1 - y)]

    def _copy(self, i, k, block, to, src=None):
        return pltpu.make_async_remote_copy(
            src_ref=self.rows_of(i, *block) if src is None else src, dst_ref=self.rows_of(i, *block),
            send_sem=self.send_sems.at[7 * i + k], recv_sem=self.recv_sems.at[7 * i + k],
            device_id=to, device_id_type=MESH)

    def _first(self, i):
        own = [self._copy(i, 0, self.me, self.sibling, src=self.src_of(i))]
        return own + [self._copy(i, 1 + j, self.me, (*chip, self.c), src=self.src_of(i))
                      for j, chip in enumerate(self.chips)]

    def _passed(self, i, j):
        return self._copy(i, 4 + j, (*self.chips[j], self.c), self.sibling)

    def start(self):
        for i in range(self.n):
            for cp in self._first(i):
                cp.start()

    def forward(self):
        for i in range(self.n):
            for j, chip in enumerate(self.chips):
                self._copy(i, 1 + j, (*chip, self.c), self.me).wait_recv()
                self._passed(i, j).start()

    def drain(self):
        for i in range(self.n):
            self._copy(i, 0, self.sibling, self.me).wait_recv()
            for j, chip in enumerate(self.chips):
                self._copy(i, 4 + j, (*chip, 1 - self.c), self.me).wait_recv()
        for i in range(self.n):
            for cp in self._first(i) + [self._passed(i, j) for j in range(3)]:
                cp.wait_send()


class _CarriedGather:
    def __init__(self, shards, padded_rows, zeros, forward_at):
        d = shards[0].shape[1]
        self.forward_at = forward_at
        self.n = len(shards)
        self.rows = [s.shape[0] for s in shards]
        self.pads = [p - N_DEV * r for r, p in zip(self.rows, padded_rows)]
        assert max(self.pads) <= zeros.shape[0] and zeros.shape[1] == d
        self.arrays = list(shards) + [zeros]
        self.out_shapes = [jax.ShapeDtypeStruct((p, d), s.dtype) for s, p in zip(shards, padded_rows)]
        self.aliases = {}
        self.n_remote, self.n_local = 7 * self.n, 2 * self.n
        self.results = None

    def _rows_of(self, outs):
        def rows_of(i, px, py, pc):
            s = self.rows[i]
            return outs[i].at[pl.ds(pl.multiple_of(_block_of(px, py, pc) * s, SUBLANE_BF16), s), :]
        return rows_of

    def _gather(self, ins, outs, send_sems, recv_sems):
        return _TwoLevelGather(self.n, self._rows_of(outs), lambda i: ins[i], send_sems, recv_sems)

    def _local(self, ins, outs, local_sems):
        x, y, c = _position()
        rows_of = self._rows_of(outs)
        cps = []
        for i in range(self.n):
            cps.append(pltpu.make_async_copy(ins[i], rows_of(i, x, y, c), local_sems.at[2 * i]))
            if self.pads[i]:
                cps.append(pltpu.make_async_copy(ins[self.n].at[pl.ds(0, self.pads[i]), :],
                                                 outs[i].at[pl.ds(N_DEV * self.rows[i], self.pads[i]), :],
                                                 local_sems.at[2 * i + 1]))
        return cps

    def start(self, ins, outs, send_sems, recv_sems, local_sems):
        for cp in self._local(ins, outs, local_sems):
            cp.start()
        self._gather(ins, outs, send_sems, recv_sems).start()

    def forward(self, ins, outs, send_sems, recv_sems, local_sems):
        self._gather(ins, outs, send_sems, recv_sems).forward()

    def finish(self, ins, outs, send_sems, recv_sems, local_sems):
        self._gather(ins, outs, send_sems, recv_sems).drain()
        for cp in self._local(ins, outs, local_sems):
            cp.wait()


class _CarriedSwap:
    def __init__(self, grads, shard_rows):
        d = grads[0].shape[1]
        self.n, self.rows = len(grads), list(shard_rows)
        self.arrays = list(grads)
        self.out_shapes = [jax.ShapeDtypeStruct((4, s, d), g.dtype) for g, s in zip(grads, shard_rows)]
        self.aliases = {}
        self.n_remote, self.n_local = 4 * self.n, 0
        self.forward_at = 1.0
        self.results = None

    def _copies(self, ins, outs, send_sems, recv_sems):
        x, y, c = _position()
        cps = []
        for i in range(self.n):
            s = self.rows[i]
            for k in range(4):
                blk = _block_of(k >> 1, k & 1, 1 - c)
                cps.append(pltpu.make_async_remote_copy(
                    src_ref=ins[i].at[pl.ds(pl.multiple_of(blk * s, SUBLANE_BF16), s), :], dst_ref=outs[i].at[k],
                    send_sem=send_sems.at[4 * i + k], recv_sem=recv_sems.at[4 * i + k],
                    device_id=(x, y, 1 - c), device_id_type=MESH))
        return cps

    def start(self, ins, outs, send_sems, recv_sems, local_sems):
        for cp in self._copies(ins, outs, send_sems, recv_sems):
            cp.start()

    def forward(self, *_):
        pass

    def finish(self, ins, outs, send_sems, recv_sems, local_sems):
        for cp in self._copies(ins, outs, send_sems, recv_sems):
            cp.wait()


class _CarriedChipExchange:
    def __init__(self, presums):
        self.n = len(presums)
        self.arrays = list(presums)
        self.out_shapes = [jax.ShapeDtypeStruct(p.shape, p.dtype) for p in presums]
        self.aliases = {}
        self.n_remote, self.n_local = 3 * self.n, 0
        self.forward_at = 1.0
        self.results = None

    def _copies(self, ins, outs, send_sems, recv_sems):
        x, y, c = _position()
        cps = []
        for i in range(self.n):
            for r in range(1, 4):
                cps.append(pltpu.make_async_remote_copy(
                    src_ref=ins[i].at[r - 1], dst_ref=outs[i].at[r - 1],
                    send_sem=send_sems.at[3 * i + r - 1], recv_sem=recv_sems.at[3 * i + r - 1],
                    device_id=(x ^ (r >> 1), y ^ (r & 1), c), device_id_type=MESH))
        return cps

    def start(self, ins, outs, send_sems, recv_sems, local_sems):
        for cp in self._copies(ins, outs, send_sems, recv_sems):
            cp.start()

    def forward(self, *_):
        pass

    def finish(self, ins, outs, send_sems, recv_sems, local_sems):
        for cp in self._copies(ins, outs, send_sems, recv_sems):
            cp.wait()


def _gather_small(block, reduce, name):
    rr, nn = block.shape

    def body(x_ref, out_ref, *rest):
        if reduce:
            stack_ref, send_sems, recv_sems, local_sem = rest
        else:
            send_sems, recv_sems, local_sem = rest
            stack_ref = out_ref
        x, y, c = _position()

        def rows_of(i, px, py, pc):
            return stack_ref.at[pl.ds(pl.multiple_of(_block_of(px, py, pc) * rr, 8), rr), :]

        own = pltpu.make_async_copy(x_ref, rows_of(0, x, y, c), local_sem)
        own.start()
        gather = _TwoLevelGather(1, rows_of, lambda i: x_ref, send_sems, recv_sems)
        gather.start()
        gather.forward()
        gather.drain()
        own.wait()
        if reduce:
            acc = stack_ref[0:rr, :]
            for k in range(1, N_DEV):
                acc = acc + stack_ref[k * rr:(k + 1) * rr, :]
            out_ref[...] = acc

    vmem = pl.BlockSpec(memory_space=pltpu.VMEM)
    scratch = [pltpu.SemaphoreType.DMA((7,)), pltpu.SemaphoreType.DMA((7,)), pltpu.SemaphoreType.DMA]
    if reduce:
        scratch = [pltpu.VMEM((N_DEV * rr, nn), F32)] + scratch
    out_rows = rr if reduce else N_DEV * rr
    return pl.pallas_call(
        body, in_specs=[vmem], out_specs=vmem, out_shape=jax.ShapeDtypeStruct((out_rows, nn), F32),
        scratch_shapes=scratch, name=name, compiler_params=_params())(block)


def _presum(where, grad, swapped, name):
    s, d = swapped.shape[1], swapped.shape[2]
    tc = _tile(d, 512, LANE)

    def body(where_ref, g_ref, sw_ref, o_ref):
        o_ref[0] = (g_ref[...].astype(F32) + sw_ref[0].astype(F32)).astype(o_ref.dtype)

    return _call(
        body, grid=(3, d // tc),
        in_specs=[pl.BlockSpec((s, tc), lambda r, j, where: (where[1 + r], j)),
                  pl.BlockSpec((1, s, tc), lambda r, j, where: (where[5 + r], 0, j))],
        out_specs=pl.BlockSpec((1, s, tc), lambda r, j, where: (r, 0, j)),
        out_shape=jax.ShapeDtypeStruct((3, s, d), WIRE_DTYPE), name=name, args=(grad, swapped), prefetch=(where,))


def _final_sum(where, grad, swapped, received, name, carried=()):
    s, d = swapped.shape[1], swapped.shape[2]
    tc = _tile(d, 512, LANE)

    def body(where_ref, g_ref, sw_ref, r_ref, o_ref):
        acc = g_ref[...].astype(F32) + sw_ref[0].astype(F32)
        for k in range(3):
            acc = acc + r_ref[k].astype(F32)
        o_ref[...] = acc

    return _call(
        body, grid=(d // tc,),
        in_specs=[pl.BlockSpec((s, tc), lambda j, where: (where[0], j)),
                  pl.BlockSpec((1, s, tc), lambda j, where: (where[4], 0, j)),
                  pl.BlockSpec((3, s, tc), lambda j, where: (0, 0, j))],
        out_specs=pl.BlockSpec((s, tc), lambda j, where: (0, j)),
        out_shape=jax.ShapeDtypeStruct((s, d), F32), name=name, args=(grad, swapped, received),
        prefetch=(where,), carried=carried)


class _GradReduction:
    def __init__(self, key, grad, shard_rows, where):
        self.key, self.grad, self.rows, self.where = key, grad, shard_rows, where

    def swap(self):
        self._swap = _CarriedSwap([self.grad], [self.rows])
        return self._swap

    def exchange(self):
        presum = _presum(self.where, self.grad, self._swap.results[0], "presum_" + self.key)
        self._exchange = _CarriedChipExchange([presum])
        return self._exchange

    def total(self, carried=()):
        return _final_sum(self.where, self.grad, self._swap.results[0], self._exchange.results[0],
                          "sum_" + self.key, carried)


def _adamw(w, g, m, v, name):
    rows, cols = w.shape
    tr = _tile(rows, 256, 8)

    def body(w_ref, g_ref, m_ref, v_ref, d_ref, nm_ref, nv_ref):
        g = g_ref[...]
        nm = ADAM_B1 * m_ref[...] + (1.0 - ADAM_B1) * g
        nv = ADAM_B2 * v_ref[...] + (1.0 - ADAM_B2) * (g * g)
        m_hat = nm / (1.0 - ADAM_B1 ** ADAM_STEP)
        v_hat = nv / (1.0 - ADAM_B2 ** ADAM_STEP)
        d_ref[...] = -ADAM_LR * (m_hat / (jnp.sqrt(v_hat) + ADAM_EPS) + ADAM_WD * w_ref[...])
        nm_ref[...] = nm
        nv_ref[...] = nv

    spec = pl.BlockSpec((tr, cols), lambda i: (i, 0))
    return pl.pallas_call(
        body, grid=(rows // tr,), in_specs=[spec] * 4, out_specs=[spec] * 3,
        out_shape=[jax.ShapeDtypeStruct((rows, cols), F32)] * 3, name=name, compiler_params=_params())(w, g, m, v)


def _pack_rows(arrays, width, row_quantum=8):
    flat = jnp.concatenate([a.reshape(-1) for a in arrays])
    total = _round_up(flat.shape[0], row_quantum * width)
    flat = jnp.pad(flat, (0, total - flat.shape[0]))
    return flat.reshape(-1, width)


def _unpack_rows(packed, shapes):
    flat = packed.reshape(-1)
    out = []
    off = 0
    for shp in shapes:
        size = 1
        for s in shp:
            size *= s
        out.append(flat[off:off + size].reshape(shp))
        off += size
    return out


def _block_diag(w):
    h, hb, _ = w.shape
    per = BD // hb
    w4 = w.reshape(h // per, per, hb, hb)
    eye = jnp.eye(per, dtype=w.dtype)
    return jnp.einsum('npij,pq->npiqj', w4, eye).reshape(h // per, BD, BD)


def _block_diag_extract(bd, hb):
    nbk = bd.shape[0]
    per = BD // hb
    b5 = bd.reshape(nbk, per, hb, per, hb)
    eye = jnp.eye(per, dtype=bd.dtype)
    return jnp.einsum('npiqj,pq->npij', b5, eye).reshape(nbk * per, hb, hb)


def kernel(x, meta_tokens, ffn1_pre_g, ffn1_w_gate, ffn1_w_up, ffn1_w_down, ffn1_post_g, mix_pre_g, w_in, lru_conv_w, lru_conv_b, lru_w_a, lru_b_a, lru_w_x, lru_b_x, lru_lambda, sconv_w, lru_out_g, sconv_out_g, w_out, mix_post_g, ffn2_pre_g, ffn2_w_gate, ffn2_w_up, ffn2_w_down, ffn2_post_g, loss_target, m_meta_tokens, m_ffn1_pre_g, m_ffn1_w_gate, m_ffn1_w_up, m_ffn1_w_down, m_ffn1_post_g, m_mix_pre_g, m_w_in, m_lru_conv_w, m_lru_conv_b, m_lru_w_a, m_lru_b_a, m_lru_w_x, m_lru_b_x, m_lru_lambda, m_sconv_w, m_lru_out_g, m_sconv_out_g, m_w_out, m_mix_post_g, m_ffn2_pre_g, m_ffn2_w_gate, m_ffn2_w_up, m_ffn2_w_down, m_ffn2_post_g, v_meta_tokens, v_ffn1_pre_g, v_ffn1_w_gate, v_ffn1_w_up, v_ffn1_w_down, v_ffn1_post_g, v_mix_pre_g, v_w_in, v_lru_conv_w, v_lru_conv_b, v_lru_w_a, v_lru_b_a, v_lru_w_x, v_lru_b_x, v_lru_lambda, v_sconv_w, v_lru_out_g, v_sconv_out_g, v_w_out, v_mix_post_g, v_ffn2_pre_g, v_ffn2_w_gate, v_ffn2_w_up, v_ffn2_w_down, v_ffn2_post_g):
    given = dict(locals())
    wts = {n: given[n] for n in WEIGHT_NAMES}
    mom = {n: given["m_" + n] for n in WEIGHT_NAMES}
    var = {n: given["v_" + n] for n in WEIGHT_NAMES}

    xi, yi, ci = _position()
    me = _block_of(xi, yi, ci)
    x2 = x[0]
    seq, d = x2.shape
    n_meta = meta_tokens.shape[0]
    m_rows = _round_up(n_meta + seq, ROW_ALIGN)
    pad = m_rows - n_meta - seq
    lead = pad + n_meta
    c = lru_conv_b.shape[1]
    hb = lru_w_a.shape[-1]
    dm = meta_tokens.shape[1]
    cs_ = lru_conv_w.shape[2]
    kw4, kw3 = lru_conv_w.shape[1], sconv_w.shape[1]
    assert d == 2 * c and c % BD == 0 and BD % hb == 0 and cs_ <= dm and kw4 == 4 and kw3 == 3

    small = jnp.zeros((_round_up(n_meta + kw4 + kw3, 8), dm), F32)
    small = small.at[0:n_meta].set(meta_tokens)
    small = small.at[n_meta:n_meta + kw4, 0:cs_].set(lru_conv_w[0])
    small = small.at[n_meta + kw4:n_meta + kw4 + kw3, 0:cs_].set(sconv_w[0])
    sr = small.shape[0]
    small_all = _gather_small(small, False, "gather_small").reshape(N_DEV, sr, dm)
    meta_full = small_all[:, 0:n_meta, :].transpose(1, 0, 2).reshape(n_meta, d)
    conv_w_full = small_all[:, n_meta:n_meta + kw4, 0:cs_].transpose(1, 0, 2).reshape(kw4, c)
    sconv_w_full = small_all[:, n_meta + kw4:n_meta + kw4 + kw3, 0:cs_].transpose(1, 0, 2).reshape(kw3, c)

    big = ['ffn1_w_gate', 'ffn1_w_up', 'ffn1_w_down', 'w_in', 'w_out', 'ffn2_w_gate', 'ffn2_w_up', 'ffn2_w_down']
    col_sharded = {'ffn1_w_gate', 'ffn1_w_up', 'w_in', 'ffn2_w_gate', 'ffn2_w_up'}
    shards = []
    for nme in big:
        w = wts[nme][0].astype(WIRE_DTYPE)
        shards.append(w.T if nme in col_sharded else w)
    shard_rows = dict(zip(big, [s.shape[0] for s in shards]))
    zeros = jnp.zeros((F_ALIGN, d), WIRE_DTYPE)

    def gather(forward_at, *names):
        sel = [shards[big.index(nme)] for nme in names]
        padded = [_round_up(N_DEV * shard_rows[nme], LANE if nme in ('w_in', 'w_out') else F_ALIGN) for nme in names]
        return _CarriedGather(sel, padded, zeros, forward_at)

    pv = jnp.zeros((16, c), F32)
    pv = pv.at[0:4].set(conv_w_full).at[4].set(lru_conv_b[0]).at[5].set(lru_b_a[0]).at[6].set(lru_b_x[0])
    pv = pv.at[7].set(lru_lambda[0]).at[8:11].set(sconv_w_full).at[11].set(lru_out_g[0]).at[12].set(sconv_out_g[0])
    wa_bd = _block_diag(lru_w_a[0]).astype(MXU_DTYPE)
    wx_bd = _block_diag(lru_w_x[0]).astype(MXU_DTYPE)
    gs = c // N_GROUPS
    gidx = jnp.arange(BD) // gs
    gm = jnp.where(gidx[:, None] == gidx[None, :], 1.0 / gs, 0.0).astype(MXU_DTYPE)

    h0 = jnp.concatenate([jnp.zeros((pad, d), F32), meta_full, x2], axis=0)
    ride = gather(1.0, 'ffn1_w_gate', 'ffn1_w_up')
    n1 = _rmsnorm(h0, ffn1_pre_g, "ffn1_prenorm", carried=[ride])
    wg1, wu1 = ride.results
    ride = gather(0.85, 'ffn1_w_down', 'w_out')
    g1, u1, a1 = _ffn_gate_up(n1, wg1, wu1, "ffn1_gate_up", carried=[ride])
    wd1, wout = ride.results
    ride = gather(0.8, 'w_in')
    fo1, h1, un = _mm_residual_norm(a1, wd1, h0, ffn1_post_g, 0.5, mix_pre_g, "ffn1_down", carried=[ride])
    (win_t,) = ride.results
    ride = gather(1.0, 'ffn2_w_gate')
    z = _mm_nt(un, win_t, "mix_in_proj", carried=[ride])
    (wg2,) = ride.results
    ride = gather(1.0, 'ffn2_w_up')
    mixed, hs = _mixer_fwd(z, pv, wa_bd, wx_bd, gm, pad, "mixer_fwd", carried=[ride])
    (wu2,) = ride.results
    o_mix, h2, n2 = _mm_residual_norm(mixed, wout, h1, mix_post_g, 1.0, ffn2_pre_g, "mix_out_proj")
    ride = gather(0.75, 'ffn2_w_down')
    g2, u2, a2 = _ffn_gate_up(n2, wg2, wu2, "ffn2_gate_up", carried=[ride])
    (wd2,) = ride.results
    target = jnp.concatenate([jnp.zeros((lead, d), F32), loss_target[0]], axis=0)
    dh3, dfo2, d_post2, loss_part = _mm_residual_loss(a2, wd2, h2, ffn2_post_g, 0.5, target, lead, "ffn2_down_loss")
    loss = lax.psum(loss_part[0, 0], ("x", "y", "c"))

    chip_rel = [2 * (xi ^ (r >> 1)) + (yi ^ (r & 1)) for r in range(4)]
    where = jnp.stack([2 * k + ci for k in chip_rel] + chip_rel).astype(jnp.int32)
    red = {}

    def reduction(nme, grad):
        red[nme] = _GradReduction(nme, grad, shard_rows[nme], where)
        return red[nme]

    r_wd2 = reduction('ffn2_w_down', _mm_tn(a2, dfo2, "ffn2_dw_down"))
    dg2, du2 = _ffn_hidden_bwd(dfo2, wd2, g2, u2, "ffn2_hidden_bwd", carried=[r_wd2.swap()])
    r_wg2 = reduction('ffn2_w_gate', _mm_tn(dg2, n2, "ffn2_dw_gate", carried=[r_wd2.exchange()]))
    r_wu2 = reduction('ffn2_w_up', _mm_tn(du2, n2, "ffn2_dw_up", carried=[r_wg2.swap()]))
    dh2, d_pre2 = _mm_norm_bwd([(dg2, wg2), (du2, wu2)], h2, ffn2_pre_g, dh3, "ffn2_dx",
                               carried=[r_wg2.exchange(), r_wu2.swap()])
    do_mix, d_mix_post = _norm_bwd(o_mix, mix_post_g, dh2, 1.0, "mix_postnorm_bwd")
    dmixed = _mm_nt(do_mix, wout, "mix_out_proj_bwd")
    r_wout = reduction('w_out', _mm_tn(mixed, do_mix, "mix_dw_out"))
    dz, dpv, dwa_bd, dwx_bd = _mixer_bwd(z, hs, dmixed, pv, wa_bd, wx_bd, gm, pad, "mixer_bwd",
                                         carried=[r_wu2.exchange(), r_wout.swap()])
    r_win = reduction('w_in', _mm_tn(dz, un, "mix_dw_in", carried=[r_wout.exchange()]))
    dh1, d_mix_pre = _mm_norm_bwd([(dz, win_t)], h1, mix_pre_g, dh2, "mix_dx", carried=[r_win.swap()])
    dfo1, d_post1 = _norm_bwd(fo1, ffn1_post_g, dh1, 0.5, "ffn1_postnorm_bwd")
    r_wd1 = reduction('ffn1_w_down', _mm_tn(a1, dfo1, "ffn1_dw_down", carried=[r_win.exchange()]))
    dg1, du1 = _ffn_hidden_bwd(dfo1, wd1, g1, u1, "ffn1_hidden_bwd", carried=[r_wd1.swap()])
    r_wg1 = reduction('ffn1_w_gate', _mm_tn(dg1, n1, "ffn1_dw_gate", carried=[r_wd1.exchange()]))
    r_wu1 = reduction('ffn1_w_up', _mm_tn(du1, n1, "ffn1_dw_up", carried=[r_wg1.swap()]))
    row_tile = _norm_bwd_row_tile(m_rows)
    n_tiles = m_rows // row_tile
    half = n_tiles // 2
    assert half >= 1 and half * row_tile >= lead
    dh0_a, d_pre1_a = _mm_norm_bwd([(dg1, wg1), (du1, wu1)], h0, ffn1_pre_g, dh1, "ffn1_dx_a",
                                   carried=[r_wg1.exchange(), r_wu1.swap()], row_tiles=(0, half))
    dh0_b, d_pre1 = _mm_norm_bwd([(dg1, wg1), (du1, wu1)], h0, ffn1_pre_g, dh1, "ffn1_dx_b",
                                 carried=[r_wu1.exchange()], row_tiles=(half, n_tiles - half), dg_init=d_pre1_a)
    grad_x = jnp.concatenate([dh0_a[lead:], dh0_b], axis=0)[None]
    d_meta = dh0_a[pad:lead]

    grads = {}
    for nme in big:
        gsum = red[nme].total()
        grads[nme] = (gsum.T if nme in col_sharded else gsum)[None]

    small_names = ['ffn1_pre_g', 'ffn1_post_g', 'mix_pre_g', 'mix_post_g', 'ffn2_pre_g', 'ffn2_post_g',
                   'lru_conv_b', 'lru_b_a', 'lru_b_x', 'lru_lambda', 'lru_out_g', 'sconv_out_g',
                   'lru_conv_w', 'sconv_w', 'lru_w_a', 'lru_w_x', 'meta_tokens']
    small_parts = [d_pre1, d_post1, d_mix_pre, d_mix_post, d_pre2, d_post2,
                   dpv[4:5], dpv[5:6], dpv[6:7], dpv[7:8], dpv[11:12], dpv[12:13],
                   dpv[0:4], dpv[8:11], _block_diag_extract(dwa_bd, hb), _block_diag_extract(dwx_bd, hb),
                   d_meta]
    small_shapes = [p.shape for p in small_parts]
    small_sum = _gather_small(_pack_rows(small_parts, d), True, "reduce_small")
    for nme, gsm in zip(small_names, _unpack_rows(small_sum, small_shapes)):
        if nme == 'meta_tokens':
            grads[nme] = lax.dynamic_slice_in_dim(gsm, me * dm, dm, axis=1)
        elif nme in ('lru_conv_w', 'sconv_w'):
            grads[nme] = lax.dynamic_slice_in_dim(gsm, me * cs_, cs_, axis=1)[None]
        else:
            grads[nme] = gsm.reshape(wts[nme].shape)

    delta, new_m, new_v = {}, {}, {}
    for nme in big:
        shp = wts[nme].shape
        dl, nm_, nv_ = _adamw(wts[nme][0], grads[nme][0], mom[nme][0], var[nme][0], "adamw_" + nme)
        delta[nme], new_m[nme], new_v[nme] = dl.reshape(shp), nm_.reshape(shp), nv_.reshape(shp)
    rest = [n for n in WEIGHT_NAMES if n not in big]
    rest_shapes = [wts[n].shape for n in rest]
    packed = [_pack_rows([src[n] for n in rest], LANE, 256) for src in (wts, grads, mom, var)]
    for out, packed_out in zip((delta, new_m, new_v), _adamw(*packed, "adamw_small")):
        for nme, arr in zip(rest, _unpack_rows(packed_out, rest_shapes)):
            out[nme] = arr

    return (loss, grad_x, *[grads[n] for n in WEIGHT_NAMES], *[delta[n] for n in WEIGHT_NAMES],
            *[new_m[n] for n in WEIGHT_NAMES], *[new_v[n] for n in WEIGHT_NAMES])
```

```python
import functools

import jax
import jax.numpy as jnp
from jax import lax
from jax.experimental import pallas as pl
from jax.experimental.pallas import tpu as pltpu

F32 = jnp.float32
MXU_DTYPE = jnp.bfloat16
WIRE_DTYPE = jnp.bfloat16
MESH = pl.DeviceIdType.MESH

EPS = 1e-6
LRU_C = 8.0
N_GROUPS = 16
ADAM_LR = 0.001
ADAM_B1 = 0.9
ADAM_B2 = 0.999
ADAM_EPS = 1e-08
ADAM_WD = 0.01
ADAM_STEP = 10

N_DEV = 8
LANE = 128
SUBLANE_BF16 = 16
ROW_ALIGN = 128
F_ALIGN = 512
BD = 256
K_TILE = 512
ACC_ROWS = 528
MIX_ROWS = 128
VMEM_LIMIT_MB = 56

WEIGHT_NAMES = ['meta_tokens', 'ffn1_pre_g', 'ffn1_w_gate', 'ffn1_w_up', 'ffn1_w_down', 'ffn1_post_g',
                'mix_pre_g', 'w_in', 'lru_conv_w', 'lru_conv_b', 'lru_w_a', 'lru_b_a', 'lru_w_x', 'lru_b_x',
                'lru_lambda', 'sconv_w', 'lru_out_g', 'sconv_out_g', 'w_out', 'mix_post_g', 'ffn2_pre_g',
                'ffn2_w_gate', 'ffn2_w_up', 'ffn2_w_down', 'ffn2_post_g']


def _round_up(n, q):
    return (n + q - 1) // q * q


def _tile(n, target, q):
    best = None
    t = q
    while t <= min(n, target):
        if n % t == 0:
            best = t
        t += q
    assert best is not None, (n, target, q)
    return best


def _params(**kw):
    return pltpu.CompilerParams(vmem_limit_bytes=VMEM_LIMIT_MB << 20, **kw)


def _call(body, *, grid, in_specs, out_specs, out_shape, name, args, scratch_shapes=(), carried=(), prefetch=()):
    carried = list(carried)
    n_pf = len(prefetch)

    def launch(fn, in_specs_, out_specs_, out_shape_, scratch_, operands, aliases_):
        if n_pf:
            spec = pltpu.PrefetchScalarGridSpec(num_scalar_prefetch=n_pf, grid=grid, in_specs=in_specs_,
                                                out_specs=out_specs_, scratch_shapes=scratch_)
            return pl.pallas_call(fn, grid_spec=spec, out_shape=out_shape_, input_output_aliases=aliases_,
                                  name=name, compiler_params=_params())(*prefetch, *operands)
        return pl.pallas_call(fn, grid=grid, in_specs=in_specs_, out_specs=out_specs_, out_shape=out_shape_,
                              scratch_shapes=scratch_, input_output_aliases=aliases_, name=name,
                              compiler_params=_params())(*operands)

    if not carried:
        return launch(body, in_specs, out_specs, out_shape, list(scratch_shapes), args, {})
    single = not isinstance(out_shape, (list, tuple))
    out_specs_l = [out_specs] if single else list(out_specs)
    out_shape_l = [out_shape] if single else list(out_shape)
    n_in, n_out, n_scr = len(in_specs), len(out_specs_l), len(scratch_shapes)
    hbm = pl.BlockSpec(memory_space=pl.ANY)
    c_in = [a for cm in carried for a in cm.arrays]
    c_out = [s for cm in carried for s in cm.out_shapes]
    c_scr = []
    aliases = {}
    in_off, out_off = n_pf + n_in, n_out
    for cm in carried:
        c_scr += [pltpu.SemaphoreType.DMA((cm.n_remote,)), pltpu.SemaphoreType.DMA((cm.n_remote,)),
                  pltpu.SemaphoreType.DMA((max(cm.n_local, 1),))]
        for k, v in cm.aliases.items():
            aliases[in_off + k] = out_off + v
        in_off += len(cm.arrays)
        out_off += len(cm.out_shapes)
    steps = 1
    for g in grid:
        steps *= g
    forward_steps = [min(int(cm.forward_at * steps), steps - 1) for cm in carried]

    def wrapped(*refs):
        pf = refs[:n_pf]
        p = n_pf
        ins = refs[p:p + n_in]
        p += n_in
        cins = refs[p:p + len(c_in)]
        p += len(c_in)
        outs = refs[p:p + n_out]
        p += n_out
        couts = refs[p:p + len(c_out)]
        p += len(c_out)
        scr = refs[p:p + n_scr]
        csem = refs[p + n_scr:]
        lin = 0
        for axis, g in enumerate(grid):
            lin = lin * g + pl.program_id(axis)
        views = []
        io = oo = 0
        for j, cm in enumerate(carried):
            views.append((cins[io:io + len(cm.arrays)], couts[oo:oo + len(cm.out_shapes)],
                          csem[3 * j], csem[3 * j + 1], csem[3 * j + 2]))
            io += len(cm.arrays)
            oo += len(cm.out_shapes)

        @pl.when(lin == 0)
        def _():
            for cm, v in zip(carried, views):
                cm.start(*v)

        body(*pf, *ins, *outs, *scr)

        for cm, v, step in zip(carried, views, forward_steps):
            pl.when(lin == step)(functools.partial(cm.forward, *v))

        @pl.when(lin == steps - 1)
        def _():
            for cm, v in zip(carried, views):
                cm.finish(*v)

    res = launch(wrapped, list(in_specs) + [hbm] * len(c_in), out_specs_l + [hbm] * len(c_out),
                 out_shape_l + c_out, list(scratch_shapes) + c_scr, (*args, *c_in), aliases)
    oo = n_out
    for cm in carried:
        cm.results = list(res[oo:oo + len(cm.out_shapes)])
        oo += len(cm.out_shapes)
    return res[0] if single else list(res[:n_out])


def _rmsnorm(h, g, name, carried=()):
    m, d = h.shape
    tm = _tile(m, 528, SUBLANE_BF16)

    def body(h_ref, g_ref, o_ref):
        x = h_ref[...]
        r = lax.rsqrt(jnp.mean(x * x, axis=-1, keepdims=True) + EPS)
        o_ref[...] = (x * r * g_ref[...]).astype(o_ref.dtype)

    return _call(
        body, grid=(m // tm,),
        in_specs=[pl.BlockSpec((tm, d), lambda i: (i, 0)), pl.BlockSpec((1, d), lambda i: (0, 0))],
        out_specs=pl.BlockSpec((tm, d), lambda i: (i, 0)),
        out_shape=jax.ShapeDtypeStruct((m, d), MXU_DTYPE), name=name, args=(h, g), carried=carried)


def _rmsnorm_bwd_rows(x, g, dy):
    r = lax.rsqrt(jnp.mean(x * x, axis=-1, keepdims=True) + EPS)
    xh = x * r
    dyh = dy * g
    dx = r * (dyh - xh * jnp.mean(dyh * xh, axis=-1, keepdims=True))
    return dx, dy * xh


def _norm_bwd(x, g, dy, scale, name, carried=()):
    m, d = x.shape
    tm = _tile(m, 528, SUBLANE_BF16)

    def body(x_ref, g_ref, dy_ref, dx_ref, dg_ref):
        @pl.when(pl.program_id(0) == 0)
        def _():
            dg_ref[...] = jnp.zeros_like(dg_ref)

        dx, dgr = _rmsnorm_bwd_rows(x_ref[...], g_ref[...], scale * dy_ref[...])
        dx_ref[...] = dx.astype(dx_ref.dtype)
        dg_ref[...] += jnp.sum(dgr, axis=0, keepdims=True)

    return _call(
        body, grid=(m // tm,),
        in_specs=[pl.BlockSpec((tm, d), lambda i: (i, 0)), pl.BlockSpec((1, d), lambda i: (0, 0)),
                  pl.BlockSpec((tm, d), lambda i: (i, 0))],
        out_specs=[pl.BlockSpec((tm, d), lambda i: (i, 0)), pl.BlockSpec((1, d), lambda i: (0, 0))],
        out_shape=[jax.ShapeDtypeStruct((m, d), MXU_DTYPE), jax.ShapeDtypeStruct((1, d), F32)],
        name=name, args=(x, g, dy), carried=carried)


def _dot_nt(a, b):
    return lax.dot_general(a, b, (((1,), (1,)), ((), ())), preferred_element_type=F32)


def _dot_tn(a, b):
    return lax.dot_general(a, b, (((0,), (0,)), ((), ())), preferred_element_type=F32)


def _mm_nt(a, w, name, carried=()):
    m, k = a.shape
    n = w.shape[0]
    tm = _tile(m, 1056, SUBLANE_BF16)
    tn = _tile(n, 512, LANE)

    def body(a_ref, w_ref, o_ref):
        o_ref[...] = _dot_nt(a_ref[...], w_ref[...])

    return _call(
        body, grid=(m // tm, n // tn),
        in_specs=[pl.BlockSpec((tm, k), lambda i, j: (i, 0)), pl.BlockSpec((tn, k), lambda i, j: (j, 0))],
        out_specs=pl.BlockSpec((tm, tn), lambda i, j: (i, j)),
        out_shape=jax.ShapeDtypeStruct((m, n), F32), name=name, args=(a, w), carried=carried)


def _ffn_gate_up(n_act, wg_t, wu_t, name, carried=()):
    m, d = n_act.shape
    fp = wg_t.shape[0]
    tm = _tile(m, 1056, SUBLANE_BF16)
    tn = _tile(fp, 512, LANE)

    def body(n_ref, wg_ref, wu_ref, g_ref, u_ref, a_ref):
        n = n_ref[...]
        g = _dot_nt(n, wg_ref[...])
        u = _dot_nt(n, wu_ref[...])
        g_ref[...] = g.astype(g_ref.dtype)
        u_ref[...] = u.astype(u_ref.dtype)
        a_ref[...] = (g * jax.nn.sigmoid(g) * u).astype(a_ref.dtype)

    act = pl.BlockSpec((tm, tn), lambda i, j: (i, j))
    wsp = pl.BlockSpec((tn, d), lambda i, j: (j, 0))
    return _call(
        body, grid=(m // tm, fp // tn),
        in_specs=[pl.BlockSpec((tm, d), lambda i, j: (i, 0)), wsp, wsp],
        out_specs=[act, act, act],
        out_shape=[jax.ShapeDtypeStruct((m, fp), MXU_DTYPE)] * 3, name=name, args=(n_act, wg_t, wu_t), carried=carried)


def _ffn_hidden_bwd(dfo, wd, g_act, u_act, name, carried=()):
    m, d = dfo.shape
    fp = wd.shape[0]
    tm = _tile(m, 1056, SUBLANE_BF16)
    tn = _tile(fp, 512, LANE)

    def body(df_ref, wd_ref, g_ref, u_ref, dg_ref, du_ref):
        da = _dot_nt(df_ref[...], wd_ref[...])
        g = g_ref[...].astype(F32)
        u = u_ref[...].astype(F32)
        s = jax.nn.sigmoid(g)
        du_ref[...] = (da * (g * s)).astype(du_ref.dtype)
        dg_ref[...] = (da * u * (s * (1.0 + g * (1.0 - s)))).astype(dg_ref.dtype)

    act = pl.BlockSpec((tm, tn), lambda i, j: (i, j))
    return _call(
        body, grid=(m // tm, fp // tn),
        in_specs=[pl.BlockSpec((tm, d), lambda i, j: (i, 0)), pl.BlockSpec((tn, d), lambda i, j: (j, 0)), act, act],
        out_specs=[act, act],
        out_shape=[jax.ShapeDtypeStruct((m, fp), MXU_DTYPE)] * 2, name=name, args=(dfo, wd, g_act, u_act),
        carried=carried)


def _mm_residual_norm(a, w, h, g, scale, next_g, name, carried=()):
    m, k = a.shape
    d = w.shape[1]
    tm = _tile(m, ACC_ROWS, SUBLANE_BF16)
    tk = _tile(k, K_TILE, LANE)
    nk = k // tk

    def body(a_ref, w_ref, h_ref, g_ref, ng_ref, fo_ref, hn_ref, nn_ref, acc_ref):
        kk = pl.program_id(1)

        @pl.when(kk == 0)
        def _():
            acc_ref[...] = jnp.zeros_like(acc_ref)

        acc_ref[...] += jnp.dot(a_ref[...], w_ref[...], preferred_element_type=F32)

        @pl.when(kk == nk - 1)
        def _():
            fo = acc_ref[...]
            fo_ref[...] = fo
            r = lax.rsqrt(jnp.mean(fo * fo, axis=-1, keepdims=True) + EPS)
            hn = h_ref[...] + scale * (fo * r * g_ref[...])
            hn_ref[...] = hn
            rn = lax.rsqrt(jnp.mean(hn * hn, axis=-1, keepdims=True) + EPS)
            nn_ref[...] = (hn * rn * ng_ref[...]).astype(nn_ref.dtype)

    row = pl.BlockSpec((tm, d), lambda i, kk: (i, 0))
    row_once = pl.BlockSpec((tm, d), lambda i, kk: (i, 0), pipeline_mode=pl.Buffered(1))
    vec = pl.BlockSpec((1, d), lambda i, kk: (0, 0))
    return _call(
        body, grid=(m // tm, nk),
        in_specs=[pl.BlockSpec((tm, tk), lambda i, kk: (i, kk)), pl.BlockSpec((tk, d), lambda i, kk: (kk, 0)),
                  row_once, vec, vec],
        out_specs=[row, row, row],
        out_shape=[jax.ShapeDtypeStruct((m, d), F32)] * 2 + [jax.ShapeDtypeStruct((m, d), MXU_DTYPE)],
        scratch_shapes=[pltpu.VMEM((tm, d), F32)], name=name, args=(a, w, h, g, next_g), carried=carried)


def _mm_residual_loss(a, w, h, g, scale, target, lead, name, carried=()):
    m, k = a.shape
    d = w.shape[1]
    tm = _tile(m, ACC_ROWS, SUBLANE_BF16)
    tk = _tile(k, K_TILE, LANE)
    nk = k // tk

    def body(a_ref, w_ref, h_ref, g_ref, t_ref, dy_ref, dfo_ref, dg_ref, l_ref, acc_ref):
        i = pl.program_id(0)
        kk = pl.program_id(1)

        @pl.when(jnp.logical_and(i == 0, kk == 0))
        def _():
            dg_ref[...] = jnp.zeros_like(dg_ref)
            l_ref[...] = jnp.zeros_like(l_ref)

        @pl.when(kk == 0)
        def _():
            acc_ref[...] = jnp.zeros_like(acc_ref)

        acc_ref[...] += jnp.dot(a_ref[...], w_ref[...], preferred_element_type=F32)

        @pl.when(kk == nk - 1)
        def _():
            fo = acc_ref[...]
            gain = g_ref[...]
            r = lax.rsqrt(jnp.mean(fo * fo, axis=-1, keepdims=True) + EPS)
            xh = fo * r
            y = h_ref[...] + scale * (xh * gain)
            row = i * tm + lax.broadcasted_iota(jnp.int32, (tm, 1), 0)
            e = jnp.where(row >= lead, y - t_ref[...], 0.0)
            dy = e * (1.0 / d)
            dy_ref[...] = dy
            l_ref[...] += 0.5 * jnp.sum(jnp.sum(e * e, axis=-1, keepdims=True) * (1.0 / d), axis=0, keepdims=True)
            dn = scale * dy
            dyh = dn * gain
            dfo_ref[...] = (r * (dyh - xh * jnp.mean(dyh * xh, axis=-1, keepdims=True))).astype(dfo_ref.dtype)
            dg_ref[...] += jnp.sum(dn * xh, axis=0, keepdims=True)

    row = pl.BlockSpec((tm, d), lambda i, kk: (i, 0))
    row_once = pl.BlockSpec((tm, d), lambda i, kk: (i, 0), pipeline_mode=pl.Buffered(1))
    vec = pl.BlockSpec((1, d), lambda i, kk: (0, 0))
    return _call(
        body, grid=(m // tm, nk),
        in_specs=[pl.BlockSpec((tm, tk), lambda i, kk: (i, kk)), pl.BlockSpec((tk, d), lambda i, kk: (kk, 0)),
                  row_once, vec, row_once],
        out_specs=[row, row, vec, pl.BlockSpec((1, 1), lambda i, kk: (0, 0))],
        out_shape=[jax.ShapeDtypeStruct((m, d), F32), jax.ShapeDtypeStruct((m, d), MXU_DTYPE),
                   jax.ShapeDtypeStruct((1, d), F32), jax.ShapeDtypeStruct((1, 1), F32)],
        scratch_shapes=[pltpu.VMEM((tm, d), F32)], name=name, args=(a, w, h, g, target), carried=carried)


def _norm_bwd_row_tile(m):
    return _tile(m, ACC_ROWS, SUBLANE_BF16)


def _mm_norm_bwd(pairs, h, g, dh_up, name, carried=(), row_tiles=None, dg_init=None):
    n_pairs = len(pairs)
    m, k = pairs[0][0].shape
    d = h.shape[1]
    tm = _norm_bwd_row_tile(m)
    tk = _tile(k, K_TILE, LANE)
    nk = k // tk
    t0, nt = row_tiles if row_tiles is not None else (0, m // tm)
    if dg_init is None:
        dg_init = jnp.zeros((1, d), F32)

    def body(*refs):
        ops = refs[:2 * n_pairs]
        h_ref, g_ref, up_ref, init_ref, dh_ref, dg_ref, acc_ref = refs[2 * n_pairs:]
        i = pl.program_id(0)
        kk = pl.program_id(1)

        @pl.when(jnp.logical_and(i == 0, kk == 0))
        def _():
            dg_ref[...] = init_ref[...]

        @pl.when(kk == 0)
        def _():
            acc_ref[...] = jnp.zeros_like(acc_ref)

        for p in range(n_pairs):
            acc_ref[...] += jnp.dot(ops[2 * p][...], ops[2 * p + 1][...], preferred_element_type=F32)

        @pl.when(kk == nk - 1)
        def _():
            dx, dgr = _rmsnorm_bwd_rows(h_ref[...], g_ref[...], acc_ref[...])
            dh_ref[...] = up_ref[...] + dx
            dg_ref[...] += jnp.sum(dgr, axis=0, keepdims=True)

    row_in = pl.BlockSpec((tm, d), lambda i, kk: (t0 + i, 0))
    vec = pl.BlockSpec((1, d), lambda i, kk: (0, 0))
    in_specs = []
    args = []
    for a, w in pairs:
        in_specs += [pl.BlockSpec((tm, tk), lambda i, kk: (t0 + i, kk)), pl.BlockSpec((tk, d), lambda i, kk: (kk, 0))]
        args += [a, w]
    return _call(
        body, grid=(nt, nk),
        in_specs=in_specs + [row_in, vec, row_in, vec], out_specs=[pl.BlockSpec((tm, d), lambda i, kk: (i, 0)), vec],
        out_shape=[jax.ShapeDtypeStruct((nt * tm, d), F32), jax.ShapeDtypeStruct((1, d), F32)],
        scratch_shapes=[pltpu.VMEM((tm, d), F32)], name=name, args=(*args, h, g, dh_up, dg_init), carried=carried)


def _mm_tn(a, b, name, carried=()):
    m, ka = a.shape
    d = b.shape[1]
    tf = _tile(ka, 512, LANE)

    def body(a_ref, b_ref, o_ref):
        o_ref[...] = _dot_tn(a_ref[...], b_ref[...]).astype(o_ref.dtype)

    return _call(
        body, grid=(ka // tf,),
        in_specs=[pl.BlockSpec((m, tf), lambda j: (0, j)),
                  pl.BlockSpec((m, d), lambda j: (0, 0), pipeline_mode=pl.Buffered(1))],
        out_specs=pl.BlockSpec((tf, d), lambda j: (j, 0)),
        out_shape=jax.ShapeDtypeStruct((ka, d), WIRE_DTYPE), name=name, args=(a, b), carried=carried)


GELU_K = 0.7978845608028654
GELU_C = 0.044715


def _expm1(x):
    series = x * (1.0 + x * (1.0 / 2 + x * (1.0 / 6 + x * (1.0 / 24 + x * (1.0 / 120 + x * (1.0 / 720 + x * (1.0 / 5040)))))))
    return jnp.where(jnp.abs(x) < 0.3, series, jnp.exp(x) - 1.0)


def _softplus(x):
    return jnp.maximum(x, 0.0) + jnp.log1p(jnp.exp(-jnp.abs(x)))


def _block_mm(v, w_ref, transposed):
    nbk = w_ref.shape[0]
    outs = []
    for j in range(nbk):
        vj = v[:, j * BD:(j + 1) * BD]
        outs.append(_dot_nt(vj, w_ref[j]) if transposed else jnp.dot(vj, w_ref[j], preferred_element_type=F32))
    return outs[0] if nbk == 1 else jnp.concatenate(outs, axis=1)


def _group_mean(q, gm_ref):
    hi = q.astype(MXU_DTYPE)
    lo = (q - hi.astype(F32)).astype(MXU_DTYPE)
    nbk = q.shape[1] // BD
    gm = gm_ref[...]
    outs = []
    for j in range(nbk):
        sl = slice(j * BD, (j + 1) * BD)
        outs.append(jnp.dot(hi[:, sl], gm, preferred_element_type=F32) + jnp.dot(lo[:, sl], gm, preferred_element_type=F32))
    return outs[0] if nbk == 1 else jnp.concatenate(outs, axis=1)


class _RowReader:
    def __init__(self, ref):
        self.ref = ref

    def __getitem__(self, rows):
        return self.ref[rows, :]


def _shifted(ext_ref, cur, before8, after8, downs=(), ups=()):
    r = cur.shape[0]
    if downs:
        ext_ref[0:8, :] = before8
    ext_ref[8:8 + r, :] = cur
    if ups:
        ext_ref[8 + r:16 + r, :] = after8
    return [ext_ref[pl.ds(8 - j, r), :] for j in downs] + [ext_ref[pl.ds(8 + j, r), :] for j in ups]


def _lru_gates(xc, pv, wa_ref, wx_ref):
    xcb = xc.astype(MXU_DTYPE)
    ga = jax.nn.sigmoid(_block_mm(xcb, wa_ref, False) + pv[5:6])
    gx = jax.nn.sigmoid(_block_mm(xcb, wx_ref, False) + pv[6:7])
    sp = _softplus(-pv[7:8])
    log_a = -LRU_C * ga * sp
    a = jnp.exp(log_a)
    e2 = _expm1(2.0 * log_a)
    mult = jnp.sqrt(-e2)
    return xcb, ga, gx, sp, a, e2, mult


def _gelu_parts(y):
    th = jnp.tanh(GELU_K * (y + GELU_C * y * y * y))
    return 0.5 * y * (1.0 + th), th


def _mixer_fwd(z, pv, wa, wx, gm, pad, name, carried=()):
    m = z.shape[0]
    c = pv.shape[1]
    r = MIX_ROWS
    nb = m // r

    def body(z_ref, pv_ref, wa_ref, wx_ref, gm_ref, mixed_ref, hs_ref, ext_ref, tailx_ref, tailc_ref, carry_ref):
        b = pl.program_id(0)

        @pl.when(b == 0)
        def _():
            tailx_ref[...] = jnp.zeros_like(tailx_ref)
            tailc_ref[...] = jnp.zeros_like(tailc_ref)
            carry_ref[...] = jnp.zeros_like(carry_ref)

        pv = _RowReader(pv_ref)
        row = b * r + lax.broadcasted_iota(jnp.int32, (r, 1), 0)
        lrow = lax.broadcasted_iota(jnp.int32, (r, c), 0)
        maskf = (row >= pad).astype(F32)
        y = z_ref[:, 0:c]
        xl = z_ref[:, c:2 * c]
        bs = z_ref[:, 2 * c:3 * c]
        cv = z_ref[:, 3 * c:4 * c] * z_ref[:, 4 * c:5 * c]

        x1, x2, x3 = _shifted(ext_ref, xl, tailx_ref[...], None, downs=(1, 2, 3))
        tailx_ref[...] = z_ref[pl.ds(r - 8, 8), c:2 * c]
        xc = pv[4:5] + pv[3:4] * xl + pv[2:3] * x1 + pv[1:2] * x2 + pv[0:1] * x3
        _, _, gx, _, a, _, mult = _lru_gates(xc, pv, wa_ref, wx_ref)
        uu = mult * (gx * xc) * maskf

        acc_a = a
        acc_h = uu
        dlt = 1
        while dlt < r:
            keep = lrow >= dlt
            sh_a = pltpu.roll(acc_a, dlt, axis=0)
            sh_h = pltpu.roll(acc_h, dlt, axis=0)
            acc_h = acc_h + acc_a * jnp.where(keep, sh_h, 0.0)
            acc_a = acc_a * jnp.where(keep, sh_a, 1.0)
            dlt *= 2
        hs = acc_h + acc_a * carry_ref[...]
        hs_ref[...] = hs
        carry_ref[...] = hs_ref[pl.ds(r - 1, 1), :]

        gelu_y, _ = _gelu_parts(y)
        lru_out = hs * gelu_y
        c1, c2 = _shifted(ext_ref, cv, tailc_ref[...], None, downs=(1, 2))
        tailc_ref[...] = cv[r - 8:r]
        sc_out = bs * (pv[10:11] * cv + pv[9:10] * c1 + pv[8:9] * c2)

        rl = lax.rsqrt(_group_mean(lru_out * lru_out, gm_ref) + EPS)
        rs = lax.rsqrt(_group_mean(sc_out * sc_out, gm_ref) + EPS)
        mixed_ref[:, 0:c] = (lru_out * rl * pv[11:12]).astype(mixed_ref.dtype)
        mixed_ref[:, c:2 * c] = (sc_out * rs * pv[12:13]).astype(mixed_ref.dtype)

    full = lambda shape: pl.BlockSpec(shape, lambda b: (0,) * len(shape))
    return _call(
        body, grid=(nb,),
        in_specs=[pl.BlockSpec((r, 5 * c), lambda b: (b, 0)), full(pv.shape), full(wa.shape), full(wx.shape), full(gm.shape)],
        out_specs=[pl.BlockSpec((r, 2 * c), lambda b: (b, 0)), pl.BlockSpec((r, c), lambda b: (b, 0))],
        out_shape=[jax.ShapeDtypeStruct((m, 2 * c), MXU_DTYPE), jax.ShapeDtypeStruct((m, c), F32)],
        scratch_shapes=[pltpu.VMEM((r + 16, c), F32), pltpu.VMEM((8, c), F32), pltpu.VMEM((8, c), F32),
                        pltpu.VMEM((1, c), F32)],
        name=name, args=(z, pv, wa, wx, gm), carried=carried)


def _mixer_bwd(z, hs, dmixed, pv, wa, wx, gm, pad, name, carried=()):
    m = z.shape[0]
    c = pv.shape[1]
    r = MIX_ROWS
    nb = m // r
    r8 = r // 8

    def body(z_ref, zp_ref, hs_ref, hsp_ref, dm_ref, pv_ref, wa_ref, wx_ref, gm_ref,
             dz_ref, dpv_ref, dwa_ref, dwx_ref, ext_ref, hxc_ref, hsc_ref, hp_ref):
        i = pl.program_id(0)
        b = nb - 1 - i

        @pl.when(i == 0)
        def _():
            hxc_ref[...] = jnp.zeros_like(hxc_ref)
            hsc_ref[...] = jnp.zeros_like(hsc_ref)
            hp_ref[...] = jnp.zeros_like(hp_ref)
            dpv_ref[...] = jnp.zeros_like(dpv_ref)
            dwa_ref[...] = jnp.zeros_like(dwa_ref)
            dwx_ref[...] = jnp.zeros_like(dwx_ref)

        pv = _RowReader(pv_ref)
        row = b * r + lax.broadcasted_iota(jnp.int32, (r, 1), 0)
        lrow = lax.broadcasted_iota(jnp.int32, (r, c), 0)
        maskf = (row >= pad).astype(F32)
        has_prev = (b > 0).astype(F32)
        y = z_ref[:, 0:c]
        xl = z_ref[:, c:2 * c]
        bs = z_ref[:, 2 * c:3 * c]
        cs = z_ref[:, 3 * c:4 * c]
        vs = z_ref[:, 4 * c:5 * c]
        cv = cs * vs
        xl_prev = zp_ref[:, c:2 * c] * has_prev
        cv_prev = zp_ref[:, 3 * c:4 * c] * zp_ref[:, 4 * c:5 * c] * has_prev
        hs = hs_ref[...]

        x1, x2, x3 = _shifted(ext_ref, xl, xl_prev, None, downs=(1, 2, 3))
        xc = pv[4:5] + pv[3:4] * xl + pv[2:3] * x1 + pv[1:2] * x2 + pv[0:1] * x3
        xcb, ga, gx, sp, a, e2, mult = _lru_gates(xc, pv, wa_ref, wx_ref)
        gxx = gx * xc
        gelu_y, th = _gelu_parts(y)
        lru_out = hs * gelu_y
        c1, c2 = _shifted(ext_ref, cv, cv_prev, None, downs=(1, 2))
        sc = pv[10:11] * cv + pv[9:10] * c1 + pv[8:9] * c2
        sc_out = bs * sc

        def group_norm_bwd(v, dm, gain):
            rr = lax.rsqrt(_group_mean(v * v, gm_ref) + EPS)
            vh = v * rr
            dvh = dm * gain
            dv = rr * (dvh - vh * _group_mean(dvh * vh, gm_ref))
            return dv, jnp.sum(dm * vh, axis=0, keepdims=True)

        d_lru_out, d_og = group_norm_bwd(lru_out, dm_ref[:, 0:c], pv[11:12])
        d_sc_out, d_sg = group_norm_bwd(sc_out, dm_ref[:, c:2 * c], pv[12:13])
        dpv_ref[11:12, :] += d_og
        dpv_ref[12:13, :] += d_sg

        dhs = d_lru_out * gelu_y
        dgelu = 0.5 * (1.0 + th) + 0.5 * y * (1.0 - th * th) * GELU_K * (1.0 + 3.0 * GELU_C * y * y)
        dy = d_lru_out * hs * dgelu

        acc_a = a
        acc_p = a * dhs
        dlt = 1
        while dlt < r:
            keep = lrow < r - dlt
            sh_a = pltpu.roll(acc_a, r - dlt, axis=0)
            sh_p = pltpu.roll(acc_p, r - dlt, axis=0)
            acc_p = acc_p + acc_a * jnp.where(keep, sh_p, 0.0)
            acc_a = acc_a * jnp.where(keep, sh_a, 1.0)
            dlt *= 2
        p_all = acc_p + acc_a * hp_ref[0:1, :]
        (p_next,) = _shifted(ext_ref, p_all, None, hp_ref[...], ups=(1,))
        hp_ref[...] = p_all[0:8]
        q = dhs + p_next
        (hs_prev,) = _shifted(ext_ref, hs, hsp_ref[...] * has_prev, None, downs=(1,))
        duu = q * maskf
        da = q * hs_prev

        dmult = duu * gxx
        dgxx = duu * mult
        dgx = dgxx * xc
        dxc = dgxx * gx
        dlog_a = da * a - dmult * ((1.0 + e2) / mult)
        dga = dlog_a * (-LRU_C * sp)
        dsp = jnp.sum(dlog_a * (-LRU_C * ga), axis=0, keepdims=True)
        dpv_ref[7:8, :] += dsp * (-jax.nn.sigmoid(-pv[7:8]))
        dga_pre = dga * ga * (1.0 - ga)
        dgx_pre = dgx * gx * (1.0 - gx)
        dpv_ref[5:6, :] += jnp.sum(dga_pre, axis=0, keepdims=True)
        dpv_ref[6:7, :] += jnp.sum(dgx_pre, axis=0, keepdims=True)
        dga_b = dga_pre.astype(MXU_DTYPE)
        dgx_b = dgx_pre.astype(MXU_DTYPE)
        dxc = dxc + _block_mm(dga_b, wa_ref, True) + _block_mm(dgx_b, wx_ref, True)
        for j in range(c // BD):
            sl = slice(j * BD, (j + 1) * BD)
            dwa_ref[j] += _dot_tn(xcb[:, sl], dga_b[:, sl])
            dwx_ref[j] += _dot_tn(xcb[:, sl], dgx_b[:, sl])

        dpv_ref[4:5, :] += jnp.sum(dxc, axis=0, keepdims=True)
        dpv_ref[3:4, :] += jnp.sum(dxc * xl, axis=0, keepdims=True)
        dpv_ref[2:3, :] += jnp.sum(dxc * x1, axis=0, keepdims=True)
        dpv_ref[1:2, :] += jnp.sum(dxc * x2, axis=0, keepdims=True)
        dpv_ref[0:1, :] += jnp.sum(dxc * x3, axis=0, keepdims=True)
        u1, u2, u3 = _shifted(ext_ref, dxc, None, hxc_ref[...], ups=(1, 2, 3))
        hxc_ref[...] = dxc[0:8]
        dxl = pv[3:4] * dxc + pv[2:3] * u1 + pv[1:2] * u2 + pv[0:1] * u3

        dbs = d_sc_out * sc
        dsc = d_sc_out * bs
        dpv_ref[10:11, :] += jnp.sum(dsc * cv, axis=0, keepdims=True)
        dpv_ref[9:10, :] += jnp.sum(dsc * c1, axis=0, keepdims=True)
        dpv_ref[8:9, :] += jnp.sum(dsc * c2, axis=0, keepdims=True)
        s1, s2 = _shifted(ext_ref, dsc, None, hsc_ref[...], ups=(1, 2))
        hsc_ref[...] = dsc[0:8]
        dcv = pv[10:11] * dsc + pv[9:10] * s1 + pv[8:9] * s2

        dz_ref[:, 0:c] = (dy * maskf).astype(dz_ref.dtype)
        dz_ref[:, c:2 * c] = (dxl * maskf).astype(dz_ref.dtype)
        dz_ref[:, 2 * c:3 * c] = (dbs * maskf).astype(dz_ref.dtype)
        dz_ref[:, 3 * c:4 * c] = (dcv * vs * maskf).astype(dz_ref.dtype)
        dz_ref[:, 4 * c:5 * c] = (dcv * cs * maskf).astype(dz_ref.dtype)

    full = lambda shape: pl.BlockSpec(shape, lambda i: (0,) * len(shape))
    cur = lambda width: pl.BlockSpec((r, width), lambda i: (nb - 1 - i, 0))
    prev8 = lambda width: pl.BlockSpec((8, width), lambda i: (jnp.maximum((nb - 1 - i) * r8 - 1, 0), 0))
    return _call(
        body, grid=(nb,),
        in_specs=[cur(5 * c), prev8(5 * c), cur(c), prev8(c), cur(2 * c),
                  full(pv.shape), full(wa.shape), full(wx.shape), full(gm.shape)],
        out_specs=[cur(5 * c), full(pv.shape), full(wa.shape), full(wx.shape)],
        out_shape=[jax.ShapeDtypeStruct((m, 5 * c), MXU_DTYPE), jax.ShapeDtypeStruct(pv.shape, F32),
                   jax.ShapeDtypeStruct(wa.shape, F32), jax.ShapeDtypeStruct(wx.shape, F32)],
        scratch_shapes=[pltpu.VMEM((r + 16, c), F32), pltpu.VMEM((8, c), F32), pltpu.VMEM((8, c), F32),
                        pltpu.VMEM((8, c), F32)],
        name=name, args=(z, z, hs, hs, dmixed, pv, wa, wx, gm), carried=carried)


def _position():
    return lax.axis_index("x"), lax.axis_index("y"), lax.axis_index("c")


def _block_of(px, py, pc):
    return 4 * px + 2 * py + pc


class _TwoLevelGather:
    def __init__(self, n_arrays, rows_of, src_of, send_sems, recv_sems):
        x, y, c = _position()
        self.n, self.rows_of, self.src_of = n_arrays, rows_of, src_of
        self.send_sems, self.recv_sems = send_sems, recv_sems
        self.c, self.me, self.sibling = c, (x, y, c), (x, y, 1 - c)
        self.chips = [(1 - x, y), (x, 1 - y), (1 - x, 1 - y)]

    def _copy(self, i, k, block, to, src=None):
        return pltpu.make_async_remote_copy(
            src_ref=self.rows_of(i, *block) if src is None else src, dst_ref=self.rows_of(i, *block),
            send_sem=self.send_sems.at[7 * i + k], recv_sem=self.recv_sems.at[7 * i + k],
            device_id=to, device_id_type=MESH)

    def _first(self, i):
        own = [self._copy(i, 0, self.me, self.sibling, src=self.src_of(i))]
        return own + [self._copy(i, 1 + j, self.me, (*chip, self.c), src=self.src_of(i))
                      for j, chip in enumerate(self.chips)]

    def _passed(self, i, j):
        return self._copy(i, 4 + j, (*self.chips[j], self.c), self.sibling)

    def start(self):
        for i in range(self.n):
            for cp in self._first(i):
                cp.start()

    def forward(self):
        for i in range(self.n):
            for j, chip in enumerate(self.chips):
                self._copy(i, 1 + j, (*chip, self.c), self.me).wait_recv()
                self._passed(i, j).start()

    def drain(self):
        for i in range(self.n):
            self._copy(i, 0, self.sibling, self.me).wait_recv()
            for j, chip in enumerate(self.chips):
                self._copy(i, 4 + j, (*chip, 1 - self.c), self.me).wait_recv()
        for i in range(self.n):
            for cp in self._first(i) + [self._passed(i, j) for j in range(3)]:
                cp.wait_send()


class _CarriedGather:
    def __init__(self, shards, padded_rows, zeros, forward_at):
        d = shards[0].shape[1]
        self.forward_at = forward_at
        self.n = len(shards)
        self.rows = [s.shape[0] for s in shards]
        self.pads = [p - N_DEV * r for r, p in zip(self.rows, padded_rows)]
        assert max(self.pads) <= zeros.shape[0] and zeros.shape[1] == d
        self.arrays = list(shards) + [zeros]
        self.out_shapes = [jax.ShapeDtypeStruct((p, d), s.dtype) for s, p in zip(shards, padded_rows)]
        self.aliases = {}
        self.n_remote, self.n_local = 7 * self.n, 2 * self.n
        self.results = None

    def _rows_of(self, outs):
        def rows_of(i, px, py, pc):
            s = self.rows[i]
            return outs[i].at[pl.ds(pl.multiple_of(_block_of(px, py, pc) * s, SUBLANE_BF16), s), :]
        return rows_of

    def _gather(self, ins, outs, send_sems, recv_sems):
        return _TwoLevelGather(self.n, self._rows_of(outs), lambda i: ins[i], send_sems, recv_sems)

    def _local(self, ins, outs, local_sems):
        x, y, c = _position()
        rows_of = self._rows_of(outs)
        cps = []
        for i in range(self.n):
            cps.append(pltpu.make_async_copy(ins[i], rows_of(i, x, y, c), local_sems.at[2 * i]))
            if self.pads[i]:
                cps.append(pltpu.make_async_copy(ins[self.n].at[pl.ds(0, self.pads[i]), :],
                                                 outs[i].at[pl.ds(N_DEV * self.rows[i], self.pads[i]), :],
                                                 local_sems.at[2 * i + 1]))
        return cps

    def start(self, ins, outs, send_sems, recv_sems, local_sems):
        for cp in self._local(ins, outs, local_sems):
            cp.start()
        self._gather(ins, outs, send_sems, recv_sems).start()

    def forward(self, ins, outs, send_sems, recv_sems, local_sems):
        self._gather(ins, outs, send_sems, recv_sems).forward()

    def finish(self, ins, outs, send_sems, recv_sems, local_sems):
        self._gather(ins, outs, send_sems, recv_sems).drain()
        for cp in self._local(ins, outs, local_sems):
            cp.wait()


class _CarriedSwap:
    def __init__(self, grads, shard_rows):
        d = grads[0].shape[1]
        self.n, self.rows = len(grads), list(shard_rows)
        self.arrays = list(grads)
        self.out_shapes = [jax.ShapeDtypeStruct((4, s, d), g.dtype) for g, s in zip(grads, shard_rows)]
        self.aliases = {}
        self.n_remote, self.n_local = 4 * self.n, 0
        self.forward_at = 1.0
        self.results = None

    def _copies(self, ins, outs, send_sems, recv_sems):
        x, y, c = _position()
        cps = []
        for i in range(self.n):
            s = self.rows[i]
            for k in range(4):
                blk = _block_of(k >> 1, k & 1, 1 - c)
                cps.append(pltpu.make_async_remote_copy(
                    src_ref=ins[i].at[pl.ds(pl.multiple_of(blk * s, SUBLANE_BF16), s), :], dst_ref=outs[i].at[k],
                    send_sem=send_sems.at[4 * i + k], recv_sem=recv_sems.at[4 * i + k],
                    device_id=(x, y, 1 - c), device_id_type=MESH))
        return cps

    def start(self, ins, outs, send_sems, recv_sems, local_sems):
        for cp in self._copies(ins, outs, send_sems, recv_sems):
            cp.start()

    def forward(self, *_):
        pass

    def finish(self, ins, outs, send_sems, recv_sems, local_sems):
        for cp in self._copies(ins, outs, send_sems, recv_sems):
            cp.wait()


class _CarriedChipExchange:
    def __init__(self, presums):
        self.n = len(presums)
        self.arrays = list(presums)
        self.out_shapes = [jax.ShapeDtypeStruct(p.shape, p.dtype) for p in presums]
        self.aliases = {}
        self.n_remote, self.n_local = 3 * self.n, 0
        self.forward_at = 1.0
        self.results = None

    def _copies(self, ins, outs, send_sems, recv_sems):
        x, y, c = _position()
        cps = []
        for i in range(self.n):
            for r in range(1, 4):
                cps.append(pltpu.make_async_remote_copy(
                    src_ref=ins[i].at[r - 1], dst_ref=outs[i].at[r - 1],
                    send_sem=send_sems.at[3 * i + r - 1], recv_sem=recv_sems.at[3 * i + r - 1],
                    device_id=(x ^ (r >> 1), y ^ (r & 1), c), device_id_type=MESH))
        return cps

    def start(self, ins, outs, send_sems, recv_sems, local_sems):
        for cp in self._copies(ins, outs, send_sems, recv_sems):
            cp.start()

    def forward(self, *_):
        pass

    def finish(self, ins, outs, send_sems, recv_sems, local_sems):
        for cp in self._copies(ins, outs, send_sems, recv_sems):
            cp.wait()


def _gather_small(block, reduce, name):
    rr, nn = block.shape

    def body(x_ref, out_ref, *rest):
        if reduce:
            stack_ref, send_sems, recv_sems, local_sem = rest
        else:
            send_sems, recv_sems, local_sem = rest
            stack_ref = out_ref
        x, y, c = _position()

        def rows_of(i, px, py, pc):
            return stack_ref.at[pl.ds(pl.multiple_of(_block_of(px, py, pc) * rr, 8), rr), :]

        own = pltpu.make_async_copy(x_ref, rows_of(0, x, y, c), local_sem)
        own.start()
        gather = _TwoLevelGather(1, rows_of, lambda i: x_ref, send_sems, recv_sems)
        gather.start()
        gather.forward()
        gather.drain()
        own.wait()
        if reduce:
            acc = stack_ref[0:rr, :]
            for k in range(1, N_DEV):
                acc = acc + stack_ref[k * rr:(k + 1) * rr, :]
            out_ref[...] = acc

    vmem = pl.BlockSpec(memory_space=pltpu.VMEM)
    scratch = [pltpu.SemaphoreType.DMA((7,)), pltpu.SemaphoreType.DMA((7,)), pltpu.SemaphoreType.DMA]
    if reduce:
        scratch = [pltpu.VMEM((N_DEV * rr, nn), F32)] + scratch
    out_rows = rr if reduce else N_DEV * rr
    return pl.pallas_call(
        body, in_specs=[vmem], out_specs=vmem, out_shape=jax.ShapeDtypeStruct((out_rows, nn), F32),
        scratch_shapes=scratch, name=name, compiler_params=_params())(block)


def _presum(where, grad, swapped, name):
    s, d = swapped.shape[1], swapped.shape[2]
    tc = _tile(d, 512, LANE)

    def body(where_ref, g_ref, sw_ref, o_ref):
        o_ref[0] = (g_ref[...].astype(F32) + sw_ref[0].astype(F32)).astype(o_ref.dtype)

    return _call(
        body, grid=(3, d // tc),
        in_specs=[pl.BlockSpec((s, tc), lambda r, j, where: (where[1 + r], j)),
                  pl.BlockSpec((1, s, tc), lambda r, j, where: (where[5 + r], 0, j))],
        out_specs=pl.BlockSpec((1, s, tc), lambda r, j, where: (r, 0, j)),
        out_shape=jax.ShapeDtypeStruct((3, s, d), WIRE_DTYPE), name=name, args=(grad, swapped), prefetch=(where,))


def _final_sum(where, grad, swapped, received, name, carried=()):
    s, d = swapped.shape[1], swapped.shape[2]
    tc = _tile(d, 512, LANE)

    def body(where_ref, g_ref, sw_ref, r_ref, o_ref):
        acc = g_ref[...].astype(F32) + sw_ref[0].astype(F32)
        for k in range(3):
            acc = acc + r_ref[k].astype(F32)
        o_ref[...] = acc

    return _call(
        body, grid=(d // tc,),
        in_specs=[pl.BlockSpec((s, tc), lambda j, where: (where[0], j)),
                  pl.BlockSpec((1, s, tc), lambda j, where: (where[4], 0, j)),
                  pl.BlockSpec((3, s, tc), lambda j, where: (0, 0, j))],
        out_specs=pl.BlockSpec((s, tc), lambda j, where: (0, j)),
        out_shape=jax.ShapeDtypeStruct((s, d), F32), name=name, args=(grad, swapped, received),
        prefetch=(where,), carried=carried)


class _GradReduction:
    def __init__(self, key, grad, shard_rows, where):
        self.key, self.grad, self.rows, self.where = key, grad, shard_rows, where

    def swap(self):
        self._swap = _CarriedSwap([self.grad], [self.rows])
        return self._swap

    def exchange(self):
        presum = _presum(self.where, self.grad, self._swap.results[0], "presum_" + self.key)
        self._exchange = _CarriedChipExchange([presum])
        return self._exchange

    def total(self, carried=()):
        return _final_sum(self.where, self.grad, self._swap.results[0], self._exchange.results[0],
                          "sum_" + self.key, carried)

    def total_and_update(self, w, m, v):
        return _sum_adamw(self.where, self.grad, self._swap.results[0], self._exchange.results[0], w, m, v,
                          "update_" + self.key)


def _adamw_math(w, g, m, v):
    nm = ADAM_B1 * m + (1.0 - ADAM_B1) * g
    nv = ADAM_B2 * v + (1.0 - ADAM_B2) * (g * g)
    m_hat = nm / (1.0 - ADAM_B1 ** ADAM_STEP)
    v_hat = nv / (1.0 - ADAM_B2 ** ADAM_STEP)
    return -ADAM_LR * (m_hat / (jnp.sqrt(v_hat) + ADAM_EPS) + ADAM_WD * w), nm, nv


def _sum_adamw(where, grad, swapped, received, w, m, v, name):
    s, d = swapped.shape[1], swapped.shape[2]
    tc = _tile(d, 512, LANE)

    def body(where_ref, g_ref, sw_ref, r_ref, w_ref, m_ref, v_ref, gs_ref, d_ref, nm_ref, nv_ref):
        g = g_ref[...].astype(F32) + sw_ref[0].astype(F32)
        for k in range(3):
            g = g + r_ref[k].astype(F32)
        gs_ref[...] = g
        d_ref[...], nm_ref[...], nv_ref[...] = _adamw_math(w_ref[...], g, m_ref[...], v_ref[...])

    blk = pl.BlockSpec((s, tc), lambda j, where: (0, j))
    return _call(
        body, grid=(d // tc,),
        in_specs=[pl.BlockSpec((s, tc), lambda j, where: (where[0], j)),
                  pl.BlockSpec((1, s, tc), lambda j, where: (where[4], 0, j)),
                  pl.BlockSpec((3, s, tc), lambda j, where: (0, 0, j)), blk, blk, blk],
        out_specs=[blk] * 4, out_shape=[jax.ShapeDtypeStruct((s, d), F32)] * 4, name=name,
        args=(grad, swapped, received, w, m, v), prefetch=(where,))


def _adamw(w, g, m, v, name):
    rows, cols = w.shape
    tr = _tile(rows, 256, 8)

    def body(w_ref, g_ref, m_ref, v_ref, d_ref, nm_ref, nv_ref):
        d_ref[...], nm_ref[...], nv_ref[...] = _adamw_math(w_ref[...], g_ref[...], m_ref[...], v_ref[...])

    spec = pl.BlockSpec((tr, cols), lambda i: (i, 0))
    return pl.pallas_call(
        body, grid=(rows // tr,), in_specs=[spec] * 4, out_specs=[spec] * 3,
        out_shape=[jax.ShapeDtypeStruct((rows, cols), F32)] * 3, name=name, compiler_params=_params())(w, g, m, v)


def _pack_rows(arrays, width, row_quantum=8):
    flat = jnp.concatenate([a.reshape(-1) for a in arrays])
    total = _round_up(flat.shape[0], row_quantum * width)
    flat = jnp.pad(flat, (0, total - flat.shape[0]))
    return flat.reshape(-1, width)


def _unpack_rows(packed, shapes):
    flat = packed.reshape(-1)
    out = []
    off = 0
    for shp in shapes:
        size = 1
        for s in shp:
            size *= s
        out.append(flat[off:off + size].reshape(shp))
        off += size
    return out


def _block_diag(w):
    h, hb, _ = w.shape
    per = BD // hb
    w4 = w.reshape(h // per, per, hb, hb)
    eye = jnp.eye(per, dtype=w.dtype)
    return jnp.einsum('npij,pq->npiqj', w4, eye).reshape(h // per, BD, BD)


def _block_diag_extract(bd, hb):
    nbk = bd.shape[0]
    per = BD // hb
    b5 = bd.reshape(nbk, per, hb, per, hb)
    eye = jnp.eye(per, dtype=bd.dtype)
    return jnp.einsum('npiqj,pq->npij', b5, eye).reshape(nbk * per, hb, hb)


def kernel(x, meta_tokens, ffn1_pre_g, ffn1_w_gate, ffn1_w_up, ffn1_w_down, ffn1_post_g, mix_pre_g, w_in, lru_conv_w, lru_conv_b, lru_w_a, lru_b_a, lru_w_x, lru_b_x, lru_lambda, sconv_w, lru_out_g, sconv_out_g, w_out, mix_post_g, ffn2_pre_g, ffn2_w_gate, ffn2_w_up, ffn2_w_down, ffn2_post_g, loss_target, m_meta_tokens, m_ffn1_pre_g, m_ffn1_w_gate, m_ffn1_w_up, m_ffn1_w_down, m_ffn1_post_g, m_mix_pre_g, m_w_in, m_lru_conv_w, m_lru_conv_b, m_lru_w_a, m_lru_b_a, m_lru_w_x, m_lru_b_x, m_lru_lambda, m_sconv_w, m_lru_out_g, m_sconv_out_g, m_w_out, m_mix_post_g, m_ffn2_pre_g, m_ffn2_w_gate, m_ffn2_w_up, m_ffn2_w_down, m_ffn2_post_g, v_meta_tokens, v_ffn1_pre_g, v_ffn1_w_gate, v_ffn1_w_up, v_ffn1_w_down, v_ffn1_post_g, v_mix_pre_g, v_w_in, v_lru_conv_w, v_lru_conv_b, v_lru_w_a, v_lru_b_a, v_lru_w_x, v_lru_b_x, v_lru_lambda, v_sconv_w, v_lru_out_g, v_sconv_out_g, v_w_out, v_mix_post_g, v_ffn2_pre_g, v_ffn2_w_gate, v_ffn2_w_up, v_ffn2_w_down, v_ffn2_post_g):
    given = dict(locals())
    wts = {n: given[n] for n in WEIGHT_NAMES}
    mom = {n: given["m_" + n] for n in WEIGHT_NAMES}
    var = {n: given["v_" + n] for n in WEIGHT_NAMES}

    xi, yi, ci = _position()
    me = _block_of(xi, yi, ci)
    x2 = x[0]
    seq, d = x2.shape
    n_meta = meta_tokens.shape[0]
    m_rows = _round_up(n_meta + seq, ROW_ALIGN)
    pad = m_rows - n_meta - seq
    lead = pad + n_meta
    c = lru_conv_b.shape[1]
    hb = lru_w_a.shape[-1]
    dm = meta_tokens.shape[1]
    cs_ = lru_conv_w.shape[2]
    kw4, kw3 = lru_conv_w.shape[1], sconv_w.shape[1]
    assert d == 2 * c and c % BD == 0 and BD % hb == 0 and cs_ <= dm and kw4 == 4 and kw3 == 3

    small = jnp.zeros((_round_up(n_meta + kw4 + kw3, 8), dm), F32)
    small = small.at[0:n_meta].set(meta_tokens)
    small = small.at[n_meta:n_meta + kw4, 0:cs_].set(lru_conv_w[0])
    small = small.at[n_meta + kw4:n_meta + kw4 + kw3, 0:cs_].set(sconv_w[0])
    sr = small.shape[0]
    small_all = _gather_small(small, False, "gather_small").reshape(N_DEV, sr, dm)
    meta_full = small_all[:, 0:n_meta, :].transpose(1, 0, 2).reshape(n_meta, d)
    conv_w_full = small_all[:, n_meta:n_meta + kw4, 0:cs_].transpose(1, 0, 2).reshape(kw4, c)
    sconv_w_full = small_all[:, n_meta + kw4:n_meta + kw4 + kw3, 0:cs_].transpose(1, 0, 2).reshape(kw3, c)

    big = ['ffn1_w_gate', 'ffn1_w_up', 'ffn1_w_down', 'w_in', 'w_out', 'ffn2_w_gate', 'ffn2_w_up', 'ffn2_w_down']
    col_sharded = {'ffn1_w_gate', 'ffn1_w_up', 'w_in', 'ffn2_w_gate', 'ffn2_w_up'}
    shards = []
    for nme in big:
        w = wts[nme][0].astype(WIRE_DTYPE)
        shards.append(w.T if nme in col_sharded else w)
    shard_rows = dict(zip(big, [s.shape[0] for s in shards]))
    zeros = jnp.zeros((F_ALIGN, d), WIRE_DTYPE)

    def gather(forward_at, *names):
        sel = [shards[big.index(nme)] for nme in names]
        padded = [_round_up(N_DEV * shard_rows[nme], LANE if nme in ('w_in', 'w_out') else F_ALIGN) for nme in names]
        return _CarriedGather(sel, padded, zeros, forward_at)

    pv = jnp.zeros((16, c), F32)
    pv = pv.at[0:4].set(conv_w_full).at[4].set(lru_conv_b[0]).at[5].set(lru_b_a[0]).at[6].set(lru_b_x[0])
    pv = pv.at[7].set(lru_lambda[0]).at[8:11].set(sconv_w_full).at[11].set(lru_out_g[0]).at[12].set(sconv_out_g[0])
    wa_bd = _block_diag(lru_w_a[0]).astype(MXU_DTYPE)
    wx_bd = _block_diag(lru_w_x[0]).astype(MXU_DTYPE)
    gs = c // N_GROUPS
    gidx = jnp.arange(BD) // gs
    gm = jnp.where(gidx[:, None] == gidx[None, :], 1.0 / gs, 0.0).astype(MXU_DTYPE)

    h0 = jnp.concatenate([jnp.zeros((pad, d), F32), meta_full, x2], axis=0)
    ride = gather(1.0, 'ffn1_w_gate', 'ffn1_w_up')
    n1 = _rmsnorm(h0, ffn1_pre_g, "ffn1_prenorm", carried=[ride])
    wg1, wu1 = ride.results
    ride = gather(0.85, 'ffn1_w_down', 'w_out')
    g1, u1, a1 = _ffn_gate_up(n1, wg1, wu1, "ffn1_gate_up", carried=[ride])
    wd1, wout = ride.results
    ride = gather(0.8, 'w_in')
    fo1, h1, un = _mm_residual_norm(a1, wd1, h0, ffn1_post_g, 0.5, mix_pre_g, "ffn1_down", carried=[ride])
    (win_t,) = ride.results
    ride = gather(1.0, 'ffn2_w_gate')
    z = _mm_nt(un, win_t, "mix_in_proj", carried=[ride])
    (wg2,) = ride.results
    ride = gather(1.0, 'ffn2_w_up')
    mixed, hs = _mixer_fwd(z, pv, wa_bd, wx_bd, gm, pad, "mixer_fwd", carried=[ride])
    (wu2,) = ride.results
    o_mix, h2, n2 = _mm_residual_norm(mixed, wout, h1, mix_post_g, 1.0, ffn2_pre_g, "mix_out_proj")
    ride = gather(0.75, 'ffn2_w_down')
    g2, u2, a2 = _ffn_gate_up(n2, wg2, wu2, "ffn2_gate_up", carried=[ride])
    (wd2,) = ride.results
    target = jnp.concatenate([jnp.zeros((lead, d), F32), loss_target[0]], axis=0)
    dh3, dfo2, d_post2, loss_part = _mm_residual_loss(a2, wd2, h2, ffn2_post_g, 0.5, target, lead, "ffn2_down_loss")
    loss = lax.psum(loss_part[0, 0], ("x", "y", "c"))

    chip_rel = [2 * (xi ^ (r >> 1)) + (yi ^ (r & 1)) for r in range(4)]
    where = jnp.stack([2 * k + ci for k in chip_rel] + chip_rel).astype(jnp.int32)
    red = {}

    def reduction(nme, grad):
        red[nme] = _GradReduction(nme, grad, shard_rows[nme], where)
        return red[nme]

    r_wd2 = reduction('ffn2_w_down', _mm_tn(a2, dfo2, "ffn2_dw_down"))
    dg2, du2 = _ffn_hidden_bwd(dfo2, wd2, g2, u2, "ffn2_hidden_bwd", carried=[r_wd2.swap()])
    r_wg2 = reduction('ffn2_w_gate', _mm_tn(dg2, n2, "ffn2_dw_gate", carried=[r_wd2.exchange()]))
    r_wu2 = reduction('ffn2_w_up', _mm_tn(du2, n2, "ffn2_dw_up", carried=[r_wg2.swap()]))
    dh2, d_pre2 = _mm_norm_bwd([(dg2, wg2), (du2, wu2)], h2, ffn2_pre_g, dh3, "ffn2_dx",
                               carried=[r_wg2.exchange(), r_wu2.swap()])
    do_mix, d_mix_post = _norm_bwd(o_mix, mix_post_g, dh2, 1.0, "mix_postnorm_bwd")
    dmixed = _mm_nt(do_mix, wout, "mix_out_proj_bwd")
    r_wout = reduction('w_out', _mm_tn(mixed, do_mix, "mix_dw_out"))
    dz, dpv, dwa_bd, dwx_bd = _mixer_bwd(z, hs, dmixed, pv, wa_bd, wx_bd, gm, pad, "mixer_bwd",
                                         carried=[r_wu2.exchange(), r_wout.swap()])
    r_win = reduction('w_in', _mm_tn(dz, un, "mix_dw_in", carried=[r_wout.exchange()]))
    dh1, d_mix_pre = _mm_norm_bwd([(dz, win_t)], h1, mix_pre_g, dh2, "mix_dx", carried=[r_win.swap()])
    dfo1, d_post1 = _norm_bwd(fo1, ffn1_post_g, dh1, 0.5, "ffn1_postnorm_bwd")
    r_wd1 = reduction('ffn1_w_down', _mm_tn(a1, dfo1, "ffn1_dw_down", carried=[r_win.exchange()]))
    dg1, du1 = _ffn_hidden_bwd(dfo1, wd1, g1, u1, "ffn1_hidden_bwd", carried=[r_wd1.swap()])
    r_wg1 = reduction('ffn1_w_gate', _mm_tn(dg1, n1, "ffn1_dw_gate", carried=[r_wd1.exchange()]))
    r_wu1 = reduction('ffn1_w_up', _mm_tn(du1, n1, "ffn1_dw_up", carried=[r_wg1.swap()]))
    row_tile = _norm_bwd_row_tile(m_rows)
    n_tiles = m_rows // row_tile
    half = n_tiles // 2
    assert half >= 1 and half * row_tile >= lead
    dh0_a, d_pre1_a = _mm_norm_bwd([(dg1, wg1), (du1, wu1)], h0, ffn1_pre_g, dh1, "ffn1_dx_a",
                                   carried=[r_wg1.exchange(), r_wu1.swap()], row_tiles=(0, half))
    dh0_b, d_pre1 = _mm_norm_bwd([(dg1, wg1), (du1, wu1)], h0, ffn1_pre_g, dh1, "ffn1_dx_b",
                                 carried=[r_wu1.exchange()], row_tiles=(half, n_tiles - half), dg_init=d_pre1_a)
    grad_x = jnp.concatenate([dh0_a[lead:], dh0_b], axis=0)[None]
    d_meta = dh0_a[pad:lead]

    grads, delta, new_m, new_v = {}, {}, {}, {}
    for nme in big:
        in_shard_layout = nme not in col_sharded or shard_rows[nme] % LANE != 0
        if in_shard_layout:
            view = (lambda t: t[0].T) if nme in col_sharded else (lambda t: t[0])
            back = (lambda t: t.T[None]) if nme in col_sharded else (lambda t: t[None])
            outs = red[nme].total_and_update(view(wts[nme]), view(mom[nme]), view(var[nme]))
            grads[nme], delta[nme], new_m[nme], new_v[nme] = [back(t) for t in outs]
        else:
            grads[nme] = red[nme].total().T[None]
            outs = _adamw(wts[nme][0], grads[nme][0], mom[nme][0], var[nme][0], "adamw_" + nme)
            delta[nme], new_m[nme], new_v[nme] = [t[None] for t in outs]

    small_names = ['ffn1_pre_g', 'ffn1_post_g', 'mix_pre_g', 'mix_post_g', 'ffn2_pre_g', 'ffn2_post_g',
                   'lru_conv_b', 'lru_b_a', 'lru_b_x', 'lru_lambda', 'lru_out_g', 'sconv_out_g',
                   'lru_conv_w', 'sconv_w', 'lru_w_a', 'lru_w_x', 'meta_tokens']
    small_parts = [d_pre1, d_post1, d_mix_pre, d_mix_post, d_pre2, d_post2,
                   dpv[4:5], dpv[5:6], dpv[6:7], dpv[7:8], dpv[11:12], dpv[12:13],
                   dpv[0:4], dpv[8:11], _block_diag_extract(dwa_bd, hb), _block_diag_extract(dwx_bd, hb),
                   d_meta]
    small_shapes = [p.shape for p in small_parts]
    small_sum = _gather_small(_pack_rows(small_parts, d), True, "reduce_small")
    for nme, gsm in zip(small_names, _unpack_rows(small_sum, small_shapes)):
        if nme == 'meta_tokens':
            grads[nme] = lax.dynamic_slice_in_dim(gsm, me * dm, dm, axis=1)
        elif nme in ('lru_conv_w', 'sconv_w'):
            grads[nme] = lax.dynamic_slice_in_dim(gsm, me * cs_, cs_, axis=1)[None]
        else:
            grads[nme] = gsm.reshape(wts[nme].shape)

    rest = [n for n in WEIGHT_NAMES if n not in big]
    rest_shapes = [wts[n].shape for n in rest]
    packed = [_pack_rows([src[n] for n in rest], LANE, 256) for src in (wts, grads, mom, var)]
    for out, packed_out in zip((delta, new_m, new_v), _adamw(*packed, "adamw_small")):
        for nme, arr in zip(rest, _unpack_rows(packed_out, rest_shapes)):
            out[nme] = arr

    return (loss, grad_x, *[grads[n] for n in WEIGHT_NAMES], *[delta[n] for n in WEIGHT_NAMES],
            *[new_m[n] for n in WEIGHT_NAMES], *[new_v[n] for n in WEIGHT_NAMES])
```

```python
import functools

import jax
import jax.numpy as jnp
from jax import lax
from jax.experimental import pallas as pl
from jax.experimental.pallas import tpu as pltpu

F32 = jnp.float32
MXU_DTYPE = jnp.bfloat16
WIRE_DTYPE = jnp.bfloat16
MESH = pl.DeviceIdType.MESH

EPS = 1e-6
LRU_C = 8.0
N_GROUPS = 16
ADAM_LR = 0.001
ADAM_B1 = 0.9
ADAM_B2 = 0.999
ADAM_EPS = 1e-08
ADAM_WD = 0.01
ADAM_STEP = 10

N_DEV = 8
LANE = 128
SUBLANE_BF16 = 16
ROW_ALIGN = 128
F_ALIGN = 512
BD = 256
K_TILE = 512
ACC_ROWS = 528
ACC_GROUP = 4
MIX_ROWS = 128
VMEM_LIMIT_MB = 56

WEIGHT_NAMES = ['meta_tokens', 'ffn1_pre_g', 'ffn1_w_gate', 'ffn1_w_up', 'ffn1_w_down', 'ffn1_post_g',
                'mix_pre_g', 'w_in', 'lru_conv_w', 'lru_conv_b', 'lru_w_a', 'lru_b_a', 'lru_w_x', 'lru_b_x',
                'lru_lambda', 'sconv_w', 'lru_out_g', 'sconv_out_g', 'w_out', 'mix_post_g', 'ffn2_pre_g',
                'ffn2_w_gate', 'ffn2_w_up', 'ffn2_w_down', 'ffn2_post_g']


def _round_up(n, q):
    return (n + q - 1) // q * q


def _tile(n, target, q):
    best = None
    t = q
    while t <= min(n, target):
        if n % t == 0:
            best = t
        t += q
    assert best is not None, (n, target, q)
    return best


def _params(**kw):
    return pltpu.CompilerParams(vmem_limit_bytes=VMEM_LIMIT_MB << 20, **kw)


def _call(body, *, grid, in_specs, out_specs, out_shape, name, args, scratch_shapes=(), carried=(), prefetch=()):
    carried = list(carried)
    n_pf = len(prefetch)

    def launch(fn, in_specs_, out_specs_, out_shape_, scratch_, operands, aliases_):
        if n_pf:
            spec = pltpu.PrefetchScalarGridSpec(num_scalar_prefetch=n_pf, grid=grid, in_specs=in_specs_,
                                                out_specs=out_specs_, scratch_shapes=scratch_)
            return pl.pallas_call(fn, grid_spec=spec, out_shape=out_shape_, input_output_aliases=aliases_,
                                  name=name, compiler_params=_params())(*prefetch, *operands)
        return pl.pallas_call(fn, grid=grid, in_specs=in_specs_, out_specs=out_specs_, out_shape=out_shape_,
                              scratch_shapes=scratch_, input_output_aliases=aliases_, name=name,
                              compiler_params=_params())(*operands)

    if not carried:
        return launch(body, in_specs, out_specs, out_shape, list(scratch_shapes), args, {})
    single = not isinstance(out_shape, (list, tuple))
    out_specs_l = [out_specs] if single else list(out_specs)
    out_shape_l = [out_shape] if single else list(out_shape)
    n_in, n_out, n_scr = len(in_specs), len(out_specs_l), len(scratch_shapes)
    hbm = pl.BlockSpec(memory_space=pl.ANY)
    c_in = [a for cm in carried for a in cm.arrays]
    c_out = [s for cm in carried for s in cm.out_shapes]
    c_scr = []
    aliases = {}
    in_off, out_off = n_pf + n_in, n_out
    for cm in carried:
        c_scr += [pltpu.SemaphoreType.DMA((cm.n_remote,)), pltpu.SemaphoreType.DMA((cm.n_remote,)),
                  pltpu.SemaphoreType.DMA((max(cm.n_local, 1),))]
        for k, v in cm.aliases.items():
            aliases[in_off + k] = out_off + v
        in_off += len(cm.arrays)
        out_off += len(cm.out_shapes)
    steps = 1
    for g in grid:
        steps *= g
    forward_steps = [min(int(cm.forward_at * steps), steps - 1) for cm in carried]

    def wrapped(*refs):
        pf = refs[:n_pf]
        p = n_pf
        ins = refs[p:p + n_in]
        p += n_in
        cins = refs[p:p + len(c_in)]
        p += len(c_in)
        outs = refs[p:p + n_out]
        p += n_out
        couts = refs[p:p + len(c_out)]
        p += len(c_out)
        scr = refs[p:p + n_scr]
        csem = refs[p + n_scr:]
        lin = 0
        for axis, g in enumerate(grid):
            lin = lin * g + pl.program_id(axis)
        views = []
        io = oo = 0
        for j, cm in enumerate(carried):
            views.append((cins[io:io + len(cm.arrays)], couts[oo:oo + len(cm.out_shapes)],
                          csem[3 * j], csem[3 * j + 1], csem[3 * j + 2]))
            io += len(cm.arrays)
            oo += len(cm.out_shapes)

        @pl.when(lin == 0)
        def _():
            for cm, v in zip(carried, views):
                cm.start(*v)

        body(*pf, *ins, *outs, *scr)

        for cm, v, step in zip(carried, views, forward_steps):
            pl.when(lin == step)(functools.partial(cm.forward, *v))

        @pl.when(lin == steps - 1)
        def _():
            for cm, v in zip(carried, views):
                cm.finish(*v)

    res = launch(wrapped, list(in_specs) + [hbm] * len(c_in), out_specs_l + [hbm] * len(c_out),
                 out_shape_l + c_out, list(scratch_shapes) + c_scr, (*args, *c_in), aliases)
    oo = n_out
    for cm in carried:
        cm.results = list(res[oo:oo + len(cm.out_shapes)])
        oo += len(cm.out_shapes)
    return res[0] if single else list(res[:n_out])


def _rmsnorm(h, g, name, carried=()):
    m, d = h.shape
    tm = _tile(m, 528, SUBLANE_BF16)

    def body(h_ref, g_ref, o_ref):
        x = h_ref[...]
        r = lax.rsqrt(jnp.mean(x * x, axis=-1, keepdims=True) + EPS)
        o_ref[...] = (x * r * g_ref[...]).astype(o_ref.dtype)

    return _call(
        body, grid=(m // tm,),
        in_specs=[pl.BlockSpec((tm, d), lambda i: (i, 0)), pl.BlockSpec((1, d), lambda i: (0, 0))],
        out_specs=pl.BlockSpec((tm, d), lambda i: (i, 0)),
        out_shape=jax.ShapeDtypeStruct((m, d), MXU_DTYPE), name=name, args=(h, g), carried=carried)


def _rmsnorm_bwd_rows(x, g, dy):
    r = lax.rsqrt(jnp.mean(x * x, axis=-1, keepdims=True) + EPS)
    xh = x * r
    dyh = dy * g
    dx = r * (dyh - xh * jnp.mean(dyh * xh, axis=-1, keepdims=True))
    return dx, dy * xh


def _norm_bwd(x, g, dy, scale, name, carried=()):
    m, d = x.shape
    tm = _tile(m, 528, SUBLANE_BF16)

    def body(x_ref, g_ref, dy_ref, dx_ref, dg_ref):
        @pl.when(pl.program_id(0) == 0)
        def _():
            dg_ref[...] = jnp.zeros_like(dg_ref)

        dx, dgr = _rmsnorm_bwd_rows(x_ref[...], g_ref[...], scale * dy_ref[...])
        dx_ref[...] = dx.astype(dx_ref.dtype)
        dg_ref[...] += jnp.sum(dgr, axis=0, keepdims=True)

    return _call(
        body, grid=(m // tm,),
        in_specs=[pl.BlockSpec((tm, d), lambda i: (i, 0)), pl.BlockSpec((1, d), lambda i: (0, 0)),
                  pl.BlockSpec((tm, d), lambda i: (i, 0))],
        out_specs=[pl.BlockSpec((tm, d), lambda i: (i, 0)), pl.BlockSpec((1, d), lambda i: (0, 0))],
        out_shape=[jax.ShapeDtypeStruct((m, d), MXU_DTYPE), jax.ShapeDtypeStruct((1, d), F32)],
        name=name, args=(x, g, dy), carried=carried)


def _dot_nt(a, b):
    return lax.dot_general(a, b, (((1,), (1,)), ((), ())), preferred_element_type=F32)


def _dot_tn(a, b):
    return lax.dot_general(a, b, (((0,), (0,)), ((), ())), preferred_element_type=F32)


def _mm_nt(a, w, name, carried=()):
    m, k = a.shape
    n = w.shape[0]
    tm = _tile(m, 1056, SUBLANE_BF16)
    tn = _tile(n, 512, LANE)

    def body(a_ref, w_ref, o_ref):
        o_ref[...] = _dot_nt(a_ref[...], w_ref[...])

    return _call(
        body, grid=(m // tm, n // tn),
        in_specs=[pl.BlockSpec((tm, k), lambda i, j: (i, 0)), pl.BlockSpec((tn, k), lambda i, j: (j, 0))],
        out_specs=pl.BlockSpec((tm, tn), lambda i, j: (i, j)),
        out_shape=jax.ShapeDtypeStruct((m, n), F32), name=name, args=(a, w), carried=carried)


def _ffn_gate_up(n_act, wg_t, wu_t, name, carried=()):
    m, d = n_act.shape
    fp = wg_t.shape[0]
    tm = _tile(m, 1056, SUBLANE_BF16)
    tn = _tile(fp, 512, LANE)

    def body(n_ref, wg_ref, wu_ref, g_ref, u_ref, a_ref):
        n = n_ref[...]
        g = _dot_nt(n, wg_ref[...])
        u = _dot_nt(n, wu_ref[...])
        g_ref[...] = g.astype(g_ref.dtype)
        u_ref[...] = u.astype(u_ref.dtype)
        a_ref[...] = (g * jax.nn.sigmoid(g) * u).astype(a_ref.dtype)

    act = pl.BlockSpec((tm, tn), lambda i, j: (i, j))
    wsp = pl.BlockSpec((tn, d), lambda i, j: (j, 0))
    return _call(
        body, grid=(m // tm, fp // tn),
        in_specs=[pl.BlockSpec((tm, d), lambda i, j: (i, 0)), wsp, wsp],
        out_specs=[act, act, act],
        out_shape=[jax.ShapeDtypeStruct((m, fp), MXU_DTYPE)] * 3, name=name, args=(n_act, wg_t, wu_t), carried=carried)


def _ffn_hidden_bwd(dfo, wd, g_act, u_act, name, carried=()):
    m, d = dfo.shape
    fp = wd.shape[0]
    tm = _tile(m, 1056, SUBLANE_BF16)
    tn = _tile(fp, 512, LANE)

    def body(df_ref, wd_ref, g_ref, u_ref, dg_ref, du_ref):
        da = _dot_nt(df_ref[...], wd_ref[...])
        g = g_ref[...].astype(F32)
        u = u_ref[...].astype(F32)
        s = jax.nn.sigmoid(g)
        du_ref[...] = (da * (g * s)).astype(du_ref.dtype)
        dg_ref[...] = (da * u * (s * (1.0 + g * (1.0 - s)))).astype(dg_ref.dtype)

    act = pl.BlockSpec((tm, tn), lambda i, j: (i, j))
    return _call(
        body, grid=(m // tm, fp // tn),
        in_specs=[pl.BlockSpec((tm, d), lambda i, j: (i, 0)), pl.BlockSpec((tn, d), lambda i, j: (j, 0)), act, act],
        out_specs=[act, act],
        out_shape=[jax.ShapeDtypeStruct((m, fp), MXU_DTYPE)] * 2, name=name, args=(dfo, wd, g_act, u_act),
        carried=carried)


def _row_groups(n_tiles, max_group, nk):
    gsz = max(q for q in range(1, max_group + 1) if n_tiles % q == 0)

    def epilogue_row(grp, kk, i):
        return grp * gsz + jnp.where(kk == nk - 1, i, 0)

    return gsz, epilogue_row


def _mm_residual_norm(a, w, h, g, scale, next_g, name, carried=()):
    m, k = a.shape
    d = w.shape[1]
    tm = _tile(m, ACC_ROWS, SUBLANE_BF16)
    tk = _tile(k, K_TILE, LANE)
    nk = k // tk
    gsz, epilogue_row = _row_groups(m // tm, 2, nk)

    def body(a_ref, w_ref, h_ref, g_ref, ng_ref, fo_ref, hn_ref, nn_ref, acc_ref):
        kk, i = pl.program_id(1), pl.program_id(2)

        @pl.when(kk == 0)
        def _():
            acc_ref[i] = jnp.zeros((tm, d), F32)

        acc_ref[i] += jnp.dot(a_ref[...], w_ref[...], preferred_element_type=F32)

        @pl.when(kk == nk - 1)
        def _():
            fo = acc_ref[i]
            fo_ref[...] = fo
            r = lax.rsqrt(jnp.mean(fo * fo, axis=-1, keepdims=True) + EPS)
            hn = h_ref[...] + scale * (fo * r * g_ref[...])
            hn_ref[...] = hn
            rn = lax.rsqrt(jnp.mean(hn * hn, axis=-1, keepdims=True) + EPS)
            nn_ref[...] = (hn * rn * ng_ref[...]).astype(nn_ref.dtype)

    row = pl.BlockSpec((tm, d), lambda grp, kk, i: (epilogue_row(grp, kk, i), 0))
    row_once = pl.BlockSpec((tm, d), lambda grp, kk, i: (epilogue_row(grp, kk, i), 0), pipeline_mode=pl.Buffered(1))
    vec = pl.BlockSpec((1, d), lambda grp, kk, i: (0, 0))
    return _call(
        body, grid=(m // tm // gsz, nk, gsz),
        in_specs=[pl.BlockSpec((tm, tk), lambda grp, kk, i: (grp * gsz + i, kk)),
                  pl.BlockSpec((tk, d), lambda grp, kk, i: (kk, 0)), row_once, vec, vec],
        out_specs=[row, row, row],
        out_shape=[jax.ShapeDtypeStruct((m, d), F32)] * 2 + [jax.ShapeDtypeStruct((m, d), MXU_DTYPE)],
        scratch_shapes=[pltpu.VMEM((gsz, tm, d), F32)], name=name, args=(a, w, h, g, next_g), carried=carried)


def _mm_residual_loss(a, w, h, g, scale, target, lead, name, carried=()):
    m, k = a.shape
    d = w.shape[1]
    tm = _tile(m, ACC_ROWS, SUBLANE_BF16)
    tk = _tile(k, K_TILE, LANE)
    nk = k // tk
    gsz, epilogue_row = _row_groups(m // tm, 2, nk)

    def body(a_ref, w_ref, h_ref, g_ref, t_ref, dy_ref, dfo_ref, dg_ref, l_ref, acc_ref):
        grp, kk, i = pl.program_id(0), pl.program_id(1), pl.program_id(2)

        @pl.when(jnp.logical_and(jnp.logical_and(grp == 0, kk == 0), i == 0))
        def _():
            dg_ref[...] = jnp.zeros_like(dg_ref)
            l_ref[...] = jnp.zeros_like(l_ref)

        @pl.when(kk == 0)
        def _():
            acc_ref[i] = jnp.zeros((tm, d), F32)

        acc_ref[i] += jnp.dot(a_ref[...], w_ref[...], preferred_element_type=F32)

        @pl.when(kk == nk - 1)
        def _():
            fo = acc_ref[i]
            gain = g_ref[...]
            r = lax.rsqrt(jnp.mean(fo * fo, axis=-1, keepdims=True) + EPS)
            xh = fo * r
            y = h_ref[...] + scale * (xh * gain)
            row = (grp * gsz + i) * tm + lax.broadcasted_iota(jnp.int32, (tm, 1), 0)
            e = jnp.where(row >= lead, y - t_ref[...], 0.0)
            dy = e * (1.0 / d)
            dy_ref[...] = dy
            l_ref[...] += 0.5 * jnp.sum(jnp.sum(e * e, axis=-1, keepdims=True) * (1.0 / d), axis=0, keepdims=True)
            dn = scale * dy
            dyh = dn * gain
            dfo_ref[...] = (r * (dyh - xh * jnp.mean(dyh * xh, axis=-1, keepdims=True))).astype(dfo_ref.dtype)
            dg_ref[...] += jnp.sum(dn * xh, axis=0, keepdims=True)

    row = pl.BlockSpec((tm, d), lambda grp, kk, i: (epilogue_row(grp, kk, i), 0))
    row_once = pl.BlockSpec((tm, d), lambda grp, kk, i: (epilogue_row(grp, kk, i), 0), pipeline_mode=pl.Buffered(1))
    vec = pl.BlockSpec((1, d), lambda grp, kk, i: (0, 0))
    return _call(
        body, grid=(m // tm // gsz, nk, gsz),
        in_specs=[pl.BlockSpec((tm, tk), lambda grp, kk, i: (grp * gsz + i, kk)),
                  pl.BlockSpec((tk, d), lambda grp, kk, i: (kk, 0)), row_once, vec, row_once],
        out_specs=[row, row, vec, pl.BlockSpec((1, 1), lambda grp, kk, i: (0, 0))],
        out_shape=[jax.ShapeDtypeStruct((m, d), F32), jax.ShapeDtypeStruct((m, d), MXU_DTYPE),
                   jax.ShapeDtypeStruct((1, d), F32), jax.ShapeDtypeStruct((1, 1), F32)],
        scratch_shapes=[pltpu.VMEM((gsz, tm, d), F32)], name=name, args=(a, w, h, g, target), carried=carried)


def _norm_bwd_row_tile(m):
    return _tile(m, ACC_ROWS, SUBLANE_BF16)


def _mm_norm_bwd(pairs, h, g, dh_up, name, carried=(), row_tiles=None, dg_init=None):
    n_pairs = len(pairs)
    m, k = pairs[0][0].shape
    d = h.shape[1]
    tm = _norm_bwd_row_tile(m)
    tk = _tile(k, K_TILE, LANE)
    nk = k // tk
    t0, nt = row_tiles if row_tiles is not None else (0, m // tm)
    gsz, epilogue_row = _row_groups(nt, ACC_GROUP, nk)
    if dg_init is None:
        dg_init = jnp.zeros((1, d), F32)

    def body(*refs):
        ops = refs[:2 * n_pairs]
        h_ref, g_ref, up_ref, init_ref, dh_ref, dg_ref, acc_ref = refs[2 * n_pairs:]
        grp, kk, i = pl.program_id(0), pl.program_id(1), pl.program_id(2)

        @pl.when(jnp.logical_and(jnp.logical_and(grp == 0, kk == 0), i == 0))
        def _():
            dg_ref[...] = init_ref[...]

        @pl.when(kk == 0)
        def _():
            acc_ref[i] = jnp.zeros((tm, d), F32)

        for p in range(n_pairs):
            acc_ref[i] += jnp.dot(ops[2 * p][...], ops[2 * p + 1][...], preferred_element_type=F32)

        @pl.when(kk == nk - 1)
        def _():
            dx, dgr = _rmsnorm_bwd_rows(h_ref[...], g_ref[...], acc_ref[i])
            dh_ref[...] = up_ref[...] + dx
            dg_ref[...] += jnp.sum(dgr, axis=0, keepdims=True)

    row_in = pl.BlockSpec((tm, d), lambda grp, kk, i: (t0 + epilogue_row(grp, kk, i), 0), pipeline_mode=pl.Buffered(1))
    vec = pl.BlockSpec((1, d), lambda grp, kk, i: (0, 0))
    in_specs = []
    args = []
    for a, w in pairs:
        in_specs += [pl.BlockSpec((tm, tk), lambda grp, kk, i: (t0 + grp * gsz + i, kk)),
                     pl.BlockSpec((tk, d), lambda grp, kk, i: (kk, 0))]
        args += [a, w]
    return _call(
        body, grid=(nt // gsz, nk, gsz),
        in_specs=in_specs + [row_in, vec, row_in, vec],
        out_specs=[pl.BlockSpec((tm, d), lambda grp, kk, i: (epilogue_row(grp, kk, i), 0)), vec],
        out_shape=[jax.ShapeDtypeStruct((nt * tm, d), F32), jax.ShapeDtypeStruct((1, d), F32)],
        scratch_shapes=[pltpu.VMEM((gsz, tm, d), F32)], name=name, args=(*args, h, g, dh_up, dg_init), carried=carried)


def _mm_tn(a, b, name, carried=()):
    m, ka = a.shape
    d = b.shape[1]
    tf = _tile(ka, 512, LANE)

    def body(a_ref, b_ref, o_ref):
        o_ref[...] = _dot_tn(a_ref[...], b_ref[...]).astype(o_ref.dtype)

    return _call(
        body, grid=(ka // tf,),
        in_specs=[pl.BlockSpec((m, tf), lambda j: (0, j)),
                  pl.BlockSpec((m, d), lambda j: (0, 0), pipeline_mode=pl.Buffered(1))],
        out_specs=pl.BlockSpec((tf, d), lambda j: (j, 0)),
        out_shape=jax.ShapeDtypeStruct((ka, d), WIRE_DTYPE), name=name, args=(a, b), carried=carried)


GELU_K = 0.7978845608028654
GELU_C = 0.044715


def _expm1(x):
    series = x * (1.0 + x * (1.0 / 2 + x * (1.0 / 6 + x * (1.0 / 24 + x * (1.0 / 120 + x * (1.0 / 720 + x * (1.0 / 5040)))))))
    return jnp.where(jnp.abs(x) < 0.3, series, jnp.exp(x) - 1.0)


def _softplus(x):
    return jnp.maximum(x, 0.0) + jnp.log1p(jnp.exp(-jnp.abs(x)))


def _block_mm(v, w_ref, transposed):
    nbk = w_ref.shape[0]
    outs = []
    for j in range(nbk):
        vj = v[:, j * BD:(j + 1) * BD]
        outs.append(_dot_nt(vj, w_ref[j]) if transposed else jnp.dot(vj, w_ref[j], preferred_element_type=F32))
    return outs[0] if nbk == 1 else jnp.concatenate(outs, axis=1)


def _group_mean(q, gm_ref):
    hi = q.astype(MXU_DTYPE)
    lo = (q - hi.astype(F32)).astype(MXU_DTYPE)
    nbk = q.shape[1] // BD
    gm = gm_ref[...]
    outs = []
    for j in range(nbk):
        sl = slice(j * BD, (j + 1) * BD)
        outs.append(jnp.dot(hi[:, sl], gm, preferred_element_type=F32) + jnp.dot(lo[:, sl], gm, preferred_element_type=F32))
    return outs[0] if nbk == 1 else jnp.concatenate(outs, axis=1)


class _RowReader:
    def __init__(self, ref):
        self.ref = ref

    def __getitem__(self, rows):
        return self.ref[rows, :]


def _shifted(ext_ref, cur, before8, after8, downs=(), ups=()):
    r = cur.shape[0]
    if downs:
        ext_ref[0:8, :] = before8
    ext_ref[8:8 + r, :] = cur
    if ups:
        ext_ref[8 + r:16 + r, :] = after8
    return [ext_ref[pl.ds(8 - j, r), :] for j in downs] + [ext_ref[pl.ds(8 + j, r), :] for j in ups]


def _lru_gates(xc, pv, wa_ref, wx_ref):
    xcb = xc.astype(MXU_DTYPE)
    ga = jax.nn.sigmoid(_block_mm(xcb, wa_ref, False) + pv[5:6])
    gx = jax.nn.sigmoid(_block_mm(xcb, wx_ref, False) + pv[6:7])
    sp = _softplus(-pv[7:8])
    log_a = -LRU_C * ga * sp
    a = jnp.exp(log_a)
    e2 = _expm1(2.0 * log_a)
    mult = jnp.sqrt(-e2)
    return xcb, ga, gx, sp, a, e2, mult


def _gelu_parts(y):
    th = jnp.tanh(GELU_K * (y + GELU_C * y * y * y))
    return 0.5 * y * (1.0 + th), th


def _mixer_fwd(z, pv, wa, wx, gm, pad, name, carried=()):
    m = z.shape[0]
    c = pv.shape[1]
    r = MIX_ROWS
    nb = m // r

    def body(z_ref, pv_ref, wa_ref, wx_ref, gm_ref, mixed_ref, hs_ref, ext_ref, tailx_ref, tailc_ref, carry_ref):
        b = pl.program_id(0)

        @pl.when(b == 0)
        def _():
            tailx_ref[...] = jnp.zeros_like(tailx_ref)
            tailc_ref[...] = jnp.zeros_like(tailc_ref)
            carry_ref[...] = jnp.zeros_like(carry_ref)

        pv = _RowReader(pv_ref)
        row = b * r + lax.broadcasted_iota(jnp.int32, (r, 1), 0)
        lrow = lax.broadcasted_iota(jnp.int32, (r, c), 0)
        maskf = (row >= pad).astype(F32)
        y = z_ref[:, 0:c]
        xl = z_ref[:, c:2 * c]
        bs = z_ref[:, 2 * c:3 * c]
        cv = z_ref[:, 3 * c:4 * c] * z_ref[:, 4 * c:5 * c]

        x1, x2, x3 = _shifted(ext_ref, xl, tailx_ref[...], None, downs=(1, 2, 3))
        tailx_ref[...] = z_ref[pl.ds(r - 8, 8), c:2 * c]
        xc = pv[4:5] + pv[3:4] * xl + pv[2:3] * x1 + pv[1:2] * x2 + pv[0:1] * x3
        _, _, gx, _, a, _, mult = _lru_gates(xc, pv, wa_ref, wx_ref)
        uu = mult * (gx * xc) * maskf

        acc_a = a
        acc_h = uu
        dlt = 1
        while dlt < r:
            keep = lrow >= dlt
            sh_a = pltpu.roll(acc_a, dlt, axis=0)
            sh_h = pltpu.roll(acc_h, dlt, axis=0)
            acc_h = acc_h + acc_a * jnp.where(keep, sh_h, 0.0)
            acc_a = acc_a * jnp.where(keep, sh_a, 1.0)
            dlt *= 2
        hs = acc_h + acc_a * carry_ref[...]
        hs_ref[...] = hs
        carry_ref[...] = hs_ref[pl.ds(r - 1, 1), :]

        gelu_y, _ = _gelu_parts(y)
        lru_out = hs * gelu_y
        c1, c2 = _shifted(ext_ref, cv, tailc_ref[...], None, downs=(1, 2))
        tailc_ref[...] = cv[r - 8:r]
        sc_out = bs * (pv[10:11] * cv + pv[9:10] * c1 + pv[8:9] * c2)

        rl = lax.rsqrt(_group_mean(lru_out * lru_out, gm_ref) + EPS)
        rs = lax.rsqrt(_group_mean(sc_out * sc_out, gm_ref) + EPS)
        mixed_ref[:, 0:c] = (lru_out * rl * pv[11:12]).astype(mixed_ref.dtype)
        mixed_ref[:, c:2 * c] = (sc_out * rs * pv[12:13]).astype(mixed_ref.dtype)

    full = lambda shape: pl.BlockSpec(shape, lambda b: (0,) * len(shape))
    return _call(
        body, grid=(nb,),
        in_specs=[pl.BlockSpec((r, 5 * c), lambda b: (b, 0)), full(pv.shape), full(wa.shape), full(wx.shape), full(gm.shape)],
        out_specs=[pl.BlockSpec((r, 2 * c), lambda b: (b, 0)), pl.BlockSpec((r, c), lambda b: (b, 0))],
        out_shape=[jax.ShapeDtypeStruct((m, 2 * c), MXU_DTYPE), jax.ShapeDtypeStruct((m, c), F32)],
        scratch_shapes=[pltpu.VMEM((r + 16, c), F32), pltpu.VMEM((8, c), F32), pltpu.VMEM((8, c), F32),
                        pltpu.VMEM((1, c), F32)],
        name=name, args=(z, pv, wa, wx, gm), carried=carried)


def _mixer_bwd(z, hs, dmixed, pv, wa, wx, gm, pad, name, carried=()):
    m = z.shape[0]
    c = pv.shape[1]
    r = MIX_ROWS
    nb = m // r
    r8 = r // 8

    def body(z_ref, zp_ref, hs_ref, hsp_ref, dm_ref, pv_ref, wa_ref, wx_ref, gm_ref,
             dz_ref, dpv_ref, dwa_ref, dwx_ref, ext_ref, hxc_ref, hsc_ref, hp_ref):
        i = pl.program_id(0)
        b = nb - 1 - i

        @pl.when(i == 0)
        def _():
            hxc_ref[...] = jnp.zeros_like(hxc_ref)
            hsc_ref[...] = jnp.zeros_like(hsc_ref)
            hp_ref[...] = jnp.zeros_like(hp_ref)
            dpv_ref[...] = jnp.zeros_like(dpv_ref)
            dwa_ref[...] = jnp.zeros_like(dwa_ref)
            dwx_ref[...] = jnp.zeros_like(dwx_ref)

        pv = _RowReader(pv_ref)
        row = b * r + lax.broadcasted_iota(jnp.int32, (r, 1), 0)
        lrow = lax.broadcasted_iota(jnp.int32, (r, c), 0)
        maskf = (row >= pad).astype(F32)
        has_prev = (b > 0).astype(F32)
        y = z_ref[:, 0:c]
        xl = z_ref[:, c:2 * c]
        bs = z_ref[:, 2 * c:3 * c]
        cs = z_ref[:, 3 * c:4 * c]
        vs = z_ref[:, 4 * c:5 * c]
        cv = cs * vs
        xl_prev = zp_ref[:, c:2 * c] * has_prev
        cv_prev = zp_ref[:, 3 * c:4 * c] * zp_ref[:, 4 * c:5 * c] * has_prev
        hs = hs_ref[...]

        x1, x2, x3 = _shifted(ext_ref, xl, xl_prev, None, downs=(1, 2, 3))
        xc = pv[4:5] + pv[3:4] * xl + pv[2:3] * x1 + pv[1:2] * x2 + pv[0:1] * x3
        xcb, ga, gx, sp, a, e2, mult = _lru_gates(xc, pv, wa_ref, wx_ref)
        gxx = gx * xc
        gelu_y, th = _gelu_parts(y)
        lru_out = hs * gelu_y
        c1, c2 = _shifted(ext_ref, cv, cv_prev, None, downs=(1, 2))
        sc = pv[10:11] * cv + pv[9:10] * c1 + pv[8:9] * c2
        sc_out = bs * sc

        def group_norm_bwd(v, dm, gain):
            rr = lax.rsqrt(_group_mean(v * v, gm_ref) + EPS)
            vh = v * rr
            dvh = dm * gain
            dv = rr * (dvh - vh * _group_mean(dvh * vh, gm_ref))
            return dv, jnp.sum(dm * vh, axis=0, keepdims=True)

        d_lru_out, d_og = group_norm_bwd(lru_out, dm_ref[:, 0:c], pv[11:12])
        d_sc_out, d_sg = group_norm_bwd(sc_out, dm_ref[:, c:2 * c], pv[12:13])
        dpv_ref[11:12, :] += d_og
        dpv_ref[12:13, :] += d_sg

        dhs = d_lru_out * gelu_y
        dgelu = 0.5 * (1.0 + th) + 0.5 * y * (1.0 - th * th) * GELU_K * (1.0 + 3.0 * GELU_C * y * y)
        dy = d_lru_out * hs * dgelu

        acc_a = a
        acc_p = a * dhs
        dlt = 1
        while dlt < r:
            keep = lrow < r - dlt
            sh_a = pltpu.roll(acc_a, r - dlt, axis=0)
            sh_p = pltpu.roll(acc_p, r - dlt, axis=0)
            acc_p = acc_p + acc_a * jnp.where(keep, sh_p, 0.0)
            acc_a = acc_a * jnp.where(keep, sh_a, 1.0)
            dlt *= 2
        p_all = acc_p + acc_a * hp_ref[0:1, :]
        (p_next,) = _shifted(ext_ref, p_all, None, hp_ref[...], ups=(1,))
        hp_ref[...] = p_all[0:8]
        q = dhs + p_next
        (hs_prev,) = _shifted(ext_ref, hs, hsp_ref[...] * has_prev, None, downs=(1,))
        duu = q * maskf
        da = q * hs_prev

        dmult = duu * gxx
        dgxx = duu * mult
        dgx = dgxx * xc
        dxc = dgxx * gx
        dlog_a = da * a - dmult * ((1.0 + e2) / mult)
        dga = dlog_a * (-LRU_C * sp)
        dsp = jnp.sum(dlog_a * (-LRU_C * ga), axis=0, keepdims=True)
        dpv_ref[7:8, :] += dsp * (-jax.nn.sigmoid(-pv[7:8]))
        dga_pre = dga * ga * (1.0 - ga)
        dgx_pre = dgx * gx * (1.0 - gx)
        dpv_ref[5:6, :] += jnp.sum(dga_pre, axis=0, keepdims=True)
        dpv_ref[6:7, :] += jnp.sum(dgx_pre, axis=0, keepdims=True)
        dga_b = dga_pre.astype(MXU_DTYPE)
        dgx_b = dgx_pre.astype(MXU_DTYPE)
        dxc = dxc + _block_mm(dga_b, wa_ref, True) + _block_mm(dgx_b, wx_ref, True)
        for j in range(c // BD):
            sl = slice(j * BD, (j + 1) * BD)
            dwa_ref[j] += _dot_tn(xcb[:, sl], dga_b[:, sl])
            dwx_ref[j] += _dot_tn(xcb[:, sl], dgx_b[:, sl])

        dpv_ref[4:5, :] += jnp.sum(dxc, axis=0, keepdims=True)
        dpv_ref[3:4, :] += jnp.sum(dxc * xl, axis=0, keepdims=True)
        dpv_ref[2:3, :] += jnp.sum(dxc * x1, axis=0, keepdims=True)
        dpv_ref[1:2, :] += jnp.sum(dxc * x2, axis=0, keepdims=True)
        dpv_ref[0:1, :] += jnp.sum(dxc * x3, axis=0, keepdims=True)
        u1, u2, u3 = _shifted(ext_ref, dxc, None, hxc_ref[...], ups=(1, 2, 3))
        hxc_ref[...] = dxc[0:8]
        dxl = pv[3:4] * dxc + pv[2:3] * u1 + pv[1:2] * u2 + pv[0:1] * u3

        dbs = d_sc_out * sc
        dsc = d_sc_out * bs
        dpv_ref[10:11, :] += jnp.sum(dsc * cv, axis=0, keepdims=True)
        dpv_ref[9:10, :] += jnp.sum(dsc * c1, axis=0, keepdims=True)
        dpv_ref[8:9, :] += jnp.sum(dsc * c2, axis=0, keepdims=True)
        s1, s2 = _shifted(ext_ref, dsc, None, hsc_ref[...], ups=(1, 2))
        hsc_ref[...] = dsc[0:8]
        dcv = pv[10:11] * dsc + pv[9:10] * s1 + pv[8:9] * s2

        dz_ref[:, 0:c] = (dy * maskf).astype(dz_ref.dtype)
        dz_ref[:, c:2 * c] = (dxl * maskf).astype(dz_ref.dtype)
        dz_ref[:, 2 * c:3 * c] = (dbs * maskf).astype(dz_ref.dtype)
        dz_ref[:, 3 * c:4 * c] = (dcv * vs * maskf).astype(dz_ref.dtype)
        dz_ref[:, 4 * c:5 * c] = (dcv * cs * maskf).astype(dz_ref.dtype)

    full = lambda shape: pl.BlockSpec(shape, lambda i: (0,) * len(shape))
    cur = lambda width: pl.BlockSpec((r, width), lambda i: (nb - 1 - i, 0))
    prev8 = lambda width: pl.BlockSpec((8, width), lambda i: (jnp.maximum((nb - 1 - i) * r8 - 1, 0), 0))
    return _call(
        body, grid=(nb,),
        in_specs=[cur(5 * c), prev8(5 * c), cur(c), prev8(c), cur(2 * c),
                  full(pv.shape), full(wa.shape), full(wx.shape), full(gm.shape)],
        out_specs=[cur(5 * c), full(pv.shape), full(wa.shape), full(wx.shape)],
        out_shape=[jax.ShapeDtypeStruct((m, 5 * c), MXU_DTYPE), jax.ShapeDtypeStruct(pv.shape, F32),
                   jax.ShapeDtypeStruct(wa.shape, F32), jax.ShapeDtypeStruct(wx.shape, F32)],
        scratch_shapes=[pltpu.VMEM((r + 16, c), F32), pltpu.VMEM((8, c), F32), pltpu.VMEM((8, c), F32),
                        pltpu.VMEM((8, c), F32)],
        name=name, args=(z, z, hs, hs, dmixed, pv, wa, wx, gm), carried=carried)


def _position():
    return lax.axis_index("x"), lax.axis_index("y"), lax.axis_index("c")


def _block_of(px, py, pc):
    return 4 * px + 2 * py + pc


class _TwoLevelGather:
    def __init__(self, n_arrays, rows_of, src_of, send_sems, recv_sems):
        x, y, c = _position()
        self.n, self.rows_of, self.src_of = n_arrays, rows_of, src_of
        self.send_sems, self.recv_sems = send_sems, recv_sems
        self.c, self.me, self.sibling = c, (x, y, c), (x, y, 1 - c)
        self.chips = [(1 - x, y), (x, 1 - y), (1 - x, 1 - y)]

    def _copy(self, i, k, block, to, src=None):
        return pltpu.make_async_remote_copy(
            src_ref=self.rows_of(i, *block) if src is None else src, dst_ref=self.rows_of(i, *block),
            send_sem=self.send_sems.at[7 * i + k], recv_sem=self.recv_sems.at[7 * i + k],
            device_id=to, device_id_type=MESH)

    def _first(self, i):
        own = [self._copy(i, 0, self.me, self.sibling, src=self.src_of(i))]
        return own + [self._copy(i, 1 + j, self.me, (*chip, self.c), src=self.src_of(i))
                      for j, chip in enumerate(self.chips)]

    def _passed(self, i, j):
        return self._copy(i, 4 + j, (*self.chips[j], self.c), self.sibling)

    def start(self):
        for i in range(self.n):
            for cp in self._first(i):
                cp.start()

    def forward(self):
        for i in range(self.n):
            for j, chip in enumerate(self.chips):
                self._copy(i, 1 + j, (*chip, self.c), self.me).wait_recv()
                self._passed(i, j).start()

    def drain(self):
        for i in range(self.n):
            self._copy(i, 0, self.sibling, self.me).wait_recv()
            for j, chip in enumerate(self.chips):
                self._copy(i, 4 + j, (*chip, 1 - self.c), self.me).wait_recv()
        for i in range(self.n):
            for cp in self._first(i) + [self._passed(i, j) for j in range(3)]:
                cp.wait_send()


class _CarriedGather:
    def __init__(self, shards, padded_rows, zeros, forward_at):
        d = shards[0].shape[1]
        self.forward_at = forward_at
        self.n = len(shards)
        self.rows = [s.shape[0] for s in shards]
        self.pads = [p - N_DEV * r for r, p in zip(self.rows, padded_rows)]
        assert max(self.pads) <= zeros.shape[0] and zeros.shape[1] == d
        self.arrays = list(shards) + [zeros]
        self.out_shapes = [jax.ShapeDtypeStruct((p, d), s.dtype) for s, p in zip(shards, padded_rows)]
        self.aliases = {}
        self.n_remote, self.n_local = 7 * self.n, 2 * self.n
        self.results = None

    def _rows_of(self, outs):
        def rows_of(i, px, py, pc):
            s = self.rows[i]
            return outs[i].at[pl.ds(pl.multiple_of(_block_of(px, py, pc) * s, SUBLANE_BF16), s), :]
        return rows_of

    def _gather(self, ins, outs, send_sems, recv_sems):
        return _TwoLevelGather(self.n, self._rows_of(outs), lambda i: ins[i], send_sems, recv_sems)

    def _local(self, ins, outs, local_sems):
        x, y, c = _position()
        rows_of = self._rows_of(outs)
        cps = []
        for i in range(self.n):
            cps.append(pltpu.make_async_copy(ins[i], rows_of(i, x, y, c), local_sems.at[2 * i]))
            if self.pads[i]:
                cps.append(pltpu.make_async_copy(ins[self.n].at[pl.ds(0, self.pads[i]), :],
                                                 outs[i].at[pl.ds(N_DEV * self.rows[i], self.pads[i]), :],
                                                 local_sems.at[2 * i + 1]))
        return cps

    def start(self, ins, outs, send_sems, recv_sems, local_sems):
        for cp in self._local(ins, outs, local_sems):
            cp.start()
        self._gather(ins, outs, send_sems, recv_sems).start()

    def forward(self, ins, outs, send_sems, recv_sems, local_sems):
        self._gather(ins, outs, send_sems, recv_sems).forward()

    def finish(self, ins, outs, send_sems, recv_sems, local_sems):
        self._gather(ins, outs, send_sems, recv_sems).drain()
        for cp in self._local(ins, outs, local_sems):
            cp.wait()


class _CarriedSwap:
    def __init__(self, grads, shard_rows):
        d = grads[0].shape[1]
        self.n, self.rows = len(grads), list(shard_rows)
        self.arrays = list(grads)
        self.out_shapes = [jax.ShapeDtypeStruct((4, s, d), g.dtype) for g, s in zip(grads, shard_rows)]
        self.aliases = {}
        self.n_remote, self.n_local = 4 * self.n, 0
        self.forward_at = 1.0
        self.results = None

    def _copies(self, ins, outs, send_sems, recv_sems):
        x, y, c = _position()
        cps = []
        for i in range(self.n):
            s = self.rows[i]
            for k in range(4):
                blk = _block_of(k >> 1, k & 1, 1 - c)
                cps.append(pltpu.make_async_remote_copy(
                    src_ref=ins[i].at[pl.ds(pl.multiple_of(blk * s, SUBLANE_BF16), s), :], dst_ref=outs[i].at[k],
                    send_sem=send_sems.at[4 * i + k], recv_sem=recv_sems.at[4 * i + k],
                    device_id=(x, y, 1 - c), device_id_type=MESH))
        return cps

    def start(self, ins, outs, send_sems, recv_sems, local_sems):
        for cp in self._copies(ins, outs, send_sems, recv_sems):
            cp.start()

    def forward(self, *_):
        pass

    def finish(self, ins, outs, send_sems, recv_sems, local_sems):
        for cp in self._copies(ins, outs, send_sems, recv_sems):
            cp.wait()


class _CarriedChipExchange:
    def __init__(self, presums):
        self.n = len(presums)
        self.arrays = list(presums)
        self.out_shapes = [jax.ShapeDtypeStruct(p.shape, p.dtype) for p in presums]
        self.aliases = {}
        self.n_remote, self.n_local = 3 * self.n, 0
        self.forward_at = 1.0
        self.results = None

    def _copies(self, ins, outs, send_sems, recv_sems):
        x, y, c = _position()
        cps = []
        for i in range(self.n):
            for r in range(1, 4):
                cps.append(pltpu.make_async_remote_copy(
                    src_ref=ins[i].at[r - 1], dst_ref=outs[i].at[r - 1],
                    send_sem=send_sems.at[3 * i + r - 1], recv_sem=recv_sems.at[3 * i + r - 1],
                    device_id=(x ^ (r >> 1), y ^ (r & 1), c), device_id_type=MESH))
        return cps

    def start(self, ins, outs, send_sems, recv_sems, local_sems):
        for cp in self._copies(ins, outs, send_sems, recv_sems):
            cp.start()

    def forward(self, *_):
        pass

    def finish(self, ins, outs, send_sems, recv_sems, local_sems):
        for cp in self._copies(ins, outs, send_sems, recv_sems):
            cp.wait()


def _gather_small(block, reduce, name):
    rr, nn = block.shape

    def body(x_ref, out_ref, *rest):
        if reduce:
            stack_ref, send_sems, recv_sems, local_sem = rest
        else:
            send_sems, recv_sems, local_sem = rest
            stack_ref = out_ref
        x, y, c = _position()

        def rows_of(i, px, py, pc):
            return stack_ref.at[pl.ds(pl.multiple_of(_block_of(px, py, pc) * rr, 8), rr), :]

        own = pltpu.make_async_copy(x_ref, rows_of(0, x, y, c), local_sem)
        own.start()
        gather = _TwoLevelGather(1, rows_of, lambda i: x_ref, send_sems, recv_sems)
        gather.start()
        gather.forward()
        gather.drain()
        own.wait()
        if reduce:
            acc = stack_ref[0:rr, :]
            for k in range(1, N_DEV):
                acc = acc + stack_ref[k * rr:(k + 1) * rr, :]
            out_ref[...] = acc

    vmem = pl.BlockSpec(memory_space=pltpu.VMEM)
    scratch = [pltpu.SemaphoreType.DMA((7,)), pltpu.SemaphoreType.DMA((7,)), pltpu.SemaphoreType.DMA]
    if reduce:
        scratch = [pltpu.VMEM((N_DEV * rr, nn), F32)] + scratch
    out_rows = rr if reduce else N_DEV * rr
    return pl.pallas_call(
        body, in_specs=[vmem], out_specs=vmem, out_shape=jax.ShapeDtypeStruct((out_rows, nn), F32),
        scratch_shapes=scratch, name=name, compiler_params=_params())(block)


def _presum(where, grad, swapped, name):
    s, d = swapped.shape[1], swapped.shape[2]
    tc = _tile(d, 512, LANE)

    def body(where_ref, g_ref, sw_ref, o_ref):
        o_ref[0] = (g_ref[...].astype(F32) + sw_ref[0].astype(F32)).astype(o_ref.dtype)

    return _call(
        body, grid=(3, d // tc),
        in_specs=[pl.BlockSpec((s, tc), lambda r, j, where: (where[1 + r], j)),
                  pl.BlockSpec((1, s, tc), lambda r, j, where: (where[5 + r], 0, j))],
        out_specs=pl.BlockSpec((1, s, tc), lambda r, j, where: (r, 0, j)),
        out_shape=jax.ShapeDtypeStruct((3, s, d), WIRE_DTYPE), name=name, args=(grad, swapped), prefetch=(where,))


def _final_sum(where, grad, swapped, received, name, carried=()):
    s, d = swapped.shape[1], swapped.shape[2]
    tc = _tile(d, 512, LANE)

    def body(where_ref, g_ref, sw_ref, r_ref, o_ref):
        acc = g_ref[...].astype(F32) + sw_ref[0].astype(F32)
        for k in range(3):
            acc = acc + r_ref[k].astype(F32)
        o_ref[...] = acc

    return _call(
        body, grid=(d // tc,),
        in_specs=[pl.BlockSpec((s, tc), lambda j, where: (where[0], j)),
                  pl.BlockSpec((1, s, tc), lambda j, where: (where[4], 0, j)),
                  pl.BlockSpec((3, s, tc), lambda j, where: (0, 0, j))],
        out_specs=pl.BlockSpec((s, tc), lambda j, where: (0, j)),
        out_shape=jax.ShapeDtypeStruct((s, d), F32), name=name, args=(grad, swapped, received),
        prefetch=(where,), carried=carried)


class _GradReduction:
    def __init__(self, key, grad, shard_rows, where):
        self.key, self.grad, self.rows, self.where = key, grad, shard_rows, where

    def swap(self):
        self._swap = _CarriedSwap([self.grad], [self.rows])
        return self._swap

    def exchange(self):
        presum = _presum(self.where, self.grad, self._swap.results[0], "presum_" + self.key)
        self._exchange = _CarriedChipExchange([presum])
        return self._exchange

    def total(self, carried=()):
        return _final_sum(self.where, self.grad, self._swap.results[0], self._exchange.results[0],
                          "sum_" + self.key, carried)

    def total_and_update(self, w, m, v):
        return _sum_adamw(self.where, self.grad, self._swap.results[0], self._exchange.results[0], w, m, v,
                          "update_" + self.key)


def _adamw_math(w, g, m, v):
    nm = ADAM_B1 * m + (1.0 - ADAM_B1) * g
    nv = ADAM_B2 * v + (1.0 - ADAM_B2) * (g * g)
    m_hat = nm / (1.0 - ADAM_B1 ** ADAM_STEP)
    v_hat = nv / (1.0 - ADAM_B2 ** ADAM_STEP)
    return -ADAM_LR * (m_hat / (jnp.sqrt(v_hat) + ADAM_EPS) + ADAM_WD * w), nm, nv


def _sum_adamw(where, grad, swapped, received, w, m, v, name):
    s, d = swapped.shape[1], swapped.shape[2]
    tc = _tile(d, 512, LANE)

    def body(where_ref, g_ref, sw_ref, r_ref, w_ref, m_ref, v_ref, gs_ref, d_ref, nm_ref, nv_ref):
        g = g_ref[...].astype(F32) + sw_ref[0].astype(F32)
        for k in range(3):
            g = g + r_ref[k].astype(F32)
        gs_ref[...] = g
        d_ref[...], nm_ref[...], nv_ref[...] = _adamw_math(w_ref[...], g, m_ref[...], v_ref[...])

    blk = pl.BlockSpec((s, tc), lambda j, where: (0, j))
    return _call(
        body, grid=(d // tc,),
        in_specs=[pl.BlockSpec((s, tc), lambda j, where: (where[0], j)),
                  pl.BlockSpec((1, s, tc), lambda j, where: (where[4], 0, j)),
                  pl.BlockSpec((3, s, tc), lambda j, where: (0, 0, j)), blk, blk, blk],
        out_specs=[blk] * 4, out_shape=[jax.ShapeDtypeStruct((s, d), F32)] * 4, name=name,
        args=(grad, swapped, received, w, m, v), prefetch=(where,))


def _adamw(w, g, m, v, name):
    rows, cols = w.shape
    tr = _tile(rows, 256, 8)

    def body(w_ref, g_ref, m_ref, v_ref, d_ref, nm_ref, nv_ref):
        d_ref[...], nm_ref[...], nv_ref[...] = _adamw_math(w_ref[...], g_ref[...], m_ref[...], v_ref[...])

    spec = pl.BlockSpec((tr, cols), lambda i: (i, 0))
    return pl.pallas_call(
        body, grid=(rows // tr,), in_specs=[spec] * 4, out_specs=[spec] * 3,
        out_shape=[jax.ShapeDtypeStruct((rows, cols), F32)] * 3, name=name, compiler_params=_params())(w, g, m, v)


def _pack_rows(arrays, width, row_quantum=8):
    flat = jnp.concatenate([a.reshape(-1) for a in arrays])
    total = _round_up(flat.shape[0], row_quantum * width)
    flat = jnp.pad(flat, (0, total - flat.shape[0]))
    return flat.reshape(-1, width)


def _unpack_rows(packed, shapes):
    flat = packed.reshape(-1)
    out = []
    off = 0
    for shp in shapes:
        size = 1
        for s in shp:
            size *= s
        out.append(flat[off:off + size].reshape(shp))
        off += size
    return out


def _block_diag(w):
    h, hb, _ = w.shape
    per = BD // hb
    w4 = w.reshape(h // per, per, hb, hb)
    eye = jnp.eye(per, dtype=w.dtype)
    return jnp.einsum('npij,pq->npiqj', w4, eye).reshape(h // per, BD, BD)


def _block_diag_extract(bd, hb):
    nbk = bd.shape[0]
    per = BD // hb
    b5 = bd.reshape(nbk, per, hb, per, hb)
    eye = jnp.eye(per, dtype=bd.dtype)
    return jnp.einsum('npiqj,pq->npij', b5, eye).reshape(nbk * per, hb, hb)


def kernel(x, meta_tokens, ffn1_pre_g, ffn1_w_gate, ffn1_w_up, ffn1_w_down, ffn1_post_g, mix_pre_g, w_in, lru_conv_w, lru_conv_b, lru_w_a, lru_b_a, lru_w_x, lru_b_x, lru_lambda, sconv_w, lru_out_g, sconv_out_g, w_out, mix_post_g, ffn2_pre_g, ffn2_w_gate, ffn2_w_up, ffn2_w_down, ffn2_post_g, loss_target, m_meta_tokens, m_ffn1_pre_g, m_ffn1_w_gate, m_ffn1_w_up, m_ffn1_w_down, m_ffn1_post_g, m_mix_pre_g, m_w_in, m_lru_conv_w, m_lru_conv_b, m_lru_w_a, m_lru_b_a, m_lru_w_x, m_lru_b_x, m_lru_lambda, m_sconv_w, m_lru_out_g, m_sconv_out_g, m_w_out, m_mix_post_g, m_ffn2_pre_g, m_ffn2_w_gate, m_ffn2_w_up, m_ffn2_w_down, m_ffn2_post_g, v_meta_tokens, v_ffn1_pre_g, v_ffn1_w_gate, v_ffn1_w_up, v_ffn1_w_down, v_ffn1_post_g, v_mix_pre_g, v_w_in, v_lru_conv_w, v_lru_conv_b, v_lru_w_a, v_lru_b_a, v_lru_w_x, v_lru_b_x, v_lru_lambda, v_sconv_w, v_lru_out_g, v_sconv_out_g, v_w_out, v_mix_post_g, v_ffn2_pre_g, v_ffn2_w_gate, v_ffn2_w_up, v_ffn2_w_down, v_ffn2_post_g):
    given = dict(locals())
    wts = {n: given[n] for n in WEIGHT_NAMES}
    mom = {n: given["m_" + n] for n in WEIGHT_NAMES}
    var = {n: given["v_" + n] for n in WEIGHT_NAMES}

    xi, yi, ci = _position()
    me = _block_of(xi, yi, ci)
    x2 = x[0]
    seq, d = x2.shape
    n_meta = meta_tokens.shape[0]
    m_rows = _round_up(n_meta + seq, ROW_ALIGN)
    pad = m_rows - n_meta - seq
    lead = pad + n_meta
    c = lru_conv_b.shape[1]
    hb = lru_w_a.shape[-1]
    dm = meta_tokens.shape[1]
    cs_ = lru_conv_w.shape[2]
    kw4, kw3 = lru_conv_w.shape[1], sconv_w.shape[1]
    assert d == 2 * c and c % BD == 0 and BD % hb == 0 and cs_ <= dm and kw4 == 4 and kw3 == 3

    small = jnp.zeros((_round_up(n_meta + kw4 + kw3, 8), dm), F32)
    small = small.at[0:n_meta].set(meta_tokens)
    small = small.at[n_meta:n_meta + kw4, 0:cs_].set(lru_conv_w[0])
    small = small.at[n_meta + kw4:n_meta + kw4 + kw3, 0:cs_].set(sconv_w[0])
    sr = small.shape[0]
    small_all = _gather_small(small, False, "gather_small").reshape(N_DEV, sr, dm)
    meta_full = small_all[:, 0:n_meta, :].transpose(1, 0, 2).reshape(n_meta, d)
    conv_w_full = small_all[:, n_meta:n_meta + kw4, 0:cs_].transpose(1, 0, 2).reshape(kw4, c)
    sconv_w_full = small_all[:, n_meta + kw4:n_meta + kw4 + kw3, 0:cs_].transpose(1, 0, 2).reshape(kw3, c)

    big = ['ffn1_w_gate', 'ffn1_w_up', 'ffn1_w_down', 'w_in', 'w_out', 'ffn2_w_gate', 'ffn2_w_up', 'ffn2_w_down']
    col_sharded = {'ffn1_w_gate', 'ffn1_w_up', 'w_in', 'ffn2_w_gate', 'ffn2_w_up'}
    shards = []
    for nme in big:
        w = wts[nme][0].astype(WIRE_DTYPE)
        shards.append(w.T if nme in col_sharded else w)
    shard_rows = dict(zip(big, [s.shape[0] for s in shards]))
    zeros = jnp.zeros((F_ALIGN, d), WIRE_DTYPE)

    def gather(forward_at, *names):
        sel = [shards[big.index(nme)] for nme in names]
        padded = [_round_up(N_DEV * shard_rows[nme], LANE if nme in ('w_in', 'w_out') else F_ALIGN) for nme in names]
        return _CarriedGather(sel, padded, zeros, forward_at)

    pv = jnp.zeros((16, c), F32)
    pv = pv.at[0:4].set(conv_w_full).at[4].set(lru_conv_b[0]).at[5].set(lru_b_a[0]).at[6].set(lru_b_x[0])
    pv = pv.at[7].set(lru_lambda[0]).at[8:11].set(sconv_w_full).at[11].set(lru_out_g[0]).at[12].set(sconv_out_g[0])
    wa_bd = _block_diag(lru_w_a[0]).astype(MXU_DTYPE)
    wx_bd = _block_diag(lru_w_x[0]).astype(MXU_DTYPE)
    gs = c // N_GROUPS
    gidx = jnp.arange(BD) // gs
    gm = jnp.where(gidx[:, None] == gidx[None, :], 1.0 / gs, 0.0).astype(MXU_DTYPE)

    h0 = jnp.concatenate([jnp.zeros((pad, d), F32), meta_full, x2], axis=0)
    ride = gather(1.0, 'ffn1_w_gate', 'ffn1_w_up')
    n1 = _rmsnorm(h0, ffn1_pre_g, "ffn1_prenorm", carried=[ride])
    wg1, wu1 = ride.results
    ride = gather(0.85, 'ffn1_w_down', 'w_out')
    g1, u1, a1 = _ffn_gate_up(n1, wg1, wu1, "ffn1_gate_up", carried=[ride])
    wd1, wout = ride.results
    ride = gather(0.8, 'w_in')
    fo1, h1, un = _mm_residual_norm(a1, wd1, h0, ffn1_post_g, 0.5, mix_pre_g, "ffn1_down", carried=[ride])
    (win_t,) = ride.results
    ride = gather(1.0, 'ffn2_w_gate')
    z = _mm_nt(un, win_t, "mix_in_proj", carried=[ride])
    (wg2,) = ride.results
    ride = gather(1.0, 'ffn2_w_up')
    mixed, hs = _mixer_fwd(z, pv, wa_bd, wx_bd, gm, pad, "mixer_fwd", carried=[ride])
    (wu2,) = ride.results
    o_mix, h2, n2 = _mm_residual_norm(mixed, wout, h1, mix_post_g, 1.0, ffn2_pre_g, "mix_out_proj")
    ride = gather(0.75, 'ffn2_w_down')
    g2, u2, a2 = _ffn_gate_up(n2, wg2, wu2, "ffn2_gate_up", carried=[ride])
    (wd2,) = ride.results
    target = jnp.concatenate([jnp.zeros((lead, d), F32), loss_target[0]], axis=0)
    dh3, dfo2, d_post2, loss_part = _mm_residual_loss(a2, wd2, h2, ffn2_post_g, 0.5, target, lead, "ffn2_down_loss")
    loss = lax.psum(loss_part[0, 0], ("x", "y", "c"))

    chip_rel = [2 * (xi ^ (r >> 1)) + (yi ^ (r & 1)) for r in range(4)]
    where = jnp.stack([2 * k + ci for k in chip_rel] + chip_rel).astype(jnp.int32)
    red = {}

    def reduction(nme, grad):
        red[nme] = _GradReduction(nme, grad, shard_rows[nme], where)
        return red[nme]

    r_wd2 = reduction('ffn2_w_down', _mm_tn(a2, dfo2, "ffn2_dw_down"))
    dg2, du2 = _ffn_hidden_bwd(dfo2, wd2, g2, u2, "ffn2_hidden_bwd", carried=[r_wd2.swap()])
    r_wg2 = reduction('ffn2_w_gate', _mm_tn(dg2, n2, "ffn2_dw_gate", carried=[r_wd2.exchange()]))
    r_wu2 = reduction('ffn2_w_up', _mm_tn(du2, n2, "ffn2_dw_up", carried=[r_wg2.swap()]))
    dh2, d_pre2 = _mm_norm_bwd([(dg2, wg2), (du2, wu2)], h2, ffn2_pre_g, dh3, "ffn2_dx",
                               carried=[r_wg2.exchange(), r_wu2.swap()])
    do_mix, d_mix_post = _norm_bwd(o_mix, mix_post_g, dh2, 1.0, "mix_postnorm_bwd")
    dmixed = _mm_nt(do_mix, wout, "mix_out_proj_bwd")
    r_wout = reduction('w_out', _mm_tn(mixed, do_mix, "mix_dw_out"))
    dz, dpv, dwa_bd, dwx_bd = _mixer_bwd(z, hs, dmixed, pv, wa_bd, wx_bd, gm, pad, "mixer_bwd",
                                         carried=[r_wu2.exchange(), r_wout.swap()])
    r_win = reduction('w_in', _mm_tn(dz, un, "mix_dw_in", carried=[r_wout.exchange()]))
    dh1, d_mix_pre = _mm_norm_bwd([(dz, win_t)], h1, mix_pre_g, dh2, "mix_dx", carried=[r_win.swap()])
    dfo1, d_post1 = _norm_bwd(fo1, ffn1_post_g, dh1, 0.5, "ffn1_postnorm_bwd")
    r_wd1 = reduction('ffn1_w_down', _mm_tn(a1, dfo1, "ffn1_dw_down", carried=[r_win.exchange()]))
    dg1, du1 = _ffn_hidden_bwd(dfo1, wd1, g1, u1, "ffn1_hidden_bwd", carried=[r_wd1.swap()])
    r_wg1 = reduction('ffn1_w_gate', _mm_tn(dg1, n1, "ffn1_dw_gate", carried=[r_wd1.exchange()]))
    r_wu1 = reduction('ffn1_w_up', _mm_tn(du1, n1, "ffn1_dw_up", carried=[r_wg1.swap()]))
    row_tile = _norm_bwd_row_tile(m_rows)
    n_tiles = m_rows // row_tile
    half = n_tiles // 2
    assert half >= 1 and half * row_tile >= lead
    dh0_a, d_pre1_a = _mm_norm_bwd([(dg1, wg1), (du1, wu1)], h0, ffn1_pre_g, dh1, "ffn1_dx_a",
                                   carried=[r_wg1.exchange(), r_wu1.swap()], row_tiles=(0, half))
    dh0_b, d_pre1 = _mm_norm_bwd([(dg1, wg1), (du1, wu1)], h0, ffn1_pre_g, dh1, "ffn1_dx_b",
                                 carried=[r_wu1.exchange()], row_tiles=(half, n_tiles - half), dg_init=d_pre1_a)
    grad_x = jnp.concatenate([dh0_a[lead:], dh0_b], axis=0)[None]
    d_meta = dh0_a[pad:lead]

    grads, delta, new_m, new_v = {}, {}, {}, {}
    for nme in big:
        in_shard_layout = nme not in col_sharded or shard_rows[nme] % LANE != 0
        if in_shard_layout:
            view = (lambda t: t[0].T) if nme in col_sharded else (lambda t: t[0])
            back = (lambda t: t.T[None]) if nme in col_sharded else (lambda t: t[None])
            outs = red[nme].total_and_update(view(wts[nme]), view(mom[nme]), view(var[nme]))
            grads[nme], delta[nme], new_m[nme], new_v[nme] = [back(t) for t in outs]
        else:
            grads[nme] = red[nme].total().T[None]
            outs = _adamw(wts[nme][0], grads[nme][0], mom[nme][0], var[nme][0], "adamw_" + nme)
            delta[nme], new_m[nme], new_v[nme] = [t[None] for t in outs]

    small_names = ['ffn1_pre_g', 'ffn1_post_g', 'mix_pre_g', 'mix_post_g', 'ffn2_pre_g', 'ffn2_post_g',
                   'lru_conv_b', 'lru_b_a', 'lru_b_x', 'lru_lambda', 'lru_out_g', 'sconv_out_g',
                   'lru_conv_w', 'sconv_w', 'lru_w_a', 'lru_w_x', 'meta_tokens']
    small_parts = [d_pre1, d_post1, d_mix_pre, d_mix_post, d_pre2, d_post2,
                   dpv[4:5], dpv[5:6], dpv[6:7], dpv[7:8], dpv[11:12], dpv[12:13],
                   dpv[0:4], dpv[8:11], _block_diag_extract(dwa_bd, hb), _block_diag_extract(dwx_bd, hb),
                   d_meta]
    small_shapes = [p.shape for p in small_parts]
    small_sum = _gather_small(_pack_rows(small_parts, d), True, "reduce_small")
    for nme, gsm in zip(small_names, _unpack_rows(small_sum, small_shapes)):
        if nme == 'meta_tokens':
            grads[nme] = lax.dynamic_slice_in_dim(gsm, me * dm, dm, axis=1)
        elif nme in ('lru_conv_w', 'sconv_w'):
            grads[nme] = lax.dynamic_slice_in_dim(gsm, me * cs_, cs_, axis=1)[None]
        else:
            grads[nme] = gsm.reshape(wts[nme].shape)

    rest = [n for n in WEIGHT_NAMES if n not in big]
    rest_shapes = [wts[n].shape for n in rest]
    packed = [_pack_rows([src[n] for n in rest], LANE, 256) for src in (wts, grads, mom, var)]
    for out, packed_out in zip((delta, new_m, new_v), _adamw(*packed, "adamw_small")):
        for nme, arr in zip(rest, _unpack_rows(packed_out, rest_shapes)):
            out[nme] = arr

    return (loss, grad_x, *[grads[n] for n in WEIGHT_NAMES], *[delta[n] for n in WEIGHT_NAMES],
            *[new_m[n] for n in WEIGHT_NAMES], *[new_v[n] for n in WEIGHT_NAMES])
```

```python
import functools

import jax
import jax.numpy as jnp
from jax import lax
from jax.experimental import pallas as pl
from jax.experimental.pallas import tpu as pltpu

F32 = jnp.float32
MXU_DTYPE = jnp.bfloat16
WIRE_DTYPE = jnp.bfloat16
MESH = pl.DeviceIdType.MESH

EPS = 1e-6
LRU_C = 8.0
N_GROUPS = 16
ADAM_LR = 0.001
ADAM_B1 = 0.9
ADAM_B2 = 0.999
ADAM_EPS = 1e-08
ADAM_WD = 0.01
ADAM_STEP = 10

N_DEV = 8
LANE = 128
SUBLANE_BF16 = 16
ROW_ALIGN = 128
F_ALIGN = 512
BD = 256
K_TILE = 512
ACC_ROWS = 528
ACC_GROUP = 1
MIX_ROWS = 128
VMEM_LIMIT_MB = 56

WEIGHT_NAMES = ['meta_tokens', 'ffn1_pre_g', 'ffn1_w_gate', 'ffn1_w_up', 'ffn1_w_down', 'ffn1_post_g',
                'mix_pre_g', 'w_in', 'lru_conv_w', 'lru_conv_b', 'lru_w_a', 'lru_b_a', 'lru_w_x', 'lru_b_x',
                'lru_lambda', 'sconv_w', 'lru_out_g', 'sconv_out_g', 'w_out', 'mix_post_g', 'ffn2_pre_g',
                'ffn2_w_gate', 'ffn2_w_up', 'ffn2_w_down', 'ffn2_post_g']


def _round_up(n, q):
    return (n + q - 1) // q * q


def _tile(n, target, q):
    best = None
    t = q
    while t <= min(n, target):
        if n % t == 0:
            best = t
        t += q
    assert best is not None, (n, target, q)
    return best


def _params(**kw):
    return pltpu.CompilerParams(vmem_limit_bytes=VMEM_LIMIT_MB << 20, **kw)


def _call(body, *, grid, in_specs, out_specs, out_shape, name, args, scratch_shapes=(), carried=(), prefetch=()):
    carried = list(carried)
    n_pf = len(prefetch)

    def launch(fn, in_specs_, out_specs_, out_shape_, scratch_, operands, aliases_):
        if n_pf:
            spec = pltpu.PrefetchScalarGridSpec(num_scalar_prefetch=n_pf, grid=grid, in_specs=in_specs_,
                                                out_specs=out_specs_, scratch_shapes=scratch_)
            return pl.pallas_call(fn, grid_spec=spec, out_shape=out_shape_, input_output_aliases=aliases_,
                                  name=name, compiler_params=_params())(*prefetch, *operands)
        return pl.pallas_call(fn, grid=grid, in_specs=in_specs_, out_specs=out_specs_, out_shape=out_shape_,
                              scratch_shapes=scratch_, input_output_aliases=aliases_, name=name,
                              compiler_params=_params())(*operands)

    if not carried:
        return launch(body, in_specs, out_specs, out_shape, list(scratch_shapes), args, {})
    single = not isinstance(out_shape, (list, tuple))
    out_specs_l = [out_specs] if single else list(out_specs)
    out_shape_l = [out_shape] if single else list(out_shape)
    n_in, n_out, n_scr = len(in_specs), len(out_specs_l), len(scratch_shapes)
    hbm = pl.BlockSpec(memory_space=pl.ANY)
    c_in = [a for cm in carried for a in cm.arrays]
    c_out = [s for cm in carried for s in cm.out_shapes]
    c_scr = []
    aliases = {}
    in_off, out_off = n_pf + n_in, n_out
    for cm in carried:
        c_scr += [pltpu.SemaphoreType.DMA((cm.n_remote,)), pltpu.SemaphoreType.DMA((cm.n_remote,)),
                  pltpu.SemaphoreType.DMA((max(cm.n_local, 1),))]
        for k, v in cm.aliases.items():
            aliases[in_off + k] = out_off + v
        in_off += len(cm.arrays)
        out_off += len(cm.out_shapes)
    steps = 1
    for g in grid:
        steps *= g
    forward_steps = [min(int(cm.forward_at * steps), steps - 1) for cm in carried]

    def wrapped(*refs):
        pf = refs[:n_pf]
        p = n_pf
        ins = refs[p:p + n_in]
        p += n_in
        cins = refs[p:p + len(c_in)]
        p += len(c_in)
        outs = refs[p:p + n_out]
        p += n_out
        couts = refs[p:p + len(c_out)]
        p += len(c_out)
        scr = refs[p:p + n_scr]
        csem = refs[p + n_scr:]
        lin = 0
        for axis, g in enumerate(grid):
            lin = lin * g + pl.program_id(axis)
        views = []
        io = oo = 0
        for j, cm in enumerate(carried):
            views.append((cins[io:io + len(cm.arrays)], couts[oo:oo + len(cm.out_shapes)],
                          csem[3 * j], csem[3 * j + 1], csem[3 * j + 2]))
            io += len(cm.arrays)
            oo += len(cm.out_shapes)

        @pl.when(lin == 0)
        def _():
            for cm, v in zip(carried, views):
                cm.start(*v)

        body(*pf, *ins, *outs, *scr)

        for cm, v, step in zip(carried, views, forward_steps):
            pl.when(lin == step)(functools.partial(cm.forward, *v))

        @pl.when(lin == steps - 1)
        def _():
            for cm, v in zip(carried, views):
                cm.finish(*v)

    res = launch(wrapped, list(in_specs) + [hbm] * len(c_in), out_specs_l + [hbm] * len(c_out),
                 out_shape_l + c_out, list(scratch_shapes) + c_scr, (*args, *c_in), aliases)
    oo = n_out
    for cm in carried:
        cm.results = list(res[oo:oo + len(cm.out_shapes)])
        oo += len(cm.out_shapes)
    return res[0] if single else list(res[:n_out])


def _embed(x, meta, target, g, pad, name, carried=()):
    seq, d = x.shape
    n_meta = meta.shape[0]
    lead = pad + n_meta
    m = lead + seq
    tr = ROW_ALIGN
    lead_blocks = lead // tr
    meta_row = pad - (lead_blocks - 1) * tr
    assert lead % tr == 0 and seq % tr == 0 and 0 <= meta_row and meta_row % 8 == 0

    def body(x_ref, meta_ref, t_ref, g_ref, h_ref, n_ref, tp_ref):
        i = pl.program_id(0)

        @pl.when(i < lead_blocks)
        def _():
            h_ref[...] = jnp.zeros_like(h_ref)
            tp_ref[...] = jnp.zeros_like(tp_ref)

        @pl.when(i == lead_blocks - 1)
        def _():
            h_ref[pl.ds(meta_row, n_meta), :] = meta_ref[...]

        @pl.when(i >= lead_blocks)
        def _():
            h_ref[...] = x_ref[...]
            tp_ref[...] = t_ref[...]

        h = h_ref[...]
        r = lax.rsqrt(jnp.mean(h * h, axis=-1, keepdims=True) + EPS)
        n_ref[...] = (h * r * g_ref[...]).astype(n_ref.dtype)

    tokens = pl.BlockSpec((tr, d), lambda i: (jnp.maximum(i - lead_blocks, 0), 0))
    rows = pl.BlockSpec((tr, d), lambda i: (i, 0))
    return _call(
        body, grid=(m // tr,),
        in_specs=[tokens, pl.BlockSpec((n_meta, d), lambda i: (0, 0)), tokens, pl.BlockSpec((1, d), lambda i: (0, 0))],
        out_specs=[rows, rows, rows],
        out_shape=[jax.ShapeDtypeStruct((m, d), F32), jax.ShapeDtypeStruct((m, d), MXU_DTYPE),
                   jax.ShapeDtypeStruct((m, d), F32)],
        name=name, args=(x, meta, target, g), carried=carried)


def _rmsnorm_bwd_rows(x, g, dy):
    r = lax.rsqrt(jnp.mean(x * x, axis=-1, keepdims=True) + EPS)
    xh = x * r
    dyh = dy * g
    dx = r * (dyh - xh * jnp.mean(dyh * xh, axis=-1, keepdims=True))
    return dx, dy * xh


def _norm_bwd(x, g, dy, scale, name, carried=()):
    m, d = x.shape
    tm = _tile(m, 528, SUBLANE_BF16)

    def body(x_ref, g_ref, dy_ref, dx_ref, dg_ref):
        @pl.when(pl.program_id(0) == 0)
        def _():
            dg_ref[...] = jnp.zeros_like(dg_ref)

        dx, dgr = _rmsnorm_bwd_rows(x_ref[...], g_ref[...], scale * dy_ref[...])
        dx_ref[...] = dx.astype(dx_ref.dtype)
        dg_ref[...] += jnp.sum(dgr, axis=0, keepdims=True)

    return _call(
        body, grid=(m // tm,),
        in_specs=[pl.BlockSpec((tm, d), lambda i: (i, 0)), pl.BlockSpec((1, d), lambda i: (0, 0)),
                  pl.BlockSpec((tm, d), lambda i: (i, 0))],
        out_specs=[pl.BlockSpec((tm, d), lambda i: (i, 0)), pl.BlockSpec((1, d), lambda i: (0, 0))],
        out_shape=[jax.ShapeDtypeStruct((m, d), MXU_DTYPE), jax.ShapeDtypeStruct((1, d), F32)],
        name=name, args=(x, g, dy), carried=carried)


def _dot_nt(a, b):
    return lax.dot_general(a, b, (((1,), (1,)), ((), ())), preferred_element_type=F32)


def _dot_tn(a, b):
    return lax.dot_general(a, b, (((0,), (0,)), ((), ())), preferred_element_type=F32)


def _mm_nt(a, w, name, carried=(), out_dtype=F32):
    m, k = a.shape
    n = w.shape[0]
    tm = _tile(m, 1056, SUBLANE_BF16)
    tn = _tile(n, 512, LANE)

    def body(a_ref, w_ref, o_ref):
        o_ref[...] = _dot_nt(a_ref[...], w_ref[...]).astype(o_ref.dtype)

    return _call(
        body, grid=(m // tm, n // tn),
        in_specs=[pl.BlockSpec((tm, k), lambda i, j: (i, 0)), pl.BlockSpec((tn, k), lambda i, j: (j, 0))],
        out_specs=pl.BlockSpec((tm, tn), lambda i, j: (i, j)),
        out_shape=jax.ShapeDtypeStruct((m, n), out_dtype), name=name, args=(a, w), carried=carried)


def _ffn_up_act(n_act, wu_t, g_act, name, carried=()):
    m, d = n_act.shape
    fp = wu_t.shape[0]
    tm = _tile(m, 1056, SUBLANE_BF16)
    tn = _tile(fp, 512, LANE)

    def body(n_ref, wu_ref, g_ref, u_ref, a_ref):
        u = _dot_nt(n_ref[...], wu_ref[...])
        g = g_ref[...].astype(F32)
        u_ref[...] = u.astype(u_ref.dtype)
        a_ref[...] = (g * jax.nn.sigmoid(g) * u).astype(a_ref.dtype)

    act = pl.BlockSpec((tm, tn), lambda i, j: (i, j))
    return _call(
        body, grid=(m // tm, fp // tn),
        in_specs=[pl.BlockSpec((tm, d), lambda i, j: (i, 0)), pl.BlockSpec((tn, d), lambda i, j: (j, 0)), act],
        out_specs=[act, act],
        out_shape=[jax.ShapeDtypeStruct((m, fp), MXU_DTYPE)] * 2, name=name, args=(n_act, wu_t, g_act), carried=carried)


def _ffn_gate_up(n_act, wg_t, wu_t, name, carried=()):
    m, d = n_act.shape
    fp = wg_t.shape[0]
    tm = _tile(m, 1056, SUBLANE_BF16)
    tn = _tile(fp, 512, LANE)

    def body(n_ref, wg_ref, wu_ref, g_ref, u_ref, a_ref):
        n = n_ref[...]
        g = _dot_nt(n, wg_ref[...])
        u = _dot_nt(n, wu_ref[...])
        g_ref[...] = g.astype(g_ref.dtype)
        u_ref[...] = u.astype(u_ref.dtype)
        a_ref[...] = (g * jax.nn.sigmoid(g) * u).astype(a_ref.dtype)

    act = pl.BlockSpec((tm, tn), lambda i, j: (i, j))
    wsp = pl.BlockSpec((tn, d), lambda i, j: (j, 0))
    return _call(
        body, grid=(m // tm, fp // tn),
        in_specs=[pl.BlockSpec((tm, d), lambda i, j: (i, 0)), wsp, wsp],
        out_specs=[act, act, act],
        out_shape=[jax.ShapeDtypeStruct((m, fp), MXU_DTYPE)] * 3, name=name, args=(n_act, wg_t, wu_t), carried=carried)


def _ffn_hidden_bwd(dfo, wd, g_act, u_act, name, carried=()):
    m, d = dfo.shape
    fp = wd.shape[0]
    tm = _tile(m, 1056, SUBLANE_BF16)
    tn = _tile(fp, 512, LANE)

    def body(df_ref, wd_ref, g_ref, u_ref, dg_ref, du_ref):
        da = _dot_nt(df_ref[...], wd_ref[...])
        g = g_ref[...].astype(F32)
        u = u_ref[...].astype(F32)
        s = jax.nn.sigmoid(g)
        du_ref[...] = (da * (g * s)).astype(du_ref.dtype)
        dg_ref[...] = (da * u * (s * (1.0 + g * (1.0 - s)))).astype(dg_ref.dtype)

    act = pl.BlockSpec((tm, tn), lambda i, j: (i, j))
    return _call(
        body, grid=(m // tm, fp // tn),
        in_specs=[pl.BlockSpec((tm, d), lambda i, j: (i, 0)), pl.BlockSpec((tn, d), lambda i, j: (j, 0)), act, act],
        out_specs=[act, act],
        out_shape=[jax.ShapeDtypeStruct((m, fp), MXU_DTYPE)] * 2, name=name, args=(dfo, wd, g_act, u_act),
        carried=carried)


def _row_groups(n_tiles, max_group, nk):
    gsz = max(q for q in range(1, max_group + 1) if n_tiles % q == 0)

    def epilogue_row(grp, kk, i):
        return grp * gsz + jnp.where(kk == nk - 1, i, 0)

    return gsz, epilogue_row


def _mm_residual_norm(a, w, h, g, scale, next_g, name, carried=()):
    m, k = a.shape
    d = w.shape[1]
    tm = _tile(m, ACC_ROWS, SUBLANE_BF16)
    tk = _tile(k, K_TILE, LANE)
    nk = k // tk
    gsz, epilogue_row = _row_groups(m // tm, ACC_GROUP, nk)

    def body(a_ref, w_ref, h_ref, g_ref, ng_ref, fo_ref, hn_ref, nn_ref, acc_ref):
        kk, i = pl.program_id(1), pl.program_id(2)

        @pl.when(kk == 0)
        def _():
            acc_ref[i] = jnp.zeros((tm, d), F32)

        acc_ref[i] += jnp.dot(a_ref[...], w_ref[...], preferred_element_type=F32)

        @pl.when(kk == nk - 1)
        def _():
            fo = acc_ref[i]
            fo_ref[...] = fo
            r = lax.rsqrt(jnp.mean(fo * fo, axis=-1, keepdims=True) + EPS)
            hn = h_ref[...] + scale * (fo * r * g_ref[...])
            hn_ref[...] = hn
            rn = lax.rsqrt(jnp.mean(hn * hn, axis=-1, keepdims=True) + EPS)
            nn_ref[...] = (hn * rn * ng_ref[...]).astype(nn_ref.dtype)

    row = pl.BlockSpec((tm, d), lambda grp, kk, i: (epilogue_row(grp, kk, i), 0))
    row_once = pl.BlockSpec((tm, d), lambda grp, kk, i: (epilogue_row(grp, kk, i), 0), pipeline_mode=pl.Buffered(1))
    vec = pl.BlockSpec((1, d), lambda grp, kk, i: (0, 0))
    return _call(
        body, grid=(m // tm // gsz, nk, gsz),
        in_specs=[pl.BlockSpec((tm, tk), lambda grp, kk, i: (grp * gsz + i, kk)),
                  pl.BlockSpec((tk, d), lambda grp, kk, i: (kk, 0)), row_once, vec, vec],
        out_specs=[row, row, row],
        out_shape=[jax.ShapeDtypeStruct((m, d), F32)] * 2 + [jax.ShapeDtypeStruct((m, d), MXU_DTYPE)],
        scratch_shapes=[pltpu.VMEM((gsz, tm, d), F32)], name=name, args=(a, w, h, g, next_g), carried=carried)


def _mm_residual_loss(a, w, h, g, scale, target, lead, name, carried=()):
    m, k = a.shape
    d = w.shape[1]
    tm = _tile(m, ACC_ROWS, SUBLANE_BF16)
    tk = _tile(k, K_TILE, LANE)
    nk = k // tk
    gsz, epilogue_row = _row_groups(m // tm, ACC_GROUP, nk)

    def body(a_ref, w_ref, h_ref, g_ref, t_ref, dy_ref, dfo_ref, dg_ref, l_ref, acc_ref):
        grp, kk, i = pl.program_id(0), pl.program_id(1), pl.program_id(2)

        @pl.when(jnp.logical_and(jnp.logical_and(grp == 0, kk == 0), i == 0))
        def _():
            dg_ref[...] = jnp.zeros_like(dg_ref)
            l_ref[...] = jnp.zeros_like(l_ref)

        @pl.when(kk == 0)
        def _():
            acc_ref[i] = jnp.zeros((tm, d), F32)

        acc_ref[i] += jnp.dot(a_ref[...], w_ref[...], preferred_element_type=F32)

        @pl.when(kk == nk - 1)
        def _():
            fo = acc_ref[i]
            gain = g_ref[...]
            r = lax.rsqrt(jnp.mean(fo * fo, axis=-1, keepdims=True) + EPS)
            xh = fo * r
            y = h_ref[...] + scale * (xh * gain)
            row = (grp * gsz + i) * tm + lax.broadcasted_iota(jnp.int32, (tm, 1), 0)
            e = jnp.where(row >= lead, y - t_ref[...], 0.0)
            dy = e * (1.0 / d)
            dy_ref[...] = dy
            l_ref[...] += 0.5 * jnp.sum(jnp.sum(e * e, axis=-1, keepdims=True) * (1.0 / d), axis=0, keepdims=True)
            dn = scale * dy
            dyh = dn * gain
            dfo_ref[...] = (r * (dyh - xh * jnp.mean(dyh * xh, axis=-1, keepdims=True))).astype(dfo_ref.dtype)
            dg_ref[...] += jnp.sum(dn * xh, axis=0, keepdims=True)

    row = pl.BlockSpec((tm, d), lambda grp, kk, i: (epilogue_row(grp, kk, i), 0))
    row_once = pl.BlockSpec((tm, d), lambda grp, kk, i: (epilogue_row(grp, kk, i), 0), pipeline_mode=pl.Buffered(1))
    vec = pl.BlockSpec((1, d), lambda grp, kk, i: (0, 0))
    return _call(
        body, grid=(m // tm // gsz, nk, gsz),
        in_specs=[pl.BlockSpec((tm, tk), lambda grp, kk, i: (grp * gsz + i, kk)),
                  pl.BlockSpec((tk, d), lambda grp, kk, i: (kk, 0)), row_once, vec, row_once],
        out_specs=[row, row, vec, pl.BlockSpec((1, 1), lambda grp, kk, i: (0, 0))],
        out_shape=[jax.ShapeDtypeStruct((m, d), F32), jax.ShapeDtypeStruct((m, d), MXU_DTYPE),
                   jax.ShapeDtypeStruct((1, d), F32), jax.ShapeDtypeStruct((1, 1), F32)],
        scratch_shapes=[pltpu.VMEM((gsz, tm, d), F32)], name=name, args=(a, w, h, g, target), carried=carried)


def _norm_bwd_row_tile(m):
    return _tile(m, ACC_ROWS, SUBLANE_BF16)


def _mm_norm_bwd(pairs, h, g, dh_up, name, carried=(), row_tiles=None, dg_init=None):
    n_pairs = len(pairs)
    m, k = pairs[0][0].shape
    d = h.shape[1]
    tm = _norm_bwd_row_tile(m)
    tk = _tile(k, K_TILE, LANE)
    nk = k // tk
    t0, nt = row_tiles if row_tiles is not None else (0, m // tm)
    gsz, epilogue_row = _row_groups(nt, ACC_GROUP, nk)
    if dg_init is None:
        dg_init = jnp.zeros((1, d), F32)

    def body(*refs):
        ops = refs[:2 * n_pairs]
        h_ref, g_ref, up_ref, init_ref, dh_ref, dg_ref, acc_ref = refs[2 * n_pairs:]
        grp, kk, i = pl.program_id(0), pl.program_id(1), pl.program_id(2)

        @pl.when(jnp.logical_and(jnp.logical_and(grp == 0, kk == 0), i == 0))
        def _():
            dg_ref[...] = init_ref[...]

        @pl.when(kk == 0)
        def _():
            acc_ref[i] = jnp.zeros((tm, d), F32)

        for p in range(n_pairs):
            acc_ref[i] += jnp.dot(ops[2 * p][...], ops[2 * p + 1][...], preferred_element_type=F32)

        @pl.when(kk == nk - 1)
        def _():
            dx, dgr = _rmsnorm_bwd_rows(h_ref[...], g_ref[...], acc_ref[i])
            dh_ref[...] = up_ref[...] + dx
            dg_ref[...] += jnp.sum(dgr, axis=0, keepdims=True)

    row_in = pl.BlockSpec((tm, d), lambda grp, kk, i: (t0 + epilogue_row(grp, kk, i), 0))
    vec = pl.BlockSpec((1, d), lambda grp, kk, i: (0, 0))
    in_specs = []
    args = []
    for a, w in pairs:
        in_specs += [pl.BlockSpec((tm, tk), lambda grp, kk, i: (t0 + grp * gsz + i, kk)),
                     pl.BlockSpec((tk, d), lambda grp, kk, i: (kk, 0))]
        args += [a, w]
    return _call(
        body, grid=(nt // gsz, nk, gsz),
        in_specs=in_specs + [row_in, vec, row_in, vec],
        out_specs=[pl.BlockSpec((tm, d), lambda grp, kk, i: (epilogue_row(grp, kk, i), 0)), vec],
        out_shape=[jax.ShapeDtypeStruct((nt * tm, d), F32), jax.ShapeDtypeStruct((1, d), F32)],
        scratch_shapes=[pltpu.VMEM((gsz, tm, d), F32)], name=name, args=(*args, h, g, dh_up, dg_init), carried=carried)


def _mm_tn(a, b, name, carried=()):
    m, ka = a.shape
    d = b.shape[1]
    tf = _tile(ka, 512, LANE)

    def body(a_ref, b_ref, o_ref):
        o_ref[...] = _dot_tn(a_ref[...], b_ref[...]).astype(o_ref.dtype)

    return _call(
        body, grid=(ka // tf,),
        in_specs=[pl.BlockSpec((m, tf), lambda j: (0, j)),
                  pl.BlockSpec((m, d), lambda j: (0, 0), pipeline_mode=pl.Buffered(1))],
        out_specs=pl.BlockSpec((tf, d), lambda j: (j, 0)),
        out_shape=jax.ShapeDtypeStruct((ka, d), WIRE_DTYPE), name=name, args=(a, b), carried=carried)


GELU_K = 0.7978845608028654
GELU_C = 0.044715


def _expm1(x):
    series = x * (1.0 + x * (1.0 / 2 + x * (1.0 / 6 + x * (1.0 / 24 + x * (1.0 / 120 + x * (1.0 / 720 + x * (1.0 / 5040)))))))
    return jnp.where(jnp.abs(x) < 0.3, series, jnp.exp(x) - 1.0)


def _softplus(x):
    return jnp.maximum(x, 0.0) + jnp.log1p(jnp.exp(-jnp.abs(x)))


def _block_mm(v, w_ref, transposed):
    nbk = w_ref.shape[0]
    outs = []
    for j in range(nbk):
        vj = v[:, j * BD:(j + 1) * BD]
        outs.append(_dot_nt(vj, w_ref[j]) if transposed else jnp.dot(vj, w_ref[j], preferred_element_type=F32))
    return outs[0] if nbk == 1 else jnp.concatenate(outs, axis=1)


def _group_mean(q, gm_ref):
    hi = q.astype(MXU_DTYPE)
    lo = (q - hi.astype(F32)).astype(MXU_DTYPE)
    nbk = q.shape[1] // BD
    gm = gm_ref[...]
    outs = []
    for j in range(nbk):
        sl = slice(j * BD, (j + 1) * BD)
        outs.append(jnp.dot(hi[:, sl], gm, preferred_element_type=F32) + jnp.dot(lo[:, sl], gm, preferred_element_type=F32))
    return outs[0] if nbk == 1 else jnp.concatenate(outs, axis=1)


class _RowReader:
    def __init__(self, ref):
        self.ref = ref

    def __getitem__(self, rows):
        return self.ref[rows, :]


def _shifted(ext_ref, cur, before8, after8, downs=(), ups=()):
    r = cur.shape[0]
    if downs:
        ext_ref[0:8, :] = before8
    ext_ref[8:8 + r, :] = cur
    if ups:
        ext_ref[8 + r:16 + r, :] = after8
    return [ext_ref[pl.ds(8 - j, r), :] for j in downs] + [ext_ref[pl.ds(8 + j, r), :] for j in ups]


def _lru_gates(xc, pv, wa_ref, wx_ref):
    xcb = xc.astype(MXU_DTYPE)
    ga = jax.nn.sigmoid(_block_mm(xcb, wa_ref, False) + pv[5:6])
    gx = jax.nn.sigmoid(_block_mm(xcb, wx_ref, False) + pv[6:7])
    sp = _softplus(-pv[7:8])
    log_a = -LRU_C * ga * sp
    a = jnp.exp(log_a)
    e2 = _expm1(2.0 * log_a)
    mult = jnp.sqrt(-e2)
    return xcb, ga, gx, sp, a, e2, mult


def _gelu_parts(y):
    th = jnp.tanh(GELU_K * (y + GELU_C * y * y * y))
    return 0.5 * y * (1.0 + th), th


def _mixer_fwd(z, pv, wa, wx, gm, pad, name, carried=()):
    m = z.shape[0]
    c = pv.shape[1]
    r = MIX_ROWS
    nb = m // r

    def body(z_ref, pv_ref, wa_ref, wx_ref, gm_ref, mixed_ref, hs_ref, ext_ref, tailx_ref, tailc_ref, carry_ref):
        b = pl.program_id(0)

        @pl.when(b == 0)
        def _():
            tailx_ref[...] = jnp.zeros_like(tailx_ref)
            tailc_ref[...] = jnp.zeros_like(tailc_ref)
            carry_ref[...] = jnp.zeros_like(carry_ref)

        pv = _RowReader(pv_ref)
        row = b * r + lax.broadcasted_iota(jnp.int32, (r, 1), 0)
        lrow = lax.broadcasted_iota(jnp.int32, (r, c), 0)
        maskf = (row >= pad).astype(F32)
        y = z_ref[:, 0:c]
        xl = z_ref[:, c:2 * c]
        bs = z_ref[:, 2 * c:3 * c]
        cv = z_ref[:, 3 * c:4 * c] * z_ref[:, 4 * c:5 * c]

        x1, x2, x3 = _shifted(ext_ref, xl, tailx_ref[...], None, downs=(1, 2, 3))
        tailx_ref[...] = z_ref[pl.ds(r - 8, 8), c:2 * c]
        xc = pv[4:5] + pv[3:4] * xl + pv[2:3] * x1 + pv[1:2] * x2 + pv[0:1] * x3
        _, _, gx, _, a, _, mult = _lru_gates(xc, pv, wa_ref, wx_ref)
        uu = mult * (gx * xc) * maskf

        acc_a = a
        acc_h = uu
        dlt = 1
        while dlt < r:
            keep = lrow >= dlt
            sh_a = pltpu.roll(acc_a, dlt, axis=0)
            sh_h = pltpu.roll(acc_h, dlt, axis=0)
            acc_h = acc_h + acc_a * jnp.where(keep, sh_h, 0.0)
            acc_a = acc_a * jnp.where(keep, sh_a, 1.0)
            dlt *= 2
        hs = acc_h + acc_a * carry_ref[...]
        hs_ref[...] = hs
        carry_ref[...] = hs_ref[pl.ds(r - 1, 1), :]

        gelu_y, _ = _gelu_parts(y)
        lru_out = hs * gelu_y
        c1, c2 = _shifted(ext_ref, cv, tailc_ref[...], None, downs=(1, 2))
        tailc_ref[...] = cv[r - 8:r]
        sc_out = bs * (pv[10:11] * cv + pv[9:10] * c1 + pv[8:9] * c2)

        rl = lax.rsqrt(_group_mean(lru_out * lru_out, gm_ref) + EPS)
        rs = lax.rsqrt(_group_mean(sc_out * sc_out, gm_ref) + EPS)
        mixed_ref[:, 0:c] = (lru_out * rl * pv[11:12]).astype(mixed_ref.dtype)
        mixed_ref[:, c:2 * c] = (sc_out * rs * pv[12:13]).astype(mixed_ref.dtype)

    full = lambda shape: pl.BlockSpec(shape, lambda b: (0,) * len(shape))
    return _call(
        body, grid=(nb,),
        in_specs=[pl.BlockSpec((r, 5 * c), lambda b: (b, 0)), full(pv.shape), full(wa.shape), full(wx.shape), full(gm.shape)],
        out_specs=[pl.BlockSpec((r, 2 * c), lambda b: (b, 0)), pl.BlockSpec((r, c), lambda b: (b, 0))],
        out_shape=[jax.ShapeDtypeStruct((m, 2 * c), MXU_DTYPE), jax.ShapeDtypeStruct((m, c), F32)],
        scratch_shapes=[pltpu.VMEM((r + 16, c), F32), pltpu.VMEM((8, c), F32), pltpu.VMEM((8, c), F32),
                        pltpu.VMEM((1, c), F32)],
        name=name, args=(z, pv, wa, wx, gm), carried=carried)


def _mixer_bwd(z, hs, dmixed, pv, wa, wx, gm, pad, name, carried=()):
    m = z.shape[0]
    c = pv.shape[1]
    r = MIX_ROWS
    nb = m // r
    r8 = r // 8

    def body(z_ref, zp_ref, hs_ref, hsp_ref, dm_ref, pv_ref, wa_ref, wx_ref, gm_ref,
             dz_ref, dpv_ref, dwa_ref, dwx_ref, ext_ref, hxc_ref, hsc_ref, hp_ref):
        i = pl.program_id(0)
        b = nb - 1 - i

        @pl.when(i == 0)
        def _():
            hxc_ref[...] = jnp.zeros_like(hxc_ref)
            hsc_ref[...] = jnp.zeros_like(hsc_ref)
            hp_ref[...] = jnp.zeros_like(hp_ref)
            dpv_ref[...] = jnp.zeros_like(dpv_ref)
            dwa_ref[...] = jnp.zeros_like(dwa_ref)
            dwx_ref[...] = jnp.zeros_like(dwx_ref)

        pv = _RowReader(pv_ref)
        row = b * r + lax.broadcasted_iota(jnp.int32, (r, 1), 0)
        lrow = lax.broadcasted_iota(jnp.int32, (r, c), 0)
        maskf = (row >= pad).astype(F32)
        has_prev = (b > 0).astype(F32)
        y = z_ref[:, 0:c]
        xl = z_ref[:, c:2 * c]
        bs = z_ref[:, 2 * c:3 * c]
        cs = z_ref[:, 3 * c:4 * c]
        vs = z_ref[:, 4 * c:5 * c]
        cv = cs * vs
        xl_prev = zp_ref[:, c:2 * c] * has_prev
        cv_prev = zp_ref[:, 3 * c:4 * c] * zp_ref[:, 4 * c:5 * c] * has_prev
        hs = hs_ref[...]

        x1, x2, x3 = _shifted(ext_ref, xl, xl_prev, None, downs=(1, 2, 3))
        xc = pv[4:5] + pv[3:4] * xl + pv[2:3] * x1 + pv[1:2] * x2 + pv[0:1] * x3
        xcb, ga, gx, sp, a, e2, mult = _lru_gates(xc, pv, wa_ref, wx_ref)
        gxx = gx * xc
        gelu_y, th = _gelu_parts(y)
        lru_out = hs * gelu_y
        c1, c2 = _shifted(ext_ref, cv, cv_prev, None, downs=(1, 2))
        sc = pv[10:11] * cv + pv[9:10] * c1 + pv[8:9] * c2
        sc_out = bs * sc

        def group_norm_bwd(v, dm, gain):
            rr = lax.rsqrt(_group_mean(v * v, gm_ref) + EPS)
            vh = v * rr
            dvh = dm * gain
            dv = rr * (dvh - vh * _group_mean(dvh * vh, gm_ref))
            return dv, jnp.sum(dm * vh, axis=0, keepdims=True)

        d_lru_out, d_og = group_norm_bwd(lru_out, dm_ref[:, 0:c], pv[11:12])
        d_sc_out, d_sg = group_norm_bwd(sc_out, dm_ref[:, c:2 * c], pv[12:13])
        dpv_ref[11:12, :] += d_og
        dpv_ref[12:13, :] += d_sg

        dhs = d_lru_out * gelu_y
        dgelu = 0.5 * (1.0 + th) + 0.5 * y * (1.0 - th * th) * GELU_K * (1.0 + 3.0 * GELU_C * y * y)
        dy = d_lru_out * hs * dgelu

        acc_a = a
        acc_p = a * dhs
        dlt = 1
        while dlt < r:
            keep = lrow < r - dlt
            sh_a = pltpu.roll(acc_a, r - dlt, axis=0)
            sh_p = pltpu.roll(acc_p, r - dlt, axis=0)
            acc_p = acc_p + acc_a * jnp.where(keep, sh_p, 0.0)
            acc_a = acc_a * jnp.where(keep, sh_a, 1.0)
            dlt *= 2
        p_all = acc_p + acc_a * hp_ref[0:1, :]
        (p_next,) = _shifted(ext_ref, p_all, None, hp_ref[...], ups=(1,))
        hp_ref[...] = p_all[0:8]
        q = dhs + p_next
        (hs_prev,) = _shifted(ext_ref, hs, hsp_ref[...] * has_prev, None, downs=(1,))
        duu = q * maskf
        da = q * hs_prev

        dmult = duu * gxx
        dgxx = duu * mult
        dgx = dgxx * xc
        dxc = dgxx * gx
        dlog_a = da * a - dmult * ((1.0 + e2) / mult)
        dga = dlog_a * (-LRU_C * sp)
        dsp = jnp.sum(dlog_a * (-LRU_C * ga), axis=0, keepdims=True)
        dpv_ref[7:8, :] += dsp * (-jax.nn.sigmoid(-pv[7:8]))
        dga_pre = dga * ga * (1.0 - ga)
        dgx_pre = dgx * gx * (1.0 - gx)
        dpv_ref[5:6, :] += jnp.sum(dga_pre, axis=0, keepdims=True)
        dpv_ref[6:7, :] += jnp.sum(dgx_pre, axis=0, keepdims=True)
        dga_b = dga_pre.astype(MXU_DTYPE)
        dgx_b = dgx_pre.astype(MXU_DTYPE)
        dxc = dxc + _block_mm(dga_b, wa_ref, True) + _block_mm(dgx_b, wx_ref, True)
        for j in range(c // BD):
            sl = slice(j * BD, (j + 1) * BD)
            dwa_ref[j] += _dot_tn(xcb[:, sl], dga_b[:, sl])
            dwx_ref[j] += _dot_tn(xcb[:, sl], dgx_b[:, sl])

        dpv_ref[4:5, :] += jnp.sum(dxc, axis=0, keepdims=True)
        dpv_ref[3:4, :] += jnp.sum(dxc * xl, axis=0, keepdims=True)
        dpv_ref[2:3, :] += jnp.sum(dxc * x1, axis=0, keepdims=True)
        dpv_ref[1:2, :] += jnp.sum(dxc * x2, axis=0, keepdims=True)
        dpv_ref[0:1, :] += jnp.sum(dxc * x3, axis=0, keepdims=True)
        u1, u2, u3 = _shifted(ext_ref, dxc, None, hxc_ref[...], ups=(1, 2, 3))
        hxc_ref[...] = dxc[0:8]
        dxl = pv[3:4] * dxc + pv[2:3] * u1 + pv[1:2] * u2 + pv[0:1] * u3

        dbs = d_sc_out * sc
        dsc = d_sc_out * bs
        dpv_ref[10:11, :] += jnp.sum(dsc * cv, axis=0, keepdims=True)
        dpv_ref[9:10, :] += jnp.sum(dsc * c1, axis=0, keepdims=True)
        dpv_ref[8:9, :] += jnp.sum(dsc * c2, axis=0, keepdims=True)
        s1, s2 = _shifted(ext_ref, dsc, None, hsc_ref[...], ups=(1, 2))
        hsc_ref[...] = dsc[0:8]
        dcv = pv[10:11] * dsc + pv[9:10] * s1 + pv[8:9] * s2

        dz_ref[:, 0:c] = (dy * maskf).astype(dz_ref.dtype)
        dz_ref[:, c:2 * c] = (dxl * maskf).astype(dz_ref.dtype)
        dz_ref[:, 2 * c:3 * c] = (dbs * maskf).astype(dz_ref.dtype)
        dz_ref[:, 3 * c:4 * c] = (dcv * vs * maskf).astype(dz_ref.dtype)
        dz_ref[:, 4 * c:5 * c] = (dcv * cs * maskf).astype(dz_ref.dtype)

    full = lambda shape: pl.BlockSpec(shape, lambda i: (0,) * len(shape))
    cur = lambda width: pl.BlockSpec((r, width), lambda i: (nb - 1 - i, 0))
    prev8 = lambda width: pl.BlockSpec((8, width), lambda i: (jnp.maximum((nb - 1 - i) * r8 - 1, 0), 0))
    return _call(
        body, grid=(nb,),
        in_specs=[cur(5 * c), prev8(5 * c), cur(c), prev8(c), cur(2 * c),
                  full(pv.shape), full(wa.shape), full(wx.shape), full(gm.shape)],
        out_specs=[cur(5 * c), full(pv.shape), full(wa.shape), full(wx.shape)],
        out_shape=[jax.ShapeDtypeStruct((m, 5 * c), MXU_DTYPE), jax.ShapeDtypeStruct(pv.shape, F32),
                   jax.ShapeDtypeStruct(wa.shape, F32), jax.ShapeDtypeStruct(wx.shape, F32)],
        scratch_shapes=[pltpu.VMEM((r + 16, c), F32), pltpu.VMEM((8, c), F32), pltpu.VMEM((8, c), F32),
                        pltpu.VMEM((8, c), F32)],
        name=name, args=(z, z, hs, hs, dmixed, pv, wa, wx, gm), carried=carried)


def _position():
    return lax.axis_index("x"), lax.axis_index("y"), lax.axis_index("c")


def _block_of(px, py, pc):
    return 4 * px + 2 * py + pc


class _TwoLevelGather:
    def __init__(self, n_arrays, rows_of, src_of, send_sems, recv_sems):
        x, y, c = _position()
        self.n, self.rows_of, self.src_of = n_arrays, rows_of, src_of
        self.send_sems, self.recv_sems = send_sems, recv_sems
        self.c, self.me, self.sibling = c, (x, y, c), (x, y, 1 - c)
        self.chips = [(1 - x, y), (x, 1 - y), (1 - x, 1 - y)]

    def _copy(self, i, k, block, to, src=None):
        return pltpu.make_async_remote_copy(
            src_ref=self.rows_of(i, *block) if src is None else src, dst_ref=self.rows_of(i, *block),
            send_sem=self.send_sems.at[7 * i + k], recv_sem=self.recv_sems.at[7 * i + k],
            device_id=to, device_id_type=MESH)

    def _first(self, i):
        own = [self._copy(i, 0, self.me, self.sibling, src=self.src_of(i))]
        return own + [self._copy(i, 1 + j, self.me, (*chip, self.c), src=self.src_of(i))
                      for j, chip in enumerate(self.chips)]

    def _passed(self, i, j):
        return self._copy(i, 4 + j, (*self.chips[j], self.c), self.sibling)

    def start(self):
        for i in range(self.n):
            for cp in self._first(i):
                cp.start()

    def forward(self):
        for i in range(self.n):
            for j, chip in enumerate(self.chips):
                self._copy(i, 1 + j, (*chip, self.c), self.me).wait_recv()
                self._passed(i, j).start()

    def drain(self):
        for i in range(self.n):
            self._copy(i, 0, self.sibling, self.me).wait_recv()
            for j, chip in enumerate(self.chips):
                self._copy(i, 4 + j, (*chip, 1 - self.c), self.me).wait_recv()
        for i in range(self.n):
            for cp in self._first(i) + [self._passed(i, j) for j in range(3)]:
                cp.wait_send()


class _CarriedGather:
    def __init__(self, shards, padded_rows, zeros, forward_at):
        d = shards[0].shape[1]
        self.forward_at = forward_at
        self.n = len(shards)
        self.rows = [s.shape[0] for s in shards]
        self.pads = [p - N_DEV * r for r, p in zip(self.rows, padded_rows)]
        assert max(self.pads) <= zeros.shape[0] and zeros.shape[1] == d
        self.arrays = list(shards) + [zeros]
        self.out_shapes = [jax.ShapeDtypeStruct((p, d), s.dtype) for s, p in zip(shards, padded_rows)]
        self.aliases = {}
        self.n_remote, self.n_local = 7 * self.n, 2 * self.n
        self.results = None

    def _rows_of(self, outs):
        def rows_of(i, px, py, pc):
            s = self.rows[i]
            return outs[i].at[pl.ds(pl.multiple_of(_block_of(px, py, pc) * s, SUBLANE_BF16), s), :]
        return rows_of

    def _gather(self, ins, outs, send_sems, recv_sems):
        return _TwoLevelGather(self.n, self._rows_of(outs), lambda i: ins[i], send_sems, recv_sems)

    def _local(self, ins, outs, local_sems):
        x, y, c = _position()
        rows_of = self._rows_of(outs)
        cps = []
        for i in range(self.n):
            cps.append(pltpu.make_async_copy(ins[i], rows_of(i, x, y, c), local_sems.at[2 * i]))
            if self.pads[i]:
                cps.append(pltpu.make_async_copy(ins[self.n].at[pl.ds(0, self.pads[i]), :],
                                                 outs[i].at[pl.ds(N_DEV * self.rows[i], self.pads[i]), :],
                                                 local_sems.at[2 * i + 1]))
        return cps

    def start(self, ins, outs, send_sems, recv_sems, local_sems):
        for cp in self._local(ins, outs, local_sems):
            cp.start()
        self._gather(ins, outs, send_sems, recv_sems).start()

    def forward(self, ins, outs, send_sems, recv_sems, local_sems):
        self._gather(ins, outs, send_sems, recv_sems).forward()

    def finish(self, ins, outs, send_sems, recv_sems, local_sems):
        self._gather(ins, outs, send_sems, recv_sems).drain()
        for cp in self._local(ins, outs, local_sems):
            cp.wait()


class _CarriedSwap:
    def __init__(self, grads, shard_rows):
        d = grads[0].shape[1]
        self.n, self.rows = len(grads), list(shard_rows)
        self.arrays = list(grads)
        self.out_shapes = [jax.ShapeDtypeStruct((4, s, d), g.dtype) for g, s in zip(grads, shard_rows)]
        self.aliases = {}
        self.n_remote, self.n_local = 4 * self.n, 0
        self.forward_at = 1.0
        self.results = None

    def _copies(self, ins, outs, send_sems, recv_sems):
        x, y, c = _position()
        cps = []
        for i in range(self.n):
            s = self.rows[i]
            for k in range(4):
                blk = _block_of(k >> 1, k & 1, 1 - c)
                cps.append(pltpu.make_async_remote_copy(
                    src_ref=ins[i].at[pl.ds(pl.multiple_of(blk * s, SUBLANE_BF16), s), :], dst_ref=outs[i].at[k],
                    send_sem=send_sems.at[4 * i + k], recv_sem=recv_sems.at[4 * i + k],
                    device_id=(x, y, 1 - c), device_id_type=MESH))
        return cps

    def start(self, ins, outs, send_sems, recv_sems, local_sems):
        for cp in self._copies(ins, outs, send_sems, recv_sems):
            cp.start()

    def forward(self, *_):
        pass

    def finish(self, ins, outs, send_sems, recv_sems, local_sems):
        for cp in self._copies(ins, outs, send_sems, recv_sems):
            cp.wait()


class _CarriedChipExchange:
    def __init__(self, presums):
        self.n = len(presums)
        self.arrays = list(presums)
        self.out_shapes = [jax.ShapeDtypeStruct(p.shape, p.dtype) for p in presums]
        self.aliases = {}
        self.n_remote, self.n_local = 3 * self.n, 0
        self.forward_at = 1.0
        self.results = None

    def _copies(self, ins, outs, send_sems, recv_sems):
        x, y, c = _position()
        cps = []
        for i in range(self.n):
            for r in range(1, 4):
                cps.append(pltpu.make_async_remote_copy(
                    src_ref=ins[i].at[r - 1], dst_ref=outs[i].at[r - 1],
                    send_sem=send_sems.at[3 * i + r - 1], recv_sem=recv_sems.at[3 * i + r - 1],
                    device_id=(x ^ (r >> 1), y ^ (r & 1), c), device_id_type=MESH))
        return cps

    def start(self, ins, outs, send_sems, recv_sems, local_sems):
        for cp in self._copies(ins, outs, send_sems, recv_sems):
            cp.start()

    def forward(self, *_):
        pass

    def finish(self, ins, outs, send_sems, recv_sems, local_sems):
        for cp in self._copies(ins, outs, send_sems, recv_sems):
            cp.wait()


def _gather_small(block, reduce, name):
    rr, nn = block.shape

    def body(x_ref, out_ref, *rest):
        if reduce:
            stack_ref, send_sems, recv_sems, local_sem = rest
        else:
            send_sems, recv_sems, local_sem = rest
            stack_ref = out_ref
        x, y, c = _position()

        def rows_of(i, px, py, pc):
            return stack_ref.at[pl.ds(pl.multiple_of(_block_of(px, py, pc) * rr, 8), rr), :]

        own = pltpu.make_async_copy(x_ref, rows_of(0, x, y, c), local_sem)
        own.start()
        gather = _TwoLevelGather(1, rows_of, lambda i: x_ref, send_sems, recv_sems)
        gather.start()
        gather.forward()
        gather.drain()
        own.wait()
        if reduce:
            acc = stack_ref[0:rr, :]
            for k in range(1, N_DEV):
                acc = acc + stack_ref[k * rr:(k + 1) * rr, :]
            out_ref[...] = acc

    vmem = pl.BlockSpec(memory_space=pltpu.VMEM)
    scratch = [pltpu.SemaphoreType.DMA((7,)), pltpu.SemaphoreType.DMA((7,)), pltpu.SemaphoreType.DMA]
    if reduce:
        scratch = [pltpu.VMEM((N_DEV * rr, nn), F32)] + scratch
    out_rows = rr if reduce else N_DEV * rr
    return pl.pallas_call(
        body, in_specs=[vmem], out_specs=vmem, out_shape=jax.ShapeDtypeStruct((out_rows, nn), F32),
        scratch_shapes=scratch, name=name, compiler_params=_params())(block)


def _presum(where, grad, swapped, name):
    s, d = swapped.shape[1], swapped.shape[2]
    tc = _tile(d, 512, LANE)

    def body(where_ref, g_ref, sw_ref, o_ref):
        o_ref[0] = (g_ref[...].astype(F32) + sw_ref[0].astype(F32)).astype(o_ref.dtype)

    return _call(
        body, grid=(3, d // tc),
        in_specs=[pl.BlockSpec((s, tc), lambda r, j, where: (where[1 + r], j)),
                  pl.BlockSpec((1, s, tc), lambda r, j, where: (where[5 + r], 0, j))],
        out_specs=pl.BlockSpec((1, s, tc), lambda r, j, where: (r, 0, j)),
        out_shape=jax.ShapeDtypeStruct((3, s, d), WIRE_DTYPE), name=name, args=(grad, swapped), prefetch=(where,))


def _final_sum(where, grad, swapped, received, name, carried=()):
    s, d = swapped.shape[1], swapped.shape[2]
    tc = _tile(d, 512, LANE)

    def body(where_ref, g_ref, sw_ref, r_ref, o_ref):
        acc = g_ref[...].astype(F32) + sw_ref[0].astype(F32)
        for k in range(3):
            acc = acc + r_ref[k].astype(F32)
        o_ref[...] = acc

    return _call(
        body, grid=(d // tc,),
        in_specs=[pl.BlockSpec((s, tc), lambda j, where: (where[0], j)),
                  pl.BlockSpec((1, s, tc), lambda j, where: (where[4], 0, j)),
                  pl.BlockSpec((3, s, tc), lambda j, where: (0, 0, j))],
        out_specs=pl.BlockSpec((s, tc), lambda j, where: (0, j)),
        out_shape=jax.ShapeDtypeStruct((s, d), F32), name=name, args=(grad, swapped, received),
        prefetch=(where,), carried=carried)


class _GradReduction:
    def __init__(self, key, grad, shard_rows, where):
        self.key, self.grad, self.rows, self.where = key, grad, shard_rows, where

    def swap(self):
        self._swap = _CarriedSwap([self.grad], [self.rows])
        return self._swap

    def exchange(self):
        presum = _presum(self.where, self.grad, self._swap.results[0], "presum_" + self.key)
        self._exchange = _CarriedChipExchange([presum])
        return self._exchange

    def total(self, carried=()):
        return _final_sum(self.where, self.grad, self._swap.results[0], self._exchange.results[0],
                          "sum_" + self.key, carried)

    def total_and_update(self, w, m, v):
        return _sum_adamw(self.where, self.grad, self._swap.results[0], self._exchange.results[0], w, m, v,
                          "update_" + self.key)


def _adamw_math(w, g, m, v):
    nm = ADAM_B1 * m + (1.0 - ADAM_B1) * g
    nv = ADAM_B2 * v + (1.0 - ADAM_B2) * (g * g)
    m_hat = nm / (1.0 - ADAM_B1 ** ADAM_STEP)
    v_hat = nv / (1.0 - ADAM_B2 ** ADAM_STEP)
    return -ADAM_LR * (m_hat / (jnp.sqrt(v_hat) + ADAM_EPS) + ADAM_WD * w), nm, nv


def _sum_adamw(where, grad, swapped, received, w, m, v, name):
    s, d = swapped.shape[1], swapped.shape[2]
    tc = _tile(d, 512, LANE)

    def body(where_ref, g_ref, sw_ref, r_ref, w_ref, m_ref, v_ref, gs_ref, d_ref, nm_ref, nv_ref):
        g = g_ref[...].astype(F32) + sw_ref[0].astype(F32)
        for k in range(3):
            g = g + r_ref[k].astype(F32)
        gs_ref[...] = g
        d_ref[...], nm_ref[...], nv_ref[...] = _adamw_math(w_ref[...], g, m_ref[...], v_ref[...])

    blk = pl.BlockSpec((s, tc), lambda j, where: (0, j))
    return _call(
        body, grid=(d // tc,),
        in_specs=[pl.BlockSpec((s, tc), lambda j, where: (where[0], j)),
                  pl.BlockSpec((1, s, tc), lambda j, where: (where[4], 0, j)),
                  pl.BlockSpec((3, s, tc), lambda j, where: (0, 0, j)), blk, blk, blk],
        out_specs=[blk] * 4, out_shape=[jax.ShapeDtypeStruct((s, d), F32)] * 4, name=name,
        args=(grad, swapped, received, w, m, v), prefetch=(where,))


def _adamw(w, g, m, v, name):
    rows, cols = w.shape
    tr = _tile(rows, 256, 8)

    def body(w_ref, g_ref, m_ref, v_ref, d_ref, nm_ref, nv_ref):
        d_ref[...], nm_ref[...], nv_ref[...] = _adamw_math(w_ref[...], g_ref[...], m_ref[...], v_ref[...])

    spec = pl.BlockSpec((tr, cols), lambda i: (i, 0))
    return pl.pallas_call(
        body, grid=(rows // tr,), in_specs=[spec] * 4, out_specs=[spec] * 3,
        out_shape=[jax.ShapeDtypeStruct((rows, cols), F32)] * 3, name=name, compiler_params=_params())(w, g, m, v)


def _pack_rows(arrays, width, row_quantum=8):
    flat = jnp.concatenate([a.reshape(-1) for a in arrays])
    total = _round_up(flat.shape[0], row_quantum * width)
    flat = jnp.pad(flat, (0, total - flat.shape[0]))
    return flat.reshape(-1, width)


def _unpack_rows(packed, shapes):
    flat = packed.reshape(-1)
    out = []
    off = 0
    for shp in shapes:
        size = 1
        for s in shp:
            size *= s
        out.append(flat[off:off + size].reshape(shp))
        off += size
    return out


def _block_diag(w):
    h, hb, _ = w.shape
    per = BD // hb
    w4 = w.reshape(h // per, per, hb, hb)
    eye = jnp.eye(per, dtype=w.dtype)
    return jnp.einsum('npij,pq->npiqj', w4, eye).reshape(h // per, BD, BD)


def _block_diag_extract(bd, hb):
    nbk = bd.shape[0]
    per = BD // hb
    b5 = bd.reshape(nbk, per, hb, per, hb)
    eye = jnp.eye(per, dtype=bd.dtype)
    return jnp.einsum('npiqj,pq->npij', b5, eye).reshape(nbk * per, hb, hb)


def kernel(x, meta_tokens, ffn1_pre_g, ffn1_w_gate, ffn1_w_up, ffn1_w_down, ffn1_post_g, mix_pre_g, w_in, lru_conv_w, lru_conv_b, lru_w_a, lru_b_a, lru_w_x, lru_b_x, lru_lambda, sconv_w, lru_out_g, sconv_out_g, w_out, mix_post_g, ffn2_pre_g, ffn2_w_gate, ffn2_w_up, ffn2_w_down, ffn2_post_g, loss_target, m_meta_tokens, m_ffn1_pre_g, m_ffn1_w_gate, m_ffn1_w_up, m_ffn1_w_down, m_ffn1_post_g, m_mix_pre_g, m_w_in, m_lru_conv_w, m_lru_conv_b, m_lru_w_a, m_lru_b_a, m_lru_w_x, m_lru_b_x, m_lru_lambda, m_sconv_w, m_lru_out_g, m_sconv_out_g, m_w_out, m_mix_post_g, m_ffn2_pre_g, m_ffn2_w_gate, m_ffn2_w_up, m_ffn2_w_down, m_ffn2_post_g, v_meta_tokens, v_ffn1_pre_g, v_ffn1_w_gate, v_ffn1_w_up, v_ffn1_w_down, v_ffn1_post_g, v_mix_pre_g, v_w_in, v_lru_conv_w, v_lru_conv_b, v_lru_w_a, v_lru_b_a, v_lru_w_x, v_lru_b_x, v_lru_lambda, v_sconv_w, v_lru_out_g, v_sconv_out_g, v_w_out, v_mix_post_g, v_ffn2_pre_g, v_ffn2_w_gate, v_ffn2_w_up, v_ffn2_w_down, v_ffn2_post_g):
    given = dict(locals())
    wts = {n: given[n] for n in WEIGHT_NAMES}
    mom = {n: given["m_" + n] for n in WEIGHT_NAMES}
    var = {n: given["v_" + n] for n in WEIGHT_NAMES}

    xi, yi, ci = _position()
    me = _block_of(xi, yi, ci)
    x2 = x[0]
    seq, d = x2.shape
    n_meta = meta_tokens.shape[0]
    m_rows = _round_up(n_meta + seq, ROW_ALIGN)
    pad = m_rows - n_meta - seq
    lead = pad + n_meta
    c = lru_conv_b.shape[1]
    hb = lru_w_a.shape[-1]
    dm = meta_tokens.shape[1]
    cs_ = lru_conv_w.shape[2]
    kw4, kw3 = lru_conv_w.shape[1], sconv_w.shape[1]
    assert d == 2 * c and c % BD == 0 and BD % hb == 0 and cs_ <= dm and kw4 == 4 and kw3 == 3

    small = jnp.zeros((_round_up(n_meta + kw4 + kw3, 8), dm), F32)
    small = small.at[0:n_meta].set(meta_tokens)
    small = small.at[n_meta:n_meta + kw4, 0:cs_].set(lru_conv_w[0])
    small = small.at[n_meta + kw4:n_meta + kw4 + kw3, 0:cs_].set(sconv_w[0])
    sr = small.shape[0]
    small_all = _gather_small(small, False, "gather_small").reshape(N_DEV, sr, dm)
    meta_full = small_all[:, 0:n_meta, :].transpose(1, 0, 2).reshape(n_meta, d)
    conv_w_full = small_all[:, n_meta:n_meta + kw4, 0:cs_].transpose(1, 0, 2).reshape(kw4, c)
    sconv_w_full = small_all[:, n_meta + kw4:n_meta + kw4 + kw3, 0:cs_].transpose(1, 0, 2).reshape(kw3, c)

    big = ['ffn1_w_gate', 'ffn1_w_up', 'ffn1_w_down', 'w_in', 'w_out', 'ffn2_w_gate', 'ffn2_w_up', 'ffn2_w_down']
    col_sharded = {'ffn1_w_gate', 'ffn1_w_up', 'w_in', 'ffn2_w_gate', 'ffn2_w_up'}
    shards = []
    for nme in big:
        w = wts[nme][0].astype(WIRE_DTYPE)
        shards.append(w.T if nme in col_sharded else w)
    shard_rows = dict(zip(big, [s.shape[0] for s in shards]))
    zeros = jnp.zeros((F_ALIGN, d), WIRE_DTYPE)

    def gather(forward_at, *names):
        sel = [shards[big.index(nme)] for nme in names]
        padded = [_round_up(N_DEV * shard_rows[nme], LANE if nme in ('w_in', 'w_out') else F_ALIGN) for nme in names]
        return _CarriedGather(sel, padded, zeros, forward_at)

    pv = jnp.zeros((16, c), F32)
    pv = pv.at[0:4].set(conv_w_full).at[4].set(lru_conv_b[0]).at[5].set(lru_b_a[0]).at[6].set(lru_b_x[0])
    pv = pv.at[7].set(lru_lambda[0]).at[8:11].set(sconv_w_full).at[11].set(lru_out_g[0]).at[12].set(sconv_out_g[0])
    wa_bd = _block_diag(lru_w_a[0]).astype(MXU_DTYPE)
    wx_bd = _block_diag(lru_w_x[0]).astype(MXU_DTYPE)
    gs = c // N_GROUPS
    gidx = jnp.arange(BD) // gs
    gm = jnp.where(gidx[:, None] == gidx[None, :], 1.0 / gs, 0.0).astype(MXU_DTYPE)

    ride = gather(1.0, 'ffn1_w_gate')
    h0, n1, target = _embed(x2, meta_full, loss_target[0], ffn1_pre_g, pad, "embed_prenorm", carried=[ride])
    (wg1,) = ride.results
    ride = gather(1.0, 'ffn1_w_up')
    g1 = _mm_nt(n1, wg1, "ffn1_gate", carried=[ride], out_dtype=MXU_DTYPE)
    (wu1,) = ride.results
    ride = gather(1.0, 'ffn1_w_down')
    u1, a1 = _ffn_up_act(n1, wu1, g1, "ffn1_up_act", carried=[ride])
    (wd1,) = ride.results
    ride = gather(0.9, 'w_in', 'w_out')
    fo1, h1, un = _mm_residual_norm(a1, wd1, h0, ffn1_post_g, 0.5, mix_pre_g, "ffn1_down", carried=[ride])
    win_t, wout = ride.results
    ride = gather(1.0, 'ffn2_w_gate')
    z = _mm_nt(un, win_t, "mix_in_proj", carried=[ride])
    (wg2,) = ride.results
    ride = gather(1.0, 'ffn2_w_up')
    mixed, hs = _mixer_fwd(z, pv, wa_bd, wx_bd, gm, pad, "mixer_fwd", carried=[ride])
    (wu2,) = ride.results
    o_mix, h2, n2 = _mm_residual_norm(mixed, wout, h1, mix_post_g, 1.0, ffn2_pre_g, "mix_out_proj")
    ride = gather(0.75, 'ffn2_w_down')
    g2, u2, a2 = _ffn_gate_up(n2, wg2, wu2, "ffn2_gate_up", carried=[ride])
    (wd2,) = ride.results
    dh3, dfo2, d_post2, loss_part = _mm_residual_loss(a2, wd2, h2, ffn2_post_g, 0.5, target, lead, "ffn2_down_loss")
    loss = lax.psum(loss_part[0, 0], ("x", "y", "c"))

    chip_rel = [2 * (xi ^ (r >> 1)) + (yi ^ (r & 1)) for r in range(4)]
    where = jnp.stack([2 * k + ci for k in chip_rel] + chip_rel).astype(jnp.int32)
    red = {}

    def reduction(nme, grad):
        red[nme] = _GradReduction(nme, grad, shard_rows[nme], where)
        return red[nme]

    r_wd2 = reduction('ffn2_w_down', _mm_tn(a2, dfo2, "ffn2_dw_down"))
    dg2, du2 = _ffn_hidden_bwd(dfo2, wd2, g2, u2, "ffn2_hidden_bwd", carried=[r_wd2.swap()])
    r_wg2 = reduction('ffn2_w_gate', _mm_tn(dg2, n2, "ffn2_dw_gate", carried=[r_wd2.exchange()]))
    r_wu2 = reduction('ffn2_w_up', _mm_tn(du2, n2, "ffn2_dw_up", carried=[r_wg2.swap()]))
    dh2, d_pre2 = _mm_norm_bwd([(dg2, wg2), (du2, wu2)], h2, ffn2_pre_g, dh3, "ffn2_dx",
                               carried=[r_wg2.exchange(), r_wu2.swap()])
    do_mix, d_mix_post = _norm_bwd(o_mix, mix_post_g, dh2, 1.0, "mix_postnorm_bwd")
    dmixed = _mm_nt(do_mix, wout, "mix_out_proj_bwd")
    r_wout = reduction('w_out', _mm_tn(mixed, do_mix, "mix_dw_out"))
    dz, dpv, dwa_bd, dwx_bd = _mixer_bwd(z, hs, dmixed, pv, wa_bd, wx_bd, gm, pad, "mixer_bwd",
                                         carried=[r_wu2.exchange(), r_wout.swap()])
    r_win = reduction('w_in', _mm_tn(dz, un, "mix_dw_in", carried=[r_wout.exchange()]))
    dh1, d_mix_pre = _mm_norm_bwd([(dz, win_t)], h1, mix_pre_g, dh2, "mix_dx", carried=[r_win.swap()])
    dfo1, d_post1 = _norm_bwd(fo1, ffn1_post_g, dh1, 0.5, "ffn1_postnorm_bwd")
    r_wd1 = reduction('ffn1_w_down', _mm_tn(a1, dfo1, "ffn1_dw_down", carried=[r_win.exchange()]))
    dg1, du1 = _ffn_hidden_bwd(dfo1, wd1, g1, u1, "ffn1_hidden_bwd", carried=[r_wd1.swap()])
    r_wg1 = reduction('ffn1_w_gate', _mm_tn(dg1, n1, "ffn1_dw_gate", carried=[r_wd1.exchange()]))
    r_wu1 = reduction('ffn1_w_up', _mm_tn(du1, n1, "ffn1_dw_up", carried=[r_wg1.swap()]))
    row_tile = _norm_bwd_row_tile(m_rows)
    n_tiles = m_rows // row_tile
    half = n_tiles // 2
    assert half >= 1 and half * row_tile >= lead
    dh0_a, d_pre1_a = _mm_norm_bwd([(dg1, wg1), (du1, wu1)], h0, ffn1_pre_g, dh1, "ffn1_dx_a",
                                   carried=[r_wg1.exchange(), r_wu1.swap()], row_tiles=(0, half))
    dh0_b, d_pre1 = _mm_norm_bwd([(dg1, wg1), (du1, wu1)], h0, ffn1_pre_g, dh1, "ffn1_dx_b",
                                 carried=[r_wu1.exchange()], row_tiles=(half, n_tiles - half), dg_init=d_pre1_a)
    grad_x = jnp.concatenate([dh0_a[lead:], dh0_b], axis=0)[None]
    d_meta = dh0_a[pad:lead]

    grads, delta, new_m, new_v = {}, {}, {}, {}
    for nme in big:
        in_shard_layout = nme not in col_sharded or shard_rows[nme] % LANE != 0
        if in_shard_layout:
            view = (lambda t: t[0].T) if nme in col_sharded else (lambda t: t[0])
            back = (lambda t: t.T[None]) if nme in col_sharded else (lambda t: t[None])
            outs = red[nme].total_and_update(view(wts[nme]), view(mom[nme]), view(var[nme]))
            grads[nme], delta[nme], new_m[nme], new_v[nme] = [back(t) for t in outs]
        else:
            grads[nme] = red[nme].total().T[None]
            outs = _adamw(wts[nme][0], grads[nme][0], mom[nme][0], var[nme][0], "adamw_" + nme)
            delta[nme], new_m[nme], new_v[nme] = [t[None] for t in outs]

    small_names = ['ffn1_pre_g', 'ffn1_post_g', 'mix_pre_g', 'mix_post_g', 'ffn2_pre_g', 'ffn2_post_g',
                   'lru_conv_b', 'lru_b_a', 'lru_b_x', 'lru_lambda', 'lru_out_g', 'sconv_out_g',
                   'lru_conv_w', 'sconv_w', 'lru_w_a', 'lru_w_x', 'meta_tokens']
    small_parts = [d_pre1, d_post1, d_mix_pre, d_mix_post, d_pre2, d_post2,
                   dpv[4:5], dpv[5:6], dpv[6:7], dpv[7:8], dpv[11:12], dpv[12:13],
                   dpv[0:4], dpv[8:11], _block_diag_extract(dwa_bd, hb), _block_diag_extract(dwx_bd, hb),
                   d_meta]
    small_shapes = [p.shape for p in small_parts]
    small_sum = _gather_small(_pack_rows(small_parts, d), True, "reduce_small")
    for nme, gsm in zip(small_names, _unpack_rows(small_sum, small_shapes)):
        if nme == 'meta_tokens':
            grads[nme] = lax.dynamic_slice_in_dim(gsm, me * dm, dm, axis=1)
        elif nme in ('lru_conv_w', 'sconv_w'):
            grads[nme] = lax.dynamic_slice_in_dim(gsm, me * cs_, cs_, axis=1)[None]
        else:
            grads[nme] = gsm.reshape(wts[nme].shape)

    rest = [n for n in WEIGHT_NAMES if n not in big]
    rest_shapes = [wts[n].shape for n in rest]
    packed = [_pack_rows([src[n] for n in rest], LANE, 256) for src in (wts, grads, mom, var)]
    for out, packed_out in zip((delta, new_m, new_v), _adamw(*packed, "adamw_small")):
        for nme, arr in zip(rest, _unpack_rows(packed_out, rest_shapes)):
            out[nme] = arr

    return (loss, grad_x, *[grads[n] for n in WEIGHT_NAMES], *[delta[n] for n in WEIGHT_NAMES],
            *[new_m[n] for n in WEIGHT_NAMES], *[new_v[n] for n in WEIGHT_NAMES])
```

```python
import functools

import jax
import jax.numpy as jnp
from jax import lax
from jax.experimental import pallas as pl
from jax.experimental.pallas import tpu as pltpu

F32 = jnp.float32
MXU_DTYPE = jnp.bfloat16
WIRE_DTYPE = jnp.bfloat16
MESH = pl.DeviceIdType.MESH

EPS = 1e-6
LRU_C = 8.0
N_GROUPS = 16
ADAM_LR = 0.001
ADAM_B1 = 0.9
ADAM_B2 = 0.999
ADAM_EPS = 1e-08
ADAM_WD = 0.01
ADAM_STEP = 10

N_DEV = 8
LANE = 128
SUBLANE_BF16 = 16
ROW_ALIGN = 128
F_ALIGN = 512
BD = 256
K_TILE = 512
ACC_ROWS = 528
ACC_GROUP = 1
MIX_ROWS = 128
VMEM_LIMIT_MB = 56

WEIGHT_NAMES = ['meta_tokens', 'ffn1_pre_g', 'ffn1_w_gate', 'ffn1_w_up', 'ffn1_w_down', 'ffn1_post_g',
                'mix_pre_g', 'w_in', 'lru_conv_w', 'lru_conv_b', 'lru_w_a', 'lru_b_a', 'lru_w_x', 'lru_b_x',
                'lru_lambda', 'sconv_w', 'lru_out_g', 'sconv_out_g', 'w_out', 'mix_post_g', 'ffn2_pre_g',
                'ffn2_w_gate', 'ffn2_w_up', 'ffn2_w_down', 'ffn2_post_g']


def _round_up(n, q):
    return (n + q - 1) // q * q


def _tile(n, target, q):
    best = None
    t = q
    while t <= min(n, target):
        if n % t == 0:
            best = t
        t += q
    assert best is not None, (n, target, q)
    return best


def _params(**kw):
    return pltpu.CompilerParams(vmem_limit_bytes=VMEM_LIMIT_MB << 20, **kw)


def _call(body, *, grid, in_specs, out_specs, out_shape, name, args, scratch_shapes=(), carried=(), prefetch=()):
    carried = list(carried)
    n_pf = len(prefetch)

    def launch(fn, in_specs_, out_specs_, out_shape_, scratch_, operands, aliases_):
        if n_pf:
            spec = pltpu.PrefetchScalarGridSpec(num_scalar_prefetch=n_pf, grid=grid, in_specs=in_specs_,
                                                out_specs=out_specs_, scratch_shapes=scratch_)
            return pl.pallas_call(fn, grid_spec=spec, out_shape=out_shape_, input_output_aliases=aliases_,
                                  name=name, compiler_params=_params())(*prefetch, *operands)
        return pl.pallas_call(fn, grid=grid, in_specs=in_specs_, out_specs=out_specs_, out_shape=out_shape_,
                              scratch_shapes=scratch_, input_output_aliases=aliases_, name=name,
                              compiler_params=_params())(*operands)

    if not carried:
        return launch(body, in_specs, out_specs, out_shape, list(scratch_shapes), args, {})
    single = not isinstance(out_shape, (list, tuple))
    out_specs_l = [out_specs] if single else list(out_specs)
    out_shape_l = [out_shape] if single else list(out_shape)
    n_in, n_out, n_scr = len(in_specs), len(out_specs_l), len(scratch_shapes)
    hbm = pl.BlockSpec(memory_space=pl.ANY)
    c_in = [a for cm in carried for a in cm.arrays]
    c_out = [s for cm in carried for s in cm.out_shapes]
    c_scr = []
    aliases = {}
    in_off, out_off = n_pf + n_in, n_out
    for cm in carried:
        c_scr += [pltpu.SemaphoreType.DMA((cm.n_remote,)), pltpu.SemaphoreType.DMA((cm.n_remote,)),
                  pltpu.SemaphoreType.DMA((max(cm.n_local, 1),))]
        for k, v in cm.aliases.items():
            aliases[in_off + k] = out_off + v
        in_off += len(cm.arrays)
        out_off += len(cm.out_shapes)
    steps = 1
    for g in grid:
        steps *= g
    forward_steps = [min(int(cm.forward_at * steps), steps - 1) for cm in carried]

    def wrapped(*refs):
        pf = refs[:n_pf]
        p = n_pf
        ins = refs[p:p + n_in]
        p += n_in
        cins = refs[p:p + len(c_in)]
        p += len(c_in)
        outs = refs[p:p + n_out]
        p += n_out
        couts = refs[p:p + len(c_out)]
        p += len(c_out)
        scr = refs[p:p + n_scr]
        csem = refs[p + n_scr:]
        lin = 0
        for axis, g in enumerate(grid):
            lin = lin * g + pl.program_id(axis)
        views = []
        io = oo = 0
        for j, cm in enumerate(carried):
            views.append((cins[io:io + len(cm.arrays)], couts[oo:oo + len(cm.out_shapes)],
                          csem[3 * j], csem[3 * j + 1], csem[3 * j + 2]))
            io += len(cm.arrays)
            oo += len(cm.out_shapes)

        @pl.when(lin == 0)
        def _():
            for cm, v in zip(carried, views):
                cm.start(*v)

        body(*pf, *ins, *outs, *scr)

        for cm, v, step in zip(carried, views, forward_steps):
            pl.when(lin == step)(functools.partial(cm.forward, *v))

        @pl.when(lin == steps - 1)
        def _():
            for cm, v in zip(carried, views):
                cm.finish(*v)

    res = launch(wrapped, list(in_specs) + [hbm] * len(c_in), out_specs_l + [hbm] * len(c_out),
                 out_shape_l + c_out, list(scratch_shapes) + c_scr, (*args, *c_in), aliases)
    oo = n_out
    for cm in carried:
        cm.results = list(res[oo:oo + len(cm.out_shapes)])
        oo += len(cm.out_shapes)
    return res[0] if single else list(res[:n_out])


def _embed(x, meta, target, g, pad, name, carried=()):
    seq, d = x.shape
    n_meta = meta.shape[0]
    lead = pad + n_meta
    m = lead + seq
    tr = ROW_ALIGN
    lead_blocks = lead // tr
    meta_row = pad - (lead_blocks - 1) * tr
    assert lead % tr == 0 and seq % tr == 0 and 0 <= meta_row and meta_row % 8 == 0

    def body(x_ref, meta_ref, t_ref, g_ref, h_ref, n_ref, tp_ref):
        i = pl.program_id(0)

        @pl.when(i < lead_blocks)
        def _():
            h_ref[...] = jnp.zeros_like(h_ref)
            tp_ref[...] = jnp.zeros_like(tp_ref)

        @pl.when(i == lead_blocks - 1)
        def _():
            h_ref[pl.ds(meta_row, n_meta), :] = meta_ref[...]

        @pl.when(i >= lead_blocks)
        def _():
            h_ref[...] = x_ref[...]
            tp_ref[...] = t_ref[...]

        h = h_ref[...]
        r = lax.rsqrt(jnp.mean(h * h, axis=-1, keepdims=True) + EPS)
        n_ref[...] = (h * r * g_ref[...]).astype(n_ref.dtype)

    tokens = pl.BlockSpec((tr, d), lambda i: (jnp.maximum(i - lead_blocks, 0), 0))
    rows = pl.BlockSpec((tr, d), lambda i: (i, 0))
    return _call(
        body, grid=(m // tr,),
        in_specs=[tokens, pl.BlockSpec((n_meta, d), lambda i: (0, 0)), tokens, pl.BlockSpec((1, d), lambda i: (0, 0))],
        out_specs=[rows, rows, rows],
        out_shape=[jax.ShapeDtypeStruct((m, d), F32), jax.ShapeDtypeStruct((m, d), MXU_DTYPE),
                   jax.ShapeDtypeStruct((m, d), F32)],
        name=name, args=(x, meta, target, g), carried=carried)


def _rmsnorm_bwd_rows(x, g, dy):
    r = lax.rsqrt(jnp.mean(x * x, axis=-1, keepdims=True) + EPS)
    xh = x * r
    dyh = dy * g
    dx = r * (dyh - xh * jnp.mean(dyh * xh, axis=-1, keepdims=True))
    return dx, dy * xh


def _norm_bwd(x, g, dy, scale, name, carried=()):
    m, d = x.shape
    tm = _tile(m, 528, SUBLANE_BF16)

    def body(x_ref, g_ref, dy_ref, dx_ref, dg_ref):
        @pl.when(pl.program_id(0) == 0)
        def _():
            dg_ref[...] = jnp.zeros_like(dg_ref)

        dx, dgr = _rmsnorm_bwd_rows(x_ref[...], g_ref[...], scale * dy_ref[...])
        dx_ref[...] = dx.astype(dx_ref.dtype)
        dg_ref[...] += jnp.sum(dgr, axis=0, keepdims=True)

    return _call(
        body, grid=(m // tm,),
        in_specs=[pl.BlockSpec((tm, d), lambda i: (i, 0)), pl.BlockSpec((1, d), lambda i: (0, 0)),
                  pl.BlockSpec((tm, d), lambda i: (i, 0))],
        out_specs=[pl.BlockSpec((tm, d), lambda i: (i, 0)), pl.BlockSpec((1, d), lambda i: (0, 0))],
        out_shape=[jax.ShapeDtypeStruct((m, d), MXU_DTYPE), jax.ShapeDtypeStruct((1, d), F32)],
        name=name, args=(x, g, dy), carried=carried)


def _dot_nt(a, b):
    return lax.dot_general(a, b, (((1,), (1,)), ((), ())), preferred_element_type=F32)


def _dot_tn(a, b):
    return lax.dot_general(a, b, (((0,), (0,)), ((), ())), preferred_element_type=F32)


def _mm_nt(a, w, name, carried=(), out_dtype=F32):
    m, k = a.shape
    n = w.shape[0]
    tm = _tile(m, 1056, SUBLANE_BF16)
    tn = _tile(n, 512, LANE)

    def body(a_ref, w_ref, o_ref):
        o_ref[...] = _dot_nt(a_ref[...], w_ref[...]).astype(o_ref.dtype)

    return _call(
        body, grid=(m // tm, n // tn),
        in_specs=[pl.BlockSpec((tm, k), lambda i, j: (i, 0)), pl.BlockSpec((tn, k), lambda i, j: (j, 0))],
        out_specs=pl.BlockSpec((tm, tn), lambda i, j: (i, j)),
        out_shape=jax.ShapeDtypeStruct((m, n), out_dtype), name=name, args=(a, w), carried=carried)


def _ffn_up_act(n_act, wu_t, g_act, name, carried=()):
    m, d = n_act.shape
    fp = wu_t.shape[0]
    tm = _tile(m, 1056, SUBLANE_BF16)
    tn = _tile(fp, 512, LANE)

    def body(n_ref, wu_ref, g_ref, u_ref, a_ref):
        u = _dot_nt(n_ref[...], wu_ref[...])
        g = g_ref[...].astype(F32)
        u_ref[...] = u.astype(u_ref.dtype)
        a_ref[...] = (g * jax.nn.sigmoid(g) * u).astype(a_ref.dtype)

    act = pl.BlockSpec((tm, tn), lambda i, j: (i, j))
    return _call(
        body, grid=(m // tm, fp // tn),
        in_specs=[pl.BlockSpec((tm, d), lambda i, j: (i, 0)), pl.BlockSpec((tn, d), lambda i, j: (j, 0)), act],
        out_specs=[act, act],
        out_shape=[jax.ShapeDtypeStruct((m, fp), MXU_DTYPE)] * 2, name=name, args=(n_act, wu_t, g_act), carried=carried)


def _ffn_gate_up(n_act, wg_t, wu_t, name, carried=()):
    m, d = n_act.shape
    fp = wg_t.shape[0]
    tm = _tile(m, 1056, SUBLANE_BF16)
    tn = _tile(fp, 512, LANE)

    def body(n_ref, wg_ref, wu_ref, g_ref, u_ref, a_ref):
        n = n_ref[...]
        g = _dot_nt(n, wg_ref[...])
        u = _dot_nt(n, wu_ref[...])
        g_ref[...] = g.astype(g_ref.dtype)
        u_ref[...] = u.astype(u_ref.dtype)
        a_ref[...] = (g * jax.nn.sigmoid(g) * u).astype(a_ref.dtype)

    act = pl.BlockSpec((tm, tn), lambda i, j: (i, j))
    wsp = pl.BlockSpec((tn, d), lambda i, j: (j, 0))
    return _call(
        body, grid=(m // tm, fp // tn),
        in_specs=[pl.BlockSpec((tm, d), lambda i, j: (i, 0)), wsp, wsp],
        out_specs=[act, act, act],
        out_shape=[jax.ShapeDtypeStruct((m, fp), MXU_DTYPE)] * 3, name=name, args=(n_act, wg_t, wu_t), carried=carried)


def _ffn_hidden_bwd(dfo, wd, g_act, u_act, name, carried=()):
    m, d = dfo.shape
    fp = wd.shape[0]
    tm = _tile(m, 1056, SUBLANE_BF16)
    tn = _tile(fp, 512, LANE)

    def body(df_ref, wd_ref, g_ref, u_ref, dg_ref, du_ref):
        da = _dot_nt(df_ref[...], wd_ref[...])
        g = g_ref[...].astype(F32)
        u = u_ref[...].astype(F32)
        s = jax.nn.sigmoid(g)
        du_ref[...] = (da * (g * s)).astype(du_ref.dtype)
        dg_ref[...] = (da * u * (s * (1.0 + g * (1.0 - s)))).astype(dg_ref.dtype)

    act = pl.BlockSpec((tm, tn), lambda i, j: (i, j))
    return _call(
        body, grid=(m // tm, fp // tn),
        in_specs=[pl.BlockSpec((tm, d), lambda i, j: (i, 0)), pl.BlockSpec((tn, d), lambda i, j: (j, 0)), act, act],
        out_specs=[act, act],
        out_shape=[jax.ShapeDtypeStruct((m, fp), MXU_DTYPE)] * 2, name=name, args=(dfo, wd, g_act, u_act),
        carried=carried)


def _row_groups(n_tiles, max_group, nk):
    gsz = max(q for q in range(1, max_group + 1) if n_tiles % q == 0)

    def epilogue_row(grp, kk, i):
        return grp * gsz + jnp.where(kk == nk - 1, i, 0)

    return gsz, epilogue_row


def _mm_residual_norm(a, w, h, g, scale, next_g, name, carried=()):
    m, k = a.shape
    d = w.shape[1]
    tm = _tile(m, ACC_ROWS, SUBLANE_BF16)
    tk = _tile(k, K_TILE, LANE)
    nk = k // tk
    gsz, epilogue_row = _row_groups(m // tm, ACC_GROUP, nk)

    def body(a_ref, w_ref, h_ref, g_ref, ng_ref, fo_ref, hn_ref, nn_ref, acc_ref):
        kk, i = pl.program_id(1), pl.program_id(2)

        @pl.when(kk == 0)
        def _():
            acc_ref[i] = jnp.zeros((tm, d), F32)

        acc_ref[i] += jnp.dot(a_ref[...], w_ref[...], preferred_element_type=F32)

        @pl.when(kk == nk - 1)
        def _():
            fo = acc_ref[i]
            fo_ref[...] = fo
            r = lax.rsqrt(jnp.mean(fo * fo, axis=-1, keepdims=True) + EPS)
            hn = h_ref[...] + scale * (fo * r * g_ref[...])
            hn_ref[...] = hn
            rn = lax.rsqrt(jnp.mean(hn * hn, axis=-1, keepdims=True) + EPS)
            nn_ref[...] = (hn * rn * ng_ref[...]).astype(nn_ref.dtype)

    row = pl.BlockSpec((tm, d), lambda grp, kk, i: (epilogue_row(grp, kk, i), 0))
    row_once = pl.BlockSpec((tm, d), lambda grp, kk, i: (epilogue_row(grp, kk, i), 0), pipeline_mode=pl.Buffered(1))
    vec = pl.BlockSpec((1, d), lambda grp, kk, i: (0, 0))
    return _call(
        body, grid=(m // tm // gsz, nk, gsz),
        in_specs=[pl.BlockSpec((tm, tk), lambda grp, kk, i: (grp * gsz + i, kk)),
                  pl.BlockSpec((tk, d), lambda grp, kk, i: (kk, 0)), row_once, vec, vec],
        out_specs=[row, row, row],
        out_shape=[jax.ShapeDtypeStruct((m, d), F32)] * 2 + [jax.ShapeDtypeStruct((m, d), MXU_DTYPE)],
        scratch_shapes=[pltpu.VMEM((gsz, tm, d), F32)], name=name, args=(a, w, h, g, next_g), carried=carried)


def _mm_residual_loss(a, w, h, g, scale, target, lead, name, carried=()):
    m, k = a.shape
    d = w.shape[1]
    tm = _tile(m, ACC_ROWS, SUBLANE_BF16)
    tk = _tile(k, K_TILE, LANE)
    nk = k // tk
    gsz, epilogue_row = _row_groups(m // tm, ACC_GROUP, nk)

    def body(a_ref, w_ref, h_ref, g_ref, t_ref, dy_ref, dfo_ref, dg_ref, l_ref, acc_ref):
        grp, kk, i = pl.program_id(0), pl.program_id(1), pl.program_id(2)

        @pl.when(jnp.logical_and(jnp.logical_and(grp == 0, kk == 0), i == 0))
        def _():
            dg_ref[...] = jnp.zeros_like(dg_ref)
            l_ref[...] = jnp.zeros_like(l_ref)

        @pl.when(kk == 0)
        def _():
            acc_ref[i] = jnp.zeros((tm, d), F32)

        acc_ref[i] += jnp.dot(a_ref[...], w_ref[...], preferred_element_type=F32)

        @pl.when(kk == nk - 1)
        def _():
            fo = acc_ref[i]
            gain = g_ref[...]
            r = lax.rsqrt(jnp.mean(fo * fo, axis=-1, keepdims=True) + EPS)
            xh = fo * r
            y = h_ref[...] + scale * (xh * gain)
            row = (grp * gsz + i) * tm + lax.broadcasted_iota(jnp.int32, (tm, 1), 0)
            e = jnp.where(row >= lead, y - t_ref[...], 0.0)
            dy = e * (1.0 / d)
            dy_ref[...] = dy
            l_ref[...] += 0.5 * jnp.sum(jnp.sum(e * e, axis=-1, keepdims=True) * (1.0 / d), axis=0, keepdims=True)
            dn = scale * dy
            dyh = dn * gain
            dfo_ref[...] = (r * (dyh - xh * jnp.mean(dyh * xh, axis=-1, keepdims=True))).astype(dfo_ref.dtype)
            dg_ref[...] += jnp.sum(dn * xh, axis=0, keepdims=True)

    row = pl.BlockSpec((tm, d), lambda grp, kk, i: (epilogue_row(grp, kk, i), 0))
    row_once = pl.BlockSpec((tm, d), lambda grp, kk, i: (epilogue_row(grp, kk, i), 0), pipeline_mode=pl.Buffered(1))
    vec = pl.BlockSpec((1, d), lambda grp, kk, i: (0, 0))
    return _call(
        body, grid=(m // tm // gsz, nk, gsz),
        in_specs=[pl.BlockSpec((tm, tk), lambda grp, kk, i: (grp * gsz + i, kk)),
                  pl.BlockSpec((tk, d), lambda grp, kk, i: (kk, 0)), row_once, vec, row_once],
        out_specs=[row, row, vec, pl.BlockSpec((1, 1), lambda grp, kk, i: (0, 0))],
        out_shape=[jax.ShapeDtypeStruct((m, d), F32), jax.ShapeDtypeStruct((m, d), MXU_DTYPE),
                   jax.ShapeDtypeStruct((1, d), F32), jax.ShapeDtypeStruct((1, 1), F32)],
        scratch_shapes=[pltpu.VMEM((gsz, tm, d), F32)], name=name, args=(a, w, h, g, target), carried=carried)


def _norm_bwd_row_tile(m):
    return _tile(m, ACC_ROWS, SUBLANE_BF16)


def _mm_norm_bwd(pairs, h, g, dh_up, name, carried=(), row_tiles=None, dg_init=None, post=None):
    n_pairs = len(pairs)
    m, k = pairs[0][0].shape
    d = h.shape[1]
    tm = _norm_bwd_row_tile(m)
    tk = _tile(k, K_TILE, LANE)
    nk = k // tk
    t0, nt = row_tiles if row_tiles is not None else (0, m // tm)
    gsz, epilogue_row = _row_groups(nt, ACC_GROUP, nk)
    if dg_init is None:
        dg_init = jnp.zeros((1, d), F32)

    n_post = 0 if post is None else 2

    def body(*refs):
        ops = refs[:2 * n_pairs]
        h_ref, g_ref, up_ref, init_ref = refs[2 * n_pairs:2 * n_pairs + 4]
        post_in = refs[2 * n_pairs + 4:2 * n_pairs + 4 + n_post]
        dh_ref, dg_ref = refs[2 * n_pairs + 4 + n_post:2 * n_pairs + 6 + n_post]
        post_out = refs[2 * n_pairs + 6 + n_post:2 * n_pairs + 6 + 2 * n_post]
        acc_ref = refs[-1]
        grp, kk, i = pl.program_id(0), pl.program_id(1), pl.program_id(2)

        @pl.when(jnp.logical_and(jnp.logical_and(grp == 0, kk == 0), i == 0))
        def _():
            dg_ref[...] = init_ref[...]
            if post is not None:
                post_out[1][...] = jnp.zeros_like(post_out[1])

        @pl.when(kk == 0)
        def _():
            acc_ref[i] = jnp.zeros((tm, d), F32)

        for p in range(n_pairs):
            acc_ref[i] += jnp.dot(ops[2 * p][...], ops[2 * p + 1][...], preferred_element_type=F32)

        @pl.when(kk == nk - 1)
        def _():
            dx, dgr = _rmsnorm_bwd_rows(h_ref[...], g_ref[...], acc_ref[i])
            dh = up_ref[...] + dx
            dh_ref[...] = dh
            dg_ref[...] += jnp.sum(dgr, axis=0, keepdims=True)
            if post is not None:
                dfo, dpr = _rmsnorm_bwd_rows(post_in[0][...], post_in[1][...], post[2] * dh)
                post_out[0][...] = dfo.astype(post_out[0].dtype)
                post_out[1][...] += jnp.sum(dpr, axis=0, keepdims=True)

    row_in = pl.BlockSpec((tm, d), lambda grp, kk, i: (t0 + epilogue_row(grp, kk, i), 0))
    row_out = pl.BlockSpec((tm, d), lambda grp, kk, i: (epilogue_row(grp, kk, i), 0))
    vec = pl.BlockSpec((1, d), lambda grp, kk, i: (0, 0))
    in_specs = []
    args = []
    for a, w in pairs:
        in_specs += [pl.BlockSpec((tm, tk), lambda grp, kk, i: (t0 + grp * gsz + i, kk)),
                     pl.BlockSpec((tk, d), lambda grp, kk, i: (kk, 0))]
        args += [a, w]
    in_specs += [row_in, vec, row_in, vec]
    args += [h, g, dh_up, dg_init]
    out_specs = [row_out, vec]
    out_shape = [jax.ShapeDtypeStruct((nt * tm, d), F32), jax.ShapeDtypeStruct((1, d), F32)]
    if post is not None:
        in_specs += [row_in, vec]
        args += [post[0], post[1]]
        out_specs += [row_out, vec]
        out_shape += [jax.ShapeDtypeStruct((nt * tm, d), MXU_DTYPE), jax.ShapeDtypeStruct((1, d), F32)]
    return _call(
        body, grid=(nt // gsz, nk, gsz), in_specs=in_specs, out_specs=out_specs, out_shape=out_shape,
        scratch_shapes=[pltpu.VMEM((gsz, tm, d), F32)], name=name, args=tuple(args), carried=carried)


def _mm_tn(a, b, name, carried=()):
    m, ka = a.shape
    d = b.shape[1]
    tf = _tile(ka, 512, LANE)

    def body(a_ref, b_ref, o_ref):
        o_ref[...] = _dot_tn(a_ref[...], b_ref[...]).astype(o_ref.dtype)

    return _call(
        body, grid=(ka // tf,),
        in_specs=[pl.BlockSpec((m, tf), lambda j: (0, j)),
                  pl.BlockSpec((m, d), lambda j: (0, 0), pipeline_mode=pl.Buffered(1))],
        out_specs=pl.BlockSpec((tf, d), lambda j: (j, 0)),
        out_shape=jax.ShapeDtypeStruct((ka, d), WIRE_DTYPE), name=name, args=(a, b), carried=carried)


GELU_K = 0.7978845608028654
GELU_C = 0.044715


def _expm1(x):
    series = x * (1.0 + x * (1.0 / 2 + x * (1.0 / 6 + x * (1.0 / 24 + x * (1.0 / 120)))))
    return jnp.where(jnp.abs(x) < 0.1, series, jnp.exp(x) - 1.0)


def _softplus(x):
    return jnp.maximum(x, 0.0) + jnp.log1p(jnp.exp(-jnp.abs(x)))


def _block_mm(v, w_ref, transposed):
    nbk = w_ref.shape[0]
    outs = []
    for j in range(nbk):
        vj = v[:, j * BD:(j + 1) * BD]
        outs.append(_dot_nt(vj, w_ref[j]) if transposed else jnp.dot(vj, w_ref[j], preferred_element_type=F32))
    return outs[0] if nbk == 1 else jnp.concatenate(outs, axis=1)


def _group_mean(q, gm_ref):
    hi = q.astype(MXU_DTYPE)
    lo = (q - hi.astype(F32)).astype(MXU_DTYPE)
    nbk = q.shape[1] // BD
    gm = gm_ref[...]
    outs = []
    for j in range(nbk):
        sl = slice(j * BD, (j + 1) * BD)
        outs.append(jnp.dot(hi[:, sl], gm, preferred_element_type=F32) + jnp.dot(lo[:, sl], gm, preferred_element_type=F32))
    return outs[0] if nbk == 1 else jnp.concatenate(outs, axis=1)


class _RowReader:
    def __init__(self, ref):
        self.ref = ref

    def __getitem__(self, rows):
        return self.ref[rows, :]


def _shifted(ext_ref, cur, before8, after8, downs=(), ups=()):
    r = cur.shape[0]
    if downs:
        ext_ref[0:8, :] = before8
    ext_ref[8:8 + r, :] = cur
    if ups:
        ext_ref[8 + r:16 + r, :] = after8
    return [ext_ref[pl.ds(8 - j, r), :] for j in downs] + [ext_ref[pl.ds(8 + j, r), :] for j in ups]


def _lru_gates(xc, pv, wa_ref, wx_ref):
    xcb = xc.astype(MXU_DTYPE)
    ga = jax.nn.sigmoid(_block_mm(xcb, wa_ref, False) + pv[5:6])
    gx = jax.nn.sigmoid(_block_mm(xcb, wx_ref, False) + pv[6:7])
    sp = _softplus(-pv[7:8])
    log_a = -LRU_C * ga * sp
    a = jnp.exp(log_a)
    e2 = _expm1(2.0 * log_a)
    mult = jnp.sqrt(-e2)
    return xcb, ga, gx, sp, a, e2, mult


def _gelu_parts(y):
    th = jnp.tanh(GELU_K * (y + GELU_C * y * y * y))
    return 0.5 * y * (1.0 + th), th


def _mixer_fwd(z, pv, wa, wx, gm, pad, name, carried=()):
    m = z.shape[0]
    c = pv.shape[1]
    r = MIX_ROWS
    nb = m // r

    def body(z_ref, pv_ref, wa_ref, wx_ref, gm_ref, mixed_ref, hs_ref, ext_ref, tailx_ref, tailc_ref, carry_ref):
        b = pl.program_id(0)

        @pl.when(b == 0)
        def _():
            tailx_ref[...] = jnp.zeros_like(tailx_ref)
            tailc_ref[...] = jnp.zeros_like(tailc_ref)
            carry_ref[...] = jnp.zeros_like(carry_ref)

        pv = _RowReader(pv_ref)
        row = b * r + lax.broadcasted_iota(jnp.int32, (r, 1), 0)
        lrow = lax.broadcasted_iota(jnp.int32, (r, c), 0)
        maskf = (row >= pad).astype(F32)
        y = z_ref[:, 0:c]
        xl = z_ref[:, c:2 * c]
        bs = z_ref[:, 2 * c:3 * c]
        cv = z_ref[:, 3 * c:4 * c] * z_ref[:, 4 * c:5 * c]

        x1, x2, x3 = _shifted(ext_ref, xl, tailx_ref[...], None, downs=(1, 2, 3))
        tailx_ref[...] = z_ref[pl.ds(r - 8, 8), c:2 * c]
        xc = pv[4:5] + pv[3:4] * xl + pv[2:3] * x1 + pv[1:2] * x2 + pv[0:1] * x3
        _, _, gx, _, a, _, mult = _lru_gates(xc, pv, wa_ref, wx_ref)
        uu = mult * (gx * xc) * maskf

        acc_a = a
        acc_h = uu
        dlt = 1
        while dlt < r:
            keep = lrow >= dlt
            sh_a = pltpu.roll(acc_a, dlt, axis=0)
            sh_h = pltpu.roll(acc_h, dlt, axis=0)
            acc_h = acc_h + acc_a * jnp.where(keep, sh_h, 0.0)
            acc_a = acc_a * jnp.where(keep, sh_a, 1.0)
            dlt *= 2
        hs = acc_h + acc_a * carry_ref[...]
        hs_ref[...] = hs
        carry_ref[...] = hs_ref[pl.ds(r - 1, 1), :]

        gelu_y, _ = _gelu_parts(y)
        lru_out = hs * gelu_y
        c1, c2 = _shifted(ext_ref, cv, tailc_ref[...], None, downs=(1, 2))
        tailc_ref[...] = cv[r - 8:r]
        sc_out = bs * (pv[10:11] * cv + pv[9:10] * c1 + pv[8:9] * c2)

        rl = lax.rsqrt(_group_mean(lru_out * lru_out, gm_ref) + EPS)
        rs = lax.rsqrt(_group_mean(sc_out * sc_out, gm_ref) + EPS)
        mixed_ref[:, 0:c] = (lru_out * rl * pv[11:12]).astype(mixed_ref.dtype)
        mixed_ref[:, c:2 * c] = (sc_out * rs * pv[12:13]).astype(mixed_ref.dtype)

    full = lambda shape: pl.BlockSpec(shape, lambda b: (0,) * len(shape))
    return _call(
        body, grid=(nb,),
        in_specs=[pl.BlockSpec((r, 5 * c), lambda b: (b, 0)), full(pv.shape), full(wa.shape), full(wx.shape), full(gm.shape)],
        out_specs=[pl.BlockSpec((r, 2 * c), lambda b: (b, 0)), pl.BlockSpec((r, c), lambda b: (b, 0))],
        out_shape=[jax.ShapeDtypeStruct((m, 2 * c), MXU_DTYPE), jax.ShapeDtypeStruct((m, c), F32)],
        scratch_shapes=[pltpu.VMEM((r + 16, c), F32), pltpu.VMEM((8, c), F32), pltpu.VMEM((8, c), F32),
                        pltpu.VMEM((1, c), F32)],
        name=name, args=(z, pv, wa, wx, gm), carried=carried)


def _mixer_bwd(z, hs, dmixed, pv, wa, wx, gm, pad, name, carried=()):
    m = z.shape[0]
    c = pv.shape[1]
    r = MIX_ROWS
    nb = m // r
    r8 = r // 8

    def body(z_ref, zp_ref, hs_ref, hsp_ref, dm_ref, pv_ref, wa_ref, wx_ref, gm_ref,
             dz_ref, dpv_ref, dwa_ref, dwx_ref, ext_ref, hxc_ref, hsc_ref, hp_ref):
        i = pl.program_id(0)
        b = nb - 1 - i

        @pl.when(i == 0)
        def _():
            hxc_ref[...] = jnp.zeros_like(hxc_ref)
            hsc_ref[...] = jnp.zeros_like(hsc_ref)
            hp_ref[...] = jnp.zeros_like(hp_ref)
            dpv_ref[...] = jnp.zeros_like(dpv_ref)
            dwa_ref[...] = jnp.zeros_like(dwa_ref)
            dwx_ref[...] = jnp.zeros_like(dwx_ref)

        pv = _RowReader(pv_ref)
        row = b * r + lax.broadcasted_iota(jnp.int32, (r, 1), 0)
        lrow = lax.broadcasted_iota(jnp.int32, (r, c), 0)
        maskf = (row >= pad).astype(F32)
        has_prev = (b > 0).astype(F32)
        y = z_ref[:, 0:c]
        xl = z_ref[:, c:2 * c]
        bs = z_ref[:, 2 * c:3 * c]
        cs = z_ref[:, 3 * c:4 * c]
        vs = z_ref[:, 4 * c:5 * c]
        cv = cs * vs
        xl_prev = zp_ref[:, c:2 * c] * has_prev
        cv_prev = zp_ref[:, 3 * c:4 * c] * zp_ref[:, 4 * c:5 * c] * has_prev
        hs = hs_ref[...]

        x1, x2, x3 = _shifted(ext_ref, xl, xl_prev, None, downs=(1, 2, 3))
        xc = pv[4:5] + pv[3:4] * xl + pv[2:3] * x1 + pv[1:2] * x2 + pv[0:1] * x3
        xcb, ga, gx, sp, a, e2, mult = _lru_gates(xc, pv, wa_ref, wx_ref)
        gxx = gx * xc
        gelu_y, th = _gelu_parts(y)
        lru_out = hs * gelu_y
        c1, c2 = _shifted(ext_ref, cv, cv_prev, None, downs=(1, 2))
        sc = pv[10:11] * cv + pv[9:10] * c1 + pv[8:9] * c2
        sc_out = bs * sc

        def group_norm_bwd(v, dm, gain):
            rr = lax.rsqrt(_group_mean(v * v, gm_ref) + EPS)
            vh = v * rr
            dvh = dm * gain
            dv = rr * (dvh - vh * _group_mean(dvh * vh, gm_ref))
            return dv, jnp.sum(dm * vh, axis=0, keepdims=True)

        d_lru_out, d_og = group_norm_bwd(lru_out, dm_ref[:, 0:c], pv[11:12])
        d_sc_out, d_sg = group_norm_bwd(sc_out, dm_ref[:, c:2 * c], pv[12:13])
        dpv_ref[11:12, :] += d_og
        dpv_ref[12:13, :] += d_sg

        dhs = d_lru_out * gelu_y
        dgelu = 0.5 * (1.0 + th) + 0.5 * y * (1.0 - th * th) * GELU_K * (1.0 + 3.0 * GELU_C * y * y)
        dy = d_lru_out * hs * dgelu

        acc_a = a
        acc_p = a * dhs
        dlt = 1
        while dlt < r:
            keep = lrow < r - dlt
            sh_a = pltpu.roll(acc_a, r - dlt, axis=0)
            sh_p = pltpu.roll(acc_p, r - dlt, axis=0)
            acc_p = acc_p + acc_a * jnp.where(keep, sh_p, 0.0)
            acc_a = acc_a * jnp.where(keep, sh_a, 1.0)
            dlt *= 2
        p_all = acc_p + acc_a * hp_ref[0:1, :]
        (p_next,) = _shifted(ext_ref, p_all, None, hp_ref[...], ups=(1,))
        hp_ref[...] = p_all[0:8]
        q = dhs + p_next
        (hs_prev,) = _shifted(ext_ref, hs, hsp_ref[...] * has_prev, None, downs=(1,))
        duu = q * maskf
        da = q * hs_prev

        dmult = duu * gxx
        dgxx = duu * mult
        dgx = dgxx * xc
        dxc = dgxx * gx
        dlog_a = da * a - dmult * ((1.0 + e2) / mult)
        dga = dlog_a * (-LRU_C * sp)
        dsp = jnp.sum(dlog_a * (-LRU_C * ga), axis=0, keepdims=True)
        dpv_ref[7:8, :] += dsp * (-jax.nn.sigmoid(-pv[7:8]))
        dga_pre = dga * ga * (1.0 - ga)
        dgx_pre = dgx * gx * (1.0 - gx)
        dpv_ref[5:6, :] += jnp.sum(dga_pre, axis=0, keepdims=True)
        dpv_ref[6:7, :] += jnp.sum(dgx_pre, axis=0, keepdims=True)
        dga_b = dga_pre.astype(MXU_DTYPE)
        dgx_b = dgx_pre.astype(MXU_DTYPE)
        dxc = dxc + _block_mm(dga_b, wa_ref, True) + _block_mm(dgx_b, wx_ref, True)
        for j in range(c // BD):
            sl = slice(j * BD, (j + 1) * BD)
            dwa_ref[j] += _dot_tn(xcb[:, sl], dga_b[:, sl])
            dwx_ref[j] += _dot_tn(xcb[:, sl], dgx_b[:, sl])

        dpv_ref[4:5, :] += jnp.sum(dxc, axis=0, keepdims=True)
        dpv_ref[3:4, :] += jnp.sum(dxc * xl, axis=0, keepdims=True)
        dpv_ref[2:3, :] += jnp.sum(dxc * x1, axis=0, keepdims=True)
        dpv_ref[1:2, :] += jnp.sum(dxc * x2, axis=0, keepdims=True)
        dpv_ref[0:1, :] += jnp.sum(dxc * x3, axis=0, keepdims=True)
        u1, u2, u3 = _shifted(ext_ref, dxc, None, hxc_ref[...], ups=(1, 2, 3))
        hxc_ref[...] = dxc[0:8]
        dxl = pv[3:4] * dxc + pv[2:3] * u1 + pv[1:2] * u2 + pv[0:1] * u3

        dbs = d_sc_out * sc
        dsc = d_sc_out * bs
        dpv_ref[10:11, :] += jnp.sum(dsc * cv, axis=0, keepdims=True)
        dpv_ref[9:10, :] += jnp.sum(dsc * c1, axis=0, keepdims=True)
        dpv_ref[8:9, :] += jnp.sum(dsc * c2, axis=0, keepdims=True)
        s1, s2 = _shifted(ext_ref, dsc, None, hsc_ref[...], ups=(1, 2))
        hsc_ref[...] = dsc[0:8]
        dcv = pv[10:11] * dsc + pv[9:10] * s1 + pv[8:9] * s2

        dz_ref[:, 0:c] = (dy * maskf).astype(dz_ref.dtype)
        dz_ref[:, c:2 * c] = (dxl * maskf).astype(dz_ref.dtype)
        dz_ref[:, 2 * c:3 * c] = (dbs * maskf).astype(dz_ref.dtype)
        dz_ref[:, 3 * c:4 * c] = (dcv * vs * maskf).astype(dz_ref.dtype)
        dz_ref[:, 4 * c:5 * c] = (dcv * cs * maskf).astype(dz_ref.dtype)

    full = lambda shape: pl.BlockSpec(shape, lambda i: (0,) * len(shape))
    cur = lambda width: pl.BlockSpec((r, width), lambda i: (nb - 1 - i, 0))
    prev8 = lambda width: pl.BlockSpec((8, width), lambda i: (jnp.maximum((nb - 1 - i) * r8 - 1, 0), 0))
    return _call(
        body, grid=(nb,),
        in_specs=[cur(5 * c), prev8(5 * c), cur(c), prev8(c), cur(2 * c),
                  full(pv.shape), full(wa.shape), full(wx.shape), full(gm.shape)],
        out_specs=[cur(5 * c), full(pv.shape), full(wa.shape), full(wx.shape)],
        out_shape=[jax.ShapeDtypeStruct((m, 5 * c), MXU_DTYPE), jax.ShapeDtypeStruct(pv.shape, F32),
                   jax.ShapeDtypeStruct(wa.shape, F32), jax.ShapeDtypeStruct(wx.shape, F32)],
        scratch_shapes=[pltpu.VMEM((r + 16, c), F32), pltpu.VMEM((8, c), F32), pltpu.VMEM((8, c), F32),
                        pltpu.VMEM((8, c), F32)],
        name=name, args=(z, z, hs, hs, dmixed, pv, wa, wx, gm), carried=carried)


def _position():
    return lax.axis_index("x"), lax.axis_index("y"), lax.axis_index("c")


def _block_of(px, py, pc):
    return 4 * px + 2 * py + pc


class _TwoLevelGather:
    def __init__(self, n_arrays, rows_of, src_of, send_sems, recv_sems):
        x, y, c = _position()
        self.n, self.rows_of, self.src_of = n_arrays, rows_of, src_of
        self.send_sems, self.recv_sems = send_sems, recv_sems
        self.c, self.me, self.sibling = c, (x, y, c), (x, y, 1 - c)
        self.chips = [(1 - x, y), (x, 1 - y), (1 - x, 1 - y)]

    def _copy(self, i, k, block, to, src=None):
        return pltpu.make_async_remote_copy(
            src_ref=self.rows_of(i, *block) if src is None else src, dst_ref=self.rows_of(i, *block),
            send_sem=self.send_sems.at[7 * i + k], recv_sem=self.recv_sems.at[7 * i + k],
            device_id=to, device_id_type=MESH)

    def _first(self, i):
        own = [self._copy(i, 0, self.me, self.sibling, src=self.src_of(i))]
        return own + [self._copy(i, 1 + j, self.me, (*chip, self.c), src=self.src_of(i))
                      for j, chip in enumerate(self.chips)]

    def _passed(self, i, j):
        return self._copy(i, 4 + j, (*self.chips[j], self.c), self.sibling)

    def start(self):
        for i in range(self.n):
            for cp in self._first(i):
                cp.start()

    def forward(self):
        for i in range(self.n):
            for j, chip in enumerate(self.chips):
                self._copy(i, 1 + j, (*chip, self.c), self.me).wait_recv()
                self._passed(i, j).start()

    def drain(self):
        for i in range(self.n):
            self._copy(i, 0, self.sibling, self.me).wait_recv()
            for j, chip in enumerate(self.chips):
                self._copy(i, 4 + j, (*chip, 1 - self.c), self.me).wait_recv()
        for i in range(self.n):
            for cp in self._first(i) + [self._passed(i, j) for j in range(3)]:
                cp.wait_send()


class _CarriedGather:
    def __init__(self, shards, padded_rows, zeros, forward_at, part=None, into=None):
        d = shards[0].shape[1]
        self.forward_at = forward_at
        self.n = len(shards)
        self.rows = [s.shape[0] for s in shards]
        self.pads = [p - N_DEV * r for r, p in zip(self.rows, padded_rows)]
        assert max(self.pads) <= zeros.shape[0] and zeros.shape[1] == d
        self.part = part if part is not None else (0, self.rows[0])
        assert (part is None and into is None) or self.n == 1
        assert self.part[0] % SUBLANE_BF16 == 0 and self.part[1] % SUBLANE_BF16 == 0
        self.arrays = list(shards) + [zeros] + ([into] if into is not None else [])
        self.out_shapes = [jax.ShapeDtypeStruct((p, d), s.dtype) for s, p in zip(shards, padded_rows)]
        self.aliases = {self.n + 1: 0} if into is not None else {}
        if into is not None:
            self.pads = [0] * self.n
        self.n_remote, self.n_local = 7 * self.n, 2 * self.n
        self.results = None

    def _rows_of(self, outs):
        def rows_of(i, px, py, pc):
            first = _block_of(px, py, pc) * self.rows[i] + (self.part[0] if self.n == 1 else 0)
            count = self.part[1] if self.n == 1 else self.rows[i]
            return outs[i].at[pl.ds(pl.multiple_of(first, SUBLANE_BF16), count), :]
        return rows_of

    def _own(self, ins, i):
        return ins[i].at[pl.ds(self.part[0], self.part[1]), :] if self.n == 1 else ins[i]

    def _gather(self, ins, outs, send_sems, recv_sems):
        return _TwoLevelGather(self.n, self._rows_of(outs), functools.partial(self._own, ins), send_sems, recv_sems)

    def _local(self, ins, outs, local_sems):
        x, y, c = _position()
        rows_of = self._rows_of(outs)
        cps = []
        for i in range(self.n):
            cps.append(pltpu.make_async_copy(self._own(ins, i), rows_of(i, x, y, c), local_sems.at[2 * i]))
            if self.pads[i]:
                cps.append(pltpu.make_async_copy(ins[self.n].at[pl.ds(0, self.pads[i]), :],
                                                 outs[i].at[pl.ds(N_DEV * self.rows[i], self.pads[i]), :],
                                                 local_sems.at[2 * i + 1]))
        return cps

    def start(self, ins, outs, send_sems, recv_sems, local_sems):
        for cp in self._local(ins, outs, local_sems):
            cp.start()
        self._gather(ins, outs, send_sems, recv_sems).start()

    def forward(self, ins, outs, send_sems, recv_sems, local_sems):
        self._gather(ins, outs, send_sems, recv_sems).forward()

    def finish(self, ins, outs, send_sems, recv_sems, local_sems):
        self._gather(ins, outs, send_sems, recv_sems).drain()
        for cp in self._local(ins, outs, local_sems):
            cp.wait()


class _CarriedSwap:
    def __init__(self, grads, shard_rows):
        d = grads[0].shape[1]
        self.n, self.rows = len(grads), list(shard_rows)
        self.arrays = list(grads)
        self.out_shapes = [jax.ShapeDtypeStruct((4, s, d), g.dtype) for g, s in zip(grads, shard_rows)]
        self.aliases = {}
        self.n_remote, self.n_local = 4 * self.n, 0
        self.forward_at = 1.0
        self.results = None

    def _copies(self, ins, outs, send_sems, recv_sems):
        x, y, c = _position()
        cps = []
        for i in range(self.n):
            s = self.rows[i]
            for k in range(4):
                blk = _block_of(k >> 1, k & 1, 1 - c)
                cps.append(pltpu.make_async_remote_copy(
                    src_ref=ins[i].at[pl.ds(pl.multiple_of(blk * s, SUBLANE_BF16), s), :], dst_ref=outs[i].at[k],
                    send_sem=send_sems.at[4 * i + k], recv_sem=recv_sems.at[4 * i + k],
                    device_id=(x, y, 1 - c), device_id_type=MESH))
        return cps

    def start(self, ins, outs, send_sems, recv_sems, local_sems):
        for cp in self._copies(ins, outs, send_sems, recv_sems):
            cp.start()

    def forward(self, *_):
        pass

    def finish(self, ins, outs, send_sems, recv_sems, local_sems):
        for cp in self._copies(ins, outs, send_sems, recv_sems):
            cp.wait()


class _CarriedChipExchange:
    def __init__(self, presums):
        self.n = len(presums)
        self.arrays = list(presums)
        self.out_shapes = [jax.ShapeDtypeStruct(p.shape, p.dtype) for p in presums]
        self.aliases = {}
        self.n_remote, self.n_local = 3 * self.n, 0
        self.forward_at = 1.0
        self.results = None

    def _copies(self, ins, outs, send_sems, recv_sems):
        x, y, c = _position()
        cps = []
        for i in range(self.n):
            for r in range(1, 4):
                cps.append(pltpu.make_async_remote_copy(
                    src_ref=ins[i].at[r - 1], dst_ref=outs[i].at[r - 1],
                    send_sem=send_sems.at[3 * i + r - 1], recv_sem=recv_sems.at[3 * i + r - 1],
                    device_id=(x ^ (r >> 1), y ^ (r & 1), c), device_id_type=MESH))
        return cps

    def start(self, ins, outs, send_sems, recv_sems, local_sems):
        for cp in self._copies(ins, outs, send_sems, recv_sems):
            cp.start()

    def forward(self, *_):
        pass

    def finish(self, ins, outs, send_sems, recv_sems, local_sems):
        for cp in self._copies(ins, outs, send_sems, recv_sems):
            cp.wait()


def _gather_small(block, reduce, name):
    rr, nn = block.shape

    def body(x_ref, out_ref, *rest):
        if reduce:
            stack_ref, send_sems, recv_sems, local_sem = rest
        else:
            send_sems, recv_sems, local_sem = rest
            stack_ref = out_ref
        x, y, c = _position()

        def rows_of(i, px, py, pc):
            return stack_ref.at[pl.ds(pl.multiple_of(_block_of(px, py, pc) * rr, 8), rr), :]

        own = pltpu.make_async_copy(x_ref, rows_of(0, x, y, c), local_sem)
        own.start()
        gather = _TwoLevelGather(1, rows_of, lambda i: x_ref, send_sems, recv_sems)
        gather.start()
        gather.forward()
        gather.drain()
        own.wait()
        if reduce:
            acc = stack_ref[0:rr, :]
            for k in range(1, N_DEV):
                acc = acc + stack_ref[k * rr:(k + 1) * rr, :]
            out_ref[...] = acc

    vmem = pl.BlockSpec(memory_space=pltpu.VMEM)
    scratch = [pltpu.SemaphoreType.DMA((7,)), pltpu.SemaphoreType.DMA((7,)), pltpu.SemaphoreType.DMA]
    if reduce:
        scratch = [pltpu.VMEM((N_DEV * rr, nn), F32)] + scratch
    out_rows = rr if reduce else N_DEV * rr
    return pl.pallas_call(
        body, in_specs=[vmem], out_specs=vmem, out_shape=jax.ShapeDtypeStruct((out_rows, nn), F32),
        scratch_shapes=scratch, name=name, compiler_params=_params())(block)


def _sum_stack(stack, name):
    rr = stack.shape[0] // N_DEV

    def body(s_ref, o_ref):
        acc = s_ref[0:rr, :]
        for k in range(1, N_DEV):
            acc = acc + s_ref[k * rr:(k + 1) * rr, :]
        o_ref[...] = acc

    vmem = pl.BlockSpec(memory_space=pltpu.VMEM)
    return pl.pallas_call(body, in_specs=[vmem], out_specs=vmem,
                          out_shape=jax.ShapeDtypeStruct((rr, stack.shape[1]), F32), name=name,
                          compiler_params=_params())(stack)


def _presum(where, grad, swapped, name):
    s, d = swapped.shape[1], swapped.shape[2]
    tc = _tile(d, 512, LANE)

    def body(where_ref, g_ref, sw_ref, o_ref):
        o_ref[0] = (g_ref[...].astype(F32) + sw_ref[0].astype(F32)).astype(o_ref.dtype)

    return _call(
        body, grid=(3, d // tc),
        in_specs=[pl.BlockSpec((s, tc), lambda r, j, where: (where[1 + r], j)),
                  pl.BlockSpec((1, s, tc), lambda r, j, where: (where[5 + r], 0, j))],
        out_specs=pl.BlockSpec((1, s, tc), lambda r, j, where: (r, 0, j)),
        out_shape=jax.ShapeDtypeStruct((3, s, d), WIRE_DTYPE), name=name, args=(grad, swapped), prefetch=(where,))


def _final_sum(where, grad, swapped, received, name, carried=()):
    s, d = swapped.shape[1], swapped.shape[2]
    tc = _tile(d, 512, LANE)

    def body(where_ref, g_ref, sw_ref, r_ref, o_ref):
        acc = g_ref[...].astype(F32) + sw_ref[0].astype(F32)
        for k in range(3):
            acc = acc + r_ref[k].astype(F32)
        o_ref[...] = acc

    return _call(
        body, grid=(d // tc,),
        in_specs=[pl.BlockSpec((s, tc), lambda j, where: (where[0], j)),
                  pl.BlockSpec((1, s, tc), lambda j, where: (where[4], 0, j)),
                  pl.BlockSpec((3, s, tc), lambda j, where: (0, 0, j))],
        out_specs=pl.BlockSpec((s, tc), lambda j, where: (0, j)),
        out_shape=jax.ShapeDtypeStruct((s, d), F32), name=name, args=(grad, swapped, received),
        prefetch=(where,), carried=carried)


class _GradReduction:
    def __init__(self, key, grad, shard_rows, where):
        self.key, self.grad, self.rows, self.where = key, grad, shard_rows, where

    def swap(self):
        self._swap = _CarriedSwap([self.grad], [self.rows])
        return self._swap

    def exchange(self):
        presum = _presum(self.where, self.grad, self._swap.results[0], "presum_" + self.key)
        self._exchange = _CarriedChipExchange([presum])
        return self._exchange

    def total(self, carried=()):
        return _final_sum(self.where, self.grad, self._swap.results[0], self._exchange.results[0],
                          "sum_" + self.key, carried)

    def total_and_update(self, w, m, v):
        return _sum_adamw(self.where, self.grad, self._swap.results[0], self._exchange.results[0], w, m, v,
                          "update_" + self.key)


def _adamw_math(w, g, m, v):
    nm = ADAM_B1 * m + (1.0 - ADAM_B1) * g
    nv = ADAM_B2 * v + (1.0 - ADAM_B2) * (g * g)
    m_hat = nm / (1.0 - ADAM_B1 ** ADAM_STEP)
    v_hat = nv / (1.0 - ADAM_B2 ** ADAM_STEP)
    return -ADAM_LR * (m_hat / (jnp.sqrt(v_hat) + ADAM_EPS) + ADAM_WD * w), nm, nv


def _sum_adamw(where, grad, swapped, received, w, m, v, name):
    s, d = swapped.shape[1], swapped.shape[2]
    tc = _tile(d, 512, LANE)

    def body(where_ref, g_ref, sw_ref, r_ref, w_ref, m_ref, v_ref, gs_ref, d_ref, nm_ref, nv_ref):
        g = g_ref[...].astype(F32) + sw_ref[0].astype(F32)
        for k in range(3):
            g = g + r_ref[k].astype(F32)
        gs_ref[...] = g
        d_ref[...], nm_ref[...], nv_ref[...] = _adamw_math(w_ref[...], g, m_ref[...], v_ref[...])

    blk = pl.BlockSpec((s, tc), lambda j, where: (0, j))
    return _call(
        body, grid=(d // tc,),
        in_specs=[pl.BlockSpec((s, tc), lambda j, where: (where[0], j)),
                  pl.BlockSpec((1, s, tc), lambda j, where: (where[4], 0, j)),
                  pl.BlockSpec((3, s, tc), lambda j, where: (0, 0, j)), blk, blk, blk],
        out_specs=[blk] * 4, out_shape=[jax.ShapeDtypeStruct((s, d), F32)] * 4, name=name,
        args=(grad, swapped, received, w, m, v), prefetch=(where,))


def _adamw(w, g, m, v, name):
    rows, cols = w.shape
    tr = _tile(rows, 256, 8)

    def body(w_ref, g_ref, m_ref, v_ref, d_ref, nm_ref, nv_ref):
        d_ref[...], nm_ref[...], nv_ref[...] = _adamw_math(w_ref[...], g_ref[...], m_ref[...], v_ref[...])

    spec = pl.BlockSpec((tr, cols), lambda i: (i, 0))
    return pl.pallas_call(
        body, grid=(rows // tr,), in_specs=[spec] * 4, out_specs=[spec] * 3,
        out_shape=[jax.ShapeDtypeStruct((rows, cols), F32)] * 3, name=name, compiler_params=_params())(w, g, m, v)


def _pack_rows(arrays, width, row_quantum=8):
    flat = jnp.concatenate([a.reshape(-1) for a in arrays])
    total = _round_up(flat.shape[0], row_quantum * width)
    flat = jnp.pad(flat, (0, total - flat.shape[0]))
    return flat.reshape(-1, width)


def _unpack_rows(packed, shapes):
    flat = packed.reshape(-1)
    out = []
    off = 0
    for shp in shapes:
        size = 1
        for s in shp:
            size *= s
        out.append(flat[off:off + size].reshape(shp))
        off += size
    return out


def _block_diag(w):
    h, hb, _ = w.shape
    per = BD // hb
    w4 = w.reshape(h // per, per, hb, hb)
    eye = jnp.eye(per, dtype=w.dtype)
    return jnp.einsum('npij,pq->npiqj', w4, eye).reshape(h // per, BD, BD)


def _block_diag_extract(bd, hb):
    nbk = bd.shape[0]
    per = BD // hb
    b5 = bd.reshape(nbk, per, hb, per, hb)
    eye = jnp.eye(per, dtype=bd.dtype)
    return jnp.einsum('npiqj,pq->npij', b5, eye).reshape(nbk * per, hb, hb)


def kernel(x, meta_tokens, ffn1_pre_g, ffn1_w_gate, ffn1_w_up, ffn1_w_down, ffn1_post_g, mix_pre_g, w_in, lru_conv_w, lru_conv_b, lru_w_a, lru_b_a, lru_w_x, lru_b_x, lru_lambda, sconv_w, lru_out_g, sconv_out_g, w_out, mix_post_g, ffn2_pre_g, ffn2_w_gate, ffn2_w_up, ffn2_w_down, ffn2_post_g, loss_target, m_meta_tokens, m_ffn1_pre_g, m_ffn1_w_gate, m_ffn1_w_up, m_ffn1_w_down, m_ffn1_post_g, m_mix_pre_g, m_w_in, m_lru_conv_w, m_lru_conv_b, m_lru_w_a, m_lru_b_a, m_lru_w_x, m_lru_b_x, m_lru_lambda, m_sconv_w, m_lru_out_g, m_sconv_out_g, m_w_out, m_mix_post_g, m_ffn2_pre_g, m_ffn2_w_gate, m_ffn2_w_up, m_ffn2_w_down, m_ffn2_post_g, v_meta_tokens, v_ffn1_pre_g, v_ffn1_w_gate, v_ffn1_w_up, v_ffn1_w_down, v_ffn1_post_g, v_mix_pre_g, v_w_in, v_lru_conv_w, v_lru_conv_b, v_lru_w_a, v_lru_b_a, v_lru_w_x, v_lru_b_x, v_lru_lambda, v_sconv_w, v_lru_out_g, v_sconv_out_g, v_w_out, v_mix_post_g, v_ffn2_pre_g, v_ffn2_w_gate, v_ffn2_w_up, v_ffn2_w_down, v_ffn2_post_g):
    given = dict(locals())
    wts = {n: given[n] for n in WEIGHT_NAMES}
    mom = {n: given["m_" + n] for n in WEIGHT_NAMES}
    var = {n: given["v_" + n] for n in WEIGHT_NAMES}

    xi, yi, ci = _position()
    me = _block_of(xi, yi, ci)
    x2 = x[0]
    seq, d = x2.shape
    n_meta = meta_tokens.shape[0]
    m_rows = _round_up(n_meta + seq, ROW_ALIGN)
    pad = m_rows - n_meta - seq
    lead = pad + n_meta
    c = lru_conv_b.shape[1]
    hb = lru_w_a.shape[-1]
    dm = meta_tokens.shape[1]
    cs_ = lru_conv_w.shape[2]
    kw4, kw3 = lru_conv_w.shape[1], sconv_w.shape[1]
    assert d == 2 * c and c % BD == 0 and BD % hb == 0 and cs_ <= dm and kw4 == 4 and kw3 == 3

    small = jnp.zeros((_round_up(n_meta + kw4 + kw3, 8), dm), F32)
    small = small.at[0:n_meta].set(meta_tokens)
    small = small.at[n_meta:n_meta + kw4, 0:cs_].set(lru_conv_w[0])
    small = small.at[n_meta + kw4:n_meta + kw4 + kw3, 0:cs_].set(sconv_w[0])
    sr = small.shape[0]
    small_all = _gather_small(small, False, "gather_small").reshape(N_DEV, sr, dm)
    meta_full = small_all[:, 0:n_meta, :].transpose(1, 0, 2).reshape(n_meta, d)
    conv_w_full = small_all[:, n_meta:n_meta + kw4, 0:cs_].transpose(1, 0, 2).reshape(kw4, c)
    sconv_w_full = small_all[:, n_meta + kw4:n_meta + kw4 + kw3, 0:cs_].transpose(1, 0, 2).reshape(kw3, c)

    big = ['ffn1_w_gate', 'ffn1_w_up', 'ffn1_w_down', 'w_in', 'w_out', 'ffn2_w_gate', 'ffn2_w_up', 'ffn2_w_down']
    col_sharded = {'ffn1_w_gate', 'ffn1_w_up', 'w_in', 'ffn2_w_gate', 'ffn2_w_up'}
    shards = []
    for nme in big:
        w = wts[nme][0].astype(WIRE_DTYPE)
        shards.append(w.T if nme in col_sharded else w)
    shard_rows = dict(zip(big, [s.shape[0] for s in shards]))
    zeros = jnp.zeros((F_ALIGN, d), WIRE_DTYPE)

    def gather(forward_at, *names, part=None, into=None):
        sel = [shards[big.index(nme)] for nme in names]
        padded = [_round_up(N_DEV * shard_rows[nme], LANE if nme in ('w_in', 'w_out') else F_ALIGN) for nme in names]
        return _CarriedGather(sel, padded, zeros, forward_at, part, into)

    pv = jnp.zeros((16, c), F32)
    pv = pv.at[0:4].set(conv_w_full).at[4].set(lru_conv_b[0]).at[5].set(lru_b_a[0]).at[6].set(lru_b_x[0])
    pv = pv.at[7].set(lru_lambda[0]).at[8:11].set(sconv_w_full).at[11].set(lru_out_g[0]).at[12].set(sconv_out_g[0])
    wa_bd = _block_diag(lru_w_a[0]).astype(MXU_DTYPE)
    wx_bd = _block_diag(lru_w_x[0]).astype(MXU_DTYPE)
    gs = c // N_GROUPS
    gidx = jnp.arange(BD) // gs
    gm = jnp.where(gidx[:, None] == gidx[None, :], 1.0 / gs, 0.0).astype(MXU_DTYPE)

    ride = gather(1.0, 'ffn1_w_gate')
    h0, n1, target = _embed(x2, meta_full, loss_target[0], ffn1_pre_g, pad, "embed_prenorm", carried=[ride])
    (wg1,) = ride.results
    ride = gather(1.0, 'ffn1_w_up')
    g1 = _mm_nt(n1, wg1, "ffn1_gate", carried=[ride], out_dtype=MXU_DTYPE)
    (wu1,) = ride.results
    ride = gather(1.0, 'ffn1_w_down')
    u1, a1 = _ffn_up_act(n1, wu1, g1, "ffn1_up_act", carried=[ride])
    (wd1,) = ride.results
    ride = gather(0.9, 'w_in', 'w_out')
    fo1, h1, un = _mm_residual_norm(a1, wd1, h0, ffn1_post_g, 0.5, mix_pre_g, "ffn1_down", carried=[ride])
    win_t, wout = ride.results
    ride = gather(1.0, 'ffn2_w_gate')
    z = _mm_nt(un, win_t, "mix_in_proj", carried=[ride])
    (wg2,) = ride.results
    rows_a = _round_up(shard_rows['ffn2_w_up'] // 2, SUBLANE_BF16)
    ride = gather(1.0, 'ffn2_w_up', part=(0, rows_a))
    mixed, hs = _mixer_fwd(z, pv, wa_bd, wx_bd, gm, pad, "mixer_fwd", carried=[ride])
    ride = gather(1.0, 'ffn2_w_up', part=(rows_a, shard_rows['ffn2_w_up'] - rows_a), into=ride.results[0])
    o_mix, h2, n2 = _mm_residual_norm(mixed, wout, h1, mix_post_g, 1.0, ffn2_pre_g, "mix_out_proj", carried=[ride])
    (wu2,) = ride.results
    ride = gather(0.75, 'ffn2_w_down')
    g2, u2, a2 = _ffn_gate_up(n2, wg2, wu2, "ffn2_gate_up", carried=[ride])
    (wd2,) = ride.results
    dh3, dfo2, d_post2, loss_part = _mm_residual_loss(a2, wd2, h2, ffn2_post_g, 0.5, target, lead, "ffn2_down_loss")
    loss = lax.psum(loss_part[0, 0], ("x", "y", "c"))

    chip_rel = [2 * (xi ^ (r >> 1)) + (yi ^ (r & 1)) for r in range(4)]
    where = jnp.stack([2 * k + ci for k in chip_rel] + chip_rel).astype(jnp.int32)
    red = {}

    def reduction(nme, grad):
        red[nme] = _GradReduction(nme, grad, shard_rows[nme], where)
        return red[nme]

    r_wd2 = reduction('ffn2_w_down', _mm_tn(a2, dfo2, "ffn2_dw_down"))
    dg2, du2 = _ffn_hidden_bwd(dfo2, wd2, g2, u2, "ffn2_hidden_bwd", carried=[r_wd2.swap()])
    r_wg2 = reduction('ffn2_w_gate', _mm_tn(dg2, n2, "ffn2_dw_gate", carried=[r_wd2.exchange()]))
    r_wu2 = reduction('ffn2_w_up', _mm_tn(du2, n2, "ffn2_dw_up", carried=[r_wg2.swap()]))
    dh2, d_pre2 = _mm_norm_bwd([(dg2, wg2), (du2, wu2)], h2, ffn2_pre_g, dh3, "ffn2_dx",
                               carried=[r_wg2.exchange(), r_wu2.swap()])
    do_mix, d_mix_post = _norm_bwd(o_mix, mix_post_g, dh2, 1.0, "mix_postnorm_bwd")
    dmixed = _mm_nt(do_mix, wout, "mix_out_proj_bwd")
    r_wout = reduction('w_out', _mm_tn(mixed, do_mix, "mix_dw_out"))
    dz, dpv, dwa_bd, dwx_bd = _mixer_bwd(z, hs, dmixed, pv, wa_bd, wx_bd, gm, pad, "mixer_bwd",
                                         carried=[r_wu2.exchange(), r_wout.swap()])
    r_win = reduction('w_in', _mm_tn(dz, un, "mix_dw_in", carried=[r_wout.exchange()]))
    dh1, d_mix_pre, dfo1, d_post1 = _mm_norm_bwd([(dz, win_t)], h1, mix_pre_g, dh2, "mix_dx", carried=[r_win.swap()],
                                                 post=(fo1, ffn1_post_g, 0.5))
    r_wd1 = reduction('ffn1_w_down', _mm_tn(a1, dfo1, "ffn1_dw_down", carried=[r_win.exchange()]))
    early_names = ['mix_pre_g', 'mix_post_g', 'ffn2_pre_g', 'ffn2_post_g', 'ffn1_post_g',
                   'lru_conv_b', 'lru_b_a', 'lru_b_x', 'lru_lambda', 'lru_out_g', 'sconv_out_g',
                   'lru_conv_w', 'sconv_w', 'lru_w_a', 'lru_w_x']
    early_parts = [d_mix_pre, d_mix_post, d_pre2, d_post2, d_post1,
                   dpv[4:5], dpv[5:6], dpv[6:7], dpv[7:8], dpv[11:12], dpv[12:13],
                   dpv[0:4], dpv[8:11], _block_diag_extract(dwa_bd, hb), _block_diag_extract(dwx_bd, hb)]
    early_packed = _pack_rows(early_parts, d, SUBLANE_BF16)
    early_ride = _CarriedGather([early_packed], [N_DEV * early_packed.shape[0]], zeros, 0.75)
    dg1, du1 = _ffn_hidden_bwd(dfo1, wd1, g1, u1, "ffn1_hidden_bwd", carried=[r_wd1.swap(), early_ride])
    early_sum = _sum_stack(early_ride.results[0], "sum_small_early")
    r_wg1 = reduction('ffn1_w_gate', _mm_tn(dg1, n1, "ffn1_dw_gate", carried=[r_wd1.exchange()]))
    r_wu1 = reduction('ffn1_w_up', _mm_tn(du1, n1, "ffn1_dw_up", carried=[r_wg1.swap()]))
    row_tile = _norm_bwd_row_tile(m_rows)
    n_tiles = m_rows // row_tile
    half = n_tiles // 2
    assert half >= 1 and half * row_tile >= lead
    dh0_a, d_pre1_a = _mm_norm_bwd([(dg1, wg1), (du1, wu1)], h0, ffn1_pre_g, dh1, "ffn1_dx_a",
                                   carried=[r_wg1.exchange(), r_wu1.swap()], row_tiles=(0, half))
    dh0_b, d_pre1 = _mm_norm_bwd([(dg1, wg1), (du1, wu1)], h0, ffn1_pre_g, dh1, "ffn1_dx_b",
                                 carried=[r_wu1.exchange()], row_tiles=(half, n_tiles - half), dg_init=d_pre1_a)
    grad_x = jnp.concatenate([dh0_a[lead:], dh0_b], axis=0)[None]
    d_meta = dh0_a[pad:lead]

    grads, delta, new_m, new_v = {}, {}, {}, {}
    for nme in big:
        in_shard_layout = nme not in col_sharded or shard_rows[nme] % LANE != 0
        if in_shard_layout:
            view = (lambda t: t[0].T) if nme in col_sharded else (lambda t: t[0])
            back = (lambda t: t.T[None]) if nme in col_sharded else (lambda t: t[None])
            outs = red[nme].total_and_update(view(wts[nme]), view(mom[nme]), view(var[nme]))
            grads[nme], delta[nme], new_m[nme], new_v[nme] = [back(t) for t in outs]
        else:
            grads[nme] = red[nme].total().T[None]
            outs = _adamw(wts[nme][0], grads[nme][0], mom[nme][0], var[nme][0], "adamw_" + nme)
            delta[nme], new_m[nme], new_v[nme] = [t[None] for t in outs]

    late_names = ['ffn1_pre_g', 'meta_tokens']
    late_parts = [d_pre1, d_meta]
    late_sum = _gather_small(_pack_rows(late_parts, d), True, "reduce_small_late")
    small_sums = (_unpack_rows(early_sum, [p.shape for p in early_parts])
                  + _unpack_rows(late_sum, [p.shape for p in late_parts]))
    for nme, gsm in zip(early_names + late_names, small_sums):
        if nme == 'meta_tokens':
            grads[nme] = lax.dynamic_slice_in_dim(gsm, me * dm, dm, axis=1)
        elif nme in ('lru_conv_w', 'sconv_w'):
            grads[nme] = lax.dynamic_slice_in_dim(gsm, me * cs_, cs_, axis=1)[None]
        else:
            grads[nme] = gsm.reshape(wts[nme].shape)

    rest = [n for n in WEIGHT_NAMES if n not in big]
    rest_shapes = [wts[n].shape for n in rest]
    packed = [_pack_rows([src[n] for n in rest], LANE, 256) for src in (wts, grads, mom, var)]
    for out, packed_out in zip((delta, new_m, new_v), _adamw(*packed, "adamw_small")):
        for nme, arr in zip(rest, _unpack_rows(packed_out, rest_shapes)):
            out[nme] = arr

    return (loss, grad_x, *[grads[n] for n in WEIGHT_NAMES], *[delta[n] for n in WEIGHT_NAMES],
            *[new_m[n] for n in WEIGHT_NAMES], *[new_v[n] for n in WEIGHT_NAMES])
```

```python
import functools

import jax
import jax.numpy as jnp
from jax import lax
from jax.experimental import pallas as pl
from jax.experimental.pallas import tpu as pltpu

F32 = jnp.float32
MXU_DTYPE = jnp.bfloat16
WIRE_DTYPE = jnp.bfloat16
MESH = pl.DeviceIdType.MESH

EPS = 1e-6
LRU_C = 8.0
N_GROUPS = 16
ADAM_LR = 0.001
ADAM_B1 = 0.9
ADAM_B2 = 0.999
ADAM_EPS = 1e-08
ADAM_WD = 0.01
ADAM_STEP = 10

N_DEV = 8
LANE = 128
SUBLANE_BF16 = 16
ROW_ALIGN = 128
F_ALIGN = 512
BD = 256
K_TILE = 512
ACC_ROWS = 528
ACC_GROUP = 1
MIX_ROWS = 128
VMEM_LIMIT_MB = 56

WEIGHT_NAMES = ['meta_tokens', 'ffn1_pre_g', 'ffn1_w_gate', 'ffn1_w_up', 'ffn1_w_down', 'ffn1_post_g',
                'mix_pre_g', 'w_in', 'lru_conv_w', 'lru_conv_b', 'lru_w_a', 'lru_b_a', 'lru_w_x', 'lru_b_x',
                'lru_lambda', 'sconv_w', 'lru_out_g', 'sconv_out_g', 'w_out', 'mix_post_g', 'ffn2_pre_g',
                'ffn2_w_gate', 'ffn2_w_up', 'ffn2_w_down', 'ffn2_post_g']


def _round_up(n, q):
    return (n + q - 1) // q * q


def _tile(n, target, q):
    best = None
    t = q
    while t <= min(n, target):
        if n % t == 0:
            best = t
        t += q
    assert best is not None, (n, target, q)
    return best


def _params(**kw):
    return pltpu.CompilerParams(vmem_limit_bytes=VMEM_LIMIT_MB << 20, **kw)


def _call(body, *, grid, in_specs, out_specs, out_shape, name, args, scratch_shapes=(), carried=(), prefetch=()):
    carried = list(carried)
    n_pf = len(prefetch)

    def launch(fn, in_specs_, out_specs_, out_shape_, scratch_, operands, aliases_):
        if n_pf:
            spec = pltpu.PrefetchScalarGridSpec(num_scalar_prefetch=n_pf, grid=grid, in_specs=in_specs_,
                                                out_specs=out_specs_, scratch_shapes=scratch_)
            return pl.pallas_call(fn, grid_spec=spec, out_shape=out_shape_, input_output_aliases=aliases_,
                                  name=name, compiler_params=_params())(*prefetch, *operands)
        return pl.pallas_call(fn, grid=grid, in_specs=in_specs_, out_specs=out_specs_, out_shape=out_shape_,
                              scratch_shapes=scratch_, input_output_aliases=aliases_, name=name,
                              compiler_params=_params())(*operands)

    if not carried:
        return launch(body, in_specs, out_specs, out_shape, list(scratch_shapes), args, {})
    single = not isinstance(out_shape, (list, tuple))
    out_specs_l = [out_specs] if single else list(out_specs)
    out_shape_l = [out_shape] if single else list(out_shape)
    n_in, n_out, n_scr = len(in_specs), len(out_specs_l), len(scratch_shapes)
    hbm = pl.BlockSpec(memory_space=pl.ANY)
    c_in = [a for cm in carried for a in cm.arrays]
    c_out = [s for cm in carried for s in cm.out_shapes]
    c_scr = []
    aliases = {}
    in_off, out_off = n_pf + n_in, n_out
    for cm in carried:
        c_scr += [pltpu.SemaphoreType.DMA((cm.n_remote,)), pltpu.SemaphoreType.DMA((cm.n_remote,)),
                  pltpu.SemaphoreType.DMA((max(cm.n_local, 1),))]
        for k, v in cm.aliases.items():
            aliases[in_off + k] = out_off + v
        in_off += len(cm.arrays)
        out_off += len(cm.out_shapes)
    steps = 1
    for g in grid:
        steps *= g
    forward_steps = [min(int(cm.forward_at * steps), steps - 1) for cm in carried]

    def wrapped(*refs):
        pf = refs[:n_pf]
        p = n_pf
        ins = refs[p:p + n_in]
        p += n_in
        cins = refs[p:p + len(c_in)]
        p += len(c_in)
        outs = refs[p:p + n_out]
        p += n_out
        couts = refs[p:p + len(c_out)]
        p += len(c_out)
        scr = refs[p:p + n_scr]
        csem = refs[p + n_scr:]
        lin = 0
        for axis, g in enumerate(grid):
            lin = lin * g + pl.program_id(axis)
        views = []
        io = oo = 0
        for j, cm in enumerate(carried):
            views.append((cins[io:io + len(cm.arrays)], couts[oo:oo + len(cm.out_shapes)],
                          csem[3 * j], csem[3 * j + 1], csem[3 * j + 2]))
            io += len(cm.arrays)
            oo += len(cm.out_shapes)

        @pl.when(lin == 0)
        def _():
            for cm, v in zip(carried, views):
                cm.start(*v)

        body(*pf, *ins, *outs, *scr)

        for cm, v, step in zip(carried, views, forward_steps):
            pl.when(lin == step)(functools.partial(cm.forward, *v))

        @pl.when(lin == steps - 1)
        def _():
            for cm, v in zip(carried, views):
                cm.finish(*v)

    res = launch(wrapped, list(in_specs) + [hbm] * len(c_in), out_specs_l + [hbm] * len(c_out),
                 out_shape_l + c_out, list(scratch_shapes) + c_scr, (*args, *c_in), aliases)
    oo = n_out
    for cm in carried:
        cm.results = list(res[oo:oo + len(cm.out_shapes)])
        oo += len(cm.out_shapes)
    return res[0] if single else list(res[:n_out])


def _embed(x, meta, target, g, pad, name, carried=()):
    seq, d = x.shape
    n_meta = meta.shape[0]
    lead = pad + n_meta
    m = lead + seq
    tr = ROW_ALIGN
    lead_blocks = lead // tr
    meta_row = pad - (lead_blocks - 1) * tr
    assert lead % tr == 0 and seq % tr == 0 and 0 <= meta_row and meta_row % 8 == 0

    def body(x_ref, meta_ref, t_ref, g_ref, h_ref, n_ref, tp_ref):
        i = pl.program_id(0)

        @pl.when(i < lead_blocks)
        def _():
            h_ref[...] = jnp.zeros_like(h_ref)
            tp_ref[...] = jnp.zeros_like(tp_ref)

        @pl.when(i == lead_blocks - 1)
        def _():
            h_ref[pl.ds(meta_row, n_meta), :] = meta_ref[...]

        @pl.when(i >= lead_blocks)
        def _():
            h_ref[...] = x_ref[...]
            tp_ref[...] = t_ref[...]

        h = h_ref[...]
        r = lax.rsqrt(jnp.mean(h * h, axis=-1, keepdims=True) + EPS)
        n_ref[...] = (h * r * g_ref[...]).astype(n_ref.dtype)

    tokens = pl.BlockSpec((tr, d), lambda i: (jnp.maximum(i - lead_blocks, 0), 0))
    rows = pl.BlockSpec((tr, d), lambda i: (i, 0))
    return _call(
        body, grid=(m // tr,),
        in_specs=[tokens, pl.BlockSpec((n_meta, d), lambda i: (0, 0)), tokens, pl.BlockSpec((1, d), lambda i: (0, 0))],
        out_specs=[rows, rows, rows],
        out_shape=[jax.ShapeDtypeStruct((m, d), F32), jax.ShapeDtypeStruct((m, d), MXU_DTYPE),
                   jax.ShapeDtypeStruct((m, d), F32)],
        name=name, args=(x, meta, target, g), carried=carried)


def _rmsnorm_bwd_rows(x, g, dy):
    r = lax.rsqrt(jnp.mean(x * x, axis=-1, keepdims=True) + EPS)
    xh = x * r
    dyh = dy * g
    dx = r * (dyh - xh * jnp.mean(dyh * xh, axis=-1, keepdims=True))
    return dx, dy * xh


def _norm_bwd(x, g, dy, scale, name, carried=()):
    m, d = x.shape
    tm = _tile(m, 528, SUBLANE_BF16)

    def body(x_ref, g_ref, dy_ref, dx_ref, dg_ref):
        @pl.when(pl.program_id(0) == 0)
        def _():
            dg_ref[...] = jnp.zeros_like(dg_ref)

        dx, dgr = _rmsnorm_bwd_rows(x_ref[...], g_ref[...], scale * dy_ref[...])
        dx_ref[...] = dx.astype(dx_ref.dtype)
        dg_ref[...] += jnp.sum(dgr, axis=0, keepdims=True)

    return _call(
        body, grid=(m // tm,),
        in_specs=[pl.BlockSpec((tm, d), lambda i: (i, 0)), pl.BlockSpec((1, d), lambda i: (0, 0)),
                  pl.BlockSpec((tm, d), lambda i: (i, 0))],
        out_specs=[pl.BlockSpec((tm, d), lambda i: (i, 0)), pl.BlockSpec((1, d), lambda i: (0, 0))],
        out_shape=[jax.ShapeDtypeStruct((m, d), MXU_DTYPE), jax.ShapeDtypeStruct((1, d), F32)],
        name=name, args=(x, g, dy), carried=carried)


def _dot_nt(a, b):
    return lax.dot_general(a, b, (((1,), (1,)), ((), ())), preferred_element_type=F32)


def _dot_tn(a, b):
    return lax.dot_general(a, b, (((0,), (0,)), ((), ())), preferred_element_type=F32)


def _mm_nt(a, w, name, carried=(), out_dtype=F32):
    m, k = a.shape
    n = w.shape[0]
    tm = _tile(m, 1056, SUBLANE_BF16)
    tn = _tile(n, 512, LANE)

    def body(a_ref, w_ref, o_ref):
        o_ref[...] = _dot_nt(a_ref[...], w_ref[...]).astype(o_ref.dtype)

    return _call(
        body, grid=(m // tm, n // tn),
        in_specs=[pl.BlockSpec((tm, k), lambda i, j: (i, 0)), pl.BlockSpec((tn, k), lambda i, j: (j, 0))],
        out_specs=pl.BlockSpec((tm, tn), lambda i, j: (i, j)),
        out_shape=jax.ShapeDtypeStruct((m, n), out_dtype), name=name, args=(a, w), carried=carried)


def _ffn_up_act(n_act, wu_t, g_act, name, carried=()):
    m, d = n_act.shape
    fp = wu_t.shape[0]
    tm = _tile(m, 1056, SUBLANE_BF16)
    tn = _tile(fp, 512, LANE)

    def body(n_ref, wu_ref, g_ref, u_ref, a_ref):
        u = _dot_nt(n_ref[...], wu_ref[...])
        g = g_ref[...].astype(F32)
        u_ref[...] = u.astype(u_ref.dtype)
        a_ref[...] = (g * jax.nn.sigmoid(g) * u).astype(a_ref.dtype)

    act = pl.BlockSpec((tm, tn), lambda i, j: (i, j))
    return _call(
        body, grid=(m // tm, fp // tn),
        in_specs=[pl.BlockSpec((tm, d), lambda i, j: (i, 0)), pl.BlockSpec((tn, d), lambda i, j: (j, 0)), act],
        out_specs=[act, act],
        out_shape=[jax.ShapeDtypeStruct((m, fp), MXU_DTYPE)] * 2, name=name, args=(n_act, wu_t, g_act), carried=carried)


def _ffn_gate_up(n_act, wg_t, wu_t, name, carried=()):
    m, d = n_act.shape
    fp = wg_t.shape[0]
    tm = _tile(m, 1056, SUBLANE_BF16)
    tn = _tile(fp, 512, LANE)

    def body(n_ref, wg_ref, wu_ref, g_ref, u_ref, a_ref):
        n = n_ref[...]
        g = _dot_nt(n, wg_ref[...])
        u = _dot_nt(n, wu_ref[...])
        g_ref[...] = g.astype(g_ref.dtype)
        u_ref[...] = u.astype(u_ref.dtype)
        a_ref[...] = (g * jax.nn.sigmoid(g) * u).astype(a_ref.dtype)

    act = pl.BlockSpec((tm, tn), lambda i, j: (i, j))
    wsp = pl.BlockSpec((tn, d), lambda i, j: (j, 0))
    return _call(
        body, grid=(m // tm, fp // tn),
        in_specs=[pl.BlockSpec((tm, d), lambda i, j: (i, 0)), wsp, wsp],
        out_specs=[act, act, act],
        out_shape=[jax.ShapeDtypeStruct((m, fp), MXU_DTYPE)] * 3, name=name, args=(n_act, wg_t, wu_t), carried=carried)


def _ffn_hidden_bwd(dfo, wd, g_act, u_act, name, carried=()):
    m, d = dfo.shape
    fp = wd.shape[0]
    tm = _tile(m, 1056, SUBLANE_BF16)
    tn = _tile(fp, 512, LANE)

    def body(df_ref, wd_ref, g_ref, u_ref, dg_ref, du_ref):
        da = _dot_nt(df_ref[...], wd_ref[...])
        g = g_ref[...].astype(F32)
        u = u_ref[...].astype(F32)
        s = jax.nn.sigmoid(g)
        du_ref[...] = (da * (g * s)).astype(du_ref.dtype)
        dg_ref[...] = (da * u * (s * (1.0 + g * (1.0 - s)))).astype(dg_ref.dtype)

    act = pl.BlockSpec((tm, tn), lambda i, j: (i, j))
    return _call(
        body, grid=(m // tm, fp // tn),
        in_specs=[pl.BlockSpec((tm, d), lambda i, j: (i, 0)), pl.BlockSpec((tn, d), lambda i, j: (j, 0)), act, act],
        out_specs=[act, act],
        out_shape=[jax.ShapeDtypeStruct((m, fp), MXU_DTYPE)] * 2, name=name, args=(dfo, wd, g_act, u_act),
        carried=carried)


def _row_groups(n_tiles, max_group, nk):
    gsz = max(q for q in range(1, max_group + 1) if n_tiles % q == 0)

    def epilogue_row(grp, kk, i):
        return grp * gsz + jnp.where(kk == nk - 1, i, 0)

    return gsz, epilogue_row


def _mm_residual_norm(a, w, h, g, scale, next_g, name, carried=()):
    m, k = a.shape
    d = w.shape[1]
    tm = _tile(m, ACC_ROWS, SUBLANE_BF16)
    tk = _tile(k, K_TILE, LANE)
    nk = k // tk
    gsz, epilogue_row = _row_groups(m // tm, ACC_GROUP, nk)

    def body(a_ref, w_ref, h_ref, g_ref, ng_ref, fo_ref, hn_ref, nn_ref, acc_ref):
        kk, i = pl.program_id(1), pl.program_id(2)

        @pl.when(kk == 0)
        def _():
            acc_ref[i] = jnp.zeros((tm, d), F32)

        acc_ref[i] += jnp.dot(a_ref[...], w_ref[...], preferred_element_type=F32)

        @pl.when(kk == nk - 1)
        def _():
            fo = acc_ref[i]
            fo_ref[...] = fo
            r = lax.rsqrt(jnp.mean(fo * fo, axis=-1, keepdims=True) + EPS)
            hn = h_ref[...] + scale * (fo * r * g_ref[...])
            hn_ref[...] = hn
            rn = lax.rsqrt(jnp.mean(hn * hn, axis=-1, keepdims=True) + EPS)
            nn_ref[...] = (hn * rn * ng_ref[...]).astype(nn_ref.dtype)

    row = pl.BlockSpec((tm, d), lambda grp, kk, i: (epilogue_row(grp, kk, i), 0))
    row_once = pl.BlockSpec((tm, d), lambda grp, kk, i: (epilogue_row(grp, kk, i), 0), pipeline_mode=pl.Buffered(1))
    vec = pl.BlockSpec((1, d), lambda grp, kk, i: (0, 0))
    return _call(
        body, grid=(m // tm // gsz, nk, gsz),
        in_specs=[pl.BlockSpec((tm, tk), lambda grp, kk, i: (grp * gsz + i, kk)),
                  pl.BlockSpec((tk, d), lambda grp, kk, i: (kk, 0)), row_once, vec, vec],
        out_specs=[row, row, row],
        out_shape=[jax.ShapeDtypeStruct((m, d), F32)] * 2 + [jax.ShapeDtypeStruct((m, d), MXU_DTYPE)],
        scratch_shapes=[pltpu.VMEM((gsz, tm, d), F32)], name=name, args=(a, w, h, g, next_g), carried=carried)


def _mm_residual_loss(a, w, h, g, scale, target, lead, name, carried=()):
    m, k = a.shape
    d = w.shape[1]
    tm = _tile(m, ACC_ROWS, SUBLANE_BF16)
    tk = _tile(k, K_TILE, LANE)
    nk = k // tk
    gsz, epilogue_row = _row_groups(m // tm, ACC_GROUP, nk)

    def body(a_ref, w_ref, h_ref, g_ref, t_ref, dy_ref, dfo_ref, dg_ref, l_ref, acc_ref):
        grp, kk, i = pl.program_id(0), pl.program_id(1), pl.program_id(2)

        @pl.when(jnp.logical_and(jnp.logical_and(grp == 0, kk == 0), i == 0))
        def _():
            dg_ref[...] = jnp.zeros_like(dg_ref)
            l_ref[...] = jnp.zeros_like(l_ref)

        @pl.when(kk == 0)
        def _():
            acc_ref[i] = jnp.zeros((tm, d), F32)

        acc_ref[i] += jnp.dot(a_ref[...], w_ref[...], preferred_element_type=F32)

        @pl.when(kk == nk - 1)
        def _():
            fo = acc_ref[i]
            gain = g_ref[...]
            r = lax.rsqrt(jnp.mean(fo * fo, axis=-1, keepdims=True) + EPS)
            xh = fo * r
            y = h_ref[...] + scale * (xh * gain)
            row = (grp * gsz + i) * tm + lax.broadcasted_iota(jnp.int32, (tm, 1), 0)
            e = jnp.where(row >= lead, y - t_ref[...], 0.0)
            dy = e * (1.0 / d)
            dy_ref[...] = dy
            l_ref[...] += 0.5 * jnp.sum(jnp.sum(e * e, axis=-1, keepdims=True) * (1.0 / d), axis=0, keepdims=True)
            dn = scale * dy
            dyh = dn * gain
            dfo_ref[...] = (r * (dyh - xh * jnp.mean(dyh * xh, axis=-1, keepdims=True))).astype(dfo_ref.dtype)
            dg_ref[...] += jnp.sum(dn * xh, axis=0, keepdims=True)

    row = pl.BlockSpec((tm, d), lambda grp, kk, i: (epilogue_row(grp, kk, i), 0))
    row_once = pl.BlockSpec((tm, d), lambda grp, kk, i: (epilogue_row(grp, kk, i), 0), pipeline_mode=pl.Buffered(1))
    vec = pl.BlockSpec((1, d), lambda grp, kk, i: (0, 0))
    return _call(
        body, grid=(m // tm // gsz, nk, gsz),
        in_specs=[pl.BlockSpec((tm, tk), lambda grp, kk, i: (grp * gsz + i, kk)),
                  pl.BlockSpec((tk, d), lambda grp, kk, i: (kk, 0)), row_once, vec, row_once],
        out_specs=[row, row, vec, pl.BlockSpec((1, 1), lambda grp, kk, i: (0, 0))],
        out_shape=[jax.ShapeDtypeStruct((m, d), F32), jax.ShapeDtypeStruct((m, d), MXU_DTYPE),
                   jax.ShapeDtypeStruct((1, d), F32), jax.ShapeDtypeStruct((1, 1), F32)],
        scratch_shapes=[pltpu.VMEM((gsz, tm, d), F32)], name=name, args=(a, w, h, g, target), carried=carried)


def _norm_bwd_row_tile(m):
    return _tile(m, ACC_ROWS, SUBLANE_BF16)


def _mm_norm_bwd(pairs, h, g, dh_up, name, carried=(), row_tiles=None, dg_init=None, post=None):
    n_pairs = len(pairs)
    m, k = pairs[0][0].shape
    d = h.shape[1]
    tm = _norm_bwd_row_tile(m)
    tk = _tile(k, K_TILE, LANE)
    nk = k // tk
    t0, nt = row_tiles if row_tiles is not None else (0, m // tm)
    gsz, epilogue_row = _row_groups(nt, ACC_GROUP, nk)
    if dg_init is None:
        dg_init = jnp.zeros((1, d), F32)

    n_post = 0 if post is None else 2

    def body(*refs):
        ops = refs[:2 * n_pairs]
        h_ref, g_ref, up_ref, init_ref = refs[2 * n_pairs:2 * n_pairs + 4]
        post_in = refs[2 * n_pairs + 4:2 * n_pairs + 4 + n_post]
        dh_ref, dg_ref = refs[2 * n_pairs + 4 + n_post:2 * n_pairs + 6 + n_post]
        post_out = refs[2 * n_pairs + 6 + n_post:2 * n_pairs + 6 + 2 * n_post]
        acc_ref = refs[-1]
        grp, kk, i = pl.program_id(0), pl.program_id(1), pl.program_id(2)

        @pl.when(jnp.logical_and(jnp.logical_and(grp == 0, kk == 0), i == 0))
        def _():
            dg_ref[...] = init_ref[...]
            if post is not None:
                post_out[1][...] = jnp.zeros_like(post_out[1])

        @pl.when(kk == 0)
        def _():
            acc_ref[i] = jnp.zeros((tm, d), F32)

        for p in range(n_pairs):
            acc_ref[i] += jnp.dot(ops[2 * p][...], ops[2 * p + 1][...], preferred_element_type=F32)

        @pl.when(kk == nk - 1)
        def _():
            dx, dgr = _rmsnorm_bwd_rows(h_ref[...], g_ref[...], acc_ref[i])
            dh = up_ref[...] + dx
            dh_ref[...] = dh
            dg_ref[...] += jnp.sum(dgr, axis=0, keepdims=True)
            if post is not None:
                dfo, dpr = _rmsnorm_bwd_rows(post_in[0][...], post_in[1][...], post[2] * dh)
                post_out[0][...] = dfo.astype(post_out[0].dtype)
                post_out[1][...] += jnp.sum(dpr, axis=0, keepdims=True)

    row_in = pl.BlockSpec((tm, d), lambda grp, kk, i: (t0 + epilogue_row(grp, kk, i), 0))
    row_out = pl.BlockSpec((tm, d), lambda grp, kk, i: (epilogue_row(grp, kk, i), 0))
    vec = pl.BlockSpec((1, d), lambda grp, kk, i: (0, 0))
    in_specs = []
    args = []
    for a, w in pairs:
        in_specs += [pl.BlockSpec((tm, tk), lambda grp, kk, i: (t0 + grp * gsz + i, kk)),
                     pl.BlockSpec((tk, d), lambda grp, kk, i: (kk, 0))]
        args += [a, w]
    in_specs += [row_in, vec, row_in, vec]
    args += [h, g, dh_up, dg_init]
    out_specs = [row_out, vec]
    out_shape = [jax.ShapeDtypeStruct((nt * tm, d), F32), jax.ShapeDtypeStruct((1, d), F32)]
    if post is not None:
        in_specs += [row_in, vec]
        args += [post[0], post[1]]
        out_specs += [row_out, vec]
        out_shape += [jax.ShapeDtypeStruct((nt * tm, d), MXU_DTYPE), jax.ShapeDtypeStruct((1, d), F32)]
    return _call(
        body, grid=(nt // gsz, nk, gsz), in_specs=in_specs, out_specs=out_specs, out_shape=out_shape,
        scratch_shapes=[pltpu.VMEM((gsz, tm, d), F32)], name=name, args=tuple(args), carried=carried)


def _mm_tn(a, b, name, carried=()):
    m, ka = a.shape
    d = b.shape[1]
    tf = _tile(ka, 512, LANE)

    def body(a_ref, b_ref, o_ref):
        o_ref[...] = _dot_tn(a_ref[...], b_ref[...]).astype(o_ref.dtype)

    return _call(
        body, grid=(ka // tf,),
        in_specs=[pl.BlockSpec((m, tf), lambda j: (0, j)),
                  pl.BlockSpec((m, d), lambda j: (0, 0), pipeline_mode=pl.Buffered(1))],
        out_specs=pl.BlockSpec((tf, d), lambda j: (j, 0)),
        out_shape=jax.ShapeDtypeStruct((ka, d), WIRE_DTYPE), name=name, args=(a, b), carried=carried)


GELU_K = 0.7978845608028654
GELU_C = 0.044715


def _expm1(x):
    series = x * (1.0 + x * (1.0 / 2 + x * (1.0 / 6 + x * (1.0 / 24 + x * (1.0 / 120)))))
    return jnp.where(jnp.abs(x) < 0.1, series, jnp.exp(x) - 1.0)


def _softplus(x):
    return jnp.maximum(x, 0.0) + jnp.log1p(jnp.exp(-jnp.abs(x)))


def _block_mm(v, w_ref, transposed):
    nbk = w_ref.shape[0]
    outs = []
    for j in range(nbk):
        vj = v[:, j * BD:(j + 1) * BD]
        outs.append(_dot_nt(vj, w_ref[j]) if transposed else jnp.dot(vj, w_ref[j], preferred_element_type=F32))
    return outs[0] if nbk == 1 else jnp.concatenate(outs, axis=1)


def _group_mean(q, gm_ref):
    hi = q.astype(MXU_DTYPE)
    lo = (q - hi.astype(F32)).astype(MXU_DTYPE)
    nbk = q.shape[1] // BD
    gm = gm_ref[...]
    outs = []
    for j in range(nbk):
        sl = slice(j * BD, (j + 1) * BD)
        outs.append(jnp.dot(hi[:, sl], gm, preferred_element_type=F32) + jnp.dot(lo[:, sl], gm, preferred_element_type=F32))
    return outs[0] if nbk == 1 else jnp.concatenate(outs, axis=1)


class _RowReader:
    def __init__(self, ref):
        self.ref = ref

    def __getitem__(self, rows):
        return self.ref[rows, :]


def _shifted(ext_ref, cur, before8, after8, downs=(), ups=()):
    r = cur.shape[0]
    if downs:
        ext_ref[0:8, :] = before8
    ext_ref[8:8 + r, :] = cur
    if ups:
        ext_ref[8 + r:16 + r, :] = after8
    return [ext_ref[pl.ds(8 - j, r), :] for j in downs] + [ext_ref[pl.ds(8 + j, r), :] for j in ups]


def _lru_gates(xc, pv, wa_ref, wx_ref):
    xcb = xc.astype(MXU_DTYPE)
    ga = jax.nn.sigmoid(_block_mm(xcb, wa_ref, False) + pv[5:6])
    gx = jax.nn.sigmoid(_block_mm(xcb, wx_ref, False) + pv[6:7])
    sp = _softplus(-pv[7:8])
    log_a = -LRU_C * ga * sp
    a = jnp.exp(log_a)
    e2 = _expm1(2.0 * log_a)
    mult = jnp.sqrt(-e2)
    return xcb, ga, gx, sp, a, e2, mult


def _gelu_parts(y):
    th = jnp.tanh(GELU_K * (y + GELU_C * y * y * y))
    return 0.5 * y * (1.0 + th), th


def _mixer_fwd(z, pv, wa, wx, gm, pad, name, carried=()):
    m = z.shape[0]
    c = pv.shape[1]
    r = MIX_ROWS
    nb = m // r

    def body(z_ref, pv_ref, wa_ref, wx_ref, gm_ref, mixed_ref, hs_ref, ext_ref, tailx_ref, tailc_ref, carry_ref):
        b = pl.program_id(0)

        @pl.when(b == 0)
        def _():
            tailx_ref[...] = jnp.zeros_like(tailx_ref)
            tailc_ref[...] = jnp.zeros_like(tailc_ref)
            carry_ref[...] = jnp.zeros_like(carry_ref)

        pv = _RowReader(pv_ref)
        row = b * r + lax.broadcasted_iota(jnp.int32, (r, 1), 0)
        lrow = lax.broadcasted_iota(jnp.int32, (r, c), 0)
        maskf = (row >= pad).astype(F32)
        y = z_ref[:, 0:c]
        xl = z_ref[:, c:2 * c]
        bs = z_ref[:, 2 * c:3 * c]
        cv = z_ref[:, 3 * c:4 * c] * z_ref[:, 4 * c:5 * c]

        x1, x2, x3 = _shifted(ext_ref, xl, tailx_ref[...], None, downs=(1, 2, 3))
        tailx_ref[...] = z_ref[pl.ds(r - 8, 8), c:2 * c]
        xc = pv[4:5] + pv[3:4] * xl + pv[2:3] * x1 + pv[1:2] * x2 + pv[0:1] * x3
        _, _, gx, _, a, _, mult = _lru_gates(xc, pv, wa_ref, wx_ref)
        uu = mult * (gx * xc) * maskf

        acc_a = a
        acc_h = uu
        dlt = 1
        while dlt < r:
            keep = lrow >= dlt
            sh_a = pltpu.roll(acc_a, dlt, axis=0)
            sh_h = pltpu.roll(acc_h, dlt, axis=0)
            acc_h = acc_h + acc_a * jnp.where(keep, sh_h, 0.0)
            acc_a = acc_a * jnp.where(keep, sh_a, 1.0)
            dlt *= 2
        hs = acc_h + acc_a * carry_ref[...]
        hs_ref[...] = hs
        carry_ref[...] = hs_ref[pl.ds(r - 1, 1), :]

        gelu_y, _ = _gelu_parts(y)
        lru_out = hs * gelu_y
        c1, c2 = _shifted(ext_ref, cv, tailc_ref[...], None, downs=(1, 2))
        tailc_ref[...] = cv[r - 8:r]
        sc_out = bs * (pv[10:11] * cv + pv[9:10] * c1 + pv[8:9] * c2)

        rl = lax.rsqrt(_group_mean(lru_out * lru_out, gm_ref) + EPS)
        rs = lax.rsqrt(_group_mean(sc_out * sc_out, gm_ref) + EPS)
        mixed_ref[:, 0:c] = (lru_out * rl * pv[11:12]).astype(mixed_ref.dtype)
        mixed_ref[:, c:2 * c] = (sc_out * rs * pv[12:13]).astype(mixed_ref.dtype)

    full = lambda shape: pl.BlockSpec(shape, lambda b: (0,) * len(shape))
    return _call(
        body, grid=(nb,),
        in_specs=[pl.BlockSpec((r, 5 * c), lambda b: (b, 0)), full(pv.shape), full(wa.shape), full(wx.shape), full(gm.shape)],
        out_specs=[pl.BlockSpec((r, 2 * c), lambda b: (b, 0)), pl.BlockSpec((r, c), lambda b: (b, 0))],
        out_shape=[jax.ShapeDtypeStruct((m, 2 * c), MXU_DTYPE), jax.ShapeDtypeStruct((m, c), F32)],
        scratch_shapes=[pltpu.VMEM((r + 16, c), F32), pltpu.VMEM((8, c), F32), pltpu.VMEM((8, c), F32),
                        pltpu.VMEM((1, c), F32)],
        name=name, args=(z, pv, wa, wx, gm), carried=carried)


def _mixer_bwd(z, hs, dmixed, pv, wa, wx, gm, pad, name, carried=()):
    m = z.shape[0]
    c = pv.shape[1]
    r = MIX_ROWS
    nb = m // r
    r8 = r // 8

    def body(z_ref, zp_ref, hs_ref, hsp_ref, dm_ref, pv_ref, wa_ref, wx_ref, gm_ref,
             dz_ref, dpv_ref, dwa_ref, dwx_ref, ext_ref, hxc_ref, hsc_ref, hp_ref):
        i = pl.program_id(0)
        b = nb - 1 - i

        @pl.when(i == 0)
        def _():
            hxc_ref[...] = jnp.zeros_like(hxc_ref)
            hsc_ref[...] = jnp.zeros_like(hsc_ref)
            hp_ref[...] = jnp.zeros_like(hp_ref)
            dpv_ref[...] = jnp.zeros_like(dpv_ref)
            dwa_ref[...] = jnp.zeros_like(dwa_ref)
            dwx_ref[...] = jnp.zeros_like(dwx_ref)

        pv = _RowReader(pv_ref)
        row = b * r + lax.broadcasted_iota(jnp.int32, (r, 1), 0)
        lrow = lax.broadcasted_iota(jnp.int32, (r, c), 0)
        maskf = (row >= pad).astype(F32)
        has_prev = (b > 0).astype(F32)
        y = z_ref[:, 0:c]
        xl = z_ref[:, c:2 * c]
        bs = z_ref[:, 2 * c:3 * c]
        cs = z_ref[:, 3 * c:4 * c]
        vs = z_ref[:, 4 * c:5 * c]
        cv = cs * vs
        xl_prev = zp_ref[:, c:2 * c] * has_prev
        cv_prev = zp_ref[:, 3 * c:4 * c] * zp_ref[:, 4 * c:5 * c] * has_prev
        hs = hs_ref[...]

        x1, x2, x3 = _shifted(ext_ref, xl, xl_prev, None, downs=(1, 2, 3))
        xc = pv[4:5] + pv[3:4] * xl + pv[2:3] * x1 + pv[1:2] * x2 + pv[0:1] * x3
        xcb, ga, gx, sp, a, e2, mult = _lru_gates(xc, pv, wa_ref, wx_ref)
        gxx = gx * xc
        gelu_y, th = _gelu_parts(y)
        lru_out = hs * gelu_y
        c1, c2 = _shifted(ext_ref, cv, cv_prev, None, downs=(1, 2))
        sc = pv[10:11] * cv + pv[9:10] * c1 + pv[8:9] * c2
        sc_out = bs * sc

        def group_norm_bwd(v, dm, gain):
            rr = lax.rsqrt(_group_mean(v * v, gm_ref) + EPS)
            vh = v * rr
            dvh = dm * gain
            dv = rr * (dvh - vh * _group_mean(dvh * vh, gm_ref))
            return dv, jnp.sum(dm * vh, axis=0, keepdims=True)

        d_lru_out, d_og = group_norm_bwd(lru_out, dm_ref[:, 0:c], pv[11:12])
        d_sc_out, d_sg = group_norm_bwd(sc_out, dm_ref[:, c:2 * c], pv[12:13])
        dpv_ref[11:12, :] += d_og
        dpv_ref[12:13, :] += d_sg

        dhs = d_lru_out * gelu_y
        dgelu = 0.5 * (1.0 + th) + 0.5 * y * (1.0 - th * th) * GELU_K * (1.0 + 3.0 * GELU_C * y * y)
        dy = d_lru_out * hs * dgelu

        acc_a = a
        acc_p = a * dhs
        dlt = 1
        while dlt < r:
            keep = lrow < r - dlt
            sh_a = pltpu.roll(acc_a, r - dlt, axis=0)
            sh_p = pltpu.roll(acc_p, r - dlt, axis=0)
            acc_p = acc_p + acc_a * jnp.where(keep, sh_p, 0.0)
            acc_a = acc_a * jnp.where(keep, sh_a, 1.0)
            dlt *= 2
        p_all = acc_p + acc_a * hp_ref[0:1, :]
        (p_next,) = _shifted(ext_ref, p_all, None, hp_ref[...], ups=(1,))
        hp_ref[...] = p_all[0:8]
        q = dhs + p_next
        (hs_prev,) = _shifted(ext_ref, hs, hsp_ref[...] * has_prev, None, downs=(1,))
        duu = q * maskf
        da = q * hs_prev

        dmult = duu * gxx
        dgxx = duu * mult
        dgx = dgxx * xc
        dxc = dgxx * gx
        dlog_a = da * a - dmult * ((1.0 + e2) / mult)
        dga = dlog_a * (-LRU_C * sp)
        dsp = jnp.sum(dlog_a * (-LRU_C * ga), axis=0, keepdims=True)
        dpv_ref[7:8, :] += dsp * (-jax.nn.sigmoid(-pv[7:8]))
        dga_pre = dga * ga * (1.0 - ga)
        dgx_pre = dgx * gx * (1.0 - gx)
        dpv_ref[5:6, :] += jnp.sum(dga_pre, axis=0, keepdims=True)
        dpv_ref[6:7, :] += jnp.sum(dgx_pre, axis=0, keepdims=True)
        dga_b = dga_pre.astype(MXU_DTYPE)
        dgx_b = dgx_pre.astype(MXU_DTYPE)
        dxc = dxc + _block_mm(dga_b, wa_ref, True) + _block_mm(dgx_b, wx_ref, True)
        for j in range(c // BD):
            sl = slice(j * BD, (j + 1) * BD)
            dwa_ref[j] += _dot_tn(xcb[:, sl], dga_b[:, sl])
            dwx_ref[j] += _dot_tn(xcb[:, sl], dgx_b[:, sl])

        dpv_ref[4:5, :] += jnp.sum(dxc, axis=0, keepdims=True)
        dpv_ref[3:4, :] += jnp.sum(dxc * xl, axis=0, keepdims=True)
        dpv_ref[2:3, :] += jnp.sum(dxc * x1, axis=0, keepdims=True)
        dpv_ref[1:2, :] += jnp.sum(dxc * x2, axis=0, keepdims=True)
        dpv_ref[0:1, :] += jnp.sum(dxc * x3, axis=0, keepdims=True)
        u1, u2, u3 = _shifted(ext_ref, dxc, None, hxc_ref[...], ups=(1, 2, 3))
        hxc_ref[...] = dxc[0:8]
        dxl = pv[3:4] * dxc + pv[2:3] * u1 + pv[1:2] * u2 + pv[0:1] * u3

        dbs = d_sc_out * sc
        dsc = d_sc_out * bs
        dpv_ref[10:11, :] += jnp.sum(dsc * cv, axis=0, keepdims=True)
        dpv_ref[9:10, :] += jnp.sum(dsc * c1, axis=0, keepdims=True)
        dpv_ref[8:9, :] += jnp.sum(dsc * c2, axis=0, keepdims=True)
        s1, s2 = _shifted(ext_ref, dsc, None, hsc_ref[...], ups=(1, 2))
        hsc_ref[...] = dsc[0:8]
        dcv = pv[10:11] * dsc + pv[9:10] * s1 + pv[8:9] * s2

        dz_ref[:, 0:c] = (dy * maskf).astype(dz_ref.dtype)
        dz_ref[:, c:2 * c] = (dxl * maskf).astype(dz_ref.dtype)
        dz_ref[:, 2 * c:3 * c] = (dbs * maskf).astype(dz_ref.dtype)
        dz_ref[:, 3 * c:4 * c] = (dcv * vs * maskf).astype(dz_ref.dtype)
        dz_ref[:, 4 * c:5 * c] = (dcv * cs * maskf).astype(dz_ref.dtype)

    full = lambda shape: pl.BlockSpec(shape, lambda i: (0,) * len(shape))
    cur = lambda width: pl.BlockSpec((r, width), lambda i: (nb - 1 - i, 0))
    prev8 = lambda width: pl.BlockSpec((8, width), lambda i: (jnp.maximum((nb - 1 - i) * r8 - 1, 0), 0))
    return _call(
        body, grid=(nb,),
        in_specs=[cur(5 * c), prev8(5 * c), cur(c), prev8(c), cur(2 * c),
                  full(pv.shape), full(wa.shape), full(wx.shape), full(gm.shape)],
        out_specs=[cur(5 * c), full(pv.shape), full(wa.shape), full(wx.shape)],
        out_shape=[jax.ShapeDtypeStruct((m, 5 * c), MXU_DTYPE), jax.ShapeDtypeStruct(pv.shape, F32),
                   jax.ShapeDtypeStruct(wa.shape, F32), jax.ShapeDtypeStruct(wx.shape, F32)],
        scratch_shapes=[pltpu.VMEM((r + 16, c), F32), pltpu.VMEM((8, c), F32), pltpu.VMEM((8, c), F32),
                        pltpu.VMEM((8, c), F32)],
        name=name, args=(z, z, hs, hs, dmixed, pv, wa, wx, gm), carried=carried)


def _position():
    return lax.axis_index("x"), lax.axis_index("y"), lax.axis_index("c")


def _block_of(px, py, pc):
    return 4 * px + 2 * py + pc


class _TwoLevelGather:
    def __init__(self, n_arrays, rows_of, src_of, send_sems, recv_sems):
        x, y, c = _position()
        self.n, self.rows_of, self.src_of = n_arrays, rows_of, src_of
        self.send_sems, self.recv_sems = send_sems, recv_sems
        self.c, self.me, self.sibling = c, (x, y, c), (x, y, 1 - c)
        self.chips = [(1 - x, y), (x, 1 - y), (1 - x, 1 - y)]

    def _copy(self, i, k, block, to, src=None):
        return pltpu.make_async_remote_copy(
            src_ref=self.rows_of(i, *block) if src is None else src, dst_ref=self.rows_of(i, *block),
            send_sem=self.send_sems.at[7 * i + k], recv_sem=self.recv_sems.at[7 * i + k],
            device_id=to, device_id_type=MESH)

    def _first(self, i):
        own = [self._copy(i, 0, self.me, self.sibling, src=self.src_of(i))]
        return own + [self._copy(i, 1 + j, self.me, (*chip, self.c), src=self.src_of(i))
                      for j, chip in enumerate(self.chips)]

    def _passed(self, i, j):
        return self._copy(i, 4 + j, (*self.chips[j], self.c), self.sibling)

    def start(self):
        for i in range(self.n):
            for cp in self._first(i):
                cp.start()

    def forward(self):
        for i in range(self.n):
            for j, chip in enumerate(self.chips):
                self._copy(i, 1 + j, (*chip, self.c), self.me).wait_recv()
                self._passed(i, j).start()

    def drain(self):
        for i in range(self.n):
            self._copy(i, 0, self.sibling, self.me).wait_recv()
            for j, chip in enumerate(self.chips):
                self._copy(i, 4 + j, (*chip, 1 - self.c), self.me).wait_recv()
        for i in range(self.n):
            for cp in self._first(i) + [self._passed(i, j) for j in range(3)]:
                cp.wait_send()


class _CarriedGather:
    def __init__(self, shards, padded_rows, zeros, forward_at, part=None, into=None):
        d = shards[0].shape[1]
        self.forward_at = forward_at
        self.n = len(shards)
        self.rows = [s.shape[0] for s in shards]
        self.pads = [p - N_DEV * r for r, p in zip(self.rows, padded_rows)]
        assert max(self.pads) <= zeros.shape[0] and zeros.shape[1] == d
        self.part = part if part is not None else (0, self.rows[0])
        assert (part is None and into is None) or self.n == 1
        assert self.part[0] % SUBLANE_BF16 == 0 and self.part[1] % SUBLANE_BF16 == 0
        self.arrays = list(shards) + [zeros] + ([into] if into is not None else [])
        self.out_shapes = [jax.ShapeDtypeStruct((p, d), s.dtype) for s, p in zip(shards, padded_rows)]
        self.aliases = {self.n + 1: 0} if into is not None else {}
        if into is not None:
            self.pads = [0] * self.n
        self.n_remote, self.n_local = 7 * self.n, 2 * self.n
        self.results = None

    def _rows_of(self, outs):
        def rows_of(i, px, py, pc):
            first = _block_of(px, py, pc) * self.rows[i] + (self.part[0] if self.n == 1 else 0)
            count = self.part[1] if self.n == 1 else self.rows[i]
            return outs[i].at[pl.ds(pl.multiple_of(first, SUBLANE_BF16), count), :]
        return rows_of

    def _own(self, ins, i):
        return ins[i].at[pl.ds(self.part[0], self.part[1]), :] if self.n == 1 else ins[i]

    def _gather(self, ins, outs, send_sems, recv_sems):
        return _TwoLevelGather(self.n, self._rows_of(outs), functools.partial(self._own, ins), send_sems, recv_sems)

    def _local(self, ins, outs, local_sems):
        x, y, c = _position()
        rows_of = self._rows_of(outs)
        cps = []
        for i in range(self.n):
            cps.append(pltpu.make_async_copy(self._own(ins, i), rows_of(i, x, y, c), local_sems.at[2 * i]))
            if self.pads[i]:
                cps.append(pltpu.make_async_copy(ins[self.n].at[pl.ds(0, self.pads[i]), :],
                                                 outs[i].at[pl.ds(N_DEV * self.rows[i], self.pads[i]), :],
                                                 local_sems.at[2 * i + 1]))
        return cps

    def start(self, ins, outs, send_sems, recv_sems, local_sems):
        for cp in self._local(ins, outs, local_sems):
            cp.start()
        self._gather(ins, outs, send_sems, recv_sems).start()

    def forward(self, ins, outs, send_sems, recv_sems, local_sems):
        self._gather(ins, outs, send_sems, recv_sems).forward()

    def finish(self, ins, outs, send_sems, recv_sems, local_sems):
        self._gather(ins, outs, send_sems, recv_sems).drain()
        for cp in self._local(ins, outs, local_sems):
            cp.wait()


class _CarriedSwap:
    def __init__(self, grads, shard_rows):
        d = grads[0].shape[1]
        self.n, self.rows = len(grads), list(shard_rows)
        self.arrays = list(grads)
        self.out_shapes = [jax.ShapeDtypeStruct((4, s, d), g.dtype) for g, s in zip(grads, shard_rows)]
        self.aliases = {}
        self.n_remote, self.n_local = 4 * self.n, 0
        self.forward_at = 1.0
        self.results = None

    def _copies(self, ins, outs, send_sems, recv_sems):
        x, y, c = _position()
        cps = []
        for i in range(self.n):
            s = self.rows[i]
            for k in range(4):
                blk = _block_of(k >> 1, k & 1, 1 - c)
                cps.append(pltpu.make_async_remote_copy(
                    src_ref=ins[i].at[pl.ds(pl.multiple_of(blk * s, SUBLANE_BF16), s), :], dst_ref=outs[i].at[k],
                    send_sem=send_sems.at[4 * i + k], recv_sem=recv_sems.at[4 * i + k],
                    device_id=(x, y, 1 - c), device_id_type=MESH))
        return cps

    def start(self, ins, outs, send_sems, recv_sems, local_sems):
        for cp in self._copies(ins, outs, send_sems, recv_sems):
            cp.start()

    def forward(self, *_):
        pass

    def finish(self, ins, outs, send_sems, recv_sems, local_sems):
        for cp in self._copies(ins, outs, send_sems, recv_sems):
            cp.wait()


class _CarriedChipExchange:
    def __init__(self, presums, part=None, into=None):
        self.n = len(presums)
        assert (part is None and into is None) or self.n == 1
        self.part = part if part is not None else (0, presums[0].shape[1])
        assert self.part[0] % SUBLANE_BF16 == 0 and self.part[1] % SUBLANE_BF16 == 0
        self.arrays = list(presums) + ([into] if into is not None else [])
        self.out_shapes = [jax.ShapeDtypeStruct(p.shape, p.dtype) for p in presums]
        self.aliases = {self.n: 0} if into is not None else {}
        self.n_remote, self.n_local = 3 * self.n, 0
        self.forward_at = 1.0
        self.results = None

    def _copies(self, ins, outs, send_sems, recv_sems):
        x, y, c = _position()
        cps = []
        for i in range(self.n):
            rows = pl.ds(*self.part) if self.n == 1 else pl.ds(0, self.arrays[i].shape[1])
            for r in range(1, 4):
                cps.append(pltpu.make_async_remote_copy(
                    src_ref=ins[i].at[r - 1, rows, :], dst_ref=outs[i].at[r - 1, rows, :],
                    send_sem=send_sems.at[3 * i + r - 1], recv_sem=recv_sems.at[3 * i + r - 1],
                    device_id=(x ^ (r >> 1), y ^ (r & 1), c), device_id_type=MESH))
        return cps

    def start(self, ins, outs, send_sems, recv_sems, local_sems):
        for cp in self._copies(ins, outs, send_sems, recv_sems):
            cp.start()

    def forward(self, *_):
        pass

    def finish(self, ins, outs, send_sems, recv_sems, local_sems):
        for cp in self._copies(ins, outs, send_sems, recv_sems):
            cp.wait()


def _gather_small(block, reduce, name):
    rr, nn = block.shape

    def body(x_ref, out_ref, *rest):
        if reduce:
            stack_ref, send_sems, recv_sems, local_sem = rest
        else:
            send_sems, recv_sems, local_sem = rest
            stack_ref = out_ref
        x, y, c = _position()

        def rows_of(i, px, py, pc):
            return stack_ref.at[pl.ds(pl.multiple_of(_block_of(px, py, pc) * rr, 8), rr), :]

        own = pltpu.make_async_copy(x_ref, rows_of(0, x, y, c), local_sem)
        own.start()
        gather = _TwoLevelGather(1, rows_of, lambda i: x_ref, send_sems, recv_sems)
        gather.start()
        gather.forward()
        gather.drain()
        own.wait()
        if reduce:
            acc = stack_ref[0:rr, :]
            for k in range(1, N_DEV):
                acc = acc + stack_ref[k * rr:(k + 1) * rr, :]
            out_ref[...] = acc

    vmem = pl.BlockSpec(memory_space=pltpu.VMEM)
    scratch = [pltpu.SemaphoreType.DMA((7,)), pltpu.SemaphoreType.DMA((7,)), pltpu.SemaphoreType.DMA]
    if reduce:
        scratch = [pltpu.VMEM((N_DEV * rr, nn), F32)] + scratch
    out_rows = rr if reduce else N_DEV * rr
    return pl.pallas_call(
        body, in_specs=[vmem], out_specs=vmem, out_shape=jax.ShapeDtypeStruct((out_rows, nn), F32),
        scratch_shapes=scratch, name=name, compiler_params=_params())(block)


def _sum_stack(stack, name):
    rr = stack.shape[0] // N_DEV

    def body(s_ref, o_ref):
        acc = s_ref[0:rr, :]
        for k in range(1, N_DEV):
            acc = acc + s_ref[k * rr:(k + 1) * rr, :]
        o_ref[...] = acc

    vmem = pl.BlockSpec(memory_space=pltpu.VMEM)
    return pl.pallas_call(body, in_specs=[vmem], out_specs=vmem,
                          out_shape=jax.ShapeDtypeStruct((rr, stack.shape[1]), F32), name=name,
                          compiler_params=_params())(stack)


def _presum(where, grad, swapped, name):
    s, d = swapped.shape[1], swapped.shape[2]
    tc = _tile(d, 512, LANE)

    def body(where_ref, g_ref, sw_ref, o_ref):
        o_ref[0] = (g_ref[...].astype(F32) + sw_ref[0].astype(F32)).astype(o_ref.dtype)

    return _call(
        body, grid=(3, d // tc),
        in_specs=[pl.BlockSpec((s, tc), lambda r, j, where: (where[1 + r], j)),
                  pl.BlockSpec((1, s, tc), lambda r, j, where: (where[5 + r], 0, j))],
        out_specs=pl.BlockSpec((1, s, tc), lambda r, j, where: (r, 0, j)),
        out_shape=jax.ShapeDtypeStruct((3, s, d), WIRE_DTYPE), name=name, args=(grad, swapped), prefetch=(where,))


def _final_sum(where, grad, swapped, received, name, carried=()):
    s, d = swapped.shape[1], swapped.shape[2]
    tc = _tile(d, 512, LANE)

    def body(where_ref, g_ref, sw_ref, r_ref, o_ref):
        acc = g_ref[...].astype(F32) + sw_ref[0].astype(F32)
        for k in range(3):
            acc = acc + r_ref[k].astype(F32)
        o_ref[...] = acc

    return _call(
        body, grid=(d // tc,),
        in_specs=[pl.BlockSpec((s, tc), lambda j, where: (where[0], j)),
                  pl.BlockSpec((1, s, tc), lambda j, where: (where[4], 0, j)),
                  pl.BlockSpec((3, s, tc), lambda j, where: (0, 0, j))],
        out_specs=pl.BlockSpec((s, tc), lambda j, where: (0, j)),
        out_shape=jax.ShapeDtypeStruct((s, d), F32), name=name, args=(grad, swapped, received),
        prefetch=(where,), carried=carried)


class _GradReduction:
    def __init__(self, key, grad, shard_rows, where):
        self.key, self.grad, self.rows, self.where = key, grad, shard_rows, where
        self._presum = self._exchange = None

    def swap(self):
        self._swap = _CarriedSwap([self.grad], [self.rows])
        return self._swap

    def exchange(self, part=None):
        if self._presum is None:
            self._presum = _presum(self.where, self.grad, self._swap.results[0], "presum_" + self.key)
        rows = None
        if part is not None:
            half = _round_up(self.rows // 2, SUBLANE_BF16)
            rows = (0, half) if part == 0 else (half, self.rows - half)
        into = self._exchange.results[0] if part == 1 else None
        self._exchange = _CarriedChipExchange([self._presum], rows, into)
        return self._exchange

    def total(self, carried=()):
        return _final_sum(self.where, self.grad, self._swap.results[0], self._exchange.results[0],
                          "sum_" + self.key, carried)

    def total_and_update(self, w, m, v):
        return _sum_adamw(self.where, self.grad, self._swap.results[0], self._exchange.results[0], w, m, v,
                          "update_" + self.key)


def _adamw_math(w, g, m, v):
    nm = ADAM_B1 * m + (1.0 - ADAM_B1) * g
    nv = ADAM_B2 * v + (1.0 - ADAM_B2) * (g * g)
    m_hat = nm / (1.0 - ADAM_B1 ** ADAM_STEP)
    v_hat = nv / (1.0 - ADAM_B2 ** ADAM_STEP)
    return -ADAM_LR * (m_hat / (jnp.sqrt(v_hat) + ADAM_EPS) + ADAM_WD * w), nm, nv


def _sum_adamw(where, grad, swapped, received, w, m, v, name):
    s, d = swapped.shape[1], swapped.shape[2]
    tc = _tile(d, 512, LANE)

    def body(where_ref, g_ref, sw_ref, r_ref, w_ref, m_ref, v_ref, gs_ref, d_ref, nm_ref, nv_ref):
        g = g_ref[...].astype(F32) + sw_ref[0].astype(F32)
        for k in range(3):
            g = g + r_ref[k].astype(F32)
        gs_ref[...] = g
        d_ref[...], nm_ref[...], nv_ref[...] = _adamw_math(w_ref[...], g, m_ref[...], v_ref[...])

    blk = pl.BlockSpec((s, tc), lambda j, where: (0, j))
    return _call(
        body, grid=(d // tc,),
        in_specs=[pl.BlockSpec((s, tc), lambda j, where: (where[0], j)),
                  pl.BlockSpec((1, s, tc), lambda j, where: (where[4], 0, j)),
                  pl.BlockSpec((3, s, tc), lambda j, where: (0, 0, j)), blk, blk, blk],
        out_specs=[blk] * 4, out_shape=[jax.ShapeDtypeStruct((s, d), F32)] * 4, name=name,
        args=(grad, swapped, received, w, m, v), prefetch=(where,))


def _adamw(w, g, m, v, name):
    rows, cols = w.shape
    tr = _tile(rows, 256, 8)

    def body(w_ref, g_ref, m_ref, v_ref, d_ref, nm_ref, nv_ref):
        d_ref[...], nm_ref[...], nv_ref[...] = _adamw_math(w_ref[...], g_ref[...], m_ref[...], v_ref[...])

    spec = pl.BlockSpec((tr, cols), lambda i: (i, 0))
    return pl.pallas_call(
        body, grid=(rows // tr,), in_specs=[spec] * 4, out_specs=[spec] * 3,
        out_shape=[jax.ShapeDtypeStruct((rows, cols), F32)] * 3, name=name, compiler_params=_params())(w, g, m, v)


def _pack_rows(arrays, width, row_quantum=8):
    flat = jnp.concatenate([a.reshape(-1) for a in arrays])
    total = _round_up(flat.shape[0], row_quantum * width)
    flat = jnp.pad(flat, (0, total - flat.shape[0]))
    return flat.reshape(-1, width)


def _unpack_rows(packed, shapes):
    flat = packed.reshape(-1)
    out = []
    off = 0
    for shp in shapes:
        size = 1
        for s in shp:
            size *= s
        out.append(flat[off:off + size].reshape(shp))
        off += size
    return out


def _block_diag(w):
    h, hb, _ = w.shape
    per = BD // hb
    w4 = w.reshape(h // per, per, hb, hb)
    eye = jnp.eye(per, dtype=w.dtype)
    return jnp.einsum('npij,pq->npiqj', w4, eye).reshape(h // per, BD, BD)


def _block_diag_extract(bd, hb):
    nbk = bd.shape[0]
    per = BD // hb
    b5 = bd.reshape(nbk, per, hb, per, hb)
    eye = jnp.eye(per, dtype=bd.dtype)
    return jnp.einsum('npiqj,pq->npij', b5, eye).reshape(nbk * per, hb, hb)


def kernel(x, meta_tokens, ffn1_pre_g, ffn1_w_gate, ffn1_w_up, ffn1_w_down, ffn1_post_g, mix_pre_g, w_in, lru_conv_w, lru_conv_b, lru_w_a, lru_b_a, lru_w_x, lru_b_x, lru_lambda, sconv_w, lru_out_g, sconv_out_g, w_out, mix_post_g, ffn2_pre_g, ffn2_w_gate, ffn2_w_up, ffn2_w_down, ffn2_post_g, loss_target, m_meta_tokens, m_ffn1_pre_g, m_ffn1_w_gate, m_ffn1_w_up, m_ffn1_w_down, m_ffn1_post_g, m_mix_pre_g, m_w_in, m_lru_conv_w, m_lru_conv_b, m_lru_w_a, m_lru_b_a, m_lru_w_x, m_lru_b_x, m_lru_lambda, m_sconv_w, m_lru_out_g, m_sconv_out_g, m_w_out, m_mix_post_g, m_ffn2_pre_g, m_ffn2_w_gate, m_ffn2_w_up, m_ffn2_w_down, m_ffn2_post_g, v_meta_tokens, v_ffn1_pre_g, v_ffn1_w_gate, v_ffn1_w_up, v_ffn1_w_down, v_ffn1_post_g, v_mix_pre_g, v_w_in, v_lru_conv_w, v_lru_conv_b, v_lru_w_a, v_lru_b_a, v_lru_w_x, v_lru_b_x, v_lru_lambda, v_sconv_w, v_lru_out_g, v_sconv_out_g, v_w_out, v_mix_post_g, v_ffn2_pre_g, v_ffn2_w_gate, v_ffn2_w_up, v_ffn2_w_down, v_ffn2_post_g):
    given = dict(locals())
    wts = {n: given[n] for n in WEIGHT_NAMES}
    mom = {n: given["m_" + n] for n in WEIGHT_NAMES}
    var = {n: given["v_" + n] for n in WEIGHT_NAMES}

    xi, yi, ci = _position()
    me = _block_of(xi, yi, ci)
    x2 = x[0]
    seq, d = x2.shape
    n_meta = meta_tokens.shape[0]
    m_rows = _round_up(n_meta + seq, ROW_ALIGN)
    pad = m_rows - n_meta - seq
    lead = pad + n_meta
    c = lru_conv_b.shape[1]
    hb = lru_w_a.shape[-1]
    dm = meta_tokens.shape[1]
    cs_ = lru_conv_w.shape[2]
    kw4, kw3 = lru_conv_w.shape[1], sconv_w.shape[1]
    assert d == 2 * c and c % BD == 0 and BD % hb == 0 and cs_ <= dm and kw4 == 4 and kw3 == 3

    small = jnp.zeros((_round_up(n_meta + kw4 + kw3, 8), dm), F32)
    small = small.at[0:n_meta].set(meta_tokens)
    small = small.at[n_meta:n_meta + kw4, 0:cs_].set(lru_conv_w[0])
    small = small.at[n_meta + kw4:n_meta + kw4 + kw3, 0:cs_].set(sconv_w[0])
    sr = small.shape[0]
    small_all = _gather_small(small, False, "gather_small").reshape(N_DEV, sr, dm)
    meta_full = small_all[:, 0:n_meta, :].transpose(1, 0, 2).reshape(n_meta, d)
    conv_w_full = small_all[:, n_meta:n_meta + kw4, 0:cs_].transpose(1, 0, 2).reshape(kw4, c)
    sconv_w_full = small_all[:, n_meta + kw4:n_meta + kw4 + kw3, 0:cs_].transpose(1, 0, 2).reshape(kw3, c)

    big = ['ffn1_w_gate', 'ffn1_w_up', 'ffn1_w_down', 'w_in', 'w_out', 'ffn2_w_gate', 'ffn2_w_up', 'ffn2_w_down']
    col_sharded = {'ffn1_w_gate', 'ffn1_w_up', 'w_in', 'ffn2_w_gate', 'ffn2_w_up'}
    shards = []
    for nme in big:
        w = wts[nme][0].astype(WIRE_DTYPE)
        shards.append(w.T if nme in col_sharded else w)
    shard_rows = dict(zip(big, [s.shape[0] for s in shards]))
    zeros = jnp.zeros((F_ALIGN, d), WIRE_DTYPE)

    def gather(forward_at, *names, part=None, into=None):
        sel = [shards[big.index(nme)] for nme in names]
        padded = [_round_up(N_DEV * shard_rows[nme], LANE if nme in ('w_in', 'w_out') else F_ALIGN) for nme in names]
        return _CarriedGather(sel, padded, zeros, forward_at, part, into)

    pv = jnp.zeros((16, c), F32)
    pv = pv.at[0:4].set(conv_w_full).at[4].set(lru_conv_b[0]).at[5].set(lru_b_a[0]).at[6].set(lru_b_x[0])
    pv = pv.at[7].set(lru_lambda[0]).at[8:11].set(sconv_w_full).at[11].set(lru_out_g[0]).at[12].set(sconv_out_g[0])
    wa_bd = _block_diag(lru_w_a[0]).astype(MXU_DTYPE)
    wx_bd = _block_diag(lru_w_x[0]).astype(MXU_DTYPE)
    gs = c // N_GROUPS
    gidx = jnp.arange(BD) // gs
    gm = jnp.where(gidx[:, None] == gidx[None, :], 1.0 / gs, 0.0).astype(MXU_DTYPE)

    ride = gather(1.0, 'ffn1_w_gate')
    h0, n1, target = _embed(x2, meta_full, loss_target[0], ffn1_pre_g, pad, "embed_prenorm", carried=[ride])
    (wg1,) = ride.results
    ride = gather(1.0, 'ffn1_w_up')
    g1 = _mm_nt(n1, wg1, "ffn1_gate", carried=[ride], out_dtype=MXU_DTYPE)
    (wu1,) = ride.results
    ride = gather(1.0, 'ffn1_w_down')
    u1, a1 = _ffn_up_act(n1, wu1, g1, "ffn1_up_act", carried=[ride])
    (wd1,) = ride.results
    ride = gather(0.9, 'w_in', 'w_out')
    fo1, h1, un = _mm_residual_norm(a1, wd1, h0, ffn1_post_g, 0.5, mix_pre_g, "ffn1_down", carried=[ride])
    win_t, wout = ride.results
    s2 = shard_rows['ffn2_w_gate']
    quarter = _round_up(s2 // 4, SUBLANE_BF16)
    ride_g = gather(1.0, 'ffn2_w_gate', part=(0, 3 * quarter))
    z = _mm_nt(un, win_t, "mix_in_proj", carried=[ride_g])
    ride_g = gather(1.0, 'ffn2_w_gate', part=(3 * quarter, s2 - 3 * quarter), into=ride_g.results[0])
    ride_u = gather(1.0, 'ffn2_w_up', part=(0, quarter))
    mixed, hs = _mixer_fwd(z, pv, wa_bd, wx_bd, gm, pad, "mixer_fwd", carried=[ride_g, ride_u])
    (wg2,) = ride_g.results
    ride_u = gather(1.0, 'ffn2_w_up', part=(quarter, s2 - quarter), into=ride_u.results[0])
    o_mix, h2, n2 = _mm_residual_norm(mixed, wout, h1, mix_post_g, 1.0, ffn2_pre_g, "mix_out_proj", carried=[ride_u])
    (wu2,) = ride_u.results
    ride = gather(0.75, 'ffn2_w_down')
    g2, u2, a2 = _ffn_gate_up(n2, wg2, wu2, "ffn2_gate_up", carried=[ride])
    (wd2,) = ride.results
    dh3, dfo2, d_post2, loss_part = _mm_residual_loss(a2, wd2, h2, ffn2_post_g, 0.5, target, lead, "ffn2_down_loss")
    loss = lax.psum(loss_part[0, 0], ("x", "y", "c"))

    chip_rel = [2 * (xi ^ (r >> 1)) + (yi ^ (r & 1)) for r in range(4)]
    where = jnp.stack([2 * k + ci for k in chip_rel] + chip_rel).astype(jnp.int32)
    red = {}

    def reduction(nme, grad):
        red[nme] = _GradReduction(nme, grad, shard_rows[nme], where)
        return red[nme]

    r_wd2 = reduction('ffn2_w_down', _mm_tn(a2, dfo2, "ffn2_dw_down"))
    dg2, du2 = _ffn_hidden_bwd(dfo2, wd2, g2, u2, "ffn2_hidden_bwd", carried=[r_wd2.swap()])
    r_wg2 = reduction('ffn2_w_gate', _mm_tn(dg2, n2, "ffn2_dw_gate", carried=[r_wd2.exchange(part=0)]))
    r_wu2 = reduction('ffn2_w_up', _mm_tn(du2, n2, "ffn2_dw_up", carried=[r_wd2.exchange(part=1), r_wg2.swap()]))
    dh2, d_pre2 = _mm_norm_bwd([(dg2, wg2), (du2, wu2)], h2, ffn2_pre_g, dh3, "ffn2_dx",
                               carried=[r_wg2.exchange(), r_wu2.swap()])
    do_mix, d_mix_post = _norm_bwd(o_mix, mix_post_g, dh2, 1.0, "mix_postnorm_bwd")
    dmixed = _mm_nt(do_mix, wout, "mix_out_proj_bwd")
    r_wout = reduction('w_out', _mm_tn(mixed, do_mix, "mix_dw_out"))
    dz, dpv, dwa_bd, dwx_bd = _mixer_bwd(z, hs, dmixed, pv, wa_bd, wx_bd, gm, pad, "mixer_bwd",
                                         carried=[r_wu2.exchange(), r_wout.swap()])
    r_win = reduction('w_in', _mm_tn(dz, un, "mix_dw_in", carried=[r_wout.exchange()]))
    dh1, d_mix_pre, dfo1, d_post1 = _mm_norm_bwd([(dz, win_t)], h1, mix_pre_g, dh2, "mix_dx", carried=[r_win.swap()],
                                                 post=(fo1, ffn1_post_g, 0.5))
    r_wd1 = reduction('ffn1_w_down', _mm_tn(a1, dfo1, "ffn1_dw_down", carried=[r_win.exchange()]))
    early_names = ['mix_pre_g', 'mix_post_g', 'ffn2_pre_g', 'ffn2_post_g', 'ffn1_post_g',
                   'lru_conv_b', 'lru_b_a', 'lru_b_x', 'lru_lambda', 'lru_out_g', 'sconv_out_g',
                   'lru_conv_w', 'sconv_w', 'lru_w_a', 'lru_w_x']
    early_parts = [d_mix_pre, d_mix_post, d_pre2, d_post2, d_post1,
                   dpv[4:5], dpv[5:6], dpv[6:7], dpv[7:8], dpv[11:12], dpv[12:13],
                   dpv[0:4], dpv[8:11], _block_diag_extract(dwa_bd, hb), _block_diag_extract(dwx_bd, hb)]
    early_packed = _pack_rows(early_parts, d, SUBLANE_BF16)
    early_ride = _CarriedGather([early_packed], [N_DEV * early_packed.shape[0]], zeros, 0.75)
    dg1, du1 = _ffn_hidden_bwd(dfo1, wd1, g1, u1, "ffn1_hidden_bwd", carried=[r_wd1.swap(), early_ride])
    early_sum = _sum_stack(early_ride.results[0], "sum_small_early")
    r_wg1 = reduction('ffn1_w_gate', _mm_tn(dg1, n1, "ffn1_dw_gate", carried=[r_wd1.exchange(part=0)]))
    r_wu1 = reduction('ffn1_w_up', _mm_tn(du1, n1, "ffn1_dw_up", carried=[r_wd1.exchange(part=1), r_wg1.swap()]))
    row_tile = _norm_bwd_row_tile(m_rows)
    n_tiles = m_rows // row_tile
    half = n_tiles // 2
    assert half >= 1 and half * row_tile >= lead
    dh0_a, d_pre1_a = _mm_norm_bwd([(dg1, wg1), (du1, wu1)], h0, ffn1_pre_g, dh1, "ffn1_dx_a",
                                   carried=[r_wg1.exchange(), r_wu1.swap()], row_tiles=(0, half))
    dh0_b, d_pre1 = _mm_norm_bwd([(dg1, wg1), (du1, wu1)], h0, ffn1_pre_g, dh1, "ffn1_dx_b",
                                 carried=[r_wu1.exchange()], row_tiles=(half, n_tiles - half), dg_init=d_pre1_a)
    grad_x = jnp.concatenate([dh0_a[lead:], dh0_b], axis=0)[None]
    d_meta = dh0_a[pad:lead]

    grads, delta, new_m, new_v = {}, {}, {}, {}
    for nme in big:
        in_shard_layout = nme not in col_sharded or shard_rows[nme] % LANE != 0
        if in_shard_layout:
            view = (lambda t: t[0].T) if nme in col_sharded else (lambda t: t[0])
            back = (lambda t: t.T[None]) if nme in col_sharded else (lambda t: t[None])
            outs = red[nme].total_and_update(view(wts[nme]), view(mom[nme]), view(var[nme]))
            grads[nme], delta[nme], new_m[nme], new_v[nme] = [back(t) for t in outs]
        else:
            grads[nme] = red[nme].total().T[None]
            outs = _adamw(wts[nme][0], grads[nme][0], mom[nme][0], var[nme][0], "adamw_" + nme)
            delta[nme], new_m[nme], new_v[nme] = [t[None] for t in outs]

    late_names = ['ffn1_pre_g', 'meta_tokens']
    late_parts = [d_pre1, d_meta]
    late_sum = _gather_small(_pack_rows(late_parts, d), True, "reduce_small_late")
    small_sums = (_unpack_rows(early_sum, [p.shape for p in early_parts])
                  + _unpack_rows(late_sum, [p.shape for p in late_parts]))
    for nme, gsm in zip(early_names + late_names, small_sums):
        if nme == 'meta_tokens':
            grads[nme] = lax.dynamic_slice_in_dim(gsm, me * dm, dm, axis=1)
        elif nme in ('lru_conv_w', 'sconv_w'):
            grads[nme] = lax.dynamic_slice_in_dim(gsm, me * cs_, cs_, axis=1)[None]
        else:
            grads[nme] = gsm.reshape(wts[nme].shape)

    rest = [n for n in WEIGHT_NAMES if n not in big]
    rest_shapes = [wts[n].shape for n in rest]
    packed = [_pack_rows([src[n] for n in rest], LANE, 256) for src in (wts, grads, mom, var)]
    for out, packed_out in zip((delta, new_m, new_v), _adamw(*packed, "adamw_small")):
        for nme, arr in zip(rest, _unpack_rows(packed_out, rest_shapes)):
            out[nme] = arr

    return (loss, grad_x, *[grads[n] for n in WEIGHT_NAMES], *[delta[n] for n in WEIGHT_NAMES],
            *[new_m[n] for n in WEIGHT_NAMES], *[new_v[n] for n in WEIGHT_NAMES])
```

```python
import functools

import jax
import jax.numpy as jnp
from jax import lax
from jax.experimental import pallas as pl
from jax.experimental.pallas import tpu as pltpu

F32 = jnp.float32
MXU_DTYPE = jnp.bfloat16
WIRE_DTYPE = jnp.bfloat16
MESH = pl.DeviceIdType.MESH

EPS = 1e-6
LRU_C = 8.0
N_GROUPS = 16
ADAM_LR = 0.001
ADAM_B1 = 0.9
ADAM_B2 = 0.999
ADAM_EPS = 1e-08
ADAM_WD = 0.01
ADAM_STEP = 10

N_DEV = 8
LANE = 128
SUBLANE_BF16 = 16
ROW_ALIGN = 128
F_ALIGN = 512
BD = 256
K_TILE = 512
ACC_ROWS = 528
ACC_GROUP = 1
MIX_ROWS = 128
VMEM_LIMIT_MB = 56

WEIGHT_NAMES = ['meta_tokens', 'ffn1_pre_g', 'ffn1_w_gate', 'ffn1_w_up', 'ffn1_w_down', 'ffn1_post_g',
                'mix_pre_g', 'w_in', 'lru_conv_w', 'lru_conv_b', 'lru_w_a', 'lru_b_a', 'lru_w_x', 'lru_b_x',
                'lru_lambda', 'sconv_w', 'lru_out_g', 'sconv_out_g', 'w_out', 'mix_post_g', 'ffn2_pre_g',
                'ffn2_w_gate', 'ffn2_w_up', 'ffn2_w_down', 'ffn2_post_g']


def _round_up(n, q):
    return (n + q - 1) // q * q


def _tile(n, target, q):
    best = None
    t = q
    while t <= min(n, target):
        if n % t == 0:
            best = t
        t += q
    assert best is not None, (n, target, q)
    return best


def _params(**kw):
    return pltpu.CompilerParams(vmem_limit_bytes=VMEM_LIMIT_MB << 20, **kw)


def _call(body, *, grid, in_specs, out_specs, out_shape, name, args, scratch_shapes=(), carried=(), prefetch=()):
    carried = list(carried)
    n_pf = len(prefetch)

    def launch(fn, in_specs_, out_specs_, out_shape_, scratch_, operands, aliases_):
        if n_pf:
            spec = pltpu.PrefetchScalarGridSpec(num_scalar_prefetch=n_pf, grid=grid, in_specs=in_specs_,
                                                out_specs=out_specs_, scratch_shapes=scratch_)
            return pl.pallas_call(fn, grid_spec=spec, out_shape=out_shape_, input_output_aliases=aliases_,
                                  name=name, compiler_params=_params())(*prefetch, *operands)
        return pl.pallas_call(fn, grid=grid, in_specs=in_specs_, out_specs=out_specs_, out_shape=out_shape_,
                              scratch_shapes=scratch_, input_output_aliases=aliases_, name=name,
                              compiler_params=_params())(*operands)

    if not carried:
        return launch(body, in_specs, out_specs, out_shape, list(scratch_shapes), args, {})
    single = not isinstance(out_shape, (list, tuple))
    out_specs_l = [out_specs] if single else list(out_specs)
    out_shape_l = [out_shape] if single else list(out_shape)
    n_in, n_out, n_scr = len(in_specs), len(out_specs_l), len(scratch_shapes)
    hbm = pl.BlockSpec(memory_space=pl.ANY)
    c_in = [a for cm in carried for a in cm.arrays]
    c_out = [s for cm in carried for s in cm.out_shapes]
    c_scr = []
    aliases = {}
    in_off, out_off = n_pf + n_in, n_out
    for cm in carried:
        c_scr += [pltpu.SemaphoreType.DMA((cm.n_remote,)), pltpu.SemaphoreType.DMA((cm.n_remote,)),
                  pltpu.SemaphoreType.DMA((max(cm.n_local, 1),))]
        for k, v in cm.aliases.items():
            aliases[in_off + k] = out_off + v
        in_off += len(cm.arrays)
        out_off += len(cm.out_shapes)
    steps = 1
    for g in grid:
        steps *= g
    forward_steps = [min(int(cm.forward_at * steps), steps - 1) for cm in carried]

    def wrapped(*refs):
        pf = refs[:n_pf]
        p = n_pf
        ins = refs[p:p + n_in]
        p += n_in
        cins = refs[p:p + len(c_in)]
        p += len(c_in)
        outs = refs[p:p + n_out]
        p += n_out
        couts = refs[p:p + len(c_out)]
        p += len(c_out)
        scr = refs[p:p + n_scr]
        csem = refs[p + n_scr:]
        lin = 0
        for axis, g in enumerate(grid):
            lin = lin * g + pl.program_id(axis)
        views = []
        io = oo = 0
        for j, cm in enumerate(carried):
            views.append((cins[io:io + len(cm.arrays)], couts[oo:oo + len(cm.out_shapes)],
                          csem[3 * j], csem[3 * j + 1], csem[3 * j + 2]))
            io += len(cm.arrays)
            oo += len(cm.out_shapes)

        @pl.when(lin == 0)
        def _():
            for cm, v in zip(carried, views):
                cm.start(*v)

        body(*pf, *ins, *outs, *scr)

        for cm, v, step in zip(carried, views, forward_steps):
            pl.when(lin == step)(functools.partial(cm.forward, *v))

        @pl.when(lin == steps - 1)
        def _():
            for cm, v in zip(carried, views):
                cm.finish(*v)

    res = launch(wrapped, list(in_specs) + [hbm] * len(c_in), out_specs_l + [hbm] * len(c_out),
                 out_shape_l + c_out, list(scratch_shapes) + c_scr, (*args, *c_in), aliases)
    oo = n_out
    for cm in carried:
        cm.results = list(res[oo:oo + len(cm.out_shapes)])
        oo += len(cm.out_shapes)
    return res[0] if single else list(res[:n_out])


def _embed(x, meta, target, g, pad, name, carried=()):
    seq, d = x.shape
    n_meta = meta.shape[0]
    lead = pad + n_meta
    m = lead + seq
    tr = ROW_ALIGN
    lead_blocks = lead // tr
    meta_row = pad - (lead_blocks - 1) * tr
    assert lead % tr == 0 and seq % tr == 0 and 0 <= meta_row and meta_row % 8 == 0

    def body(x_ref, meta_ref, t_ref, g_ref, h_ref, n_ref, tp_ref):
        i = pl.program_id(0)

        @pl.when(i < lead_blocks)
        def _():
            h_ref[...] = jnp.zeros_like(h_ref)
            tp_ref[...] = jnp.zeros_like(tp_ref)

        @pl.when(i == lead_blocks - 1)
        def _():
            h_ref[pl.ds(meta_row, n_meta), :] = meta_ref[...]

        @pl.when(i >= lead_blocks)
        def _():
            h_ref[...] = x_ref[...]
            tp_ref[...] = t_ref[...]

        h = h_ref[...]
        r = lax.rsqrt(jnp.mean(h * h, axis=-1, keepdims=True) + EPS)
        n_ref[...] = (h * r * g_ref[...]).astype(n_ref.dtype)

    tokens = pl.BlockSpec((tr, d), lambda i: (jnp.maximum(i - lead_blocks, 0), 0))
    rows = pl.BlockSpec((tr, d), lambda i: (i, 0))
    return _call(
        body, grid=(m // tr,),
        in_specs=[tokens, pl.BlockSpec((n_meta, d), lambda i: (0, 0)), tokens, pl.BlockSpec((1, d), lambda i: (0, 0))],
        out_specs=[rows, rows, rows],
        out_shape=[jax.ShapeDtypeStruct((m, d), F32), jax.ShapeDtypeStruct((m, d), MXU_DTYPE),
                   jax.ShapeDtypeStruct((m, d), F32)],
        name=name, args=(x, meta, target, g), carried=carried)


def _rmsnorm_bwd_rows(x, g, dy):
    r = lax.rsqrt(jnp.mean(x * x, axis=-1, keepdims=True) + EPS)
    xh = x * r
    dyh = dy * g
    dx = r * (dyh - xh * jnp.mean(dyh * xh, axis=-1, keepdims=True))
    return dx, dy * xh


def _dot_nt(a, b):
    return lax.dot_general(a, b, (((1,), (1,)), ((), ())), preferred_element_type=F32)


def _dot_tn(a, b):
    return lax.dot_general(a, b, (((0,), (0,)), ((), ())), preferred_element_type=F32)


def _mm_nt(a, w, name, carried=(), out_dtype=F32):
    m, k = a.shape
    n = w.shape[0]
    tm = _tile(m, 1056, SUBLANE_BF16)
    tn = _tile(n, 512, LANE)

    def body(a_ref, w_ref, o_ref):
        o_ref[...] = _dot_nt(a_ref[...], w_ref[...]).astype(o_ref.dtype)

    return _call(
        body, grid=(m // tm, n // tn),
        in_specs=[pl.BlockSpec((tm, k), lambda i, j: (i, 0)), pl.BlockSpec((tn, k), lambda i, j: (j, 0))],
        out_specs=pl.BlockSpec((tm, tn), lambda i, j: (i, j)),
        out_shape=jax.ShapeDtypeStruct((m, n), out_dtype), name=name, args=(a, w), carried=carried)


def _norm_bwd_mm_nt(x, g, dy, scale, w, name, carried=()):
    m, d = x.shape
    n = w.shape[0]
    tm = _tile(m, 528, SUBLANE_BF16)

    def body(x_ref, g_ref, dy_ref, w_ref, dx_ref, dg_ref, o_ref):
        @pl.when(pl.program_id(0) == 0)
        def _():
            dg_ref[...] = jnp.zeros_like(dg_ref)

        dx, dgr = _rmsnorm_bwd_rows(x_ref[...], g_ref[...], scale * dy_ref[...])
        dxb = dx.astype(dx_ref.dtype)
        dx_ref[...] = dxb
        dg_ref[...] += jnp.sum(dgr, axis=0, keepdims=True)
        o_ref[...] = _dot_nt(dxb, w_ref[...])

    row = pl.BlockSpec((tm, d), lambda i: (i, 0))
    vec = pl.BlockSpec((1, d), lambda i: (0, 0))
    return _call(
        body, grid=(m // tm,),
        in_specs=[row, vec, row, pl.BlockSpec((n, d), lambda i: (0, 0), pipeline_mode=pl.Buffered(1))],
        out_specs=[row, vec, pl.BlockSpec((tm, n), lambda i: (i, 0))],
        out_shape=[jax.ShapeDtypeStruct((m, d), MXU_DTYPE), jax.ShapeDtypeStruct((1, d), F32),
                   jax.ShapeDtypeStruct((m, n), F32)],
        name=name, args=(x, g, dy, w), carried=carried)


def _ffn_up_act(n_act, wu_t, g_act, name, carried=()):
    m, d = n_act.shape
    fp = wu_t.shape[0]
    tm = _tile(m, 1056, SUBLANE_BF16)
    tn = _tile(fp, 512, LANE)

    def body(n_ref, wu_ref, g_ref, u_ref, a_ref):
        u = _dot_nt(n_ref[...], wu_ref[...])
        g = g_ref[...].astype(F32)
        u_ref[...] = u.astype(u_ref.dtype)
        a_ref[...] = (g * jax.nn.sigmoid(g) * u).astype(a_ref.dtype)

    act = pl.BlockSpec((tm, tn), lambda i, j: (i, j))
    return _call(
        body, grid=(m // tm, fp // tn),
        in_specs=[pl.BlockSpec((tm, d), lambda i, j: (i, 0)), pl.BlockSpec((tn, d), lambda i, j: (j, 0)), act],
        out_specs=[act, act],
        out_shape=[jax.ShapeDtypeStruct((m, fp), MXU_DTYPE)] * 2, name=name, args=(n_act, wu_t, g_act), carried=carried)


def _ffn_gate_up(n_act, wg_t, wu_t, name, carried=()):
    m, d = n_act.shape
    fp = wg_t.shape[0]
    tm = _tile(m, 1056, SUBLANE_BF16)
    tn = _tile(fp, 512, LANE)

    def body(n_ref, wg_ref, wu_ref, g_ref, u_ref, a_ref):
        n = n_ref[...]
        g = _dot_nt(n, wg_ref[...])
        u = _dot_nt(n, wu_ref[...])
        g_ref[...] = g.astype(g_ref.dtype)
        u_ref[...] = u.astype(u_ref.dtype)
        a_ref[...] = (g * jax.nn.sigmoid(g) * u).astype(a_ref.dtype)

    act = pl.BlockSpec((tm, tn), lambda i, j: (i, j))
    wsp = pl.BlockSpec((tn, d), lambda i, j: (j, 0))
    return _call(
        body, grid=(m // tm, fp // tn),
        in_specs=[pl.BlockSpec((tm, d), lambda i, j: (i, 0)), wsp, wsp],
        out_specs=[act, act, act],
        out_shape=[jax.ShapeDtypeStruct((m, fp), MXU_DTYPE)] * 3, name=name, args=(n_act, wg_t, wu_t), carried=carried)


def _ffn_hidden_bwd(dfo, wd, g_act, u_act, name, carried=()):
    m, d = dfo.shape
    fp = wd.shape[0]
    tm = _tile(m, 1056, SUBLANE_BF16)
    tn = _tile(fp, 512, LANE)

    def body(df_ref, wd_ref, g_ref, u_ref, dg_ref, du_ref):
        da = _dot_nt(df_ref[...], wd_ref[...])
        g = g_ref[...].astype(F32)
        u = u_ref[...].astype(F32)
        s = jax.nn.sigmoid(g)
        du_ref[...] = (da * (g * s)).astype(du_ref.dtype)
        dg_ref[...] = (da * u * (s * (1.0 + g * (1.0 - s)))).astype(dg_ref.dtype)

    act = pl.BlockSpec((tm, tn), lambda i, j: (i, j))
    return _call(
        body, grid=(m // tm, fp // tn),
        in_specs=[pl.BlockSpec((tm, d), lambda i, j: (i, 0)), pl.BlockSpec((tn, d), lambda i, j: (j, 0)), act, act],
        out_specs=[act, act],
        out_shape=[jax.ShapeDtypeStruct((m, fp), MXU_DTYPE)] * 2, name=name, args=(dfo, wd, g_act, u_act),
        carried=carried)


def _row_groups(n_tiles, max_group, nk):
    gsz = max(q for q in range(1, max_group + 1) if n_tiles % q == 0)

    def epilogue_row(grp, kk, i):
        return grp * gsz + jnp.where(kk == nk - 1, i, 0)

    return gsz, epilogue_row


def _mm_residual_norm(a, w, h, g, scale, next_g, name, carried=()):
    m, k = a.shape
    d = w.shape[1]
    tm = _tile(m, ACC_ROWS, SUBLANE_BF16)
    tk = _tile(k, K_TILE, LANE)
    nk = k // tk
    gsz, epilogue_row = _row_groups(m // tm, ACC_GROUP, nk)

    def body(a_ref, w_ref, h_ref, g_ref, ng_ref, fo_ref, hn_ref, nn_ref, acc_ref):
        kk, i = pl.program_id(1), pl.program_id(2)

        @pl.when(kk == 0)
        def _():
            acc_ref[i] = jnp.zeros((tm, d), F32)

        acc_ref[i] += jnp.dot(a_ref[...], w_ref[...], preferred_element_type=F32)

        @pl.when(kk == nk - 1)
        def _():
            fo = acc_ref[i]
            fo_ref[...] = fo
            r = lax.rsqrt(jnp.mean(fo * fo, axis=-1, keepdims=True) + EPS)
            hn = h_ref[...] + scale * (fo * r * g_ref[...])
            hn_ref[...] = hn
            rn = lax.rsqrt(jnp.mean(hn * hn, axis=-1, keepdims=True) + EPS)
            nn_ref[...] = (hn * rn * ng_ref[...]).astype(nn_ref.dtype)

    row = pl.BlockSpec((tm, d), lambda grp, kk, i: (epilogue_row(grp, kk, i), 0))
    row_once = pl.BlockSpec((tm, d), lambda grp, kk, i: (epilogue_row(grp, kk, i), 0), pipeline_mode=pl.Buffered(1))
    vec = pl.BlockSpec((1, d), lambda grp, kk, i: (0, 0))
    return _call(
        body, grid=(m // tm // gsz, nk, gsz),
        in_specs=[pl.BlockSpec((tm, tk), lambda grp, kk, i: (grp * gsz + i, kk)),
                  pl.BlockSpec((tk, d), lambda grp, kk, i: (kk, 0)), row_once, vec, vec],
        out_specs=[row, row, row],
        out_shape=[jax.ShapeDtypeStruct((m, d), F32)] * 2 + [jax.ShapeDtypeStruct((m, d), MXU_DTYPE)],
        scratch_shapes=[pltpu.VMEM((gsz, tm, d), F32)], name=name, args=(a, w, h, g, next_g), carried=carried)


def _mm_residual_loss(a, w, h, g, scale, target, lead, name, carried=()):
    m, k = a.shape
    d = w.shape[1]
    tm = _tile(m, ACC_ROWS, SUBLANE_BF16)
    tk = _tile(k, K_TILE, LANE)
    nk = k // tk
    gsz, epilogue_row = _row_groups(m // tm, ACC_GROUP, nk)

    def body(a_ref, w_ref, h_ref, g_ref, t_ref, dy_ref, dfo_ref, dg_ref, l_ref, acc_ref):
        grp, kk, i = pl.program_id(0), pl.program_id(1), pl.program_id(2)

        @pl.when(jnp.logical_and(jnp.logical_and(grp == 0, kk == 0), i == 0))
        def _():
            dg_ref[...] = jnp.zeros_like(dg_ref)
            l_ref[...] = jnp.zeros_like(l_ref)

        @pl.when(kk == 0)
        def _():
            acc_ref[i] = jnp.zeros((tm, d), F32)

        acc_ref[i] += jnp.dot(a_ref[...], w_ref[...], preferred_element_type=F32)

        @pl.when(kk == nk - 1)
        def _():
            fo = acc_ref[i]
            gain = g_ref[...]
            r = lax.rsqrt(jnp.mean(fo * fo, axis=-1, keepdims=True) + EPS)
            xh = fo * r
            y = h_ref[...] + scale * (xh * gain)
            row = (grp * gsz + i) * tm + lax.broadcasted_iota(jnp.int32, (tm, 1), 0)
            e = jnp.where(row >= lead, y - t_ref[...], 0.0)
            dy = e * (1.0 / d)
            dy_ref[...] = dy
            l_ref[...] += 0.5 * jnp.sum(jnp.sum(e * e, axis=-1, keepdims=True) * (1.0 / d), axis=0, keepdims=True)
            dn = scale * dy
            dyh = dn * gain
            dfo_ref[...] = (r * (dyh - xh * jnp.mean(dyh * xh, axis=-1, keepdims=True))).astype(dfo_ref.dtype)
            dg_ref[...] += jnp.sum(dn * xh, axis=0, keepdims=True)

    row = pl.BlockSpec((tm, d), lambda grp, kk, i: (epilogue_row(grp, kk, i), 0))
    row_once = pl.BlockSpec((tm, d), lambda grp, kk, i: (epilogue_row(grp, kk, i), 0), pipeline_mode=pl.Buffered(1))
    vec = pl.BlockSpec((1, d), lambda grp, kk, i: (0, 0))
    return _call(
        body, grid=(m // tm // gsz, nk, gsz),
        in_specs=[pl.BlockSpec((tm, tk), lambda grp, kk, i: (grp * gsz + i, kk)),
                  pl.BlockSpec((tk, d), lambda grp, kk, i: (kk, 0)), row_once, vec, row_once],
        out_specs=[row, row, vec, pl.BlockSpec((1, 1), lambda grp, kk, i: (0, 0))],
        out_shape=[jax.ShapeDtypeStruct((m, d), F32), jax.ShapeDtypeStruct((m, d), MXU_DTYPE),
                   jax.ShapeDtypeStruct((1, d), F32), jax.ShapeDtypeStruct((1, 1), F32)],
        scratch_shapes=[pltpu.VMEM((gsz, tm, d), F32)], name=name, args=(a, w, h, g, target), carried=carried)


def _norm_bwd_row_tile(m):
    return _tile(m, ACC_ROWS, SUBLANE_BF16)


def _mm_norm_bwd(pairs, h, g, dh_up, name, carried=(), row_tiles=None, dg_init=None, post=None):
    n_pairs = len(pairs)
    m, k = pairs[0][0].shape
    d = h.shape[1]
    tm = _norm_bwd_row_tile(m)
    tk = _tile(k, K_TILE, LANE)
    nk = k // tk
    t0, nt = row_tiles if row_tiles is not None else (0, m // tm)
    gsz, epilogue_row = _row_groups(nt, ACC_GROUP, nk)
    if dg_init is None:
        dg_init = jnp.zeros((1, d), F32)

    n_post = 0 if post is None else 2

    def body(*refs):
        ops = refs[:2 * n_pairs]
        h_ref, g_ref, up_ref, init_ref = refs[2 * n_pairs:2 * n_pairs + 4]
        post_in = refs[2 * n_pairs + 4:2 * n_pairs + 4 + n_post]
        dh_ref, dg_ref = refs[2 * n_pairs + 4 + n_post:2 * n_pairs + 6 + n_post]
        post_out = refs[2 * n_pairs + 6 + n_post:2 * n_pairs + 6 + 2 * n_post]
        acc_ref = refs[-1]
        grp, kk, i = pl.program_id(0), pl.program_id(1), pl.program_id(2)

        @pl.when(jnp.logical_and(jnp.logical_and(grp == 0, kk == 0), i == 0))
        def _():
            dg_ref[...] = init_ref[...]
            if post is not None:
                post_out[1][...] = jnp.zeros_like(post_out[1])

        @pl.when(kk == 0)
        def _():
            acc_ref[i] = jnp.zeros((tm, d), F32)

        for p in range(n_pairs):
            acc_ref[i] += jnp.dot(ops[2 * p][...], ops[2 * p + 1][...], preferred_element_type=F32)

        @pl.when(kk == nk - 1)
        def _():
            dx, dgr = _rmsnorm_bwd_rows(h_ref[...], g_ref[...], acc_ref[i])
            dh = up_ref[...] + dx
            dh_ref[...] = dh
            dg_ref[...] += jnp.sum(dgr, axis=0, keepdims=True)
            if post is not None:
                dfo, dpr = _rmsnorm_bwd_rows(post_in[0][...], post_in[1][...], post[2] * dh)
                post_out[0][...] = dfo.astype(post_out[0].dtype)
                post_out[1][...] += jnp.sum(dpr, axis=0, keepdims=True)

    row_in = pl.BlockSpec((tm, d), lambda grp, kk, i: (t0 + epilogue_row(grp, kk, i), 0))
    row_out = pl.BlockSpec((tm, d), lambda grp, kk, i: (epilogue_row(grp, kk, i), 0))
    vec = pl.BlockSpec((1, d), lambda grp, kk, i: (0, 0))
    in_specs = []
    args = []
    for a, w in pairs:
        in_specs += [pl.BlockSpec((tm, tk), lambda grp, kk, i: (t0 + grp * gsz + i, kk)),
                     pl.BlockSpec((tk, d), lambda grp, kk, i: (kk, 0))]
        args += [a, w]
    in_specs += [row_in, vec, row_in, vec]
    args += [h, g, dh_up, dg_init]
    out_specs = [row_out, vec]
    out_shape = [jax.ShapeDtypeStruct((nt * tm, d), F32), jax.ShapeDtypeStruct((1, d), F32)]
    if post is not None:
        in_specs += [row_in, vec]
        args += [post[0], post[1]]
        out_specs += [row_out, vec]
        out_shape += [jax.ShapeDtypeStruct((nt * tm, d), MXU_DTYPE), jax.ShapeDtypeStruct((1, d), F32)]
    return _call(
        body, grid=(nt // gsz, nk, gsz), in_specs=in_specs, out_specs=out_specs, out_shape=out_shape,
        scratch_shapes=[pltpu.VMEM((gsz, tm, d), F32)], name=name, args=tuple(args), carried=carried)


def _mm_tn(a, b, name, carried=()):
    m, ka = a.shape
    d = b.shape[1]
    tf = _tile(ka, 512, LANE)

    def body(a_ref, b_ref, o_ref):
        o_ref[...] = _dot_tn(a_ref[...], b_ref[...]).astype(o_ref.dtype)

    return _call(
        body, grid=(ka // tf,),
        in_specs=[pl.BlockSpec((m, tf), lambda j: (0, j)),
                  pl.BlockSpec((m, d), lambda j: (0, 0), pipeline_mode=pl.Buffered(1))],
        out_specs=pl.BlockSpec((tf, d), lambda j: (j, 0)),
        out_shape=jax.ShapeDtypeStruct((ka, d), WIRE_DTYPE), name=name, args=(a, b), carried=carried)


GELU_K = 0.7978845608028654
GELU_C = 0.044715


def _expm1(x):
    series = x * (1.0 + x * (1.0 / 2 + x * (1.0 / 6 + x * (1.0 / 24 + x * (1.0 / 120)))))
    return jnp.where(jnp.abs(x) < 0.1, series, jnp.exp(x) - 1.0)


def _softplus(x):
    return jnp.maximum(x, 0.0) + jnp.log1p(jnp.exp(-jnp.abs(x)))


def _block_mm(v, w_ref, transposed):
    nbk = w_ref.shape[0]
    outs = []
    for j in range(nbk):
        vj = v[:, j * BD:(j + 1) * BD]
        outs.append(_dot_nt(vj, w_ref[j]) if transposed else jnp.dot(vj, w_ref[j], preferred_element_type=F32))
    return outs[0] if nbk == 1 else jnp.concatenate(outs, axis=1)


def _group_mean(q, gm_ref):
    hi = q.astype(MXU_DTYPE)
    lo = (q - hi.astype(F32)).astype(MXU_DTYPE)
    nbk = q.shape[1] // BD
    gm = gm_ref[...]
    outs = []
    for j in range(nbk):
        sl = slice(j * BD, (j + 1) * BD)
        outs.append(jnp.dot(hi[:, sl], gm, preferred_element_type=F32) + jnp.dot(lo[:, sl], gm, preferred_element_type=F32))
    return outs[0] if nbk == 1 else jnp.concatenate(outs, axis=1)


class _RowReader:
    def __init__(self, ref):
        self.ref = ref

    def __getitem__(self, rows):
        return self.ref[rows, :]


def _shifted(ext_ref, cur, before8, after8, downs=(), ups=()):
    r = cur.shape[0]
    if downs:
        ext_ref[0:8, :] = before8
    ext_ref[8:8 + r, :] = cur
    if ups:
        ext_ref[8 + r:16 + r, :] = after8
    return [ext_ref[pl.ds(8 - j, r), :] for j in downs] + [ext_ref[pl.ds(8 + j, r), :] for j in ups]


def _lru_gates(xc, pv, wa_ref, wx_ref):
    xcb = xc.astype(MXU_DTYPE)
    ga = jax.nn.sigmoid(_block_mm(xcb, wa_ref, False) + pv[5:6])
    gx = jax.nn.sigmoid(_block_mm(xcb, wx_ref, False) + pv[6:7])
    sp = _softplus(-pv[7:8])
    log_a = -LRU_C * ga * sp
    a = jnp.exp(log_a)
    e2 = _expm1(2.0 * log_a)
    mult = jnp.sqrt(-e2)
    return xcb, ga, gx, sp, a, e2, mult


def _gelu_parts(y):
    th = jnp.tanh(GELU_K * (y + GELU_C * y * y * y))
    return 0.5 * y * (1.0 + th), th


def _scan_block(a, u, sa_ref, su_ref, carry_ref, out_ref, reverse):
    r, c = a.shape
    n = r // 8
    a3 = a.reshape(n, 8, c)
    u3 = u.reshape(n, 8, c)
    sub = lax.broadcasted_iota(jnp.int32, (n, 8, c), 1)
    for dlt in (1, 2, 4):
        keep = (sub < 8 - dlt) if reverse else (sub >= dlt)
        shift = 8 - dlt if reverse else dlt
        sh_a = pltpu.roll(a3, shift, axis=1)
        sh_u = pltpu.roll(u3, shift, axis=1)
        u3 = u3 + a3 * jnp.where(keep, sh_u, 0.0)
        a3 = a3 * jnp.where(keep, sh_a, 1.0)
    sa_ref[...] = a3.reshape(r, c)
    su_ref[...] = u3.reshape(r, c)
    for k in (range(n - 1, -1, -1) if reverse else range(n)):
        rows = pl.ds(8 * k, 8)
        out_ref[rows, :] = su_ref[rows, :] + sa_ref[rows, :] * carry_ref[...]
        carry_ref[...] = out_ref[pl.ds(8 * k if reverse else 8 * k + 7, 1), :]


def _mixer_fwd(z, pv, wa, wx, gm, pad, name, carried=()):
    m = z.shape[0]
    c = pv.shape[1]
    r = MIX_ROWS
    nb = m // r

    def body(z_ref, pv_ref, wa_ref, wx_ref, gm_ref, mixed_ref, hs_ref, ext_ref, tailx_ref, tailc_ref, carry_ref,
             sa_ref, su_ref):
        b = pl.program_id(0)

        @pl.when(b == 0)
        def _():
            tailx_ref[...] = jnp.zeros_like(tailx_ref)
            tailc_ref[...] = jnp.zeros_like(tailc_ref)
            carry_ref[...] = jnp.zeros_like(carry_ref)

        pv = _RowReader(pv_ref)
        row = b * r + lax.broadcasted_iota(jnp.int32, (r, 1), 0)
        maskf = (row >= pad).astype(F32)
        y = z_ref[:, 0:c]
        xl = z_ref[:, c:2 * c]
        bs = z_ref[:, 2 * c:3 * c]
        cv = z_ref[:, 3 * c:4 * c] * z_ref[:, 4 * c:5 * c]

        x1, x2, x3 = _shifted(ext_ref, xl, tailx_ref[...], None, downs=(1, 2, 3))
        tailx_ref[...] = z_ref[pl.ds(r - 8, 8), c:2 * c]
        xc = pv[4:5] + pv[3:4] * xl + pv[2:3] * x1 + pv[1:2] * x2 + pv[0:1] * x3
        _, _, gx, _, a, _, mult = _lru_gates(xc, pv, wa_ref, wx_ref)
        uu = mult * (gx * xc) * maskf

        _scan_block(a, uu, sa_ref, su_ref, carry_ref, hs_ref, reverse=False)
        hs = hs_ref[...]

        gelu_y, _ = _gelu_parts(y)
        lru_out = hs * gelu_y
        c1, c2 = _shifted(ext_ref, cv, tailc_ref[...], None, downs=(1, 2))
        tailc_ref[...] = cv[r - 8:r]
        sc_out = bs * (pv[10:11] * cv + pv[9:10] * c1 + pv[8:9] * c2)

        rl = lax.rsqrt(_group_mean(lru_out * lru_out, gm_ref) + EPS)
        rs = lax.rsqrt(_group_mean(sc_out * sc_out, gm_ref) + EPS)
        mixed_ref[:, 0:c] = (lru_out * rl * pv[11:12]).astype(mixed_ref.dtype)
        mixed_ref[:, c:2 * c] = (sc_out * rs * pv[12:13]).astype(mixed_ref.dtype)

    full = lambda shape: pl.BlockSpec(shape, lambda b: (0,) * len(shape))
    return _call(
        body, grid=(nb,),
        in_specs=[pl.BlockSpec((r, 5 * c), lambda b: (b, 0)), full(pv.shape), full(wa.shape), full(wx.shape), full(gm.shape)],
        out_specs=[pl.BlockSpec((r, 2 * c), lambda b: (b, 0)), pl.BlockSpec((r, c), lambda b: (b, 0))],
        out_shape=[jax.ShapeDtypeStruct((m, 2 * c), MXU_DTYPE), jax.ShapeDtypeStruct((m, c), F32)],
        scratch_shapes=[pltpu.VMEM((r + 16, c), F32), pltpu.VMEM((8, c), F32), pltpu.VMEM((8, c), F32),
                        pltpu.VMEM((1, c), F32), pltpu.VMEM((r, c), F32), pltpu.VMEM((r, c), F32)],
        name=name, args=(z, pv, wa, wx, gm), carried=carried)


def _mixer_bwd(z, hs, dmixed, pv, wa, wx, gm, pad, name, carried=()):
    m = z.shape[0]
    c = pv.shape[1]
    r = MIX_ROWS
    nb = m // r
    r8 = r // 8
    assert pad <= r and pad % SUBLANE_BF16 == 0

    def body(z_ref, zp_ref, hs_ref, hsp_ref, dm_ref, pv_ref, wa_ref, wx_ref, gm_ref,
             dz_ref, dpv_ref, dwa_ref, dwx_ref, ext_ref, hxc_ref, hsc_ref, hp_ref, pc_ref, sa_ref, su_ref, p_ref):
        i = pl.program_id(0)
        b = nb - 1 - i

        @pl.when(i == 0)
        def _():
            hxc_ref[...] = jnp.zeros_like(hxc_ref)
            hsc_ref[...] = jnp.zeros_like(hsc_ref)
            hp_ref[...] = jnp.zeros_like(hp_ref)
            pc_ref[...] = jnp.zeros_like(pc_ref)
            dpv_ref[...] = jnp.zeros_like(dpv_ref)
            dwa_ref[...] = jnp.zeros_like(dwa_ref)
            dwx_ref[...] = jnp.zeros_like(dwx_ref)

        pv = _RowReader(pv_ref)
        row = b * r + lax.broadcasted_iota(jnp.int32, (r, 1), 0)
        maskf = (row >= pad).astype(F32)
        has_prev = (b > 0).astype(F32)
        y = z_ref[:, 0:c]
        xl = z_ref[:, c:2 * c]
        bs = z_ref[:, 2 * c:3 * c]
        cs = z_ref[:, 3 * c:4 * c]
        vs = z_ref[:, 4 * c:5 * c]
        cv = cs * vs
        xl_prev = zp_ref[:, c:2 * c] * has_prev
        cv_prev = zp_ref[:, 3 * c:4 * c] * zp_ref[:, 4 * c:5 * c] * has_prev
        hs = hs_ref[...]

        x1, x2, x3 = _shifted(ext_ref, xl, xl_prev, None, downs=(1, 2, 3))
        xc = pv[4:5] + pv[3:4] * xl + pv[2:3] * x1 + pv[1:2] * x2 + pv[0:1] * x3
        xcb, ga, gx, sp, a, e2, mult = _lru_gates(xc, pv, wa_ref, wx_ref)
        gxx = gx * xc
        gelu_y, th = _gelu_parts(y)
        lru_out = hs * gelu_y
        c1, c2 = _shifted(ext_ref, cv, cv_prev, None, downs=(1, 2))
        sc = pv[10:11] * cv + pv[9:10] * c1 + pv[8:9] * c2
        sc_out = bs * sc

        def group_norm_bwd(v, dm, gain):
            rr = lax.rsqrt(_group_mean(v * v, gm_ref) + EPS)
            vh = v * rr
            dvh = dm * gain
            dv = rr * (dvh - vh * _group_mean(dvh * vh, gm_ref))
            return dv, jnp.sum(dm * vh, axis=0, keepdims=True)

        d_lru_out, d_og = group_norm_bwd(lru_out, dm_ref[:, 0:c], pv[11:12])
        d_sc_out, d_sg = group_norm_bwd(sc_out, dm_ref[:, c:2 * c], pv[12:13])
        dpv_ref[11:12, :] += d_og
        dpv_ref[12:13, :] += d_sg

        dhs = d_lru_out * gelu_y
        dgelu = 0.5 * (1.0 + th) + 0.5 * y * (1.0 - th * th) * GELU_K * (1.0 + 3.0 * GELU_C * y * y)
        dy = d_lru_out * hs * dgelu

        _scan_block(a, a * dhs, sa_ref, su_ref, pc_ref, p_ref, reverse=True)
        (p_next,) = _shifted(ext_ref, p_ref[...], None, hp_ref[...], ups=(1,))
        hp_ref[...] = p_ref[0:8, :]
        q = dhs + p_next
        (hs_prev,) = _shifted(ext_ref, hs, hsp_ref[...] * has_prev, None, downs=(1,))
        duu = q * maskf
        da = q * hs_prev

        dmult = duu * gxx
        dgxx = duu * mult
        dgx = dgxx * xc
        dxc = dgxx * gx
        dlog_a = da * a - dmult * ((1.0 + e2) / mult)
        dga = dlog_a * (-LRU_C * sp)
        dsp = jnp.sum(dlog_a * (-LRU_C * ga), axis=0, keepdims=True)
        dpv_ref[7:8, :] += dsp * (-jax.nn.sigmoid(-pv[7:8]))
        dga_pre = dga * ga * (1.0 - ga)
        dgx_pre = dgx * gx * (1.0 - gx)
        dpv_ref[5:6, :] += jnp.sum(dga_pre, axis=0, keepdims=True)
        dpv_ref[6:7, :] += jnp.sum(dgx_pre, axis=0, keepdims=True)
        dga_b = dga_pre.astype(MXU_DTYPE)
        dgx_b = dgx_pre.astype(MXU_DTYPE)
        dxc = dxc + _block_mm(dga_b, wa_ref, True) + _block_mm(dgx_b, wx_ref, True)
        for j in range(c // BD):
            sl = slice(j * BD, (j + 1) * BD)
            dwa_ref[j] += _dot_tn(xcb[:, sl], dga_b[:, sl])
            dwx_ref[j] += _dot_tn(xcb[:, sl], dgx_b[:, sl])

        dpv_ref[4:5, :] += jnp.sum(dxc, axis=0, keepdims=True)
        dpv_ref[3:4, :] += jnp.sum(dxc * xl, axis=0, keepdims=True)
        dpv_ref[2:3, :] += jnp.sum(dxc * x1, axis=0, keepdims=True)
        dpv_ref[1:2, :] += jnp.sum(dxc * x2, axis=0, keepdims=True)
        dpv_ref[0:1, :] += jnp.sum(dxc * x3, axis=0, keepdims=True)
        u1, u2, u3 = _shifted(ext_ref, dxc, None, hxc_ref[...], ups=(1, 2, 3))
        hxc_ref[...] = dxc[0:8]
        dxl = pv[3:4] * dxc + pv[2:3] * u1 + pv[1:2] * u2 + pv[0:1] * u3

        dbs = d_sc_out * sc
        dsc = d_sc_out * bs
        dpv_ref[10:11, :] += jnp.sum(dsc * cv, axis=0, keepdims=True)
        dpv_ref[9:10, :] += jnp.sum(dsc * c1, axis=0, keepdims=True)
        dpv_ref[8:9, :] += jnp.sum(dsc * c2, axis=0, keepdims=True)
        s1, s2 = _shifted(ext_ref, dsc, None, hsc_ref[...], ups=(1, 2))
        hsc_ref[...] = dsc[0:8]
        dcv = pv[10:11] * dsc + pv[9:10] * s1 + pv[8:9] * s2

        dz_ref[:, 0:c] = dy.astype(dz_ref.dtype)
        dz_ref[:, c:2 * c] = dxl.astype(dz_ref.dtype)
        dz_ref[:, 2 * c:3 * c] = dbs.astype(dz_ref.dtype)
        dz_ref[:, 3 * c:4 * c] = (dcv * vs).astype(dz_ref.dtype)
        dz_ref[:, 4 * c:5 * c] = (dcv * cs).astype(dz_ref.dtype)

        if pad:
            @pl.when(b == 0)
            def _():
                dz_ref[0:pad, :] = jnp.zeros((pad, 5 * c), dz_ref.dtype)

    full = lambda shape: pl.BlockSpec(shape, lambda i: (0,) * len(shape))
    cur = lambda width: pl.BlockSpec((r, width), lambda i: (nb - 1 - i, 0))
    prev8 = lambda width: pl.BlockSpec((8, width), lambda i: (jnp.maximum((nb - 1 - i) * r8 - 1, 0), 0))
    return _call(
        body, grid=(nb,),
        in_specs=[cur(5 * c), prev8(5 * c), cur(c), prev8(c), cur(2 * c),
                  full(pv.shape), full(wa.shape), full(wx.shape), full(gm.shape)],
        out_specs=[cur(5 * c), full(pv.shape), full(wa.shape), full(wx.shape)],
        out_shape=[jax.ShapeDtypeStruct((m, 5 * c), MXU_DTYPE), jax.ShapeDtypeStruct(pv.shape, F32),
                   jax.ShapeDtypeStruct(wa.shape, F32), jax.ShapeDtypeStruct(wx.shape, F32)],
        scratch_shapes=[pltpu.VMEM((r + 16, c), F32), pltpu.VMEM((8, c), F32), pltpu.VMEM((8, c), F32),
                        pltpu.VMEM((8, c), F32), pltpu.VMEM((1, c), F32), pltpu.VMEM((r, c), F32),
                        pltpu.VMEM((r, c), F32), pltpu.VMEM((r, c), F32)],
        name=name, args=(z, z, hs, hs, dmixed, pv, wa, wx, gm), carried=carried)


def _position():
    return lax.axis_index("x"), lax.axis_index("y"), lax.axis_index("c")


def _block_of(px, py, pc):
    return 4 * px + 2 * py + pc


class _TwoLevelGather:
    def __init__(self, n_arrays, rows_of, src_of, send_sems, recv_sems):
        x, y, c = _position()
        self.n, self.rows_of, self.src_of = n_arrays, rows_of, src_of
        self.send_sems, self.recv_sems = send_sems, recv_sems
        self.c, self.me, self.sibling = c, (x, y, c), (x, y, 1 - c)
        self.chips = [(1 - x, y), (x, 1 - y), (1 - x, 1 - y)]

    def _copy(self, i, k, block, to, src=None):
        return pltpu.make_async_remote_copy(
            src_ref=self.rows_of(i, *block) if src is None else src, dst_ref=self.rows_of(i, *block),
            send_sem=self.send_sems.at[7 * i + k], recv_sem=self.recv_sems.at[7 * i + k],
            device_id=to, device_id_type=MESH)

    def _first(self, i):
        own = [self._copy(i, 0, self.me, self.sibling, src=self.src_of(i))]
        return own + [self._copy(i, 1 + j, self.me, (*chip, self.c), src=self.src_of(i))
                      for j, chip in enumerate(self.chips)]

    def _passed(self, i, j):
        return self._copy(i, 4 + j, (*self.chips[j], self.c), self.sibling)

    def start(self):
        for i in range(self.n):
            for cp in self._first(i):
                cp.start()

    def forward(self):
        for i in range(self.n):
            for j, chip in enumerate(self.chips):
                self._copy(i, 1 + j, (*chip, self.c), self.me).wait_recv()
                self._passed(i, j).start()

    def drain(self):
        for i in range(self.n):
            self._copy(i, 0, self.sibling, self.me).wait_recv()
            for j, chip in enumerate(self.chips):
                self._copy(i, 4 + j, (*chip, 1 - self.c), self.me).wait_recv()
        for i in range(self.n):
            for cp in self._first(i) + [self._passed(i, j) for j in range(3)]:
                cp.wait_send()


class _CarriedGather:
    def __init__(self, shards, padded_rows, zeros, forward_at, part=None, into=None):
        d = shards[0].shape[1]
        self.forward_at = forward_at
        self.n = len(shards)
        self.rows = [s.shape[0] for s in shards]
        self.pads = [p - N_DEV * r for r, p in zip(self.rows, padded_rows)]
        assert max(self.pads) <= zeros.shape[0] and zeros.shape[1] == d
        self.part = part if part is not None else (0, self.rows[0])
        assert (part is None and into is None) or self.n == 1
        assert self.part[0] % SUBLANE_BF16 == 0 and self.part[1] % SUBLANE_BF16 == 0
        self.arrays = list(shards) + [zeros] + ([into] if into is not None else [])
        self.out_shapes = [jax.ShapeDtypeStruct((p, d), s.dtype) for s, p in zip(shards, padded_rows)]
        self.aliases = {self.n + 1: 0} if into is not None else {}
        if into is not None:
            self.pads = [0] * self.n
        self.n_remote, self.n_local = 7 * self.n, 2 * self.n
        self.results = None

    def _rows_of(self, outs):
        def rows_of(i, px, py, pc):
            first = _block_of(px, py, pc) * self.rows[i] + (self.part[0] if self.n == 1 else 0)
            count = self.part[1] if self.n == 1 else self.rows[i]
            return outs[i].at[pl.ds(pl.multiple_of(first, SUBLANE_BF16), count), :]
        return rows_of

    def _own(self, ins, i):
        return ins[i].at[pl.ds(self.part[0], self.part[1]), :] if self.n == 1 else ins[i]

    def _gather(self, ins, outs, send_sems, recv_sems):
        return _TwoLevelGather(self.n, self._rows_of(outs), functools.partial(self._own, ins), send_sems, recv_sems)

    def _local(self, ins, outs, local_sems):
        x, y, c = _position()
        rows_of = self._rows_of(outs)
        cps = []
        for i in range(self.n):
            cps.append(pltpu.make_async_copy(self._own(ins, i), rows_of(i, x, y, c), local_sems.at[2 * i]))
            if self.pads[i]:
                cps.append(pltpu.make_async_copy(ins[self.n].at[pl.ds(0, self.pads[i]), :],
                                                 outs[i].at[pl.ds(N_DEV * self.rows[i], self.pads[i]), :],
                                                 local_sems.at[2 * i + 1]))
        return cps

    def start(self, ins, outs, send_sems, recv_sems, local_sems):
        for cp in self._local(ins, outs, local_sems):
            cp.start()
        self._gather(ins, outs, send_sems, recv_sems).start()

    def forward(self, ins, outs, send_sems, recv_sems, local_sems):
        self._gather(ins, outs, send_sems, recv_sems).forward()

    def finish(self, ins, outs, send_sems, recv_sems, local_sems):
        self._gather(ins, outs, send_sems, recv_sems).drain()
        for cp in self._local(ins, outs, local_sems):
            cp.wait()


class _CarriedSwap:
    def __init__(self, grads, shard_rows):
        d = grads[0].shape[1]
        self.n, self.rows = len(grads), list(shard_rows)
        self.arrays = list(grads)
        self.out_shapes = [jax.ShapeDtypeStruct((4, s, d), g.dtype) for g, s in zip(grads, shard_rows)]
        self.aliases = {}
        self.n_remote, self.n_local = 4 * self.n, 0
        self.forward_at = 1.0
        self.results = None

    def _copies(self, ins, outs, send_sems, recv_sems):
        x, y, c = _position()
        cps = []
        for i in range(self.n):
            s = self.rows[i]
            for k in range(4):
                blk = _block_of(k >> 1, k & 1, 1 - c)
                cps.append(pltpu.make_async_remote_copy(
                    src_ref=ins[i].at[pl.ds(pl.multiple_of(blk * s, SUBLANE_BF16), s), :], dst_ref=outs[i].at[k],
                    send_sem=send_sems.at[4 * i + k], recv_sem=recv_sems.at[4 * i + k],
                    device_id=(x, y, 1 - c), device_id_type=MESH))
        return cps

    def start(self, ins, outs, send_sems, recv_sems, local_sems):
        for cp in self._copies(ins, outs, send_sems, recv_sems):
            cp.start()

    def forward(self, *_):
        pass

    def finish(self, ins, outs, send_sems, recv_sems, local_sems):
        for cp in self._copies(ins, outs, send_sems, recv_sems):
            cp.wait()


class _CarriedChipExchange:
    def __init__(self, presums, part=None, into=None):
        self.n = len(presums)
        assert (part is None and into is None) or self.n == 1
        self.part = part if part is not None else (0, presums[0].shape[1])
        assert self.part[0] % SUBLANE_BF16 == 0 and self.part[1] % SUBLANE_BF16 == 0
        self.arrays = list(presums) + ([into] if into is not None else [])
        self.out_shapes = [jax.ShapeDtypeStruct(p.shape, p.dtype) for p in presums]
        self.aliases = {self.n: 0} if into is not None else {}
        self.n_remote, self.n_local = 3 * self.n, 0
        self.forward_at = 1.0
        self.results = None

    def _copies(self, ins, outs, send_sems, recv_sems):
        x, y, c = _position()
        cps = []
        for i in range(self.n):
            rows = pl.ds(*self.part) if self.n == 1 else pl.ds(0, self.arrays[i].shape[1])
            for r in range(1, 4):
                cps.append(pltpu.make_async_remote_copy(
                    src_ref=ins[i].at[r - 1, rows, :], dst_ref=outs[i].at[r - 1, rows, :],
                    send_sem=send_sems.at[3 * i + r - 1], recv_sem=recv_sems.at[3 * i + r - 1],
                    device_id=(x ^ (r >> 1), y ^ (r & 1), c), device_id_type=MESH))
        return cps

    def start(self, ins, outs, send_sems, recv_sems, local_sems):
        for cp in self._copies(ins, outs, send_sems, recv_sems):
            cp.start()

    def forward(self, *_):
        pass

    def finish(self, ins, outs, send_sems, recv_sems, local_sems):
        for cp in self._copies(ins, outs, send_sems, recv_sems):
            cp.wait()


def _gather_small(block, reduce, name):
    rr, nn = block.shape

    def body(x_ref, out_ref, *rest):
        if reduce:
            stack_ref, send_sems, recv_sems, local_sem = rest
        else:
            send_sems, recv_sems, local_sem = rest
            stack_ref = out_ref
        x, y, c = _position()

        def rows_of(i, px, py, pc):
            return stack_ref.at[pl.ds(pl.multiple_of(_block_of(px, py, pc) * rr, 8), rr), :]

        own = pltpu.make_async_copy(x_ref, rows_of(0, x, y, c), local_sem)
        own.start()
        gather = _TwoLevelGather(1, rows_of, lambda i: x_ref, send_sems, recv_sems)
        gather.start()
        gather.forward()
        gather.drain()
        own.wait()
        if reduce:
            acc = stack_ref[0:rr, :]
            for k in range(1, N_DEV):
                acc = acc + stack_ref[k * rr:(k + 1) * rr, :]
            out_ref[...] = acc

    vmem = pl.BlockSpec(memory_space=pltpu.VMEM)
    scratch = [pltpu.SemaphoreType.DMA((7,)), pltpu.SemaphoreType.DMA((7,)), pltpu.SemaphoreType.DMA]
    if reduce:
        scratch = [pltpu.VMEM((N_DEV * rr, nn), F32)] + scratch
    out_rows = rr if reduce else N_DEV * rr
    return pl.pallas_call(
        body, in_specs=[vmem], out_specs=vmem, out_shape=jax.ShapeDtypeStruct((out_rows, nn), F32),
        scratch_shapes=scratch, name=name, compiler_params=_params())(block)


def _sum_stack(stack, name):
    rr = stack.shape[0] // N_DEV

    def body(s_ref, o_ref):
        acc = s_ref[0:rr, :]
        for k in range(1, N_DEV):
            acc = acc + s_ref[k * rr:(k + 1) * rr, :]
        o_ref[...] = acc

    vmem = pl.BlockSpec(memory_space=pltpu.VMEM)
    return pl.pallas_call(body, in_specs=[vmem], out_specs=vmem,
                          out_shape=jax.ShapeDtypeStruct((rr, stack.shape[1]), F32), name=name,
                          compiler_params=_params())(stack)


def _presum(where, grad, swapped, name):
    s, d = swapped.shape[1], swapped.shape[2]
    tc = _tile(d, 512, LANE)

    def body(where_ref, g_ref, sw_ref, o_ref):
        o_ref[0] = (g_ref[...].astype(F32) + sw_ref[0].astype(F32)).astype(o_ref.dtype)

    return _call(
        body, grid=(3, d // tc),
        in_specs=[pl.BlockSpec((s, tc), lambda r, j, where: (where[1 + r], j)),
                  pl.BlockSpec((1, s, tc), lambda r, j, where: (where[5 + r], 0, j))],
        out_specs=pl.BlockSpec((1, s, tc), lambda r, j, where: (r, 0, j)),
        out_shape=jax.ShapeDtypeStruct((3, s, d), WIRE_DTYPE), name=name, args=(grad, swapped), prefetch=(where,))


def _final_sum(where, grad, swapped, received, name, carried=()):
    s, d = swapped.shape[1], swapped.shape[2]
    tc = _tile(d, 512, LANE)

    def body(where_ref, g_ref, sw_ref, r_ref, o_ref):
        acc = g_ref[...].astype(F32) + sw_ref[0].astype(F32)
        for k in range(3):
            acc = acc + r_ref[k].astype(F32)
        o_ref[...] = acc

    return _call(
        body, grid=(d // tc,),
        in_specs=[pl.BlockSpec((s, tc), lambda j, where: (where[0], j)),
                  pl.BlockSpec((1, s, tc), lambda j, where: (where[4], 0, j)),
                  pl.BlockSpec((3, s, tc), lambda j, where: (0, 0, j))],
        out_specs=pl.BlockSpec((s, tc), lambda j, where: (0, j)),
        out_shape=jax.ShapeDtypeStruct((s, d), F32), name=name, args=(grad, swapped, received),
        prefetch=(where,), carried=carried)


class _GradReduction:
    def __init__(self, key, grad, shard_rows, where):
        self.key, self.grad, self.rows, self.where = key, grad, shard_rows, where
        self._presum = self._exchange = None

    def swap(self):
        self._swap = _CarriedSwap([self.grad], [self.rows])
        return self._swap

    def exchange(self, part=None):
        if self._presum is None:
            self._presum = _presum(self.where, self.grad, self._swap.results[0], "presum_" + self.key)
        rows = None
        if part is not None:
            half = _round_up(self.rows // 2, SUBLANE_BF16)
            rows = (0, half) if part == 0 else (half, self.rows - half)
        into = self._exchange.results[0] if part == 1 else None
        self._exchange = _CarriedChipExchange([self._presum], rows, into)
        return self._exchange

    def total(self, carried=()):
        return _final_sum(self.where, self.grad, self._swap.results[0], self._exchange.results[0],
                          "sum_" + self.key, carried)

    def total_and_update(self, w, m, v):
        return _sum_adamw(self.where, self.grad, self._swap.results[0], self._exchange.results[0], w, m, v,
                          "update_" + self.key)


def _adamw_math(w, g, m, v):
    nm = ADAM_B1 * m + (1.0 - ADAM_B1) * g
    nv = ADAM_B2 * v + (1.0 - ADAM_B2) * (g * g)
    m_hat = nm / (1.0 - ADAM_B1 ** ADAM_STEP)
    v_hat = nv / (1.0 - ADAM_B2 ** ADAM_STEP)
    return -ADAM_LR * (m_hat / (jnp.sqrt(v_hat) + ADAM_EPS) + ADAM_WD * w), nm, nv


def _sum_adamw(where, grad, swapped, received, w, m, v, name):
    s, d = swapped.shape[1], swapped.shape[2]
    tc = _tile(d, 512, LANE)

    def body(where_ref, g_ref, sw_ref, r_ref, w_ref, m_ref, v_ref, gs_ref, d_ref, nm_ref, nv_ref):
        g = g_ref[...].astype(F32) + sw_ref[0].astype(F32)
        for k in range(3):
            g = g + r_ref[k].astype(F32)
        gs_ref[...] = g
        d_ref[...], nm_ref[...], nv_ref[...] = _adamw_math(w_ref[...], g, m_ref[...], v_ref[...])

    blk = pl.BlockSpec((s, tc), lambda j, where: (0, j))
    return _call(
        body, grid=(d // tc,),
        in_specs=[pl.BlockSpec((s, tc), lambda j, where: (where[0], j)),
                  pl.BlockSpec((1, s, tc), lambda j, where: (where[4], 0, j)),
                  pl.BlockSpec((3, s, tc), lambda j, where: (0, 0, j)), blk, blk, blk],
        out_specs=[blk] * 4, out_shape=[jax.ShapeDtypeStruct((s, d), F32)] * 4, name=name,
        args=(grad, swapped, received, w, m, v), prefetch=(where,))


def _adamw(w, g, m, v, name):
    rows, cols = w.shape
    tr = _tile(rows, 256, 8)

    def body(w_ref, g_ref, m_ref, v_ref, d_ref, nm_ref, nv_ref):
        d_ref[...], nm_ref[...], nv_ref[...] = _adamw_math(w_ref[...], g_ref[...], m_ref[...], v_ref[...])

    spec = pl.BlockSpec((tr, cols), lambda i: (i, 0))
    return pl.pallas_call(
        body, grid=(rows // tr,), in_specs=[spec] * 4, out_specs=[spec] * 3,
        out_shape=[jax.ShapeDtypeStruct((rows, cols), F32)] * 3, name=name, compiler_params=_params())(w, g, m, v)


def _pack_rows(arrays, width, row_quantum=8):
    flat = jnp.concatenate([a.reshape(-1) for a in arrays])
    total = _round_up(flat.shape[0], row_quantum * width)
    flat = jnp.pad(flat, (0, total - flat.shape[0]))
    return flat.reshape(-1, width)


def _unpack_rows(packed, shapes):
    flat = packed.reshape(-1)
    out = []
    off = 0
    for shp in shapes:
        size = 1
        for s in shp:
            size *= s
        out.append(flat[off:off + size].reshape(shp))
        off += size
    return out


def _block_diag(w):
    h, hb, _ = w.shape
    per = BD // hb
    w4 = w.reshape(h // per, per, hb, hb)
    eye = jnp.eye(per, dtype=w.dtype)
    return jnp.einsum('npij,pq->npiqj', w4, eye).reshape(h // per, BD, BD)


def _block_diag_extract(bd, hb):
    nbk = bd.shape[0]
    per = BD // hb
    b5 = bd.reshape(nbk, per, hb, per, hb)
    eye = jnp.eye(per, dtype=bd.dtype)
    return jnp.einsum('npiqj,pq->npij', b5, eye).reshape(nbk * per, hb, hb)


def kernel(x, meta_tokens, ffn1_pre_g, ffn1_w_gate, ffn1_w_up, ffn1_w_down, ffn1_post_g, mix_pre_g, w_in, lru_conv_w, lru_conv_b, lru_w_a, lru_b_a, lru_w_x, lru_b_x, lru_lambda, sconv_w, lru_out_g, sconv_out_g, w_out, mix_post_g, ffn2_pre_g, ffn2_w_gate, ffn2_w_up, ffn2_w_down, ffn2_post_g, loss_target, m_meta_tokens, m_ffn1_pre_g, m_ffn1_w_gate, m_ffn1_w_up, m_ffn1_w_down, m_ffn1_post_g, m_mix_pre_g, m_w_in, m_lru_conv_w, m_lru_conv_b, m_lru_w_a, m_lru_b_a, m_lru_w_x, m_lru_b_x, m_lru_lambda, m_sconv_w, m_lru_out_g, m_sconv_out_g, m_w_out, m_mix_post_g, m_ffn2_pre_g, m_ffn2_w_gate, m_ffn2_w_up, m_ffn2_w_down, m_ffn2_post_g, v_meta_tokens, v_ffn1_pre_g, v_ffn1_w_gate, v_ffn1_w_up, v_ffn1_w_down, v_ffn1_post_g, v_mix_pre_g, v_w_in, v_lru_conv_w, v_lru_conv_b, v_lru_w_a, v_lru_b_a, v_lru_w_x, v_lru_b_x, v_lru_lambda, v_sconv_w, v_lru_out_g, v_sconv_out_g, v_w_out, v_mix_post_g, v_ffn2_pre_g, v_ffn2_w_gate, v_ffn2_w_up, v_ffn2_w_down, v_ffn2_post_g):
    given = dict(locals())
    wts = {n: given[n] for n in WEIGHT_NAMES}
    mom = {n: given["m_" + n] for n in WEIGHT_NAMES}
    var = {n: given["v_" + n] for n in WEIGHT_NAMES}

    xi, yi, ci = _position()
    me = _block_of(xi, yi, ci)
    x2 = x[0]
    seq, d = x2.shape
    n_meta = meta_tokens.shape[0]
    m_rows = _round_up(n_meta + seq, ROW_ALIGN)
    pad = m_rows - n_meta - seq
    lead = pad + n_meta
    c = lru_conv_b.shape[1]
    hb = lru_w_a.shape[-1]
    dm = meta_tokens.shape[1]
    cs_ = lru_conv_w.shape[2]
    kw4, kw3 = lru_conv_w.shape[1], sconv_w.shape[1]
    assert d == 2 * c and c % BD == 0 and BD % hb == 0 and cs_ <= dm and kw4 == 4 and kw3 == 3

    small = jnp.zeros((_round_up(n_meta + kw4 + kw3, 8), dm), F32)
    small = small.at[0:n_meta].set(meta_tokens)
    small = small.at[n_meta:n_meta + kw4, 0:cs_].set(lru_conv_w[0])
    small = small.at[n_meta + kw4:n_meta + kw4 + kw3, 0:cs_].set(sconv_w[0])
    sr = small.shape[0]
    small_all = _gather_small(small, False, "gather_small").reshape(N_DEV, sr, dm)
    meta_full = small_all[:, 0:n_meta, :].transpose(1, 0, 2).reshape(n_meta, d)
    conv_w_full = small_all[:, n_meta:n_meta + kw4, 0:cs_].transpose(1, 0, 2).reshape(kw4, c)
    sconv_w_full = small_all[:, n_meta + kw4:n_meta + kw4 + kw3, 0:cs_].transpose(1, 0, 2).reshape(kw3, c)

    big = ['ffn1_w_gate', 'ffn1_w_up', 'ffn1_w_down', 'w_in', 'w_out', 'ffn2_w_gate', 'ffn2_w_up', 'ffn2_w_down']
    col_sharded = {'ffn1_w_gate', 'ffn1_w_up', 'w_in', 'ffn2_w_gate', 'ffn2_w_up'}
    shards = []
    for nme in big:
        w = wts[nme][0].astype(WIRE_DTYPE)
        shards.append(w.T if nme in col_sharded else w)
    shard_rows = dict(zip(big, [s.shape[0] for s in shards]))
    zeros = jnp.zeros((F_ALIGN, d), WIRE_DTYPE)

    def gather(forward_at, *names, part=None, into=None):
        sel = [shards[big.index(nme)] for nme in names]
        padded = [_round_up(N_DEV * shard_rows[nme], LANE if nme in ('w_in', 'w_out') else F_ALIGN) for nme in names]
        return _CarriedGather(sel, padded, zeros, forward_at, part, into)

    pv = jnp.zeros((16, c), F32)
    pv = pv.at[0:4].set(conv_w_full).at[4].set(lru_conv_b[0]).at[5].set(lru_b_a[0]).at[6].set(lru_b_x[0])
    pv = pv.at[7].set(lru_lambda[0]).at[8:11].set(sconv_w_full).at[11].set(lru_out_g[0]).at[12].set(sconv_out_g[0])
    wa_bd = _block_diag(lru_w_a[0]).astype(MXU_DTYPE)
    wx_bd = _block_diag(lru_w_x[0]).astype(MXU_DTYPE)
    gs = c // N_GROUPS
    gidx = jnp.arange(BD) // gs
    gm = jnp.where(gidx[:, None] == gidx[None, :], 1.0 / gs, 0.0).astype(MXU_DTYPE)

    ride = gather(1.0, 'ffn1_w_gate')
    h0, n1, target = _embed(x2, meta_full, loss_target[0], ffn1_pre_g, pad, "embed_prenorm", carried=[ride])
    (wg1,) = ride.results
    ride = gather(1.0, 'ffn1_w_up')
    g1 = _mm_nt(n1, wg1, "ffn1_gate", carried=[ride], out_dtype=MXU_DTYPE)
    (wu1,) = ride.results
    ride = gather(1.0, 'ffn1_w_down')
    u1, a1 = _ffn_up_act(n1, wu1, g1, "ffn1_up_act", carried=[ride])
    (wd1,) = ride.results
    ride = gather(0.9, 'w_in', 'w_out')
    fo1, h1, un = _mm_residual_norm(a1, wd1, h0, ffn1_post_g, 0.5, mix_pre_g, "ffn1_down", carried=[ride])
    win_t, wout = ride.results
    s2 = shard_rows['ffn2_w_gate']
    quarter = _round_up(s2 // 4, SUBLANE_BF16)
    ride_g = gather(1.0, 'ffn2_w_gate', part=(0, 3 * quarter))
    z = _mm_nt(un, win_t, "mix_in_proj", carried=[ride_g])
    ride_g = gather(1.0, 'ffn2_w_gate', part=(3 * quarter, s2 - 3 * quarter), into=ride_g.results[0])
    ride_u = gather(1.0, 'ffn2_w_up', part=(0, quarter))
    mixed, hs = _mixer_fwd(z, pv, wa_bd, wx_bd, gm, pad, "mixer_fwd", carried=[ride_g, ride_u])
    (wg2,) = ride_g.results
    ride_u = gather(1.0, 'ffn2_w_up', part=(quarter, s2 - quarter), into=ride_u.results[0])
    o_mix, h2, n2 = _mm_residual_norm(mixed, wout, h1, mix_post_g, 1.0, ffn2_pre_g, "mix_out_proj", carried=[ride_u])
    (wu2,) = ride_u.results
    ride = gather(0.75, 'ffn2_w_down')
    g2, u2, a2 = _ffn_gate_up(n2, wg2, wu2, "ffn2_gate_up", carried=[ride])
    (wd2,) = ride.results
    dh3, dfo2, d_post2, loss_part = _mm_residual_loss(a2, wd2, h2, ffn2_post_g, 0.5, target, lead, "ffn2_down_loss")
    loss = lax.psum(loss_part[0, 0], ("x", "y", "c"))

    chip_rel = [2 * (xi ^ (r >> 1)) + (yi ^ (r & 1)) for r in range(4)]
    where = jnp.stack([2 * k + ci for k in chip_rel] + chip_rel).astype(jnp.int32)
    red = {}

    def reduction(nme, grad):
        red[nme] = _GradReduction(nme, grad, shard_rows[nme], where)
        return red[nme]

    r_wd2 = reduction('ffn2_w_down', _mm_tn(a2, dfo2, "ffn2_dw_down"))
    dg2, du2 = _ffn_hidden_bwd(dfo2, wd2, g2, u2, "ffn2_hidden_bwd", carried=[r_wd2.swap()])
    r_wg2 = reduction('ffn2_w_gate', _mm_tn(dg2, n2, "ffn2_dw_gate", carried=[r_wd2.exchange(part=0)]))
    r_wu2 = reduction('ffn2_w_up', _mm_tn(du2, n2, "ffn2_dw_up", carried=[r_wd2.exchange(part=1), r_wg2.swap()]))
    dh2, d_pre2 = _mm_norm_bwd([(dg2, wg2), (du2, wu2)], h2, ffn2_pre_g, dh3, "ffn2_dx",
                               carried=[r_wg2.exchange(), r_wu2.swap()])
    do_mix, d_mix_post, dmixed = _norm_bwd_mm_nt(o_mix, mix_post_g, dh2, 1.0, wout, "mix_out_proj_bwd")
    r_wout = reduction('w_out', _mm_tn(mixed, do_mix, "mix_dw_out"))
    dz, dpv, dwa_bd, dwx_bd = _mixer_bwd(z, hs, dmixed, pv, wa_bd, wx_bd, gm, pad, "mixer_bwd",
                                         carried=[r_wu2.exchange(), r_wout.swap()])
    r_win = reduction('w_in', _mm_tn(dz, un, "mix_dw_in", carried=[r_wout.exchange()]))
    dh1, d_mix_pre, dfo1, d_post1 = _mm_norm_bwd([(dz, win_t)], h1, mix_pre_g, dh2, "mix_dx", carried=[r_win.swap()],
                                                 post=(fo1, ffn1_post_g, 0.5))
    r_wd1 = reduction('ffn1_w_down', _mm_tn(a1, dfo1, "ffn1_dw_down", carried=[r_win.exchange()]))
    early_names = ['mix_pre_g', 'mix_post_g', 'ffn2_pre_g', 'ffn2_post_g', 'ffn1_post_g',
                   'lru_conv_b', 'lru_b_a', 'lru_b_x', 'lru_lambda', 'lru_out_g', 'sconv_out_g',
                   'lru_conv_w', 'sconv_w', 'lru_w_a', 'lru_w_x']
    early_parts = [d_mix_pre, d_mix_post, d_pre2, d_post2, d_post1,
                   dpv[4:5], dpv[5:6], dpv[6:7], dpv[7:8], dpv[11:12], dpv[12:13],
                   dpv[0:4], dpv[8:11], _block_diag_extract(dwa_bd, hb), _block_diag_extract(dwx_bd, hb)]
    early_packed = _pack_rows(early_parts, d, SUBLANE_BF16)
    early_ride = _CarriedGather([early_packed], [N_DEV * early_packed.shape[0]], zeros, 0.75)
    dg1, du1 = _ffn_hidden_bwd(dfo1, wd1, g1, u1, "ffn1_hidden_bwd", carried=[r_wd1.swap(), early_ride])
    early_sum = _sum_stack(early_ride.results[0], "sum_small_early")
    r_wg1 = reduction('ffn1_w_gate', _mm_tn(dg1, n1, "ffn1_dw_gate", carried=[r_wd1.exchange(part=0)]))
    r_wu1 = reduction('ffn1_w_up', _mm_tn(du1, n1, "ffn1_dw_up", carried=[r_wd1.exchange(part=1), r_wg1.swap()]))
    row_tile = _norm_bwd_row_tile(m_rows)
    n_tiles = m_rows // row_tile
    half = n_tiles // 2
    assert half >= 1 and half * row_tile >= lead
    dh0_a, d_pre1_a = _mm_norm_bwd([(dg1, wg1), (du1, wu1)], h0, ffn1_pre_g, dh1, "ffn1_dx_a",
                                   carried=[r_wg1.exchange(), r_wu1.swap()], row_tiles=(0, half))
    dh0_b, d_pre1 = _mm_norm_bwd([(dg1, wg1), (du1, wu1)], h0, ffn1_pre_g, dh1, "ffn1_dx_b",
                                 carried=[r_wu1.exchange()], row_tiles=(half, n_tiles - half), dg_init=d_pre1_a)
    grad_x = jnp.concatenate([dh0_a[lead:], dh0_b], axis=0)[None]
    d_meta = dh0_a[pad:lead]

    grads, delta, new_m, new_v = {}, {}, {}, {}
    for nme in big:
        in_shard_layout = nme not in col_sharded or shard_rows[nme] % LANE != 0
        if in_shard_layout:
            view = (lambda t: t[0].T) if nme in col_sharded else (lambda t: t[0])
            back = (lambda t: t.T[None]) if nme in col_sharded else (lambda t: t[None])
            outs = red[nme].total_and_update(view(wts[nme]), view(mom[nme]), view(var[nme]))
            grads[nme], delta[nme], new_m[nme], new_v[nme] = [back(t) for t in outs]
        else:
            grads[nme] = red[nme].total().T[None]
            outs = _adamw(wts[nme][0], grads[nme][0], mom[nme][0], var[nme][0], "adamw_" + nme)
            delta[nme], new_m[nme], new_v[nme] = [t[None] for t in outs]

    late_names = ['ffn1_pre_g', 'meta_tokens']
    late_parts = [d_pre1, d_meta]
    late_sum = _gather_small(_pack_rows(late_parts, d), True, "reduce_small_late")
    small_sums = (_unpack_rows(early_sum, [p.shape for p in early_parts])
                  + _unpack_rows(late_sum, [p.shape for p in late_parts]))
    for nme, gsm in zip(early_names + late_names, small_sums):
        if nme == 'meta_tokens':
            grads[nme] = lax.dynamic_slice_in_dim(gsm, me * dm, dm, axis=1)
        elif nme in ('lru_conv_w', 'sconv_w'):
            grads[nme] = lax.dynamic_slice_in_dim(gsm, me * cs_, cs_, axis=1)[None]
        else:
            grads[nme] = gsm.reshape(wts[nme].shape)

    rest = [n for n in WEIGHT_NAMES if n not in big]
    rest_shapes = [wts[n].shape for n in rest]
    packed = [_pack_rows([src[n] for n in rest], LANE, 256) for src in (wts, grads, mom, var)]
    for out, packed_out in zip((delta, new_m, new_v), _adamw(*packed, "adamw_small")):
        for nme, arr in zip(rest, _unpack_rows(packed_out, rest_shapes)):
            out[nme] = arr

    return (loss, grad_x, *[grads[n] for n in WEIGHT_NAMES], *[delta[n] for n in WEIGHT_NAMES],
            *[new_m[n] for n in WEIGHT_NAMES], *[new_v[n] for n in WEIGHT_NAMES])
```

```python
import functools

import jax
import jax.numpy as jnp
from jax import lax
from jax.experimental import pallas as pl
from jax.experimental.pallas import tpu as pltpu

F32 = jnp.float32
MXU_DTYPE = jnp.bfloat16
WIRE_DTYPE = jnp.bfloat16
MESH = pl.DeviceIdType.MESH

EPS = 1e-6
LRU_C = 8.0
N_GROUPS = 16
ADAM_LR = 0.001
ADAM_B1 = 0.9
ADAM_B2 = 0.999
ADAM_EPS = 1e-08
ADAM_WD = 0.01
ADAM_STEP = 10

N_DEV = 8
LANE = 128
SUBLANE_BF16 = 16
ROW_ALIGN = 128
F_ALIGN = 512
BD = 256
K_TILE = 512
ACC_ROWS = 528
ACC_GROUP = 1
MIX_ROWS = 128
VMEM_LIMIT_MB = 56

WEIGHT_NAMES = ['meta_tokens', 'ffn1_pre_g', 'ffn1_w_gate', 'ffn1_w_up', 'ffn1_w_down', 'ffn1_post_g',
                'mix_pre_g', 'w_in', 'lru_conv_w', 'lru_conv_b', 'lru_w_a', 'lru_b_a', 'lru_w_x', 'lru_b_x',
                'lru_lambda', 'sconv_w', 'lru_out_g', 'sconv_out_g', 'w_out', 'mix_post_g', 'ffn2_pre_g',
                'ffn2_w_gate', 'ffn2_w_up', 'ffn2_w_down', 'ffn2_post_g']


def _round_up(n, q):
    return (n + q - 1) // q * q


def _tile(n, target, q):
    best = None
    t = q
    while t <= min(n, target):
        if n % t == 0:
            best = t
        t += q
    assert best is not None, (n, target, q)
    return best


def _params(**kw):
    return pltpu.CompilerParams(vmem_limit_bytes=VMEM_LIMIT_MB << 20, **kw)


def _call(body, *, grid, in_specs, out_specs, out_shape, name, args, scratch_shapes=(), carried=(), prefetch=()):
    carried = list(carried)
    n_pf = len(prefetch)

    def launch(fn, in_specs_, out_specs_, out_shape_, scratch_, operands, aliases_):
        if n_pf:
            spec = pltpu.PrefetchScalarGridSpec(num_scalar_prefetch=n_pf, grid=grid, in_specs=in_specs_,
                                                out_specs=out_specs_, scratch_shapes=scratch_)
            return pl.pallas_call(fn, grid_spec=spec, out_shape=out_shape_, input_output_aliases=aliases_,
                                  name=name, compiler_params=_params())(*prefetch, *operands)
        return pl.pallas_call(fn, grid=grid, in_specs=in_specs_, out_specs=out_specs_, out_shape=out_shape_,
                              scratch_shapes=scratch_, input_output_aliases=aliases_, name=name,
                              compiler_params=_params())(*operands)

    if not carried:
        return launch(body, in_specs, out_specs, out_shape, list(scratch_shapes), args, {})
    single = not isinstance(out_shape, (list, tuple))
    out_specs_l = [out_specs] if single else list(out_specs)
    out_shape_l = [out_shape] if single else list(out_shape)
    n_in, n_out, n_scr = len(in_specs), len(out_specs_l), len(scratch_shapes)
    hbm = pl.BlockSpec(memory_space=pl.ANY)
    c_in = [a for cm in carried for a in cm.arrays]
    c_out = [s for cm in carried for s in cm.out_shapes]
    c_scr = []
    aliases = {}
    in_off, out_off = n_pf + n_in, n_out
    for cm in carried:
        c_scr += [pltpu.SemaphoreType.DMA((cm.n_remote,)), pltpu.SemaphoreType.DMA((cm.n_remote,)),
                  pltpu.SemaphoreType.DMA((max(cm.n_local, 1),))]
        for k, v in cm.aliases.items():
            aliases[in_off + k] = out_off + v
        in_off += len(cm.arrays)
        out_off += len(cm.out_shapes)
    steps = 1
    for g in grid:
        steps *= g
    forward_steps = [min(int(cm.forward_at * steps), steps - 1) for cm in carried]

    def wrapped(*refs):
        pf = refs[:n_pf]
        p = n_pf
        ins = refs[p:p + n_in]
        p += n_in
        cins = refs[p:p + len(c_in)]
        p += len(c_in)
        outs = refs[p:p + n_out]
        p += n_out
        couts = refs[p:p + len(c_out)]
        p += len(c_out)
        scr = refs[p:p + n_scr]
        csem = refs[p + n_scr:]
        lin = 0
        for axis, g in enumerate(grid):
            lin = lin * g + pl.program_id(axis)
        views = []
        io = oo = 0
        for j, cm in enumerate(carried):
            views.append((cins[io:io + len(cm.arrays)], couts[oo:oo + len(cm.out_shapes)],
                          csem[3 * j], csem[3 * j + 1], csem[3 * j + 2]))
            io += len(cm.arrays)
            oo += len(cm.out_shapes)

        @pl.when(lin == 0)
        def _():
            for cm, v in zip(carried, views):
                cm.start(*v)

        body(*pf, *ins, *outs, *scr)

        for cm, v, step in zip(carried, views, forward_steps):
            pl.when(lin == step)(functools.partial(cm.forward, *v))

        @pl.when(lin == steps - 1)
        def _():
            for cm, v in zip(carried, views):
                cm.finish(*v)

    res = launch(wrapped, list(in_specs) + [hbm] * len(c_in), out_specs_l + [hbm] * len(c_out),
                 out_shape_l + c_out, list(scratch_shapes) + c_scr, (*args, *c_in), aliases)
    oo = n_out
    for cm in carried:
        cm.results = list(res[oo:oo + len(cm.out_shapes)])
        oo += len(cm.out_shapes)
    return res[0] if single else list(res[:n_out])


def _embed(x, meta, target, g, pad, name, carried=()):
    seq, d = x.shape
    n_meta = meta.shape[0]
    lead = pad + n_meta
    m = lead + seq
    tr = ROW_ALIGN
    lead_blocks = lead // tr
    meta_row = pad - (lead_blocks - 1) * tr
    assert lead % tr == 0 and seq % tr == 0 and 0 <= meta_row and meta_row % 8 == 0

    def body(x_ref, meta_ref, t_ref, g_ref, h_ref, n_ref, tp_ref):
        i = pl.program_id(0)

        @pl.when(i < lead_blocks)
        def _():
            h_ref[...] = jnp.zeros_like(h_ref)
            tp_ref[...] = jnp.zeros_like(tp_ref)

        @pl.when(i == lead_blocks - 1)
        def _():
            h_ref[pl.ds(meta_row, n_meta), :] = meta_ref[...]

        @pl.when(i >= lead_blocks)
        def _():
            h_ref[...] = x_ref[...]
            tp_ref[...] = t_ref[...]

        h = h_ref[...]
        r = lax.rsqrt(jnp.mean(h * h, axis=-1, keepdims=True) + EPS)
        n_ref[...] = (h * r * g_ref[...]).astype(n_ref.dtype)

    tokens = pl.BlockSpec((tr, d), lambda i: (jnp.maximum(i - lead_blocks, 0), 0))
    rows = pl.BlockSpec((tr, d), lambda i: (i, 0))
    return _call(
        body, grid=(m // tr,),
        in_specs=[tokens, pl.BlockSpec((n_meta, d), lambda i: (0, 0)), tokens, pl.BlockSpec((1, d), lambda i: (0, 0))],
        out_specs=[rows, rows, rows],
        out_shape=[jax.ShapeDtypeStruct((m, d), F32), jax.ShapeDtypeStruct((m, d), MXU_DTYPE),
                   jax.ShapeDtypeStruct((m, d), F32)],
        name=name, args=(x, meta, target, g), carried=carried)


def _rmsnorm_bwd_rows(x, g, dy):
    r = lax.rsqrt(jnp.mean(x * x, axis=-1, keepdims=True) + EPS)
    xh = x * r
    dyh = dy * g
    dx = r * (dyh - xh * jnp.mean(dyh * xh, axis=-1, keepdims=True))
    return dx, dy * xh


def _dot_nt(a, b):
    return lax.dot_general(a, b, (((1,), (1,)), ((), ())), preferred_element_type=F32)


def _dot_tn(a, b):
    return lax.dot_general(a, b, (((0,), (0,)), ((), ())), preferred_element_type=F32)


def _mm_nt(a, w, name, carried=(), out_dtype=F32):
    m, k = a.shape
    n = w.shape[0]
    tm = _tile(m, 1056, SUBLANE_BF16)
    tn = _tile(n, 512, LANE)

    def body(a_ref, w_ref, o_ref):
        o_ref[...] = _dot_nt(a_ref[...], w_ref[...]).astype(o_ref.dtype)

    return _call(
        body, grid=(m // tm, n // tn),
        in_specs=[pl.BlockSpec((tm, k), lambda i, j: (i, 0)), pl.BlockSpec((tn, k), lambda i, j: (j, 0))],
        out_specs=pl.BlockSpec((tm, tn), lambda i, j: (i, j)),
        out_shape=jax.ShapeDtypeStruct((m, n), out_dtype), name=name, args=(a, w), carried=carried)


def _norm_bwd_mm_nt(x, g, dy, scale, w, name, carried=()):
    m, d = x.shape
    n = w.shape[0]
    tm = _tile(m, 528, SUBLANE_BF16)

    def body(x_ref, g_ref, dy_ref, w_ref, dx_ref, dg_ref, o_ref):
        @pl.when(pl.program_id(0) == 0)
        def _():
            dg_ref[...] = jnp.zeros_like(dg_ref)

        dx, dgr = _rmsnorm_bwd_rows(x_ref[...], g_ref[...], scale * dy_ref[...])
        dxb = dx.astype(dx_ref.dtype)
        dx_ref[...] = dxb
        dg_ref[...] += jnp.sum(dgr, axis=0, keepdims=True)
        o_ref[...] = _dot_nt(dxb, w_ref[...])

    row = pl.BlockSpec((tm, d), lambda i: (i, 0))
    vec = pl.BlockSpec((1, d), lambda i: (0, 0))
    return _call(
        body, grid=(m // tm,),
        in_specs=[row, vec, row, pl.BlockSpec((n, d), lambda i: (0, 0), pipeline_mode=pl.Buffered(1))],
        out_specs=[row, vec, pl.BlockSpec((tm, n), lambda i: (i, 0))],
        out_shape=[jax.ShapeDtypeStruct((m, d), MXU_DTYPE), jax.ShapeDtypeStruct((1, d), F32),
                   jax.ShapeDtypeStruct((m, n), F32)],
        name=name, args=(x, g, dy, w), carried=carried)


def _ffn_up_act(n_act, wu_t, g_act, name, carried=()):
    m, d = n_act.shape
    fp = wu_t.shape[0]
    tm = _tile(m, 1056, SUBLANE_BF16)
    tn = _tile(fp, 512, LANE)

    def body(n_ref, wu_ref, g_ref, u_ref, a_ref):
        u = _dot_nt(n_ref[...], wu_ref[...])
        g = g_ref[...].astype(F32)
        u_ref[...] = u.astype(u_ref.dtype)
        a_ref[...] = (g * jax.nn.sigmoid(g) * u).astype(a_ref.dtype)

    act = pl.BlockSpec((tm, tn), lambda i, j: (i, j))
    return _call(
        body, grid=(m // tm, fp // tn),
        in_specs=[pl.BlockSpec((tm, d), lambda i, j: (i, 0)), pl.BlockSpec((tn, d), lambda i, j: (j, 0)), act],
        out_specs=[act, act],
        out_shape=[jax.ShapeDtypeStruct((m, fp), MXU_DTYPE)] * 2, name=name, args=(n_act, wu_t, g_act), carried=carried)


def _ffn_gate_up(n_act, wg_t, wu_t, name, carried=()):
    m, d = n_act.shape
    fp = wg_t.shape[0]
    tm = _tile(m, 1056, SUBLANE_BF16)
    tn = _tile(fp, 512, LANE)

    def body(n_ref, wg_ref, wu_ref, g_ref, u_ref, a_ref):
        n = n_ref[...]
        g = _dot_nt(n, wg_ref[...])
        u = _dot_nt(n, wu_ref[...])
        g_ref[...] = g.astype(g_ref.dtype)
        u_ref[...] = u.astype(u_ref.dtype)
        a_ref[...] = (g * jax.nn.sigmoid(g) * u).astype(a_ref.dtype)

    act = pl.BlockSpec((tm, tn), lambda i, j: (i, j))
    wsp = pl.BlockSpec((tn, d), lambda i, j: (j, 0))
    return _call(
        body, grid=(m // tm, fp // tn),
        in_specs=[pl.BlockSpec((tm, d), lambda i, j: (i, 0)), wsp, wsp],
        out_specs=[act, act, act],
        out_shape=[jax.ShapeDtypeStruct((m, fp), MXU_DTYPE)] * 3, name=name, args=(n_act, wg_t, wu_t), carried=carried)


def _ffn_hidden_bwd(dfo, wd, g_act, u_act, name, carried=()):
    m, d = dfo.shape
    fp = wd.shape[0]
    tm = _tile(m, 1056, SUBLANE_BF16)
    tn = _tile(fp, 512, LANE)

    def body(df_ref, wd_ref, g_ref, u_ref, dg_ref, du_ref):
        da = _dot_nt(df_ref[...], wd_ref[...])
        g = g_ref[...].astype(F32)
        u = u_ref[...].astype(F32)
        s = jax.nn.sigmoid(g)
        du_ref[...] = (da * (g * s)).astype(du_ref.dtype)
        dg_ref[...] = (da * u * (s * (1.0 + g * (1.0 - s)))).astype(dg_ref.dtype)

    act = pl.BlockSpec((tm, tn), lambda i, j: (i, j))
    return _call(
        body, grid=(m // tm, fp // tn),
        in_specs=[pl.BlockSpec((tm, d), lambda i, j: (i, 0)), pl.BlockSpec((tn, d), lambda i, j: (j, 0)), act, act],
        out_specs=[act, act],
        out_shape=[jax.ShapeDtypeStruct((m, fp), MXU_DTYPE)] * 2, name=name, args=(dfo, wd, g_act, u_act),
        carried=carried)


def _row_groups(n_tiles, max_group, nk):
    gsz = max(q for q in range(1, max_group + 1) if n_tiles % q == 0)

    def epilogue_row(grp, kk, i):
        return grp * gsz + jnp.where(kk == nk - 1, i, 0)

    return gsz, epilogue_row


def _mm_residual_norm(a, w, h, g, scale, next_g, name, carried=()):
    m, k = a.shape
    d = w.shape[1]
    tm = _tile(m, ACC_ROWS, SUBLANE_BF16)
    tk = _tile(k, K_TILE, LANE)
    nk = k // tk
    gsz, epilogue_row = _row_groups(m // tm, ACC_GROUP, nk)

    def body(a_ref, w_ref, h_ref, g_ref, ng_ref, fo_ref, hn_ref, nn_ref, acc_ref):
        kk, i = pl.program_id(1), pl.program_id(2)

        @pl.when(kk == 0)
        def _():
            acc_ref[i] = jnp.zeros((tm, d), F32)

        acc_ref[i] += jnp.dot(a_ref[...], w_ref[...], preferred_element_type=F32)

        @pl.when(kk == nk - 1)
        def _():
            fo = acc_ref[i]
            fo_ref[...] = fo
            r = lax.rsqrt(jnp.mean(fo * fo, axis=-1, keepdims=True) + EPS)
            hn = h_ref[...] + scale * (fo * r * g_ref[...])
            hn_ref[...] = hn
            rn = lax.rsqrt(jnp.mean(hn * hn, axis=-1, keepdims=True) + EPS)
            nn_ref[...] = (hn * rn * ng_ref[...]).astype(nn_ref.dtype)

    row = pl.BlockSpec((tm, d), lambda grp, kk, i: (epilogue_row(grp, kk, i), 0))
    row_once = pl.BlockSpec((tm, d), lambda grp, kk, i: (epilogue_row(grp, kk, i), 0), pipeline_mode=pl.Buffered(1))
    vec = pl.BlockSpec((1, d), lambda grp, kk, i: (0, 0))
    return _call(
        body, grid=(m // tm // gsz, nk, gsz),
        in_specs=[pl.BlockSpec((tm, tk), lambda grp, kk, i: (grp * gsz + i, kk)),
                  pl.BlockSpec((tk, d), lambda grp, kk, i: (kk, 0)), row_once, vec, vec],
        out_specs=[row, row, row],
        out_shape=[jax.ShapeDtypeStruct((m, d), F32)] * 2 + [jax.ShapeDtypeStruct((m, d), MXU_DTYPE)],
        scratch_shapes=[pltpu.VMEM((gsz, tm, d), F32)], name=name, args=(a, w, h, g, next_g), carried=carried)


def _mm_residual_loss(a, w, h, g, scale, target, lead, name, carried=()):
    m, k = a.shape
    d = w.shape[1]
    tm = _tile(m, ACC_ROWS, SUBLANE_BF16)
    tk = _tile(k, K_TILE, LANE)
    nk = k // tk
    gsz, epilogue_row = _row_groups(m // tm, ACC_GROUP, nk)

    def body(a_ref, w_ref, h_ref, g_ref, t_ref, dy_ref, dfo_ref, dg_ref, l_ref, acc_ref):
        grp, kk, i = pl.program_id(0), pl.program_id(1), pl.program_id(2)

        @pl.when(jnp.logical_and(jnp.logical_and(grp == 0, kk == 0), i == 0))
        def _():
            dg_ref[...] = jnp.zeros_like(dg_ref)
            l_ref[...] = jnp.zeros_like(l_ref)

        @pl.when(kk == 0)
        def _():
            acc_ref[i] = jnp.zeros((tm, d), F32)

        acc_ref[i] += jnp.dot(a_ref[...], w_ref[...], preferred_element_type=F32)

        @pl.when(kk == nk - 1)
        def _():
            fo = acc_ref[i]
            gain = g_ref[...]
            r = lax.rsqrt(jnp.mean(fo * fo, axis=-1, keepdims=True) + EPS)
            xh = fo * r
            y = h_ref[...] + scale * (xh * gain)
            row = (grp * gsz + i) * tm + lax.broadcasted_iota(jnp.int32, (tm, 1), 0)
            e = jnp.where(row >= lead, y - t_ref[...], 0.0)
            dy = e * (1.0 / d)
            dy_ref[...] = dy
            l_ref[...] += 0.5 * jnp.sum(jnp.sum(e * e, axis=-1, keepdims=True) * (1.0 / d), axis=0, keepdims=True)
            dn = scale * dy
            dyh = dn * gain
            dfo_ref[...] = (r * (dyh - xh * jnp.mean(dyh * xh, axis=-1, keepdims=True))).astype(dfo_ref.dtype)
            dg_ref[...] += jnp.sum(dn * xh, axis=0, keepdims=True)

    row = pl.BlockSpec((tm, d), lambda grp, kk, i: (epilogue_row(grp, kk, i), 0))
    row_once = pl.BlockSpec((tm, d), lambda grp, kk, i: (epilogue_row(grp, kk, i), 0), pipeline_mode=pl.Buffered(1))
    vec = pl.BlockSpec((1, d), lambda grp, kk, i: (0, 0))
    return _call(
        body, grid=(m // tm // gsz, nk, gsz),
        in_specs=[pl.BlockSpec((tm, tk), lambda grp, kk, i: (grp * gsz + i, kk)),
                  pl.BlockSpec((tk, d), lambda grp, kk, i: (kk, 0)), row_once, vec, row_once],
        out_specs=[row, row, vec, pl.BlockSpec((1, 1), lambda grp, kk, i: (0, 0))],
        out_shape=[jax.ShapeDtypeStruct((m, d), F32), jax.ShapeDtypeStruct((m, d), MXU_DTYPE),
                   jax.ShapeDtypeStruct((1, d), F32), jax.ShapeDtypeStruct((1, 1), F32)],
        scratch_shapes=[pltpu.VMEM((gsz, tm, d), F32)], name=name, args=(a, w, h, g, target), carried=carried)


def _norm_bwd_row_tile(m):
    return _tile(m, ACC_ROWS, SUBLANE_BF16)


def _mm_norm_bwd(pairs, h, g, dh_up, name, carried=(), row_tiles=None, dg_init=None, post=None):
    n_pairs = len(pairs)
    m, k = pairs[0][0].shape
    d = h.shape[1]
    tm = _norm_bwd_row_tile(m)
    tk = _tile(k, K_TILE, LANE)
    nk = k // tk
    t0, nt = row_tiles if row_tiles is not None else (0, m // tm)
    gsz, epilogue_row = _row_groups(nt, ACC_GROUP, nk)
    if dg_init is None:
        dg_init = jnp.zeros((1, d), F32)

    n_post = 0 if post is None else 2

    def body(*refs):
        ops = refs[:2 * n_pairs]
        h_ref, g_ref, up_ref, init_ref = refs[2 * n_pairs:2 * n_pairs + 4]
        post_in = refs[2 * n_pairs + 4:2 * n_pairs + 4 + n_post]
        dh_ref, dg_ref = refs[2 * n_pairs + 4 + n_post:2 * n_pairs + 6 + n_post]
        post_out = refs[2 * n_pairs + 6 + n_post:2 * n_pairs + 6 + 2 * n_post]
        acc_ref = refs[-1]
        grp, kk, i = pl.program_id(0), pl.program_id(1), pl.program_id(2)

        @pl.when(jnp.logical_and(jnp.logical_and(grp == 0, kk == 0), i == 0))
        def _():
            dg_ref[...] = init_ref[...]
            if post is not None:
                post_out[1][...] = jnp.zeros_like(post_out[1])

        @pl.when(kk == 0)
        def _():
            acc_ref[i] = jnp.zeros((tm, d), F32)

        for p in range(n_pairs):
            acc_ref[i] += jnp.dot(ops[2 * p][...], ops[2 * p + 1][...], preferred_element_type=F32)

        @pl.when(kk == nk - 1)
        def _():
            dx, dgr = _rmsnorm_bwd_rows(h_ref[...], g_ref[...], acc_ref[i])
            dh = up_ref[...] + dx
            dh_ref[...] = dh
            dg_ref[...] += jnp.sum(dgr, axis=0, keepdims=True)
            if post is not None:
                dfo, dpr = _rmsnorm_bwd_rows(post_in[0][...], post_in[1][...], post[2] * dh)
                post_out[0][...] = dfo.astype(post_out[0].dtype)
                post_out[1][...] += jnp.sum(dpr, axis=0, keepdims=True)

    row_in = pl.BlockSpec((tm, d), lambda grp, kk, i: (t0 + epilogue_row(grp, kk, i), 0))
    row_out = pl.BlockSpec((tm, d), lambda grp, kk, i: (epilogue_row(grp, kk, i), 0))
    vec = pl.BlockSpec((1, d), lambda grp, kk, i: (0, 0))
    in_specs = []
    args = []
    for a, w in pairs:
        in_specs += [pl.BlockSpec((tm, tk), lambda grp, kk, i: (t0 + grp * gsz + i, kk)),
                     pl.BlockSpec((tk, d), lambda grp, kk, i: (kk, 0))]
        args += [a, w]
    in_specs += [row_in, vec, row_in, vec]
    args += [h, g, dh_up, dg_init]
    out_specs = [row_out, vec]
    out_shape = [jax.ShapeDtypeStruct((nt * tm, d), F32), jax.ShapeDtypeStruct((1, d), F32)]
    if post is not None:
        in_specs += [row_in, vec]
        args += [post[0], post[1]]
        out_specs += [row_out, vec]
        out_shape += [jax.ShapeDtypeStruct((nt * tm, d), MXU_DTYPE), jax.ShapeDtypeStruct((1, d), F32)]
    return _call(
        body, grid=(nt // gsz, nk, gsz), in_specs=in_specs, out_specs=out_specs, out_shape=out_shape,
        scratch_shapes=[pltpu.VMEM((gsz, tm, d), F32)], name=name, args=tuple(args), carried=carried)


def _mm_tn(a, b, name, carried=()):
    m, ka = a.shape
    d = b.shape[1]
    tf = _tile(ka, 512, LANE)

    def body(a_ref, b_ref, o_ref):
        o_ref[...] = _dot_tn(a_ref[...], b_ref[...]).astype(o_ref.dtype)

    return _call(
        body, grid=(ka // tf,),
        in_specs=[pl.BlockSpec((m, tf), lambda j: (0, j)),
                  pl.BlockSpec((m, d), lambda j: (0, 0), pipeline_mode=pl.Buffered(1))],
        out_specs=pl.BlockSpec((tf, d), lambda j: (j, 0)),
        out_shape=jax.ShapeDtypeStruct((ka, d), WIRE_DTYPE), name=name, args=(a, b), carried=carried)


GELU_K = 0.7978845608028654
GELU_C = 0.044715


def _expm1(x):
    series = x * (1.0 + x * (1.0 / 2 + x * (1.0 / 6 + x * (1.0 / 24 + x * (1.0 / 120)))))
    return jnp.where(jnp.abs(x) < 0.1, series, jnp.exp(x) - 1.0)


def _softplus(x):
    return jnp.maximum(x, 0.0) + jnp.log1p(jnp.exp(-jnp.abs(x)))


def _block_mm(v, w_ref, transposed):
    nbk = w_ref.shape[0]
    outs = []
    for j in range(nbk):
        vj = v[:, j * BD:(j + 1) * BD]
        outs.append(_dot_nt(vj, w_ref[j]) if transposed else jnp.dot(vj, w_ref[j], preferred_element_type=F32))
    return outs[0] if nbk == 1 else jnp.concatenate(outs, axis=1)


def _group_mean(q, gm_ref):
    hi = q.astype(MXU_DTYPE)
    lo = (q - hi.astype(F32)).astype(MXU_DTYPE)
    nbk = q.shape[1] // BD
    gm = gm_ref[...]
    outs = []
    for j in range(nbk):
        sl = slice(j * BD, (j + 1) * BD)
        outs.append(jnp.dot(hi[:, sl], gm, preferred_element_type=F32) + jnp.dot(lo[:, sl], gm, preferred_element_type=F32))
    return outs[0] if nbk == 1 else jnp.concatenate(outs, axis=1)


class _RowReader:
    def __init__(self, ref):
        self.ref = ref

    def __getitem__(self, rows):
        return self.ref[rows, :]


def _shifted(ext_ref, cur, before8, after8, downs=(), ups=()):
    r = cur.shape[0]
    if downs:
        ext_ref[0:8, :] = before8
    ext_ref[8:8 + r, :] = cur
    if ups:
        ext_ref[8 + r:16 + r, :] = after8
    return [ext_ref[pl.ds(8 - j, r), :] for j in downs] + [ext_ref[pl.ds(8 + j, r), :] for j in ups]


def _lru_gates(xc, pv, wa_ref, wx_ref):
    xcb = xc.astype(MXU_DTYPE)
    ga = jax.nn.sigmoid(_block_mm(xcb, wa_ref, False) + pv[5:6])
    gx = jax.nn.sigmoid(_block_mm(xcb, wx_ref, False) + pv[6:7])
    sp = _softplus(-pv[7:8])
    log_a = -LRU_C * ga * sp
    a = jnp.exp(log_a)
    e2 = _expm1(2.0 * log_a)
    mult = jnp.sqrt(-e2)
    return xcb, ga, gx, sp, a, e2, mult


def _gelu_parts(y):
    th = jnp.tanh(GELU_K * (y + GELU_C * y * y * y))
    return 0.5 * y * (1.0 + th), th


def _scan_block(a, u, sa_ref, su_ref, carry_ref, out_ref, reverse):
    r, c = a.shape
    n = r // 8
    a3 = a.reshape(n, 8, c)
    u3 = u.reshape(n, 8, c)
    sub = lax.broadcasted_iota(jnp.int32, (n, 8, c), 1)
    for dlt in (1, 2, 4):
        keep = (sub < 8 - dlt) if reverse else (sub >= dlt)
        shift = 8 - dlt if reverse else dlt
        sh_a = pltpu.roll(a3, shift, axis=1)
        sh_u = pltpu.roll(u3, shift, axis=1)
        u3 = u3 + a3 * jnp.where(keep, sh_u, 0.0)
        a3 = a3 * jnp.where(keep, sh_a, 1.0)
    sa_ref[...] = a3.reshape(r, c)
    su_ref[...] = u3.reshape(r, c)
    for k in (range(n - 1, -1, -1) if reverse else range(n)):
        rows = pl.ds(8 * k, 8)
        out_ref[rows, :] = su_ref[rows, :] + sa_ref[rows, :] * carry_ref[...]
        carry_ref[...] = out_ref[pl.ds(8 * k if reverse else 8 * k + 7, 1), :]


def _mixer_fwd(z, pv, wa, wx, gm, pad, name, carried=()):
    m = z.shape[0]
    c = pv.shape[1]
    r = MIX_ROWS
    nb = m // r

    def body(z_ref, pv_ref, wa_ref, wx_ref, gm_ref, mixed_ref, hs_ref, ext_ref, tailx_ref, tailc_ref, carry_ref,
             sa_ref, su_ref):
        b = pl.program_id(0)

        @pl.when(b == 0)
        def _():
            tailx_ref[...] = jnp.zeros_like(tailx_ref)
            tailc_ref[...] = jnp.zeros_like(tailc_ref)
            carry_ref[...] = jnp.zeros_like(carry_ref)

        pv = _RowReader(pv_ref)
        row = b * r + lax.broadcasted_iota(jnp.int32, (r, 1), 0)
        maskf = (row >= pad).astype(F32)
        y = z_ref[:, 0:c]
        xl = z_ref[:, c:2 * c]
        bs = z_ref[:, 2 * c:3 * c]
        cv = z_ref[:, 3 * c:4 * c] * z_ref[:, 4 * c:5 * c]

        x1, x2, x3 = _shifted(ext_ref, xl, tailx_ref[...], None, downs=(1, 2, 3))
        tailx_ref[...] = z_ref[pl.ds(r - 8, 8), c:2 * c]
        xc = pv[4:5] + pv[3:4] * xl + pv[2:3] * x1 + pv[1:2] * x2 + pv[0:1] * x3
        _, _, gx, _, a, _, mult = _lru_gates(xc, pv, wa_ref, wx_ref)
        uu = mult * (gx * xc) * maskf

        _scan_block(a, uu, sa_ref, su_ref, carry_ref, hs_ref, reverse=False)
        hs = hs_ref[...]

        gelu_y, _ = _gelu_parts(y)
        lru_out = hs * gelu_y
        c1, c2 = _shifted(ext_ref, cv, tailc_ref[...], None, downs=(1, 2))
        tailc_ref[...] = cv[r - 8:r]
        sc_out = bs * (pv[10:11] * cv + pv[9:10] * c1 + pv[8:9] * c2)

        rl = lax.rsqrt(_group_mean(lru_out * lru_out, gm_ref) + EPS)
        rs = lax.rsqrt(_group_mean(sc_out * sc_out, gm_ref) + EPS)
        mixed_ref[:, 0:c] = (lru_out * rl * pv[11:12]).astype(mixed_ref.dtype)
        mixed_ref[:, c:2 * c] = (sc_out * rs * pv[12:13]).astype(mixed_ref.dtype)

    full = lambda shape: pl.BlockSpec(shape, lambda b: (0,) * len(shape))
    return _call(
        body, grid=(nb,),
        in_specs=[pl.BlockSpec((r, 5 * c), lambda b: (b, 0)), full(pv.shape), full(wa.shape), full(wx.shape), full(gm.shape)],
        out_specs=[pl.BlockSpec((r, 2 * c), lambda b: (b, 0)), pl.BlockSpec((r, c), lambda b: (b, 0))],
        out_shape=[jax.ShapeDtypeStruct((m, 2 * c), MXU_DTYPE), jax.ShapeDtypeStruct((m, c), F32)],
        scratch_shapes=[pltpu.VMEM((r + 16, c), F32), pltpu.VMEM((8, c), F32), pltpu.VMEM((8, c), F32),
                        pltpu.VMEM((1, c), F32), pltpu.VMEM((r, c), F32), pltpu.VMEM((r, c), F32)],
        name=name, args=(z, pv, wa, wx, gm), carried=carried)


def _mixer_bwd(z, hs, dmixed, pv, wa, wx, gm, pad, name, carried=()):
    m = z.shape[0]
    c = pv.shape[1]
    r = MIX_ROWS
    nb = m // r
    r8 = r // 8
    assert pad <= r and pad % SUBLANE_BF16 == 0

    def body(z_ref, zp_ref, hs_ref, hsp_ref, dm_ref, pv_ref, wa_ref, wx_ref, gm_ref,
             dz_ref, dpv_ref, dwa_ref, dwx_ref, ext_ref, hxc_ref, hsc_ref, hp_ref, pc_ref, sa_ref, su_ref, p_ref):
        i = pl.program_id(0)
        b = nb - 1 - i

        @pl.when(i == 0)
        def _():
            hxc_ref[...] = jnp.zeros_like(hxc_ref)
            hsc_ref[...] = jnp.zeros_like(hsc_ref)
            hp_ref[...] = jnp.zeros_like(hp_ref)
            pc_ref[...] = jnp.zeros_like(pc_ref)
            dpv_ref[...] = jnp.zeros_like(dpv_ref)
            dwa_ref[...] = jnp.zeros_like(dwa_ref)
            dwx_ref[...] = jnp.zeros_like(dwx_ref)

        pv = _RowReader(pv_ref)
        row = b * r + lax.broadcasted_iota(jnp.int32, (r, 1), 0)
        maskf = (row >= pad).astype(F32)
        has_prev = (b > 0).astype(F32)
        y = z_ref[:, 0:c]
        xl = z_ref[:, c:2 * c]
        bs = z_ref[:, 2 * c:3 * c]
        cs = z_ref[:, 3 * c:4 * c]
        vs = z_ref[:, 4 * c:5 * c]
        cv = cs * vs
        xl_prev = zp_ref[:, c:2 * c] * has_prev
        cv_prev = zp_ref[:, 3 * c:4 * c] * zp_ref[:, 4 * c:5 * c] * has_prev
        hs = hs_ref[...]

        x1, x2, x3 = _shifted(ext_ref, xl, xl_prev, None, downs=(1, 2, 3))
        xc = pv[4:5] + pv[3:4] * xl + pv[2:3] * x1 + pv[1:2] * x2 + pv[0:1] * x3
        xcb, ga, gx, sp, a, e2, mult = _lru_gates(xc, pv, wa_ref, wx_ref)
        gxx = gx * xc
        gelu_y, th = _gelu_parts(y)
        lru_out = hs * gelu_y
        c1, c2 = _shifted(ext_ref, cv, cv_prev, None, downs=(1, 2))
        sc = pv[10:11] * cv + pv[9:10] * c1 + pv[8:9] * c2
        sc_out = bs * sc

        def group_norm_bwd(v, dm, gain):
            rr = lax.rsqrt(_group_mean(v * v, gm_ref) + EPS)
            vh = v * rr
            dvh = dm * gain
            dv = rr * (dvh - vh * _group_mean(dvh * vh, gm_ref))
            return dv, jnp.sum(dm * vh, axis=0, keepdims=True)

        d_lru_out, d_og = group_norm_bwd(lru_out, dm_ref[:, 0:c], pv[11:12])
        d_sc_out, d_sg = group_norm_bwd(sc_out, dm_ref[:, c:2 * c], pv[12:13])
        dpv_ref[11:12, :] += d_og
        dpv_ref[12:13, :] += d_sg

        dhs = d_lru_out * gelu_y
        dgelu = 0.5 * (1.0 + th) + 0.5 * y * (1.0 - th * th) * GELU_K * (1.0 + 3.0 * GELU_C * y * y)
        dy = d_lru_out * hs * dgelu

        _scan_block(a, a * dhs, sa_ref, su_ref, pc_ref, p_ref, reverse=True)
        (p_next,) = _shifted(ext_ref, p_ref[...], None, hp_ref[...], ups=(1,))
        hp_ref[...] = p_ref[0:8, :]
        q = dhs + p_next
        (hs_prev,) = _shifted(ext_ref, hs, hsp_ref[...] * has_prev, None, downs=(1,))
        duu = q * maskf
        da = q * hs_prev

        dmult = duu * gxx
        dgxx = duu * mult
        dgx = dgxx * xc
        dxc = dgxx * gx
        dlog_a = da * a - dmult * ((1.0 + e2) / mult)
        dga = dlog_a * (-LRU_C * sp)
        dsp = jnp.sum(dlog_a * (-LRU_C * ga), axis=0, keepdims=True)
        dpv_ref[7:8, :] += dsp * (-jax.nn.sigmoid(-pv[7:8]))
        dga_pre = dga * ga * (1.0 - ga)
        dgx_pre = dgx * gx * (1.0 - gx)
        dpv_ref[5:6, :] += jnp.sum(dga_pre, axis=0, keepdims=True)
        dpv_ref[6:7, :] += jnp.sum(dgx_pre, axis=0, keepdims=True)
        dga_b = dga_pre.astype(MXU_DTYPE)
        dgx_b = dgx_pre.astype(MXU_DTYPE)
        dxc = dxc + _block_mm(dga_b, wa_ref, True) + _block_mm(dgx_b, wx_ref, True)
        for j in range(c // BD):
            sl = slice(j * BD, (j + 1) * BD)
            dwa_ref[j] += _dot_tn(xcb[:, sl], dga_b[:, sl])
            dwx_ref[j] += _dot_tn(xcb[:, sl], dgx_b[:, sl])

        dpv_ref[4:5, :] += jnp.sum(dxc, axis=0, keepdims=True)
        dpv_ref[3:4, :] += jnp.sum(dxc * xl, axis=0, keepdims=True)
        dpv_ref[2:3, :] += jnp.sum(dxc * x1, axis=0, keepdims=True)
        dpv_ref[1:2, :] += jnp.sum(dxc * x2, axis=0, keepdims=True)
        dpv_ref[0:1, :] += jnp.sum(dxc * x3, axis=0, keepdims=True)
        u1, u2, u3 = _shifted(ext_ref, dxc, None, hxc_ref[...], ups=(1, 2, 3))
        hxc_ref[...] = dxc[0:8]
        dxl = pv[3:4] * dxc + pv[2:3] * u1 + pv[1:2] * u2 + pv[0:1] * u3

        dbs = d_sc_out * sc
        dsc = d_sc_out * bs
        dpv_ref[10:11, :] += jnp.sum(dsc * cv, axis=0, keepdims=True)
        dpv_ref[9:10, :] += jnp.sum(dsc * c1, axis=0, keepdims=True)
        dpv_ref[8:9, :] += jnp.sum(dsc * c2, axis=0, keepdims=True)
        s1, s2 = _shifted(ext_ref, dsc, None, hsc_ref[...], ups=(1, 2))
        hsc_ref[...] = dsc[0:8]
        dcv = pv[10:11] * dsc + pv[9:10] * s1 + pv[8:9] * s2

        dz_ref[:, 0:c] = dy.astype(dz_ref.dtype)
        dz_ref[:, c:2 * c] = dxl.astype(dz_ref.dtype)
        dz_ref[:, 2 * c:3 * c] = dbs.astype(dz_ref.dtype)
        dz_ref[:, 3 * c:4 * c] = (dcv * vs).astype(dz_ref.dtype)
        dz_ref[:, 4 * c:5 * c] = (dcv * cs).astype(dz_ref.dtype)

        if pad:
            @pl.when(b == 0)
            def _():
                dz_ref[0:pad, :] = jnp.zeros((pad, 5 * c), dz_ref.dtype)

    full = lambda shape: pl.BlockSpec(shape, lambda i: (0,) * len(shape))
    cur = lambda width: pl.BlockSpec((r, width), lambda i: (nb - 1 - i, 0))
    prev8 = lambda width: pl.BlockSpec((8, width), lambda i: (jnp.maximum((nb - 1 - i) * r8 - 1, 0), 0))
    return _call(
        body, grid=(nb,),
        in_specs=[cur(5 * c), prev8(5 * c), cur(c), prev8(c), cur(2 * c),
                  full(pv.shape), full(wa.shape), full(wx.shape), full(gm.shape)],
        out_specs=[cur(5 * c), full(pv.shape), full(wa.shape), full(wx.shape)],
        out_shape=[jax.ShapeDtypeStruct((m, 5 * c), MXU_DTYPE), jax.ShapeDtypeStruct(pv.shape, F32),
                   jax.ShapeDtypeStruct(wa.shape, F32), jax.ShapeDtypeStruct(wx.shape, F32)],
        scratch_shapes=[pltpu.VMEM((r + 16, c), F32), pltpu.VMEM((8, c), F32), pltpu.VMEM((8, c), F32),
                        pltpu.VMEM((8, c), F32), pltpu.VMEM((1, c), F32), pltpu.VMEM((r, c), F32),
                        pltpu.VMEM((r, c), F32), pltpu.VMEM((r, c), F32)],
        name=name, args=(z, z, hs, hs, dmixed, pv, wa, wx, gm), carried=carried)


def _position():
    return lax.axis_index("x"), lax.axis_index("y"), lax.axis_index("c")


def _block_of(px, py, pc):
    return 4 * px + 2 * py + pc


class _TwoLevelGather:
    def __init__(self, n_arrays, rows_of, src_of, send_sems, recv_sems):
        x, y, c = _position()
        self.n, self.rows_of, self.src_of = n_arrays, rows_of, src_of
        self.send_sems, self.recv_sems = send_sems, recv_sems
        self.c, self.me, self.sibling = c, (x, y, c), (x, y, 1 - c)
        self.chips = [(1 - x, y), (x, 1 - y), (1 - x, 1 - y)]

    def _copy(self, i, k, block, to, src=None):
        return pltpu.make_async_remote_copy(
            src_ref=self.rows_of(i, *block) if src is None else src, dst_ref=self.rows_of(i, *block),
            send_sem=self.send_sems.at[7 * i + k], recv_sem=self.recv_sems.at[7 * i + k],
            device_id=to, device_id_type=MESH)

    def _first(self, i):
        own = [self._copy(i, 0, self.me, self.sibling, src=self.src_of(i))]
        return own + [self._copy(i, 1 + j, self.me, (*chip, self.c), src=self.src_of(i))
                      for j, chip in enumerate(self.chips)]

    def _passed(self, i, j):
        return self._copy(i, 4 + j, (*self.chips[j], self.c), self.sibling)

    def start(self):
        for i in range(self.n):
            for cp in self._first(i):
                cp.start()

    def forward(self):
        for i in range(self.n):
            for j, chip in enumerate(self.chips):
                self._copy(i, 1 + j, (*chip, self.c), self.me).wait_recv()
                self._passed(i, j).start()

    def drain(self):
        for i in range(self.n):
            self._copy(i, 0, self.sibling, self.me).wait_recv()
            for j, chip in enumerate(self.chips):
                self._copy(i, 4 + j, (*chip, 1 - self.c), self.me).wait_recv()
        for i in range(self.n):
            for cp in self._first(i) + [self._passed(i, j) for j in range(3)]:
                cp.wait_send()


class _RelayGather:
    def __init__(self, n_arrays, rows_of, src_of, send_sems, recv_sems):
        x, y, c = _position()
        self.n, self.rows_of, self.src_of = n_arrays, rows_of, src_of
        self.send_sems, self.recv_sems = send_sems, recv_sems
        self.me, self.sibling = (x, y, c), (x, y, 1 - c)
        self.xn, self.yn, self.dg = (1 - x, y, c), (x, 1 - y, c), (1 - x, 1 - y, c)

    def _copy(self, i, k, block, to, half=None, src=None):
        rows = self.rows_of(i, *block, half)
        return pltpu.make_async_remote_copy(
            src_ref=rows if src is None else src, dst_ref=rows,
            send_sem=self.send_sems.at[8 * i + k], recv_sem=self.recv_sems.at[8 * i + k],
            device_id=to, device_id_type=MESH)

    def _sends(self, i):
        own = self.src_of(i)
        return [self._copy(i, 0, self.me, self.sibling, src=own), self._copy(i, 1, self.me, self.xn, src=own),
                self._copy(i, 2, self.me, self.yn, src=own),
                self._copy(i, 3, self.xn, self.yn, half=0), self._copy(i, 4, self.yn, self.xn, half=1),
                self._copy(i, 5, self.xn, self.sibling), self._copy(i, 6, self.yn, self.sibling),
                self._copy(i, 7, self.dg, self.sibling)]

    def start(self):
        for i in range(self.n):
            for cp in self._sends(i)[0:3]:
                cp.start()

    def forward(self):
        for i in range(self.n):
            self._copy(i, 1, self.xn, self.me).wait_recv()
            self._copy(i, 2, self.yn, self.me).wait_recv()
            for cp in self._sends(i)[3:7]:
                cp.start()

    def drain(self):
        x, y, c = self.me
        for i in range(self.n):
            self._copy(i, 3, self.dg, self.me, half=0).wait_recv()
            self._copy(i, 4, self.dg, self.me, half=1).wait_recv()
            self._sends(i)[7].start()
        for i in range(self.n):
            self._copy(i, 0, self.sibling, self.me).wait_recv()
            self._copy(i, 5, (1 - x, y, 1 - c), self.me).wait_recv()
            self._copy(i, 6, (x, 1 - y, 1 - c), self.me).wait_recv()
            self._copy(i, 7, (1 - x, 1 - y, 1 - c), self.me).wait_recv()
        for i in range(self.n):
            for cp in self._sends(i):
                cp.wait_send()


class _CarriedGather:
    def __init__(self, shards, padded_rows, zeros, forward_at, part=None, into=None):
        d = shards[0].shape[1]
        self.forward_at = forward_at
        self.n = len(shards)
        self.rows = [s.shape[0] for s in shards]
        self.pads = [p - N_DEV * r for r, p in zip(self.rows, padded_rows)]
        assert max(self.pads) <= zeros.shape[0] and zeros.shape[1] == d
        self.part = part if part is not None else (0, self.rows[0])
        assert (part is None and into is None) or self.n == 1
        assert self.part[0] % SUBLANE_BF16 == 0 and self.part[1] % SUBLANE_BF16 == 0
        self.arrays = list(shards) + [zeros] + ([into] if into is not None else [])
        self.out_shapes = [jax.ShapeDtypeStruct((p, d), s.dtype) for s, p in zip(shards, padded_rows)]
        self.aliases = {self.n + 1: 0} if into is not None else {}
        if into is not None:
            self.pads = [0] * self.n
        self.n_remote, self.n_local = 8 * self.n, 2 * self.n
        self.results = None

    def _rows_of(self, outs):
        def rows_of(i, px, py, pc, half):
            first, count = (self.part if self.n == 1 else (0, self.rows[i]))
            head = _round_up(count // 2, SUBLANE_BF16)
            if half == 0:
                count = head
            elif half == 1:
                first, count = first + head, count - head
            first = _block_of(px, py, pc) * self.rows[i] + first
            return outs[i].at[pl.ds(pl.multiple_of(first, SUBLANE_BF16), count), :]
        return rows_of

    def _own(self, ins, i):
        return ins[i].at[pl.ds(self.part[0], self.part[1]), :] if self.n == 1 else ins[i]

    def _gather(self, ins, outs, send_sems, recv_sems):
        return _RelayGather(self.n, self._rows_of(outs), functools.partial(self._own, ins), send_sems, recv_sems)

    def _local(self, ins, outs, local_sems):
        x, y, c = _position()
        rows_of = self._rows_of(outs)
        cps = []
        for i in range(self.n):
            cps.append(pltpu.make_async_copy(self._own(ins, i), rows_of(i, x, y, c, None), local_sems.at[2 * i]))
            if self.pads[i]:
                cps.append(pltpu.make_async_copy(ins[self.n].at[pl.ds(0, self.pads[i]), :],
                                                 outs[i].at[pl.ds(N_DEV * self.rows[i], self.pads[i]), :],
                                                 local_sems.at[2 * i + 1]))
        return cps

    def start(self, ins, outs, send_sems, recv_sems, local_sems):
        for cp in self._local(ins, outs, local_sems):
            cp.start()
        self._gather(ins, outs, send_sems, recv_sems).start()

    def forward(self, ins, outs, send_sems, recv_sems, local_sems):
        self._gather(ins, outs, send_sems, recv_sems).forward()

    def finish(self, ins, outs, send_sems, recv_sems, local_sems):
        self._gather(ins, outs, send_sems, recv_sems).drain()
        for cp in self._local(ins, outs, local_sems):
            cp.wait()


class _CarriedSwap:
    def __init__(self, grads, shard_rows):
        d = grads[0].shape[1]
        self.n, self.rows = len(grads), list(shard_rows)
        self.arrays = list(grads)
        self.out_shapes = [jax.ShapeDtypeStruct((4, s, d), g.dtype) for g, s in zip(grads, shard_rows)]
        self.aliases = {}
        self.n_remote, self.n_local = 4 * self.n, 0
        self.forward_at = 1.0
        self.results = None

    def _copies(self, ins, outs, send_sems, recv_sems):
        x, y, c = _position()
        cps = []
        for i in range(self.n):
            s = self.rows[i]
            for k in range(4):
                blk = _block_of(k >> 1, k & 1, 1 - c)
                cps.append(pltpu.make_async_remote_copy(
                    src_ref=ins[i].at[pl.ds(pl.multiple_of(blk * s, SUBLANE_BF16), s), :], dst_ref=outs[i].at[k],
                    send_sem=send_sems.at[4 * i + k], recv_sem=recv_sems.at[4 * i + k],
                    device_id=(x, y, 1 - c), device_id_type=MESH))
        return cps

    def start(self, ins, outs, send_sems, recv_sems, local_sems):
        for cp in self._copies(ins, outs, send_sems, recv_sems):
            cp.start()

    def forward(self, *_):
        pass

    def finish(self, ins, outs, send_sems, recv_sems, local_sems):
        for cp in self._copies(ins, outs, send_sems, recv_sems):
            cp.wait()


class _CarriedChipExchange:
    def __init__(self, presums, part=None, into=None):
        self.n = len(presums)
        assert (part is None and into is None) or self.n == 1
        self.part = part if part is not None else (0, presums[0].shape[1])
        assert self.part[0] % SUBLANE_BF16 == 0 and self.part[1] % SUBLANE_BF16 == 0
        self.arrays = list(presums) + ([into] if into is not None else [])
        self.out_shapes = [jax.ShapeDtypeStruct(p.shape, p.dtype) for p in presums]
        self.aliases = {self.n: 0} if into is not None else {}
        self.n_remote, self.n_local = 3 * self.n, 0
        self.forward_at = 1.0
        self.results = None

    def _copies(self, ins, outs, send_sems, recv_sems):
        x, y, c = _position()
        cps = []
        for i in range(self.n):
            rows = pl.ds(*self.part) if self.n == 1 else pl.ds(0, self.arrays[i].shape[1])
            for r in range(1, 4):
                cps.append(pltpu.make_async_remote_copy(
                    src_ref=ins[i].at[r - 1, rows, :], dst_ref=outs[i].at[r - 1, rows, :],
                    send_sem=send_sems.at[3 * i + r - 1], recv_sem=recv_sems.at[3 * i + r - 1],
                    device_id=(x ^ (r >> 1), y ^ (r & 1), c), device_id_type=MESH))
        return cps

    def start(self, ins, outs, send_sems, recv_sems, local_sems):
        for cp in self._copies(ins, outs, send_sems, recv_sems):
            cp.start()

    def forward(self, *_):
        pass

    def finish(self, ins, outs, send_sems, recv_sems, local_sems):
        for cp in self._copies(ins, outs, send_sems, recv_sems):
            cp.wait()


def _gather_small(block, reduce, name):
    rr, nn = block.shape

    def body(x_ref, out_ref, *rest):
        if reduce:
            stack_ref, send_sems, recv_sems, local_sem = rest
        else:
            send_sems, recv_sems, local_sem = rest
            stack_ref = out_ref
        x, y, c = _position()

        def rows_of(i, px, py, pc):
            return stack_ref.at[pl.ds(pl.multiple_of(_block_of(px, py, pc) * rr, 8), rr), :]

        own = pltpu.make_async_copy(x_ref, rows_of(0, x, y, c), local_sem)
        own.start()
        gather = _TwoLevelGather(1, rows_of, lambda i: x_ref, send_sems, recv_sems)
        gather.start()
        gather.forward()
        gather.drain()
        own.wait()
        if reduce:
            acc = stack_ref[0:rr, :]
            for k in range(1, N_DEV):
                acc = acc + stack_ref[k * rr:(k + 1) * rr, :]
            out_ref[...] = acc

    vmem = pl.BlockSpec(memory_space=pltpu.VMEM)
    scratch = [pltpu.SemaphoreType.DMA((7,)), pltpu.SemaphoreType.DMA((7,)), pltpu.SemaphoreType.DMA]
    if reduce:
        scratch = [pltpu.VMEM((N_DEV * rr, nn), F32)] + scratch
    out_rows = rr if reduce else N_DEV * rr
    return pl.pallas_call(
        body, in_specs=[vmem], out_specs=vmem, out_shape=jax.ShapeDtypeStruct((out_rows, nn), F32),
        scratch_shapes=scratch, name=name, compiler_params=_params())(block)


def _sum_stack(stack, name):
    rr = stack.shape[0] // N_DEV

    def body(s_ref, o_ref):
        acc = s_ref[0:rr, :]
        for k in range(1, N_DEV):
            acc = acc + s_ref[k * rr:(k + 1) * rr, :]
        o_ref[...] = acc

    vmem = pl.BlockSpec(memory_space=pltpu.VMEM)
    return pl.pallas_call(body, in_specs=[vmem], out_specs=vmem,
                          out_shape=jax.ShapeDtypeStruct((rr, stack.shape[1]), F32), name=name,
                          compiler_params=_params())(stack)


def _presum(where, grad, swapped, name):
    s, d = swapped.shape[1], swapped.shape[2]
    tc = _tile(d, 512, LANE)

    def body(where_ref, g_ref, sw_ref, o_ref):
        o_ref[0] = (g_ref[...].astype(F32) + sw_ref[0].astype(F32)).astype(o_ref.dtype)

    return _call(
        body, grid=(3, d // tc),
        in_specs=[pl.BlockSpec((s, tc), lambda r, j, where: (where[1 + r], j)),
                  pl.BlockSpec((1, s, tc), lambda r, j, where: (where[5 + r], 0, j))],
        out_specs=pl.BlockSpec((1, s, tc), lambda r, j, where: (r, 0, j)),
        out_shape=jax.ShapeDtypeStruct((3, s, d), WIRE_DTYPE), name=name, args=(grad, swapped), prefetch=(where,))


def _final_sum(where, grad, swapped, received, name, carried=()):
    s, d = swapped.shape[1], swapped.shape[2]
    tc = _tile(d, 512, LANE)

    def body(where_ref, g_ref, sw_ref, r_ref, o_ref):
        acc = g_ref[...].astype(F32) + sw_ref[0].astype(F32)
        for k in range(3):
            acc = acc + r_ref[k].astype(F32)
        o_ref[...] = acc

    return _call(
        body, grid=(d // tc,),
        in_specs=[pl.BlockSpec((s, tc), lambda j, where: (where[0], j)),
                  pl.BlockSpec((1, s, tc), lambda j, where: (where[4], 0, j)),
                  pl.BlockSpec((3, s, tc), lambda j, where: (0, 0, j))],
        out_specs=pl.BlockSpec((s, tc), lambda j, where: (0, j)),
        out_shape=jax.ShapeDtypeStruct((s, d), F32), name=name, args=(grad, swapped, received),
        prefetch=(where,), carried=carried)


class _GradReduction:
    def __init__(self, key, grad, shard_rows, where):
        self.key, self.grad, self.rows, self.where = key, grad, shard_rows, where
        self._presum = self._exchange = None

    def swap(self):
        self._swap = _CarriedSwap([self.grad], [self.rows])
        return self._swap

    def exchange(self, part=None):
        if self._presum is None:
            self._presum = _presum(self.where, self.grad, self._swap.results[0], "presum_" + self.key)
        rows = None
        if part is not None:
            half = _round_up(self.rows // 2, SUBLANE_BF16)
            rows = (0, half) if part == 0 else (half, self.rows - half)
        into = self._exchange.results[0] if part == 1 else None
        self._exchange = _CarriedChipExchange([self._presum], rows, into)
        return self._exchange

    def total(self, carried=()):
        return _final_sum(self.where, self.grad, self._swap.results[0], self._exchange.results[0],
                          "sum_" + self.key, carried)

    def total_and_update(self, w, m, v):
        return _sum_adamw(self.where, self.grad, self._swap.results[0], self._exchange.results[0], w, m, v,
                          "update_" + self.key)


def _adamw_math(w, g, m, v):
    nm = ADAM_B1 * m + (1.0 - ADAM_B1) * g
    nv = ADAM_B2 * v + (1.0 - ADAM_B2) * (g * g)
    m_hat = nm / (1.0 - ADAM_B1 ** ADAM_STEP)
    v_hat = nv / (1.0 - ADAM_B2 ** ADAM_STEP)
    return -ADAM_LR * (m_hat / (jnp.sqrt(v_hat) + ADAM_EPS) + ADAM_WD * w), nm, nv


def _sum_adamw(where, grad, swapped, received, w, m, v, name):
    s, d = swapped.shape[1], swapped.shape[2]
    tc = _tile(d, 512, LANE)

    def body(where_ref, g_ref, sw_ref, r_ref, w_ref, m_ref, v_ref, gs_ref, d_ref, nm_ref, nv_ref):
        g = g_ref[...].astype(F32) + sw_ref[0].astype(F32)
        for k in range(3):
            g = g + r_ref[k].astype(F32)
        gs_ref[...] = g
        d_ref[...], nm_ref[...], nv_ref[...] = _adamw_math(w_ref[...], g, m_ref[...], v_ref[...])

    blk = pl.BlockSpec((s, tc), lambda j, where: (0, j))
    return _call(
        body, grid=(d // tc,),
        in_specs=[pl.BlockSpec((s, tc), lambda j, where: (where[0], j)),
                  pl.BlockSpec((1, s, tc), lambda j, where: (where[4], 0, j)),
                  pl.BlockSpec((3, s, tc), lambda j, where: (0, 0, j)), blk, blk, blk],
        out_specs=[blk] * 4, out_shape=[jax.ShapeDtypeStruct((s, d), F32)] * 4, name=name,
        args=(grad, swapped, received, w, m, v), prefetch=(where,))


def _adamw(w, g, m, v, name):
    rows, cols = w.shape
    tr = _tile(rows, 256, 8)

    def body(w_ref, g_ref, m_ref, v_ref, d_ref, nm_ref, nv_ref):
        d_ref[...], nm_ref[...], nv_ref[...] = _adamw_math(w_ref[...], g_ref[...], m_ref[...], v_ref[...])

    spec = pl.BlockSpec((tr, cols), lambda i: (i, 0))
    return pl.pallas_call(
        body, grid=(rows // tr,), in_specs=[spec] * 4, out_specs=[spec] * 3,
        out_shape=[jax.ShapeDtypeStruct((rows, cols), F32)] * 3, name=name, compiler_params=_params())(w, g, m, v)


def _pack_rows(arrays, width, row_quantum=8):
    flat = jnp.concatenate([a.reshape(-1) for a in arrays])
    total = _round_up(flat.shape[0], row_quantum * width)
    flat = jnp.pad(flat, (0, total - flat.shape[0]))
    return flat.reshape(-1, width)


def _unpack_rows(packed, shapes):
    flat = packed.reshape(-1)
    out = []
    off = 0
    for shp in shapes:
        size = 1
        for s in shp:
            size *= s
        out.append(flat[off:off + size].reshape(shp))
        off += size
    return out


def _block_diag(w):
    h, hb, _ = w.shape
    per = BD // hb
    w4 = w.reshape(h // per, per, hb, hb)
    eye = jnp.eye(per, dtype=w.dtype)
    return jnp.einsum('npij,pq->npiqj', w4, eye).reshape(h // per, BD, BD)


def _block_diag_extract(bd, hb):
    nbk = bd.shape[0]
    per = BD // hb
    b5 = bd.reshape(nbk, per, hb, per, hb)
    eye = jnp.eye(per, dtype=bd.dtype)
    return jnp.einsum('npiqj,pq->npij', b5, eye).reshape(nbk * per, hb, hb)


def kernel(x, meta_tokens, ffn1_pre_g, ffn1_w_gate, ffn1_w_up, ffn1_w_down, ffn1_post_g, mix_pre_g, w_in, lru_conv_w, lru_conv_b, lru_w_a, lru_b_a, lru_w_x, lru_b_x, lru_lambda, sconv_w, lru_out_g, sconv_out_g, w_out, mix_post_g, ffn2_pre_g, ffn2_w_gate, ffn2_w_up, ffn2_w_down, ffn2_post_g, loss_target, m_meta_tokens, m_ffn1_pre_g, m_ffn1_w_gate, m_ffn1_w_up, m_ffn1_w_down, m_ffn1_post_g, m_mix_pre_g, m_w_in, m_lru_conv_w, m_lru_conv_b, m_lru_w_a, m_lru_b_a, m_lru_w_x, m_lru_b_x, m_lru_lambda, m_sconv_w, m_lru_out_g, m_sconv_out_g, m_w_out, m_mix_post_g, m_ffn2_pre_g, m_ffn2_w_gate, m_ffn2_w_up, m_ffn2_w_down, m_ffn2_post_g, v_meta_tokens, v_ffn1_pre_g, v_ffn1_w_gate, v_ffn1_w_up, v_ffn1_w_down, v_ffn1_post_g, v_mix_pre_g, v_w_in, v_lru_conv_w, v_lru_conv_b, v_lru_w_a, v_lru_b_a, v_lru_w_x, v_lru_b_x, v_lru_lambda, v_sconv_w, v_lru_out_g, v_sconv_out_g, v_w_out, v_mix_post_g, v_ffn2_pre_g, v_ffn2_w_gate, v_ffn2_w_up, v_ffn2_w_down, v_ffn2_post_g):
    given = dict(locals())
    wts = {n: given[n] for n in WEIGHT_NAMES}
    mom = {n: given["m_" + n] for n in WEIGHT_NAMES}
    var = {n: given["v_" + n] for n in WEIGHT_NAMES}

    xi, yi, ci = _position()
    me = _block_of(xi, yi, ci)
    x2 = x[0]
    seq, d = x2.shape
    n_meta = meta_tokens.shape[0]
    m_rows = _round_up(n_meta + seq, ROW_ALIGN)
    pad = m_rows - n_meta - seq
    lead = pad + n_meta
    c = lru_conv_b.shape[1]
    hb = lru_w_a.shape[-1]
    dm = meta_tokens.shape[1]
    cs_ = lru_conv_w.shape[2]
    kw4, kw3 = lru_conv_w.shape[1], sconv_w.shape[1]
    assert d == 2 * c and c % BD == 0 and BD % hb == 0 and cs_ <= dm and kw4 == 4 and kw3 == 3

    small = jnp.zeros((_round_up(n_meta + kw4 + kw3, 8), dm), F32)
    small = small.at[0:n_meta].set(meta_tokens)
    small = small.at[n_meta:n_meta + kw4, 0:cs_].set(lru_conv_w[0])
    small = small.at[n_meta + kw4:n_meta + kw4 + kw3, 0:cs_].set(sconv_w[0])
    sr = small.shape[0]
    small_all = _gather_small(small, False, "gather_small").reshape(N_DEV, sr, dm)
    meta_full = small_all[:, 0:n_meta, :].transpose(1, 0, 2).reshape(n_meta, d)
    conv_w_full = small_all[:, n_meta:n_meta + kw4, 0:cs_].transpose(1, 0, 2).reshape(kw4, c)
    sconv_w_full = small_all[:, n_meta + kw4:n_meta + kw4 + kw3, 0:cs_].transpose(1, 0, 2).reshape(kw3, c)

    big = ['ffn1_w_gate', 'ffn1_w_up', 'ffn1_w_down', 'w_in', 'w_out', 'ffn2_w_gate', 'ffn2_w_up', 'ffn2_w_down']
    col_sharded = {'ffn1_w_gate', 'ffn1_w_up', 'w_in', 'ffn2_w_gate', 'ffn2_w_up'}
    shards = []
    for nme in big:
        w = wts[nme][0].astype(WIRE_DTYPE)
        shards.append(w.T if nme in col_sharded else w)
    shard_rows = dict(zip(big, [s.shape[0] for s in shards]))
    zeros = jnp.zeros((F_ALIGN, d), WIRE_DTYPE)

    def gather(forward_at, *names, part=None, into=None):
        sel = [shards[big.index(nme)] for nme in names]
        padded = [_round_up(N_DEV * shard_rows[nme], LANE if nme in ('w_in', 'w_out') else F_ALIGN) for nme in names]
        return _CarriedGather(sel, padded, zeros, forward_at, part, into)

    pv = jnp.zeros((16, c), F32)
    pv = pv.at[0:4].set(conv_w_full).at[4].set(lru_conv_b[0]).at[5].set(lru_b_a[0]).at[6].set(lru_b_x[0])
    pv = pv.at[7].set(lru_lambda[0]).at[8:11].set(sconv_w_full).at[11].set(lru_out_g[0]).at[12].set(sconv_out_g[0])
    wa_bd = _block_diag(lru_w_a[0]).astype(MXU_DTYPE)
    wx_bd = _block_diag(lru_w_x[0]).astype(MXU_DTYPE)
    gs = c // N_GROUPS
    gidx = jnp.arange(BD) // gs
    gm = jnp.where(gidx[:, None] == gidx[None, :], 1.0 / gs, 0.0).astype(MXU_DTYPE)

    ride = gather(0.3, 'ffn1_w_gate')
    h0, n1, target = _embed(x2, meta_full, loss_target[0], ffn1_pre_g, pad, "embed_prenorm", carried=[ride])
    (wg1,) = ride.results
    ride = gather(0.6, 'ffn1_w_up')
    g1 = _mm_nt(n1, wg1, "ffn1_gate", carried=[ride], out_dtype=MXU_DTYPE)
    (wu1,) = ride.results
    ride = gather(0.6, 'ffn1_w_down')
    u1, a1 = _ffn_up_act(n1, wu1, g1, "ffn1_up_act", carried=[ride])
    (wd1,) = ride.results
    ride = gather(0.5, 'w_in', 'w_out')
    fo1, h1, un = _mm_residual_norm(a1, wd1, h0, ffn1_post_g, 0.5, mix_pre_g, "ffn1_down", carried=[ride])
    win_t, wout = ride.results
    s2 = shard_rows['ffn2_w_gate']
    quarter = _round_up(s2 // 4, SUBLANE_BF16)
    ride_g = gather(0.5, 'ffn2_w_gate', part=(0, 3 * quarter))
    z = _mm_nt(un, win_t, "mix_in_proj", carried=[ride_g])
    ride_g = gather(0.5, 'ffn2_w_gate', part=(3 * quarter, s2 - 3 * quarter), into=ride_g.results[0])
    ride_u = gather(0.5, 'ffn2_w_up', part=(0, quarter))
    mixed, hs = _mixer_fwd(z, pv, wa_bd, wx_bd, gm, pad, "mixer_fwd", carried=[ride_g, ride_u])
    (wg2,) = ride_g.results
    ride_u = gather(0.5, 'ffn2_w_up', part=(quarter, s2 - quarter), into=ride_u.results[0])
    o_mix, h2, n2 = _mm_residual_norm(mixed, wout, h1, mix_post_g, 1.0, ffn2_pre_g, "mix_out_proj", carried=[ride_u])
    (wu2,) = ride_u.results
    ride = gather(0.5, 'ffn2_w_down')
    g2, u2, a2 = _ffn_gate_up(n2, wg2, wu2, "ffn2_gate_up", carried=[ride])
    (wd2,) = ride.results
    dh3, dfo2, d_post2, loss_part = _mm_residual_loss(a2, wd2, h2, ffn2_post_g, 0.5, target, lead, "ffn2_down_loss")
    loss = lax.psum(loss_part[0, 0], ("x", "y", "c"))

    chip_rel = [2 * (xi ^ (r >> 1)) + (yi ^ (r & 1)) for r in range(4)]
    where = jnp.stack([2 * k + ci for k in chip_rel] + chip_rel).astype(jnp.int32)
    red = {}

    def reduction(nme, grad):
        red[nme] = _GradReduction(nme, grad, shard_rows[nme], where)
        return red[nme]

    r_wd2 = reduction('ffn2_w_down', _mm_tn(a2, dfo2, "ffn2_dw_down"))
    dg2, du2 = _ffn_hidden_bwd(dfo2, wd2, g2, u2, "ffn2_hidden_bwd", carried=[r_wd2.swap()])
    r_wg2 = reduction('ffn2_w_gate', _mm_tn(dg2, n2, "ffn2_dw_gate", carried=[r_wd2.exchange(part=0)]))
    r_wu2 = reduction('ffn2_w_up', _mm_tn(du2, n2, "ffn2_dw_up", carried=[r_wd2.exchange(part=1), r_wg2.swap()]))
    dh2, d_pre2 = _mm_norm_bwd([(dg2, wg2), (du2, wu2)], h2, ffn2_pre_g, dh3, "ffn2_dx",
                               carried=[r_wg2.exchange(), r_wu2.swap()])
    do_mix, d_mix_post, dmixed = _norm_bwd_mm_nt(o_mix, mix_post_g, dh2, 1.0, wout, "mix_out_proj_bwd")
    r_wout = reduction('w_out', _mm_tn(mixed, do_mix, "mix_dw_out"))
    dz, dpv, dwa_bd, dwx_bd = _mixer_bwd(z, hs, dmixed, pv, wa_bd, wx_bd, gm, pad, "mixer_bwd",
                                         carried=[r_wu2.exchange(), r_wout.swap()])
    r_win = reduction('w_in', _mm_tn(dz, un, "mix_dw_in", carried=[r_wout.exchange()]))
    dh1, d_mix_pre, dfo1, d_post1 = _mm_norm_bwd([(dz, win_t)], h1, mix_pre_g, dh2, "mix_dx", carried=[r_win.swap()],
                                                 post=(fo1, ffn1_post_g, 0.5))
    r_wd1 = reduction('ffn1_w_down', _mm_tn(a1, dfo1, "ffn1_dw_down", carried=[r_win.exchange()]))
    early_names = ['mix_pre_g', 'mix_post_g', 'ffn2_pre_g', 'ffn2_post_g', 'ffn1_post_g',
                   'lru_conv_b', 'lru_b_a', 'lru_b_x', 'lru_lambda', 'lru_out_g', 'sconv_out_g',
                   'lru_conv_w', 'sconv_w', 'lru_w_a', 'lru_w_x']
    early_parts = [d_mix_pre, d_mix_post, d_pre2, d_post2, d_post1,
                   dpv[4:5], dpv[5:6], dpv[6:7], dpv[7:8], dpv[11:12], dpv[12:13],
                   dpv[0:4], dpv[8:11], _block_diag_extract(dwa_bd, hb), _block_diag_extract(dwx_bd, hb)]
    early_packed = _pack_rows(early_parts, d, SUBLANE_BF16)
    early_ride = _CarriedGather([early_packed], [N_DEV * early_packed.shape[0]], zeros, 0.5)
    dg1, du1 = _ffn_hidden_bwd(dfo1, wd1, g1, u1, "ffn1_hidden_bwd", carried=[r_wd1.swap(), early_ride])
    early_sum = _sum_stack(early_ride.results[0], "sum_small_early")
    r_wg1 = reduction('ffn1_w_gate', _mm_tn(dg1, n1, "ffn1_dw_gate", carried=[r_wd1.exchange(part=0)]))
    r_wu1 = reduction('ffn1_w_up', _mm_tn(du1, n1, "ffn1_dw_up", carried=[r_wd1.exchange(part=1), r_wg1.swap()]))
    row_tile = _norm_bwd_row_tile(m_rows)
    n_tiles = m_rows // row_tile
    half = n_tiles // 2
    assert half >= 1 and half * row_tile >= lead
    dh0_a, d_pre1_a = _mm_norm_bwd([(dg1, wg1), (du1, wu1)], h0, ffn1_pre_g, dh1, "ffn1_dx_a",
                                   carried=[r_wg1.exchange(), r_wu1.swap()], row_tiles=(0, half))
    dh0_b, d_pre1 = _mm_norm_bwd([(dg1, wg1), (du1, wu1)], h0, ffn1_pre_g, dh1, "ffn1_dx_b",
                                 carried=[r_wu1.exchange()], row_tiles=(half, n_tiles - half), dg_init=d_pre1_a)
    grad_x = jnp.concatenate([dh0_a[lead:], dh0_b], axis=0)[None]
    d_meta = dh0_a[pad:lead]

    grads, delta, new_m, new_v = {}, {}, {}, {}
    for nme in big:
        in_shard_layout = nme not in col_sharded or shard_rows[nme] % LANE != 0
        if in_shard_layout:
            view = (lambda t: t[0].T) if nme in col_sharded else (lambda t: t[0])
            back = (lambda t: t.T[None]) if nme in col_sharded else (lambda t: t[None])
            outs = red[nme].total_and_update(view(wts[nme]), view(mom[nme]), view(var[nme]))
            grads[nme], delta[nme], new_m[nme], new_v[nme] = [back(t) for t in outs]
        else:
            grads[nme] = red[nme].total().T[None]
            outs = _adamw(wts[nme][0], grads[nme][0], mom[nme][0], var[nme][0], "adamw_" + nme)
            delta[nme], new_m[nme], new_v[nme] = [t[None] for t in outs]

    late_names = ['ffn1_pre_g', 'meta_tokens']
    late_parts = [d_pre1, d_meta]
    late_sum = _gather_small(_pack_rows(late_parts, d), True, "reduce_small_late")
    small_sums = (_unpack_rows(early_sum, [p.shape for p in early_parts])
                  + _unpack_rows(late_sum, [p.shape for p in late_parts]))
    for nme, gsm in zip(early_names + late_names, small_sums):
        if nme == 'meta_tokens':
            grads[nme] = lax.dynamic_slice_in_dim(gsm, me * dm, dm, axis=1)
        elif nme in ('lru_conv_w', 'sconv_w'):
            grads[nme] = lax.dynamic_slice_in_dim(gsm, me * cs_, cs_, axis=1)[None]
        else:
            grads[nme] = gsm.reshape(wts[nme].shape)

    rest = [n for n in WEIGHT_NAMES if n not in big]
    rest_shapes = [wts[n].shape for n in rest]
    packed = [_pack_rows([src[n] for n in rest], LANE, 256) for src in (wts, grads, mom, var)]
    for out, packed_out in zip((delta, new_m, new_v), _adamw(*packed, "adamw_small")):
        for nme, arr in zip(rest, _unpack_rows(packed_out, rest_shapes)):
            out[nme] = arr

    return (loss, grad_x, *[grads[n] for n in WEIGHT_NAMES], *[delta[n] for n in WEIGHT_NAMES],
            *[new_m[n] for n in WEIGHT_NAMES], *[new_v[n] for n in WEIGHT_NAMES])
```

```python
import functools

import jax
import jax.numpy as jnp
from jax import lax
from jax.experimental import pallas as pl
from jax.experimental.pallas import tpu as pltpu

F32 = jnp.float32
MXU_DTYPE = jnp.bfloat16
WIRE_DTYPE = jnp.bfloat16
MESH = pl.DeviceIdType.MESH

EPS = 1e-6
LRU_C = 8.0
N_GROUPS = 16
ADAM_LR = 0.001
ADAM_B1 = 0.9
ADAM_B2 = 0.999
ADAM_EPS = 1e-08
ADAM_WD = 0.01
ADAM_STEP = 10

N_DEV = 8
LANE = 128
SUBLANE_BF16 = 16
ROW_ALIGN = 128
F_ALIGN = 512
BD = 256
K_TILE = 512
ACC_ROWS = 528
ACC_GROUP = 1
MIX_ROWS = 128
VMEM_LIMIT_MB = 56

WEIGHT_NAMES = ['meta_tokens', 'ffn1_pre_g', 'ffn1_w_gate', 'ffn1_w_up', 'ffn1_w_down', 'ffn1_post_g',
                'mix_pre_g', 'w_in', 'lru_conv_w', 'lru_conv_b', 'lru_w_a', 'lru_b_a', 'lru_w_x', 'lru_b_x',
                'lru_lambda', 'sconv_w', 'lru_out_g', 'sconv_out_g', 'w_out', 'mix_post_g', 'ffn2_pre_g',
                'ffn2_w_gate', 'ffn2_w_up', 'ffn2_w_down', 'ffn2_post_g']


def _round_up(n, q):
    return (n + q - 1) // q * q


def _tile(n, target, q):
    best = None
    t = q
    while t <= min(n, target):
        if n % t == 0:
            best = t
        t += q
    assert best is not None, (n, target, q)
    return best


def _params(**kw):
    return pltpu.CompilerParams(vmem_limit_bytes=VMEM_LIMIT_MB << 20, **kw)


def _call(body, *, grid, in_specs, out_specs, out_shape, name, args, scratch_shapes=(), carried=(), prefetch=()):
    carried = list(carried)
    n_pf = len(prefetch)

    def launch(fn, in_specs_, out_specs_, out_shape_, scratch_, operands, aliases_):
        if n_pf:
            spec = pltpu.PrefetchScalarGridSpec(num_scalar_prefetch=n_pf, grid=grid, in_specs=in_specs_,
                                                out_specs=out_specs_, scratch_shapes=scratch_)
            return pl.pallas_call(fn, grid_spec=spec, out_shape=out_shape_, input_output_aliases=aliases_,
                                  name=name, compiler_params=_params())(*prefetch, *operands)
        return pl.pallas_call(fn, grid=grid, in_specs=in_specs_, out_specs=out_specs_, out_shape=out_shape_,
                              scratch_shapes=scratch_, input_output_aliases=aliases_, name=name,
                              compiler_params=_params())(*operands)

    if not carried:
        return launch(body, in_specs, out_specs, out_shape, list(scratch_shapes), args, {})
    single = not isinstance(out_shape, (list, tuple))
    out_specs_l = [out_specs] if single else list(out_specs)
    out_shape_l = [out_shape] if single else list(out_shape)
    n_in, n_out, n_scr = len(in_specs), len(out_specs_l), len(scratch_shapes)
    hbm = pl.BlockSpec(memory_space=pl.ANY)
    c_in = [a for cm in carried for a in cm.arrays]
    c_out = [s for cm in carried for s in cm.out_shapes]
    c_scr = []
    aliases = {}
    in_off, out_off = n_pf + n_in, n_out
    for cm in carried:
        c_scr += [pltpu.SemaphoreType.DMA((cm.n_remote,)), pltpu.SemaphoreType.DMA((cm.n_remote,)),
                  pltpu.SemaphoreType.DMA((max(cm.n_local, 1),))]
        for k, v in cm.aliases.items():
            aliases[in_off + k] = out_off + v
        in_off += len(cm.arrays)
        out_off += len(cm.out_shapes)
    steps = 1
    for g in grid:
        steps *= g
    forward_steps = [min(int(cm.forward_at * steps), steps - 1) for cm in carried]

    def wrapped(*refs):
        pf = refs[:n_pf]
        p = n_pf
        ins = refs[p:p + n_in]
        p += n_in
        cins = refs[p:p + len(c_in)]
        p += len(c_in)
        outs = refs[p:p + n_out]
        p += n_out
        couts = refs[p:p + len(c_out)]
        p += len(c_out)
        scr = refs[p:p + n_scr]
        csem = refs[p + n_scr:]
        lin = 0
        for axis, g in enumerate(grid):
            lin = lin * g + pl.program_id(axis)
        views = []
        io = oo = 0
        for j, cm in enumerate(carried):
            views.append((cins[io:io + len(cm.arrays)], couts[oo:oo + len(cm.out_shapes)],
                          csem[3 * j], csem[3 * j + 1], csem[3 * j + 2]))
            io += len(cm.arrays)
            oo += len(cm.out_shapes)

        @pl.when(lin == 0)
        def _():
            for cm, v in zip(carried, views):
                cm.start(*v)

        body(*pf, *ins, *outs, *scr)

        for cm, v, step in zip(carried, views, forward_steps):
            pl.when(lin == step)(functools.partial(cm.forward, *v))

        @pl.when(lin == steps - 1)
        def _():
            for cm, v in zip(carried, views):
                cm.finish(*v)

    res = launch(wrapped, list(in_specs) + [hbm] * len(c_in), out_specs_l + [hbm] * len(c_out),
                 out_shape_l + c_out, list(scratch_shapes) + c_scr, (*args, *c_in), aliases)
    oo = n_out
    for cm in carried:
        cm.results = list(res[oo:oo + len(cm.out_shapes)])
        oo += len(cm.out_shapes)
    return res[0] if single else list(res[:n_out])


def _embed(x, meta, target, g, pad, name, carried=()):
    seq, d = x.shape
    n_meta = meta.shape[0]
    lead = pad + n_meta
    m = lead + seq
    tr = ROW_ALIGN
    lead_blocks = lead // tr
    meta_row = pad - (lead_blocks - 1) * tr
    assert lead % tr == 0 and seq % tr == 0 and 0 <= meta_row and meta_row % 8 == 0

    def body(x_ref, meta_ref, t_ref, g_ref, h_ref, n_ref, tp_ref):
        i = pl.program_id(0)

        @pl.when(i < lead_blocks)
        def _():
            h_ref[...] = jnp.zeros_like(h_ref)
            tp_ref[...] = jnp.zeros_like(tp_ref)

        @pl.when(i == lead_blocks - 1)
        def _():
            h_ref[pl.ds(meta_row, n_meta), :] = meta_ref[...]

        @pl.when(i >= lead_blocks)
        def _():
            h_ref[...] = x_ref[...]
            tp_ref[...] = t_ref[...]

        h = h_ref[...]
        r = lax.rsqrt(jnp.mean(h * h, axis=-1, keepdims=True) + EPS)
        n_ref[...] = (h * r * g_ref[...]).astype(n_ref.dtype)

    tokens = pl.BlockSpec((tr, d), lambda i: (jnp.maximum(i - lead_blocks, 0), 0))
    rows = pl.BlockSpec((tr, d), lambda i: (i, 0))
    return _call(
        body, grid=(m // tr,),
        in_specs=[tokens, pl.BlockSpec((n_meta, d), lambda i: (0, 0)), tokens, pl.BlockSpec((1, d), lambda i: (0, 0))],
        out_specs=[rows, rows, rows],
        out_shape=[jax.ShapeDtypeStruct((m, d), F32), jax.ShapeDtypeStruct((m, d), MXU_DTYPE),
                   jax.ShapeDtypeStruct((m, d), F32)],
        name=name, args=(x, meta, target, g), carried=carried)


def _rmsnorm_bwd_rows(x, g, dy):
    r = lax.rsqrt(jnp.mean(x * x, axis=-1, keepdims=True) + EPS)
    xh = x * r
    dyh = dy * g
    dx = r * (dyh - xh * jnp.mean(dyh * xh, axis=-1, keepdims=True))
    return dx, dy * xh


def _dot_nt(a, b):
    return lax.dot_general(a, b, (((1,), (1,)), ((), ())), preferred_element_type=F32)


def _dot_tn(a, b):
    return lax.dot_general(a, b, (((0,), (0,)), ((), ())), preferred_element_type=F32)


def _mm_nt(a, w, name, carried=(), out_dtype=F32):
    m, k = a.shape
    n = w.shape[0]
    tm = _tile(m, 1056, SUBLANE_BF16)
    tn = _tile(n, 512, LANE)

    def body(a_ref, w_ref, o_ref):
        o_ref[...] = _dot_nt(a_ref[...], w_ref[...]).astype(o_ref.dtype)

    return _call(
        body, grid=(m // tm, n // tn),
        in_specs=[pl.BlockSpec((tm, k), lambda i, j: (i, 0)), pl.BlockSpec((tn, k), lambda i, j: (j, 0))],
        out_specs=pl.BlockSpec((tm, tn), lambda i, j: (i, j)),
        out_shape=jax.ShapeDtypeStruct((m, n), out_dtype), name=name, args=(a, w), carried=carried)


def _norm_bwd_mm_nt(x, g, dy, scale, w, name, carried=()):
    m, d = x.shape
    n = w.shape[0]
    tm = _tile(m, 528, SUBLANE_BF16)

    def body(x_ref, g_ref, dy_ref, w_ref, dx_ref, dg_ref, o_ref):
        @pl.when(pl.program_id(0) == 0)
        def _():
            dg_ref[...] = jnp.zeros_like(dg_ref)

        dx, dgr = _rmsnorm_bwd_rows(x_ref[...], g_ref[...], scale * dy_ref[...])
        dxb = dx.astype(dx_ref.dtype)
        dx_ref[...] = dxb
        dg_ref[...] += jnp.sum(dgr, axis=0, keepdims=True)
        o_ref[...] = _dot_nt(dxb, w_ref[...])

    row = pl.BlockSpec((tm, d), lambda i: (i, 0))
    vec = pl.BlockSpec((1, d), lambda i: (0, 0))
    return _call(
        body, grid=(m // tm,),
        in_specs=[row, vec, row, pl.BlockSpec((n, d), lambda i: (0, 0), pipeline_mode=pl.Buffered(1))],
        out_specs=[row, vec, pl.BlockSpec((tm, n), lambda i: (i, 0))],
        out_shape=[jax.ShapeDtypeStruct((m, d), MXU_DTYPE), jax.ShapeDtypeStruct((1, d), F32),
                   jax.ShapeDtypeStruct((m, n), F32)],
        name=name, args=(x, g, dy, w), carried=carried)


def _ffn_up_act(n_act, wu_t, g_act, name, carried=()):
    m, d = n_act.shape
    fp = wu_t.shape[0]
    tm = _tile(m, 1056, SUBLANE_BF16)
    tn = _tile(fp, 512, LANE)

    def body(n_ref, wu_ref, g_ref, u_ref, a_ref):
        u = _dot_nt(n_ref[...], wu_ref[...])
        g = g_ref[...].astype(F32)
        u_ref[...] = u.astype(u_ref.dtype)
        a_ref[...] = (g * jax.nn.sigmoid(g) * u).astype(a_ref.dtype)

    act = pl.BlockSpec((tm, tn), lambda i, j: (i, j))
    return _call(
        body, grid=(m // tm, fp // tn),
        in_specs=[pl.BlockSpec((tm, d), lambda i, j: (i, 0)), pl.BlockSpec((tn, d), lambda i, j: (j, 0)), act],
        out_specs=[act, act],
        out_shape=[jax.ShapeDtypeStruct((m, fp), MXU_DTYPE)] * 2, name=name, args=(n_act, wu_t, g_act), carried=carried)


def _ffn_gate_up(n_act, wg_t, wu_t, name, carried=()):
    m, d = n_act.shape
    fp = wg_t.shape[0]
    tm = _tile(m, 1056, SUBLANE_BF16)
    tn = _tile(fp, 512, LANE)

    def body(n_ref, wg_ref, wu_ref, g_ref, u_ref, a_ref):
        n = n_ref[...]
        g = _dot_nt(n, wg_ref[...])
        u = _dot_nt(n, wu_ref[...])
        g_ref[...] = g.astype(g_ref.dtype)
        u_ref[...] = u.astype(u_ref.dtype)
        a_ref[...] = (g * jax.nn.sigmoid(g) * u).astype(a_ref.dtype)

    act = pl.BlockSpec((tm, tn), lambda i, j: (i, j))
    wsp = pl.BlockSpec((tn, d), lambda i, j: (j, 0))
    return _call(
        body, grid=(m // tm, fp // tn),
        in_specs=[pl.BlockSpec((tm, d), lambda i, j: (i, 0)), wsp, wsp],
        out_specs=[act, act, act],
        out_shape=[jax.ShapeDtypeStruct((m, fp), MXU_DTYPE)] * 3, name=name, args=(n_act, wg_t, wu_t), carried=carried)


def _ffn_hidden_bwd(dfo, wd, g_act, u_act, name, carried=()):
    m, d = dfo.shape
    fp = wd.shape[0]
    tm = _tile(m, 1056, SUBLANE_BF16)
    tn = _tile(fp, 512, LANE)

    def body(df_ref, wd_ref, g_ref, u_ref, dg_ref, du_ref):
        da = _dot_nt(df_ref[...], wd_ref[...]).astype(dg_ref.dtype)
        g = g_ref[...]
        u = u_ref[...]
        s = jax.nn.sigmoid(g)
        du_ref[...] = da * (g * s)
        dg_ref[...] = da * (u * (s * (1.0 + g * (1.0 - s))))

    act = pl.BlockSpec((tm, tn), lambda i, j: (i, j))
    return _call(
        body, grid=(m // tm, fp // tn),
        in_specs=[pl.BlockSpec((tm, d), lambda i, j: (i, 0)), pl.BlockSpec((tn, d), lambda i, j: (j, 0)), act, act],
        out_specs=[act, act],
        out_shape=[jax.ShapeDtypeStruct((m, fp), MXU_DTYPE)] * 2, name=name, args=(dfo, wd, g_act, u_act),
        carried=carried)


def _row_groups(n_tiles, max_group, nk):
    gsz = max(q for q in range(1, max_group + 1) if n_tiles % q == 0)

    def epilogue_row(grp, kk, i):
        return grp * gsz + jnp.where(kk == nk - 1, i, 0)

    return gsz, epilogue_row


def _mm_residual_norm(a, w, h, g, scale, next_g, name, carried=()):
    m, k = a.shape
    d = w.shape[1]
    tm = _tile(m, ACC_ROWS, SUBLANE_BF16)
    tk = _tile(k, K_TILE, LANE)
    nk = k // tk
    gsz, epilogue_row = _row_groups(m // tm, ACC_GROUP, nk)

    def body(a_ref, w_ref, h_ref, g_ref, ng_ref, fo_ref, hn_ref, nn_ref, acc_ref):
        kk, i = pl.program_id(1), pl.program_id(2)

        @pl.when(kk == 0)
        def _():
            acc_ref[i] = jnp.zeros((tm, d), F32)

        acc_ref[i] += jnp.dot(a_ref[...], w_ref[...], preferred_element_type=F32)

        @pl.when(kk == nk - 1)
        def _():
            fo = acc_ref[i]
            fo_ref[...] = fo
            r = lax.rsqrt(jnp.mean(fo * fo, axis=-1, keepdims=True) + EPS)
            hn = h_ref[...] + scale * (fo * r * g_ref[...])
            hn_ref[...] = hn
            rn = lax.rsqrt(jnp.mean(hn * hn, axis=-1, keepdims=True) + EPS)
            nn_ref[...] = (hn * rn * ng_ref[...]).astype(nn_ref.dtype)

    row = pl.BlockSpec((tm, d), lambda grp, kk, i: (epilogue_row(grp, kk, i), 0))
    row_once = pl.BlockSpec((tm, d), lambda grp, kk, i: (epilogue_row(grp, kk, i), 0), pipeline_mode=pl.Buffered(1))
    vec = pl.BlockSpec((1, d), lambda grp, kk, i: (0, 0))
    return _call(
        body, grid=(m // tm // gsz, nk, gsz),
        in_specs=[pl.BlockSpec((tm, tk), lambda grp, kk, i: (grp * gsz + i, kk)),
                  pl.BlockSpec((tk, d), lambda grp, kk, i: (kk, 0)), row_once, vec, vec],
        out_specs=[row, row, row],
        out_shape=[jax.ShapeDtypeStruct((m, d), F32)] * 2 + [jax.ShapeDtypeStruct((m, d), MXU_DTYPE)],
        scratch_shapes=[pltpu.VMEM((gsz, tm, d), F32)], name=name, args=(a, w, h, g, next_g), carried=carried)


def _mm_residual_loss(a, w, h, g, scale, target, lead, name, carried=()):
    m, k = a.shape
    d = w.shape[1]
    tm = _tile(m, ACC_ROWS, SUBLANE_BF16)
    tk = _tile(k, K_TILE, LANE)
    nk = k // tk
    gsz, epilogue_row = _row_groups(m // tm, ACC_GROUP, nk)

    def body(a_ref, w_ref, h_ref, g_ref, t_ref, dy_ref, dfo_ref, dg_ref, l_ref, acc_ref):
        grp, kk, i = pl.program_id(0), pl.program_id(1), pl.program_id(2)

        @pl.when(jnp.logical_and(jnp.logical_and(grp == 0, kk == 0), i == 0))
        def _():
            dg_ref[...] = jnp.zeros_like(dg_ref)
            l_ref[...] = jnp.zeros_like(l_ref)

        @pl.when(kk == 0)
        def _():
            acc_ref[i] = jnp.zeros((tm, d), F32)

        acc_ref[i] += jnp.dot(a_ref[...], w_ref[...], preferred_element_type=F32)

        @pl.when(kk == nk - 1)
        def _():
            fo = acc_ref[i]
            gain = g_ref[...]
            r = lax.rsqrt(jnp.mean(fo * fo, axis=-1, keepdims=True) + EPS)
            xh = fo * r
            y = h_ref[...] + scale * (xh * gain)
            row = (grp * gsz + i) * tm + lax.broadcasted_iota(jnp.int32, (tm, 1), 0)
            e = jnp.where(row >= lead, y - t_ref[...], 0.0)
            dy = e * (1.0 / d)
            dy_ref[...] = dy
            l_ref[...] += 0.5 * jnp.sum(jnp.sum(e * e, axis=-1, keepdims=True) * (1.0 / d), axis=0, keepdims=True)
            dn = scale * dy
            dyh = dn * gain
            dfo_ref[...] = (r * (dyh - xh * jnp.mean(dyh * xh, axis=-1, keepdims=True))).astype(dfo_ref.dtype)
            dg_ref[...] += jnp.sum(dn * xh, axis=0, keepdims=True)

    row = pl.BlockSpec((tm, d), lambda grp, kk, i: (epilogue_row(grp, kk, i), 0))
    row_once = pl.BlockSpec((tm, d), lambda grp, kk, i: (epilogue_row(grp, kk, i), 0), pipeline_mode=pl.Buffered(1))
    vec = pl.BlockSpec((1, d), lambda grp, kk, i: (0, 0))
    return _call(
        body, grid=(m // tm // gsz, nk, gsz),
        in_specs=[pl.BlockSpec((tm, tk), lambda grp, kk, i: (grp * gsz + i, kk)),
                  pl.BlockSpec((tk, d), lambda grp, kk, i: (kk, 0)), row_once, vec, row_once],
        out_specs=[row, row, vec, pl.BlockSpec((1, 1), lambda grp, kk, i: (0, 0))],
        out_shape=[jax.ShapeDtypeStruct((m, d), F32), jax.ShapeDtypeStruct((m, d), MXU_DTYPE),
                   jax.ShapeDtypeStruct((1, d), F32), jax.ShapeDtypeStruct((1, 1), F32)],
        scratch_shapes=[pltpu.VMEM((gsz, tm, d), F32)], name=name, args=(a, w, h, g, target), carried=carried)


def _norm_bwd_row_tile(m):
    return _tile(m, ACC_ROWS, SUBLANE_BF16)


def _mm_norm_bwd(pairs, h, g, dh_up, name, carried=(), row_tiles=None, dg_init=None, post=None):
    n_pairs = len(pairs)
    m, k = pairs[0][0].shape
    d = h.shape[1]
    tm = _norm_bwd_row_tile(m)
    tk = _tile(k, K_TILE, LANE)
    nk = k // tk
    t0, nt = row_tiles if row_tiles is not None else (0, m // tm)
    gsz, epilogue_row = _row_groups(nt, ACC_GROUP, nk)
    if dg_init is None:
        dg_init = jnp.zeros((1, d), F32)

    n_post = 0 if post is None else 2

    def body(*refs):
        ops = refs[:2 * n_pairs]
        h_ref, g_ref, up_ref, init_ref = refs[2 * n_pairs:2 * n_pairs + 4]
        post_in = refs[2 * n_pairs + 4:2 * n_pairs + 4 + n_post]
        dh_ref, dg_ref = refs[2 * n_pairs + 4 + n_post:2 * n_pairs + 6 + n_post]
        post_out = refs[2 * n_pairs + 6 + n_post:2 * n_pairs + 6 + 2 * n_post]
        acc_ref = refs[-1]
        grp, kk, i = pl.program_id(0), pl.program_id(1), pl.program_id(2)

        @pl.when(jnp.logical_and(jnp.logical_and(grp == 0, kk == 0), i == 0))
        def _():
            dg_ref[...] = init_ref[...]
            if post is not None:
                post_out[1][...] = jnp.zeros_like(post_out[1])

        @pl.when(kk == 0)
        def _():
            acc_ref[i] = jnp.zeros((tm, d), F32)

        for p in range(n_pairs):
            acc_ref[i] += jnp.dot(ops[2 * p][...], ops[2 * p + 1][...], preferred_element_type=F32)

        @pl.when(kk == nk - 1)
        def _():
            dx, dgr = _rmsnorm_bwd_rows(h_ref[...], g_ref[...], acc_ref[i])
            dh = up_ref[...] + dx
            dh_ref[...] = dh
            dg_ref[...] += jnp.sum(dgr, axis=0, keepdims=True)
            if post is not None:
                dfo, dpr = _rmsnorm_bwd_rows(post_in[0][...], post_in[1][...], post[2] * dh)
                post_out[0][...] = dfo.astype(post_out[0].dtype)
                post_out[1][...] += jnp.sum(dpr, axis=0, keepdims=True)

    row_in = pl.BlockSpec((tm, d), lambda grp, kk, i: (t0 + epilogue_row(grp, kk, i), 0))
    row_out = pl.BlockSpec((tm, d), lambda grp, kk, i: (epilogue_row(grp, kk, i), 0))
    vec = pl.BlockSpec((1, d), lambda grp, kk, i: (0, 0))
    in_specs = []
    args = []
    for a, w in pairs:
        in_specs += [pl.BlockSpec((tm, tk), lambda grp, kk, i: (t0 + grp * gsz + i, kk)),
                     pl.BlockSpec((tk, d), lambda grp, kk, i: (kk, 0))]
        args += [a, w]
    in_specs += [row_in, vec, row_in, vec]
    args += [h, g, dh_up, dg_init]
    out_specs = [row_out, vec]
    out_shape = [jax.ShapeDtypeStruct((nt * tm, d), F32), jax.ShapeDtypeStruct((1, d), F32)]
    if post is not None:
        in_specs += [row_in, vec]
        args += [post[0], post[1]]
        out_specs += [row_out, vec]
        out_shape += [jax.ShapeDtypeStruct((nt * tm, d), MXU_DTYPE), jax.ShapeDtypeStruct((1, d), F32)]
    return _call(
        body, grid=(nt // gsz, nk, gsz), in_specs=in_specs, out_specs=out_specs, out_shape=out_shape,
        scratch_shapes=[pltpu.VMEM((gsz, tm, d), F32)], name=name, args=tuple(args), carried=carried)


def _mm_tn(a, b, name, carried=()):
    m, ka = a.shape
    d = b.shape[1]
    tf = _tile(ka, 512, LANE)

    def body(a_ref, b_ref, o_ref):
        o_ref[...] = _dot_tn(a_ref[...], b_ref[...]).astype(o_ref.dtype)

    return _call(
        body, grid=(ka // tf,),
        in_specs=[pl.BlockSpec((m, tf), lambda j: (0, j)),
                  pl.BlockSpec((m, d), lambda j: (0, 0), pipeline_mode=pl.Buffered(1))],
        out_specs=pl.BlockSpec((tf, d), lambda j: (j, 0)),
        out_shape=jax.ShapeDtypeStruct((ka, d), WIRE_DTYPE), name=name, args=(a, b), carried=carried)


GELU_K = 0.7978845608028654
GELU_C = 0.044715


def _expm1(x):
    series = x * (1.0 + x * (1.0 / 2 + x * (1.0 / 6 + x * (1.0 / 24 + x * (1.0 / 120)))))
    return jnp.where(jnp.abs(x) < 0.1, series, jnp.exp(x) - 1.0)


def _softplus(x):
    return jnp.maximum(x, 0.0) + jnp.log1p(jnp.exp(-jnp.abs(x)))


def _block_mm(v, w_ref, transposed):
    nbk = w_ref.shape[0]
    outs = []
    for j in range(nbk):
        vj = v[:, j * BD:(j + 1) * BD]
        outs.append(_dot_nt(vj, w_ref[j]) if transposed else jnp.dot(vj, w_ref[j], preferred_element_type=F32))
    return outs[0] if nbk == 1 else jnp.concatenate(outs, axis=1)


def _group_mean(q, gm_ref):
    hi = q.astype(MXU_DTYPE)
    lo = (q - hi.astype(F32)).astype(MXU_DTYPE)
    nbk = q.shape[1] // BD
    gm = gm_ref[...]
    outs = []
    for j in range(nbk):
        sl = slice(j * BD, (j + 1) * BD)
        outs.append(jnp.dot(hi[:, sl], gm, preferred_element_type=F32) + jnp.dot(lo[:, sl], gm, preferred_element_type=F32))
    return outs[0] if nbk == 1 else jnp.concatenate(outs, axis=1)


class _RowReader:
    def __init__(self, ref):
        self.ref = ref

    def __getitem__(self, rows):
        return self.ref[rows, :]


def _shifted(ext_ref, cur, before8, after8, downs=(), ups=()):
    r = cur.shape[0]
    if downs:
        ext_ref[0:8, :] = before8
    ext_ref[8:8 + r, :] = cur
    if ups:
        ext_ref[8 + r:16 + r, :] = after8
    return [ext_ref[pl.ds(8 - j, r), :] for j in downs] + [ext_ref[pl.ds(8 + j, r), :] for j in ups]


def _lru_gates(xc, pv, wa_ref, wx_ref):
    xcb = xc.astype(MXU_DTYPE)
    ga = jax.nn.sigmoid(_block_mm(xcb, wa_ref, False) + pv[5:6])
    gx = jax.nn.sigmoid(_block_mm(xcb, wx_ref, False) + pv[6:7])
    sp = _softplus(-pv[7:8])
    log_a = -LRU_C * ga * sp
    a = jnp.exp(log_a)
    e2 = _expm1(2.0 * log_a)
    mult = jnp.sqrt(-e2)
    return xcb, ga, gx, sp, a, e2, mult


def _gelu_parts(y):
    th = jnp.tanh(GELU_K * (y + GELU_C * y * y * y))
    return 0.5 * y * (1.0 + th), th


def _scan_block(a, u, sa_ref, su_ref, carry_ref, out_ref, reverse):
    r, c = a.shape
    n = r // 8
    a3 = a.reshape(n, 8, c)
    u3 = u.reshape(n, 8, c)
    sub = lax.broadcasted_iota(jnp.int32, (n, 8, c), 1)
    for dlt in (1, 2, 4):
        keep = (sub < 8 - dlt) if reverse else (sub >= dlt)
        shift = 8 - dlt if reverse else dlt
        sh_a = pltpu.roll(a3, shift, axis=1)
        sh_u = pltpu.roll(u3, shift, axis=1)
        u3 = u3 + a3 * jnp.where(keep, sh_u, 0.0)
        a3 = a3 * jnp.where(keep, sh_a, 1.0)
    sa_ref[...] = a3.reshape(r, c)
    su_ref[...] = u3.reshape(r, c)
    for k in (range(n - 1, -1, -1) if reverse else range(n)):
        rows = pl.ds(8 * k, 8)
        out_ref[rows, :] = su_ref[rows, :] + sa_ref[rows, :] * carry_ref[...]
        carry_ref[...] = out_ref[pl.ds(8 * k if reverse else 8 * k + 7, 1), :]


def _mixer_fwd(z, pv, wa, wx, gm, pad, name, carried=()):
    m = z.shape[0]
    c = pv.shape[1]
    r = MIX_ROWS
    nb = m // r

    def body(z_ref, pv_ref, wa_ref, wx_ref, gm_ref, mixed_ref, hs_ref, ext_ref, tailx_ref, tailc_ref, carry_ref,
             sa_ref, su_ref):
        b = pl.program_id(0)

        @pl.when(b == 0)
        def _():
            tailx_ref[...] = jnp.zeros_like(tailx_ref)
            tailc_ref[...] = jnp.zeros_like(tailc_ref)
            carry_ref[...] = jnp.zeros_like(carry_ref)

        pv = _RowReader(pv_ref)
        row = b * r + lax.broadcasted_iota(jnp.int32, (r, 1), 0)
        maskf = (row >= pad).astype(F32)
        y = z_ref[:, 0:c]
        xl = z_ref[:, c:2 * c]
        bs = z_ref[:, 2 * c:3 * c]
        cv = z_ref[:, 3 * c:4 * c] * z_ref[:, 4 * c:5 * c]

        x1, x2, x3 = _shifted(ext_ref, xl, tailx_ref[...], None, downs=(1, 2, 3))
        tailx_ref[...] = z_ref[pl.ds(r - 8, 8), c:2 * c]
        xc = pv[4:5] + pv[3:4] * xl + pv[2:3] * x1 + pv[1:2] * x2 + pv[0:1] * x3
        _, _, gx, _, a, _, mult = _lru_gates(xc, pv, wa_ref, wx_ref)
        uu = mult * (gx * xc) * maskf

        _scan_block(a, uu, sa_ref, su_ref, carry_ref, hs_ref, reverse=False)
        hs = hs_ref[...]

        gelu_y, _ = _gelu_parts(y)
        lru_out = hs * gelu_y
        c1, c2 = _shifted(ext_ref, cv, tailc_ref[...], None, downs=(1, 2))
        tailc_ref[...] = cv[r - 8:r]
        sc_out = bs * (pv[10:11] * cv + pv[9:10] * c1 + pv[8:9] * c2)

        rl = lax.rsqrt(_group_mean(lru_out * lru_out, gm_ref) + EPS)
        rs = lax.rsqrt(_group_mean(sc_out * sc_out, gm_ref) + EPS)
        mixed_ref[:, 0:c] = (lru_out * rl * pv[11:12]).astype(mixed_ref.dtype)
        mixed_ref[:, c:2 * c] = (sc_out * rs * pv[12:13]).astype(mixed_ref.dtype)

    full = lambda shape: pl.BlockSpec(shape, lambda b: (0,) * len(shape))
    return _call(
        body, grid=(nb,),
        in_specs=[pl.BlockSpec((r, 5 * c), lambda b: (b, 0)), full(pv.shape), full(wa.shape), full(wx.shape), full(gm.shape)],
        out_specs=[pl.BlockSpec((r, 2 * c), lambda b: (b, 0)), pl.BlockSpec((r, c), lambda b: (b, 0))],
        out_shape=[jax.ShapeDtypeStruct((m, 2 * c), MXU_DTYPE), jax.ShapeDtypeStruct((m, c), F32)],
        scratch_shapes=[pltpu.VMEM((r + 16, c), F32), pltpu.VMEM((8, c), F32), pltpu.VMEM((8, c), F32),
                        pltpu.VMEM((1, c), F32), pltpu.VMEM((r, c), F32), pltpu.VMEM((r, c), F32)],
        name=name, args=(z, pv, wa, wx, gm), carried=carried)


def _mixer_bwd(z, hs, dmixed, pv, wa, wx, gm, pad, name, carried=()):
    m = z.shape[0]
    c = pv.shape[1]
    r = MIX_ROWS
    nb = m // r
    r8 = r // 8
    assert pad <= r and pad % SUBLANE_BF16 == 0

    def body(z_ref, zp_ref, hs_ref, hsp_ref, dm_ref, pv_ref, wa_ref, wx_ref, gm_ref,
             dz_ref, dpv_ref, dwa_ref, dwx_ref, ext_ref, hxc_ref, hsc_ref, hp_ref, pc_ref, sa_ref, su_ref, p_ref):
        i = pl.program_id(0)
        b = nb - 1 - i

        @pl.when(i == 0)
        def _():
            hxc_ref[...] = jnp.zeros_like(hxc_ref)
            hsc_ref[...] = jnp.zeros_like(hsc_ref)
            hp_ref[...] = jnp.zeros_like(hp_ref)
            pc_ref[...] = jnp.zeros_like(pc_ref)
            dpv_ref[...] = jnp.zeros_like(dpv_ref)
            dwa_ref[...] = jnp.zeros_like(dwa_ref)
            dwx_ref[...] = jnp.zeros_like(dwx_ref)

        pv = _RowReader(pv_ref)
        row = b * r + lax.broadcasted_iota(jnp.int32, (r, 1), 0)
        maskf = (row >= pad).astype(F32)
        has_prev = (b > 0).astype(F32)
        y = z_ref[:, 0:c]
        xl = z_ref[:, c:2 * c]
        bs = z_ref[:, 2 * c:3 * c]
        cs = z_ref[:, 3 * c:4 * c]
        vs = z_ref[:, 4 * c:5 * c]
        cv = cs * vs
        xl_prev = zp_ref[:, c:2 * c] * has_prev
        cv_prev = zp_ref[:, 3 * c:4 * c] * zp_ref[:, 4 * c:5 * c] * has_prev
        hs = hs_ref[...]

        x1, x2, x3 = _shifted(ext_ref, xl, xl_prev, None, downs=(1, 2, 3))
        xc = pv[4:5] + pv[3:4] * xl + pv[2:3] * x1 + pv[1:2] * x2 + pv[0:1] * x3
        xcb, ga, gx, sp, a, e2, mult = _lru_gates(xc, pv, wa_ref, wx_ref)
        gxx = gx * xc
        gelu_y, th = _gelu_parts(y)
        lru_out = hs * gelu_y
        c1, c2 = _shifted(ext_ref, cv, cv_prev, None, downs=(1, 2))
        sc = pv[10:11] * cv + pv[9:10] * c1 + pv[8:9] * c2
        sc_out = bs * sc

        def group_norm_bwd(v, dm, gain):
            rr = lax.rsqrt(_group_mean(v * v, gm_ref) + EPS)
            vh = v * rr
            dvh = dm * gain
            dv = rr * (dvh - vh * _group_mean(dvh * vh, gm_ref))
            return dv, jnp.sum(dm * vh, axis=0, keepdims=True)

        d_lru_out, d_og = group_norm_bwd(lru_out, dm_ref[:, 0:c], pv[11:12])
        d_sc_out, d_sg = group_norm_bwd(sc_out, dm_ref[:, c:2 * c], pv[12:13])
        dpv_ref[11:12, :] += d_og
        dpv_ref[12:13, :] += d_sg

        dhs = d_lru_out * gelu_y
        dgelu = 0.5 * (1.0 + th) + 0.5 * y * (1.0 - th * th) * GELU_K * (1.0 + 3.0 * GELU_C * y * y)
        dy = d_lru_out * hs * dgelu

        _scan_block(a, a * dhs, sa_ref, su_ref, pc_ref, p_ref, reverse=True)
        (p_next,) = _shifted(ext_ref, p_ref[...], None, hp_ref[...], ups=(1,))
        hp_ref[...] = p_ref[0:8, :]
        q = dhs + p_next
        (hs_prev,) = _shifted(ext_ref, hs, hsp_ref[...] * has_prev, None, downs=(1,))
        duu = q * maskf
        da = q * hs_prev

        dmult = duu * gxx
        dgxx = duu * mult
        dgx = dgxx * xc
        dxc = dgxx * gx
        dlog_a = da * a - dmult * ((1.0 + e2) / mult)
        dga = dlog_a * (-LRU_C * sp)
        dsp = jnp.sum(dlog_a * (-LRU_C * ga), axis=0, keepdims=True)
        dpv_ref[7:8, :] += dsp * (-jax.nn.sigmoid(-pv[7:8]))
        dga_pre = dga * ga * (1.0 - ga)
        dgx_pre = dgx * gx * (1.0 - gx)
        dpv_ref[5:6, :] += jnp.sum(dga_pre, axis=0, keepdims=True)
        dpv_ref[6:7, :] += jnp.sum(dgx_pre, axis=0, keepdims=True)
        dga_b = dga_pre.astype(MXU_DTYPE)
        dgx_b = dgx_pre.astype(MXU_DTYPE)
        dxc = dxc + _block_mm(dga_b, wa_ref, True) + _block_mm(dgx_b, wx_ref, True)
        for j in range(c // BD):
            sl = slice(j * BD, (j + 1) * BD)
            dwa_ref[j] += _dot_tn(xcb[:, sl], dga_b[:, sl])
            dwx_ref[j] += _dot_tn(xcb[:, sl], dgx_b[:, sl])

        dpv_ref[4:5, :] += jnp.sum(dxc, axis=0, keepdims=True)
        dpv_ref[3:4, :] += jnp.sum(dxc * xl, axis=0, keepdims=True)
        dpv_ref[2:3, :] += jnp.sum(dxc * x1, axis=0, keepdims=True)
        dpv_ref[1:2, :] += jnp.sum(dxc * x2, axis=0, keepdims=True)
        dpv_ref[0:1, :] += jnp.sum(dxc * x3, axis=0, keepdims=True)
        u1, u2, u3 = _shifted(ext_ref, dxc, None, hxc_ref[...], ups=(1, 2, 3))
        hxc_ref[...] = dxc[0:8]
        dxl = pv[3:4] * dxc + pv[2:3] * u1 + pv[1:2] * u2 + pv[0:1] * u3

        dbs = d_sc_out * sc
        dsc = d_sc_out * bs
        dpv_ref[10:11, :] += jnp.sum(dsc * cv, axis=0, keepdims=True)
        dpv_ref[9:10, :] += jnp.sum(dsc * c1, axis=0, keepdims=True)
        dpv_ref[8:9, :] += jnp.sum(dsc * c2, axis=0, keepdims=True)
        s1, s2 = _shifted(ext_ref, dsc, None, hsc_ref[...], ups=(1, 2))
        hsc_ref[...] = dsc[0:8]
        dcv = pv[10:11] * dsc + pv[9:10] * s1 + pv[8:9] * s2

        dz_ref[:, 0:c] = dy.astype(dz_ref.dtype)
        dz_ref[:, c:2 * c] = dxl.astype(dz_ref.dtype)
        dz_ref[:, 2 * c:3 * c] = dbs.astype(dz_ref.dtype)
        dz_ref[:, 3 * c:4 * c] = (dcv * vs).astype(dz_ref.dtype)
        dz_ref[:, 4 * c:5 * c] = (dcv * cs).astype(dz_ref.dtype)

        if pad:
            @pl.when(b == 0)
            def _():
                dz_ref[0:pad, :] = jnp.zeros((pad, 5 * c), dz_ref.dtype)

    full = lambda shape: pl.BlockSpec(shape, lambda i: (0,) * len(shape))
    cur = lambda width: pl.BlockSpec((r, width), lambda i: (nb - 1 - i, 0))
    prev8 = lambda width: pl.BlockSpec((8, width), lambda i: (jnp.maximum((nb - 1 - i) * r8 - 1, 0), 0))
    return _call(
        body, grid=(nb,),
        in_specs=[cur(5 * c), prev8(5 * c), cur(c), prev8(c), cur(2 * c),
                  full(pv.shape), full(wa.shape), full(wx.shape), full(gm.shape)],
        out_specs=[cur(5 * c), full(pv.shape), full(wa.shape), full(wx.shape)],
        out_shape=[jax.ShapeDtypeStruct((m, 5 * c), MXU_DTYPE), jax.ShapeDtypeStruct(pv.shape, F32),
                   jax.ShapeDtypeStruct(wa.shape, F32), jax.ShapeDtypeStruct(wx.shape, F32)],
        scratch_shapes=[pltpu.VMEM((r + 16, c), F32), pltpu.VMEM((8, c), F32), pltpu.VMEM((8, c), F32),
                        pltpu.VMEM((8, c), F32), pltpu.VMEM((1, c), F32), pltpu.VMEM((r, c), F32),
                        pltpu.VMEM((r, c), F32), pltpu.VMEM((r, c), F32)],
        name=name, args=(z, z, hs, hs, dmixed, pv, wa, wx, gm), carried=carried)


def _position():
    return lax.axis_index("x"), lax.axis_index("y"), lax.axis_index("c")


def _block_of(px, py, pc):
    return 4 * px + 2 * py + pc


class _TwoLevelGather:
    def __init__(self, n_arrays, rows_of, src_of, send_sems, recv_sems):
        x, y, c = _position()
        self.n, self.rows_of, self.src_of = n_arrays, rows_of, src_of
        self.send_sems, self.recv_sems = send_sems, recv_sems
        self.c, self.me, self.sibling = c, (x, y, c), (x, y, 1 - c)
        self.chips = [(1 - x, y), (x, 1 - y), (1 - x, 1 - y)]

    def _copy(self, i, k, block, to, src=None):
        return pltpu.make_async_remote_copy(
            src_ref=self.rows_of(i, *block) if src is None else src, dst_ref=self.rows_of(i, *block),
            send_sem=self.send_sems.at[7 * i + k], recv_sem=self.recv_sems.at[7 * i + k],
            device_id=to, device_id_type=MESH)

    def _first(self, i):
        own = [self._copy(i, 0, self.me, self.sibling, src=self.src_of(i))]
        return own + [self._copy(i, 1 + j, self.me, (*chip, self.c), src=self.src_of(i))
                      for j, chip in enumerate(self.chips)]

    def _passed(self, i, j):
        return self._copy(i, 4 + j, (*self.chips[j], self.c), self.sibling)

    def start(self):
        for i in range(self.n):
            for cp in self._first(i):
                cp.start()

    def forward(self):
        for i in range(self.n):
            for j, chip in enumerate(self.chips):
                self._copy(i, 1 + j, (*chip, self.c), self.me).wait_recv()
                self._passed(i, j).start()

    def drain(self):
        for i in range(self.n):
            self._copy(i, 0, self.sibling, self.me).wait_recv()
            for j, chip in enumerate(self.chips):
                self._copy(i, 4 + j, (*chip, 1 - self.c), self.me).wait_recv()
        for i in range(self.n):
            for cp in self._first(i) + [self._passed(i, j) for j in range(3)]:
                cp.wait_send()


class _RelayGather:
    def __init__(self, n_arrays, rows_of, src_of, send_sems, recv_sems):
        x, y, c = _position()
        self.n, self.rows_of, self.src_of = n_arrays, rows_of, src_of
        self.send_sems, self.recv_sems = send_sems, recv_sems
        self.me, self.sibling = (x, y, c), (x, y, 1 - c)
        self.xn, self.yn, self.dg = (1 - x, y, c), (x, 1 - y, c), (1 - x, 1 - y, c)

    def _copy(self, i, k, block, to, half=None, src=None):
        rows = self.rows_of(i, *block, half)
        return pltpu.make_async_remote_copy(
            src_ref=rows if src is None else src, dst_ref=rows,
            send_sem=self.send_sems.at[8 * i + k], recv_sem=self.recv_sems.at[8 * i + k],
            device_id=to, device_id_type=MESH)

    def _sends(self, i):
        own = self.src_of(i)
        return [self._copy(i, 0, self.me, self.sibling, src=own), self._copy(i, 1, self.me, self.xn, src=own),
                self._copy(i, 2, self.me, self.yn, src=own),
                self._copy(i, 3, self.xn, self.yn, half=0), self._copy(i, 4, self.yn, self.xn, half=1),
                self._copy(i, 5, self.xn, self.sibling), self._copy(i, 6, self.yn, self.sibling),
                self._copy(i, 7, self.dg, self.sibling)]

    def start(self):
        for i in range(self.n):
            for cp in self._sends(i)[0:3]:
                cp.start()

    def forward(self):
        for i in range(self.n):
            self._copy(i, 1, self.xn, self.me).wait_recv()
            self._copy(i, 2, self.yn, self.me).wait_recv()
            for cp in self._sends(i)[3:7]:
                cp.start()

    def drain(self):
        x, y, c = self.me
        for i in range(self.n):
            self._copy(i, 3, self.dg, self.me, half=0).wait_recv()
            self._copy(i, 4, self.dg, self.me, half=1).wait_recv()
            self._sends(i)[7].start()
        for i in range(self.n):
            self._copy(i, 0, self.sibling, self.me).wait_recv()
            self._copy(i, 5, (1 - x, y, 1 - c), self.me).wait_recv()
            self._copy(i, 6, (x, 1 - y, 1 - c), self.me).wait_recv()
            self._copy(i, 7, (1 - x, 1 - y, 1 - c), self.me).wait_recv()
        for i in range(self.n):
            for cp in self._sends(i):
                cp.wait_send()


class _CarriedGather:
    def __init__(self, shards, padded_rows, zeros, forward_at, part=None, into=None):
        d = shards[0].shape[1]
        self.forward_at = forward_at
        self.n = len(shards)
        self.rows = [s.shape[0] for s in shards]
        self.pads = [p - N_DEV * r for r, p in zip(self.rows, padded_rows)]
        assert max(self.pads) <= zeros.shape[0] and zeros.shape[1] == d
        self.part = part if part is not None else (0, self.rows[0])
        assert (part is None and into is None) or self.n == 1
        assert self.part[0] % SUBLANE_BF16 == 0 and self.part[1] % SUBLANE_BF16 == 0
        self.arrays = list(shards) + [zeros] + ([into] if into is not None else [])
        self.out_shapes = [jax.ShapeDtypeStruct((p, d), s.dtype) for s, p in zip(shards, padded_rows)]
        self.aliases = {self.n + 1: 0} if into is not None else {}
        if into is not None:
            self.pads = [0] * self.n
        self.n_remote, self.n_local = 8 * self.n, 2 * self.n
        self.results = None

    def _rows_of(self, outs):
        def rows_of(i, px, py, pc, half):
            first, count = (self.part if self.n == 1 else (0, self.rows[i]))
            head = _round_up(count // 2, SUBLANE_BF16)
            if half == 0:
                count = head
            elif half == 1:
                first, count = first + head, count - head
            first = _block_of(px, py, pc) * self.rows[i] + first
            return outs[i].at[pl.ds(pl.multiple_of(first, SUBLANE_BF16), count), :]
        return rows_of

    def _own(self, ins, i):
        return ins[i].at[pl.ds(self.part[0], self.part[1]), :] if self.n == 1 else ins[i]

    def _gather(self, ins, outs, send_sems, recv_sems):
        return _RelayGather(self.n, self._rows_of(outs), functools.partial(self._own, ins), send_sems, recv_sems)

    def _local(self, ins, outs, local_sems):
        x, y, c = _position()
        rows_of = self._rows_of(outs)
        cps = []
        for i in range(self.n):
            cps.append(pltpu.make_async_copy(self._own(ins, i), rows_of(i, x, y, c, None), local_sems.at[2 * i]))
            if self.pads[i]:
                cps.append(pltpu.make_async_copy(ins[self.n].at[pl.ds(0, self.pads[i]), :],
                                                 outs[i].at[pl.ds(N_DEV * self.rows[i], self.pads[i]), :],
                                                 local_sems.at[2 * i + 1]))
        return cps

    def start(self, ins, outs, send_sems, recv_sems, local_sems):
        for cp in self._local(ins, outs, local_sems):
            cp.start()
        self._gather(ins, outs, send_sems, recv_sems).start()

    def forward(self, ins, outs, send_sems, recv_sems, local_sems):
        self._gather(ins, outs, send_sems, recv_sems).forward()

    def finish(self, ins, outs, send_sems, recv_sems, local_sems):
        self._gather(ins, outs, send_sems, recv_sems).drain()
        for cp in self._local(ins, outs, local_sems):
            cp.wait()


class _CarriedSwap:
    def __init__(self, grads, shard_rows):
        d = grads[0].shape[1]
        self.n, self.rows = len(grads), list(shard_rows)
        self.arrays = list(grads)
        self.out_shapes = [jax.ShapeDtypeStruct((4, s, d), g.dtype) for g, s in zip(grads, shard_rows)]
        self.aliases = {}
        self.n_remote, self.n_local = 4 * self.n, 0
        self.forward_at = 1.0
        self.results = None

    def _copies(self, ins, outs, send_sems, recv_sems):
        x, y, c = _position()
        cps = []
        for i in range(self.n):
            s = self.rows[i]
            for k in range(4):
                blk = _block_of(k >> 1, k & 1, 1 - c)
                cps.append(pltpu.make_async_remote_copy(
                    src_ref=ins[i].at[pl.ds(pl.multiple_of(blk * s, SUBLANE_BF16), s), :], dst_ref=outs[i].at[k],
                    send_sem=send_sems.at[4 * i + k], recv_sem=recv_sems.at[4 * i + k],
                    device_id=(x, y, 1 - c), device_id_type=MESH))
        return cps

    def start(self, ins, outs, send_sems, recv_sems, local_sems):
        for cp in self._copies(ins, outs, send_sems, recv_sems):
            cp.start()

    def forward(self, *_):
        pass

    def finish(self, ins, outs, send_sems, recv_sems, local_sems):
        for cp in self._copies(ins, outs, send_sems, recv_sems):
            cp.wait()


class _CarriedChipExchange:
    def __init__(self, presums, part=None, into=None):
        self.n = len(presums)
        assert (part is None and into is None) or self.n == 1
        self.part = part if part is not None else (0, presums[0].shape[1])
        assert self.part[0] % SUBLANE_BF16 == 0 and self.part[1] % SUBLANE_BF16 == 0
        self.arrays = list(presums) + ([into] if into is not None else [])
        self.out_shapes = [jax.ShapeDtypeStruct(p.shape, p.dtype) for p in presums]
        self.aliases = {self.n: 0} if into is not None else {}
        self.n_remote, self.n_local = 3 * self.n, 0
        self.forward_at = 1.0
        self.results = None

    def _copies(self, ins, outs, send_sems, recv_sems):
        x, y, c = _position()
        cps = []
        for i in range(self.n):
            rows = pl.ds(*self.part) if self.n == 1 else pl.ds(0, self.arrays[i].shape[1])
            for r in range(1, 4):
                cps.append(pltpu.make_async_remote_copy(
                    src_ref=ins[i].at[r - 1, rows, :], dst_ref=outs[i].at[r - 1, rows, :],
                    send_sem=send_sems.at[3 * i + r - 1], recv_sem=recv_sems.at[3 * i + r - 1],
                    device_id=(x ^ (r >> 1), y ^ (r & 1), c), device_id_type=MESH))
        return cps

    def start(self, ins, outs, send_sems, recv_sems, local_sems):
        for cp in self._copies(ins, outs, send_sems, recv_sems):
            cp.start()

    def forward(self, *_):
        pass

    def finish(self, ins, outs, send_sems, recv_sems, local_sems):
        for cp in self._copies(ins, outs, send_sems, recv_sems):
            cp.wait()


def _gather_small(block, reduce, name):
    rr, nn = block.shape

    def body(x_ref, out_ref, *rest):
        if reduce:
            stack_ref, send_sems, recv_sems, local_sem = rest
        else:
            send_sems, recv_sems, local_sem = rest
            stack_ref = out_ref
        x, y, c = _position()

        def rows_of(i, px, py, pc):
            return stack_ref.at[pl.ds(pl.multiple_of(_block_of(px, py, pc) * rr, 8), rr), :]

        own = pltpu.make_async_copy(x_ref, rows_of(0, x, y, c), local_sem)
        own.start()
        gather = _TwoLevelGather(1, rows_of, lambda i: x_ref, send_sems, recv_sems)
        gather.start()
        gather.forward()
        gather.drain()
        own.wait()
        if reduce:
            acc = stack_ref[0:rr, :]
            for k in range(1, N_DEV):
                acc = acc + stack_ref[k * rr:(k + 1) * rr, :]
            out_ref[...] = acc

    vmem = pl.BlockSpec(memory_space=pltpu.VMEM)
    scratch = [pltpu.SemaphoreType.DMA((7,)), pltpu.SemaphoreType.DMA((7,)), pltpu.SemaphoreType.DMA]
    if reduce:
        scratch = [pltpu.VMEM((N_DEV * rr, nn), F32)] + scratch
    out_rows = rr if reduce else N_DEV * rr
    return pl.pallas_call(
        body, in_specs=[vmem], out_specs=vmem, out_shape=jax.ShapeDtypeStruct((out_rows, nn), F32),
        scratch_shapes=scratch, name=name, compiler_params=_params())(block)


def _sum_stack(stack, name):
    rr = stack.shape[0] // N_DEV

    def body(s_ref, o_ref):
        acc = s_ref[0:rr, :]
        for k in range(1, N_DEV):
            acc = acc + s_ref[k * rr:(k + 1) * rr, :]
        o_ref[...] = acc

    vmem = pl.BlockSpec(memory_space=pltpu.VMEM)
    return pl.pallas_call(body, in_specs=[vmem], out_specs=vmem,
                          out_shape=jax.ShapeDtypeStruct((rr, stack.shape[1]), F32), name=name,
                          compiler_params=_params())(stack)


def _presum(where, grad, swapped, name):
    s, d = swapped.shape[1], swapped.shape[2]
    tc = _tile(d, 512, LANE)

    def body(where_ref, g_ref, sw_ref, o_ref):
        o_ref[0] = (g_ref[...].astype(F32) + sw_ref[0].astype(F32)).astype(o_ref.dtype)

    return _call(
        body, grid=(3, d // tc),
        in_specs=[pl.BlockSpec((s, tc), lambda r, j, where: (where[1 + r], j)),
                  pl.BlockSpec((1, s, tc), lambda r, j, where: (where[5 + r], 0, j))],
        out_specs=pl.BlockSpec((1, s, tc), lambda r, j, where: (r, 0, j)),
        out_shape=jax.ShapeDtypeStruct((3, s, d), WIRE_DTYPE), name=name, args=(grad, swapped), prefetch=(where,))


def _final_sum(where, grad, swapped, received, name, carried=()):
    s, d = swapped.shape[1], swapped.shape[2]
    tc = _tile(d, 512, LANE)

    def body(where_ref, g_ref, sw_ref, r_ref, o_ref):
        acc = g_ref[...].astype(F32) + sw_ref[0].astype(F32)
        for k in range(3):
            acc = acc + r_ref[k].astype(F32)
        o_ref[...] = acc

    return _call(
        body, grid=(d // tc,),
        in_specs=[pl.BlockSpec((s, tc), lambda j, where: (where[0], j)),
                  pl.BlockSpec((1, s, tc), lambda j, where: (where[4], 0, j)),
                  pl.BlockSpec((3, s, tc), lambda j, where: (0, 0, j))],
        out_specs=pl.BlockSpec((s, tc), lambda j, where: (0, j)),
        out_shape=jax.ShapeDtypeStruct((s, d), F32), name=name, args=(grad, swapped, received),
        prefetch=(where,), carried=carried)


class _GradReduction:
    def __init__(self, key, grad, shard_rows, where):
        self.key, self.grad, self.rows, self.where = key, grad, shard_rows, where
        self._presum = self._exchange = None

    def swap(self):
        self._swap = _CarriedSwap([self.grad], [self.rows])
        return self._swap

    def exchange(self, part=None):
        if self._presum is None:
            self._presum = _presum(self.where, self.grad, self._swap.results[0], "presum_" + self.key)
        rows = None
        if part is not None:
            half = _round_up(self.rows // 2, SUBLANE_BF16)
            rows = (0, half) if part == 0 else (half, self.rows - half)
        into = self._exchange.results[0] if part == 1 else None
        self._exchange = _CarriedChipExchange([self._presum], rows, into)
        return self._exchange

    def total(self, carried=()):
        return _final_sum(self.where, self.grad, self._swap.results[0], self._exchange.results[0],
                          "sum_" + self.key, carried)

    def total_and_update(self, w, m, v):
        return _sum_adamw(self.where, self.grad, self._swap.results[0], self._exchange.results[0], w, m, v,
                          "update_" + self.key)


def _adamw_math(w, g, m, v):
    nm = ADAM_B1 * m + (1.0 - ADAM_B1) * g
    nv = ADAM_B2 * v + (1.0 - ADAM_B2) * (g * g)
    m_hat = nm / (1.0 - ADAM_B1 ** ADAM_STEP)
    v_hat = nv / (1.0 - ADAM_B2 ** ADAM_STEP)
    return -ADAM_LR * (m_hat / (jnp.sqrt(v_hat) + ADAM_EPS) + ADAM_WD * w), nm, nv


def _sum_adamw(where, grad, swapped, received, w, m, v, name):
    s, d = swapped.shape[1], swapped.shape[2]
    tc = _tile(d, 512, LANE)

    def body(where_ref, g_ref, sw_ref, r_ref, w_ref, m_ref, v_ref, gs_ref, d_ref, nm_ref, nv_ref):
        g = g_ref[...].astype(F32) + sw_ref[0].astype(F32)
        for k in range(3):
            g = g + r_ref[k].astype(F32)
        gs_ref[...] = g
        d_ref[...], nm_ref[...], nv_ref[...] = _adamw_math(w_ref[...], g, m_ref[...], v_ref[...])

    blk = pl.BlockSpec((s, tc), lambda j, where: (0, j))
    return _call(
        body, grid=(d // tc,),
        in_specs=[pl.BlockSpec((s, tc), lambda j, where: (where[0], j)),
                  pl.BlockSpec((1, s, tc), lambda j, where: (where[4], 0, j)),
                  pl.BlockSpec((3, s, tc), lambda j, where: (0, 0, j)), blk, blk, blk],
        out_specs=[blk] * 4, out_shape=[jax.ShapeDtypeStruct((s, d), F32)] * 4, name=name,
        args=(grad, swapped, received, w, m, v), prefetch=(where,))


def _adamw(w, g, m, v, name):
    rows, cols = w.shape
    tr = _tile(rows, 256, 8)

    def body(w_ref, g_ref, m_ref, v_ref, d_ref, nm_ref, nv_ref):
        d_ref[...], nm_ref[...], nv_ref[...] = _adamw_math(w_ref[...], g_ref[...], m_ref[...], v_ref[...])

    spec = pl.BlockSpec((tr, cols), lambda i: (i, 0))
    return pl.pallas_call(
        body, grid=(rows // tr,), in_specs=[spec] * 4, out_specs=[spec] * 3,
        out_shape=[jax.ShapeDtypeStruct((rows, cols), F32)] * 3, name=name, compiler_params=_params())(w, g, m, v)


def _pack_rows(arrays, width, row_quantum=8):
    flat = jnp.concatenate([a.reshape(-1) for a in arrays])
    total = _round_up(flat.shape[0], row_quantum * width)
    flat = jnp.pad(flat, (0, total - flat.shape[0]))
    return flat.reshape(-1, width)


def _unpack_rows(packed, shapes):
    flat = packed.reshape(-1)
    out = []
    off = 0
    for shp in shapes:
        size = 1
        for s in shp:
            size *= s
        out.append(flat[off:off + size].reshape(shp))
        off += size
    return out


def _block_diag(w):
    h, hb, _ = w.shape
    per = BD // hb
    w4 = w.reshape(h // per, per, hb, hb)
    eye = jnp.eye(per, dtype=w.dtype)
    return jnp.einsum('npij,pq->npiqj', w4, eye).reshape(h // per, BD, BD)


def _block_diag_extract(bd, hb):
    nbk = bd.shape[0]
    per = BD // hb
    b5 = bd.reshape(nbk, per, hb, per, hb)
    eye = jnp.eye(per, dtype=bd.dtype)
    return jnp.einsum('npiqj,pq->npij', b5, eye).reshape(nbk * per, hb, hb)


def kernel(x, meta_tokens, ffn1_pre_g, ffn1_w_gate, ffn1_w_up, ffn1_w_down, ffn1_post_g, mix_pre_g, w_in, lru_conv_w, lru_conv_b, lru_w_a, lru_b_a, lru_w_x, lru_b_x, lru_lambda, sconv_w, lru_out_g, sconv_out_g, w_out, mix_post_g, ffn2_pre_g, ffn2_w_gate, ffn2_w_up, ffn2_w_down, ffn2_post_g, loss_target, m_meta_tokens, m_ffn1_pre_g, m_ffn1_w_gate, m_ffn1_w_up, m_ffn1_w_down, m_ffn1_post_g, m_mix_pre_g, m_w_in, m_lru_conv_w, m_lru_conv_b, m_lru_w_a, m_lru_b_a, m_lru_w_x, m_lru_b_x, m_lru_lambda, m_sconv_w, m_lru_out_g, m_sconv_out_g, m_w_out, m_mix_post_g, m_ffn2_pre_g, m_ffn2_w_gate, m_ffn2_w_up, m_ffn2_w_down, m_ffn2_post_g, v_meta_tokens, v_ffn1_pre_g, v_ffn1_w_gate, v_ffn1_w_up, v_ffn1_w_down, v_ffn1_post_g, v_mix_pre_g, v_w_in, v_lru_conv_w, v_lru_conv_b, v_lru_w_a, v_lru_b_a, v_lru_w_x, v_lru_b_x, v_lru_lambda, v_sconv_w, v_lru_out_g, v_sconv_out_g, v_w_out, v_mix_post_g, v_ffn2_pre_g, v_ffn2_w_gate, v_ffn2_w_up, v_ffn2_w_down, v_ffn2_post_g):
    given = dict(locals())
    wts = {n: given[n] for n in WEIGHT_NAMES}
    mom = {n: given["m_" + n] for n in WEIGHT_NAMES}
    var = {n: given["v_" + n] for n in WEIGHT_NAMES}

    xi, yi, ci = _position()
    me = _block_of(xi, yi, ci)
    x2 = x[0]
    seq, d = x2.shape
    n_meta = meta_tokens.shape[0]
    m_rows = _round_up(n_meta + seq, ROW_ALIGN)
    pad = m_rows - n_meta - seq
    lead = pad + n_meta
    c = lru_conv_b.shape[1]
    hb = lru_w_a.shape[-1]
    dm = meta_tokens.shape[1]
    cs_ = lru_conv_w.shape[2]
    kw4, kw3 = lru_conv_w.shape[1], sconv_w.shape[1]
    assert d == 2 * c and c % BD == 0 and BD % hb == 0 and cs_ <= dm and kw4 == 4 and kw3 == 3

    small = jnp.zeros((_round_up(n_meta + kw4 + kw3, 8), dm), F32)
    small = small.at[0:n_meta].set(meta_tokens)
    small = small.at[n_meta:n_meta + kw4, 0:cs_].set(lru_conv_w[0])
    small = small.at[n_meta + kw4:n_meta + kw4 + kw3, 0:cs_].set(sconv_w[0])
    sr = small.shape[0]
    small_all = _gather_small(small, False, "gather_small").reshape(N_DEV, sr, dm)
    meta_full = small_all[:, 0:n_meta, :].transpose(1, 0, 2).reshape(n_meta, d)
    conv_w_full = small_all[:, n_meta:n_meta + kw4, 0:cs_].transpose(1, 0, 2).reshape(kw4, c)
    sconv_w_full = small_all[:, n_meta + kw4:n_meta + kw4 + kw3, 0:cs_].transpose(1, 0, 2).reshape(kw3, c)

    big = ['ffn1_w_gate', 'ffn1_w_up', 'ffn1_w_down', 'w_in', 'w_out', 'ffn2_w_gate', 'ffn2_w_up', 'ffn2_w_down']
    col_sharded = {'ffn1_w_gate', 'ffn1_w_up', 'w_in', 'ffn2_w_gate', 'ffn2_w_up'}
    shards = []
    for nme in big:
        w = wts[nme][0].astype(WIRE_DTYPE)
        shards.append(w.T if nme in col_sharded else w)
    shard_rows = dict(zip(big, [s.shape[0] for s in shards]))
    zeros = jnp.zeros((F_ALIGN, d), WIRE_DTYPE)

    def gather(forward_at, *names, part=None, into=None):
        sel = [shards[big.index(nme)] for nme in names]
        padded = [_round_up(N_DEV * shard_rows[nme], LANE if nme in ('w_in', 'w_out') else F_ALIGN) for nme in names]
        return _CarriedGather(sel, padded, zeros, forward_at, part, into)

    pv = jnp.zeros((16, c), F32)
    pv = pv.at[0:4].set(conv_w_full).at[4].set(lru_conv_b[0]).at[5].set(lru_b_a[0]).at[6].set(lru_b_x[0])
    pv = pv.at[7].set(lru_lambda[0]).at[8:11].set(sconv_w_full).at[11].set(lru_out_g[0]).at[12].set(sconv_out_g[0])
    wa_bd = _block_diag(lru_w_a[0]).astype(MXU_DTYPE)
    wx_bd = _block_diag(lru_w_x[0]).astype(MXU_DTYPE)
    gs = c // N_GROUPS
    gidx = jnp.arange(BD) // gs
    gm = jnp.where(gidx[:, None] == gidx[None, :], 1.0 / gs, 0.0).astype(MXU_DTYPE)

    ride = gather(0.3, 'ffn1_w_gate')
    h0, n1, target = _embed(x2, meta_full, loss_target[0], ffn1_pre_g, pad, "embed_prenorm", carried=[ride])
    (wg1,) = ride.results
    ride = gather(0.6, 'ffn1_w_up')
    g1 = _mm_nt(n1, wg1, "ffn1_gate", carried=[ride], out_dtype=MXU_DTYPE)
    (wu1,) = ride.results
    ride = gather(0.6, 'ffn1_w_down')
    u1, a1 = _ffn_up_act(n1, wu1, g1, "ffn1_up_act", carried=[ride])
    (wd1,) = ride.results
    ride = gather(0.75, 'w_in', 'w_out')
    fo1, h1, un = _mm_residual_norm(a1, wd1, h0, ffn1_post_g, 0.5, mix_pre_g, "ffn1_down", carried=[ride])
    win_t, wout = ride.results
    s2 = shard_rows['ffn2_w_gate']
    quarter = _round_up(s2 // 4, SUBLANE_BF16)
    ride_g = gather(0.5, 'ffn2_w_gate', part=(0, 3 * quarter))
    z = _mm_nt(un, win_t, "mix_in_proj", carried=[ride_g])
    ride_g = gather(0.5, 'ffn2_w_gate', part=(3 * quarter, s2 - 3 * quarter), into=ride_g.results[0])
    ride_u = gather(0.5, 'ffn2_w_up', part=(0, quarter))
    mixed, hs = _mixer_fwd(z, pv, wa_bd, wx_bd, gm, pad, "mixer_fwd", carried=[ride_g, ride_u])
    (wg2,) = ride_g.results
    ride_u = gather(0.5, 'ffn2_w_up', part=(quarter, s2 - quarter), into=ride_u.results[0])
    o_mix, h2, n2 = _mm_residual_norm(mixed, wout, h1, mix_post_g, 1.0, ffn2_pre_g, "mix_out_proj", carried=[ride_u])
    (wu2,) = ride_u.results
    ride = gather(0.75, 'ffn2_w_down')
    g2, u2, a2 = _ffn_gate_up(n2, wg2, wu2, "ffn2_gate_up", carried=[ride])
    (wd2,) = ride.results
    dh3, dfo2, d_post2, loss_part = _mm_residual_loss(a2, wd2, h2, ffn2_post_g, 0.5, target, lead, "ffn2_down_loss")
    loss = lax.psum(loss_part[0, 0], ("x", "y", "c"))

    chip_rel = [2 * (xi ^ (r >> 1)) + (yi ^ (r & 1)) for r in range(4)]
    where = jnp.stack([2 * k + ci for k in chip_rel] + chip_rel).astype(jnp.int32)
    red = {}

    def reduction(nme, grad):
        red[nme] = _GradReduction(nme, grad, shard_rows[nme], where)
        return red[nme]

    r_wd2 = reduction('ffn2_w_down', _mm_tn(a2, dfo2, "ffn2_dw_down"))
    dg2, du2 = _ffn_hidden_bwd(dfo2, wd2, g2, u2, "ffn2_hidden_bwd", carried=[r_wd2.swap()])
    r_wg2 = reduction('ffn2_w_gate', _mm_tn(dg2, n2, "ffn2_dw_gate", carried=[r_wd2.exchange(part=0)]))
    r_wu2 = reduction('ffn2_w_up', _mm_tn(du2, n2, "ffn2_dw_up", carried=[r_wd2.exchange(part=1), r_wg2.swap()]))
    dh2, d_pre2 = _mm_norm_bwd([(dg2, wg2), (du2, wu2)], h2, ffn2_pre_g, dh3, "ffn2_dx",
                               carried=[r_wg2.exchange(), r_wu2.swap()])
    do_mix, d_mix_post, dmixed = _norm_bwd_mm_nt(o_mix, mix_post_g, dh2, 1.0, wout, "mix_out_proj_bwd")
    r_wout = reduction('w_out', _mm_tn(mixed, do_mix, "mix_dw_out"))
    dz, dpv, dwa_bd, dwx_bd = _mixer_bwd(z, hs, dmixed, pv, wa_bd, wx_bd, gm, pad, "mixer_bwd",
                                         carried=[r_wu2.exchange(), r_wout.swap()])
    r_win = reduction('w_in', _mm_tn(dz, un, "mix_dw_in", carried=[r_wout.exchange()]))
    dh1, d_mix_pre, dfo1, d_post1 = _mm_norm_bwd([(dz, win_t)], h1, mix_pre_g, dh2, "mix_dx", carried=[r_win.swap()],
                                                 post=(fo1, ffn1_post_g, 0.5))
    r_wd1 = reduction('ffn1_w_down', _mm_tn(a1, dfo1, "ffn1_dw_down", carried=[r_win.exchange(part=0)]))
    early_names = ['mix_pre_g', 'mix_post_g', 'ffn2_pre_g', 'ffn2_post_g', 'ffn1_post_g',
                   'lru_conv_b', 'lru_b_a', 'lru_b_x', 'lru_lambda', 'lru_out_g', 'sconv_out_g',
                   'lru_conv_w', 'sconv_w', 'lru_w_a', 'lru_w_x']
    early_parts = [d_mix_pre, d_mix_post, d_pre2, d_post2, d_post1,
                   dpv[4:5], dpv[5:6], dpv[6:7], dpv[7:8], dpv[11:12], dpv[12:13],
                   dpv[0:4], dpv[8:11], _block_diag_extract(dwa_bd, hb), _block_diag_extract(dwx_bd, hb)]
    early_packed = _pack_rows(early_parts, d, SUBLANE_BF16)
    early_ride = _CarriedGather([early_packed], [N_DEV * early_packed.shape[0]], zeros, 0.75)
    dg1, du1 = _ffn_hidden_bwd(dfo1, wd1, g1, u1, "ffn1_hidden_bwd",
                               carried=[r_win.exchange(part=1), r_wd1.swap(), early_ride])
    early_sum = _sum_stack(early_ride.results[0], "sum_small_early")
    r_wg1 = reduction('ffn1_w_gate', _mm_tn(dg1, n1, "ffn1_dw_gate", carried=[r_wd1.exchange(part=0)]))
    r_wu1 = reduction('ffn1_w_up', _mm_tn(du1, n1, "ffn1_dw_up", carried=[r_wd1.exchange(part=1), r_wg1.swap()]))
    row_tile = _norm_bwd_row_tile(m_rows)
    n_tiles = m_rows // row_tile
    half = n_tiles // 2
    assert half >= 1 and half * row_tile >= lead
    dh0_a, d_pre1_a = _mm_norm_bwd([(dg1, wg1), (du1, wu1)], h0, ffn1_pre_g, dh1, "ffn1_dx_a",
                                   carried=[r_wg1.exchange(), r_wu1.swap()], row_tiles=(0, half))
    dh0_b, d_pre1 = _mm_norm_bwd([(dg1, wg1), (du1, wu1)], h0, ffn1_pre_g, dh1, "ffn1_dx_b",
                                 carried=[r_wu1.exchange()], row_tiles=(half, n_tiles - half), dg_init=d_pre1_a)
    grad_x = jnp.concatenate([dh0_a[lead:], dh0_b], axis=0)[None]
    d_meta = dh0_a[pad:lead]

    grads, delta, new_m, new_v = {}, {}, {}, {}
    for nme in big:
        in_shard_layout = nme not in col_sharded or shard_rows[nme] % LANE != 0
        if in_shard_layout:
            view = (lambda t: t[0].T) if nme in col_sharded else (lambda t: t[0])
            back = (lambda t: t.T[None]) if nme in col_sharded else (lambda t: t[None])
            outs = red[nme].total_and_update(view(wts[nme]), view(mom[nme]), view(var[nme]))
            grads[nme], delta[nme], new_m[nme], new_v[nme] = [back(t) for t in outs]
        else:
            grads[nme] = red[nme].total().T[None]
            outs = _adamw(wts[nme][0], grads[nme][0], mom[nme][0], var[nme][0], "adamw_" + nme)
            delta[nme], new_m[nme], new_v[nme] = [t[None] for t in outs]

    late_names = ['ffn1_pre_g', 'meta_tokens']
    late_parts = [d_pre1, d_meta]
    late_sum = _gather_small(_pack_rows(late_parts, d), True, "reduce_small_late")
    small_sums = (_unpack_rows(early_sum, [p.shape for p in early_parts])
                  + _unpack_rows(late_sum, [p.shape for p in late_parts]))
    for nme, gsm in zip(early_names + late_names, small_sums):
        if nme == 'meta_tokens':
            grads[nme] = lax.dynamic_slice_in_dim(gsm, me * dm, dm, axis=1)
        elif nme in ('lru_conv_w', 'sconv_w'):
            grads[nme] = lax.dynamic_slice_in_dim(gsm, me * cs_, cs_, axis=1)[None]
        else:
            grads[nme] = gsm.reshape(wts[nme].shape)

    rest = [n for n in WEIGHT_NAMES if n not in big]
    rest_shapes = [wts[n].shape for n in rest]
    packed = [_pack_rows([src[n] for n in rest], LANE, 256) for src in (wts, grads, mom, var)]
    for out, packed_out in zip((delta, new_m, new_v), _adamw(*packed, "adamw_small")):
        for nme, arr in zip(rest, _unpack_rows(packed_out, rest_shapes)):
            out[nme] = arr

    return (loss, grad_x, *[grads[n] for n in WEIGHT_NAMES], *[delta[n] for n in WEIGHT_NAMES],
            *[new_m[n] for n in WEIGHT_NAMES], *[new_v[n] for n in WEIGHT_NAMES])
```

```python
import functools

import jax
import jax.numpy as jnp
from jax import lax
from jax.experimental import pallas as pl
from jax.experimental.pallas import tpu as pltpu

F32 = jnp.float32
MXU_DTYPE = jnp.bfloat16
WIRE_DTYPE = jnp.bfloat16
MESH = pl.DeviceIdType.MESH

EPS = 1e-6
LRU_C = 8.0
N_GROUPS = 16
ADAM_LR = 0.001
ADAM_B1 = 0.9
ADAM_B2 = 0.999
ADAM_EPS = 1e-08
ADAM_WD = 0.01
ADAM_STEP = 10

N_DEV = 8
LANE = 128
SUBLANE_BF16 = 16
ROW_ALIGN = 128
F_ALIGN = 512
BD = 256
K_TILE = 512
ACC_ROWS = 528
ACC_GROUP = 1
MIX_ROWS = 128
VMEM_LIMIT_MB = 56

WEIGHT_NAMES = ['meta_tokens', 'ffn1_pre_g', 'ffn1_w_gate', 'ffn1_w_up', 'ffn1_w_down', 'ffn1_post_g',
                'mix_pre_g', 'w_in', 'lru_conv_w', 'lru_conv_b', 'lru_w_a', 'lru_b_a', 'lru_w_x', 'lru_b_x',
                'lru_lambda', 'sconv_w', 'lru_out_g', 'sconv_out_g', 'w_out', 'mix_post_g', 'ffn2_pre_g',
                'ffn2_w_gate', 'ffn2_w_up', 'ffn2_w_down', 'ffn2_post_g']


def _round_up(n, q):
    return (n + q - 1) // q * q


def _tile(n, target, q):
    best = None
    t = q
    while t <= min(n, target):
        if n % t == 0:
            best = t
        t += q
    assert best is not None, (n, target, q)
    return best


def _params(**kw):
    return pltpu.CompilerParams(vmem_limit_bytes=VMEM_LIMIT_MB << 20, **kw)


def _call(body, *, grid, in_specs, out_specs, out_shape, name, args, scratch_shapes=(), carried=(), prefetch=()):
    carried = list(carried)
    n_pf = len(prefetch)

    def launch(fn, in_specs_, out_specs_, out_shape_, scratch_, operands, aliases_):
        if n_pf:
            spec = pltpu.PrefetchScalarGridSpec(num_scalar_prefetch=n_pf, grid=grid, in_specs=in_specs_,
                                                out_specs=out_specs_, scratch_shapes=scratch_)
            return pl.pallas_call(fn, grid_spec=spec, out_shape=out_shape_, input_output_aliases=aliases_,
                                  name=name, compiler_params=_params())(*prefetch, *operands)
        return pl.pallas_call(fn, grid=grid, in_specs=in_specs_, out_specs=out_specs_, out_shape=out_shape_,
                              scratch_shapes=scratch_, input_output_aliases=aliases_, name=name,
                              compiler_params=_params())(*operands)

    if not carried:
        return launch(body, in_specs, out_specs, out_shape, list(scratch_shapes), args, {})
    single = not isinstance(out_shape, (list, tuple))
    out_specs_l = [out_specs] if single else list(out_specs)
    out_shape_l = [out_shape] if single else list(out_shape)
    n_in, n_out, n_scr = len(in_specs), len(out_specs_l), len(scratch_shapes)
    hbm = pl.BlockSpec(memory_space=pl.ANY)
    c_in = [a for cm in carried for a in cm.arrays]
    c_out = [s for cm in carried for s in cm.out_shapes]
    c_scr = []
    aliases = {}
    in_off, out_off = n_pf + n_in, n_out
    for cm in carried:
        c_scr += [pltpu.SemaphoreType.DMA((cm.n_remote,)), pltpu.SemaphoreType.DMA((cm.n_remote,)),
                  pltpu.SemaphoreType.DMA((max(cm.n_local, 1),))]
        for k, v in cm.aliases.items():
            aliases[in_off + k] = out_off + v
        in_off += len(cm.arrays)
        out_off += len(cm.out_shapes)
    steps = 1
    for g in grid:
        steps *= g
    forward_steps = [min(int(cm.forward_at * steps), steps - 1) for cm in carried]

    def wrapped(*refs):
        pf = refs[:n_pf]
        p = n_pf
        ins = refs[p:p + n_in]
        p += n_in
        cins = refs[p:p + len(c_in)]
        p += len(c_in)
        outs = refs[p:p + n_out]
        p += n_out
        couts = refs[p:p + len(c_out)]
        p += len(c_out)
        scr = refs[p:p + n_scr]
        csem = refs[p + n_scr:]
        lin = 0
        for axis, g in enumerate(grid):
            lin = lin * g + pl.program_id(axis)
        views = []
        io = oo = 0
        for j, cm in enumerate(carried):
            views.append((cins[io:io + len(cm.arrays)], couts[oo:oo + len(cm.out_shapes)],
                          csem[3 * j], csem[3 * j + 1], csem[3 * j + 2]))
            io += len(cm.arrays)
            oo += len(cm.out_shapes)

        @pl.when(lin == 0)
        def _():
            for cm, v in zip(carried, views):
                cm.start(*v)

        body(*pf, *ins, *outs, *scr)

        for cm, v, step in zip(carried, views, forward_steps):
            pl.when(lin == step)(functools.partial(cm.forward, *v))

        @pl.when(lin == steps - 1)
        def _():
            for cm, v in zip(carried, views):
                cm.finish(*v)

    res = launch(wrapped, list(in_specs) + [hbm] * len(c_in), out_specs_l + [hbm] * len(c_out),
                 out_shape_l + c_out, list(scratch_shapes) + c_scr, (*args, *c_in), aliases)
    oo = n_out
    for cm in carried:
        cm.results = list(res[oo:oo + len(cm.out_shapes)])
        oo += len(cm.out_shapes)
    return res[0] if single else list(res[:n_out])


def _embed(x, meta, target, g, pad, name, carried=()):
    seq, d = x.shape
    n_meta = meta.shape[0]
    lead = pad + n_meta
    m = lead + seq
    tr = ROW_ALIGN
    lead_blocks = lead // tr
    meta_row = pad - (lead_blocks - 1) * tr
    assert lead % tr == 0 and seq % tr == 0 and 0 <= meta_row and meta_row % 8 == 0

    def body(x_ref, meta_ref, t_ref, g_ref, h_ref, n_ref, tp_ref):
        i = pl.program_id(0)

        @pl.when(i < lead_blocks)
        def _():
            h_ref[...] = jnp.zeros_like(h_ref)
            tp_ref[...] = jnp.zeros_like(tp_ref)

        @pl.when(i == lead_blocks - 1)
        def _():
            h_ref[pl.ds(meta_row, n_meta), :] = meta_ref[...]

        @pl.when(i >= lead_blocks)
        def _():
            h_ref[...] = x_ref[...]
            tp_ref[...] = t_ref[...]

        h = h_ref[...]
        r = lax.rsqrt(jnp.mean(h * h, axis=-1, keepdims=True) + EPS)
        n_ref[...] = (h * r * g_ref[...]).astype(n_ref.dtype)

    tokens = pl.BlockSpec((tr, d), lambda i: (jnp.maximum(i - lead_blocks, 0), 0))
    rows = pl.BlockSpec((tr, d), lambda i: (i, 0))
    return _call(
        body, grid=(m // tr,),
        in_specs=[tokens, pl.BlockSpec((n_meta, d), lambda i: (0, 0)), tokens, pl.BlockSpec((1, d), lambda i: (0, 0))],
        out_specs=[rows, rows, rows],
        out_shape=[jax.ShapeDtypeStruct((m, d), F32), jax.ShapeDtypeStruct((m, d), MXU_DTYPE),
                   jax.ShapeDtypeStruct((m, d), F32)],
        name=name, args=(x, meta, target, g), carried=carried)


def _rmsnorm_bwd_rows(x, g, dy):
    r = lax.rsqrt(jnp.mean(x * x, axis=-1, keepdims=True) + EPS)
    xh = x * r
    dyh = dy * g
    dx = r * (dyh - xh * jnp.mean(dyh * xh, axis=-1, keepdims=True))
    return dx, dy * xh


def _dot_nt(a, b):
    return lax.dot_general(a, b, (((1,), (1,)), ((), ())), preferred_element_type=F32)


def _dot_tn(a, b):
    return lax.dot_general(a, b, (((0,), (0,)), ((), ())), preferred_element_type=F32)


def _mm_nt(a, w, name, carried=(), out_dtype=F32):
    m, k = a.shape
    n = w.shape[0]
    tm = _tile(m, 1056, SUBLANE_BF16)
    tn = _tile(n, 512, LANE)

    def body(a_ref, w_ref, o_ref):
        o_ref[...] = _dot_nt(a_ref[...], w_ref[...]).astype(o_ref.dtype)

    return _call(
        body, grid=(m // tm, n // tn),
        in_specs=[pl.BlockSpec((tm, k), lambda i, j: (i, 0)), pl.BlockSpec((tn, k), lambda i, j: (j, 0))],
        out_specs=pl.BlockSpec((tm, tn), lambda i, j: (i, j)),
        out_shape=jax.ShapeDtypeStruct((m, n), out_dtype), name=name, args=(a, w), carried=carried)


def _norm_bwd_mm_nt(x, g, dy, scale, w, name, carried=()):
    m, d = x.shape
    n = w.shape[0]
    tm = _tile(m, 528, SUBLANE_BF16)

    def body(x_ref, g_ref, dy_ref, w_ref, dx_ref, dg_ref, o_ref):
        @pl.when(pl.program_id(0) == 0)
        def _():
            dg_ref[...] = jnp.zeros_like(dg_ref)

        dx, dgr = _rmsnorm_bwd_rows(x_ref[...], g_ref[...], scale * dy_ref[...])
        dxb = dx.astype(dx_ref.dtype)
        dx_ref[...] = dxb
        dg_ref[...] += jnp.sum(dgr, axis=0, keepdims=True)
        o_ref[...] = _dot_nt(dxb, w_ref[...])

    row = pl.BlockSpec((tm, d), lambda i: (i, 0))
    vec = pl.BlockSpec((1, d), lambda i: (0, 0))
    return _call(
        body, grid=(m // tm,),
        in_specs=[row, vec, row, pl.BlockSpec((n, d), lambda i: (0, 0), pipeline_mode=pl.Buffered(1))],
        out_specs=[row, vec, pl.BlockSpec((tm, n), lambda i: (i, 0))],
        out_shape=[jax.ShapeDtypeStruct((m, d), MXU_DTYPE), jax.ShapeDtypeStruct((1, d), F32),
                   jax.ShapeDtypeStruct((m, n), F32)],
        name=name, args=(x, g, dy, w), carried=carried)


def _ffn_up_act(n_act, wu_t, g_act, name, carried=()):
    m, d = n_act.shape
    fp = wu_t.shape[0]
    tm = _tile(m, 1056, SUBLANE_BF16)
    tn = _tile(fp, 512, LANE)

    def body(n_ref, wu_ref, g_ref, u_ref, a_ref):
        u = _dot_nt(n_ref[...], wu_ref[...])
        g = g_ref[...].astype(F32)
        u_ref[...] = u.astype(u_ref.dtype)
        a_ref[...] = (g * jax.nn.sigmoid(g) * u).astype(a_ref.dtype)

    act = pl.BlockSpec((tm, tn), lambda i, j: (i, j))
    return _call(
        body, grid=(m // tm, fp // tn),
        in_specs=[pl.BlockSpec((tm, d), lambda i, j: (i, 0)), pl.BlockSpec((tn, d), lambda i, j: (j, 0)), act],
        out_specs=[act, act],
        out_shape=[jax.ShapeDtypeStruct((m, fp), MXU_DTYPE)] * 2, name=name, args=(n_act, wu_t, g_act), carried=carried)


def _ffn_gate_up(n_act, wg_t, wu_t, name, carried=()):
    m, d = n_act.shape
    fp = wg_t.shape[0]
    tm = _tile(m, 1056, SUBLANE_BF16)
    tn = _tile(fp, 512, LANE)

    def body(n_ref, wg_ref, wu_ref, g_ref, u_ref, a_ref):
        n = n_ref[...]
        g = _dot_nt(n, wg_ref[...])
        u = _dot_nt(n, wu_ref[...])
        g_ref[...] = g.astype(g_ref.dtype)
        u_ref[...] = u.astype(u_ref.dtype)
        a_ref[...] = (g * jax.nn.sigmoid(g) * u).astype(a_ref.dtype)

    act = pl.BlockSpec((tm, tn), lambda i, j: (i, j))
    wsp = pl.BlockSpec((tn, d), lambda i, j: (j, 0))
    return _call(
        body, grid=(m // tm, fp // tn),
        in_specs=[pl.BlockSpec((tm, d), lambda i, j: (i, 0)), wsp, wsp],
        out_specs=[act, act, act],
        out_shape=[jax.ShapeDtypeStruct((m, fp), MXU_DTYPE)] * 3, name=name, args=(n_act, wg_t, wu_t), carried=carried)


def _ffn_hidden_bwd(dfo, wd, g_act, u_act, name, carried=()):
    m, d = dfo.shape
    fp = wd.shape[0]
    tm = _tile(m, 1056, SUBLANE_BF16)
    tn = _tile(fp, 512, LANE)

    def body(df_ref, wd_ref, g_ref, u_ref, dg_ref, du_ref):
        da = _dot_nt(df_ref[...], wd_ref[...]).astype(dg_ref.dtype)
        g = g_ref[...]
        u = u_ref[...]
        s = jax.nn.sigmoid(g)
        du_ref[...] = da * (g * s)
        dg_ref[...] = da * (u * (s * (1.0 + g * (1.0 - s))))

    act = pl.BlockSpec((tm, tn), lambda i, j: (i, j))
    return _call(
        body, grid=(m // tm, fp // tn),
        in_specs=[pl.BlockSpec((tm, d), lambda i, j: (i, 0)), pl.BlockSpec((tn, d), lambda i, j: (j, 0)), act, act],
        out_specs=[act, act],
        out_shape=[jax.ShapeDtypeStruct((m, fp), MXU_DTYPE)] * 2, name=name, args=(dfo, wd, g_act, u_act),
        carried=carried)


def _row_groups(n_tiles, max_group, nk):
    gsz = max(q for q in range(1, max_group + 1) if n_tiles % q == 0)

    def epilogue_row(grp, kk, i):
        return grp * gsz + jnp.where(kk == nk - 1, i, 0)

    return gsz, epilogue_row


def _mm_residual_norm(a, w, h, g, scale, next_g, name, carried=()):
    m, k = a.shape
    d = w.shape[1]
    tm = _tile(m, ACC_ROWS, SUBLANE_BF16)
    tk = _tile(k, K_TILE, LANE)
    nk = k // tk
    gsz, epilogue_row = _row_groups(m // tm, ACC_GROUP, nk)

    def body(a_ref, w_ref, h_ref, g_ref, ng_ref, fo_ref, hn_ref, nn_ref, acc_ref):
        kk, i = pl.program_id(1), pl.program_id(2)

        @pl.when(kk == 0)
        def _():
            acc_ref[i] = jnp.zeros((tm, d), F32)

        acc_ref[i] += jnp.dot(a_ref[...], w_ref[...], preferred_element_type=F32)

        @pl.when(kk == nk - 1)
        def _():
            fo = acc_ref[i]
            fo_ref[...] = fo
            r = lax.rsqrt(jnp.mean(fo * fo, axis=-1, keepdims=True) + EPS)
            hn = h_ref[...] + scale * (fo * r * g_ref[...])
            hn_ref[...] = hn
            rn = lax.rsqrt(jnp.mean(hn * hn, axis=-1, keepdims=True) + EPS)
            nn_ref[...] = (hn * rn * ng_ref[...]).astype(nn_ref.dtype)

    row = pl.BlockSpec((tm, d), lambda grp, kk, i: (epilogue_row(grp, kk, i), 0))
    row_once = pl.BlockSpec((tm, d), lambda grp, kk, i: (epilogue_row(grp, kk, i), 0), pipeline_mode=pl.Buffered(1))
    vec = pl.BlockSpec((1, d), lambda grp, kk, i: (0, 0))
    return _call(
        body, grid=(m // tm // gsz, nk, gsz),
        in_specs=[pl.BlockSpec((tm, tk), lambda grp, kk, i: (grp * gsz + i, kk)),
                  pl.BlockSpec((tk, d), lambda grp, kk, i: (kk, 0)), row_once, vec, vec],
        out_specs=[row, row, row],
        out_shape=[jax.ShapeDtypeStruct((m, d), F32)] * 2 + [jax.ShapeDtypeStruct((m, d), MXU_DTYPE)],
        scratch_shapes=[pltpu.VMEM((gsz, tm, d), F32)], name=name, args=(a, w, h, g, next_g), carried=carried)


def _mm_residual_loss(a, w, h, g, scale, target, lead, name, carried=()):
    m, k = a.shape
    d = w.shape[1]
    tm = _tile(m, ACC_ROWS, SUBLANE_BF16)
    tk = _tile(k, K_TILE, LANE)
    nk = k // tk
    gsz, epilogue_row = _row_groups(m // tm, ACC_GROUP, nk)

    def body(a_ref, w_ref, h_ref, g_ref, t_ref, dy_ref, dfo_ref, dg_ref, l_ref, acc_ref):
        grp, kk, i = pl.program_id(0), pl.program_id(1), pl.program_id(2)

        @pl.when(jnp.logical_and(jnp.logical_and(grp == 0, kk == 0), i == 0))
        def _():
            dg_ref[...] = jnp.zeros_like(dg_ref)
            l_ref[...] = jnp.zeros_like(l_ref)

        @pl.when(kk == 0)
        def _():
            acc_ref[i] = jnp.zeros((tm, d), F32)

        acc_ref[i] += jnp.dot(a_ref[...], w_ref[...], preferred_element_type=F32)

        @pl.when(kk == nk - 1)
        def _():
            fo = acc_ref[i]
            gain = g_ref[...]
            r = lax.rsqrt(jnp.mean(fo * fo, axis=-1, keepdims=True) + EPS)
            xh = fo * r
            y = h_ref[...] + scale * (xh * gain)
            row = (grp * gsz + i) * tm + lax.broadcasted_iota(jnp.int32, (tm, 1), 0)
            e = jnp.where(row >= lead, y - t_ref[...], 0.0)
            dy = e * (1.0 / d)
            dy_ref[...] = dy
            l_ref[...] += 0.5 * jnp.sum(jnp.sum(e * e, axis=-1, keepdims=True) * (1.0 / d), axis=0, keepdims=True)
            dn = scale * dy
            dyh = dn * gain
            dfo_ref[...] = (r * (dyh - xh * jnp.mean(dyh * xh, axis=-1, keepdims=True))).astype(dfo_ref.dtype)
            dg_ref[...] += jnp.sum(dn * xh, axis=0, keepdims=True)

    row = pl.BlockSpec((tm, d), lambda grp, kk, i: (epilogue_row(grp, kk, i), 0))
    row_once = pl.BlockSpec((tm, d), lambda grp, kk, i: (epilogue_row(grp, kk, i), 0), pipeline_mode=pl.Buffered(1))
    vec = pl.BlockSpec((1, d), lambda grp, kk, i: (0, 0))
    return _call(
        body, grid=(m // tm // gsz, nk, gsz),
        in_specs=[pl.BlockSpec((tm, tk), lambda grp, kk, i: (grp * gsz + i, kk)),
                  pl.BlockSpec((tk, d), lambda grp, kk, i: (kk, 0)), row_once, vec, row_once],
        out_specs=[row, row, vec, pl.BlockSpec((1, 1), lambda grp, kk, i: (0, 0))],
        out_shape=[jax.ShapeDtypeStruct((m, d), F32), jax.ShapeDtypeStruct((m, d), MXU_DTYPE),
                   jax.ShapeDtypeStruct((1, d), F32), jax.ShapeDtypeStruct((1, 1), F32)],
        scratch_shapes=[pltpu.VMEM((gsz, tm, d), F32)], name=name, args=(a, w, h, g, target), carried=carried)


def _norm_bwd_row_tile(m):
    return _tile(m, ACC_ROWS, SUBLANE_BF16)


def _mm_norm_bwd(pairs, h, g, dh_up, name, carried=(), row_tiles=None, dg_init=None, post=None):
    n_pairs = len(pairs)
    m, k = pairs[0][0].shape
    d = h.shape[1]
    tm = _norm_bwd_row_tile(m)
    tk = _tile(k, K_TILE, LANE)
    nk = k // tk
    t0, nt = row_tiles if row_tiles is not None else (0, m // tm)
    gsz, epilogue_row = _row_groups(nt, ACC_GROUP, nk)
    if dg_init is None:
        dg_init = jnp.zeros((1, d), F32)

    n_post = 0 if post is None else 2

    def body(*refs):
        ops = refs[:2 * n_pairs]
        h_ref, g_ref, up_ref, init_ref = refs[2 * n_pairs:2 * n_pairs + 4]
        post_in = refs[2 * n_pairs + 4:2 * n_pairs + 4 + n_post]
        dh_ref, dg_ref = refs[2 * n_pairs + 4 + n_post:2 * n_pairs + 6 + n_post]
        post_out = refs[2 * n_pairs + 6 + n_post:2 * n_pairs + 6 + 2 * n_post]
        acc_ref = refs[-1]
        grp, kk, i = pl.program_id(0), pl.program_id(1), pl.program_id(2)

        @pl.when(jnp.logical_and(jnp.logical_and(grp == 0, kk == 0), i == 0))
        def _():
            dg_ref[...] = init_ref[...]
            if post is not None:
                post_out[1][...] = jnp.zeros_like(post_out[1])

        @pl.when(kk == 0)
        def _():
            acc_ref[i] = jnp.zeros((tm, d), F32)

        for p in range(n_pairs):
            acc_ref[i] += jnp.dot(ops[2 * p][...], ops[2 * p + 1][...], preferred_element_type=F32)

        @pl.when(kk == nk - 1)
        def _():
            dx, dgr = _rmsnorm_bwd_rows(h_ref[...], g_ref[...], acc_ref[i])
            dh = up_ref[...] + dx
            dh_ref[...] = dh
            dg_ref[...] += jnp.sum(dgr, axis=0, keepdims=True)
            if post is not None:
                dfo, dpr = _rmsnorm_bwd_rows(post_in[0][...], post_in[1][...], post[2] * dh)
                post_out[0][...] = dfo.astype(post_out[0].dtype)
                post_out[1][...] += jnp.sum(dpr, axis=0, keepdims=True)

    row_in = pl.BlockSpec((tm, d), lambda grp, kk, i: (t0 + epilogue_row(grp, kk, i), 0))
    row_out = pl.BlockSpec((tm, d), lambda grp, kk, i: (epilogue_row(grp, kk, i), 0))
    vec = pl.BlockSpec((1, d), lambda grp, kk, i: (0, 0))
    in_specs = []
    args = []
    for a, w in pairs:
        in_specs += [pl.BlockSpec((tm, tk), lambda grp, kk, i: (t0 + grp * gsz + i, kk)),
                     pl.BlockSpec((tk, d), lambda grp, kk, i: (kk, 0))]
        args += [a, w]
    in_specs += [row_in, vec, row_in, vec]
    args += [h, g, dh_up, dg_init]
    out_specs = [row_out, vec]
    out_shape = [jax.ShapeDtypeStruct((nt * tm, d), F32), jax.ShapeDtypeStruct((1, d), F32)]
    if post is not None:
        in_specs += [row_in, vec]
        args += [post[0], post[1]]
        out_specs += [row_out, vec]
        out_shape += [jax.ShapeDtypeStruct((nt * tm, d), MXU_DTYPE), jax.ShapeDtypeStruct((1, d), F32)]
    return _call(
        body, grid=(nt // gsz, nk, gsz), in_specs=in_specs, out_specs=out_specs, out_shape=out_shape,
        scratch_shapes=[pltpu.VMEM((gsz, tm, d), F32)], name=name, args=tuple(args), carried=carried)


def _mm_tn(a, b, name, carried=()):
    m, ka = a.shape
    d = b.shape[1]
    tf = _tile(ka, 512, LANE)

    def body(a_ref, b_ref, o_ref):
        o_ref[...] = _dot_tn(a_ref[...], b_ref[...]).astype(o_ref.dtype)

    return _call(
        body, grid=(ka // tf,),
        in_specs=[pl.BlockSpec((m, tf), lambda j: (0, j)),
                  pl.BlockSpec((m, d), lambda j: (0, 0), pipeline_mode=pl.Buffered(1))],
        out_specs=pl.BlockSpec((tf, d), lambda j: (j, 0)),
        out_shape=jax.ShapeDtypeStruct((ka, d), WIRE_DTYPE), name=name, args=(a, b), carried=carried)


GELU_K = 0.7978845608028654
GELU_C = 0.044715


def _expm1(x):
    series = x * (1.0 + x * (1.0 / 2 + x * (1.0 / 6 + x * (1.0 / 24 + x * (1.0 / 120)))))
    return jnp.where(jnp.abs(x) < 0.1, series, jnp.exp(x) - 1.0)


def _softplus(x):
    return jnp.maximum(x, 0.0) + jnp.log1p(jnp.exp(-jnp.abs(x)))


def _block_mm(v, w_ref, transposed):
    nbk = w_ref.shape[0]
    outs = []
    for j in range(nbk):
        vj = v[:, j * BD:(j + 1) * BD]
        outs.append(_dot_nt(vj, w_ref[j]) if transposed else jnp.dot(vj, w_ref[j], preferred_element_type=F32))
    return outs[0] if nbk == 1 else jnp.concatenate(outs, axis=1)


def _group_mean(q, gm_ref):
    qb = q.astype(MXU_DTYPE)
    nbk = q.shape[1] // BD
    gm = gm_ref[...]
    outs = [jnp.dot(qb[:, j * BD:(j + 1) * BD], gm, preferred_element_type=F32) for j in range(nbk)]
    return outs[0] if nbk == 1 else jnp.concatenate(outs, axis=1)


class _RowReader:
    def __init__(self, ref):
        self.ref = ref

    def __getitem__(self, rows):
        return self.ref[rows, :]


def _shifted(ext_ref, cur, before8, after8, downs=(), ups=()):
    r = cur.shape[0]
    if downs:
        ext_ref[0:8, :] = before8
    ext_ref[8:8 + r, :] = cur
    if ups:
        ext_ref[8 + r:16 + r, :] = after8
    return [ext_ref[pl.ds(8 - j, r), :] for j in downs] + [ext_ref[pl.ds(8 + j, r), :] for j in ups]


def _lru_gates(xc, pv, wa_ref, wx_ref):
    xcb = xc.astype(MXU_DTYPE)
    ga = jax.nn.sigmoid(_block_mm(xcb, wa_ref, False) + pv[5:6])
    gx = jax.nn.sigmoid(_block_mm(xcb, wx_ref, False) + pv[6:7])
    sp = _softplus(-pv[7:8])
    log_a = -LRU_C * ga * sp
    a = jnp.exp(log_a)
    e2 = _expm1(2.0 * log_a)
    mult = jnp.sqrt(-e2)
    return xcb, ga, gx, sp, a, e2, mult


def _gelu_parts(y):
    th = jnp.tanh(GELU_K * (y + GELU_C * y * y * y))
    return 0.5 * y * (1.0 + th), th


def _scan_block(a, u, sa_ref, su_ref, carry_ref, out_ref, reverse):
    r, c = a.shape
    n = r // 8
    a3 = a.reshape(n, 8, c)
    u3 = u.reshape(n, 8, c)
    sub = lax.broadcasted_iota(jnp.int32, (n, 8, c), 1)
    for dlt in (1, 2, 4):
        keep = (sub < 8 - dlt) if reverse else (sub >= dlt)
        shift = 8 - dlt if reverse else dlt
        sh_a = pltpu.roll(a3, shift, axis=1)
        sh_u = pltpu.roll(u3, shift, axis=1)
        u3 = u3 + a3 * jnp.where(keep, sh_u, 0.0)
        a3 = a3 * jnp.where(keep, sh_a, 1.0)
    sa_ref[...] = a3.reshape(r, c)
    su_ref[...] = u3.reshape(r, c)
    for k in (range(n - 1, -1, -1) if reverse else range(n)):
        rows = pl.ds(8 * k, 8)
        out_ref[rows, :] = su_ref[rows, :] + sa_ref[rows, :] * carry_ref[...]
        carry_ref[...] = out_ref[pl.ds(8 * k if reverse else 8 * k + 7, 1), :]


def _mixer_fwd(z, pv, wa, wx, gm, pad, name, carried=()):
    m = z.shape[0]
    c = pv.shape[1]
    r = MIX_ROWS
    nb = m // r

    def body(z_ref, pv_ref, wa_ref, wx_ref, gm_ref, mixed_ref, hs_ref, ext_ref, tailx_ref, tailc_ref, carry_ref,
             sa_ref, su_ref):
        b = pl.program_id(0)

        @pl.when(b == 0)
        def _():
            tailx_ref[...] = jnp.zeros_like(tailx_ref)
            tailc_ref[...] = jnp.zeros_like(tailc_ref)
            carry_ref[...] = jnp.zeros_like(carry_ref)

        pv = _RowReader(pv_ref)
        row = b * r + lax.broadcasted_iota(jnp.int32, (r, 1), 0)
        maskf = (row >= pad).astype(F32)
        y = z_ref[:, 0:c]
        xl = z_ref[:, c:2 * c]
        bs = z_ref[:, 2 * c:3 * c]
        cv = z_ref[:, 3 * c:4 * c] * z_ref[:, 4 * c:5 * c]

        x1, x2, x3 = _shifted(ext_ref, xl, tailx_ref[...], None, downs=(1, 2, 3))
        tailx_ref[...] = z_ref[pl.ds(r - 8, 8), c:2 * c]
        xc = pv[4:5] + pv[3:4] * xl + pv[2:3] * x1 + pv[1:2] * x2 + pv[0:1] * x3
        _, _, gx, _, a, _, mult = _lru_gates(xc, pv, wa_ref, wx_ref)
        uu = mult * (gx * xc) * maskf

        _scan_block(a, uu, sa_ref, su_ref, carry_ref, hs_ref, reverse=False)
        hs = hs_ref[...]

        gelu_y, _ = _gelu_parts(y)
        lru_out = hs * gelu_y
        c1, c2 = _shifted(ext_ref, cv, tailc_ref[...], None, downs=(1, 2))
        tailc_ref[...] = cv[r - 8:r]
        sc_out = bs * (pv[10:11] * cv + pv[9:10] * c1 + pv[8:9] * c2)

        rl = lax.rsqrt(_group_mean(lru_out * lru_out, gm_ref) + EPS)
        rs = lax.rsqrt(_group_mean(sc_out * sc_out, gm_ref) + EPS)
        mixed_ref[:, 0:c] = (lru_out * rl * pv[11:12]).astype(mixed_ref.dtype)
        mixed_ref[:, c:2 * c] = (sc_out * rs * pv[12:13]).astype(mixed_ref.dtype)

    full = lambda shape: pl.BlockSpec(shape, lambda b: (0,) * len(shape))
    return _call(
        body, grid=(nb,),
        in_specs=[pl.BlockSpec((r, 5 * c), lambda b: (b, 0)), full(pv.shape), full(wa.shape), full(wx.shape), full(gm.shape)],
        out_specs=[pl.BlockSpec((r, 2 * c), lambda b: (b, 0)), pl.BlockSpec((r, c), lambda b: (b, 0))],
        out_shape=[jax.ShapeDtypeStruct((m, 2 * c), MXU_DTYPE), jax.ShapeDtypeStruct((m, c), F32)],
        scratch_shapes=[pltpu.VMEM((r + 16, c), F32), pltpu.VMEM((8, c), F32), pltpu.VMEM((8, c), F32),
                        pltpu.VMEM((1, c), F32), pltpu.VMEM((r, c), F32), pltpu.VMEM((r, c), F32)],
        name=name, args=(z, pv, wa, wx, gm), carried=carried)


def _mixer_bwd(z, hs, dmixed, pv, wa, wx, gm, pad, name, carried=()):
    m = z.shape[0]
    c = pv.shape[1]
    r = MIX_ROWS
    nb = m // r
    r8 = r // 8
    assert pad <= r and pad % SUBLANE_BF16 == 0

    def body(z_ref, zp_ref, hs_ref, hsp_ref, dm_ref, pv_ref, wa_ref, wx_ref, gm_ref,
             dz_ref, dpv_ref, dwa_ref, dwx_ref, ext_ref, hxc_ref, hsc_ref, hp_ref, pc_ref, sa_ref, su_ref, p_ref):
        i = pl.program_id(0)
        b = nb - 1 - i

        @pl.when(i == 0)
        def _():
            hxc_ref[...] = jnp.zeros_like(hxc_ref)
            hsc_ref[...] = jnp.zeros_like(hsc_ref)
            hp_ref[...] = jnp.zeros_like(hp_ref)
            pc_ref[...] = jnp.zeros_like(pc_ref)
            dpv_ref[...] = jnp.zeros_like(dpv_ref)
            dwa_ref[...] = jnp.zeros_like(dwa_ref)
            dwx_ref[...] = jnp.zeros_like(dwx_ref)

        pv = _RowReader(pv_ref)
        row = b * r + lax.broadcasted_iota(jnp.int32, (r, 1), 0)
        maskf = (row >= pad).astype(F32)
        has_prev = (b > 0).astype(F32)
        y = z_ref[:, 0:c]
        xl = z_ref[:, c:2 * c]
        bs = z_ref[:, 2 * c:3 * c]
        cs = z_ref[:, 3 * c:4 * c]
        vs = z_ref[:, 4 * c:5 * c]
        cv = cs * vs
        xl_prev = zp_ref[:, c:2 * c] * has_prev
        cv_prev = zp_ref[:, 3 * c:4 * c] * zp_ref[:, 4 * c:5 * c] * has_prev
        hs = hs_ref[...]

        x1, x2, x3 = _shifted(ext_ref, xl, xl_prev, None, downs=(1, 2, 3))
        xc = pv[4:5] + pv[3:4] * xl + pv[2:3] * x1 + pv[1:2] * x2 + pv[0:1] * x3
        xcb, ga, gx, sp, a, e2, mult = _lru_gates(xc, pv, wa_ref, wx_ref)
        gxx = gx * xc
        gelu_y, th = _gelu_parts(y)
        lru_out = hs * gelu_y
        c1, c2 = _shifted(ext_ref, cv, cv_prev, None, downs=(1, 2))
        sc = pv[10:11] * cv + pv[9:10] * c1 + pv[8:9] * c2
        sc_out = bs * sc

        def group_norm_bwd(v, dm, gain):
            rr = lax.rsqrt(_group_mean(v * v, gm_ref) + EPS)
            vh = v * rr
            dvh = dm * gain
            dv = rr * (dvh - vh * _group_mean(dvh * vh, gm_ref))
            return dv, jnp.sum(dm * vh, axis=0, keepdims=True)

        d_lru_out, d_og = group_norm_bwd(lru_out, dm_ref[:, 0:c], pv[11:12])
        d_sc_out, d_sg = group_norm_bwd(sc_out, dm_ref[:, c:2 * c], pv[12:13])
        dpv_ref[11:12, :] += d_og
        dpv_ref[12:13, :] += d_sg

        dhs = d_lru_out * gelu_y
        dgelu = 0.5 * (1.0 + th) + 0.5 * y * (1.0 - th * th) * GELU_K * (1.0 + 3.0 * GELU_C * y * y)
        dy = d_lru_out * hs * dgelu

        _scan_block(a, a * dhs, sa_ref, su_ref, pc_ref, p_ref, reverse=True)
        (p_next,) = _shifted(ext_ref, p_ref[...], None, hp_ref[...], ups=(1,))
        hp_ref[...] = p_ref[0:8, :]
        q = dhs + p_next
        (hs_prev,) = _shifted(ext_ref, hs, hsp_ref[...] * has_prev, None, downs=(1,))
        duu = q * maskf
        da = q * hs_prev

        dmult = duu * gxx
        dgxx = duu * mult
        dgx = dgxx * xc
        dxc = dgxx * gx
        dlog_a = da * a - dmult * ((1.0 + e2) / mult)
        dga = dlog_a * (-LRU_C * sp)
        dsp = jnp.sum(dlog_a * (-LRU_C * ga), axis=0, keepdims=True)
        dpv_ref[7:8, :] += dsp * (-jax.nn.sigmoid(-pv[7:8]))
        dga_pre = dga * ga * (1.0 - ga)
        dgx_pre = dgx * gx * (1.0 - gx)
        dpv_ref[5:6, :] += jnp.sum(dga_pre, axis=0, keepdims=True)
        dpv_ref[6:7, :] += jnp.sum(dgx_pre, axis=0, keepdims=True)
        dga_b = dga_pre.astype(MXU_DTYPE)
        dgx_b = dgx_pre.astype(MXU_DTYPE)
        dxc = dxc + _block_mm(dga_b, wa_ref, True) + _block_mm(dgx_b, wx_ref, True)
        for j in range(c // BD):
            sl = slice(j * BD, (j + 1) * BD)
            dwa_ref[j] += _dot_tn(xcb[:, sl], dga_b[:, sl])
            dwx_ref[j] += _dot_tn(xcb[:, sl], dgx_b[:, sl])

        dpv_ref[4:5, :] += jnp.sum(dxc, axis=0, keepdims=True)
        dpv_ref[3:4, :] += jnp.sum(dxc * xl, axis=0, keepdims=True)
        dpv_ref[2:3, :] += jnp.sum(dxc * x1, axis=0, keepdims=True)
        dpv_ref[1:2, :] += jnp.sum(dxc * x2, axis=0, keepdims=True)
        dpv_ref[0:1, :] += jnp.sum(dxc * x3, axis=0, keepdims=True)
        u1, u2, u3 = _shifted(ext_ref, dxc, None, hxc_ref[...], ups=(1, 2, 3))
        hxc_ref[...] = dxc[0:8]
        dxl = pv[3:4] * dxc + pv[2:3] * u1 + pv[1:2] * u2 + pv[0:1] * u3

        dbs = d_sc_out * sc
        dsc = d_sc_out * bs
        dpv_ref[10:11, :] += jnp.sum(dsc * cv, axis=0, keepdims=True)
        dpv_ref[9:10, :] += jnp.sum(dsc * c1, axis=0, keepdims=True)
        dpv_ref[8:9, :] += jnp.sum(dsc * c2, axis=0, keepdims=True)
        s1, s2 = _shifted(ext_ref, dsc, None, hsc_ref[...], ups=(1, 2))
        hsc_ref[...] = dsc[0:8]
        dcv = pv[10:11] * dsc + pv[9:10] * s1 + pv[8:9] * s2

        dz_ref[:, 0:c] = dy.astype(dz_ref.dtype)
        dz_ref[:, c:2 * c] = dxl.astype(dz_ref.dtype)
        dz_ref[:, 2 * c:3 * c] = dbs.astype(dz_ref.dtype)
        dz_ref[:, 3 * c:4 * c] = (dcv * vs).astype(dz_ref.dtype)
        dz_ref[:, 4 * c:5 * c] = (dcv * cs).astype(dz_ref.dtype)

        if pad:
            @pl.when(b == 0)
            def _():
                dz_ref[0:pad, :] = jnp.zeros((pad, 5 * c), dz_ref.dtype)

    full = lambda shape: pl.BlockSpec(shape, lambda i: (0,) * len(shape))
    cur = lambda width: pl.BlockSpec((r, width), lambda i: (nb - 1 - i, 0))
    prev8 = lambda width: pl.BlockSpec((8, width), lambda i: (jnp.maximum((nb - 1 - i) * r8 - 1, 0), 0))
    return _call(
        body, grid=(nb,),
        in_specs=[cur(5 * c), prev8(5 * c), cur(c), prev8(c), cur(2 * c),
                  full(pv.shape), full(wa.shape), full(wx.shape), full(gm.shape)],
        out_specs=[cur(5 * c), full(pv.shape), full(wa.shape), full(wx.shape)],
        out_shape=[jax.ShapeDtypeStruct((m, 5 * c), MXU_DTYPE), jax.ShapeDtypeStruct(pv.shape, F32),
                   jax.ShapeDtypeStruct(wa.shape, F32), jax.ShapeDtypeStruct(wx.shape, F32)],
        scratch_shapes=[pltpu.VMEM((r + 16, c), F32), pltpu.VMEM((8, c), F32), pltpu.VMEM((8, c), F32),
                        pltpu.VMEM((8, c), F32), pltpu.VMEM((1, c), F32), pltpu.VMEM((r, c), F32),
                        pltpu.VMEM((r, c), F32), pltpu.VMEM((r, c), F32)],
        name=name, args=(z, z, hs, hs, dmixed, pv, wa, wx, gm), carried=carried)


def _position():
    return lax.axis_index("x"), lax.axis_index("y"), lax.axis_index("c")


def _block_of(px, py, pc):
    return 4 * px + 2 * py + pc


class _TwoLevelGather:
    def __init__(self, n_arrays, rows_of, src_of, send_sems, recv_sems):
        x, y, c = _position()
        self.n, self.rows_of, self.src_of = n_arrays, rows_of, src_of
        self.send_sems, self.recv_sems = send_sems, recv_sems
        self.c, self.me, self.sibling = c, (x, y, c), (x, y, 1 - c)
        self.chips = [(1 - x, y), (x, 1 - y), (1 - x, 1 - y)]

    def _copy(self, i, k, block, to, src=None):
        return pltpu.make_async_remote_copy(
            src_ref=self.rows_of(i, *block) if src is None else src, dst_ref=self.rows_of(i, *block),
            send_sem=self.send_sems.at[7 * i + k], recv_sem=self.recv_sems.at[7 * i + k],
            device_id=to, device_id_type=MESH)

    def _first(self, i):
        own = [self._copy(i, 0, self.me, self.sibling, src=self.src_of(i))]
        return own + [self._copy(i, 1 + j, self.me, (*chip, self.c), src=self.src_of(i))
                      for j, chip in enumerate(self.chips)]

    def _passed(self, i, j):
        return self._copy(i, 4 + j, (*self.chips[j], self.c), self.sibling)

    def start(self):
        for i in range(self.n):
            for cp in self._first(i):
                cp.start()

    def forward(self):
        for i in range(self.n):
            for j, chip in enumerate(self.chips):
                self._copy(i, 1 + j, (*chip, self.c), self.me).wait_recv()
                self._passed(i, j).start()

    def drain(self):
        for i in range(self.n):
            self._copy(i, 0, self.sibling, self.me).wait_recv()
            for j, chip in enumerate(self.chips):
                self._copy(i, 4 + j, (*chip, 1 - self.c), self.me).wait_recv()
        for i in range(self.n):
            for cp in self._first(i) + [self._passed(i, j) for j in range(3)]:
                cp.wait_send()


class _RelayGather:
    def __init__(self, n_arrays, rows_of, src_of, send_sems, recv_sems):
        x, y, c = _position()
        self.n, self.rows_of, self.src_of = n_arrays, rows_of, src_of
        self.send_sems, self.recv_sems = send_sems, recv_sems
        self.me, self.sibling = (x, y, c), (x, y, 1 - c)
        self.xn, self.yn, self.dg = (1 - x, y, c), (x, 1 - y, c), (1 - x, 1 - y, c)

    def _copy(self, i, k, block, to, half=None, src=None):
        rows = self.rows_of(i, *block, half)
        return pltpu.make_async_remote_copy(
            src_ref=rows if src is None else src, dst_ref=rows,
            send_sem=self.send_sems.at[8 * i + k], recv_sem=self.recv_sems.at[8 * i + k],
            device_id=to, device_id_type=MESH)

    def _sends(self, i):
        own = self.src_of(i)
        return [self._copy(i, 0, self.me, self.sibling, src=own), self._copy(i, 1, self.me, self.xn, src=own),
                self._copy(i, 2, self.me, self.yn, src=own),
                self._copy(i, 3, self.xn, self.yn, half=0), self._copy(i, 4, self.yn, self.xn, half=1),
                self._copy(i, 5, self.xn, self.sibling), self._copy(i, 6, self.yn, self.sibling),
                self._copy(i, 7, self.dg, self.sibling)]

    def start(self):
        for i in range(self.n):
            for cp in self._sends(i)[0:3]:
                cp.start()

    def forward(self):
        for i in range(self.n):
            self._copy(i, 1, self.xn, self.me).wait_recv()
            self._copy(i, 2, self.yn, self.me).wait_recv()
            for cp in self._sends(i)[3:7]:
                cp.start()

    def drain(self):
        x, y, c = self.me
        for i in range(self.n):
            self._copy(i, 3, self.dg, self.me, half=0).wait_recv()
            self._copy(i, 4, self.dg, self.me, half=1).wait_recv()
            self._sends(i)[7].start()
        for i in range(self.n):
            self._copy(i, 0, self.sibling, self.me).wait_recv()
            self._copy(i, 5, (1 - x, y, 1 - c), self.me).wait_recv()
            self._copy(i, 6, (x, 1 - y, 1 - c), self.me).wait_recv()
            self._copy(i, 7, (1 - x, 1 - y, 1 - c), self.me).wait_recv()
        for i in range(self.n):
            for cp in self._sends(i):
                cp.wait_send()


class _CarriedGather:
    def __init__(self, shards, padded_rows, zeros, forward_at, part=None, into=None):
        d = shards[0].shape[1]
        self.forward_at = forward_at
        self.n = len(shards)
        self.rows = [s.shape[0] for s in shards]
        self.pads = [p - N_DEV * r for r, p in zip(self.rows, padded_rows)]
        assert max(self.pads) <= zeros.shape[0] and zeros.shape[1] == d
        self.part = part if part is not None else (0, self.rows[0])
        assert (part is None and into is None) or self.n == 1
        assert self.part[0] % SUBLANE_BF16 == 0 and self.part[1] % SUBLANE_BF16 == 0
        self.arrays = list(shards) + [zeros] + ([into] if into is not None else [])
        self.out_shapes = [jax.ShapeDtypeStruct((p, d), s.dtype) for s, p in zip(shards, padded_rows)]
        self.aliases = {self.n + 1: 0} if into is not None else {}
        if into is not None:
            self.pads = [0] * self.n
        self.n_remote, self.n_local = 8 * self.n, 2 * self.n
        self.results = None

    def _rows_of(self, outs):
        def rows_of(i, px, py, pc, half):
            first, count = (self.part if self.n == 1 else (0, self.rows[i]))
            head = _round_up(count // 2, SUBLANE_BF16)
            if half == 0:
                count = head
            elif half == 1:
                first, count = first + head, count - head
            first = _block_of(px, py, pc) * self.rows[i] + first
            return outs[i].at[pl.ds(pl.multiple_of(first, SUBLANE_BF16), count), :]
        return rows_of

    def _own(self, ins, i):
        return ins[i].at[pl.ds(self.part[0], self.part[1]), :] if self.n == 1 else ins[i]

    def _gather(self, ins, outs, send_sems, recv_sems):
        return _RelayGather(self.n, self._rows_of(outs), functools.partial(self._own, ins), send_sems, recv_sems)

    def _local(self, ins, outs, local_sems):
        x, y, c = _position()
        rows_of = self._rows_of(outs)
        cps = []
        for i in range(self.n):
            cps.append(pltpu.make_async_copy(self._own(ins, i), rows_of(i, x, y, c, None), local_sems.at[2 * i]))
            if self.pads[i]:
                cps.append(pltpu.make_async_copy(ins[self.n].at[pl.ds(0, self.pads[i]), :],
                                                 outs[i].at[pl.ds(N_DEV * self.rows[i], self.pads[i]), :],
                                                 local_sems.at[2 * i + 1]))
        return cps

    def start(self, ins, outs, send_sems, recv_sems, local_sems):
        for cp in self._local(ins, outs, local_sems):
            cp.start()
        self._gather(ins, outs, send_sems, recv_sems).start()

    def forward(self, ins, outs, send_sems, recv_sems, local_sems):
        self._gather(ins, outs, send_sems, recv_sems).forward()

    def finish(self, ins, outs, send_sems, recv_sems, local_sems):
        self._gather(ins, outs, send_sems, recv_sems).drain()
        for cp in self._local(ins, outs, local_sems):
            cp.wait()


class _CarriedSwap:
    def __init__(self, grads, shard_rows):
        d = grads[0].shape[1]
        self.n, self.rows = len(grads), list(shard_rows)
        self.arrays = list(grads)
        self.out_shapes = [jax.ShapeDtypeStruct((4, s, d), g.dtype) for g, s in zip(grads, shard_rows)]
        self.aliases = {}
        self.n_remote, self.n_local = 4 * self.n, 0
        self.forward_at = 1.0
        self.results = None

    def _copies(self, ins, outs, send_sems, recv_sems):
        x, y, c = _position()
        cps = []
        for i in range(self.n):
            s = self.rows[i]
            for k in range(4):
                blk = _block_of(k >> 1, k & 1, 1 - c)
                cps.append(pltpu.make_async_remote_copy(
                    src_ref=ins[i].at[pl.ds(pl.multiple_of(blk * s, SUBLANE_BF16), s), :], dst_ref=outs[i].at[k],
                    send_sem=send_sems.at[4 * i + k], recv_sem=recv_sems.at[4 * i + k],
                    device_id=(x, y, 1 - c), device_id_type=MESH))
        return cps

    def start(self, ins, outs, send_sems, recv_sems, local_sems):
        for cp in self._copies(ins, outs, send_sems, recv_sems):
            cp.start()

    def forward(self, *_):
        pass

    def finish(self, ins, outs, send_sems, recv_sems, local_sems):
        for cp in self._copies(ins, outs, send_sems, recv_sems):
            cp.wait()


class _CarriedChipExchange:
    def __init__(self, presums, part=None, into=None):
        self.n = len(presums)
        assert (part is None and into is None) or self.n == 1
        self.part = part if part is not None else (0, presums[0].shape[1])
        assert self.part[0] % SUBLANE_BF16 == 0 and self.part[1] % SUBLANE_BF16 == 0
        self.arrays = list(presums) + ([into] if into is not None else [])
        self.out_shapes = [jax.ShapeDtypeStruct(p.shape, p.dtype) for p in presums]
        self.aliases = {self.n: 0} if into is not None else {}
        self.n_remote, self.n_local = 3 * self.n, 0
        self.forward_at = 1.0
        self.results = None

    def _copies(self, ins, outs, send_sems, recv_sems):
        x, y, c = _position()
        cps = []
        for i in range(self.n):
            rows = pl.ds(*self.part) if self.n == 1 else pl.ds(0, self.arrays[i].shape[1])
            for r in range(1, 4):
                cps.append(pltpu.make_async_remote_copy(
                    src_ref=ins[i].at[r - 1, rows, :], dst_ref=outs[i].at[r - 1, rows, :],
                    send_sem=send_sems.at[3 * i + r - 1], recv_sem=recv_sems.at[3 * i + r - 1],
                    device_id=(x ^ (r >> 1), y ^ (r & 1), c), device_id_type=MESH))
        return cps

    def start(self, ins, outs, send_sems, recv_sems, local_sems):
        for cp in self._copies(ins, outs, send_sems, recv_sems):
            cp.start()

    def forward(self, *_):
        pass

    def finish(self, ins, outs, send_sems, recv_sems, local_sems):
        for cp in self._copies(ins, outs, send_sems, recv_sems):
            cp.wait()


def _gather_small(block, reduce, name):
    rr, nn = block.shape

    def body(x_ref, out_ref, *rest):
        if reduce:
            stack_ref, send_sems, recv_sems, local_sem = rest
        else:
            send_sems, recv_sems, local_sem = rest
            stack_ref = out_ref
        x, y, c = _position()

        def rows_of(i, px, py, pc):
            return stack_ref.at[pl.ds(pl.multiple_of(_block_of(px, py, pc) * rr, 8), rr), :]

        own = pltpu.make_async_copy(x_ref, rows_of(0, x, y, c), local_sem)
        own.start()
        gather = _TwoLevelGather(1, rows_of, lambda i: x_ref, send_sems, recv_sems)
        gather.start()
        gather.forward()
        gather.drain()
        own.wait()
        if reduce:
            acc = stack_ref[0:rr, :]
            for k in range(1, N_DEV):
                acc = acc + stack_ref[k * rr:(k + 1) * rr, :]
            out_ref[...] = acc

    vmem = pl.BlockSpec(memory_space=pltpu.VMEM)
    scratch = [pltpu.SemaphoreType.DMA((7,)), pltpu.SemaphoreType.DMA((7,)), pltpu.SemaphoreType.DMA]
    if reduce:
        scratch = [pltpu.VMEM((N_DEV * rr, nn), F32)] + scratch
    out_rows = rr if reduce else N_DEV * rr
    return pl.pallas_call(
        body, in_specs=[vmem], out_specs=vmem, out_shape=jax.ShapeDtypeStruct((out_rows, nn), F32),
        scratch_shapes=scratch, name=name, compiler_params=_params())(block)


def _sum_stack(stack, name):
    rr = stack.shape[0] // N_DEV

    def body(s_ref, o_ref):
        acc = s_ref[0:rr, :]
        for k in range(1, N_DEV):
            acc = acc + s_ref[k * rr:(k + 1) * rr, :]
        o_ref[...] = acc

    vmem = pl.BlockSpec(memory_space=pltpu.VMEM)
    return pl.pallas_call(body, in_specs=[vmem], out_specs=vmem,
                          out_shape=jax.ShapeDtypeStruct((rr, stack.shape[1]), F32), name=name,
                          compiler_params=_params())(stack)


def _presum(where, grad, swapped, name):
    s, d = swapped.shape[1], swapped.shape[2]
    tc = _tile(d, 512, LANE)

    def body(where_ref, g_ref, sw_ref, o_ref):
        o_ref[0] = (g_ref[...].astype(F32) + sw_ref[0].astype(F32)).astype(o_ref.dtype)

    return _call(
        body, grid=(3, d // tc),
        in_specs=[pl.BlockSpec((s, tc), lambda r, j, where: (where[1 + r], j)),
                  pl.BlockSpec((1, s, tc), lambda r, j, where: (where[5 + r], 0, j))],
        out_specs=pl.BlockSpec((1, s, tc), lambda r, j, where: (r, 0, j)),
        out_shape=jax.ShapeDtypeStruct((3, s, d), WIRE_DTYPE), name=name, args=(grad, swapped), prefetch=(where,))


def _final_sum(where, grad, swapped, received, name, carried=()):
    s, d = swapped.shape[1], swapped.shape[2]
    tc = _tile(d, 512, LANE)

    def body(where_ref, g_ref, sw_ref, r_ref, o_ref):
        acc = g_ref[...].astype(F32) + sw_ref[0].astype(F32)
        for k in range(3):
            acc = acc + r_ref[k].astype(F32)
        o_ref[...] = acc

    return _call(
        body, grid=(d // tc,),
        in_specs=[pl.BlockSpec((s, tc), lambda j, where: (where[0], j)),
                  pl.BlockSpec((1, s, tc), lambda j, where: (where[4], 0, j)),
                  pl.BlockSpec((3, s, tc), lambda j, where: (0, 0, j))],
        out_specs=pl.BlockSpec((s, tc), lambda j, where: (0, j)),
        out_shape=jax.ShapeDtypeStruct((s, d), F32), name=name, args=(grad, swapped, received),
        prefetch=(where,), carried=carried)


class _GradReduction:
    def __init__(self, key, grad, shard_rows, where):
        self.key, self.grad, self.rows, self.where = key, grad, shard_rows, where
        self._presum = self._exchange = None

    def swap(self):
        self._swap = _CarriedSwap([self.grad], [self.rows])
        return self._swap

    def exchange(self, part=None):
        if self._presum is None:
            self._presum = _presum(self.where, self.grad, self._swap.results[0], "presum_" + self.key)
        rows = None
        if part is not None:
            half = _round_up(self.rows // 2, SUBLANE_BF16)
            rows = (0, half) if part == 0 else (half, self.rows - half)
        into = self._exchange.results[0] if part == 1 else None
        self._exchange = _CarriedChipExchange([self._presum], rows, into)
        return self._exchange

    def total(self, carried=()):
        return _final_sum(self.where, self.grad, self._swap.results[0], self._exchange.results[0],
                          "sum_" + self.key, carried)

    def total_and_update(self, w, m, v):
        return _sum_adamw(self.where, self.grad, self._swap.results[0], self._exchange.results[0], w, m, v,
                          "update_" + self.key)


def _adamw_math(w, g, m, v):
    nm = ADAM_B1 * m + (1.0 - ADAM_B1) * g
    nv = ADAM_B2 * v + (1.0 - ADAM_B2) * (g * g)
    m_hat = nm / (1.0 - ADAM_B1 ** ADAM_STEP)
    v_hat = nv / (1.0 - ADAM_B2 ** ADAM_STEP)
    return -ADAM_LR * (m_hat / (jnp.sqrt(v_hat) + ADAM_EPS) + ADAM_WD * w), nm, nv


def _sum_adamw(where, grad, swapped, received, w, m, v, name):
    s, d = swapped.shape[1], swapped.shape[2]
    tc = _tile(d, 512, LANE)

    def body(where_ref, g_ref, sw_ref, r_ref, w_ref, m_ref, v_ref, gs_ref, d_ref, nm_ref, nv_ref):
        g = g_ref[...].astype(F32) + sw_ref[0].astype(F32)
        for k in range(3):
            g = g + r_ref[k].astype(F32)
        gs_ref[...] = g
        d_ref[...], nm_ref[...], nv_ref[...] = _adamw_math(w_ref[...], g, m_ref[...], v_ref[...])

    blk = pl.BlockSpec((s, tc), lambda j, where: (0, j))
    return _call(
        body, grid=(d // tc,),
        in_specs=[pl.BlockSpec((s, tc), lambda j, where: (where[0], j)),
                  pl.BlockSpec((1, s, tc), lambda j, where: (where[4], 0, j)),
                  pl.BlockSpec((3, s, tc), lambda j, where: (0, 0, j)), blk, blk, blk],
        out_specs=[blk] * 4, out_shape=[jax.ShapeDtypeStruct((s, d), F32)] * 4, name=name,
        args=(grad, swapped, received, w, m, v), prefetch=(where,))


def _adamw(w, g, m, v, name):
    rows, cols = w.shape
    tr = _tile(rows, 256, 8)

    def body(w_ref, g_ref, m_ref, v_ref, d_ref, nm_ref, nv_ref):
        d_ref[...], nm_ref[...], nv_ref[...] = _adamw_math(w_ref[...], g_ref[...], m_ref[...], v_ref[...])

    spec = pl.BlockSpec((tr, cols), lambda i: (i, 0))
    return pl.pallas_call(
        body, grid=(rows // tr,), in_specs=[spec] * 4, out_specs=[spec] * 3,
        out_shape=[jax.ShapeDtypeStruct((rows, cols), F32)] * 3, name=name, compiler_params=_params())(w, g, m, v)


def _pack_rows(arrays, width, row_quantum=8):
    flat = jnp.concatenate([a.reshape(-1) for a in arrays])
    total = _round_up(flat.shape[0], row_quantum * width)
    flat = jnp.pad(flat, (0, total - flat.shape[0]))
    return flat.reshape(-1, width)


def _unpack_rows(packed, shapes):
    flat = packed.reshape(-1)
    out = []
    off = 0
    for shp in shapes:
        size = 1
        for s in shp:
            size *= s
        out.append(flat[off:off + size].reshape(shp))
        off += size
    return out


def _block_diag(w):
    h, hb, _ = w.shape
    per = BD // hb
    w4 = w.reshape(h // per, per, hb, hb)
    eye = jnp.eye(per, dtype=w.dtype)
    return jnp.einsum('npij,pq->npiqj', w4, eye).reshape(h // per, BD, BD)


def _block_diag_extract(bd, hb):
    nbk = bd.shape[0]
    per = BD // hb
    b5 = bd.reshape(nbk, per, hb, per, hb)
    eye = jnp.eye(per, dtype=bd.dtype)
    return jnp.einsum('npiqj,pq->npij', b5, eye).reshape(nbk * per, hb, hb)


def kernel(x, meta_tokens, ffn1_pre_g, ffn1_w_gate, ffn1_w_up, ffn1_w_down, ffn1_post_g, mix_pre_g, w_in, lru_conv_w, lru_conv_b, lru_w_a, lru_b_a, lru_w_x, lru_b_x, lru_lambda, sconv_w, lru_out_g, sconv_out_g, w_out, mix_post_g, ffn2_pre_g, ffn2_w_gate, ffn2_w_up, ffn2_w_down, ffn2_post_g, loss_target, m_meta_tokens, m_ffn1_pre_g, m_ffn1_w_gate, m_ffn1_w_up, m_ffn1_w_down, m_ffn1_post_g, m_mix_pre_g, m_w_in, m_lru_conv_w, m_lru_conv_b, m_lru_w_a, m_lru_b_a, m_lru_w_x, m_lru_b_x, m_lru_lambda, m_sconv_w, m_lru_out_g, m_sconv_out_g, m_w_out, m_mix_post_g, m_ffn2_pre_g, m_ffn2_w_gate, m_ffn2_w_up, m_ffn2_w_down, m_ffn2_post_g, v_meta_tokens, v_ffn1_pre_g, v_ffn1_w_gate, v_ffn1_w_up, v_ffn1_w_down, v_ffn1_post_g, v_mix_pre_g, v_w_in, v_lru_conv_w, v_lru_conv_b, v_lru_w_a, v_lru_b_a, v_lru_w_x, v_lru_b_x, v_lru_lambda, v_sconv_w, v_lru_out_g, v_sconv_out_g, v_w_out, v_mix_post_g, v_ffn2_pre_g, v_ffn2_w_gate, v_ffn2_w_up, v_ffn2_w_down, v_ffn2_post_g):
    given = dict(locals())
    wts = {n: given[n] for n in WEIGHT_NAMES}
    mom = {n: given["m_" + n] for n in WEIGHT_NAMES}
    var = {n: given["v_" + n] for n in WEIGHT_NAMES}

    xi, yi, ci = _position()
    me = _block_of(xi, yi, ci)
    x2 = x[0]
    seq, d = x2.shape
    n_meta = meta_tokens.shape[0]
    m_rows = _round_up(n_meta + seq, ROW_ALIGN)
    pad = m_rows - n_meta - seq
    lead = pad + n_meta
    c = lru_conv_b.shape[1]
    hb = lru_w_a.shape[-1]
    dm = meta_tokens.shape[1]
    cs_ = lru_conv_w.shape[2]
    kw4, kw3 = lru_conv_w.shape[1], sconv_w.shape[1]
    assert d == 2 * c and c % BD == 0 and BD % hb == 0 and cs_ <= dm and kw4 == 4 and kw3 == 3

    small = jnp.zeros((_round_up(n_meta + kw4 + kw3, 8), dm), F32)
    small = small.at[0:n_meta].set(meta_tokens)
    small = small.at[n_meta:n_meta + kw4, 0:cs_].set(lru_conv_w[0])
    small = small.at[n_meta + kw4:n_meta + kw4 + kw3, 0:cs_].set(sconv_w[0])
    sr = small.shape[0]
    small_all = _gather_small(small, False, "gather_small").reshape(N_DEV, sr, dm)
    meta_full = small_all[:, 0:n_meta, :].transpose(1, 0, 2).reshape(n_meta, d)
    conv_w_full = small_all[:, n_meta:n_meta + kw4, 0:cs_].transpose(1, 0, 2).reshape(kw4, c)
    sconv_w_full = small_all[:, n_meta + kw4:n_meta + kw4 + kw3, 0:cs_].transpose(1, 0, 2).reshape(kw3, c)

    big = ['ffn1_w_gate', 'ffn1_w_up', 'ffn1_w_down', 'w_in', 'w_out', 'ffn2_w_gate', 'ffn2_w_up', 'ffn2_w_down']
    col_sharded = {'ffn1_w_gate', 'ffn1_w_up', 'w_in', 'ffn2_w_gate', 'ffn2_w_up'}
    shards = []
    for nme in big:
        w = wts[nme][0].astype(WIRE_DTYPE)
        shards.append(w.T if nme in col_sharded else w)
    shard_rows = dict(zip(big, [s.shape[0] for s in shards]))
    zeros = jnp.zeros((F_ALIGN, d), WIRE_DTYPE)

    def gather(forward_at, *names, part=None, into=None):
        sel = [shards[big.index(nme)] for nme in names]
        padded = [_round_up(N_DEV * shard_rows[nme], LANE if nme in ('w_in', 'w_out') else F_ALIGN) for nme in names]
        return _CarriedGather(sel, padded, zeros, forward_at, part, into)

    pv = jnp.zeros((16, c), F32)
    pv = pv.at[0:4].set(conv_w_full).at[4].set(lru_conv_b[0]).at[5].set(lru_b_a[0]).at[6].set(lru_b_x[0])
    pv = pv.at[7].set(lru_lambda[0]).at[8:11].set(sconv_w_full).at[11].set(lru_out_g[0]).at[12].set(sconv_out_g[0])
    wa_bd = _block_diag(lru_w_a[0]).astype(MXU_DTYPE)
    wx_bd = _block_diag(lru_w_x[0]).astype(MXU_DTYPE)
    gs = c // N_GROUPS
    gidx = jnp.arange(BD) // gs
    gm = jnp.where(gidx[:, None] == gidx[None, :], 1.0 / gs, 0.0).astype(MXU_DTYPE)

    ride = gather(0.3, 'ffn1_w_gate')
    h0, n1, target = _embed(x2, meta_full, loss_target[0], ffn1_pre_g, pad, "embed_prenorm", carried=[ride])
    (wg1,) = ride.results
    ride = gather(0.6, 'ffn1_w_up')
    g1 = _mm_nt(n1, wg1, "ffn1_gate", carried=[ride], out_dtype=MXU_DTYPE)
    (wu1,) = ride.results
    ride = gather(0.6, 'ffn1_w_down')
    u1, a1 = _ffn_up_act(n1, wu1, g1, "ffn1_up_act", carried=[ride])
    (wd1,) = ride.results
    ride = gather(0.75, 'w_in', 'w_out')
    fo1, h1, un = _mm_residual_norm(a1, wd1, h0, ffn1_post_g, 0.5, mix_pre_g, "ffn1_down", carried=[ride])
    win_t, wout = ride.results
    s2 = shard_rows['ffn2_w_gate']
    quarter = _round_up(s2 // 4, SUBLANE_BF16)
    ride_g = gather(0.5, 'ffn2_w_gate', part=(0, 3 * quarter))
    z = _mm_nt(un, win_t, "mix_in_proj", carried=[ride_g])
    ride_g = gather(0.5, 'ffn2_w_gate', part=(3 * quarter, s2 - 3 * quarter), into=ride_g.results[0])
    ride_u = gather(0.5, 'ffn2_w_up', part=(0, quarter))
    mixed, hs = _mixer_fwd(z, pv, wa_bd, wx_bd, gm, pad, "mixer_fwd", carried=[ride_g, ride_u])
    (wg2,) = ride_g.results
    ride_u = gather(0.5, 'ffn2_w_up', part=(quarter, s2 - quarter), into=ride_u.results[0])
    o_mix, h2, n2 = _mm_residual_norm(mixed, wout, h1, mix_post_g, 1.0, ffn2_pre_g, "mix_out_proj", carried=[ride_u])
    (wu2,) = ride_u.results
    ride = gather(0.75, 'ffn2_w_down')
    g2, u2, a2 = _ffn_gate_up(n2, wg2, wu2, "ffn2_gate_up", carried=[ride])
    (wd2,) = ride.results
    dh3, dfo2, d_post2, loss_part = _mm_residual_loss(a2, wd2, h2, ffn2_post_g, 0.5, target, lead, "ffn2_down_loss")
    loss = lax.psum(loss_part[0, 0], ("x", "y", "c"))

    chip_rel = [2 * (xi ^ (r >> 1)) + (yi ^ (r & 1)) for r in range(4)]
    where = jnp.stack([2 * k + ci for k in chip_rel] + chip_rel).astype(jnp.int32)
    red = {}

    def reduction(nme, grad):
        red[nme] = _GradReduction(nme, grad, shard_rows[nme], where)
        return red[nme]

    r_wd2 = reduction('ffn2_w_down', _mm_tn(a2, dfo2, "ffn2_dw_down"))
    dg2, du2 = _ffn_hidden_bwd(dfo2, wd2, g2, u2, "ffn2_hidden_bwd", carried=[r_wd2.swap()])
    r_wg2 = reduction('ffn2_w_gate', _mm_tn(dg2, n2, "ffn2_dw_gate", carried=[r_wd2.exchange(part=0)]))
    r_wu2 = reduction('ffn2_w_up', _mm_tn(du2, n2, "ffn2_dw_up", carried=[r_wd2.exchange(part=1), r_wg2.swap()]))
    dh2, d_pre2 = _mm_norm_bwd([(dg2, wg2), (du2, wu2)], h2, ffn2_pre_g, dh3, "ffn2_dx",
                               carried=[r_wg2.exchange(), r_wu2.swap()])
    do_mix, d_mix_post, dmixed = _norm_bwd_mm_nt(o_mix, mix_post_g, dh2, 1.0, wout, "mix_out_proj_bwd")
    r_wout = reduction('w_out', _mm_tn(mixed, do_mix, "mix_dw_out"))
    dz, dpv, dwa_bd, dwx_bd = _mixer_bwd(z, hs, dmixed, pv, wa_bd, wx_bd, gm, pad, "mixer_bwd",
                                         carried=[r_wu2.exchange(), r_wout.swap()])
    r_win = reduction('w_in', _mm_tn(dz, un, "mix_dw_in", carried=[r_wout.exchange()]))
    dh1, d_mix_pre, dfo1, d_post1 = _mm_norm_bwd([(dz, win_t)], h1, mix_pre_g, dh2, "mix_dx", carried=[r_win.swap()],
                                                 post=(fo1, ffn1_post_g, 0.5))
    r_wd1 = reduction('ffn1_w_down', _mm_tn(a1, dfo1, "ffn1_dw_down", carried=[r_win.exchange(part=0)]))
    early_names = ['mix_pre_g', 'mix_post_g', 'ffn2_pre_g', 'ffn2_post_g', 'ffn1_post_g',
                   'lru_conv_b', 'lru_b_a', 'lru_b_x', 'lru_lambda', 'lru_out_g', 'sconv_out_g',
                   'lru_conv_w', 'sconv_w', 'lru_w_a', 'lru_w_x']
    early_parts = [d_mix_pre, d_mix_post, d_pre2, d_post2, d_post1,
                   dpv[4:5], dpv[5:6], dpv[6:7], dpv[7:8], dpv[11:12], dpv[12:13],
                   dpv[0:4], dpv[8:11], _block_diag_extract(dwa_bd, hb), _block_diag_extract(dwx_bd, hb)]
    early_packed = _pack_rows(early_parts, d, SUBLANE_BF16)
    early_ride = _CarriedGather([early_packed], [N_DEV * early_packed.shape[0]], zeros, 0.75)
    dg1, du1 = _ffn_hidden_bwd(dfo1, wd1, g1, u1, "ffn1_hidden_bwd",
                               carried=[r_win.exchange(part=1), r_wd1.swap(), early_ride])
    early_sum = _sum_stack(early_ride.results[0], "sum_small_early")
    r_wg1 = reduction('ffn1_w_gate', _mm_tn(dg1, n1, "ffn1_dw_gate", carried=[r_wd1.exchange(part=0)]))
    r_wu1 = reduction('ffn1_w_up', _mm_tn(du1, n1, "ffn1_dw_up", carried=[r_wd1.exchange(part=1), r_wg1.swap()]))
    row_tile = _norm_bwd_row_tile(m_rows)
    n_tiles = m_rows // row_tile
    half = n_tiles // 2
    assert half >= 1 and half * row_tile >= lead
    dh0_a, d_pre1_a = _mm_norm_bwd([(dg1, wg1), (du1, wu1)], h0, ffn1_pre_g, dh1, "ffn1_dx_a",
                                   carried=[r_wg1.exchange(), r_wu1.swap()], row_tiles=(0, half))
    dh0_b, d_pre1 = _mm_norm_bwd([(dg1, wg1), (du1, wu1)], h0, ffn1_pre_g, dh1, "ffn1_dx_b",
                                 carried=[r_wu1.exchange()], row_tiles=(half, n_tiles - half), dg_init=d_pre1_a)
    grad_x = jnp.concatenate([dh0_a[lead:], dh0_b], axis=0)[None]
    d_meta = dh0_a[pad:lead]

    grads, delta, new_m, new_v = {}, {}, {}, {}
    for nme in big:
        in_shard_layout = nme not in col_sharded or shard_rows[nme] % LANE != 0
        if in_shard_layout:
            view = (lambda t: t[0].T) if nme in col_sharded else (lambda t: t[0])
            back = (lambda t: t.T[None]) if nme in col_sharded else (lambda t: t[None])
            outs = red[nme].total_and_update(view(wts[nme]), view(mom[nme]), view(var[nme]))
            grads[nme], delta[nme], new_m[nme], new_v[nme] = [back(t) for t in outs]
        else:
            grads[nme] = red[nme].total().T[None]
            outs = _adamw(wts[nme][0], grads[nme][0], mom[nme][0], var[nme][0], "adamw_" + nme)
            delta[nme], new_m[nme], new_v[nme] = [t[None] for t in outs]

    late_names = ['ffn1_pre_g', 'meta_tokens']
    late_parts = [d_pre1, d_meta]
    late_sum = _gather_small(_pack_rows(late_parts, d), True, "reduce_small_late")
    small_sums = (_unpack_rows(early_sum, [p.shape for p in early_parts])
                  + _unpack_rows(late_sum, [p.shape for p in late_parts]))
    for nme, gsm in zip(early_names + late_names, small_sums):
        if nme == 'meta_tokens':
            grads[nme] = lax.dynamic_slice_in_dim(gsm, me * dm, dm, axis=1)
        elif nme in ('lru_conv_w', 'sconv_w'):
            grads[nme] = lax.dynamic_slice_in_dim(gsm, me * cs_, cs_, axis=1)[None]
        else:
            grads[nme] = gsm.reshape(wts[nme].shape)

    rest = [n for n in WEIGHT_NAMES if n not in big]
    rest_shapes = [wts[n].shape for n in rest]
    packed = [_pack_rows([src[n] for n in rest], LANE, 256) for src in (wts, grads, mom, var)]
    for out, packed_out in zip((delta, new_m, new_v), _adamw(*packed, "adamw_small")):
        for nme, arr in zip(rest, _unpack_rows(packed_out, rest_shapes)):
            out[nme] = arr

    return (loss, grad_x, *[grads[n] for n in WEIGHT_NAMES], *[delta[n] for n in WEIGHT_NAMES],
            *[new_m[n] for n in WEIGHT_NAMES], *[new_v[n] for n in WEIGHT_NAMES])
```

```python
import functools

import jax
import jax.numpy as jnp
from jax import lax
from jax.experimental import pallas as pl
from jax.experimental.pallas import tpu as pltpu

F32 = jnp.float32
MXU_DTYPE = jnp.bfloat16
WIRE_DTYPE = jnp.bfloat16
MESH = pl.DeviceIdType.MESH

EPS = 1e-6
LRU_C = 8.0
N_GROUPS = 16
ADAM_LR = 0.001
ADAM_B1 = 0.9
ADAM_B2 = 0.999
ADAM_EPS = 1e-08
ADAM_WD = 0.01
ADAM_STEP = 10

N_DEV = 8
LANE = 128
SUBLANE_BF16 = 16
ROW_ALIGN = 128
F_ALIGN = 512
BD = 256
K_TILE = 512
ACC_ROWS = 528
ACC_GROUP = 1
MIX_ROWS = 128
VMEM_LIMIT_MB = 56

WEIGHT_NAMES = ['meta_tokens', 'ffn1_pre_g', 'ffn1_w_gate', 'ffn1_w_up', 'ffn1_w_down', 'ffn1_post_g',
                'mix_pre_g', 'w_in', 'lru_conv_w', 'lru_conv_b', 'lru_w_a', 'lru_b_a', 'lru_w_x', 'lru_b_x',
                'lru_lambda', 'sconv_w', 'lru_out_g', 'sconv_out_g', 'w_out', 'mix_post_g', 'ffn2_pre_g',
                'ffn2_w_gate', 'ffn2_w_up', 'ffn2_w_down', 'ffn2_post_g']


def _round_up(n, q):
    return (n + q - 1) // q * q


def _tile(n, target, q):
    best = None
    t = q
    while t <= min(n, target):
        if n % t == 0:
            best = t
        t += q
    assert best is not None, (n, target, q)
    return best


def _params(**kw):
    return pltpu.CompilerParams(vmem_limit_bytes=VMEM_LIMIT_MB << 20, **kw)


def _call(body, *, grid, in_specs, out_specs, out_shape, name, args, scratch_shapes=(), carried=(), prefetch=()):
    carried = list(carried)
    n_pf = len(prefetch)

    def launch(fn, in_specs_, out_specs_, out_shape_, scratch_, operands, aliases_):
        if n_pf:
            spec = pltpu.PrefetchScalarGridSpec(num_scalar_prefetch=n_pf, grid=grid, in_specs=in_specs_,
                                                out_specs=out_specs_, scratch_shapes=scratch_)
            return pl.pallas_call(fn, grid_spec=spec, out_shape=out_shape_, input_output_aliases=aliases_,
                                  name=name, compiler_params=_params())(*prefetch, *operands)
        return pl.pallas_call(fn, grid=grid, in_specs=in_specs_, out_specs=out_specs_, out_shape=out_shape_,
                              scratch_shapes=scratch_, input_output_aliases=aliases_, name=name,
                              compiler_params=_params())(*operands)

    if not carried:
        return launch(body, in_specs, out_specs, out_shape, list(scratch_shapes), args, {})
    single = not isinstance(out_shape, (list, tuple))
    out_specs_l = [out_specs] if single else list(out_specs)
    out_shape_l = [out_shape] if single else list(out_shape)
    n_in, n_out, n_scr = len(in_specs), len(out_specs_l), len(scratch_shapes)
    hbm = pl.BlockSpec(memory_space=pl.ANY)
    c_in = [a for cm in carried for a in cm.arrays]
    c_out = [s for cm in carried for s in cm.out_shapes]
    c_scr = []
    aliases = {}
    in_off, out_off = n_pf + n_in, n_out
    for cm in carried:
        c_scr += [pltpu.SemaphoreType.DMA((cm.n_remote,)), pltpu.SemaphoreType.DMA((cm.n_remote,)),
                  pltpu.SemaphoreType.DMA((max(cm.n_local, 1),))]
        for k, v in cm.aliases.items():
            aliases[in_off + k] = out_off + v
        in_off += len(cm.arrays)
        out_off += len(cm.out_shapes)
    steps = 1
    for g in grid:
        steps *= g
    forward_steps = [min(int(cm.forward_at * steps), steps - 1) for cm in carried]

    def wrapped(*refs):
        pf = refs[:n_pf]
        p = n_pf
        ins = refs[p:p + n_in]
        p += n_in
        cins = refs[p:p + len(c_in)]
        p += len(c_in)
        outs = refs[p:p + n_out]
        p += n_out
        couts = refs[p:p + len(c_out)]
        p += len(c_out)
        scr = refs[p:p + n_scr]
        csem = refs[p + n_scr:]
        lin = 0
        for axis, g in enumerate(grid):
            lin = lin * g + pl.program_id(axis)
        views = []
        io = oo = 0
        for j, cm in enumerate(carried):
            views.append((cins[io:io + len(cm.arrays)], couts[oo:oo + len(cm.out_shapes)],
                          csem[3 * j], csem[3 * j + 1], csem[3 * j + 2]))
            io += len(cm.arrays)
            oo += len(cm.out_shapes)

        @pl.when(lin == 0)
        def _():
            for cm, v in zip(carried, views):
                cm.start(*v)

        body(*pf, *ins, *outs, *scr)

        for cm, v, step in zip(carried, views, forward_steps):
            pl.when(lin == step)(functools.partial(cm.forward, *v))

        @pl.when(lin == steps - 1)
        def _():
            for cm, v in zip(carried, views):
                cm.finish(*v)

    res = launch(wrapped, list(in_specs) + [hbm] * len(c_in), out_specs_l + [hbm] * len(c_out),
                 out_shape_l + c_out, list(scratch_shapes) + c_scr, (*args, *c_in), aliases)
    oo = n_out
    for cm in carried:
        cm.results = list(res[oo:oo + len(cm.out_shapes)])
        oo += len(cm.out_shapes)
    return res[0] if single else list(res[:n_out])


def _embed(x, meta, target, g, pad, name, carried=()):
    seq, d = x.shape
    n_meta = meta.shape[0]
    lead = pad + n_meta
    m = lead + seq
    tr = ROW_ALIGN
    lead_blocks = lead // tr
    meta_row = pad - (lead_blocks - 1) * tr
    assert lead % tr == 0 and seq % tr == 0 and 0 <= meta_row and meta_row % 8 == 0

    def body(x_ref, meta_ref, t_ref, g_ref, h_ref, n_ref, tp_ref):
        i = pl.program_id(0)

        @pl.when(i < lead_blocks)
        def _():
            h_ref[...] = jnp.zeros_like(h_ref)
            tp_ref[...] = jnp.zeros_like(tp_ref)

        @pl.when(i == lead_blocks - 1)
        def _():
            h_ref[pl.ds(meta_row, n_meta), :] = meta_ref[...]

        @pl.when(i >= lead_blocks)
        def _():
            h_ref[...] = x_ref[...]
            tp_ref[...] = t_ref[...]

        h = h_ref[...]
        r = lax.rsqrt(jnp.mean(h * h, axis=-1, keepdims=True) + EPS)
        n_ref[...] = (h * r * g_ref[...]).astype(n_ref.dtype)

    tokens = pl.BlockSpec((tr, d), lambda i: (jnp.maximum(i - lead_blocks, 0), 0))
    rows = pl.BlockSpec((tr, d), lambda i: (i, 0))
    return _call(
        body, grid=(m // tr,),
        in_specs=[tokens, pl.BlockSpec((n_meta, d), lambda i: (0, 0)), tokens, pl.BlockSpec((1, d), lambda i: (0, 0))],
        out_specs=[rows, rows, rows],
        out_shape=[jax.ShapeDtypeStruct((m, d), F32), jax.ShapeDtypeStruct((m, d), MXU_DTYPE),
                   jax.ShapeDtypeStruct((m, d), F32)],
        name=name, args=(x, meta, target, g), carried=carried)


def _rmsnorm_bwd_rows(x, g, dy):
    r = lax.rsqrt(jnp.mean(x * x, axis=-1, keepdims=True) + EPS)
    xh = x * r
    dyh = dy * g
    dx = r * (dyh - xh * jnp.mean(dyh * xh, axis=-1, keepdims=True))
    return dx, dy * xh


def _dot_nt(a, b):
    return lax.dot_general(a, b, (((1,), (1,)), ((), ())), preferred_element_type=F32)


def _dot_tn(a, b):
    return lax.dot_general(a, b, (((0,), (0,)), ((), ())), preferred_element_type=F32)


def _mm_nt(a, w, name, carried=(), out_dtype=F32):
    m, k = a.shape
    n = w.shape[0]
    tm = _tile(m, 1056, SUBLANE_BF16)
    tn = _tile(n, 512, LANE)

    def body(a_ref, w_ref, o_ref):
        o_ref[...] = _dot_nt(a_ref[...], w_ref[...]).astype(o_ref.dtype)

    return _call(
        body, grid=(m // tm, n // tn),
        in_specs=[pl.BlockSpec((tm, k), lambda i, j: (i, 0)), pl.BlockSpec((tn, k), lambda i, j: (j, 0))],
        out_specs=pl.BlockSpec((tm, tn), lambda i, j: (i, j)),
        out_shape=jax.ShapeDtypeStruct((m, n), out_dtype), name=name, args=(a, w), carried=carried)


def _norm_bwd_mm_nt(x, g, dy, scale, w, name, carried=()):
    m, d = x.shape
    n = w.shape[0]
    tm = _tile(m, 528, SUBLANE_BF16)

    def body(x_ref, g_ref, dy_ref, w_ref, dx_ref, dg_ref, o_ref):
        @pl.when(pl.program_id(0) == 0)
        def _():
            dg_ref[...] = jnp.zeros_like(dg_ref)

        dx, dgr = _rmsnorm_bwd_rows(x_ref[...], g_ref[...], scale * dy_ref[...])
        dxb = dx.astype(dx_ref.dtype)
        dx_ref[...] = dxb
        dg_ref[...] += jnp.sum(dgr, axis=0, keepdims=True)
        o_ref[...] = _dot_nt(dxb, w_ref[...])

    row = pl.BlockSpec((tm, d), lambda i: (i, 0))
    vec = pl.BlockSpec((1, d), lambda i: (0, 0))
    return _call(
        body, grid=(m // tm,),
        in_specs=[row, vec, row, pl.BlockSpec((n, d), lambda i: (0, 0), pipeline_mode=pl.Buffered(1))],
        out_specs=[row, vec, pl.BlockSpec((tm, n), lambda i: (i, 0))],
        out_shape=[jax.ShapeDtypeStruct((m, d), MXU_DTYPE), jax.ShapeDtypeStruct((1, d), F32),
                   jax.ShapeDtypeStruct((m, n), F32)],
        name=name, args=(x, g, dy, w), carried=carried)


def _ffn_up_act(n_act, wu_t, g_act, name, carried=()):
    m, d = n_act.shape
    fp = wu_t.shape[0]
    tm = _tile(m, 1056, SUBLANE_BF16)
    tn = _tile(fp, 512, LANE)

    def body(n_ref, wu_ref, g_ref, u_ref, a_ref):
        u = _dot_nt(n_ref[...], wu_ref[...])
        g = g_ref[...].astype(F32)
        u_ref[...] = u.astype(u_ref.dtype)
        a_ref[...] = (g * jax.nn.sigmoid(g) * u).astype(a_ref.dtype)

    act = pl.BlockSpec((tm, tn), lambda i, j: (i, j))
    return _call(
        body, grid=(m // tm, fp // tn),
        in_specs=[pl.BlockSpec((tm, d), lambda i, j: (i, 0)), pl.BlockSpec((tn, d), lambda i, j: (j, 0)), act],
        out_specs=[act, act],
        out_shape=[jax.ShapeDtypeStruct((m, fp), MXU_DTYPE)] * 2, name=name, args=(n_act, wu_t, g_act), carried=carried)


def _ffn_gate_up(n_act, wg_t, wu_t, name, carried=()):
    m, d = n_act.shape
    fp = wg_t.shape[0]
    tm = _tile(m, 1056, SUBLANE_BF16)
    tn = _tile(fp, 512, LANE)

    def body(n_ref, wg_ref, wu_ref, g_ref, u_ref, a_ref):
        n = n_ref[...]
        g = _dot_nt(n, wg_ref[...])
        u = _dot_nt(n, wu_ref[...])
        g_ref[...] = g.astype(g_ref.dtype)
        u_ref[...] = u.astype(u_ref.dtype)
        a_ref[...] = (g * jax.nn.sigmoid(g) * u).astype(a_ref.dtype)

    act = pl.BlockSpec((tm, tn), lambda i, j: (i, j))
    wsp = pl.BlockSpec((tn, d), lambda i, j: (j, 0))
    return _call(
        body, grid=(m // tm, fp // tn),
        in_specs=[pl.BlockSpec((tm, d), lambda i, j: (i, 0)), wsp, wsp],
        out_specs=[act, act, act],
        out_shape=[jax.ShapeDtypeStruct((m, fp), MXU_DTYPE)] * 3, name=name, args=(n_act, wg_t, wu_t), carried=carried)


def _ffn_hidden_bwd(dfo, wd, g_act, u_act, name, carried=()):
    m, d = dfo.shape
    fp = wd.shape[0]
    tm = _tile(m, 1056, SUBLANE_BF16)
    tn = _tile(fp, 512, LANE)

    def body(df_ref, wd_ref, g_ref, u_ref, dg_ref, du_ref):
        da = _dot_nt(df_ref[...], wd_ref[...]).astype(dg_ref.dtype)
        g = g_ref[...]
        u = u_ref[...]
        s = jax.nn.sigmoid(g)
        du_ref[...] = da * (g * s)
        dg_ref[...] = da * (u * (s * (1.0 + g * (1.0 - s))))

    act = pl.BlockSpec((tm, tn), lambda i, j: (i, j))
    return _call(
        body, grid=(m // tm, fp // tn),
        in_specs=[pl.BlockSpec((tm, d), lambda i, j: (i, 0)), pl.BlockSpec((tn, d), lambda i, j: (j, 0)), act, act],
        out_specs=[act, act],
        out_shape=[jax.ShapeDtypeStruct((m, fp), MXU_DTYPE)] * 2, name=name, args=(dfo, wd, g_act, u_act),
        carried=carried)


def _row_groups(n_tiles, max_group, nk):
    gsz = max(q for q in range(1, max_group + 1) if n_tiles % q == 0)

    def epilogue_row(grp, kk, i):
        return grp * gsz + jnp.where(kk == nk - 1, i, 0)

    return gsz, epilogue_row


def _mm_residual_norm(a, w, h, g, scale, next_g, name, carried=()):
    m, k = a.shape
    d = w.shape[1]
    tm = _tile(m, ACC_ROWS, SUBLANE_BF16)
    tk = _tile(k, K_TILE, LANE)
    nk = k // tk
    gsz, epilogue_row = _row_groups(m // tm, ACC_GROUP, nk)

    def body(a_ref, w_ref, h_ref, g_ref, ng_ref, fo_ref, hn_ref, nn_ref, acc_ref):
        kk, i = pl.program_id(1), pl.program_id(2)

        @pl.when(kk == 0)
        def _():
            acc_ref[i] = jnp.zeros((tm, d), F32)

        acc_ref[i] += jnp.dot(a_ref[...], w_ref[...], preferred_element_type=F32)

        @pl.when(kk == nk - 1)
        def _():
            fo = acc_ref[i]
            fo_ref[...] = fo
            r = lax.rsqrt(jnp.mean(fo * fo, axis=-1, keepdims=True) + EPS)
            hn = h_ref[...] + scale * (fo * r * g_ref[...])
            hn_ref[...] = hn
            rn = lax.rsqrt(jnp.mean(hn * hn, axis=-1, keepdims=True) + EPS)
            nn_ref[...] = (hn * rn * ng_ref[...]).astype(nn_ref.dtype)

    row = pl.BlockSpec((tm, d), lambda grp, kk, i: (epilogue_row(grp, kk, i), 0))
    row_once = pl.BlockSpec((tm, d), lambda grp, kk, i: (epilogue_row(grp, kk, i), 0), pipeline_mode=pl.Buffered(1))
    vec = pl.BlockSpec((1, d), lambda grp, kk, i: (0, 0))
    return _call(
        body, grid=(m // tm // gsz, nk, gsz),
        in_specs=[pl.BlockSpec((tm, tk), lambda grp, kk, i: (grp * gsz + i, kk)),
                  pl.BlockSpec((tk, d), lambda grp, kk, i: (kk, 0)), row_once, vec, vec],
        out_specs=[row, row, row],
        out_shape=[jax.ShapeDtypeStruct((m, d), F32)] * 2 + [jax.ShapeDtypeStruct((m, d), MXU_DTYPE)],
        scratch_shapes=[pltpu.VMEM((gsz, tm, d), F32)], name=name, args=(a, w, h, g, next_g), carried=carried)


def _mm_residual_loss(a, w, h, g, scale, target, lead, name, carried=()):
    m, k = a.shape
    d = w.shape[1]
    tm = _tile(m, ACC_ROWS, SUBLANE_BF16)
    tk = _tile(k, K_TILE, LANE)
    nk = k // tk
    gsz, epilogue_row = _row_groups(m // tm, ACC_GROUP, nk)

    def body(a_ref, w_ref, h_ref, g_ref, t_ref, dy_ref, dfo_ref, dg_ref, l_ref, acc_ref):
        grp, kk, i = pl.program_id(0), pl.program_id(1), pl.program_id(2)

        @pl.when(jnp.logical_and(jnp.logical_and(grp == 0, kk == 0), i == 0))
        def _():
            dg_ref[...] = jnp.zeros_like(dg_ref)
            l_ref[...] = jnp.zeros_like(l_ref)

        @pl.when(kk == 0)
        def _():
            acc_ref[i] = jnp.zeros((tm, d), F32)

        acc_ref[i] += jnp.dot(a_ref[...], w_ref[...], preferred_element_type=F32)

        @pl.when(kk == nk - 1)
        def _():
            fo = acc_ref[i]
            gain = g_ref[...]
            r = lax.rsqrt(jnp.mean(fo * fo, axis=-1, keepdims=True) + EPS)
            xh = fo * r
            y = h_ref[...] + scale * (xh * gain)
            row = (grp * gsz + i) * tm + lax.broadcasted_iota(jnp.int32, (tm, 1), 0)
            e = jnp.where(row >= lead, y - t_ref[...], 0.0)
            dy = e * (1.0 / d)
            dy_ref[...] = dy
            l_ref[...] += 0.5 * jnp.sum(jnp.sum(e * e, axis=-1, keepdims=True) * (1.0 / d), axis=0, keepdims=True)
            dn = scale * dy
            dyh = dn * gain
            dfo_ref[...] = (r * (dyh - xh * jnp.mean(dyh * xh, axis=-1, keepdims=True))).astype(dfo_ref.dtype)
            dg_ref[...] += jnp.sum(dn * xh, axis=0, keepdims=True)

    row = pl.BlockSpec((tm, d), lambda grp, kk, i: (epilogue_row(grp, kk, i), 0))
    row_once = pl.BlockSpec((tm, d), lambda grp, kk, i: (epilogue_row(grp, kk, i), 0), pipeline_mode=pl.Buffered(1))
    vec = pl.BlockSpec((1, d), lambda grp, kk, i: (0, 0))
    return _call(
        body, grid=(m // tm // gsz, nk, gsz),
        in_specs=[pl.BlockSpec((tm, tk), lambda grp, kk, i: (grp * gsz + i, kk)),
                  pl.BlockSpec((tk, d), lambda grp, kk, i: (kk, 0)), row_once, vec, row_once],
        out_specs=[row, row, vec, pl.BlockSpec((1, 1), lambda grp, kk, i: (0, 0))],
        out_shape=[jax.ShapeDtypeStruct((m, d), F32), jax.ShapeDtypeStruct((m, d), MXU_DTYPE),
                   jax.ShapeDtypeStruct((1, d), F32), jax.ShapeDtypeStruct((1, 1), F32)],
        scratch_shapes=[pltpu.VMEM((gsz, tm, d), F32)], name=name, args=(a, w, h, g, target), carried=carried)


def _norm_bwd_row_tile(m):
    return _tile(m, ACC_ROWS, SUBLANE_BF16)


def _mm_norm_bwd(pairs, h, g, dh_up, name, carried=(), row_tiles=None, dg_init=None, post=None):
    n_pairs = len(pairs)
    m, k = pairs[0][0].shape
    d = h.shape[1]
    tm = _norm_bwd_row_tile(m)
    tk = _tile(k, K_TILE, LANE)
    nk = k // tk
    t0, nt = row_tiles if row_tiles is not None else (0, m // tm)
    gsz, epilogue_row = _row_groups(nt, ACC_GROUP, nk)
    if dg_init is None:
        dg_init = jnp.zeros((1, d), F32)

    n_post = 0 if post is None else 2

    def body(*refs):
        ops = refs[:2 * n_pairs]
        h_ref, g_ref, up_ref, init_ref = refs[2 * n_pairs:2 * n_pairs + 4]
        post_in = refs[2 * n_pairs + 4:2 * n_pairs + 4 + n_post]
        dh_ref, dg_ref = refs[2 * n_pairs + 4 + n_post:2 * n_pairs + 6 + n_post]
        post_out = refs[2 * n_pairs + 6 + n_post:2 * n_pairs + 6 + 2 * n_post]
        acc_ref = refs[-1]
        grp, kk, i = pl.program_id(0), pl.program_id(1), pl.program_id(2)

        @pl.when(jnp.logical_and(jnp.logical_and(grp == 0, kk == 0), i == 0))
        def _():
            dg_ref[...] = init_ref[...]
            if post is not None:
                post_out[1][...] = jnp.zeros_like(post_out[1])

        @pl.when(kk == 0)
        def _():
            acc_ref[i] = jnp.zeros((tm, d), F32)

        for p in range(n_pairs):
            acc_ref[i] += jnp.dot(ops[2 * p][...], ops[2 * p + 1][...], preferred_element_type=F32)

        @pl.when(kk == nk - 1)
        def _():
            dx, dgr = _rmsnorm_bwd_rows(h_ref[...], g_ref[...], acc_ref[i])
            dh = up_ref[...] + dx
            dh_ref[...] = dh
            dg_ref[...] += jnp.sum(dgr, axis=0, keepdims=True)
            if post is not None:
                dfo, dpr = _rmsnorm_bwd_rows(post_in[0][...], post_in[1][...], post[2] * dh)
                post_out[0][...] = dfo.astype(post_out[0].dtype)
                post_out[1][...] += jnp.sum(dpr, axis=0, keepdims=True)

    row_in = pl.BlockSpec((tm, d), lambda grp, kk, i: (t0 + epilogue_row(grp, kk, i), 0))
    row_out = pl.BlockSpec((tm, d), lambda grp, kk, i: (epilogue_row(grp, kk, i), 0))
    vec = pl.BlockSpec((1, d), lambda grp, kk, i: (0, 0))
    in_specs = []
    args = []
    for a, w in pairs:
        in_specs += [pl.BlockSpec((tm, tk), lambda grp, kk, i: (t0 + grp * gsz + i, kk)),
                     pl.BlockSpec((tk, d), lambda grp, kk, i: (kk, 0))]
        args += [a, w]
    in_specs += [row_in, vec, row_in, vec]
    args += [h, g, dh_up, dg_init]
    out_specs = [row_out, vec]
    out_shape = [jax.ShapeDtypeStruct((nt * tm, d), F32), jax.ShapeDtypeStruct((1, d), F32)]
    if post is not None:
        in_specs += [row_in, vec]
        args += [post[0], post[1]]
        out_specs += [row_out, vec]
        out_shape += [jax.ShapeDtypeStruct((nt * tm, d), MXU_DTYPE), jax.ShapeDtypeStruct((1, d), F32)]
    return _call(
        body, grid=(nt // gsz, nk, gsz), in_specs=in_specs, out_specs=out_specs, out_shape=out_shape,
        scratch_shapes=[pltpu.VMEM((gsz, tm, d), F32)], name=name, args=tuple(args), carried=carried)


def _mm_tn(a, b, name, carried=()):
    m, ka = a.shape
    d = b.shape[1]
    tf = _tile(ka, 512, LANE)

    def body(a_ref, b_ref, o_ref):
        o_ref[...] = _dot_tn(a_ref[...], b_ref[...]).astype(o_ref.dtype)

    return _call(
        body, grid=(ka // tf,),
        in_specs=[pl.BlockSpec((m, tf), lambda j: (0, j)),
                  pl.BlockSpec((m, d), lambda j: (0, 0), pipeline_mode=pl.Buffered(1))],
        out_specs=pl.BlockSpec((tf, d), lambda j: (j, 0)),
        out_shape=jax.ShapeDtypeStruct((ka, d), WIRE_DTYPE), name=name, args=(a, b), carried=carried)


GELU_K = 0.7978845608028654
GELU_C = 0.044715


def _expm1(x):
    series = x * (1.0 + x * (1.0 / 2 + x * (1.0 / 6 + x * (1.0 / 24 + x * (1.0 / 120)))))
    return jnp.where(jnp.abs(x) < 0.1, series, jnp.exp(x) - 1.0)


def _softplus(x):
    return jnp.maximum(x, 0.0) + jnp.log1p(jnp.exp(-jnp.abs(x)))


def _block_mm(v, w_ref, transposed):
    nbk = w_ref.shape[0]
    outs = []
    for j in range(nbk):
        vj = v[:, j * BD:(j + 1) * BD]
        outs.append(_dot_nt(vj, w_ref[j]) if transposed else jnp.dot(vj, w_ref[j], preferred_element_type=F32))
    return outs[0] if nbk == 1 else jnp.concatenate(outs, axis=1)


def _group_mean(q, gm_ref):
    hi = q.astype(MXU_DTYPE)
    lo = (q - hi.astype(F32)).astype(MXU_DTYPE)
    nbk = q.shape[1] // BD
    gm = gm_ref[...]
    outs = []
    for j in range(nbk):
        sl = slice(j * BD, (j + 1) * BD)
        outs.append(jnp.dot(hi[:, sl], gm, preferred_element_type=F32) + jnp.dot(lo[:, sl], gm, preferred_element_type=F32))
    return outs[0] if nbk == 1 else jnp.concatenate(outs, axis=1)


class _RowReader:
    def __init__(self, ref):
        self.ref = ref

    def __getitem__(self, rows):
        return self.ref[rows, :]


def _shifted(ext_ref, cur, before8, after8, downs=(), ups=()):
    r = cur.shape[0]
    if downs:
        ext_ref[0:8, :] = before8
    ext_ref[8:8 + r, :] = cur
    if ups:
        ext_ref[8 + r:16 + r, :] = after8
    return [ext_ref[pl.ds(8 - j, r), :] for j in downs] + [ext_ref[pl.ds(8 + j, r), :] for j in ups]


def _lru_gates(xc, pv, wa_ref, wx_ref):
    xcb = xc.astype(MXU_DTYPE)
    ga = jax.nn.sigmoid(_block_mm(xcb, wa_ref, False) + pv[5:6])
    gx = jax.nn.sigmoid(_block_mm(xcb, wx_ref, False) + pv[6:7])
    sp = _softplus(-pv[7:8])
    log_a = -LRU_C * ga * sp
    a = jnp.exp(log_a)
    e2 = _expm1(2.0 * log_a)
    mult = jnp.sqrt(-e2)
    return xcb, ga, gx, sp, a, e2, mult


def _gelu_parts(y):
    th = jnp.tanh(GELU_K * (y + GELU_C * y * y * y))
    return 0.5 * y * (1.0 + th), th


def _scan_block(a, u, sa_ref, su_ref, carry_ref, out_ref, reverse):
    r, c = a.shape
    n = r // 8
    a3 = a.reshape(n, 8, c)
    u3 = u.reshape(n, 8, c)
    sub = lax.broadcasted_iota(jnp.int32, (n, 8, c), 1)
    for dlt in (1, 2, 4):
        keep = (sub < 8 - dlt) if reverse else (sub >= dlt)
        shift = 8 - dlt if reverse else dlt
        sh_a = pltpu.roll(a3, shift, axis=1)
        sh_u = pltpu.roll(u3, shift, axis=1)
        u3 = u3 + a3 * jnp.where(keep, sh_u, 0.0)
        a3 = a3 * jnp.where(keep, sh_a, 1.0)
    sa_ref[...] = a3.reshape(r, c)
    su_ref[...] = u3.reshape(r, c)
    for k in (range(n - 1, -1, -1) if reverse else range(n)):
        rows = pl.ds(8 * k, 8)
        out_ref[rows, :] = su_ref[rows, :] + sa_ref[rows, :] * carry_ref[...]
        carry_ref[...] = out_ref[pl.ds(8 * k if reverse else 8 * k + 7, 1), :]


def _mixer_fwd(z, pv, wa, wx, gm, pad, name, carried=()):
    m = z.shape[0]
    c = pv.shape[1]
    r = MIX_ROWS
    nb = m // r

    def body(z_ref, pv_ref, wa_ref, wx_ref, gm_ref, mixed_ref, hs_ref, ext_ref, tailx_ref, tailc_ref, carry_ref,
             sa_ref, su_ref):
        b = pl.program_id(0)

        @pl.when(b == 0)
        def _():
            tailx_ref[...] = jnp.zeros_like(tailx_ref)
            tailc_ref[...] = jnp.zeros_like(tailc_ref)
            carry_ref[...] = jnp.zeros_like(carry_ref)

        pv = _RowReader(pv_ref)
        row = b * r + lax.broadcasted_iota(jnp.int32, (r, 1), 0)
        maskf = (row >= pad).astype(F32)
        y = z_ref[:, 0:c]
        xl = z_ref[:, c:2 * c]
        bs = z_ref[:, 2 * c:3 * c]
        cv = z_ref[:, 3 * c:4 * c] * z_ref[:, 4 * c:5 * c]

        x1, x2, x3 = _shifted(ext_ref, xl, tailx_ref[...], None, downs=(1, 2, 3))
        tailx_ref[...] = z_ref[pl.ds(r - 8, 8), c:2 * c]
        xc = pv[4:5] + pv[3:4] * xl + pv[2:3] * x1 + pv[1:2] * x2 + pv[0:1] * x3
        _, _, gx, _, a, _, mult = _lru_gates(xc, pv, wa_ref, wx_ref)
        uu = mult * (gx * xc) * maskf

        _scan_block(a, uu, sa_ref, su_ref, carry_ref, hs_ref, reverse=False)
        hs = hs_ref[...]

        gelu_y, _ = _gelu_parts(y)
        lru_out = hs * gelu_y
        c1, c2 = _shifted(ext_ref, cv, tailc_ref[...], None, downs=(1, 2))
        tailc_ref[...] = cv[r - 8:r]
        sc_out = bs * (pv[10:11] * cv + pv[9:10] * c1 + pv[8:9] * c2)

        rl = lax.rsqrt(_group_mean(lru_out * lru_out, gm_ref) + EPS)
        rs = lax.rsqrt(_group_mean(sc_out * sc_out, gm_ref) + EPS)
        mixed_ref[:, 0:c] = (lru_out * rl * pv[11:12]).astype(mixed_ref.dtype)
        mixed_ref[:, c:2 * c] = (sc_out * rs * pv[12:13]).astype(mixed_ref.dtype)

    full = lambda shape: pl.BlockSpec(shape, lambda b: (0,) * len(shape))
    return _call(
        body, grid=(nb,),
        in_specs=[pl.BlockSpec((r, 5 * c), lambda b: (b, 0)), full(pv.shape), full(wa.shape), full(wx.shape), full(gm.shape)],
        out_specs=[pl.BlockSpec((r, 2 * c), lambda b: (b, 0)), pl.BlockSpec((r, c), lambda b: (b, 0))],
        out_shape=[jax.ShapeDtypeStruct((m, 2 * c), MXU_DTYPE), jax.ShapeDtypeStruct((m, c), F32)],
        scratch_shapes=[pltpu.VMEM((r + 16, c), F32), pltpu.VMEM((8, c), F32), pltpu.VMEM((8, c), F32),
                        pltpu.VMEM((1, c), F32), pltpu.VMEM((r, c), F32), pltpu.VMEM((r, c), F32)],
        name=name, args=(z, pv, wa, wx, gm), carried=carried)


def _mixer_bwd(z, hs, dmixed, pv, wa, wx, gm, pad, name, carried=()):
    m = z.shape[0]
    c = pv.shape[1]
    r = MIX_ROWS
    nb = m // r
    r8 = r // 8
    assert pad <= r and pad % SUBLANE_BF16 == 0

    def body(z_ref, zp_ref, hs_ref, hsp_ref, dm_ref, pv_ref, wa_ref, wx_ref, gm_ref,
             dz_ref, dpv_ref, dwa_ref, dwx_ref, ext_ref, hxc_ref, hsc_ref, hp_ref, pc_ref, sa_ref, su_ref, p_ref):
        i = pl.program_id(0)
        b = nb - 1 - i

        @pl.when(i == 0)
        def _():
            hxc_ref[...] = jnp.zeros_like(hxc_ref)
            hsc_ref[...] = jnp.zeros_like(hsc_ref)
            hp_ref[...] = jnp.zeros_like(hp_ref)
            pc_ref[...] = jnp.zeros_like(pc_ref)
            dpv_ref[...] = jnp.zeros_like(dpv_ref)
            dwa_ref[...] = jnp.zeros_like(dwa_ref)
            dwx_ref[...] = jnp.zeros_like(dwx_ref)

        pv = _RowReader(pv_ref)
        row = b * r + lax.broadcasted_iota(jnp.int32, (r, 1), 0)
        maskf = (row >= pad).astype(F32)
        has_prev = (b > 0).astype(F32)
        y = z_ref[:, 0:c]
        xl = z_ref[:, c:2 * c]
        bs = z_ref[:, 2 * c:3 * c]
        cs = z_ref[:, 3 * c:4 * c]
        vs = z_ref[:, 4 * c:5 * c]
        cv = cs * vs
        xl_prev = zp_ref[:, c:2 * c] * has_prev
        cv_prev = zp_ref[:, 3 * c:4 * c] * zp_ref[:, 4 * c:5 * c] * has_prev
        hs = hs_ref[...]

        x1, x2, x3 = _shifted(ext_ref, xl, xl_prev, None, downs=(1, 2, 3))
        xc = pv[4:5] + pv[3:4] * xl + pv[2:3] * x1 + pv[1:2] * x2 + pv[0:1] * x3
        xcb, ga, gx, sp, a, e2, mult = _lru_gates(xc, pv, wa_ref, wx_ref)
        gxx = gx * xc
        gelu_y, th = _gelu_parts(y)
        lru_out = hs * gelu_y
        c1, c2 = _shifted(ext_ref, cv, cv_prev, None, downs=(1, 2))
        sc = pv[10:11] * cv + pv[9:10] * c1 + pv[8:9] * c2
        sc_out = bs * sc

        def group_norm_bwd(v, dm, gain):
            rr = lax.rsqrt(_group_mean(v * v, gm_ref) + EPS)
            vh = v * rr
            dvh = dm * gain
            dv = rr * (dvh - vh * _group_mean(dvh * vh, gm_ref))
            return dv, jnp.sum(dm * vh, axis=0, keepdims=True)

        d_lru_out, d_og = group_norm_bwd(lru_out, dm_ref[:, 0:c], pv[11:12])
        d_sc_out, d_sg = group_norm_bwd(sc_out, dm_ref[:, c:2 * c], pv[12:13])
        dpv_ref[11:12, :] += d_og
        dpv_ref[12:13, :] += d_sg

        dhs = d_lru_out * gelu_y
        dgelu = 0.5 * (1.0 + th) + 0.5 * y * (1.0 - th * th) * GELU_K * (1.0 + 3.0 * GELU_C * y * y)
        dy = d_lru_out * hs * dgelu

        _scan_block(a, a * dhs, sa_ref, su_ref, pc_ref, p_ref, reverse=True)
        (p_next,) = _shifted(ext_ref, p_ref[...], None, hp_ref[...], ups=(1,))
        hp_ref[...] = p_ref[0:8, :]
        q = dhs + p_next
        (hs_prev,) = _shifted(ext_ref, hs, hsp_ref[...] * has_prev, None, downs=(1,))
        duu = q * maskf
        da = q * hs_prev

        dmult = duu * gxx
        dgxx = duu * mult
        dgx = dgxx * xc
        dxc = dgxx * gx
        dlog_a = da * a - dmult * ((1.0 + e2) / mult)
        dga = dlog_a * (-LRU_C * sp)
        dsp = jnp.sum(dlog_a * (-LRU_C * ga), axis=0, keepdims=True)
        dpv_ref[7:8, :] += dsp * (-jax.nn.sigmoid(-pv[7:8]))
        dga_pre = dga * ga * (1.0 - ga)
        dgx_pre = dgx * gx * (1.0 - gx)
        dpv_ref[5:6, :] += jnp.sum(dga_pre, axis=0, keepdims=True)
        dpv_ref[6:7, :] += jnp.sum(dgx_pre, axis=0, keepdims=True)
        dga_b = dga_pre.astype(MXU_DTYPE)
        dgx_b = dgx_pre.astype(MXU_DTYPE)
        dxc = dxc + _block_mm(dga_b, wa_ref, True) + _block_mm(dgx_b, wx_ref, True)
        for j in range(c // BD):
            sl = slice(j * BD, (j + 1) * BD)
            dwa_ref[j] += _dot_tn(xcb[:, sl], dga_b[:, sl])
            dwx_ref[j] += _dot_tn(xcb[:, sl], dgx_b[:, sl])

        dpv_ref[4:5, :] += jnp.sum(dxc, axis=0, keepdims=True)
        dpv_ref[3:4, :] += jnp.sum(dxc * xl, axis=0, keepdims=True)
        dpv_ref[2:3, :] += jnp.sum(dxc * x1, axis=0, keepdims=True)
        dpv_ref[1:2, :] += jnp.sum(dxc * x2, axis=0, keepdims=True)
        dpv_ref[0:1, :] += jnp.sum(dxc * x3, axis=0, keepdims=True)
        u1, u2, u3 = _shifted(ext_ref, dxc, None, hxc_ref[...], ups=(1, 2, 3))
        hxc_ref[...] = dxc[0:8]
        dxl = pv[3:4] * dxc + pv[2:3] * u1 + pv[1:2] * u2 + pv[0:1] * u3

        dbs = d_sc_out * sc
        dsc = d_sc_out * bs
        dpv_ref[10:11, :] += jnp.sum(dsc * cv, axis=0, keepdims=True)
        dpv_ref[9:10, :] += jnp.sum(dsc * c1, axis=0, keepdims=True)
        dpv_ref[8:9, :] += jnp.sum(dsc * c2, axis=0, keepdims=True)
        s1, s2 = _shifted(ext_ref, dsc, None, hsc_ref[...], ups=(1, 2))
        hsc_ref[...] = dsc[0:8]
        dcv = pv[10:11] * dsc + pv[9:10] * s1 + pv[8:9] * s2

        dz_ref[:, 0:c] = dy.astype(dz_ref.dtype)
        dz_ref[:, c:2 * c] = dxl.astype(dz_ref.dtype)
        dz_ref[:, 2 * c:3 * c] = dbs.astype(dz_ref.dtype)
        dz_ref[:, 3 * c:4 * c] = (dcv * vs).astype(dz_ref.dtype)
        dz_ref[:, 4 * c:5 * c] = (dcv * cs).astype(dz_ref.dtype)

        if pad:
            @pl.when(b == 0)
            def _():
                dz_ref[0:pad, :] = jnp.zeros((pad, 5 * c), dz_ref.dtype)

    full = lambda shape: pl.BlockSpec(shape, lambda i: (0,) * len(shape))
    cur = lambda width: pl.BlockSpec((r, width), lambda i: (nb - 1 - i, 0))
    prev8 = lambda width: pl.BlockSpec((8, width), lambda i: (jnp.maximum((nb - 1 - i) * r8 - 1, 0), 0))
    return _call(
        body, grid=(nb,),
        in_specs=[cur(5 * c), prev8(5 * c), cur(c), prev8(c), cur(2 * c),
                  full(pv.shape), full(wa.shape), full(wx.shape), full(gm.shape)],
        out_specs=[cur(5 * c), full(pv.shape), full(wa.shape), full(wx.shape)],
        out_shape=[jax.ShapeDtypeStruct((m, 5 * c), MXU_DTYPE), jax.ShapeDtypeStruct(pv.shape, F32),
                   jax.ShapeDtypeStruct(wa.shape, F32), jax.ShapeDtypeStruct(wx.shape, F32)],
        scratch_shapes=[pltpu.VMEM((r + 16, c), F32), pltpu.VMEM((8, c), F32), pltpu.VMEM((8, c), F32),
                        pltpu.VMEM((8, c), F32), pltpu.VMEM((1, c), F32), pltpu.VMEM((r, c), F32),
                        pltpu.VMEM((r, c), F32), pltpu.VMEM((r, c), F32)],
        name=name, args=(z, z, hs, hs, dmixed, pv, wa, wx, gm), carried=carried)


def _position():
    return lax.axis_index("x"), lax.axis_index("y"), lax.axis_index("c")


def _block_of(px, py, pc):
    return 4 * px + 2 * py + pc


class _TwoLevelGather:
    def __init__(self, n_arrays, rows_of, src_of, send_sems, recv_sems):
        x, y, c = _position()
        self.n, self.rows_of, self.src_of = n_arrays, rows_of, src_of
        self.send_sems, self.recv_sems = send_sems, recv_sems
        self.c, self.me, self.sibling = c, (x, y, c), (x, y, 1 - c)
        self.chips = [(1 - x, y), (x, 1 - y), (1 - x, 1 - y)]

    def _copy(self, i, k, block, to, src=None):
        return pltpu.make_async_remote_copy(
            src_ref=self.rows_of(i, *block) if src is None else src, dst_ref=self.rows_of(i, *block),
            send_sem=self.send_sems.at[7 * i + k], recv_sem=self.recv_sems.at[7 * i + k],
            device_id=to, device_id_type=MESH)

    def _first(self, i):
        own = [self._copy(i, 0, self.me, self.sibling, src=self.src_of(i))]
        return own + [self._copy(i, 1 + j, self.me, (*chip, self.c), src=self.src_of(i))
                      for j, chip in enumerate(self.chips)]

    def _passed(self, i, j):
        return self._copy(i, 4 + j, (*self.chips[j], self.c), self.sibling)

    def start(self):
        for i in range(self.n):
            for cp in self._first(i):
                cp.start()

    def forward(self):
        for i in range(self.n):
            for j, chip in enumerate(self.chips):
                self._copy(i, 1 + j, (*chip, self.c), self.me).wait_recv()
                self._passed(i, j).start()

    def drain(self):
        for i in range(self.n):
            self._copy(i, 0, self.sibling, self.me).wait_recv()
            for j, chip in enumerate(self.chips):
                self._copy(i, 4 + j, (*chip, 1 - self.c), self.me).wait_recv()
        for i in range(self.n):
            for cp in self._first(i) + [self._passed(i, j) for j in range(3)]:
                cp.wait_send()


class _RelayGather:
    def __init__(self, n_arrays, rows_of, src_of, send_sems, recv_sems):
        x, y, c = _position()
        self.n, self.rows_of, self.src_of = n_arrays, rows_of, src_of
        self.send_sems, self.recv_sems = send_sems, recv_sems
        self.me, self.sibling = (x, y, c), (x, y, 1 - c)
        self.xn, self.yn, self.dg = (1 - x, y, c), (x, 1 - y, c), (1 - x, 1 - y, c)

    def _copy(self, i, k, block, to, half=None, src=None):
        rows = self.rows_of(i, *block, half)
        return pltpu.make_async_remote_copy(
            src_ref=rows if src is None else src, dst_ref=rows,
            send_sem=self.send_sems.at[8 * i + k], recv_sem=self.recv_sems.at[8 * i + k],
            device_id=to, device_id_type=MESH)

    def _sends(self, i):
        own = self.src_of(i)
        return [self._copy(i, 0, self.me, self.sibling, src=own), self._copy(i, 1, self.me, self.xn, src=own),
                self._copy(i, 2, self.me, self.yn, src=own),
                self._copy(i, 3, self.xn, self.yn, half=0), self._copy(i, 4, self.yn, self.xn, half=1),
                self._copy(i, 5, self.xn, self.sibling), self._copy(i, 6, self.yn, self.sibling),
                self._copy(i, 7, self.dg, self.sibling)]

    def start(self):
        for i in range(self.n):
            for cp in self._sends(i)[0:3]:
                cp.start()

    def forward(self):
        for i in range(self.n):
            self._copy(i, 1, self.xn, self.me).wait_recv()
            self._copy(i, 2, self.yn, self.me).wait_recv()
            for cp in self._sends(i)[3:7]:
                cp.start()

    def drain(self):
        x, y, c = self.me
        for i in range(self.n):
            self._copy(i, 3, self.dg, self.me, half=0).wait_recv()
            self._copy(i, 4, self.dg, self.me, half=1).wait_recv()
            self._sends(i)[7].start()
        for i in range(self.n):
            self._copy(i, 0, self.sibling, self.me).wait_recv()
            self._copy(i, 5, (1 - x, y, 1 - c), self.me).wait_recv()
            self._copy(i, 6, (x, 1 - y, 1 - c), self.me).wait_recv()
            self._copy(i, 7, (1 - x, 1 - y, 1 - c), self.me).wait_recv()
        for i in range(self.n):
            for cp in self._sends(i):
                cp.wait_send()


class _CarriedGather:
    def __init__(self, shards, padded_rows, zeros, forward_at, part=None, into=None):
        d = shards[0].shape[1]
        self.forward_at = forward_at
        self.n = len(shards)
        self.rows = [s.shape[0] for s in shards]
        self.pads = [p - N_DEV * r for r, p in zip(self.rows, padded_rows)]
        assert max(self.pads) <= zeros.shape[0] and zeros.shape[1] == d
        self.part = part if part is not None else (0, self.rows[0])
        assert (part is None and into is None) or self.n == 1
        assert self.part[0] % SUBLANE_BF16 == 0 and self.part[1] % SUBLANE_BF16 == 0
        self.arrays = list(shards) + [zeros] + ([into] if into is not None else [])
        self.out_shapes = [jax.ShapeDtypeStruct((p, d), s.dtype) for s, p in zip(shards, padded_rows)]
        self.aliases = {self.n + 1: 0} if into is not None else {}
        if into is not None:
            self.pads = [0] * self.n
        self.n_remote, self.n_local = 8 * self.n, 2 * self.n
        self.results = None

    def _rows_of(self, outs):
        def rows_of(i, px, py, pc, half):
            first, count = (self.part if self.n == 1 else (0, self.rows[i]))
            head = _round_up(count // 2, SUBLANE_BF16)
            if half == 0:
                count = head
            elif half == 1:
                first, count = first + head, count - head
            first = _block_of(px, py, pc) * self.rows[i] + first
            return outs[i].at[pl.ds(pl.multiple_of(first, SUBLANE_BF16), count), :]
        return rows_of

    def _own(self, ins, i):
        return ins[i].at[pl.ds(self.part[0], self.part[1]), :] if self.n == 1 else ins[i]

    def _gather(self, ins, outs, send_sems, recv_sems):
        return _RelayGather(self.n, self._rows_of(outs), functools.partial(self._own, ins), send_sems, recv_sems)

    def _local(self, ins, outs, local_sems):
        x, y, c = _position()
        rows_of = self._rows_of(outs)
        cps = []
        for i in range(self.n):
            cps.append(pltpu.make_async_copy(self._own(ins, i), rows_of(i, x, y, c, None), local_sems.at[2 * i]))
            if self.pads[i]:
                cps.append(pltpu.make_async_copy(ins[self.n].at[pl.ds(0, self.pads[i]), :],
                                                 outs[i].at[pl.ds(N_DEV * self.rows[i], self.pads[i]), :],
                                                 local_sems.at[2 * i + 1]))
        return cps

    def start(self, ins, outs, send_sems, recv_sems, local_sems):
        for cp in self._local(ins, outs, local_sems):
            cp.start()
        self._gather(ins, outs, send_sems, recv_sems).start()

    def forward(self, ins, outs, send_sems, recv_sems, local_sems):
        self._gather(ins, outs, send_sems, recv_sems).forward()

    def finish(self, ins, outs, send_sems, recv_sems, local_sems):
        self._gather(ins, outs, send_sems, recv_sems).drain()
        for cp in self._local(ins, outs, local_sems):
            cp.wait()


class _CarriedSwap:
    def __init__(self, grads, shard_rows):
        d = grads[0].shape[1]
        self.n, self.rows = len(grads), list(shard_rows)
        self.arrays = list(grads)
        self.out_shapes = [jax.ShapeDtypeStruct((4, s, d), g.dtype) for g, s in zip(grads, shard_rows)]
        self.aliases = {}
        self.n_remote, self.n_local = 4 * self.n, 0
        self.forward_at = 1.0
        self.results = None

    def _copies(self, ins, outs, send_sems, recv_sems):
        x, y, c = _position()
        cps = []
        for i in range(self.n):
            s = self.rows[i]
            for k in range(4):
                blk = _block_of(k >> 1, k & 1, 1 - c)
                cps.append(pltpu.make_async_remote_copy(
                    src_ref=ins[i].at[pl.ds(pl.multiple_of(blk * s, SUBLANE_BF16), s), :], dst_ref=outs[i].at[k],
                    send_sem=send_sems.at[4 * i + k], recv_sem=recv_sems.at[4 * i + k],
                    device_id=(x, y, 1 - c), device_id_type=MESH))
        return cps

    def start(self, ins, outs, send_sems, recv_sems, local_sems):
        for cp in self._copies(ins, outs, send_sems, recv_sems):
            cp.start()

    def forward(self, *_):
        pass

    def finish(self, ins, outs, send_sems, recv_sems, local_sems):
        for cp in self._copies(ins, outs, send_sems, recv_sems):
            cp.wait()


class _CarriedChipExchange:
    def __init__(self, presums, part=None, into=None):
        self.n = len(presums)
        assert (part is None and into is None) or self.n == 1
        self.part = part if part is not None else (0, presums[0].shape[1])
        assert self.part[0] % SUBLANE_BF16 == 0 and self.part[1] % SUBLANE_BF16 == 0
        self.arrays = list(presums) + ([into] if into is not None else [])
        self.out_shapes = [jax.ShapeDtypeStruct(p.shape, p.dtype) for p in presums]
        self.aliases = {self.n: 0} if into is not None else {}
        self.n_remote, self.n_local = 3 * self.n, 0
        self.forward_at = 1.0
        self.results = None

    def _copies(self, ins, outs, send_sems, recv_sems):
        x, y, c = _position()
        cps = []
        for i in range(self.n):
            rows = pl.ds(*self.part) if self.n == 1 else pl.ds(0, self.arrays[i].shape[1])
            for r in range(1, 4):
                cps.append(pltpu.make_async_remote_copy(
                    src_ref=ins[i].at[r - 1, rows, :], dst_ref=outs[i].at[r - 1, rows, :],
                    send_sem=send_sems.at[3 * i + r - 1], recv_sem=recv_sems.at[3 * i + r - 1],
                    device_id=(x ^ (r >> 1), y ^ (r & 1), c), device_id_type=MESH))
        return cps

    def start(self, ins, outs, send_sems, recv_sems, local_sems):
        for cp in self._copies(ins, outs, send_sems, recv_sems):
            cp.start()

    def forward(self, *_):
        pass

    def finish(self, ins, outs, send_sems, recv_sems, local_sems):
        for cp in self._copies(ins, outs, send_sems, recv_sems):
            cp.wait()


def _gather_small(block, reduce, name):
    rr, nn = block.shape

    def body(x_ref, out_ref, *rest):
        if reduce:
            stack_ref, send_sems, recv_sems, local_sem = rest
        else:
            send_sems, recv_sems, local_sem = rest
            stack_ref = out_ref
        x, y, c = _position()

        def rows_of(i, px, py, pc):
            return stack_ref.at[pl.ds(pl.multiple_of(_block_of(px, py, pc) * rr, 8), rr), :]

        own = pltpu.make_async_copy(x_ref, rows_of(0, x, y, c), local_sem)
        own.start()
        gather = _TwoLevelGather(1, rows_of, lambda i: x_ref, send_sems, recv_sems)
        gather.start()
        gather.forward()
        gather.drain()
        own.wait()
        if reduce:
            acc = stack_ref[0:rr, :]
            for k in range(1, N_DEV):
                acc = acc + stack_ref[k * rr:(k + 1) * rr, :]
            out_ref[...] = acc

    vmem = pl.BlockSpec(memory_space=pltpu.VMEM)
    scratch = [pltpu.SemaphoreType.DMA((7,)), pltpu.SemaphoreType.DMA((7,)), pltpu.SemaphoreType.DMA]
    if reduce:
        scratch = [pltpu.VMEM((N_DEV * rr, nn), F32)] + scratch
    out_rows = rr if reduce else N_DEV * rr
    return pl.pallas_call(
        body, in_specs=[vmem], out_specs=vmem, out_shape=jax.ShapeDtypeStruct((out_rows, nn), F32),
        scratch_shapes=scratch, name=name, compiler_params=_params())(block)


def _sum_stack(stack, name):
    rr = stack.shape[0] // N_DEV

    def body(s_ref, o_ref):
        acc = s_ref[0:rr, :]
        for k in range(1, N_DEV):
            acc = acc + s_ref[k * rr:(k + 1) * rr, :]
        o_ref[...] = acc

    vmem = pl.BlockSpec(memory_space=pltpu.VMEM)
    return pl.pallas_call(body, in_specs=[vmem], out_specs=vmem,
                          out_shape=jax.ShapeDtypeStruct((rr, stack.shape[1]), F32), name=name,
                          compiler_params=_params())(stack)


def _presum(where, grad, swapped, name):
    s, d = swapped.shape[1], swapped.shape[2]
    tc = _tile(d, 2048, LANE)

    def body(where_ref, g_ref, sw_ref, o_ref):
        o_ref[0] = (g_ref[...].astype(F32) + sw_ref[0].astype(F32)).astype(o_ref.dtype)

    return _call(
        body, grid=(3, d // tc),
        in_specs=[pl.BlockSpec((s, tc), lambda r, j, where: (where[1 + r], j)),
                  pl.BlockSpec((1, s, tc), lambda r, j, where: (where[5 + r], 0, j))],
        out_specs=pl.BlockSpec((1, s, tc), lambda r, j, where: (r, 0, j)),
        out_shape=jax.ShapeDtypeStruct((3, s, d), WIRE_DTYPE), name=name, args=(grad, swapped), prefetch=(where,))


def _final_sum(where, grad, swapped, received, name, carried=()):
    s, d = swapped.shape[1], swapped.shape[2]
    tc = _tile(d, 512, LANE)

    def body(where_ref, g_ref, sw_ref, r_ref, o_ref):
        acc = g_ref[...].astype(F32) + sw_ref[0].astype(F32)
        for k in range(3):
            acc = acc + r_ref[k].astype(F32)
        o_ref[...] = acc

    return _call(
        body, grid=(d // tc,),
        in_specs=[pl.BlockSpec((s, tc), lambda j, where: (where[0], j)),
                  pl.BlockSpec((1, s, tc), lambda j, where: (where[4], 0, j)),
                  pl.BlockSpec((3, s, tc), lambda j, where: (0, 0, j))],
        out_specs=pl.BlockSpec((s, tc), lambda j, where: (0, j)),
        out_shape=jax.ShapeDtypeStruct((s, d), F32), name=name, args=(grad, swapped, received),
        prefetch=(where,), carried=carried)


class _GradReduction:
    def __init__(self, key, grad, shard_rows, where):
        self.key, self.grad, self.rows, self.where = key, grad, shard_rows, where
        self._presum = self._exchange = None

    def swap(self):
        self._swap = _CarriedSwap([self.grad], [self.rows])
        return self._swap

    def exchange(self, part=None):
        if self._presum is None:
            self._presum = _presum(self.where, self.grad, self._swap.results[0], "presum_" + self.key)
        rows = None
        if part is not None:
            half = _round_up(self.rows // 2, SUBLANE_BF16)
            rows = (0, half) if part == 0 else (half, self.rows - half)
        into = self._exchange.results[0] if part == 1 else None
        self._exchange = _CarriedChipExchange([self._presum], rows, into)
        return self._exchange

    def total(self, carried=()):
        return _final_sum(self.where, self.grad, self._swap.results[0], self._exchange.results[0],
                          "sum_" + self.key, carried)

    def total_and_update(self, w, m, v):
        return _sum_adamw(self.where, self.grad, self._swap.results[0], self._exchange.results[0], w, m, v,
                          "update_" + self.key)


def _adamw_math(w, g, m, v):
    nm = ADAM_B1 * m + (1.0 - ADAM_B1) * g
    nv = ADAM_B2 * v + (1.0 - ADAM_B2) * (g * g)
    m_hat = nm / (1.0 - ADAM_B1 ** ADAM_STEP)
    v_hat = nv / (1.0 - ADAM_B2 ** ADAM_STEP)
    return -ADAM_LR * (m_hat / (jnp.sqrt(v_hat) + ADAM_EPS) + ADAM_WD * w), nm, nv


def _sum_adamw(where, grad, swapped, received, w, m, v, name):
    s, d = swapped.shape[1], swapped.shape[2]
    tc = _tile(d, 512, LANE)

    def body(where_ref, g_ref, sw_ref, r_ref, w_ref, m_ref, v_ref, gs_ref, d_ref, nm_ref, nv_ref):
        g = g_ref[...].astype(F32) + sw_ref[0].astype(F32)
        for k in range(3):
            g = g + r_ref[k].astype(F32)
        gs_ref[...] = g
        d_ref[...], nm_ref[...], nv_ref[...] = _adamw_math(w_ref[...], g, m_ref[...], v_ref[...])

    blk = pl.BlockSpec((s, tc), lambda j, where: (0, j))
    return _call(
        body, grid=(d // tc,),
        in_specs=[pl.BlockSpec((s, tc), lambda j, where: (where[0], j)),
                  pl.BlockSpec((1, s, tc), lambda j, where: (where[4], 0, j)),
                  pl.BlockSpec((3, s, tc), lambda j, where: (0, 0, j)), blk, blk, blk],
        out_specs=[blk] * 4, out_shape=[jax.ShapeDtypeStruct((s, d), F32)] * 4, name=name,
        args=(grad, swapped, received, w, m, v), prefetch=(where,))


def _adamw(w, g, m, v, name):
    rows, cols = w.shape
    tr = _tile(rows, 256, 8)

    def body(w_ref, g_ref, m_ref, v_ref, d_ref, nm_ref, nv_ref):
        d_ref[...], nm_ref[...], nv_ref[...] = _adamw_math(w_ref[...], g_ref[...], m_ref[...], v_ref[...])

    spec = pl.BlockSpec((tr, cols), lambda i: (i, 0))
    return pl.pallas_call(
        body, grid=(rows // tr,), in_specs=[spec] * 4, out_specs=[spec] * 3,
        out_shape=[jax.ShapeDtypeStruct((rows, cols), F32)] * 3, name=name, compiler_params=_params())(w, g, m, v)


def _pack_rows(arrays, width, row_quantum=8):
    flat = jnp.concatenate([a.reshape(-1) for a in arrays])
    total = _round_up(flat.shape[0], row_quantum * width)
    flat = jnp.pad(flat, (0, total - flat.shape[0]))
    return flat.reshape(-1, width)


def _unpack_rows(packed, shapes):
    flat = packed.reshape(-1)
    out = []
    off = 0
    for shp in shapes:
        size = 1
        for s in shp:
            size *= s
        out.append(flat[off:off + size].reshape(shp))
        off += size
    return out


def _block_diag(w):
    h, hb, _ = w.shape
    per = BD // hb
    w4 = w.reshape(h // per, per, hb, hb)
    eye = jnp.eye(per, dtype=w.dtype)
    return jnp.einsum('npij,pq->npiqj', w4, eye).reshape(h // per, BD, BD)


def _block_diag_extract(bd, hb):
    nbk = bd.shape[0]
    per = BD // hb
    b5 = bd.reshape(nbk, per, hb, per, hb)
    eye = jnp.eye(per, dtype=bd.dtype)
    return jnp.einsum('npiqj,pq->npij', b5, eye).reshape(nbk * per, hb, hb)


def kernel(x, meta_tokens, ffn1_pre_g, ffn1_w_gate, ffn1_w_up, ffn1_w_down, ffn1_post_g, mix_pre_g, w_in, lru_conv_w, lru_conv_b, lru_w_a, lru_b_a, lru_w_x, lru_b_x, lru_lambda, sconv_w, lru_out_g, sconv_out_g, w_out, mix_post_g, ffn2_pre_g, ffn2_w_gate, ffn2_w_up, ffn2_w_down, ffn2_post_g, loss_target, m_meta_tokens, m_ffn1_pre_g, m_ffn1_w_gate, m_ffn1_w_up, m_ffn1_w_down, m_ffn1_post_g, m_mix_pre_g, m_w_in, m_lru_conv_w, m_lru_conv_b, m_lru_w_a, m_lru_b_a, m_lru_w_x, m_lru_b_x, m_lru_lambda, m_sconv_w, m_lru_out_g, m_sconv_out_g, m_w_out, m_mix_post_g, m_ffn2_pre_g, m_ffn2_w_gate, m_ffn2_w_up, m_ffn2_w_down, m_ffn2_post_g, v_meta_tokens, v_ffn1_pre_g, v_ffn1_w_gate, v_ffn1_w_up, v_ffn1_w_down, v_ffn1_post_g, v_mix_pre_g, v_w_in, v_lru_conv_w, v_lru_conv_b, v_lru_w_a, v_lru_b_a, v_lru_w_x, v_lru_b_x, v_lru_lambda, v_sconv_w, v_lru_out_g, v_sconv_out_g, v_w_out, v_mix_post_g, v_ffn2_pre_g, v_ffn2_w_gate, v_ffn2_w_up, v_ffn2_w_down, v_ffn2_post_g):
    given = dict(locals())
    wts = {n: given[n] for n in WEIGHT_NAMES}
    mom = {n: given["m_" + n] for n in WEIGHT_NAMES}
    var = {n: given["v_" + n] for n in WEIGHT_NAMES}

    xi, yi, ci = _position()
    me = _block_of(xi, yi, ci)
    x2 = x[0]
    seq, d = x2.shape
    n_meta = meta_tokens.shape[0]
    m_rows = _round_up(n_meta + seq, ROW_ALIGN)
    pad = m_rows - n_meta - seq
    lead = pad + n_meta
    c = lru_conv_b.shape[1]
    hb = lru_w_a.shape[-1]
    dm = meta_tokens.shape[1]
    cs_ = lru_conv_w.shape[2]
    kw4, kw3 = lru_conv_w.shape[1], sconv_w.shape[1]
    assert d == 2 * c and c % BD == 0 and BD % hb == 0 and cs_ <= dm and kw4 == 4 and kw3 == 3

    small = jnp.zeros((_round_up(n_meta + kw4 + kw3, 8), dm), F32)
    small = small.at[0:n_meta].set(meta_tokens)
    small = small.at[n_meta:n_meta + kw4, 0:cs_].set(lru_conv_w[0])
    small = small.at[n_meta + kw4:n_meta + kw4 + kw3, 0:cs_].set(sconv_w[0])
    sr = small.shape[0]
    small_all = _gather_small(small, False, "gather_small").reshape(N_DEV, sr, dm)
    meta_full = small_all[:, 0:n_meta, :].transpose(1, 0, 2).reshape(n_meta, d)
    conv_w_full = small_all[:, n_meta:n_meta + kw4, 0:cs_].transpose(1, 0, 2).reshape(kw4, c)
    sconv_w_full = small_all[:, n_meta + kw4:n_meta + kw4 + kw3, 0:cs_].transpose(1, 0, 2).reshape(kw3, c)

    big = ['ffn1_w_gate', 'ffn1_w_up', 'ffn1_w_down', 'w_in', 'w_out', 'ffn2_w_gate', 'ffn2_w_up', 'ffn2_w_down']
    col_sharded = {'ffn1_w_gate', 'ffn1_w_up', 'w_in', 'ffn2_w_gate', 'ffn2_w_up'}
    shards = []
    for nme in big:
        w = wts[nme][0].astype(WIRE_DTYPE)
        shards.append(w.T if nme in col_sharded else w)
    shard_rows = dict(zip(big, [s.shape[0] for s in shards]))
    zeros = jnp.zeros((F_ALIGN, d), WIRE_DTYPE)

    def gather(forward_at, *names, part=None, into=None):
        sel = [shards[big.index(nme)] for nme in names]
        padded = [_round_up(N_DEV * shard_rows[nme], LANE if nme in ('w_in', 'w_out') else F_ALIGN) for nme in names]
        return _CarriedGather(sel, padded, zeros, forward_at, part, into)

    pv = jnp.zeros((16, c), F32)
    pv = pv.at[0:4].set(conv_w_full).at[4].set(lru_conv_b[0]).at[5].set(lru_b_a[0]).at[6].set(lru_b_x[0])
    pv = pv.at[7].set(lru_lambda[0]).at[8:11].set(sconv_w_full).at[11].set(lru_out_g[0]).at[12].set(sconv_out_g[0])
    wa_bd = _block_diag(lru_w_a[0]).astype(MXU_DTYPE)
    wx_bd = _block_diag(lru_w_x[0]).astype(MXU_DTYPE)
    gs = c // N_GROUPS
    gidx = jnp.arange(BD) // gs
    gm = jnp.where(gidx[:, None] == gidx[None, :], 1.0 / gs, 0.0).astype(MXU_DTYPE)

    ride = gather(0.3, 'ffn1_w_gate')
    h0, n1, target = _embed(x2, meta_full, loss_target[0], ffn1_pre_g, pad, "embed_prenorm", carried=[ride])
    (wg1,) = ride.results
    ride = gather(0.6, 'ffn1_w_up')
    g1 = _mm_nt(n1, wg1, "ffn1_gate", carried=[ride], out_dtype=MXU_DTYPE)
    (wu1,) = ride.results
    ride = gather(0.6, 'ffn1_w_down')
    u1, a1 = _ffn_up_act(n1, wu1, g1, "ffn1_up_act", carried=[ride])
    (wd1,) = ride.results
    ride = gather(0.75, 'w_in', 'w_out')
    fo1, h1, un = _mm_residual_norm(a1, wd1, h0, ffn1_post_g, 0.5, mix_pre_g, "ffn1_down", carried=[ride])
    win_t, wout = ride.results
    s2 = shard_rows['ffn2_w_gate']
    quarter = _round_up(s2 // 4, SUBLANE_BF16)
    ride_g = gather(0.5, 'ffn2_w_gate', part=(0, 3 * quarter))
    z = _mm_nt(un, win_t, "mix_in_proj", carried=[ride_g])
    ride_g = gather(0.5, 'ffn2_w_gate', part=(3 * quarter, s2 - 3 * quarter), into=ride_g.results[0])
    ride_u = gather(0.5, 'ffn2_w_up', part=(0, quarter))
    mixed, hs = _mixer_fwd(z, pv, wa_bd, wx_bd, gm, pad, "mixer_fwd", carried=[ride_g, ride_u])
    (wg2,) = ride_g.results
    ride_u = gather(0.5, 'ffn2_w_up', part=(quarter, s2 - quarter), into=ride_u.results[0])
    o_mix, h2, n2 = _mm_residual_norm(mixed, wout, h1, mix_post_g, 1.0, ffn2_pre_g, "mix_out_proj", carried=[ride_u])
    (wu2,) = ride_u.results
    ride = gather(0.75, 'ffn2_w_down')
    g2, u2, a2 = _ffn_gate_up(n2, wg2, wu2, "ffn2_gate_up", carried=[ride])
    (wd2,) = ride.results
    dh3, dfo2, d_post2, loss_part = _mm_residual_loss(a2, wd2, h2, ffn2_post_g, 0.5, target, lead, "ffn2_down_loss")
    loss = lax.psum(loss_part[0, 0], ("x", "y", "c"))

    chip_rel = [2 * (xi ^ (r >> 1)) + (yi ^ (r & 1)) for r in range(4)]
    where = jnp.stack([2 * k + ci for k in chip_rel] + chip_rel).astype(jnp.int32)
    red = {}

    def reduction(nme, grad):
        red[nme] = _GradReduction(nme, grad, shard_rows[nme], where)
        return red[nme]

    r_wd2 = reduction('ffn2_w_down', _mm_tn(a2, dfo2, "ffn2_dw_down"))
    dg2, du2 = _ffn_hidden_bwd(dfo2, wd2, g2, u2, "ffn2_hidden_bwd", carried=[r_wd2.swap()])
    r_wg2 = reduction('ffn2_w_gate', _mm_tn(dg2, n2, "ffn2_dw_gate", carried=[r_wd2.exchange(part=0)]))
    r_wu2 = reduction('ffn2_w_up', _mm_tn(du2, n2, "ffn2_dw_up", carried=[r_wd2.exchange(part=1), r_wg2.swap()]))
    dh2, d_pre2 = _mm_norm_bwd([(dg2, wg2), (du2, wu2)], h2, ffn2_pre_g, dh3, "ffn2_dx",
                               carried=[r_wg2.exchange(), r_wu2.swap()])
    do_mix, d_mix_post, dmixed = _norm_bwd_mm_nt(o_mix, mix_post_g, dh2, 1.0, wout, "mix_out_proj_bwd")
    r_wout = reduction('w_out', _mm_tn(mixed, do_mix, "mix_dw_out"))
    dz, dpv, dwa_bd, dwx_bd = _mixer_bwd(z, hs, dmixed, pv, wa_bd, wx_bd, gm, pad, "mixer_bwd",
                                         carried=[r_wu2.exchange(), r_wout.swap()])
    r_win = reduction('w_in', _mm_tn(dz, un, "mix_dw_in", carried=[r_wout.exchange()]))
    dh1, d_mix_pre, dfo1, d_post1 = _mm_norm_bwd([(dz, win_t)], h1, mix_pre_g, dh2, "mix_dx", carried=[r_win.swap()],
                                                 post=(fo1, ffn1_post_g, 0.5))
    r_wd1 = reduction('ffn1_w_down', _mm_tn(a1, dfo1, "ffn1_dw_down", carried=[r_win.exchange(part=0)]))
    early_names = ['mix_pre_g', 'mix_post_g', 'ffn2_pre_g', 'ffn2_post_g', 'ffn1_post_g',
                   'lru_conv_b', 'lru_b_a', 'lru_b_x', 'lru_lambda', 'lru_out_g', 'sconv_out_g',
                   'lru_conv_w', 'sconv_w', 'lru_w_a', 'lru_w_x']
    early_parts = [d_mix_pre, d_mix_post, d_pre2, d_post2, d_post1,
                   dpv[4:5], dpv[5:6], dpv[6:7], dpv[7:8], dpv[11:12], dpv[12:13],
                   dpv[0:4], dpv[8:11], _block_diag_extract(dwa_bd, hb), _block_diag_extract(dwx_bd, hb)]
    early_packed = _pack_rows(early_parts, d, SUBLANE_BF16)
    early_ride = _CarriedGather([early_packed], [N_DEV * early_packed.shape[0]], zeros, 0.75)
    dg1, du1 = _ffn_hidden_bwd(dfo1, wd1, g1, u1, "ffn1_hidden_bwd",
                               carried=[r_win.exchange(part=1), r_wd1.swap(), early_ride])
    early_sum = _sum_stack(early_ride.results[0], "sum_small_early")
    r_wg1 = reduction('ffn1_w_gate', _mm_tn(dg1, n1, "ffn1_dw_gate", carried=[r_wd1.exchange(part=0)]))
    r_wu1 = reduction('ffn1_w_up', _mm_tn(du1, n1, "ffn1_dw_up", carried=[r_wd1.exchange(part=1), r_wg1.swap()]))
    row_tile = _norm_bwd_row_tile(m_rows)
    n_tiles = m_rows // row_tile
    half = n_tiles // 2
    assert half >= 1 and half * row_tile >= lead
    dh0_a, d_pre1_a = _mm_norm_bwd([(dg1, wg1), (du1, wu1)], h0, ffn1_pre_g, dh1, "ffn1_dx_a",
                                   carried=[r_wg1.exchange(), r_wu1.swap()], row_tiles=(0, half))
    dh0_b, d_pre1 = _mm_norm_bwd([(dg1, wg1), (du1, wu1)], h0, ffn1_pre_g, dh1, "ffn1_dx_b",
                                 carried=[r_wu1.exchange()], row_tiles=(half, n_tiles - half), dg_init=d_pre1_a)
    grad_x = jnp.concatenate([dh0_a[lead:], dh0_b], axis=0)[None]
    d_meta = dh0_a[pad:lead]

    grads, delta, new_m, new_v = {}, {}, {}, {}
    for nme in big:
        in_shard_layout = nme not in col_sharded or shard_rows[nme] % LANE != 0
        if in_shard_layout:
            view = (lambda t: t[0].T) if nme in col_sharded else (lambda t: t[0])
            back = (lambda t: t.T[None]) if nme in col_sharded else (lambda t: t[None])
            outs = red[nme].total_and_update(view(wts[nme]), view(mom[nme]), view(var[nme]))
            grads[nme], delta[nme], new_m[nme], new_v[nme] = [back(t) for t in outs]
        else:
            grads[nme] = red[nme].total().T[None]
            outs = _adamw(wts[nme][0], grads[nme][0], mom[nme][0], var[nme][0], "adamw_" + nme)
            delta[nme], new_m[nme], new_v[nme] = [t[None] for t in outs]

    late_names = ['ffn1_pre_g', 'meta_tokens']
    late_parts = [d_pre1, d_meta]
    late_sum = _gather_small(_pack_rows(late_parts, d), True, "reduce_small_late")
    small_sums = (_unpack_rows(early_sum, [p.shape for p in early_parts])
                  + _unpack_rows(late_sum, [p.shape for p in late_parts]))
    for nme, gsm in zip(early_names + late_names, small_sums):
        if nme == 'meta_tokens':
            grads[nme] = lax.dynamic_slice_in_dim(gsm, me * dm, dm, axis=1)
        elif nme in ('lru_conv_w', 'sconv_w'):
            grads[nme] = lax.dynamic_slice_in_dim(gsm, me * cs_, cs_, axis=1)[None]
        else:
            grads[nme] = gsm.reshape(wts[nme].shape)

    rest = [n for n in WEIGHT_NAMES if n not in big]
    rest_shapes = [wts[n].shape for n in rest]
    packed = [_pack_rows([src[n] for n in rest], LANE, 256) for src in (wts, grads, mom, var)]
    for out, packed_out in zip((delta, new_m, new_v), _adamw(*packed, "adamw_small")):
        for nme, arr in zip(rest, _unpack_rows(packed_out, rest_shapes)):
            out[nme] = arr

    return (loss, grad_x, *[grads[n] for n in WEIGHT_NAMES], *[delta[n] for n in WEIGHT_NAMES],
            *[new_m[n] for n in WEIGHT_NAMES], *[new_v[n] for n in WEIGHT_NAMES])
```

```python
import functools

import jax
import jax.numpy as jnp
from jax import lax
from jax.experimental import pallas as pl
from jax.experimental.pallas import tpu as pltpu

F32 = jnp.float32
MXU_DTYPE = jnp.bfloat16
WIRE_DTYPE = jnp.bfloat16
MESH = pl.DeviceIdType.MESH

EPS = 1e-6
LRU_C = 8.0
N_GROUPS = 16
ADAM_LR = 0.001
ADAM_B1 = 0.9
ADAM_B2 = 0.999
ADAM_EPS = 1e-08
ADAM_WD = 0.01
ADAM_STEP = 10

N_DEV = 8
LANE = 128
SUBLANE_BF16 = 16
ROW_ALIGN = 128
F_ALIGN = 512
BD = 256
K_TILE = 512
ACC_ROWS = 528
ACC_GROUP = 1
MIX_ROWS = 128
VMEM_LIMIT_MB = 56

WEIGHT_NAMES = ['meta_tokens', 'ffn1_pre_g', 'ffn1_w_gate', 'ffn1_w_up', 'ffn1_w_down', 'ffn1_post_g',
                'mix_pre_g', 'w_in', 'lru_conv_w', 'lru_conv_b', 'lru_w_a', 'lru_b_a', 'lru_w_x', 'lru_b_x',
                'lru_lambda', 'sconv_w', 'lru_out_g', 'sconv_out_g', 'w_out', 'mix_post_g', 'ffn2_pre_g',
                'ffn2_w_gate', 'ffn2_w_up', 'ffn2_w_down', 'ffn2_post_g']


def _round_up(n, q):
    return (n + q - 1) // q * q


def _tile(n, target, q):
    best = None
    t = q
    while t <= min(n, target):
        if n % t == 0:
            best = t
        t += q
    assert best is not None, (n, target, q)
    return best


def _params(**kw):
    return pltpu.CompilerParams(vmem_limit_bytes=VMEM_LIMIT_MB << 20, **kw)


def _call(body, *, grid, in_specs, out_specs, out_shape, name, args, scratch_shapes=(), carried=(), prefetch=()):
    carried = list(carried)
    n_pf = len(prefetch)

    def launch(fn, in_specs_, out_specs_, out_shape_, scratch_, operands, aliases_):
        if n_pf:
            spec = pltpu.PrefetchScalarGridSpec(num_scalar_prefetch=n_pf, grid=grid, in_specs=in_specs_,
                                                out_specs=out_specs_, scratch_shapes=scratch_)
            return pl.pallas_call(fn, grid_spec=spec, out_shape=out_shape_, input_output_aliases=aliases_,
                                  name=name, compiler_params=_params())(*prefetch, *operands)
        return pl.pallas_call(fn, grid=grid, in_specs=in_specs_, out_specs=out_specs_, out_shape=out_shape_,
                              scratch_shapes=scratch_, input_output_aliases=aliases_, name=name,
                              compiler_params=_params())(*operands)

    if not carried:
        return launch(body, in_specs, out_specs, out_shape, list(scratch_shapes), args, {})
    single = not isinstance(out_shape, (list, tuple))
    out_specs_l = [out_specs] if single else list(out_specs)
    out_shape_l = [out_shape] if single else list(out_shape)
    n_in, n_out, n_scr = len(in_specs), len(out_specs_l), len(scratch_shapes)
    hbm = pl.BlockSpec(memory_space=pl.ANY)
    c_in = [a for cm in carried for a in cm.arrays]
    c_out = [s for cm in carried for s in cm.out_shapes]
    c_scr = []
    aliases = {}
    in_off, out_off = n_pf + n_in, n_out
    for cm in carried:
        c_scr += [pltpu.SemaphoreType.DMA((cm.n_remote,)), pltpu.SemaphoreType.DMA((cm.n_remote,)),
                  pltpu.SemaphoreType.DMA((max(cm.n_local, 1),))]
        for k, v in cm.aliases.items():
            aliases[in_off + k] = out_off + v
        in_off += len(cm.arrays)
        out_off += len(cm.out_shapes)
    steps = 1
    for g in grid:
        steps *= g
    forward_steps = [min(int(cm.forward_at * steps), steps - 1) for cm in carried]

    def wrapped(*refs):
        pf = refs[:n_pf]
        p = n_pf
        ins = refs[p:p + n_in]
        p += n_in
        cins = refs[p:p + len(c_in)]
        p += len(c_in)
        outs = refs[p:p + n_out]
        p += n_out
        couts = refs[p:p + len(c_out)]
        p += len(c_out)
        scr = refs[p:p + n_scr]
        csem = refs[p + n_scr:]
        lin = 0
        for axis, g in enumerate(grid):
            lin = lin * g + pl.program_id(axis)
        views = []
        io = oo = 0
        for j, cm in enumerate(carried):
            views.append((cins[io:io + len(cm.arrays)], couts[oo:oo + len(cm.out_shapes)],
                          csem[3 * j], csem[3 * j + 1], csem[3 * j + 2]))
            io += len(cm.arrays)
            oo += len(cm.out_shapes)

        @pl.when(lin == 0)
        def _():
            for cm, v in zip(carried, views):
                cm.start(*v)

        body(*pf, *ins, *outs, *scr)

        for cm, v, step in zip(carried, views, forward_steps):
            pl.when(lin == step)(functools.partial(cm.forward, *v))

        @pl.when(lin == steps - 1)
        def _():
            for cm, v in zip(carried, views):
                cm.finish(*v)

    res = launch(wrapped, list(in_specs) + [hbm] * len(c_in), out_specs_l + [hbm] * len(c_out),
                 out_shape_l + c_out, list(scratch_shapes) + c_scr, (*args, *c_in), aliases)
    oo = n_out
    for cm in carried:
        cm.results = list(res[oo:oo + len(cm.out_shapes)])
        oo += len(cm.out_shapes)
    return res[0] if single else list(res[:n_out])


def _embed(x, meta, target, g, pad, name, carried=()):
    seq, d = x.shape
    n_meta = meta.shape[0]
    lead = pad + n_meta
    m = lead + seq
    tr = ROW_ALIGN
    lead_blocks = lead // tr
    meta_row = pad - (lead_blocks - 1) * tr
    assert lead % tr == 0 and seq % tr == 0 and 0 <= meta_row and meta_row % 8 == 0

    def body(x_ref, meta_ref, t_ref, g_ref, h_ref, n_ref, tp_ref):
        i = pl.program_id(0)

        @pl.when(i < lead_blocks)
        def _():
            h_ref[...] = jnp.zeros_like(h_ref)
            tp_ref[...] = jnp.zeros_like(tp_ref)

        @pl.when(i == lead_blocks - 1)
        def _():
            h_ref[pl.ds(meta_row, n_meta), :] = meta_ref[...]

        @pl.when(i >= lead_blocks)
        def _():
            h_ref[...] = x_ref[...]
            tp_ref[...] = t_ref[...]

        h = h_ref[...]
        r = lax.rsqrt(jnp.mean(h * h, axis=-1, keepdims=True) + EPS)
        n_ref[...] = (h * r * g_ref[...]).astype(n_ref.dtype)

    tokens = pl.BlockSpec((tr, d), lambda i: (jnp.maximum(i - lead_blocks, 0), 0))
    rows = pl.BlockSpec((tr, d), lambda i: (i, 0))
    return _call(
        body, grid=(m // tr,),
        in_specs=[tokens, pl.BlockSpec((n_meta, d), lambda i: (0, 0)), tokens, pl.BlockSpec((1, d), lambda i: (0, 0))],
        out_specs=[rows, rows, rows],
        out_shape=[jax.ShapeDtypeStruct((m, d), F32), jax.ShapeDtypeStruct((m, d), MXU_DTYPE),
                   jax.ShapeDtypeStruct((m, d), F32)],
        name=name, args=(x, meta, target, g), carried=carried)


def _rmsnorm_bwd_rows(x, g, dy):
    r = lax.rsqrt(jnp.mean(x * x, axis=-1, keepdims=True) + EPS)
    xh = x * r
    dyh = dy * g
    dx = r * (dyh - xh * jnp.mean(dyh * xh, axis=-1, keepdims=True))
    return dx, dy * xh


def _dot_nt(a, b):
    return lax.dot_general(a, b, (((1,), (1,)), ((), ())), preferred_element_type=F32)


def _dot_tn(a, b):
    return lax.dot_general(a, b, (((0,), (0,)), ((), ())), preferred_element_type=F32)


def _mm_nt(a, w, name, carried=(), out_dtype=F32):
    m, k = a.shape
    n = w.shape[0]
    tm = _tile(m, 1056, SUBLANE_BF16)
    tn = _tile(n, 512, LANE)

    def body(a_ref, w_ref, o_ref):
        o_ref[...] = _dot_nt(a_ref[...], w_ref[...]).astype(o_ref.dtype)

    return _call(
        body, grid=(m // tm, n // tn),
        in_specs=[pl.BlockSpec((tm, k), lambda i, j: (i, 0)), pl.BlockSpec((tn, k), lambda i, j: (j, 0))],
        out_specs=pl.BlockSpec((tm, tn), lambda i, j: (i, j)),
        out_shape=jax.ShapeDtypeStruct((m, n), out_dtype), name=name, args=(a, w), carried=carried)


def _norm_bwd_mm_nt(x, g, dy, scale, w, name, carried=()):
    m, d = x.shape
    n = w.shape[0]
    tm = _tile(m, 528, SUBLANE_BF16)

    def body(x_ref, g_ref, dy_ref, w_ref, dx_ref, dg_ref, o_ref):
        @pl.when(pl.program_id(0) == 0)
        def _():
            dg_ref[...] = jnp.zeros_like(dg_ref)

        dx, dgr = _rmsnorm_bwd_rows(x_ref[...], g_ref[...], scale * dy_ref[...])
        dxb = dx.astype(dx_ref.dtype)
        dx_ref[...] = dxb
        dg_ref[...] += jnp.sum(dgr, axis=0, keepdims=True)
        o_ref[...] = _dot_nt(dxb, w_ref[...])

    row = pl.BlockSpec((tm, d), lambda i: (i, 0))
    vec = pl.BlockSpec((1, d), lambda i: (0, 0))
    return _call(
        body, grid=(m // tm,),
        in_specs=[row, vec, row, pl.BlockSpec((n, d), lambda i: (0, 0), pipeline_mode=pl.Buffered(1))],
        out_specs=[row, vec, pl.BlockSpec((tm, n), lambda i: (i, 0))],
        out_shape=[jax.ShapeDtypeStruct((m, d), MXU_DTYPE), jax.ShapeDtypeStruct((1, d), F32),
                   jax.ShapeDtypeStruct((m, n), F32)],
        name=name, args=(x, g, dy, w), carried=carried)


def _ffn_up_act(n_act, wu_t, g_act, name, carried=()):
    m, d = n_act.shape
    fp = wu_t.shape[0]
    tm = _tile(m, 1056, SUBLANE_BF16)
    tn = _tile(fp, 512, LANE)

    def body(n_ref, wu_ref, g_ref, u_ref, a_ref):
        u = _dot_nt(n_ref[...], wu_ref[...])
        g = g_ref[...].astype(F32)
        u_ref[...] = u.astype(u_ref.dtype)
        a_ref[...] = (g * jax.nn.sigmoid(g) * u).astype(a_ref.dtype)

    act = pl.BlockSpec((tm, tn), lambda i, j: (i, j))
    return _call(
        body, grid=(m // tm, fp // tn),
        in_specs=[pl.BlockSpec((tm, d), lambda i, j: (i, 0)), pl.BlockSpec((tn, d), lambda i, j: (j, 0)), act],
        out_specs=[act, act],
        out_shape=[jax.ShapeDtypeStruct((m, fp), MXU_DTYPE)] * 2, name=name, args=(n_act, wu_t, g_act), carried=carried)


def _ffn_gate_up(n_act, wg_t, wu_t, name, carried=()):
    m, d = n_act.shape
    fp = wg_t.shape[0]
    tm = _tile(m, 1056, SUBLANE_BF16)
    tn = _tile(fp, 512, LANE)

    def body(n_ref, wg_ref, wu_ref, g_ref, u_ref, a_ref):
        n = n_ref[...]
        g = _dot_nt(n, wg_ref[...])
        u = _dot_nt(n, wu_ref[...])
        g_ref[...] = g.astype(g_ref.dtype)
        u_ref[...] = u.astype(u_ref.dtype)
        a_ref[...] = (g * jax.nn.sigmoid(g) * u).astype(a_ref.dtype)

    act = pl.BlockSpec((tm, tn), lambda i, j: (i, j))
    wsp = pl.BlockSpec((tn, d), lambda i, j: (j, 0))
    return _call(
        body, grid=(m // tm, fp // tn),
        in_specs=[pl.BlockSpec((tm, d), lambda i, j: (i, 0)), wsp, wsp],
        out_specs=[act, act, act],
        out_shape=[jax.ShapeDtypeStruct((m, fp), MXU_DTYPE)] * 3, name=name, args=(n_act, wg_t, wu_t), carried=carried)


def _ffn_hidden_bwd(dfo, wd, g_act, u_act, name, carried=()):
    m, d = dfo.shape
    fp = wd.shape[0]
    tm = _tile(m, 1056, SUBLANE_BF16)
    tn = _tile(fp, 512, LANE)

    def body(df_ref, wd_ref, g_ref, u_ref, dg_ref, du_ref):
        da = _dot_nt(df_ref[...], wd_ref[...]).astype(dg_ref.dtype)
        g = g_ref[...]
        u = u_ref[...]
        s = jax.nn.sigmoid(g)
        du_ref[...] = da * (g * s)
        dg_ref[...] = da * (u * (s * (1.0 + g * (1.0 - s))))

    act = pl.BlockSpec((tm, tn), lambda i, j: (i, j))
    return _call(
        body, grid=(m // tm, fp // tn),
        in_specs=[pl.BlockSpec((tm, d), lambda i, j: (i, 0)), pl.BlockSpec((tn, d), lambda i, j: (j, 0)), act, act],
        out_specs=[act, act],
        out_shape=[jax.ShapeDtypeStruct((m, fp), MXU_DTYPE)] * 2, name=name, args=(dfo, wd, g_act, u_act),
        carried=carried)


def _row_groups(n_tiles, max_group, nk):
    gsz = max(q for q in range(1, max_group + 1) if n_tiles % q == 0)

    def epilogue_row(grp, kk, i):
        return grp * gsz + jnp.where(kk == nk - 1, i, 0)

    return gsz, epilogue_row


def _mm_residual_norm(a, w, h, g, scale, next_g, name, carried=()):
    m, k = a.shape
    d = w.shape[1]
    tm = _tile(m, ACC_ROWS, SUBLANE_BF16)
    tk = _tile(k, K_TILE, LANE)
    nk = k // tk
    gsz, epilogue_row = _row_groups(m // tm, ACC_GROUP, nk)

    def body(a_ref, w_ref, h_ref, g_ref, ng_ref, fo_ref, hn_ref, nn_ref, acc_ref):
        kk, i = pl.program_id(1), pl.program_id(2)

        @pl.when(kk == 0)
        def _():
            acc_ref[i] = jnp.zeros((tm, d), F32)

        acc_ref[i] += jnp.dot(a_ref[...], w_ref[...], preferred_element_type=F32)

        @pl.when(kk == nk - 1)
        def _():
            fo = acc_ref[i]
            fo_ref[...] = fo
            r = lax.rsqrt(jnp.mean(fo * fo, axis=-1, keepdims=True) + EPS)
            hn = h_ref[...] + scale * (fo * r * g_ref[...])
            hn_ref[...] = hn
            rn = lax.rsqrt(jnp.mean(hn * hn, axis=-1, keepdims=True) + EPS)
            nn_ref[...] = (hn * rn * ng_ref[...]).astype(nn_ref.dtype)

    row = pl.BlockSpec((tm, d), lambda grp, kk, i: (epilogue_row(grp, kk, i), 0))
    row_once = pl.BlockSpec((tm, d), lambda grp, kk, i: (epilogue_row(grp, kk, i), 0), pipeline_mode=pl.Buffered(1))
    vec = pl.BlockSpec((1, d), lambda grp, kk, i: (0, 0))
    return _call(
        body, grid=(m // tm // gsz, nk, gsz),
        in_specs=[pl.BlockSpec((tm, tk), lambda grp, kk, i: (grp * gsz + i, kk)),
                  pl.BlockSpec((tk, d), lambda grp, kk, i: (kk, 0)), row_once, vec, vec],
        out_specs=[row, row, row],
        out_shape=[jax.ShapeDtypeStruct((m, d), F32)] * 2 + [jax.ShapeDtypeStruct((m, d), MXU_DTYPE)],
        scratch_shapes=[pltpu.VMEM((gsz, tm, d), F32)], name=name, args=(a, w, h, g, next_g), carried=carried)


def _mm_residual_loss(a, w, h, g, scale, target, lead, name, carried=()):
    m, k = a.shape
    d = w.shape[1]
    tm = _tile(m, ACC_ROWS, SUBLANE_BF16)
    tk = _tile(k, K_TILE, LANE)
    nk = k // tk
    gsz, epilogue_row = _row_groups(m // tm, ACC_GROUP, nk)

    def body(a_ref, w_ref, h_ref, g_ref, t_ref, dy_ref, dfo_ref, dg_ref, l_ref, acc_ref):
        grp, kk, i = pl.program_id(0), pl.program_id(1), pl.program_id(2)

        @pl.when(jnp.logical_and(jnp.logical_and(grp == 0, kk == 0), i == 0))
        def _():
            dg_ref[...] = jnp.zeros_like(dg_ref)
            l_ref[...] = jnp.zeros_like(l_ref)

        @pl.when(kk == 0)
        def _():
            acc_ref[i] = jnp.zeros((tm, d), F32)

        acc_ref[i] += jnp.dot(a_ref[...], w_ref[...], preferred_element_type=F32)

        @pl.when(kk == nk - 1)
        def _():
            fo = acc_ref[i]
            gain = g_ref[...]
            r = lax.rsqrt(jnp.mean(fo * fo, axis=-1, keepdims=True) + EPS)
            xh = fo * r
            y = h_ref[...] + scale * (xh * gain)
            row = (grp * gsz + i) * tm + lax.broadcasted_iota(jnp.int32, (tm, 1), 0)
            e = jnp.where(row >= lead, y - t_ref[...], 0.0)
            dy = e * (1.0 / d)
            dy_ref[...] = dy
            l_ref[...] += 0.5 * jnp.sum(jnp.sum(e * e, axis=-1, keepdims=True) * (1.0 / d), axis=0, keepdims=True)
            dn = scale * dy
            dyh = dn * gain
            dfo_ref[...] = (r * (dyh - xh * jnp.mean(dyh * xh, axis=-1, keepdims=True))).astype(dfo_ref.dtype)
            dg_ref[...] += jnp.sum(dn * xh, axis=0, keepdims=True)

    row = pl.BlockSpec((tm, d), lambda grp, kk, i: (epilogue_row(grp, kk, i), 0))
    row_once = pl.BlockSpec((tm, d), lambda grp, kk, i: (epilogue_row(grp, kk, i), 0), pipeline_mode=pl.Buffered(1))
    vec = pl.BlockSpec((1, d), lambda grp, kk, i: (0, 0))
    return _call(
        body, grid=(m // tm // gsz, nk, gsz),
        in_specs=[pl.BlockSpec((tm, tk), lambda grp, kk, i: (grp * gsz + i, kk)),
                  pl.BlockSpec((tk, d), lambda grp, kk, i: (kk, 0)), row_once, vec, row_once],
        out_specs=[row, row, vec, pl.BlockSpec((1, 1), lambda grp, kk, i: (0, 0))],
        out_shape=[jax.ShapeDtypeStruct((m, d), F32), jax.ShapeDtypeStruct((m, d), MXU_DTYPE),
                   jax.ShapeDtypeStruct((1, d), F32), jax.ShapeDtypeStruct((1, 1), F32)],
        scratch_shapes=[pltpu.VMEM((gsz, tm, d), F32)], name=name, args=(a, w, h, g, target), carried=carried)


def _norm_bwd_row_tile(m):
    return _tile(m, ACC_ROWS, SUBLANE_BF16)


def _mm_norm_bwd(pairs, h, g, dh_up, name, carried=(), row_tiles=None, dg_init=None, post=None):
    n_pairs = len(pairs)
    m, k = pairs[0][0].shape
    d = h.shape[1]
    tm = _norm_bwd_row_tile(m)
    tk = _tile(k, K_TILE, LANE)
    nk = k // tk
    t0, nt = row_tiles if row_tiles is not None else (0, m // tm)
    gsz, epilogue_row = _row_groups(nt, ACC_GROUP, nk)
    if dg_init is None:
        dg_init = jnp.zeros((1, d), F32)

    n_post = 0 if post is None else 2

    def body(*refs):
        ops = refs[:2 * n_pairs]
        h_ref, g_ref, up_ref, init_ref = refs[2 * n_pairs:2 * n_pairs + 4]
        post_in = refs[2 * n_pairs + 4:2 * n_pairs + 4 + n_post]
        dh_ref, dg_ref = refs[2 * n_pairs + 4 + n_post:2 * n_pairs + 6 + n_post]
        post_out = refs[2 * n_pairs + 6 + n_post:2 * n_pairs + 6 + 2 * n_post]
        acc_ref = refs[-1]
        grp, kk, i = pl.program_id(0), pl.program_id(1), pl.program_id(2)

        @pl.when(jnp.logical_and(jnp.logical_and(grp == 0, kk == 0), i == 0))
        def _():
            dg_ref[...] = init_ref[...]
            if post is not None:
                post_out[1][...] = jnp.zeros_like(post_out[1])

        @pl.when(kk == 0)
        def _():
            acc_ref[i] = jnp.zeros((tm, d), F32)

        for p in range(n_pairs):
            acc_ref[i] += jnp.dot(ops[2 * p][...], ops[2 * p + 1][...], preferred_element_type=F32)

        @pl.when(kk == nk - 1)
        def _():
            dx, dgr = _rmsnorm_bwd_rows(h_ref[...], g_ref[...], acc_ref[i])
            dh = up_ref[...] + dx
            dh_ref[...] = dh
            dg_ref[...] += jnp.sum(dgr, axis=0, keepdims=True)
            if post is not None:
                dfo, dpr = _rmsnorm_bwd_rows(post_in[0][...], post_in[1][...], post[2] * dh)
                post_out[0][...] = dfo.astype(post_out[0].dtype)
                post_out[1][...] += jnp.sum(dpr, axis=0, keepdims=True)

    row_in = pl.BlockSpec((tm, d), lambda grp, kk, i: (t0 + epilogue_row(grp, kk, i), 0))
    row_out = pl.BlockSpec((tm, d), lambda grp, kk, i: (epilogue_row(grp, kk, i), 0))
    vec = pl.BlockSpec((1, d), lambda grp, kk, i: (0, 0))
    in_specs = []
    args = []
    for a, w in pairs:
        in_specs += [pl.BlockSpec((tm, tk), lambda grp, kk, i: (t0 + grp * gsz + i, kk)),
                     pl.BlockSpec((tk, d), lambda grp, kk, i: (kk, 0))]
        args += [a, w]
    in_specs += [row_in, vec, row_in, vec]
    args += [h, g, dh_up, dg_init]
    out_specs = [row_out, vec]
    out_shape = [jax.ShapeDtypeStruct((nt * tm, d), F32), jax.ShapeDtypeStruct((1, d), F32)]
    if post is not None:
        in_specs += [row_in, vec]
        args += [post[0], post[1]]
        out_specs += [row_out, vec]
        out_shape += [jax.ShapeDtypeStruct((nt * tm, d), MXU_DTYPE), jax.ShapeDtypeStruct((1, d), F32)]
    return _call(
        body, grid=(nt // gsz, nk, gsz), in_specs=in_specs, out_specs=out_specs, out_shape=out_shape,
        scratch_shapes=[pltpu.VMEM((gsz, tm, d), F32)], name=name, args=tuple(args), carried=carried)


def _mm_tn(a, b, name, carried=()):
    m, ka = a.shape
    d = b.shape[1]
    tf = _tile(ka, 512, LANE)

    def body(a_ref, b_ref, o_ref):
        o_ref[...] = _dot_tn(a_ref[...], b_ref[...]).astype(o_ref.dtype)

    return _call(
        body, grid=(ka // tf,),
        in_specs=[pl.BlockSpec((m, tf), lambda j: (0, j)),
                  pl.BlockSpec((m, d), lambda j: (0, 0), pipeline_mode=pl.Buffered(1))],
        out_specs=pl.BlockSpec((tf, d), lambda j: (j, 0)),
        out_shape=jax.ShapeDtypeStruct((ka, d), WIRE_DTYPE), name=name, args=(a, b), carried=carried)


GELU_K = 0.7978845608028654
GELU_C = 0.044715


def _expm1(x):
    series = x * (1.0 + x * (1.0 / 2 + x * (1.0 / 6 + x * (1.0 / 24 + x * (1.0 / 120)))))
    return jnp.where(jnp.abs(x) < 0.1, series, jnp.exp(x) - 1.0)


def _softplus(x):
    return jnp.maximum(x, 0.0) + jnp.log1p(jnp.exp(-jnp.abs(x)))


def _block_mm(v, w_ref, transposed):
    nbk = w_ref.shape[0]
    outs = []
    for j in range(nbk):
        vj = v[:, j * BD:(j + 1) * BD]
        outs.append(_dot_nt(vj, w_ref[j]) if transposed else jnp.dot(vj, w_ref[j], preferred_element_type=F32))
    return outs[0] if nbk == 1 else jnp.concatenate(outs, axis=1)


def _group_mean(q, gm_ref):
    hi = q.astype(MXU_DTYPE)
    lo = (q - hi.astype(F32)).astype(MXU_DTYPE)
    nbk = q.shape[1] // BD
    gm = gm_ref[...]
    outs = []
    for j in range(nbk):
        sl = slice(j * BD, (j + 1) * BD)
        outs.append(jnp.dot(hi[:, sl], gm, preferred_element_type=F32) + jnp.dot(lo[:, sl], gm, preferred_element_type=F32))
    return outs[0] if nbk == 1 else jnp.concatenate(outs, axis=1)


class _RowReader:
    def __init__(self, ref):
        self.ref = ref

    def __getitem__(self, rows):
        return self.ref[rows, :]


def _shifted(ext_ref, cur, before8, after8, downs=(), ups=()):
    r = cur.shape[0]
    if downs:
        ext_ref[0:8, :] = before8
    ext_ref[8:8 + r, :] = cur
    if ups:
        ext_ref[8 + r:16 + r, :] = after8
    return [ext_ref[pl.ds(8 - j, r), :] for j in downs] + [ext_ref[pl.ds(8 + j, r), :] for j in ups]


def _lru_gates(xc, pv, wa_ref, wx_ref):
    xcb = xc.astype(MXU_DTYPE)
    ga = jax.nn.sigmoid(_block_mm(xcb, wa_ref, False) + pv[5:6])
    gx = jax.nn.sigmoid(_block_mm(xcb, wx_ref, False) + pv[6:7])
    sp = _softplus(-pv[7:8])
    log_a = -LRU_C * ga * sp
    a = jnp.exp(log_a)
    e2 = _expm1(2.0 * log_a)
    mult = jnp.sqrt(-e2)
    return xcb, ga, gx, sp, a, e2, mult


def _gelu_parts(y):
    th = jnp.tanh(GELU_K * (y + GELU_C * y * y * y))
    return 0.5 * y * (1.0 + th), th


def _scan_block(a, u, sa_ref, su_ref, carry_ref, out_ref, reverse):
    r, c = a.shape
    n = r // 8
    a3 = a.reshape(n, 8, c)
    u3 = u.reshape(n, 8, c)
    sub = lax.broadcasted_iota(jnp.int32, (n, 8, c), 1)
    for dlt in (1, 2, 4):
        keep = (sub < 8 - dlt) if reverse else (sub >= dlt)
        shift = 8 - dlt if reverse else dlt
        sh_a = pltpu.roll(a3, shift, axis=1)
        sh_u = pltpu.roll(u3, shift, axis=1)
        u3 = u3 + a3 * jnp.where(keep, sh_u, 0.0)
        a3 = a3 * jnp.where(keep, sh_a, 1.0)
    sa_ref[...] = a3.reshape(r, c)
    su_ref[...] = u3.reshape(r, c)
    for k in (range(n - 1, -1, -1) if reverse else range(n)):
        rows = pl.ds(8 * k, 8)
        out_ref[rows, :] = su_ref[rows, :] + sa_ref[rows, :] * carry_ref[...]
        carry_ref[...] = out_ref[pl.ds(8 * k if reverse else 8 * k + 7, 1), :]


def _mixer_fwd(z, pv, wa, wx, gm, pad, name, carried=()):
    m = z.shape[0]
    c = pv.shape[1]
    r = MIX_ROWS
    nb = m // r

    def body(z_ref, pv_ref, wa_ref, wx_ref, gm_ref, mixed_ref, hs_ref, ext_ref, tailx_ref, tailc_ref, carry_ref,
             sa_ref, su_ref):
        b = pl.program_id(0)

        @pl.when(b == 0)
        def _():
            tailx_ref[...] = jnp.zeros_like(tailx_ref)
            tailc_ref[...] = jnp.zeros_like(tailc_ref)
            carry_ref[...] = jnp.zeros_like(carry_ref)

        pv = _RowReader(pv_ref)
        row = b * r + lax.broadcasted_iota(jnp.int32, (r, 1), 0)
        maskf = (row >= pad).astype(F32)
        y = z_ref[:, 0:c]
        xl = z_ref[:, c:2 * c]
        bs = z_ref[:, 2 * c:3 * c]
        cv = z_ref[:, 3 * c:4 * c] * z_ref[:, 4 * c:5 * c]

        x1, x2, x3 = _shifted(ext_ref, xl, tailx_ref[...], None, downs=(1, 2, 3))
        tailx_ref[...] = z_ref[pl.ds(r - 8, 8), c:2 * c]
        xc = pv[4:5] + pv[3:4] * xl + pv[2:3] * x1 + pv[1:2] * x2 + pv[0:1] * x3
        _, _, gx, _, a, _, mult = _lru_gates(xc, pv, wa_ref, wx_ref)
        uu = mult * (gx * xc) * maskf

        _scan_block(a, uu, sa_ref, su_ref, carry_ref, hs_ref, reverse=False)
        hs = hs_ref[...]

        gelu_y, _ = _gelu_parts(y)
        lru_out = hs * gelu_y
        c1, c2 = _shifted(ext_ref, cv, tailc_ref[...], None, downs=(1, 2))
        tailc_ref[...] = cv[r - 8:r]
        sc_out = bs * (pv[10:11] * cv + pv[9:10] * c1 + pv[8:9] * c2)

        rl = lax.rsqrt(_group_mean(lru_out * lru_out, gm_ref) + EPS)
        rs = lax.rsqrt(_group_mean(sc_out * sc_out, gm_ref) + EPS)
        mixed_ref[:, 0:c] = (lru_out * rl * pv[11:12]).astype(mixed_ref.dtype)
        mixed_ref[:, c:2 * c] = (sc_out * rs * pv[12:13]).astype(mixed_ref.dtype)

    full = lambda shape: pl.BlockSpec(shape, lambda b: (0,) * len(shape))
    return _call(
        body, grid=(nb,),
        in_specs=[pl.BlockSpec((r, 5 * c), lambda b: (b, 0)), full(pv.shape), full(wa.shape), full(wx.shape), full(gm.shape)],
        out_specs=[pl.BlockSpec((r, 2 * c), lambda b: (b, 0)), pl.BlockSpec((r, c), lambda b: (b, 0))],
        out_shape=[jax.ShapeDtypeStruct((m, 2 * c), MXU_DTYPE), jax.ShapeDtypeStruct((m, c), F32)],
        scratch_shapes=[pltpu.VMEM((r + 16, c), F32), pltpu.VMEM((8, c), F32), pltpu.VMEM((8, c), F32),
                        pltpu.VMEM((1, c), F32), pltpu.VMEM((r, c), F32), pltpu.VMEM((r, c), F32)],
        name=name, args=(z, pv, wa, wx, gm), carried=carried)


def _mixer_bwd(z, hs, dmixed, pv, wa, wx, gm, pad, name, carried=()):
    m = z.shape[0]
    c = pv.shape[1]
    r = MIX_ROWS
    nb = m // r
    r8 = r // 8
    assert pad <= r and pad % SUBLANE_BF16 == 0

    def body(z_ref, zp_ref, hs_ref, hsp_ref, dm_ref, pv_ref, wa_ref, wx_ref, gm_ref,
             dz_ref, dpv_ref, dwa_ref, dwx_ref, ext_ref, hxc_ref, hsc_ref, hp_ref, pc_ref, sa_ref, su_ref, p_ref):
        i = pl.program_id(0)
        b = nb - 1 - i

        @pl.when(i == 0)
        def _():
            hxc_ref[...] = jnp.zeros_like(hxc_ref)
            hsc_ref[...] = jnp.zeros_like(hsc_ref)
            hp_ref[...] = jnp.zeros_like(hp_ref)
            pc_ref[...] = jnp.zeros_like(pc_ref)
            dpv_ref[...] = jnp.zeros_like(dpv_ref)
            dwa_ref[...] = jnp.zeros_like(dwa_ref)
            dwx_ref[...] = jnp.zeros_like(dwx_ref)

        pv = _RowReader(pv_ref)
        row = b * r + lax.broadcasted_iota(jnp.int32, (r, 1), 0)
        maskf = (row >= pad).astype(F32)
        has_prev = (b > 0).astype(F32)
        y = z_ref[:, 0:c]
        xl = z_ref[:, c:2 * c]
        bs = z_ref[:, 2 * c:3 * c]
        cs = z_ref[:, 3 * c:4 * c]
        vs = z_ref[:, 4 * c:5 * c]
        cv = cs * vs
        xl_prev = zp_ref[:, c:2 * c] * has_prev
        cv_prev = zp_ref[:, 3 * c:4 * c] * zp_ref[:, 4 * c:5 * c] * has_prev
        hs = hs_ref[...]

        x1, x2, x3 = _shifted(ext_ref, xl, xl_prev, None, downs=(1, 2, 3))
        xc = pv[4:5] + pv[3:4] * xl + pv[2:3] * x1 + pv[1:2] * x2 + pv[0:1] * x3
        xcb, ga, gx, sp, a, e2, mult = _lru_gates(xc, pv, wa_ref, wx_ref)
        gxx = gx * xc
        gelu_y, th = _gelu_parts(y)
        lru_out = hs * gelu_y
        c1, c2 = _shifted(ext_ref, cv, cv_prev, None, downs=(1, 2))
        sc = pv[10:11] * cv + pv[9:10] * c1 + pv[8:9] * c2
        sc_out = bs * sc

        def group_norm_bwd(v, dm, gain):
            rr = lax.rsqrt(_group_mean(v * v, gm_ref) + EPS)
            vh = v * rr
            dvh = dm * gain
            dv = rr * (dvh - vh * _group_mean(dvh * vh, gm_ref))
            return dv, jnp.sum(dm * vh, axis=0, keepdims=True)

        d_lru_out, d_og = group_norm_bwd(lru_out, dm_ref[:, 0:c], pv[11:12])
        d_sc_out, d_sg = group_norm_bwd(sc_out, dm_ref[:, c:2 * c], pv[12:13])
        dpv_ref[11:12, :] += d_og
        dpv_ref[12:13, :] += d_sg

        dhs = d_lru_out * gelu_y
        dgelu = 0.5 * (1.0 + th) + 0.5 * y * (1.0 - th * th) * GELU_K * (1.0 + 3.0 * GELU_C * y * y)
        dy = d_lru_out * hs * dgelu

        _scan_block(a, a * dhs, sa_ref, su_ref, pc_ref, p_ref, reverse=True)
        (p_next,) = _shifted(ext_ref, p_ref[...], None, hp_ref[...], ups=(1,))
        hp_ref[...] = p_ref[0:8, :]
        q = dhs + p_next
        (hs_prev,) = _shifted(ext_ref, hs, hsp_ref[...] * has_prev, None, downs=(1,))
        duu = q * maskf
        da = q * hs_prev

        dmult = duu * gxx
        dgxx = duu * mult
        dgx = dgxx * xc
        dxc = dgxx * gx
        dlog_a = da * a - dmult * ((1.0 + e2) / mult)
        dga = dlog_a * (-LRU_C * sp)
        dsp = jnp.sum(dlog_a * (-LRU_C * ga), axis=0, keepdims=True)
        dpv_ref[7:8, :] += dsp * (-jax.nn.sigmoid(-pv[7:8]))
        dga_pre = dga * ga * (1.0 - ga)
        dgx_pre = dgx * gx * (1.0 - gx)
        dpv_ref[5:6, :] += jnp.sum(dga_pre, axis=0, keepdims=True)
        dpv_ref[6:7, :] += jnp.sum(dgx_pre, axis=0, keepdims=True)
        dga_b = dga_pre.astype(MXU_DTYPE)
        dgx_b = dgx_pre.astype(MXU_DTYPE)
        dxc = dxc + _block_mm(dga_b, wa_ref, True) + _block_mm(dgx_b, wx_ref, True)
        for j in range(c // BD):
            sl = slice(j * BD, (j + 1) * BD)
            dwa_ref[j] += _dot_tn(xcb[:, sl], dga_b[:, sl])
            dwx_ref[j] += _dot_tn(xcb[:, sl], dgx_b[:, sl])

        dpv_ref[4:5, :] += jnp.sum(dxc, axis=0, keepdims=True)
        dpv_ref[3:4, :] += jnp.sum(dxc * xl, axis=0, keepdims=True)
        dpv_ref[2:3, :] += jnp.sum(dxc * x1, axis=0, keepdims=True)
        dpv_ref[1:2, :] += jnp.sum(dxc * x2, axis=0, keepdims=True)
        dpv_ref[0:1, :] += jnp.sum(dxc * x3, axis=0, keepdims=True)
        u1, u2, u3 = _shifted(ext_ref, dxc, None, hxc_ref[...], ups=(1, 2, 3))
        hxc_ref[...] = dxc[0:8]
        dxl = pv[3:4] * dxc + pv[2:3] * u1 + pv[1:2] * u2 + pv[0:1] * u3

        dbs = d_sc_out * sc
        dsc = d_sc_out * bs
        dpv_ref[10:11, :] += jnp.sum(dsc * cv, axis=0, keepdims=True)
        dpv_ref[9:10, :] += jnp.sum(dsc * c1, axis=0, keepdims=True)
        dpv_ref[8:9, :] += jnp.sum(dsc * c2, axis=0, keepdims=True)
        s1, s2 = _shifted(ext_ref, dsc, None, hsc_ref[...], ups=(1, 2))
        hsc_ref[...] = dsc[0:8]
        dcv = pv[10:11] * dsc + pv[9:10] * s1 + pv[8:9] * s2

        dz_ref[:, 0:c] = dy.astype(dz_ref.dtype)
        dz_ref[:, c:2 * c] = dxl.astype(dz_ref.dtype)
        dz_ref[:, 2 * c:3 * c] = dbs.astype(dz_ref.dtype)
        dz_ref[:, 3 * c:4 * c] = (dcv * vs).astype(dz_ref.dtype)
        dz_ref[:, 4 * c:5 * c] = (dcv * cs).astype(dz_ref.dtype)

        if pad:
            @pl.when(b == 0)
            def _():
                dz_ref[0:pad, :] = jnp.zeros((pad, 5 * c), dz_ref.dtype)

    full = lambda shape: pl.BlockSpec(shape, lambda i: (0,) * len(shape))
    cur = lambda width: pl.BlockSpec((r, width), lambda i: (nb - 1 - i, 0))
    prev8 = lambda width: pl.BlockSpec((8, width), lambda i: (jnp.maximum((nb - 1 - i) * r8 - 1, 0), 0))
    return _call(
        body, grid=(nb,),
        in_specs=[cur(5 * c), prev8(5 * c), cur(c), prev8(c), cur(2 * c),
                  full(pv.shape), full(wa.shape), full(wx.shape), full(gm.shape)],
        out_specs=[cur(5 * c), full(pv.shape), full(wa.shape), full(wx.shape)],
        out_shape=[jax.ShapeDtypeStruct((m, 5 * c), MXU_DTYPE), jax.ShapeDtypeStruct(pv.shape, F32),
                   jax.ShapeDtypeStruct(wa.shape, F32), jax.ShapeDtypeStruct(wx.shape, F32)],
        scratch_shapes=[pltpu.VMEM((r + 16, c), F32), pltpu.VMEM((8, c), F32), pltpu.VMEM((8, c), F32),
                        pltpu.VMEM((8, c), F32), pltpu.VMEM((1, c), F32), pltpu.VMEM((r, c), F32),
                        pltpu.VMEM((r, c), F32), pltpu.VMEM((r, c), F32)],
        name=name, args=(z, z, hs, hs, dmixed, pv, wa, wx, gm), carried=carried)


def _position():
    return lax.axis_index("x"), lax.axis_index("y"), lax.axis_index("c")


def _block_of(px, py, pc):
    return 4 * px + 2 * py + pc


class _TwoLevelGather:
    def __init__(self, n_arrays, rows_of, src_of, send_sems, recv_sems):
        x, y, c = _position()
        self.n, self.rows_of, self.src_of = n_arrays, rows_of, src_of
        self.send_sems, self.recv_sems = send_sems, recv_sems
        self.c, self.me, self.sibling = c, (x, y, c), (x, y, 1 - c)
        self.chips = [(1 - x, y), (x, 1 - y), (1 - x, 1 - y)]

    def _copy(self, i, k, block, to, src=None):
        return pltpu.make_async_remote_copy(
            src_ref=self.rows_of(i, *block) if src is None else src, dst_ref=self.rows_of(i, *block),
            send_sem=self.send_sems.at[7 * i + k], recv_sem=self.recv_sems.at[7 * i + k],
            device_id=to, device_id_type=MESH)

    def _first(self, i):
        own = [self._copy(i, 0, self.me, self.sibling, src=self.src_of(i))]
        return own + [self._copy(i, 1 + j, self.me, (*chip, self.c), src=self.src_of(i))
                      for j, chip in enumerate(self.chips)]

    def _passed(self, i, j):
        return self._copy(i, 4 + j, (*self.chips[j], self.c), self.sibling)

    def start(self):
        for i in range(self.n):
            for cp in self._first(i):
                cp.start()

    def forward(self):
        for i in range(self.n):
            for j, chip in enumerate(self.chips):
                self._copy(i, 1 + j, (*chip, self.c), self.me).wait_recv()
                self._passed(i, j).start()

    def drain(self):
        for i in range(self.n):
            self._copy(i, 0, self.sibling, self.me).wait_recv()
            for j, chip in enumerate(self.chips):
                self._copy(i, 4 + j, (*chip, 1 - self.c), self.me).wait_recv()
        for i in range(self.n):
            for cp in self._first(i) + [self._passed(i, j) for j in range(3)]:
                cp.wait_send()


class _RelayGather:
    def __init__(self, n_arrays, rows_of, src_of, send_sems, recv_sems):
        x, y, c = _position()
        self.n, self.rows_of, self.src_of = n_arrays, rows_of, src_of
        self.send_sems, self.recv_sems = send_sems, recv_sems
        self.me, self.sibling = (x, y, c), (x, y, 1 - c)
        self.xn, self.yn, self.dg = (1 - x, y, c), (x, 1 - y, c), (1 - x, 1 - y, c)

    def _copy(self, i, k, block, to, half=None, src=None):
        rows = self.rows_of(i, *block, half)
        return pltpu.make_async_remote_copy(
            src_ref=rows if src is None else src, dst_ref=rows,
            send_sem=self.send_sems.at[8 * i + k], recv_sem=self.recv_sems.at[8 * i + k],
            device_id=to, device_id_type=MESH)

    def _sends(self, i):
        own = self.src_of(i)
        return [self._copy(i, 0, self.me, self.sibling, src=own), self._copy(i, 1, self.me, self.xn, src=own),
                self._copy(i, 2, self.me, self.yn, src=own),
                self._copy(i, 3, self.xn, self.yn, half=0), self._copy(i, 4, self.yn, self.xn, half=1),
                self._copy(i, 5, self.xn, self.sibling), self._copy(i, 6, self.yn, self.sibling),
                self._copy(i, 7, self.dg, self.sibling)]

    def start(self):
        for i in range(self.n):
            for cp in self._sends(i)[0:3]:
                cp.start()

    def forward(self):
        for i in range(self.n):
            self._copy(i, 1, self.xn, self.me).wait_recv()
            self._copy(i, 2, self.yn, self.me).wait_recv()
            for cp in self._sends(i)[3:7]:
                cp.start()

    def drain(self):
        x, y, c = self.me
        for i in range(self.n):
            self._copy(i, 3, self.dg, self.me, half=0).wait_recv()
            self._copy(i, 4, self.dg, self.me, half=1).wait_recv()
            self._sends(i)[7].start()
        for i in range(self.n):
            self._copy(i, 0, self.sibling, self.me).wait_recv()
            self._copy(i, 5, (1 - x, y, 1 - c), self.me).wait_recv()
            self._copy(i, 6, (x, 1 - y, 1 - c), self.me).wait_recv()
            self._copy(i, 7, (1 - x, 1 - y, 1 - c), self.me).wait_recv()
        for i in range(self.n):
            for cp in self._sends(i):
                cp.wait_send()


class _CarriedGather:
    def __init__(self, shards, padded_rows, zeros, forward_at, part=None, into=None):
        d = shards[0].shape[1]
        self.forward_at = forward_at
        self.n = len(shards)
        self.rows = [s.shape[0] for s in shards]
        self.pads = [p - N_DEV * r for r, p in zip(self.rows, padded_rows)]
        assert max(self.pads) <= zeros.shape[0] and zeros.shape[1] == d
        self.part = part if part is not None else (0, self.rows[0])
        assert (part is None and into is None) or self.n == 1
        assert self.part[0] % SUBLANE_BF16 == 0 and self.part[1] % SUBLANE_BF16 == 0
        self.arrays = list(shards) + [zeros] + ([into] if into is not None else [])
        self.out_shapes = [jax.ShapeDtypeStruct((p, d), s.dtype) for s, p in zip(shards, padded_rows)]
        self.aliases = {self.n + 1: 0} if into is not None else {}
        if into is not None:
            self.pads = [0] * self.n
        self.n_remote, self.n_local = 8 * self.n, 2 * self.n
        self.results = None

    def _rows_of(self, outs):
        def rows_of(i, px, py, pc, half):
            first, count = (self.part if self.n == 1 else (0, self.rows[i]))
            head = _round_up(count // 2, SUBLANE_BF16)
            if half == 0:
                count = head
            elif half == 1:
                first, count = first + head, count - head
            first = _block_of(px, py, pc) * self.rows[i] + first
            return outs[i].at[pl.ds(pl.multiple_of(first, SUBLANE_BF16), count), :]
        return rows_of

    def _own(self, ins, i):
        return ins[i].at[pl.ds(self.part[0], self.part[1]), :] if self.n == 1 else ins[i]

    def _gather(self, ins, outs, send_sems, recv_sems):
        return _RelayGather(self.n, self._rows_of(outs), functools.partial(self._own, ins), send_sems, recv_sems)

    def _local(self, ins, outs, local_sems):
        x, y, c = _position()
        rows_of = self._rows_of(outs)
        cps = []
        for i in range(self.n):
            cps.append(pltpu.make_async_copy(self._own(ins, i), rows_of(i, x, y, c, None), local_sems.at[2 * i]))
            if self.pads[i]:
                cps.append(pltpu.make_async_copy(ins[self.n].at[pl.ds(0, self.pads[i]), :],
                                                 outs[i].at[pl.ds(N_DEV * self.rows[i], self.pads[i]), :],
                                                 local_sems.at[2 * i + 1]))
        return cps

    def start(self, ins, outs, send_sems, recv_sems, local_sems):
        for cp in self._local(ins, outs, local_sems):
            cp.start()
        self._gather(ins, outs, send_sems, recv_sems).start()

    def forward(self, ins, outs, send_sems, recv_sems, local_sems):
        self._gather(ins, outs, send_sems, recv_sems).forward()

    def finish(self, ins, outs, send_sems, recv_sems, local_sems):
        self._gather(ins, outs, send_sems, recv_sems).drain()
        for cp in self._local(ins, outs, local_sems):
            cp.wait()


class _CarriedSwap:
    def __init__(self, grads, shard_rows):
        d = grads[0].shape[1]
        self.n, self.rows = len(grads), list(shard_rows)
        self.arrays = list(grads)
        self.out_shapes = [jax.ShapeDtypeStruct((4, s, d), g.dtype) for g, s in zip(grads, shard_rows)]
        self.aliases = {}
        self.n_remote, self.n_local = 4 * self.n, 0
        self.forward_at = 1.0
        self.results = None

    def _copies(self, ins, outs, send_sems, recv_sems):
        x, y, c = _position()
        cps = []
        for i in range(self.n):
            s = self.rows[i]
            for k in range(4):
                blk = _block_of(k >> 1, k & 1, 1 - c)
                cps.append(pltpu.make_async_remote_copy(
                    src_ref=ins[i].at[pl.ds(pl.multiple_of(blk * s, SUBLANE_BF16), s), :], dst_ref=outs[i].at[k],
                    send_sem=send_sems.at[4 * i + k], recv_sem=recv_sems.at[4 * i + k],
                    device_id=(x, y, 1 - c), device_id_type=MESH))
        return cps

    def start(self, ins, outs, send_sems, recv_sems, local_sems):
        for cp in self._copies(ins, outs, send_sems, recv_sems):
            cp.start()

    def forward(self, *_):
        pass

    def finish(self, ins, outs, send_sems, recv_sems, local_sems):
        for cp in self._copies(ins, outs, send_sems, recv_sems):
            cp.wait()


class _CarriedChipExchange:
    def __init__(self, presums, part=None, into=None):
        self.n = len(presums)
        assert (part is None and into is None) or self.n == 1
        self.part = part if part is not None else (0, presums[0].shape[1])
        assert self.part[0] % SUBLANE_BF16 == 0 and self.part[1] % SUBLANE_BF16 == 0
        self.arrays = list(presums) + ([into] if into is not None else [])
        self.out_shapes = [jax.ShapeDtypeStruct(p.shape, p.dtype) for p in presums]
        self.aliases = {self.n: 0} if into is not None else {}
        self.n_remote, self.n_local = 3 * self.n, 0
        self.forward_at = 1.0
        self.results = None

    def _copies(self, ins, outs, send_sems, recv_sems):
        x, y, c = _position()
        cps = []
        for i in range(self.n):
            rows = pl.ds(*self.part) if self.n == 1 else pl.ds(0, self.arrays[i].shape[1])
            for r in range(1, 4):
                cps.append(pltpu.make_async_remote_copy(
                    src_ref=ins[i].at[r - 1, rows, :], dst_ref=outs[i].at[r - 1, rows, :],
                    send_sem=send_sems.at[3 * i + r - 1], recv_sem=recv_sems.at[3 * i + r - 1],
                    device_id=(x ^ (r >> 1), y ^ (r & 1), c), device_id_type=MESH))
        return cps

    def start(self, ins, outs, send_sems, recv_sems, local_sems):
        for cp in self._copies(ins, outs, send_sems, recv_sems):
            cp.start()

    def forward(self, *_):
        pass

    def finish(self, ins, outs, send_sems, recv_sems, local_sems):
        for cp in self._copies(ins, outs, send_sems, recv_sems):
            cp.wait()


def _gather_small(block, reduce, name):
    rr, nn = block.shape

    def body(x_ref, out_ref, *rest):
        if reduce:
            stack_ref, send_sems, recv_sems, local_sem = rest
        else:
            send_sems, recv_sems, local_sem = rest
            stack_ref = out_ref
        x, y, c = _position()

        def rows_of(i, px, py, pc):
            return stack_ref.at[pl.ds(pl.multiple_of(_block_of(px, py, pc) * rr, 8), rr), :]

        own = pltpu.make_async_copy(x_ref, rows_of(0, x, y, c), local_sem)
        own.start()
        gather = _TwoLevelGather(1, rows_of, lambda i: x_ref, send_sems, recv_sems)
        gather.start()
        gather.forward()
        gather.drain()
        own.wait()
        if reduce:
            acc = stack_ref[0:rr, :]
            for k in range(1, N_DEV):
                acc = acc + stack_ref[k * rr:(k + 1) * rr, :]
            out_ref[...] = acc

    vmem = pl.BlockSpec(memory_space=pltpu.VMEM)
    scratch = [pltpu.SemaphoreType.DMA((7,)), pltpu.SemaphoreType.DMA((7,)), pltpu.SemaphoreType.DMA]
    if reduce:
        scratch = [pltpu.VMEM((N_DEV * rr, nn), F32)] + scratch
    out_rows = rr if reduce else N_DEV * rr
    return pl.pallas_call(
        body, in_specs=[vmem], out_specs=vmem, out_shape=jax.ShapeDtypeStruct((out_rows, nn), F32),
        scratch_shapes=scratch, name=name, compiler_params=_params())(block)


def _sum_stack(stack, name):
    rr = stack.shape[0] // N_DEV

    def body(s_ref, o_ref):
        acc = s_ref[0:rr, :]
        for k in range(1, N_DEV):
            acc = acc + s_ref[k * rr:(k + 1) * rr, :]
        o_ref[...] = acc

    vmem = pl.BlockSpec(memory_space=pltpu.VMEM)
    return pl.pallas_call(body, in_specs=[vmem], out_specs=vmem,
                          out_shape=jax.ShapeDtypeStruct((rr, stack.shape[1]), F32), name=name,
                          compiler_params=_params())(stack)


def _presum(where, grad, swapped, name):
    s, d = swapped.shape[1], swapped.shape[2]
    tc = _tile(d, 2048, LANE)

    def body(where_ref, g_ref, sw_ref, o_ref):
        o_ref[0] = (g_ref[...].astype(F32) + sw_ref[0].astype(F32)).astype(o_ref.dtype)

    return _call(
        body, grid=(3, d // tc),
        in_specs=[pl.BlockSpec((s, tc), lambda r, j, where: (where[1 + r], j)),
                  pl.BlockSpec((1, s, tc), lambda r, j, where: (where[5 + r], 0, j))],
        out_specs=pl.BlockSpec((1, s, tc), lambda r, j, where: (r, 0, j)),
        out_shape=jax.ShapeDtypeStruct((3, s, d), WIRE_DTYPE), name=name, args=(grad, swapped), prefetch=(where,))


def _final_sum(where, grad, swapped, received, name, carried=()):
    s, d = swapped.shape[1], swapped.shape[2]
    tc = _tile(d, 512, LANE)

    def body(where_ref, g_ref, sw_ref, r_ref, o_ref):
        acc = g_ref[...].astype(F32) + sw_ref[0].astype(F32)
        for k in range(3):
            acc = acc + r_ref[k].astype(F32)
        o_ref[...] = acc

    return _call(
        body, grid=(d // tc,),
        in_specs=[pl.BlockSpec((s, tc), lambda j, where: (where[0], j)),
                  pl.BlockSpec((1, s, tc), lambda j, where: (where[4], 0, j)),
                  pl.BlockSpec((3, s, tc), lambda j, where: (0, 0, j))],
        out_specs=pl.BlockSpec((s, tc), lambda j, where: (0, j)),
        out_shape=jax.ShapeDtypeStruct((s, d), F32), name=name, args=(grad, swapped, received),
        prefetch=(where,), carried=carried)


class _GradReduction:
    def __init__(self, key, grad, shard_rows, where):
        self.key, self.grad, self.rows, self.where = key, grad, shard_rows, where
        self._presum = self._exchange = None

    def swap(self):
        self._swap = _CarriedSwap([self.grad], [self.rows])
        return self._swap

    def exchange(self, part=None):
        if self._presum is None:
            self._presum = _presum(self.where, self.grad, self._swap.results[0], "presum_" + self.key)
        rows = None
        if part is not None:
            half = _round_up(self.rows // 2, SUBLANE_BF16)
            rows = (0, half) if part == 0 else (half, self.rows - half)
        into = self._exchange.results[0] if part == 1 else None
        self._exchange = _CarriedChipExchange([self._presum], rows, into)
        return self._exchange

    def total(self, carried=()):
        return _final_sum(self.where, self.grad, self._swap.results[0], self._exchange.results[0],
                          "sum_" + self.key, carried)

    def total_and_update(self, w, m, v):
        return _sum_adamw(self.where, self.grad, self._swap.results[0], self._exchange.results[0], w, m, v,
                          "update_" + self.key)


def _adamw_math(w, g, m, v):
    nm = ADAM_B1 * m + (1.0 - ADAM_B1) * g
    nv = ADAM_B2 * v + (1.0 - ADAM_B2) * (g * g)
    m_hat = nm / (1.0 - ADAM_B1 ** ADAM_STEP)
    v_hat = nv / (1.0 - ADAM_B2 ** ADAM_STEP)
    return -ADAM_LR * (m_hat / (jnp.sqrt(v_hat) + ADAM_EPS) + ADAM_WD * w), nm, nv


def _sum_adamw(where, grad, swapped, received, w, m, v, name):
    s, d = swapped.shape[1], swapped.shape[2]
    tc = _tile(d, 512, LANE)

    def body(where_ref, g_ref, sw_ref, r_ref, w_ref, m_ref, v_ref, gs_ref, d_ref, nm_ref, nv_ref):
        g = g_ref[...].astype(F32) + sw_ref[0].astype(F32)
        for k in range(3):
            g = g + r_ref[k].astype(F32)
        gs_ref[...] = g
        d_ref[...], nm_ref[...], nv_ref[...] = _adamw_math(w_ref[...], g, m_ref[...], v_ref[...])

    blk = pl.BlockSpec((s, tc), lambda j, where: (0, j))
    return _call(
        body, grid=(d // tc,),
        in_specs=[pl.BlockSpec((s, tc), lambda j, where: (where[0], j)),
                  pl.BlockSpec((1, s, tc), lambda j, where: (where[4], 0, j)),
                  pl.BlockSpec((3, s, tc), lambda j, where: (0, 0, j)), blk, blk, blk],
        out_specs=[blk] * 4, out_shape=[jax.ShapeDtypeStruct((s, d), F32)] * 4, name=name,
        args=(grad, swapped, received, w, m, v), prefetch=(where,))


def _adamw(w, g, m, v, name):
    rows, cols = w.shape
    tr = _tile(rows, 256, 8)

    def body(w_ref, g_ref, m_ref, v_ref, d_ref, nm_ref, nv_ref):
        d_ref[...], nm_ref[...], nv_ref[...] = _adamw_math(w_ref[...], g_ref[...], m_ref[...], v_ref[...])

    spec = pl.BlockSpec((tr, cols), lambda i: (i, 0))
    return pl.pallas_call(
        body, grid=(rows // tr,), in_specs=[spec] * 4, out_specs=[spec] * 3,
        out_shape=[jax.ShapeDtypeStruct((rows, cols), F32)] * 3, name=name, compiler_params=_params())(w, g, m, v)


def _pack_rows(arrays, width, row_quantum=8):
    flat = jnp.concatenate([a.reshape(-1) for a in arrays])
    total = _round_up(flat.shape[0], row_quantum * width)
    flat = jnp.pad(flat, (0, total - flat.shape[0]))
    return flat.reshape(-1, width)


def _unpack_rows(packed, shapes):
    flat = packed.reshape(-1)
    out = []
    off = 0
    for shp in shapes:
        size = 1
        for s in shp:
            size *= s
        out.append(flat[off:off + size].reshape(shp))
        off += size
    return out


def _block_diag(w):
    h, hb, _ = w.shape
    per = BD // hb
    w4 = w.reshape(h // per, per, hb, hb)
    eye = jnp.eye(per, dtype=w.dtype)
    return jnp.einsum('npij,pq->npiqj', w4, eye).reshape(h // per, BD, BD)


def _block_diag_extract(bd, hb):
    nbk = bd.shape[0]
    per = BD // hb
    b5 = bd.reshape(nbk, per, hb, per, hb)
    eye = jnp.eye(per, dtype=bd.dtype)
    return jnp.einsum('npiqj,pq->npij', b5, eye).reshape(nbk * per, hb, hb)


def kernel(x, meta_tokens, ffn1_pre_g, ffn1_w_gate, ffn1_w_up, ffn1_w_down, ffn1_post_g, mix_pre_g, w_in, lru_conv_w, lru_conv_b, lru_w_a, lru_b_a, lru_w_x, lru_b_x, lru_lambda, sconv_w, lru_out_g, sconv_out_g, w_out, mix_post_g, ffn2_pre_g, ffn2_w_gate, ffn2_w_up, ffn2_w_down, ffn2_post_g, loss_target, m_meta_tokens, m_ffn1_pre_g, m_ffn1_w_gate, m_ffn1_w_up, m_ffn1_w_down, m_ffn1_post_g, m_mix_pre_g, m_w_in, m_lru_conv_w, m_lru_conv_b, m_lru_w_a, m_lru_b_a, m_lru_w_x, m_lru_b_x, m_lru_lambda, m_sconv_w, m_lru_out_g, m_sconv_out_g, m_w_out, m_mix_post_g, m_ffn2_pre_g, m_ffn2_w_gate, m_ffn2_w_up, m_ffn2_w_down, m_ffn2_post_g, v_meta_tokens, v_ffn1_pre_g, v_ffn1_w_gate, v_ffn1_w_up, v_ffn1_w_down, v_ffn1_post_g, v_mix_pre_g, v_w_in, v_lru_conv_w, v_lru_conv_b, v_lru_w_a, v_lru_b_a, v_lru_w_x, v_lru_b_x, v_lru_lambda, v_sconv_w, v_lru_out_g, v_sconv_out_g, v_w_out, v_mix_post_g, v_ffn2_pre_g, v_ffn2_w_gate, v_ffn2_w_up, v_ffn2_w_down, v_ffn2_post_g):
    given = dict(locals())
    wts = {n: given[n] for n in WEIGHT_NAMES}
    mom = {n: given["m_" + n] for n in WEIGHT_NAMES}
    var = {n: given["v_" + n] for n in WEIGHT_NAMES}

    xi, yi, ci = _position()
    me = _block_of(xi, yi, ci)
    x2 = x[0]
    seq, d = x2.shape
    n_meta = meta_tokens.shape[0]
    m_rows = _round_up(n_meta + seq, ROW_ALIGN)
    pad = m_rows - n_meta - seq
    lead = pad + n_meta
    c = lru_conv_b.shape[1]
    hb = lru_w_a.shape[-1]
    dm = meta_tokens.shape[1]
    cs_ = lru_conv_w.shape[2]
    kw4, kw3 = lru_conv_w.shape[1], sconv_w.shape[1]
    assert d == 2 * c and c % BD == 0 and BD % hb == 0 and cs_ <= dm and kw4 == 4 and kw3 == 3

    small = jnp.zeros((_round_up(n_meta + kw4 + kw3, 8), dm), F32)
    small = small.at[0:n_meta].set(meta_tokens)
    small = small.at[n_meta:n_meta + kw4, 0:cs_].set(lru_conv_w[0])
    small = small.at[n_meta + kw4:n_meta + kw4 + kw3, 0:cs_].set(sconv_w[0])
    sr = small.shape[0]
    small_all = _gather_small(small, False, "gather_small").reshape(N_DEV, sr, dm)
    meta_full = small_all[:, 0:n_meta, :].transpose(1, 0, 2).reshape(n_meta, d)
    conv_w_full = small_all[:, n_meta:n_meta + kw4, 0:cs_].transpose(1, 0, 2).reshape(kw4, c)
    sconv_w_full = small_all[:, n_meta + kw4:n_meta + kw4 + kw3, 0:cs_].transpose(1, 0, 2).reshape(kw3, c)

    big = ['ffn1_w_gate', 'ffn1_w_up', 'ffn1_w_down', 'w_in', 'w_out', 'ffn2_w_gate', 'ffn2_w_up', 'ffn2_w_down']
    col_sharded = {'ffn1_w_gate', 'ffn1_w_up', 'w_in', 'ffn2_w_gate', 'ffn2_w_up'}
    shards = []
    for nme in big:
        w = wts[nme][0].astype(WIRE_DTYPE)
        shards.append(w.T if nme in col_sharded else w)
    shard_rows = dict(zip(big, [s.shape[0] for s in shards]))
    zeros = jnp.zeros((F_ALIGN, d), WIRE_DTYPE)

    def gather(forward_at, *names, part=None, into=None):
        sel = [shards[big.index(nme)] for nme in names]
        padded = [_round_up(N_DEV * shard_rows[nme], LANE if nme in ('w_in', 'w_out') else F_ALIGN) for nme in names]
        return _CarriedGather(sel, padded, zeros, forward_at, part, into)

    pv = jnp.zeros((16, c), F32)
    pv = pv.at[0:4].set(conv_w_full).at[4].set(lru_conv_b[0]).at[5].set(lru_b_a[0]).at[6].set(lru_b_x[0])
    pv = pv.at[7].set(lru_lambda[0]).at[8:11].set(sconv_w_full).at[11].set(lru_out_g[0]).at[12].set(sconv_out_g[0])
    wa_bd = _block_diag(lru_w_a[0]).astype(MXU_DTYPE)
    wx_bd = _block_diag(lru_w_x[0]).astype(MXU_DTYPE)
    gs = c // N_GROUPS
    gidx = jnp.arange(BD) // gs
    gm = jnp.where(gidx[:, None] == gidx[None, :], 1.0 / gs, 0.0).astype(MXU_DTYPE)

    ride = gather(0.3, 'ffn1_w_gate')
    h0, n1, target = _embed(x2, meta_full, loss_target[0], ffn1_pre_g, pad, "embed_prenorm", carried=[ride])
    (wg1,) = ride.results
    ride = gather(0.6, 'ffn1_w_up')
    g1 = _mm_nt(n1, wg1, "ffn1_gate", carried=[ride], out_dtype=MXU_DTYPE)
    (wu1,) = ride.results
    ride = gather(0.6, 'ffn1_w_down')
    u1, a1 = _ffn_up_act(n1, wu1, g1, "ffn1_up_act", carried=[ride])
    (wd1,) = ride.results
    ride = gather(0.75, 'w_in', 'w_out')
    fo1, h1, un = _mm_residual_norm(a1, wd1, h0, ffn1_post_g, 0.5, mix_pre_g, "ffn1_down", carried=[ride])
    win_t, wout = ride.results
    s2 = shard_rows['ffn2_w_gate']
    quarter = _round_up(s2 // 4, SUBLANE_BF16)
    ride_g = gather(0.5, 'ffn2_w_gate', part=(0, 3 * quarter))
    z = _mm_nt(un, win_t, "mix_in_proj", carried=[ride_g])
    ride_g = gather(0.5, 'ffn2_w_gate', part=(3 * quarter, s2 - 3 * quarter), into=ride_g.results[0])
    ride_u = gather(0.5, 'ffn2_w_up', part=(0, quarter))
    mixed, hs = _mixer_fwd(z, pv, wa_bd, wx_bd, gm, pad, "mixer_fwd", carried=[ride_g, ride_u])
    (wg2,) = ride_g.results
    ride_u = gather(0.5, 'ffn2_w_up', part=(quarter, s2 - quarter), into=ride_u.results[0])
    o_mix, h2, n2 = _mm_residual_norm(mixed, wout, h1, mix_post_g, 1.0, ffn2_pre_g, "mix_out_proj", carried=[ride_u])
    (wu2,) = ride_u.results
    ride = gather(0.75, 'ffn2_w_down')
    g2, u2, a2 = _ffn_gate_up(n2, wg2, wu2, "ffn2_gate_up", carried=[ride])
    (wd2,) = ride.results
    dh3, dfo2, d_post2, loss_part = _mm_residual_loss(a2, wd2, h2, ffn2_post_g, 0.5, target, lead, "ffn2_down_loss")

    chip_rel = [2 * (xi ^ (r >> 1)) + (yi ^ (r & 1)) for r in range(4)]
    where = jnp.stack([2 * k + ci for k in chip_rel] + chip_rel).astype(jnp.int32)
    red = {}

    def reduction(nme, grad):
        red[nme] = _GradReduction(nme, grad, shard_rows[nme], where)
        return red[nme]

    r_wd2 = reduction('ffn2_w_down', _mm_tn(a2, dfo2, "ffn2_dw_down"))
    dg2, du2 = _ffn_hidden_bwd(dfo2, wd2, g2, u2, "ffn2_hidden_bwd", carried=[r_wd2.swap()])
    r_wg2 = reduction('ffn2_w_gate', _mm_tn(dg2, n2, "ffn2_dw_gate", carried=[r_wd2.exchange(part=0)]))
    r_wu2 = reduction('ffn2_w_up', _mm_tn(du2, n2, "ffn2_dw_up", carried=[r_wd2.exchange(part=1), r_wg2.swap()]))
    dh2, d_pre2 = _mm_norm_bwd([(dg2, wg2), (du2, wu2)], h2, ffn2_pre_g, dh3, "ffn2_dx",
                               carried=[r_wg2.exchange(), r_wu2.swap()])
    do_mix, d_mix_post, dmixed = _norm_bwd_mm_nt(o_mix, mix_post_g, dh2, 1.0, wout, "mix_out_proj_bwd")
    r_wout = reduction('w_out', _mm_tn(mixed, do_mix, "mix_dw_out"))
    dz, dpv, dwa_bd, dwx_bd = _mixer_bwd(z, hs, dmixed, pv, wa_bd, wx_bd, gm, pad, "mixer_bwd",
                                         carried=[r_wu2.exchange(), r_wout.swap()])
    r_win = reduction('w_in', _mm_tn(dz, un, "mix_dw_in", carried=[r_wout.exchange()]))
    dh1, d_mix_pre, dfo1, d_post1 = _mm_norm_bwd([(dz, win_t)], h1, mix_pre_g, dh2, "mix_dx", carried=[r_win.swap()],
                                                 post=(fo1, ffn1_post_g, 0.5))
    r_wd1 = reduction('ffn1_w_down', _mm_tn(a1, dfo1, "ffn1_dw_down", carried=[r_win.exchange(part=0)]))
    early_names = ['mix_pre_g', 'mix_post_g', 'ffn2_pre_g', 'ffn2_post_g', 'ffn1_post_g',
                   'lru_conv_b', 'lru_b_a', 'lru_b_x', 'lru_lambda', 'lru_out_g', 'sconv_out_g',
                   'lru_conv_w', 'sconv_w', 'lru_w_a', 'lru_w_x']
    early_parts = [d_mix_pre, d_mix_post, d_pre2, d_post2, d_post1,
                   dpv[4:5], dpv[5:6], dpv[6:7], dpv[7:8], dpv[11:12], dpv[12:13],
                   dpv[0:4], dpv[8:11], _block_diag_extract(dwa_bd, hb), _block_diag_extract(dwx_bd, hb)]
    early_packed = _pack_rows(early_parts, d, SUBLANE_BF16)
    early_ride = _CarriedGather([early_packed], [N_DEV * early_packed.shape[0]], zeros, 0.75)
    dg1, du1 = _ffn_hidden_bwd(dfo1, wd1, g1, u1, "ffn1_hidden_bwd",
                               carried=[r_win.exchange(part=1), r_wd1.swap(), early_ride])
    early_sum = _sum_stack(early_ride.results[0], "sum_small_early")
    r_wg1 = reduction('ffn1_w_gate', _mm_tn(dg1, n1, "ffn1_dw_gate", carried=[r_wd1.exchange(part=0)]))
    r_wu1 = reduction('ffn1_w_up', _mm_tn(du1, n1, "ffn1_dw_up", carried=[r_wd1.exchange(part=1), r_wg1.swap()]))
    row_tile = _norm_bwd_row_tile(m_rows)
    n_tiles = m_rows // row_tile
    half = n_tiles // 2
    assert half >= 1 and half * row_tile >= lead
    dh0_a, d_pre1_a = _mm_norm_bwd([(dg1, wg1), (du1, wu1)], h0, ffn1_pre_g, dh1, "ffn1_dx_a",
                                   carried=[r_wg1.exchange(), r_wu1.swap()], row_tiles=(0, half))
    dh0_b, d_pre1 = _mm_norm_bwd([(dg1, wg1), (du1, wu1)], h0, ffn1_pre_g, dh1, "ffn1_dx_b",
                                 carried=[r_wu1.exchange()], row_tiles=(half, n_tiles - half), dg_init=d_pre1_a)
    grad_x = jnp.concatenate([dh0_a[lead:], dh0_b], axis=0)[None]
    d_meta = dh0_a[pad:lead]

    grads, delta, new_m, new_v = {}, {}, {}, {}
    for nme in big:
        in_shard_layout = nme not in col_sharded or shard_rows[nme] % LANE != 0
        if in_shard_layout:
            view = (lambda t: t[0].T) if nme in col_sharded else (lambda t: t[0])
            back = (lambda t: t.T[None]) if nme in col_sharded else (lambda t: t[None])
            outs = red[nme].total_and_update(view(wts[nme]), view(mom[nme]), view(var[nme]))
            grads[nme], delta[nme], new_m[nme], new_v[nme] = [back(t) for t in outs]
        else:
            grads[nme] = red[nme].total().T[None]
            outs = _adamw(wts[nme][0], grads[nme][0], mom[nme][0], var[nme][0], "adamw_" + nme)
            delta[nme], new_m[nme], new_v[nme] = [t[None] for t in outs]

    late_names = ['ffn1_pre_g', 'meta_tokens']
    late_parts = [d_pre1, d_meta, loss_part]
    late_sum = _gather_small(_pack_rows(late_parts, d), True, "reduce_small_late")
    small_sums = (_unpack_rows(early_sum, [p.shape for p in early_parts])
                  + _unpack_rows(late_sum, [p.shape for p in late_parts]))
    loss = small_sums.pop()[0, 0]
    for nme, gsm in zip(early_names + late_names, small_sums):
        if nme == 'meta_tokens':
            grads[nme] = lax.dynamic_slice_in_dim(gsm, me * dm, dm, axis=1)
        elif nme in ('lru_conv_w', 'sconv_w'):
            grads[nme] = lax.dynamic_slice_in_dim(gsm, me * cs_, cs_, axis=1)[None]
        else:
            grads[nme] = gsm.reshape(wts[nme].shape)

    rest = [n for n in WEIGHT_NAMES if n not in big]
    rest_shapes = [wts[n].shape for n in rest]
    packed = [_pack_rows([src[n] for n in rest], LANE, 256) for src in (wts, grads, mom, var)]
    for out, packed_out in zip((delta, new_m, new_v), _adamw(*packed, "adamw_small")):
        for nme, arr in zip(rest, _unpack_rows(packed_out, rest_shapes)):
            out[nme] = arr

    return (loss, grad_x, *[grads[n] for n in WEIGHT_NAMES], *[delta[n] for n in WEIGHT_NAMES],
            *[new_m[n] for n in WEIGHT_NAMES], *[new_v[n] for n in WEIGHT_NAMES])
```

```python
import functools

import jax
import jax.numpy as jnp
from jax import lax
from jax.experimental import pallas as pl
from jax.experimental.pallas import tpu as pltpu

F32 = jnp.float32
MXU_DTYPE = jnp.bfloat16
WIRE_DTYPE = jnp.bfloat16
MESH = pl.DeviceIdType.MESH

EPS = 1e-6
LRU_C = 8.0
N_GROUPS = 16
ADAM_LR = 0.001
ADAM_B1 = 0.9
ADAM_B2 = 0.999
ADAM_EPS = 1e-08
ADAM_WD = 0.01
ADAM_STEP = 10

N_DEV = 8
LANE = 128
SUBLANE_BF16 = 16
ROW_ALIGN = 128
F_ALIGN = 512
BD = 256
K_TILE = 512
ACC_ROWS = 528
ACC_GROUP = 1
MIX_ROWS = 128
VMEM_LIMIT_MB = 56

WEIGHT_NAMES = ['meta_tokens', 'ffn1_pre_g', 'ffn1_w_gate', 'ffn1_w_up', 'ffn1_w_down', 'ffn1_post_g',
                'mix_pre_g', 'w_in', 'lru_conv_w', 'lru_conv_b', 'lru_w_a', 'lru_b_a', 'lru_w_x', 'lru_b_x',
                'lru_lambda', 'sconv_w', 'lru_out_g', 'sconv_out_g', 'w_out', 'mix_post_g', 'ffn2_pre_g',
                'ffn2_w_gate', 'ffn2_w_up', 'ffn2_w_down', 'ffn2_post_g']


def _round_up(n, q):
    return (n + q - 1) // q * q


def _tile(n, target, q):
    best = None
    t = q
    while t <= min(n, target):
        if n % t == 0:
            best = t
        t += q
    assert best is not None, (n, target, q)
    return best


def _params(**kw):
    return pltpu.CompilerParams(vmem_limit_bytes=VMEM_LIMIT_MB << 20, **kw)


def _call(body, *, grid, in_specs, out_specs, out_shape, name, args, scratch_shapes=(), carried=(), prefetch=()):
    carried = list(carried)
    n_pf = len(prefetch)

    def launch(fn, in_specs_, out_specs_, out_shape_, scratch_, operands, aliases_):
        if n_pf:
            spec = pltpu.PrefetchScalarGridSpec(num_scalar_prefetch=n_pf, grid=grid, in_specs=in_specs_,
                                                out_specs=out_specs_, scratch_shapes=scratch_)
            return pl.pallas_call(fn, grid_spec=spec, out_shape=out_shape_, input_output_aliases=aliases_,
                                  name=name, compiler_params=_params())(*prefetch, *operands)
        return pl.pallas_call(fn, grid=grid, in_specs=in_specs_, out_specs=out_specs_, out_shape=out_shape_,
                              scratch_shapes=scratch_, input_output_aliases=aliases_, name=name,
                              compiler_params=_params())(*operands)

    if not carried:
        return launch(body, in_specs, out_specs, out_shape, list(scratch_shapes), args, {})
    single = not isinstance(out_shape, (list, tuple))
    out_specs_l = [out_specs] if single else list(out_specs)
    out_shape_l = [out_shape] if single else list(out_shape)
    n_in, n_out, n_scr = len(in_specs), len(out_specs_l), len(scratch_shapes)
    hbm = pl.BlockSpec(memory_space=pl.ANY)
    c_in = [a for cm in carried for a in cm.arrays]
    c_out = [s for cm in carried for s in cm.out_shapes]
    c_scr = []
    aliases = {}
    in_off, out_off = n_pf + n_in, n_out
    for cm in carried:
        c_scr += [pltpu.SemaphoreType.DMA((cm.n_remote,)), pltpu.SemaphoreType.DMA((cm.n_remote,)),
                  pltpu.SemaphoreType.DMA((max(cm.n_local, 1),))]
        for k, v in cm.aliases.items():
            aliases[in_off + k] = out_off + v
        in_off += len(cm.arrays)
        out_off += len(cm.out_shapes)
    steps = 1
    for g in grid:
        steps *= g
    forward_steps = [min(int(cm.forward_at * steps), steps - 1) for cm in carried]

    def wrapped(*refs):
        pf = refs[:n_pf]
        p = n_pf
        ins = refs[p:p + n_in]
        p += n_in
        cins = refs[p:p + len(c_in)]
        p += len(c_in)
        outs = refs[p:p + n_out]
        p += n_out
        couts = refs[p:p + len(c_out)]
        p += len(c_out)
        scr = refs[p:p + n_scr]
        csem = refs[p + n_scr:]
        lin = 0
        for axis, g in enumerate(grid):
            lin = lin * g + pl.program_id(axis)
        views = []
        io = oo = 0
        for j, cm in enumerate(carried):
            views.append((cins[io:io + len(cm.arrays)], couts[oo:oo + len(cm.out_shapes)],
                          csem[3 * j], csem[3 * j + 1], csem[3 * j + 2]))
            io += len(cm.arrays)
            oo += len(cm.out_shapes)

        @pl.when(lin == 0)
        def _():
            for cm, v in zip(carried, views):
                cm.start(*v)

        body(*pf, *ins, *outs, *scr)

        for cm, v, step in zip(carried, views, forward_steps):
            pl.when(lin == step)(functools.partial(cm.forward, *v))

        @pl.when(lin == steps - 1)
        def _():
            for cm, v in zip(carried, views):
                cm.finish(*v)

    res = launch(wrapped, list(in_specs) + [hbm] * len(c_in), out_specs_l + [hbm] * len(c_out),
                 out_shape_l + c_out, list(scratch_shapes) + c_scr, (*args, *c_in), aliases)
    oo = n_out
    for cm in carried:
        cm.results = list(res[oo:oo + len(cm.out_shapes)])
        oo += len(cm.out_shapes)
    return res[0] if single else list(res[:n_out])


def _embed(x, meta, target, g, pad, name, carried=()):
    seq, d = x.shape
    n_meta = meta.shape[0]
    lead = pad + n_meta
    m = lead + seq
    tr = ROW_ALIGN
    lead_blocks = lead // tr
    meta_row = pad - (lead_blocks - 1) * tr
    assert lead % tr == 0 and seq % tr == 0 and 0 <= meta_row and meta_row % 8 == 0

    def body(x_ref, meta_ref, t_ref, g_ref, h_ref, n_ref, tp_ref):
        i = pl.program_id(0)

        @pl.when(i < lead_blocks)
        def _():
            h_ref[...] = jnp.zeros_like(h_ref)
            tp_ref[...] = jnp.zeros_like(tp_ref)

        @pl.when(i == lead_blocks - 1)
        def _():
            h_ref[pl.ds(meta_row, n_meta), :] = meta_ref[...]

        @pl.when(i >= lead_blocks)
        def _():
            h_ref[...] = x_ref[...]
            tp_ref[...] = t_ref[...]

        h = h_ref[...]
        r = lax.rsqrt(jnp.mean(h * h, axis=-1, keepdims=True) + EPS)
        n_ref[...] = (h * r * g_ref[...]).astype(n_ref.dtype)

    tokens = pl.BlockSpec((tr, d), lambda i: (jnp.maximum(i - lead_blocks, 0), 0))
    rows = pl.BlockSpec((tr, d), lambda i: (i, 0))
    return _call(
        body, grid=(m // tr,),
        in_specs=[tokens, pl.BlockSpec((n_meta, d), lambda i: (0, 0)), tokens, pl.BlockSpec((1, d), lambda i: (0, 0))],
        out_specs=[rows, rows, rows],
        out_shape=[jax.ShapeDtypeStruct((m, d), F32), jax.ShapeDtypeStruct((m, d), MXU_DTYPE),
                   jax.ShapeDtypeStruct((m, d), F32)],
        name=name, args=(x, meta, target, g), carried=carried)


def _rmsnorm_bwd_rows(x, g, dy):
    r = lax.rsqrt(jnp.mean(x * x, axis=-1, keepdims=True) + EPS)
    xh = x * r
    dyh = dy * g
    dx = r * (dyh - xh * jnp.mean(dyh * xh, axis=-1, keepdims=True))
    return dx, dy * xh


def _dot_nt(a, b):
    return lax.dot_general(a, b, (((1,), (1,)), ((), ())), preferred_element_type=F32)


def _dot_tn(a, b):
    return lax.dot_general(a, b, (((0,), (0,)), ((), ())), preferred_element_type=F32)


def _mm_nt(a, w, name, carried=(), out_dtype=F32):
    m, k = a.shape
    n = w.shape[0]
    tm = _tile(m, 1056, SUBLANE_BF16)
    tn = _tile(n, 512, LANE)

    def body(a_ref, w_ref, o_ref):
        o_ref[...] = _dot_nt(a_ref[...], w_ref[...]).astype(o_ref.dtype)

    return _call(
        body, grid=(m // tm, n // tn),
        in_specs=[pl.BlockSpec((tm, k), lambda i, j: (i, 0)), pl.BlockSpec((tn, k), lambda i, j: (j, 0))],
        out_specs=pl.BlockSpec((tm, tn), lambda i, j: (i, j)),
        out_shape=jax.ShapeDtypeStruct((m, n), out_dtype), name=name, args=(a, w), carried=carried)


def _norm_bwd_mm_nt(x, g, dy, scale, w, name, carried=()):
    m, d = x.shape
    n = w.shape[0]
    tm = _tile(m, 528, SUBLANE_BF16)

    def body(x_ref, g_ref, dy_ref, w_ref, dx_ref, dg_ref, o_ref):
        @pl.when(pl.program_id(0) == 0)
        def _():
            dg_ref[...] = jnp.zeros_like(dg_ref)

        dx, dgr = _rmsnorm_bwd_rows(x_ref[...], g_ref[...], scale * dy_ref[...])
        dxb = dx.astype(dx_ref.dtype)
        dx_ref[...] = dxb
        dg_ref[...] += jnp.sum(dgr, axis=0, keepdims=True)
        o_ref[...] = _dot_nt(dxb, w_ref[...])

    row = pl.BlockSpec((tm, d), lambda i: (i, 0))
    vec = pl.BlockSpec((1, d), lambda i: (0, 0))
    return _call(
        body, grid=(m // tm,),
        in_specs=[row, vec, row, pl.BlockSpec((n, d), lambda i: (0, 0), pipeline_mode=pl.Buffered(1))],
        out_specs=[row, vec, pl.BlockSpec((tm, n), lambda i: (i, 0))],
        out_shape=[jax.ShapeDtypeStruct((m, d), MXU_DTYPE), jax.ShapeDtypeStruct((1, d), F32),
                   jax.ShapeDtypeStruct((m, n), F32)],
        name=name, args=(x, g, dy, w), carried=carried)


def _ffn_up_act(n_act, wu_t, g_act, name, carried=()):
    m, d = n_act.shape
    fp = wu_t.shape[0]
    tm = _tile(m, 1056, SUBLANE_BF16)
    tn = _tile(fp, 512, LANE)

    def body(n_ref, wu_ref, g_ref, u_ref, a_ref):
        u = _dot_nt(n_ref[...], wu_ref[...])
        g = g_ref[...].astype(F32)
        u_ref[...] = u.astype(u_ref.dtype)
        a_ref[...] = (g * jax.nn.sigmoid(g) * u).astype(a_ref.dtype)

    act = pl.BlockSpec((tm, tn), lambda i, j: (i, j))
    return _call(
        body, grid=(m // tm, fp // tn),
        in_specs=[pl.BlockSpec((tm, d), lambda i, j: (i, 0)), pl.BlockSpec((tn, d), lambda i, j: (j, 0)), act],
        out_specs=[act, act],
        out_shape=[jax.ShapeDtypeStruct((m, fp), MXU_DTYPE)] * 2, name=name, args=(n_act, wu_t, g_act), carried=carried)


def _ffn_gate_up(n_act, wg_t, wu_t, name, carried=()):
    m, d = n_act.shape
    fp = wg_t.shape[0]
    tm = _tile(m, 1056, SUBLANE_BF16)
    tn = _tile(fp, 512, LANE)

    def body(n_ref, wg_ref, wu_ref, g_ref, u_ref, a_ref):
        n = n_ref[...]
        g = _dot_nt(n, wg_ref[...])
        u = _dot_nt(n, wu_ref[...])
        g_ref[...] = g.astype(g_ref.dtype)
        u_ref[...] = u.astype(u_ref.dtype)
        a_ref[...] = (g * jax.nn.sigmoid(g) * u).astype(a_ref.dtype)

    act = pl.BlockSpec((tm, tn), lambda i, j: (i, j))
    wsp = pl.BlockSpec((tn, d), lambda i, j: (j, 0))
    return _call(
        body, grid=(m // tm, fp // tn),
        in_specs=[pl.BlockSpec((tm, d), lambda i, j: (i, 0)), wsp, wsp],
        out_specs=[act, act, act],
        out_shape=[jax.ShapeDtypeStruct((m, fp), MXU_DTYPE)] * 3, name=name, args=(n_act, wg_t, wu_t), carried=carried)


def _ffn_hidden_bwd(dfo, wd, g_act, u_act, name, carried=()):
    m, d = dfo.shape
    fp = wd.shape[0]
    tm = _tile(m, 1056, SUBLANE_BF16)
    tn = _tile(fp, 512, LANE)

    def body(df_ref, wd_ref, g_ref, u_ref, dg_ref, du_ref):
        da = _dot_nt(df_ref[...], wd_ref[...]).astype(dg_ref.dtype)
        g = g_ref[...]
        u = u_ref[...]
        s = jax.nn.sigmoid(g)
        du_ref[...] = da * (g * s)
        dg_ref[...] = da * (u * (s * (1.0 + g * (1.0 - s))))

    act = pl.BlockSpec((tm, tn), lambda i, j: (i, j))
    return _call(
        body, grid=(m // tm, fp // tn),
        in_specs=[pl.BlockSpec((tm, d), lambda i, j: (i, 0)), pl.BlockSpec((tn, d), lambda i, j: (j, 0)), act, act],
        out_specs=[act, act],
        out_shape=[jax.ShapeDtypeStruct((m, fp), MXU_DTYPE)] * 2, name=name, args=(dfo, wd, g_act, u_act),
        carried=carried)


def _row_groups(n_tiles, max_group, nk):
    gsz = max(q for q in range(1, max_group + 1) if n_tiles % q == 0)

    def epilogue_row(grp, kk, i):
        return grp * gsz + jnp.where(kk == nk - 1, i, 0)

    return gsz, epilogue_row


def _mm_residual_norm(a, w, h, g, scale, next_g, name, carried=()):
    m, k = a.shape
    d = w.shape[1]
    tm = _tile(m, ACC_ROWS, SUBLANE_BF16)
    tk = _tile(k, K_TILE, LANE)
    nk = k // tk
    gsz, epilogue_row = _row_groups(m // tm, ACC_GROUP, nk)

    def body(a_ref, w_ref, h_ref, g_ref, ng_ref, fo_ref, hn_ref, nn_ref, acc_ref):
        kk, i = pl.program_id(1), pl.program_id(2)

        @pl.when(kk == 0)
        def _():
            acc_ref[i] = jnp.zeros((tm, d), F32)

        acc_ref[i] += jnp.dot(a_ref[...], w_ref[...], preferred_element_type=F32)

        @pl.when(kk == nk - 1)
        def _():
            fo = acc_ref[i]
            fo_ref[...] = fo
            r = lax.rsqrt(jnp.mean(fo * fo, axis=-1, keepdims=True) + EPS)
            hn = h_ref[...] + scale * (fo * r * g_ref[...])
            hn_ref[...] = hn
            rn = lax.rsqrt(jnp.mean(hn * hn, axis=-1, keepdims=True) + EPS)
            nn_ref[...] = (hn * rn * ng_ref[...]).astype(nn_ref.dtype)

    row = pl.BlockSpec((tm, d), lambda grp, kk, i: (epilogue_row(grp, kk, i), 0))
    vec = pl.BlockSpec((1, d), lambda grp, kk, i: (0, 0))
    return _call(
        body, grid=(m // tm // gsz, nk, gsz),
        in_specs=[pl.BlockSpec((tm, tk), lambda grp, kk, i: (grp * gsz + i, kk)),
                  pl.BlockSpec((tk, d), lambda grp, kk, i: (kk, 0)), row, vec, vec],
        out_specs=[row, row, row],
        out_shape=[jax.ShapeDtypeStruct((m, d), F32)] * 2 + [jax.ShapeDtypeStruct((m, d), MXU_DTYPE)],
        scratch_shapes=[pltpu.VMEM((gsz, tm, d), F32)], name=name, args=(a, w, h, g, next_g), carried=carried)


def _mm_residual_loss(a, w, h, g, scale, target, lead, name, carried=()):
    m, k = a.shape
    d = w.shape[1]
    tm = _tile(m, ACC_ROWS, SUBLANE_BF16)
    tk = _tile(k, K_TILE, LANE)
    nk = k // tk
    gsz, epilogue_row = _row_groups(m // tm, ACC_GROUP, nk)

    def body(a_ref, w_ref, h_ref, g_ref, t_ref, dy_ref, dfo_ref, dg_ref, l_ref, acc_ref):
        grp, kk, i = pl.program_id(0), pl.program_id(1), pl.program_id(2)

        @pl.when(jnp.logical_and(jnp.logical_and(grp == 0, kk == 0), i == 0))
        def _():
            dg_ref[...] = jnp.zeros_like(dg_ref)
            l_ref[...] = jnp.zeros_like(l_ref)

        @pl.when(kk == 0)
        def _():
            acc_ref[i] = jnp.zeros((tm, d), F32)

        acc_ref[i] += jnp.dot(a_ref[...], w_ref[...], preferred_element_type=F32)

        @pl.when(kk == nk - 1)
        def _():
            fo = acc_ref[i]
            gain = g_ref[...]
            r = lax.rsqrt(jnp.mean(fo * fo, axis=-1, keepdims=True) + EPS)
            xh = fo * r
            y = h_ref[...] + scale * (xh * gain)
            row = (grp * gsz + i) * tm + lax.broadcasted_iota(jnp.int32, (tm, 1), 0)
            e = jnp.where(row >= lead, y - t_ref[...], 0.0)
            dy = e * (1.0 / d)
            dy_ref[...] = dy
            l_ref[...] += 0.5 * jnp.sum(jnp.sum(e * e, axis=-1, keepdims=True) * (1.0 / d), axis=0, keepdims=True)
            dn = scale * dy
            dyh = dn * gain
            dfo_ref[...] = (r * (dyh - xh * jnp.mean(dyh * xh, axis=-1, keepdims=True))).astype(dfo_ref.dtype)
            dg_ref[...] += jnp.sum(dn * xh, axis=0, keepdims=True)

    row = pl.BlockSpec((tm, d), lambda grp, kk, i: (epilogue_row(grp, kk, i), 0))
    vec = pl.BlockSpec((1, d), lambda grp, kk, i: (0, 0))
    return _call(
        body, grid=(m // tm // gsz, nk, gsz),
        in_specs=[pl.BlockSpec((tm, tk), lambda grp, kk, i: (grp * gsz + i, kk)),
                  pl.BlockSpec((tk, d), lambda grp, kk, i: (kk, 0)), row, vec, row],
        out_specs=[row, row, vec, pl.BlockSpec((1, 1), lambda grp, kk, i: (0, 0))],
        out_shape=[jax.ShapeDtypeStruct((m, d), F32), jax.ShapeDtypeStruct((m, d), MXU_DTYPE),
                   jax.ShapeDtypeStruct((1, d), F32), jax.ShapeDtypeStruct((1, 1), F32)],
        scratch_shapes=[pltpu.VMEM((gsz, tm, d), F32)], name=name, args=(a, w, h, g, target), carried=carried)


def _norm_bwd_row_tile(m):
    return _tile(m, ACC_ROWS, SUBLANE_BF16)


def _mm_norm_bwd(pairs, h, g, dh_up, name, carried=(), row_tiles=None, dg_init=None, post=None):
    n_pairs = len(pairs)
    m, k = pairs[0][0].shape
    d = h.shape[1]
    tm = _norm_bwd_row_tile(m)
    tk = _tile(k, K_TILE, LANE)
    nk = k // tk
    t0, nt = row_tiles if row_tiles is not None else (0, m // tm)
    gsz, epilogue_row = _row_groups(nt, ACC_GROUP, nk)
    if dg_init is None:
        dg_init = jnp.zeros((1, d), F32)

    n_post = 0 if post is None else 2

    def body(*refs):
        ops = refs[:2 * n_pairs]
        h_ref, g_ref, up_ref, init_ref = refs[2 * n_pairs:2 * n_pairs + 4]
        post_in = refs[2 * n_pairs + 4:2 * n_pairs + 4 + n_post]
        dh_ref, dg_ref = refs[2 * n_pairs + 4 + n_post:2 * n_pairs + 6 + n_post]
        post_out = refs[2 * n_pairs + 6 + n_post:2 * n_pairs + 6 + 2 * n_post]
        acc_ref = refs[-1]
        grp, kk, i = pl.program_id(0), pl.program_id(1), pl.program_id(2)

        @pl.when(jnp.logical_and(jnp.logical_and(grp == 0, kk == 0), i == 0))
        def _():
            dg_ref[...] = init_ref[...]
            if post is not None:
                post_out[1][...] = jnp.zeros_like(post_out[1])

        @pl.when(kk == 0)
        def _():
            acc_ref[i] = jnp.zeros((tm, d), F32)

        for p in range(n_pairs):
            acc_ref[i] += jnp.dot(ops[2 * p][...], ops[2 * p + 1][...], preferred_element_type=F32)

        @pl.when(kk == nk - 1)
        def _():
            dx, dgr = _rmsnorm_bwd_rows(h_ref[...], g_ref[...], acc_ref[i])
            dh = up_ref[...] + dx
            dh_ref[...] = dh
            dg_ref[...] += jnp.sum(dgr, axis=0, keepdims=True)
            if post is not None:
                dfo, dpr = _rmsnorm_bwd_rows(post_in[0][...], post_in[1][...], post[2] * dh)
                post_out[0][...] = dfo.astype(post_out[0].dtype)
                post_out[1][...] += jnp.sum(dpr, axis=0, keepdims=True)

    row_in = pl.BlockSpec((tm, d), lambda grp, kk, i: (t0 + epilogue_row(grp, kk, i), 0))
    row_out = pl.BlockSpec((tm, d), lambda grp, kk, i: (epilogue_row(grp, kk, i), 0))
    vec = pl.BlockSpec((1, d), lambda grp, kk, i: (0, 0))
    in_specs = []
    args = []
    for a, w in pairs:
        in_specs += [pl.BlockSpec((tm, tk), lambda grp, kk, i: (t0 + grp * gsz + i, kk)),
                     pl.BlockSpec((tk, d), lambda grp, kk, i: (kk, 0))]
        args += [a, w]
    in_specs += [row_in, vec, row_in, vec]
    args += [h, g, dh_up, dg_init]
    out_specs = [row_out, vec]
    out_shape = [jax.ShapeDtypeStruct((nt * tm, d), F32), jax.ShapeDtypeStruct((1, d), F32)]
    if post is not None:
        in_specs += [row_in, vec]
        args += [post[0], post[1]]
        out_specs += [row_out, vec]
        out_shape += [jax.ShapeDtypeStruct((nt * tm, d), MXU_DTYPE), jax.ShapeDtypeStruct((1, d), F32)]
    return _call(
        body, grid=(nt // gsz, nk, gsz), in_specs=in_specs, out_specs=out_specs, out_shape=out_shape,
        scratch_shapes=[pltpu.VMEM((gsz, tm, d), F32)], name=name, args=tuple(args), carried=carried)


def _mm_tn(a, b, name, carried=()):
    m, ka = a.shape
    d = b.shape[1]
    tf = _tile(ka, 512, LANE)

    def body(a_ref, b_ref, o_ref):
        o_ref[...] = _dot_tn(a_ref[...], b_ref[...]).astype(o_ref.dtype)

    return _call(
        body, grid=(ka // tf,),
        in_specs=[pl.BlockSpec((m, tf), lambda j: (0, j)),
                  pl.BlockSpec((m, d), lambda j: (0, 0), pipeline_mode=pl.Buffered(1))],
        out_specs=pl.BlockSpec((tf, d), lambda j: (j, 0)),
        out_shape=jax.ShapeDtypeStruct((ka, d), WIRE_DTYPE), name=name, args=(a, b), carried=carried)


GELU_K = 0.7978845608028654
GELU_C = 0.044715


def _expm1(x):
    series = x * (1.0 + x * (1.0 / 2 + x * (1.0 / 6 + x * (1.0 / 24 + x * (1.0 / 120)))))
    return jnp.where(jnp.abs(x) < 0.1, series, jnp.exp(x) - 1.0)


def _softplus(x):
    return jnp.maximum(x, 0.0) + jnp.log1p(jnp.exp(-jnp.abs(x)))


def _block_mm(v, w_ref, transposed):
    nbk = w_ref.shape[0]
    outs = []
    for j in range(nbk):
        vj = v[:, j * BD:(j + 1) * BD]
        outs.append(_dot_nt(vj, w_ref[j]) if transposed else jnp.dot(vj, w_ref[j], preferred_element_type=F32))
    return outs[0] if nbk == 1 else jnp.concatenate(outs, axis=1)


def _group_mean(q, gm_ref):
    hi = q.astype(MXU_DTYPE)
    lo = (q - hi.astype(F32)).astype(MXU_DTYPE)
    nbk = q.shape[1] // BD
    gm = gm_ref[...]
    outs = []
    for j in range(nbk):
        sl = slice(j * BD, (j + 1) * BD)
        outs.append(jnp.dot(hi[:, sl], gm, preferred_element_type=F32) + jnp.dot(lo[:, sl], gm, preferred_element_type=F32))
    return outs[0] if nbk == 1 else jnp.concatenate(outs, axis=1)


class _RowReader:
    def __init__(self, ref):
        self.ref = ref

    def __getitem__(self, rows):
        return self.ref[rows, :]


def _shifted(ext_ref, cur, before8, after8, downs=(), ups=()):
    r = cur.shape[0]
    if downs:
        ext_ref[0:8, :] = before8
    ext_ref[8:8 + r, :] = cur
    if ups:
        ext_ref[8 + r:16 + r, :] = after8
    return [ext_ref[pl.ds(8 - j, r), :] for j in downs] + [ext_ref[pl.ds(8 + j, r), :] for j in ups]


def _lru_gates(xc, pv, wa_ref, wx_ref):
    xcb = xc.astype(MXU_DTYPE)
    ga = jax.nn.sigmoid(_block_mm(xcb, wa_ref, False) + pv[5:6])
    gx = jax.nn.sigmoid(_block_mm(xcb, wx_ref, False) + pv[6:7])
    sp = _softplus(-pv[7:8])
    log_a = -LRU_C * ga * sp
    a = jnp.exp(log_a)
    e2 = _expm1(2.0 * log_a)
    mult = jnp.sqrt(-e2)
    return xcb, ga, gx, sp, a, e2, mult


def _gelu_parts(y):
    th = jnp.tanh(GELU_K * (y + GELU_C * y * y * y))
    return 0.5 * y * (1.0 + th), th


def _scan_block(a, u, sa_ref, su_ref, carry_ref, out_ref, reverse):
    r, c = a.shape
    n = r // 8
    a3 = a.reshape(n, 8, c)
    u3 = u.reshape(n, 8, c)
    sub = lax.broadcasted_iota(jnp.int32, (n, 8, c), 1)
    for dlt in (1, 2, 4):
        keep = (sub < 8 - dlt) if reverse else (sub >= dlt)
        shift = 8 - dlt if reverse else dlt
        sh_a = pltpu.roll(a3, shift, axis=1)
        sh_u = pltpu.roll(u3, shift, axis=1)
        u3 = u3 + a3 * jnp.where(keep, sh_u, 0.0)
        a3 = a3 * jnp.where(keep, sh_a, 1.0)
    sa_ref[...] = a3.reshape(r, c)
    su_ref[...] = u3.reshape(r, c)
    for k in (range(n - 1, -1, -1) if reverse else range(n)):
        rows = pl.ds(8 * k, 8)
        out_ref[rows, :] = su_ref[rows, :] + sa_ref[rows, :] * carry_ref[...]
        carry_ref[...] = out_ref[pl.ds(8 * k if reverse else 8 * k + 7, 1), :]


def _mixer_fwd(z, pv, wa, wx, gm, pad, name, carried=()):
    m = z.shape[0]
    c = pv.shape[1]
    r = MIX_ROWS
    nb = m // r

    def body(z_ref, pv_ref, wa_ref, wx_ref, gm_ref, mixed_ref, hs_ref, ext_ref, tailx_ref, tailc_ref, carry_ref,
             sa_ref, su_ref):
        b = pl.program_id(0)

        @pl.when(b == 0)
        def _():
            tailx_ref[...] = jnp.zeros_like(tailx_ref)
            tailc_ref[...] = jnp.zeros_like(tailc_ref)
            carry_ref[...] = jnp.zeros_like(carry_ref)

        pv = _RowReader(pv_ref)
        row = b * r + lax.broadcasted_iota(jnp.int32, (r, 1), 0)
        maskf = (row >= pad).astype(F32)
        y = z_ref[:, 0:c]
        xl = z_ref[:, c:2 * c]
        bs = z_ref[:, 2 * c:3 * c]
        cv = z_ref[:, 3 * c:4 * c] * z_ref[:, 4 * c:5 * c]

        x1, x2, x3 = _shifted(ext_ref, xl, tailx_ref[...], None, downs=(1, 2, 3))
        tailx_ref[...] = z_ref[pl.ds(r - 8, 8), c:2 * c]
        xc = pv[4:5] + pv[3:4] * xl + pv[2:3] * x1 + pv[1:2] * x2 + pv[0:1] * x3
        _, _, gx, _, a, _, mult = _lru_gates(xc, pv, wa_ref, wx_ref)
        uu = mult * (gx * xc) * maskf

        _scan_block(a, uu, sa_ref, su_ref, carry_ref, hs_ref, reverse=False)
        hs = hs_ref[...]

        gelu_y, _ = _gelu_parts(y)
        lru_out = hs * gelu_y
        c1, c2 = _shifted(ext_ref, cv, tailc_ref[...], None, downs=(1, 2))
        tailc_ref[...] = cv[r - 8:r]
        sc_out = bs * (pv[10:11] * cv + pv[9:10] * c1 + pv[8:9] * c2)

        rl = lax.rsqrt(_group_mean(lru_out * lru_out, gm_ref) + EPS)
        rs = lax.rsqrt(_group_mean(sc_out * sc_out, gm_ref) + EPS)
        mixed_ref[:, 0:c] = (lru_out * rl * pv[11:12]).astype(mixed_ref.dtype)
        mixed_ref[:, c:2 * c] = (sc_out * rs * pv[12:13]).astype(mixed_ref.dtype)

    full = lambda shape: pl.BlockSpec(shape, lambda b: (0,) * len(shape))
    return _call(
        body, grid=(nb,),
        in_specs=[pl.BlockSpec((r, 5 * c), lambda b: (b, 0)), full(pv.shape), full(wa.shape), full(wx.shape), full(gm.shape)],
        out_specs=[pl.BlockSpec((r, 2 * c), lambda b: (b, 0)), pl.BlockSpec((r, c), lambda b: (b, 0))],
        out_shape=[jax.ShapeDtypeStruct((m, 2 * c), MXU_DTYPE), jax.ShapeDtypeStruct((m, c), F32)],
        scratch_shapes=[pltpu.VMEM((r + 16, c), F32), pltpu.VMEM((8, c), F32), pltpu.VMEM((8, c), F32),
                        pltpu.VMEM((1, c), F32), pltpu.VMEM((r, c), F32), pltpu.VMEM((r, c), F32)],
        name=name, args=(z, pv, wa, wx, gm), carried=carried)


def _mixer_bwd(z, hs, dmixed, pv, wa, wx, gm, pad, name, carried=()):
    m = z.shape[0]
    c = pv.shape[1]
    r = MIX_ROWS
    nb = m // r
    r8 = r // 8
    assert pad <= r and pad % SUBLANE_BF16 == 0

    def body(z_ref, zp_ref, hs_ref, hsp_ref, dm_ref, pv_ref, wa_ref, wx_ref, gm_ref,
             dz_ref, dpv_ref, dwa_ref, dwx_ref, ext_ref, hxc_ref, hsc_ref, hp_ref, pc_ref, sa_ref, su_ref, p_ref):
        i = pl.program_id(0)
        b = nb - 1 - i

        @pl.when(i == 0)
        def _():
            hxc_ref[...] = jnp.zeros_like(hxc_ref)
            hsc_ref[...] = jnp.zeros_like(hsc_ref)
            hp_ref[...] = jnp.zeros_like(hp_ref)
            pc_ref[...] = jnp.zeros_like(pc_ref)
            dpv_ref[...] = jnp.zeros_like(dpv_ref)
            dwa_ref[...] = jnp.zeros_like(dwa_ref)
            dwx_ref[...] = jnp.zeros_like(dwx_ref)

        pv = _RowReader(pv_ref)
        row = b * r + lax.broadcasted_iota(jnp.int32, (r, 1), 0)
        maskf = (row >= pad).astype(F32)
        has_prev = (b > 0).astype(F32)
        y = z_ref[:, 0:c]
        xl = z_ref[:, c:2 * c]
        bs = z_ref[:, 2 * c:3 * c]
        cs = z_ref[:, 3 * c:4 * c]
        vs = z_ref[:, 4 * c:5 * c]
        cv = cs * vs
        xl_prev = zp_ref[:, c:2 * c] * has_prev
        cv_prev = zp_ref[:, 3 * c:4 * c] * zp_ref[:, 4 * c:5 * c] * has_prev
        hs = hs_ref[...]

        x1, x2, x3 = _shifted(ext_ref, xl, xl_prev, None, downs=(1, 2, 3))
        xc = pv[4:5] + pv[3:4] * xl + pv[2:3] * x1 + pv[1:2] * x2 + pv[0:1] * x3
        xcb, ga, gx, sp, a, e2, mult = _lru_gates(xc, pv, wa_ref, wx_ref)
        gxx = gx * xc
        gelu_y, th = _gelu_parts(y)
        lru_out = hs * gelu_y
        c1, c2 = _shifted(ext_ref, cv, cv_prev, None, downs=(1, 2))
        sc = pv[10:11] * cv + pv[9:10] * c1 + pv[8:9] * c2
        sc_out = bs * sc

        def group_norm_bwd(v, dm, gain):
            rr = lax.rsqrt(_group_mean(v * v, gm_ref) + EPS)
            vh = v * rr
            dvh = dm * gain
            dv = rr * (dvh - vh * _group_mean(dvh * vh, gm_ref))
            return dv, jnp.sum(dm * vh, axis=0, keepdims=True)

        d_lru_out, d_og = group_norm_bwd(lru_out, dm_ref[:, 0:c], pv[11:12])
        d_sc_out, d_sg = group_norm_bwd(sc_out, dm_ref[:, c:2 * c], pv[12:13])
        dpv_ref[11:12, :] += d_og
        dpv_ref[12:13, :] += d_sg

        dhs = d_lru_out * gelu_y
        dgelu = 0.5 * (1.0 + th) + 0.5 * y * (1.0 - th * th) * GELU_K * (1.0 + 3.0 * GELU_C * y * y)
        dy = d_lru_out * hs * dgelu

        _scan_block(a, a * dhs, sa_ref, su_ref, pc_ref, p_ref, reverse=True)
        (p_next,) = _shifted(ext_ref, p_ref[...], None, hp_ref[...], ups=(1,))
        hp_ref[...] = p_ref[0:8, :]
        q = dhs + p_next
        (hs_prev,) = _shifted(ext_ref, hs, hsp_ref[...] * has_prev, None, downs=(1,))
        duu = q * maskf
        da = q * hs_prev

        dmult = duu * gxx
        dgxx = duu * mult
        dgx = dgxx * xc
        dxc = dgxx * gx
        dlog_a = da * a - dmult * ((1.0 + e2) / mult)
        dga = dlog_a * (-LRU_C * sp)
        dsp = jnp.sum(dlog_a * (-LRU_C * ga), axis=0, keepdims=True)
        dpv_ref[7:8, :] += dsp * (-jax.nn.sigmoid(-pv[7:8]))
        dga_pre = dga * ga * (1.0 - ga)
        dgx_pre = dgx * gx * (1.0 - gx)
        dpv_ref[5:6, :] += jnp.sum(dga_pre, axis=0, keepdims=True)
        dpv_ref[6:7, :] += jnp.sum(dgx_pre, axis=0, keepdims=True)
        dga_b = dga_pre.astype(MXU_DTYPE)
        dgx_b = dgx_pre.astype(MXU_DTYPE)
        dxc = dxc + _block_mm(dga_b, wa_ref, True) + _block_mm(dgx_b, wx_ref, True)
        for j in range(c // BD):
            sl = slice(j * BD, (j + 1) * BD)
            dwa_ref[j] += _dot_tn(xcb[:, sl], dga_b[:, sl])
            dwx_ref[j] += _dot_tn(xcb[:, sl], dgx_b[:, sl])

        dpv_ref[4:5, :] += jnp.sum(dxc, axis=0, keepdims=True)
        dpv_ref[3:4, :] += jnp.sum(dxc * xl, axis=0, keepdims=True)
        dpv_ref[2:3, :] += jnp.sum(dxc * x1, axis=0, keepdims=True)
        dpv_ref[1:2, :] += jnp.sum(dxc * x2, axis=0, keepdims=True)
        dpv_ref[0:1, :] += jnp.sum(dxc * x3, axis=0, keepdims=True)
        u1, u2, u3 = _shifted(ext_ref, dxc, None, hxc_ref[...], ups=(1, 2, 3))
        hxc_ref[...] = dxc[0:8]
        dxl = pv[3:4] * dxc + pv[2:3] * u1 + pv[1:2] * u2 + pv[0:1] * u3

        dbs = d_sc_out * sc
        dsc = d_sc_out * bs
        dpv_ref[10:11, :] += jnp.sum(dsc * cv, axis=0, keepdims=True)
        dpv_ref[9:10, :] += jnp.sum(dsc * c1, axis=0, keepdims=True)
        dpv_ref[8:9, :] += jnp.sum(dsc * c2, axis=0, keepdims=True)
        s1, s2 = _shifted(ext_ref, dsc, None, hsc_ref[...], ups=(1, 2))
        hsc_ref[...] = dsc[0:8]
        dcv = pv[10:11] * dsc + pv[9:10] * s1 + pv[8:9] * s2

        dz_ref[:, 0:c] = dy.astype(dz_ref.dtype)
        dz_ref[:, c:2 * c] = dxl.astype(dz_ref.dtype)
        dz_ref[:, 2 * c:3 * c] = dbs.astype(dz_ref.dtype)
        dz_ref[:, 3 * c:4 * c] = (dcv * vs).astype(dz_ref.dtype)
        dz_ref[:, 4 * c:5 * c] = (dcv * cs).astype(dz_ref.dtype)

        if pad:
            @pl.when(b == 0)
            def _():
                dz_ref[0:pad, :] = jnp.zeros((pad, 5 * c), dz_ref.dtype)

    full = lambda shape: pl.BlockSpec(shape, lambda i: (0,) * len(shape))
    cur = lambda width: pl.BlockSpec((r, width), lambda i: (nb - 1 - i, 0))
    prev8 = lambda width: pl.BlockSpec((8, width), lambda i: (jnp.maximum((nb - 1 - i) * r8 - 1, 0), 0))
    return _call(
        body, grid=(nb,),
        in_specs=[cur(5 * c), prev8(5 * c), cur(c), prev8(c), cur(2 * c),
                  full(pv.shape), full(wa.shape), full(wx.shape), full(gm.shape)],
        out_specs=[cur(5 * c), full(pv.shape), full(wa.shape), full(wx.shape)],
        out_shape=[jax.ShapeDtypeStruct((m, 5 * c), MXU_DTYPE), jax.ShapeDtypeStruct(pv.shape, F32),
                   jax.ShapeDtypeStruct(wa.shape, F32), jax.ShapeDtypeStruct(wx.shape, F32)],
        scratch_shapes=[pltpu.VMEM((r + 16, c), F32), pltpu.VMEM((8, c), F32), pltpu.VMEM((8, c), F32),
                        pltpu.VMEM((8, c), F32), pltpu.VMEM((1, c), F32), pltpu.VMEM((r, c), F32),
                        pltpu.VMEM((r, c), F32), pltpu.VMEM((r, c), F32)],
        name=name, args=(z, z, hs, hs, dmixed, pv, wa, wx, gm), carried=carried)


def _position():
    return lax.axis_index("x"), lax.axis_index("y"), lax.axis_index("c")


def _block_of(px, py, pc):
    return 4 * px + 2 * py + pc


class _TwoLevelGather:
    def __init__(self, n_arrays, rows_of, src_of, send_sems, recv_sems):
        x, y, c = _position()
        self.n, self.rows_of, self.src_of = n_arrays, rows_of, src_of
        self.send_sems, self.recv_sems = send_sems, recv_sems
        self.c, self.me, self.sibling = c, (x, y, c), (x, y, 1 - c)
        self.chips = [(1 - x, y), (x, 1 - y), (1 - x, 1 - y)]

    def _copy(self, i, k, block, to, src=None):
        return pltpu.make_async_remote_copy(
            src_ref=self.rows_of(i, *block) if src is None else src, dst_ref=self.rows_of(i, *block),
            send_sem=self.send_sems.at[7 * i + k], recv_sem=self.recv_sems.at[7 * i + k],
            device_id=to, device_id_type=MESH)

    def _first(self, i):
        own = [self._copy(i, 0, self.me, self.sibling, src=self.src_of(i))]
        return own + [self._copy(i, 1 + j, self.me, (*chip, self.c), src=self.src_of(i))
                      for j, chip in enumerate(self.chips)]

    def _passed(self, i, j):
        return self._copy(i, 4 + j, (*self.chips[j], self.c), self.sibling)

    def start(self):
        for i in range(self.n):
            for cp in self._first(i):
                cp.start()

    def forward(self):
        for i in range(self.n):
            for j, chip in enumerate(self.chips):
                self._copy(i, 1 + j, (*chip, self.c), self.me).wait_recv()
                self._passed(i, j).start()

    def drain(self):
        for i in range(self.n):
            self._copy(i, 0, self.sibling, self.me).wait_recv()
            for j, chip in enumerate(self.chips):
                self._copy(i, 4 + j, (*chip, 1 - self.c), self.me).wait_recv()
        for i in range(self.n):
            for cp in self._first(i) + [self._passed(i, j) for j in range(3)]:
                cp.wait_send()


class _RelayGather:
    def __init__(self, n_arrays, rows_of, src_of, send_sems, recv_sems):
        x, y, c = _position()
        self.n, self.rows_of, self.src_of = n_arrays, rows_of, src_of
        self.send_sems, self.recv_sems = send_sems, recv_sems
        self.me, self.sibling = (x, y, c), (x, y, 1 - c)
        self.xn, self.yn, self.dg = (1 - x, y, c), (x, 1 - y, c), (1 - x, 1 - y, c)

    def _copy(self, i, k, block, to, half=None, src=None):
        rows = self.rows_of(i, *block, half)
        return pltpu.make_async_remote_copy(
            src_ref=rows if src is None else src, dst_ref=rows,
            send_sem=self.send_sems.at[8 * i + k], recv_sem=self.recv_sems.at[8 * i + k],
            device_id=to, device_id_type=MESH)

    def _sends(self, i):
        own = self.src_of(i)
        return [self._copy(i, 0, self.me, self.sibling, src=own), self._copy(i, 1, self.me, self.xn, src=own),
                self._copy(i, 2, self.me, self.yn, src=own),
                self._copy(i, 3, self.xn, self.yn, half=0), self._copy(i, 4, self.yn, self.xn, half=1),
                self._copy(i, 5, self.xn, self.sibling), self._copy(i, 6, self.yn, self.sibling),
                self._copy(i, 7, self.dg, self.sibling)]

    def start(self):
        for i in range(self.n):
            for cp in self._sends(i)[0:3]:
                cp.start()

    def forward(self):
        for i in range(self.n):
            self._copy(i, 1, self.xn, self.me).wait_recv()
            self._copy(i, 2, self.yn, self.me).wait_recv()
            for cp in self._sends(i)[3:7]:
                cp.start()

    def drain(self):
        x, y, c = self.me
        for i in range(self.n):
            self._copy(i, 3, self.dg, self.me, half=0).wait_recv()
            self._copy(i, 4, self.dg, self.me, half=1).wait_recv()
            self._sends(i)[7].start()
        for i in range(self.n):
            self._copy(i, 0, self.sibling, self.me).wait_recv()
            self._copy(i, 5, (1 - x, y, 1 - c), self.me).wait_recv()
            self._copy(i, 6, (x, 1 - y, 1 - c), self.me).wait_recv()
            self._copy(i, 7, (1 - x, 1 - y, 1 - c), self.me).wait_recv()
        for i in range(self.n):
            for cp in self._sends(i):
                cp.wait_send()


class _CarriedGather:
    def __init__(self, shards, padded_rows, zeros, forward_at, part=None, into=None):
        d = shards[0].shape[1]
        self.forward_at = forward_at
        self.n = len(shards)
        self.rows = [s.shape[0] for s in shards]
        self.pads = [p - N_DEV * r for r, p in zip(self.rows, padded_rows)]
        assert max(self.pads) <= zeros.shape[0] and zeros.shape[1] == d
        self.part = part if part is not None else (0, self.rows[0])
        assert (part is None and into is None) or self.n == 1
        assert self.part[0] % SUBLANE_BF16 == 0 and self.part[1] % SUBLANE_BF16 == 0
        self.arrays = list(shards) + [zeros] + ([into] if into is not None else [])
        self.out_shapes = [jax.ShapeDtypeStruct((p, d), s.dtype) for s, p in zip(shards, padded_rows)]
        self.aliases = {self.n + 1: 0} if into is not None else {}
        if into is not None:
            self.pads = [0] * self.n
        self.n_remote, self.n_local = 8 * self.n, 2 * self.n
        self.results = None

    def _rows_of(self, outs):
        def rows_of(i, px, py, pc, half):
            first, count = (self.part if self.n == 1 else (0, self.rows[i]))
            head = _round_up(count // 2, SUBLANE_BF16)
            if half == 0:
                count = head
            elif half == 1:
                first, count = first + head, count - head
            first = _block_of(px, py, pc) * self.rows[i] + first
            return outs[i].at[pl.ds(pl.multiple_of(first, SUBLANE_BF16), count), :]
        return rows_of

    def _own(self, ins, i):
        return ins[i].at[pl.ds(self.part[0], self.part[1]), :] if self.n == 1 else ins[i]

    def _gather(self, ins, outs, send_sems, recv_sems):
        return _RelayGather(self.n, self._rows_of(outs), functools.partial(self._own, ins), send_sems, recv_sems)

    def _local(self, ins, outs, local_sems):
        x, y, c = _position()
        rows_of = self._rows_of(outs)
        cps = []
        for i in range(self.n):
            cps.append(pltpu.make_async_copy(self._own(ins, i), rows_of(i, x, y, c, None), local_sems.at[2 * i]))
            if self.pads[i]:
                cps.append(pltpu.make_async_copy(ins[self.n].at[pl.ds(0, self.pads[i]), :],
                                                 outs[i].at[pl.ds(N_DEV * self.rows[i], self.pads[i]), :],
                                                 local_sems.at[2 * i + 1]))
        return cps

    def start(self, ins, outs, send_sems, recv_sems, local_sems):
        for cp in self._local(ins, outs, local_sems):
            cp.start()
        self._gather(ins, outs, send_sems, recv_sems).start()

    def forward(self, ins, outs, send_sems, recv_sems, local_sems):
        self._gather(ins, outs, send_sems, recv_sems).forward()

    def finish(self, ins, outs, send_sems, recv_sems, local_sems):
        self._gather(ins, outs, send_sems, recv_sems).drain()
        for cp in self._local(ins, outs, local_sems):
            cp.wait()


class _CarriedSwap:
    def __init__(self, grads, shard_rows):
        d = grads[0].shape[1]
        self.n, self.rows = len(grads), list(shard_rows)
        self.arrays = list(grads)
        self.out_shapes = [jax.ShapeDtypeStruct((4, s, d), g.dtype) for g, s in zip(grads, shard_rows)]
        self.aliases = {}
        self.n_remote, self.n_local = 4 * self.n, 0
        self.forward_at = 1.0
        self.results = None

    def _copies(self, ins, outs, send_sems, recv_sems):
        x, y, c = _position()
        cps = []
        for i in range(self.n):
            s = self.rows[i]
            for k in range(4):
                blk = _block_of(k >> 1, k & 1, 1 - c)
                cps.append(pltpu.make_async_remote_copy(
                    src_ref=ins[i].at[pl.ds(pl.multiple_of(blk * s, SUBLANE_BF16), s), :], dst_ref=outs[i].at[k],
                    send_sem=send_sems.at[4 * i + k], recv_sem=recv_sems.at[4 * i + k],
                    device_id=(x, y, 1 - c), device_id_type=MESH))
        return cps

    def start(self, ins, outs, send_sems, recv_sems, local_sems):
        for cp in self._copies(ins, outs, send_sems, recv_sems):
            cp.start()

    def forward(self, *_):
        pass

    def finish(self, ins, outs, send_sems, recv_sems, local_sems):
        for cp in self._copies(ins, outs, send_sems, recv_sems):
            cp.wait()


class _CarriedChipExchange:
    def __init__(self, presums, part=None, into=None):
        self.n = len(presums)
        assert (part is None and into is None) or self.n == 1
        self.part = part if part is not None else (0, presums[0].shape[1])
        assert self.part[0] % SUBLANE_BF16 == 0 and self.part[1] % SUBLANE_BF16 == 0
        self.arrays = list(presums) + ([into] if into is not None else [])
        self.out_shapes = [jax.ShapeDtypeStruct(p.shape, p.dtype) for p in presums]
        self.aliases = {self.n: 0} if into is not None else {}
        self.n_remote, self.n_local = 3 * self.n, 0
        self.forward_at = 1.0
        self.results = None

    def _copies(self, ins, outs, send_sems, recv_sems):
        x, y, c = _position()
        cps = []
        for i in range(self.n):
            rows = pl.ds(*self.part) if self.n == 1 else pl.ds(0, self.arrays[i].shape[1])
            for r in range(1, 4):
                cps.append(pltpu.make_async_remote_copy(
                    src_ref=ins[i].at[r - 1, rows, :], dst_ref=outs[i].at[r - 1, rows, :],
                    send_sem=send_sems.at[3 * i + r - 1], recv_sem=recv_sems.at[3 * i + r - 1],
                    device_id=(x ^ (r >> 1), y ^ (r & 1), c), device_id_type=MESH))
        return cps

    def start(self, ins, outs, send_sems, recv_sems, local_sems):
        for cp in self._copies(ins, outs, send_sems, recv_sems):
            cp.start()

    def forward(self, *_):
        pass

    def finish(self, ins, outs, send_sems, recv_sems, local_sems):
        for cp in self._copies(ins, outs, send_sems, recv_sems):
            cp.wait()


def _gather_small(block, reduce, name):
    rr, nn = block.shape

    def body(x_ref, out_ref, *rest):
        if reduce:
            stack_ref, send_sems, recv_sems, local_sem = rest
        else:
            send_sems, recv_sems, local_sem = rest
            stack_ref = out_ref
        x, y, c = _position()

        def rows_of(i, px, py, pc):
            return stack_ref.at[pl.ds(pl.multiple_of(_block_of(px, py, pc) * rr, 8), rr), :]

        own = pltpu.make_async_copy(x_ref, rows_of(0, x, y, c), local_sem)
        own.start()
        gather = _TwoLevelGather(1, rows_of, lambda i: x_ref, send_sems, recv_sems)
        gather.start()
        gather.forward()
        gather.drain()
        own.wait()
        if reduce:
            acc = stack_ref[0:rr, :]
            for k in range(1, N_DEV):
                acc = acc + stack_ref[k * rr:(k + 1) * rr, :]
            out_ref[...] = acc

    vmem = pl.BlockSpec(memory_space=pltpu.VMEM)
    scratch = [pltpu.SemaphoreType.DMA((7,)), pltpu.SemaphoreType.DMA((7,)), pltpu.SemaphoreType.DMA]
    if reduce:
        scratch = [pltpu.VMEM((N_DEV * rr, nn), F32)] + scratch
    out_rows = rr if reduce else N_DEV * rr
    return pl.pallas_call(
        body, in_specs=[vmem], out_specs=vmem, out_shape=jax.ShapeDtypeStruct((out_rows, nn), F32),
        scratch_shapes=scratch, name=name, compiler_params=_params())(block)


def _sum_stack(stack, name):
    rr = stack.shape[0] // N_DEV

    def body(s_ref, o_ref):
        acc = s_ref[0:rr, :]
        for k in range(1, N_DEV):
            acc = acc + s_ref[k * rr:(k + 1) * rr, :]
        o_ref[...] = acc

    vmem = pl.BlockSpec(memory_space=pltpu.VMEM)
    return pl.pallas_call(body, in_specs=[vmem], out_specs=vmem,
                          out_shape=jax.ShapeDtypeStruct((rr, stack.shape[1]), F32), name=name,
                          compiler_params=_params())(stack)


def _presum(where, grad, swapped, name):
    s, d = swapped.shape[1], swapped.shape[2]
    tc = _tile(d, 2048, LANE)

    def body(where_ref, g_ref, sw_ref, o_ref):
        o_ref[0] = (g_ref[...].astype(F32) + sw_ref[0].astype(F32)).astype(o_ref.dtype)

    return _call(
        body, grid=(3, d // tc),
        in_specs=[pl.BlockSpec((s, tc), lambda r, j, where: (where[1 + r], j)),
                  pl.BlockSpec((1, s, tc), lambda r, j, where: (where[5 + r], 0, j))],
        out_specs=pl.BlockSpec((1, s, tc), lambda r, j, where: (r, 0, j)),
        out_shape=jax.ShapeDtypeStruct((3, s, d), WIRE_DTYPE), name=name, args=(grad, swapped), prefetch=(where,))


def _final_sum(where, grad, swapped, received, name, carried=()):
    s, d = swapped.shape[1], swapped.shape[2]
    tc = _tile(d, 512, LANE)

    def body(where_ref, g_ref, sw_ref, r_ref, o_ref):
        acc = g_ref[...].astype(F32) + sw_ref[0].astype(F32)
        for k in range(3):
            acc = acc + r_ref[k].astype(F32)
        o_ref[...] = acc

    return _call(
        body, grid=(d // tc,),
        in_specs=[pl.BlockSpec((s, tc), lambda j, where: (where[0], j)),
                  pl.BlockSpec((1, s, tc), lambda j, where: (where[4], 0, j)),
                  pl.BlockSpec((3, s, tc), lambda j, where: (0, 0, j))],
        out_specs=pl.BlockSpec((s, tc), lambda j, where: (0, j)),
        out_shape=jax.ShapeDtypeStruct((s, d), F32), name=name, args=(grad, swapped, received),
        prefetch=(where,), carried=carried)


class _GradReduction:
    def __init__(self, key, grad, shard_rows, where):
        self.key, self.grad, self.rows, self.where = key, grad, shard_rows, where
        self._presum = self._exchange = None

    def swap(self):
        self._swap = _CarriedSwap([self.grad], [self.rows])
        return self._swap

    def exchange(self, part=None):
        if self._presum is None:
            self._presum = _presum(self.where, self.grad, self._swap.results[0], "presum_" + self.key)
        rows = None
        if part is not None:
            half = _round_up(self.rows // 2, SUBLANE_BF16)
            rows = (0, half) if part == 0 else (half, self.rows - half)
        into = self._exchange.results[0] if part == 1 else None
        self._exchange = _CarriedChipExchange([self._presum], rows, into)
        return self._exchange

    def total(self, carried=()):
        return _final_sum(self.where, self.grad, self._swap.results[0], self._exchange.results[0],
                          "sum_" + self.key, carried)

    def total_and_update(self, w, m, v):
        return _sum_adamw(self.where, self.grad, self._swap.results[0], self._exchange.results[0], w, m, v,
                          "update_" + self.key)


def _adamw_math(w, g, m, v):
    nm = ADAM_B1 * m + (1.0 - ADAM_B1) * g
    nv = ADAM_B2 * v + (1.0 - ADAM_B2) * (g * g)
    m_hat = nm / (1.0 - ADAM_B1 ** ADAM_STEP)
    v_hat = nv / (1.0 - ADAM_B2 ** ADAM_STEP)
    return -ADAM_LR * (m_hat / (jnp.sqrt(v_hat) + ADAM_EPS) + ADAM_WD * w), nm, nv


def _sum_adamw(where, grad, swapped, received, w, m, v, name):
    s, d = swapped.shape[1], swapped.shape[2]
    tc = _tile(d, 512, LANE)

    def body(where_ref, g_ref, sw_ref, r_ref, w_ref, m_ref, v_ref, gs_ref, d_ref, nm_ref, nv_ref):
        g = g_ref[...].astype(F32) + sw_ref[0].astype(F32)
        for k in range(3):
            g = g + r_ref[k].astype(F32)
        gs_ref[...] = g
        d_ref[...], nm_ref[...], nv_ref[...] = _adamw_math(w_ref[...], g, m_ref[...], v_ref[...])

    blk = pl.BlockSpec((s, tc), lambda j, where: (0, j))
    return _call(
        body, grid=(d // tc,),
        in_specs=[pl.BlockSpec((s, tc), lambda j, where: (where[0], j)),
                  pl.BlockSpec((1, s, tc), lambda j, where: (where[4], 0, j)),
                  pl.BlockSpec((3, s, tc), lambda j, where: (0, 0, j)), blk, blk, blk],
        out_specs=[blk] * 4, out_shape=[jax.ShapeDtypeStruct((s, d), F32)] * 4, name=name,
        args=(grad, swapped, received, w, m, v), prefetch=(where,))


def _adamw(w, g, m, v, name):
    rows, cols = w.shape
    tr = _tile(rows, 256, 8)

    def body(w_ref, g_ref, m_ref, v_ref, d_ref, nm_ref, nv_ref):
        d_ref[...], nm_ref[...], nv_ref[...] = _adamw_math(w_ref[...], g_ref[...], m_ref[...], v_ref[...])

    spec = pl.BlockSpec((tr, cols), lambda i: (i, 0))
    return pl.pallas_call(
        body, grid=(rows // tr,), in_specs=[spec] * 4, out_specs=[spec] * 3,
        out_shape=[jax.ShapeDtypeStruct((rows, cols), F32)] * 3, name=name, compiler_params=_params())(w, g, m, v)


def _pack_rows(arrays, width, row_quantum=8):
    flat = jnp.concatenate([a.reshape(-1) for a in arrays])
    total = _round_up(flat.shape[0], row_quantum * width)
    flat = jnp.pad(flat, (0, total - flat.shape[0]))
    return flat.reshape(-1, width)


def _unpack_rows(packed, shapes):
    flat = packed.reshape(-1)
    out = []
    off = 0
    for shp in shapes:
        size = 1
        for s in shp:
            size *= s
        out.append(flat[off:off + size].reshape(shp))
        off += size
    return out


def _block_diag(w):
    h, hb, _ = w.shape
    per = BD // hb
    w4 = w.reshape(h // per, per, hb, hb)
    eye = jnp.eye(per, dtype=w.dtype)
    return jnp.einsum('npij,pq->npiqj', w4, eye).reshape(h // per, BD, BD)


def _block_diag_extract(bd, hb):
    nbk = bd.shape[0]
    per = BD // hb
    b5 = bd.reshape(nbk, per, hb, per, hb)
    eye = jnp.eye(per, dtype=bd.dtype)
    return jnp.einsum('npiqj,pq->npij', b5, eye).reshape(nbk * per, hb, hb)


def kernel(x, meta_tokens, ffn1_pre_g, ffn1_w_gate, ffn1_w_up, ffn1_w_down, ffn1_post_g, mix_pre_g, w_in, lru_conv_w, lru_conv_b, lru_w_a, lru_b_a, lru_w_x, lru_b_x, lru_lambda, sconv_w, lru_out_g, sconv_out_g, w_out, mix_post_g, ffn2_pre_g, ffn2_w_gate, ffn2_w_up, ffn2_w_down, ffn2_post_g, loss_target, m_meta_tokens, m_ffn1_pre_g, m_ffn1_w_gate, m_ffn1_w_up, m_ffn1_w_down, m_ffn1_post_g, m_mix_pre_g, m_w_in, m_lru_conv_w, m_lru_conv_b, m_lru_w_a, m_lru_b_a, m_lru_w_x, m_lru_b_x, m_lru_lambda, m_sconv_w, m_lru_out_g, m_sconv_out_g, m_w_out, m_mix_post_g, m_ffn2_pre_g, m_ffn2_w_gate, m_ffn2_w_up, m_ffn2_w_down, m_ffn2_post_g, v_meta_tokens, v_ffn1_pre_g, v_ffn1_w_gate, v_ffn1_w_up, v_ffn1_w_down, v_ffn1_post_g, v_mix_pre_g, v_w_in, v_lru_conv_w, v_lru_conv_b, v_lru_w_a, v_lru_b_a, v_lru_w_x, v_lru_b_x, v_lru_lambda, v_sconv_w, v_lru_out_g, v_sconv_out_g, v_w_out, v_mix_post_g, v_ffn2_pre_g, v_ffn2_w_gate, v_ffn2_w_up, v_ffn2_w_down, v_ffn2_post_g):
    given = dict(locals())
    wts = {n: given[n] for n in WEIGHT_NAMES}
    mom = {n: given["m_" + n] for n in WEIGHT_NAMES}
    var = {n: given["v_" + n] for n in WEIGHT_NAMES}

    xi, yi, ci = _position()
    me = _block_of(xi, yi, ci)
    x2 = x[0]
    seq, d = x2.shape
    n_meta = meta_tokens.shape[0]
    m_rows = _round_up(n_meta + seq, ROW_ALIGN)
    pad = m_rows - n_meta - seq
    lead = pad + n_meta
    c = lru_conv_b.shape[1]
    hb = lru_w_a.shape[-1]
    dm = meta_tokens.shape[1]
    cs_ = lru_conv_w.shape[2]
    kw4, kw3 = lru_conv_w.shape[1], sconv_w.shape[1]
    assert d == 2 * c and c % BD == 0 and BD % hb == 0 and cs_ <= dm and kw4 == 4 and kw3 == 3

    small = jnp.zeros((_round_up(n_meta + kw4 + kw3, 8), dm), F32)
    small = small.at[0:n_meta].set(meta_tokens)
    small = small.at[n_meta:n_meta + kw4, 0:cs_].set(lru_conv_w[0])
    small = small.at[n_meta + kw4:n_meta + kw4 + kw3, 0:cs_].set(sconv_w[0])
    sr = small.shape[0]
    small_all = _gather_small(small, False, "gather_small").reshape(N_DEV, sr, dm)
    meta_full = small_all[:, 0:n_meta, :].transpose(1, 0, 2).reshape(n_meta, d)
    conv_w_full = small_all[:, n_meta:n_meta + kw4, 0:cs_].transpose(1, 0, 2).reshape(kw4, c)
    sconv_w_full = small_all[:, n_meta + kw4:n_meta + kw4 + kw3, 0:cs_].transpose(1, 0, 2).reshape(kw3, c)

    big = ['ffn1_w_gate', 'ffn1_w_up', 'ffn1_w_down', 'w_in', 'w_out', 'ffn2_w_gate', 'ffn2_w_up', 'ffn2_w_down']
    col_sharded = {'ffn1_w_gate', 'ffn1_w_up', 'w_in', 'ffn2_w_gate', 'ffn2_w_up'}
    shards = []
    for nme in big:
        w = wts[nme][0].astype(WIRE_DTYPE)
        shards.append(w.T if nme in col_sharded else w)
    shard_rows = dict(zip(big, [s.shape[0] for s in shards]))
    zeros = jnp.zeros((F_ALIGN, d), WIRE_DTYPE)

    def gather(forward_at, *names, part=None, into=None):
        sel = [shards[big.index(nme)] for nme in names]
        padded = [_round_up(N_DEV * shard_rows[nme], LANE if nme in ('w_in', 'w_out') else F_ALIGN) for nme in names]
        return _CarriedGather(sel, padded, zeros, forward_at, part, into)

    pv = jnp.zeros((16, c), F32)
    pv = pv.at[0:4].set(conv_w_full).at[4].set(lru_conv_b[0]).at[5].set(lru_b_a[0]).at[6].set(lru_b_x[0])
    pv = pv.at[7].set(lru_lambda[0]).at[8:11].set(sconv_w_full).at[11].set(lru_out_g[0]).at[12].set(sconv_out_g[0])
    wa_bd = _block_diag(lru_w_a[0]).astype(MXU_DTYPE)
    wx_bd = _block_diag(lru_w_x[0]).astype(MXU_DTYPE)
    gs = c // N_GROUPS
    gidx = jnp.arange(BD) // gs
    gm = jnp.where(gidx[:, None] == gidx[None, :], 1.0 / gs, 0.0).astype(MXU_DTYPE)

    ride = gather(0.3, 'ffn1_w_gate')
    h0, n1, target = _embed(x2, meta_full, loss_target[0], ffn1_pre_g, pad, "embed_prenorm", carried=[ride])
    (wg1,) = ride.results
    ride = gather(0.6, 'ffn1_w_up')
    g1 = _mm_nt(n1, wg1, "ffn1_gate", carried=[ride], out_dtype=MXU_DTYPE)
    (wu1,) = ride.results
    ride = gather(0.6, 'ffn1_w_down')
    u1, a1 = _ffn_up_act(n1, wu1, g1, "ffn1_up_act", carried=[ride])
    (wd1,) = ride.results
    ride = gather(0.75, 'w_in', 'w_out')
    fo1, h1, un = _mm_residual_norm(a1, wd1, h0, ffn1_post_g, 0.5, mix_pre_g, "ffn1_down", carried=[ride])
    win_t, wout = ride.results
    s2 = shard_rows['ffn2_w_gate']
    quarter = _round_up(s2 // 4, SUBLANE_BF16)
    ride_g = gather(0.5, 'ffn2_w_gate', part=(0, 3 * quarter))
    z = _mm_nt(un, win_t, "mix_in_proj", carried=[ride_g])
    ride_g = gather(0.5, 'ffn2_w_gate', part=(3 * quarter, s2 - 3 * quarter), into=ride_g.results[0])
    ride_u = gather(0.5, 'ffn2_w_up', part=(0, quarter))
    mixed, hs = _mixer_fwd(z, pv, wa_bd, wx_bd, gm, pad, "mixer_fwd", carried=[ride_g, ride_u])
    (wg2,) = ride_g.results
    ride_u = gather(0.5, 'ffn2_w_up', part=(quarter, s2 - quarter), into=ride_u.results[0])
    o_mix, h2, n2 = _mm_residual_norm(mixed, wout, h1, mix_post_g, 1.0, ffn2_pre_g, "mix_out_proj", carried=[ride_u])
    (wu2,) = ride_u.results
    ride = gather(0.75, 'ffn2_w_down')
    g2, u2, a2 = _ffn_gate_up(n2, wg2, wu2, "ffn2_gate_up", carried=[ride])
    (wd2,) = ride.results
    dh3, dfo2, d_post2, loss_part = _mm_residual_loss(a2, wd2, h2, ffn2_post_g, 0.5, target, lead, "ffn2_down_loss")

    chip_rel = [2 * (xi ^ (r >> 1)) + (yi ^ (r & 1)) for r in range(4)]
    where = jnp.stack([2 * k + ci for k in chip_rel] + chip_rel).astype(jnp.int32)
    red = {}

    def reduction(nme, grad):
        red[nme] = _GradReduction(nme, grad, shard_rows[nme], where)
        return red[nme]

    r_wd2 = reduction('ffn2_w_down', _mm_tn(a2, dfo2, "ffn2_dw_down"))
    dg2, du2 = _ffn_hidden_bwd(dfo2, wd2, g2, u2, "ffn2_hidden_bwd", carried=[r_wd2.swap()])
    r_wg2 = reduction('ffn2_w_gate', _mm_tn(dg2, n2, "ffn2_dw_gate", carried=[r_wd2.exchange(part=0)]))
    r_wu2 = reduction('ffn2_w_up', _mm_tn(du2, n2, "ffn2_dw_up", carried=[r_wd2.exchange(part=1), r_wg2.swap()]))
    dh2, d_pre2 = _mm_norm_bwd([(dg2, wg2), (du2, wu2)], h2, ffn2_pre_g, dh3, "ffn2_dx",
                               carried=[r_wg2.exchange(), r_wu2.swap()])
    do_mix, d_mix_post, dmixed = _norm_bwd_mm_nt(o_mix, mix_post_g, dh2, 1.0, wout, "mix_out_proj_bwd")
    r_wout = reduction('w_out', _mm_tn(mixed, do_mix, "mix_dw_out"))
    dz, dpv, dwa_bd, dwx_bd = _mixer_bwd(z, hs, dmixed, pv, wa_bd, wx_bd, gm, pad, "mixer_bwd",
                                         carried=[r_wu2.exchange(), r_wout.swap()])
    r_win = reduction('w_in', _mm_tn(dz, un, "mix_dw_in", carried=[r_wout.exchange()]))
    dh1, d_mix_pre, dfo1, d_post1 = _mm_norm_bwd([(dz, win_t)], h1, mix_pre_g, dh2, "mix_dx", carried=[r_win.swap()],
                                                 post=(fo1, ffn1_post_g, 0.5))
    r_wd1 = reduction('ffn1_w_down', _mm_tn(a1, dfo1, "ffn1_dw_down", carried=[r_win.exchange(part=0)]))
    early_names = ['mix_pre_g', 'mix_post_g', 'ffn2_pre_g', 'ffn2_post_g', 'ffn1_post_g',
                   'lru_conv_b', 'lru_b_a', 'lru_b_x', 'lru_lambda', 'lru_out_g', 'sconv_out_g',
                   'lru_conv_w', 'sconv_w', 'lru_w_a', 'lru_w_x']
    early_parts = [d_mix_pre, d_mix_post, d_pre2, d_post2, d_post1,
                   dpv[4:5], dpv[5:6], dpv[6:7], dpv[7:8], dpv[11:12], dpv[12:13],
                   dpv[0:4], dpv[8:11], _block_diag_extract(dwa_bd, hb), _block_diag_extract(dwx_bd, hb)]
    early_packed = _pack_rows(early_parts, d, SUBLANE_BF16)
    early_ride = _CarriedGather([early_packed], [N_DEV * early_packed.shape[0]], zeros, 0.75)
    dg1, du1 = _ffn_hidden_bwd(dfo1, wd1, g1, u1, "ffn1_hidden_bwd",
                               carried=[r_win.exchange(part=1), r_wd1.swap(), early_ride])
    early_sum = _sum_stack(early_ride.results[0], "sum_small_early")
    r_wg1 = reduction('ffn1_w_gate', _mm_tn(dg1, n1, "ffn1_dw_gate", carried=[r_wd1.exchange(part=0)]))
    r_wu1 = reduction('ffn1_w_up', _mm_tn(du1, n1, "ffn1_dw_up", carried=[r_wd1.exchange(part=1), r_wg1.swap()]))
    row_tile = _norm_bwd_row_tile(m_rows)
    n_tiles = m_rows // row_tile
    half = n_tiles // 2
    assert half >= 1 and half * row_tile >= lead
    dh0_a, d_pre1_a = _mm_norm_bwd([(dg1, wg1), (du1, wu1)], h0, ffn1_pre_g, dh1, "ffn1_dx_a",
                                   carried=[r_wg1.exchange(), r_wu1.swap()], row_tiles=(0, half))
    dh0_b, d_pre1 = _mm_norm_bwd([(dg1, wg1), (du1, wu1)], h0, ffn1_pre_g, dh1, "ffn1_dx_b",
                                 carried=[r_wu1.exchange()], row_tiles=(half, n_tiles - half), dg_init=d_pre1_a)
    grad_x = jnp.concatenate([dh0_a[lead:], dh0_b], axis=0)[None]
    d_meta = dh0_a[pad:lead]

    grads, delta, new_m, new_v = {}, {}, {}, {}
    for nme in big:
        in_shard_layout = nme not in col_sharded or shard_rows[nme] % LANE != 0
        if in_shard_layout:
            view = (lambda t: t[0].T) if nme in col_sharded else (lambda t: t[0])
            back = (lambda t: t.T[None]) if nme in col_sharded else (lambda t: t[None])
            outs = red[nme].total_and_update(view(wts[nme]), view(mom[nme]), view(var[nme]))
            grads[nme], delta[nme], new_m[nme], new_v[nme] = [back(t) for t in outs]
        else:
            grads[nme] = red[nme].total().T[None]
            outs = _adamw(wts[nme][0], grads[nme][0], mom[nme][0], var[nme][0], "adamw_" + nme)
            delta[nme], new_m[nme], new_v[nme] = [t[None] for t in outs]

    late_names = ['ffn1_pre_g', 'meta_tokens']
    late_parts = [d_pre1, d_meta, loss_part]
    late_sum = _gather_small(_pack_rows(late_parts, d), True, "reduce_small_late")
    small_sums = (_unpack_rows(early_sum, [p.shape for p in early_parts])
                  + _unpack_rows(late_sum, [p.shape for p in late_parts]))
    loss = small_sums.pop()[0, 0]
    for nme, gsm in zip(early_names + late_names, small_sums):
        if nme == 'meta_tokens':
            grads[nme] = lax.dynamic_slice_in_dim(gsm, me * dm, dm, axis=1)
        elif nme in ('lru_conv_w', 'sconv_w'):
            grads[nme] = lax.dynamic_slice_in_dim(gsm, me * cs_, cs_, axis=1)[None]
        else:
            grads[nme] = gsm.reshape(wts[nme].shape)

    rest = [n for n in WEIGHT_NAMES if n not in big]
    rest_shapes = [wts[n].shape for n in rest]
    packed = [_pack_rows([src[n] for n in rest], LANE, 256) for src in (wts, grads, mom, var)]
    for out, packed_out in zip((delta, new_m, new_v), _adamw(*packed, "adamw_small")):
        for nme, arr in zip(rest, _unpack_rows(packed_out, rest_shapes)):
            out[nme] = arr

    return (loss, grad_x, *[grads[n] for n in WEIGHT_NAMES], *[delta[n] for n in WEIGHT_NAMES],
            *[new_m[n] for n in WEIGHT_NAMES], *[new_v[n] for n in WEIGHT_NAMES])
```

```python
import functools

import jax
import jax.numpy as jnp
from jax import lax
from jax.experimental import pallas as pl
from jax.experimental.pallas import tpu as pltpu

F32 = jnp.float32
MXU_DTYPE = jnp.bfloat16
WIRE_DTYPE = jnp.bfloat16
MESH = pl.DeviceIdType.MESH

EPS = 1e-6
LRU_C = 8.0
N_GROUPS = 16
ADAM_LR = 0.001
ADAM_B1 = 0.9
ADAM_B2 = 0.999
ADAM_EPS = 1e-08
ADAM_WD = 0.01
ADAM_STEP = 10

N_DEV = 8
LANE = 128
SUBLANE_BF16 = 16
ROW_ALIGN = 128
F_ALIGN = 512
BD = 256
K_TILE = 512
ACC_ROWS = 528
ACC_GROUP = 1
MIX_ROWS = 128
VMEM_LIMIT_MB = 56

WEIGHT_NAMES = ['meta_tokens', 'ffn1_pre_g', 'ffn1_w_gate', 'ffn1_w_up', 'ffn1_w_down', 'ffn1_post_g',
                'mix_pre_g', 'w_in', 'lru_conv_w', 'lru_conv_b', 'lru_w_a', 'lru_b_a', 'lru_w_x', 'lru_b_x',
                'lru_lambda', 'sconv_w', 'lru_out_g', 'sconv_out_g', 'w_out', 'mix_post_g', 'ffn2_pre_g',
                'ffn2_w_gate', 'ffn2_w_up', 'ffn2_w_down', 'ffn2_post_g']


def _round_up(n, q):
    return (n + q - 1) // q * q


def _tile(n, target, q):
    best = None
    t = q
    while t <= min(n, target):
        if n % t == 0:
            best = t
        t += q
    assert best is not None, (n, target, q)
    return best


def _params(**kw):
    return pltpu.CompilerParams(vmem_limit_bytes=VMEM_LIMIT_MB << 20, **kw)


def _call(body, *, grid, in_specs, out_specs, out_shape, name, args, scratch_shapes=(), carried=(), prefetch=()):
    carried = list(carried)
    n_pf = len(prefetch)

    def launch(fn, in_specs_, out_specs_, out_shape_, scratch_, operands, aliases_):
        if n_pf:
            spec = pltpu.PrefetchScalarGridSpec(num_scalar_prefetch=n_pf, grid=grid, in_specs=in_specs_,
                                                out_specs=out_specs_, scratch_shapes=scratch_)
            return pl.pallas_call(fn, grid_spec=spec, out_shape=out_shape_, input_output_aliases=aliases_,
                                  name=name, compiler_params=_params())(*prefetch, *operands)
        return pl.pallas_call(fn, grid=grid, in_specs=in_specs_, out_specs=out_specs_, out_shape=out_shape_,
                              scratch_shapes=scratch_, input_output_aliases=aliases_, name=name,
                              compiler_params=_params())(*operands)

    if not carried:
        return launch(body, in_specs, out_specs, out_shape, list(scratch_shapes), args, {})
    single = not isinstance(out_shape, (list, tuple))
    out_specs_l = [out_specs] if single else list(out_specs)
    out_shape_l = [out_shape] if single else list(out_shape)
    n_in, n_out, n_scr = len(in_specs), len(out_specs_l), len(scratch_shapes)
    hbm = pl.BlockSpec(memory_space=pl.ANY)
    c_in = [a for cm in carried for a in cm.arrays]
    c_out = [s for cm in carried for s in cm.out_shapes]
    c_scr = []
    aliases = {}
    in_off, out_off = n_pf + n_in, n_out
    for cm in carried:
        c_scr += [pltpu.SemaphoreType.DMA((cm.n_remote,)), pltpu.SemaphoreType.DMA((cm.n_remote,)),
                  pltpu.SemaphoreType.DMA((max(cm.n_local, 1),))]
        for k, v in cm.aliases.items():
            aliases[in_off + k] = out_off + v
        in_off += len(cm.arrays)
        out_off += len(cm.out_shapes)
    steps = 1
    for g in grid:
        steps *= g
    forward_steps = [min(int(cm.forward_at * steps), steps - 1) for cm in carried]

    def wrapped(*refs):
        pf = refs[:n_pf]
        p = n_pf
        ins = refs[p:p + n_in]
        p += n_in
        cins = refs[p:p + len(c_in)]
        p += len(c_in)
        outs = refs[p:p + n_out]
        p += n_out
        couts = refs[p:p + len(c_out)]
        p += len(c_out)
        scr = refs[p:p + n_scr]
        csem = refs[p + n_scr:]
        lin = 0
        for axis, g in enumerate(grid):
            lin = lin * g + pl.program_id(axis)
        views = []
        io = oo = 0
        for j, cm in enumerate(carried):
            views.append((cins[io:io + len(cm.arrays)], couts[oo:oo + len(cm.out_shapes)],
                          csem[3 * j], csem[3 * j + 1], csem[3 * j + 2]))
            io += len(cm.arrays)
            oo += len(cm.out_shapes)

        @pl.when(lin == 0)
        def _():
            for cm, v in zip(carried, views):
                cm.start(*v)

        body(*pf, *ins, *outs, *scr)

        for cm, v, step in zip(carried, views, forward_steps):
            pl.when(lin == step)(functools.partial(cm.forward, *v))

        @pl.when(lin == steps - 1)
        def _():
            for cm, v in zip(carried, views):
                cm.finish(*v)

    res = launch(wrapped, list(in_specs) + [hbm] * len(c_in), out_specs_l + [hbm] * len(c_out),
                 out_shape_l + c_out, list(scratch_shapes) + c_scr, (*args, *c_in), aliases)
    oo = n_out
    for cm in carried:
        cm.results = list(res[oo:oo + len(cm.out_shapes)])
        oo += len(cm.out_shapes)
    return res[0] if single else list(res[:n_out])


def _embed(x, meta, target, g, pad, name, carried=()):
    seq, d = x.shape
    n_meta = meta.shape[0]
    lead = pad + n_meta
    m = lead + seq
    tr = ROW_ALIGN
    lead_blocks = lead // tr
    meta_row = pad - (lead_blocks - 1) * tr
    assert lead % tr == 0 and seq % tr == 0 and 0 <= meta_row and meta_row % 8 == 0

    def body(x_ref, meta_ref, t_ref, g_ref, h_ref, n_ref, tp_ref):
        i = pl.program_id(0)

        @pl.when(i < lead_blocks)
        def _():
            h_ref[...] = jnp.zeros_like(h_ref)
            tp_ref[...] = jnp.zeros_like(tp_ref)

        @pl.when(i == lead_blocks - 1)
        def _():
            h_ref[pl.ds(meta_row, n_meta), :] = meta_ref[...]

        @pl.when(i >= lead_blocks)
        def _():
            h_ref[...] = x_ref[...]
            tp_ref[...] = t_ref[...]

        h = h_ref[...]
        r = lax.rsqrt(jnp.mean(h * h, axis=-1, keepdims=True) + EPS)
        n_ref[...] = (h * r * g_ref[...]).astype(n_ref.dtype)

    tokens = pl.BlockSpec((tr, d), lambda i: (jnp.maximum(i - lead_blocks, 0), 0))
    rows = pl.BlockSpec((tr, d), lambda i: (i, 0))
    return _call(
        body, grid=(m // tr,),
        in_specs=[tokens, pl.BlockSpec((n_meta, d), lambda i: (0, 0)), tokens, pl.BlockSpec((1, d), lambda i: (0, 0))],
        out_specs=[rows, rows, rows],
        out_shape=[jax.ShapeDtypeStruct((m, d), F32), jax.ShapeDtypeStruct((m, d), MXU_DTYPE),
                   jax.ShapeDtypeStruct((m, d), F32)],
        name=name, args=(x, meta, target, g), carried=carried)


def _rmsnorm_bwd_rows(x, g, dy):
    r = lax.rsqrt(jnp.mean(x * x, axis=-1, keepdims=True) + EPS)
    xh = x * r
    dyh = dy * g
    dx = r * (dyh - xh * jnp.mean(dyh * xh, axis=-1, keepdims=True))
    return dx, dy * xh


def _dot_nt(a, b):
    return lax.dot_general(a, b, (((1,), (1,)), ((), ())), preferred_element_type=F32)


def _dot_tn(a, b):
    return lax.dot_general(a, b, (((0,), (0,)), ((), ())), preferred_element_type=F32)


def _mm_nt(a, w, name, carried=(), out_dtype=F32):
    m, k = a.shape
    n = w.shape[0]
    tm = _tile(m, 1056, SUBLANE_BF16)
    tn = _tile(n, 512, LANE)

    def body(a_ref, w_ref, o_ref):
        o_ref[...] = _dot_nt(a_ref[...], w_ref[...]).astype(o_ref.dtype)

    return _call(
        body, grid=(m // tm, n // tn),
        in_specs=[pl.BlockSpec((tm, k), lambda i, j: (i, 0)), pl.BlockSpec((tn, k), lambda i, j: (j, 0))],
        out_specs=pl.BlockSpec((tm, tn), lambda i, j: (i, j)),
        out_shape=jax.ShapeDtypeStruct((m, n), out_dtype), name=name, args=(a, w), carried=carried)


def _norm_bwd_mm_nt(x, g, dy, scale, w, name, carried=()):
    m, d = x.shape
    n = w.shape[0]
    tm = _tile(m, 528, SUBLANE_BF16)

    def body(x_ref, g_ref, dy_ref, w_ref, dx_ref, dg_ref, o_ref):
        @pl.when(pl.program_id(0) == 0)
        def _():
            dg_ref[...] = jnp.zeros_like(dg_ref)

        dx, dgr = _rmsnorm_bwd_rows(x_ref[...], g_ref[...], scale * dy_ref[...])
        dxb = dx.astype(dx_ref.dtype)
        dx_ref[...] = dxb
        dg_ref[...] += jnp.sum(dgr, axis=0, keepdims=True)
        o_ref[...] = _dot_nt(dxb, w_ref[...])

    row = pl.BlockSpec((tm, d), lambda i: (i, 0))
    vec = pl.BlockSpec((1, d), lambda i: (0, 0))
    return _call(
        body, grid=(m // tm,),
        in_specs=[row, vec, row, pl.BlockSpec((n, d), lambda i: (0, 0), pipeline_mode=pl.Buffered(1))],
        out_specs=[row, vec, pl.BlockSpec((tm, n), lambda i: (i, 0))],
        out_shape=[jax.ShapeDtypeStruct((m, d), MXU_DTYPE), jax.ShapeDtypeStruct((1, d), F32),
                   jax.ShapeDtypeStruct((m, n), F32)],
        name=name, args=(x, g, dy, w), carried=carried)


def _ffn_up_act(n_act, wu_t, g_act, name, carried=()):
    m, d = n_act.shape
    fp = wu_t.shape[0]
    tm = _tile(m, 1056, SUBLANE_BF16)
    tn = _tile(fp, 512, LANE)

    def body(n_ref, wu_ref, g_ref, u_ref, a_ref):
        u = _dot_nt(n_ref[...], wu_ref[...])
        g = g_ref[...].astype(F32)
        u_ref[...] = u.astype(u_ref.dtype)
        a_ref[...] = (g * jax.nn.sigmoid(g) * u).astype(a_ref.dtype)

    act = pl.BlockSpec((tm, tn), lambda i, j: (i, j))
    return _call(
        body, grid=(m // tm, fp // tn),
        in_specs=[pl.BlockSpec((tm, d), lambda i, j: (i, 0)), pl.BlockSpec((tn, d), lambda i, j: (j, 0)), act],
        out_specs=[act, act],
        out_shape=[jax.ShapeDtypeStruct((m, fp), MXU_DTYPE)] * 2, name=name, args=(n_act, wu_t, g_act), carried=carried)


def _ffn_gate_up(n_act, wg_t, wu_t, name, carried=()):
    m, d = n_act.shape
    fp = wg_t.shape[0]
    tm = _tile(m, 1056, SUBLANE_BF16)
    tn = _tile(fp, 512, LANE)

    def body(n_ref, wg_ref, wu_ref, g_ref, u_ref, a_ref):
        n = n_ref[...]
        g = _dot_nt(n, wg_ref[...])
        u = _dot_nt(n, wu_ref[...])
        g_ref[...] = g.astype(g_ref.dtype)
        u_ref[...] = u.astype(u_ref.dtype)
        a_ref[...] = (g * jax.nn.sigmoid(g) * u).astype(a_ref.dtype)

    act = pl.BlockSpec((tm, tn), lambda i, j: (i, j))
    wsp = pl.BlockSpec((tn, d), lambda i, j: (j, 0))
    return _call(
        body, grid=(m // tm, fp // tn),
        in_specs=[pl.BlockSpec((tm, d), lambda i, j: (i, 0)), wsp, wsp],
        out_specs=[act, act, act],
        out_shape=[jax.ShapeDtypeStruct((m, fp), MXU_DTYPE)] * 3, name=name, args=(n_act, wg_t, wu_t), carried=carried)


def _ffn_hidden_bwd(dfo, wd, g_act, u_act, name, carried=()):
    m, d = dfo.shape
    fp = wd.shape[0]
    tm = _tile(m, 1056, SUBLANE_BF16)
    tn = _tile(fp, 512, LANE)

    def body(df_ref, wd_ref, g_ref, u_ref, dg_ref, du_ref):
        da = _dot_nt(df_ref[...], wd_ref[...]).astype(dg_ref.dtype)
        g = g_ref[...]
        u = u_ref[...]
        s = jax.nn.sigmoid(g)
        du_ref[...] = da * (g * s)
        dg_ref[...] = da * (u * (s * (1.0 + g * (1.0 - s))))

    act = pl.BlockSpec((tm, tn), lambda i, j: (i, j))
    return _call(
        body, grid=(m // tm, fp // tn),
        in_specs=[pl.BlockSpec((tm, d), lambda i, j: (i, 0)), pl.BlockSpec((tn, d), lambda i, j: (j, 0)), act, act],
        out_specs=[act, act],
        out_shape=[jax.ShapeDtypeStruct((m, fp), MXU_DTYPE)] * 2, name=name, args=(dfo, wd, g_act, u_act),
        carried=carried)


def _row_groups(n_tiles, max_group, nk):
    gsz = max(q for q in range(1, max_group + 1) if n_tiles % q == 0)

    def epilogue_row(grp, kk, i):
        return grp * gsz + jnp.where(kk == nk - 1, i, 0)

    return gsz, epilogue_row


def _mm_residual_norm(a, w, h, g, scale, next_g, name, carried=()):
    m, k = a.shape
    d = w.shape[1]
    tm = _tile(m, ACC_ROWS, SUBLANE_BF16)
    tk = _tile(k, K_TILE, LANE)
    nk = k // tk
    gsz, epilogue_row = _row_groups(m // tm, ACC_GROUP, nk)

    def body(a_ref, w_ref, h_ref, g_ref, ng_ref, fo_ref, hn_ref, nn_ref, acc_ref):
        kk, i = pl.program_id(1), pl.program_id(2)

        @pl.when(kk == 0)
        def _():
            acc_ref[i] = jnp.zeros((tm, d), F32)

        acc_ref[i] += jnp.dot(a_ref[...], w_ref[...], preferred_element_type=F32)

        @pl.when(kk == nk - 1)
        def _():
            fo = acc_ref[i]
            fo_ref[...] = fo
            r = lax.rsqrt(jnp.mean(fo * fo, axis=-1, keepdims=True) + EPS)
            hn = h_ref[...] + scale * (fo * r * g_ref[...])
            hn_ref[...] = hn
            rn = lax.rsqrt(jnp.mean(hn * hn, axis=-1, keepdims=True) + EPS)
            nn_ref[...] = (hn * rn * ng_ref[...]).astype(nn_ref.dtype)

    row = pl.BlockSpec((tm, d), lambda grp, kk, i: (epilogue_row(grp, kk, i), 0))
    vec = pl.BlockSpec((1, d), lambda grp, kk, i: (0, 0))
    return _call(
        body, grid=(m // tm // gsz, nk, gsz),
        in_specs=[pl.BlockSpec((tm, tk), lambda grp, kk, i: (grp * gsz + i, kk)),
                  pl.BlockSpec((tk, d), lambda grp, kk, i: (kk, 0)), row, vec, vec],
        out_specs=[row, row, row],
        out_shape=[jax.ShapeDtypeStruct((m, d), F32)] * 2 + [jax.ShapeDtypeStruct((m, d), MXU_DTYPE)],
        scratch_shapes=[pltpu.VMEM((gsz, tm, d), F32)], name=name, args=(a, w, h, g, next_g), carried=carried)


def _mm_residual_loss(a, w, h, g, scale, target, lead, name, carried=()):
    m, k = a.shape
    d = w.shape[1]
    tm = _tile(m, ACC_ROWS, SUBLANE_BF16)
    tk = _tile(k, K_TILE, LANE)
    nk = k // tk
    gsz, epilogue_row = _row_groups(m // tm, ACC_GROUP, nk)

    def body(a_ref, w_ref, h_ref, g_ref, t_ref, dy_ref, dfo_ref, dg_ref, l_ref, acc_ref):
        grp, kk, i = pl.program_id(0), pl.program_id(1), pl.program_id(2)

        @pl.when(jnp.logical_and(jnp.logical_and(grp == 0, kk == 0), i == 0))
        def _():
            dg_ref[...] = jnp.zeros_like(dg_ref)
            l_ref[...] = jnp.zeros_like(l_ref)

        @pl.when(kk == 0)
        def _():
            acc_ref[i] = jnp.zeros((tm, d), F32)

        acc_ref[i] += jnp.dot(a_ref[...], w_ref[...], preferred_element_type=F32)

        @pl.when(kk == nk - 1)
        def _():
            fo = acc_ref[i]
            gain = g_ref[...]
            r = lax.rsqrt(jnp.mean(fo * fo, axis=-1, keepdims=True) + EPS)
            xh = fo * r
            y = h_ref[...] + scale * (xh * gain)
            row = (grp * gsz + i) * tm + lax.broadcasted_iota(jnp.int32, (tm, 1), 0)
            e = jnp.where(row >= lead, y - t_ref[...], 0.0)
            dy = e * (1.0 / d)
            dy_ref[...] = dy
            l_ref[...] += 0.5 * jnp.sum(jnp.sum(e * e, axis=-1, keepdims=True) * (1.0 / d), axis=0, keepdims=True)
            dn = scale * dy
            dyh = dn * gain
            dfo_ref[...] = (r * (dyh - xh * jnp.mean(dyh * xh, axis=-1, keepdims=True))).astype(dfo_ref.dtype)
            dg_ref[...] += jnp.sum(dn * xh, axis=0, keepdims=True)

    row = pl.BlockSpec((tm, d), lambda grp, kk, i: (epilogue_row(grp, kk, i), 0))
    vec = pl.BlockSpec((1, d), lambda grp, kk, i: (0, 0))
    return _call(
        body, grid=(m // tm // gsz, nk, gsz),
        in_specs=[pl.BlockSpec((tm, tk), lambda grp, kk, i: (grp * gsz + i, kk)),
                  pl.BlockSpec((tk, d), lambda grp, kk, i: (kk, 0)), row, vec, row],
        out_specs=[row, row, vec, pl.BlockSpec((1, 1), lambda grp, kk, i: (0, 0))],
        out_shape=[jax.ShapeDtypeStruct((m, d), F32), jax.ShapeDtypeStruct((m, d), MXU_DTYPE),
                   jax.ShapeDtypeStruct((1, d), F32), jax.ShapeDtypeStruct((1, 1), F32)],
        scratch_shapes=[pltpu.VMEM((gsz, tm, d), F32)], name=name, args=(a, w, h, g, target), carried=carried)


def _norm_bwd_row_tile(m):
    return _tile(m, ACC_ROWS, SUBLANE_BF16)


def _mm_norm_bwd(pairs, h, g, dh_up, name, carried=(), row_tiles=None, dg_init=None, post=None):
    n_pairs = len(pairs)
    m, k = pairs[0][0].shape
    d = h.shape[1]
    tm = _norm_bwd_row_tile(m)
    tk = _tile(k, K_TILE, LANE)
    nk = k // tk
    t0, nt = row_tiles if row_tiles is not None else (0, m // tm)
    gsz, epilogue_row = _row_groups(nt, ACC_GROUP, nk)
    if dg_init is None:
        dg_init = jnp.zeros((1, d), F32)

    n_post = 0 if post is None else 2

    def body(*refs):
        ops = refs[:2 * n_pairs]
        h_ref, g_ref, up_ref, init_ref = refs[2 * n_pairs:2 * n_pairs + 4]
        post_in = refs[2 * n_pairs + 4:2 * n_pairs + 4 + n_post]
        dh_ref, dg_ref = refs[2 * n_pairs + 4 + n_post:2 * n_pairs + 6 + n_post]
        post_out = refs[2 * n_pairs + 6 + n_post:2 * n_pairs + 6 + 2 * n_post]
        acc_ref = refs[-1]
        grp, kk, i = pl.program_id(0), pl.program_id(1), pl.program_id(2)

        @pl.when(jnp.logical_and(jnp.logical_and(grp == 0, kk == 0), i == 0))
        def _():
            dg_ref[...] = init_ref[...]
            if post is not None:
                post_out[1][...] = jnp.zeros_like(post_out[1])

        @pl.when(kk == 0)
        def _():
            acc_ref[i] = jnp.zeros((tm, d), F32)

        for p in range(n_pairs):
            acc_ref[i] += jnp.dot(ops[2 * p][...], ops[2 * p + 1][...], preferred_element_type=F32)

        @pl.when(kk == nk - 1)
        def _():
            dx, dgr = _rmsnorm_bwd_rows(h_ref[...], g_ref[...], acc_ref[i])
            dh = up_ref[...] + dx
            dh_ref[...] = dh
            dg_ref[...] += jnp.sum(dgr, axis=0, keepdims=True)
            if post is not None:
                dfo, dpr = _rmsnorm_bwd_rows(post_in[0][...], post_in[1][...], post[2] * dh)
                post_out[0][...] = dfo.astype(post_out[0].dtype)
                post_out[1][...] += jnp.sum(dpr, axis=0, keepdims=True)

    row_in = pl.BlockSpec((tm, d), lambda grp, kk, i: (t0 + epilogue_row(grp, kk, i), 0))
    row_out = pl.BlockSpec((tm, d), lambda grp, kk, i: (epilogue_row(grp, kk, i), 0))
    vec = pl.BlockSpec((1, d), lambda grp, kk, i: (0, 0))
    in_specs = []
    args = []
    for a, w in pairs:
        in_specs += [pl.BlockSpec((tm, tk), lambda grp, kk, i: (t0 + grp * gsz + i, kk)),
                     pl.BlockSpec((tk, d), lambda grp, kk, i: (kk, 0))]
        args += [a, w]
    in_specs += [row_in, vec, row_in, vec]
    args += [h, g, dh_up, dg_init]
    out_specs = [row_out, vec]
    out_shape = [jax.ShapeDtypeStruct((nt * tm, d), F32), jax.ShapeDtypeStruct((1, d), F32)]
    if post is not None:
        in_specs += [row_in, vec]
        args += [post[0], post[1]]
        out_specs += [row_out, vec]
        out_shape += [jax.ShapeDtypeStruct((nt * tm, d), MXU_DTYPE), jax.ShapeDtypeStruct((1, d), F32)]
    return _call(
        body, grid=(nt // gsz, nk, gsz), in_specs=in_specs, out_specs=out_specs, out_shape=out_shape,
        scratch_shapes=[pltpu.VMEM((gsz, tm, d), F32)], name=name, args=tuple(args), carried=carried)


def _mm_tn(a, b, name, carried=()):
    m, ka = a.shape
    d = b.shape[1]
    tf = _tile(ka, 512, LANE)

    def body(a_ref, b_ref, o_ref):
        o_ref[...] = _dot_tn(a_ref[...], b_ref[...]).astype(o_ref.dtype)

    return _call(
        body, grid=(ka // tf,),
        in_specs=[pl.BlockSpec((m, tf), lambda j: (0, j)),
                  pl.BlockSpec((m, d), lambda j: (0, 0), pipeline_mode=pl.Buffered(1))],
        out_specs=pl.BlockSpec((tf, d), lambda j: (j, 0)),
        out_shape=jax.ShapeDtypeStruct((ka, d), WIRE_DTYPE), name=name, args=(a, b), carried=carried)


GELU_K = 0.7978845608028654
GELU_C = 0.044715


def _expm1(x):
    series = x * (1.0 + x * (1.0 / 2 + x * (1.0 / 6 + x * (1.0 / 24 + x * (1.0 / 120)))))
    return jnp.where(jnp.abs(x) < 0.1, series, jnp.exp(x) - 1.0)


def _softplus(x):
    return jnp.maximum(x, 0.0) + jnp.log1p(jnp.exp(-jnp.abs(x)))


def _block_mm(v, w_ref, transposed):
    nbk = w_ref.shape[0]
    outs = []
    for j in range(nbk):
        vj = v[:, j * BD:(j + 1) * BD]
        outs.append(_dot_nt(vj, w_ref[j]) if transposed else jnp.dot(vj, w_ref[j], preferred_element_type=F32))
    return outs[0] if nbk == 1 else jnp.concatenate(outs, axis=1)


def _group_mean(q, gm_ref):
    hi = q.astype(MXU_DTYPE)
    lo = (q - hi.astype(F32)).astype(MXU_DTYPE)
    nbk = q.shape[1] // BD
    gm = gm_ref[...]
    outs = []
    for j in range(nbk):
        sl = slice(j * BD, (j + 1) * BD)
        outs.append(jnp.dot(hi[:, sl], gm, preferred_element_type=F32) + jnp.dot(lo[:, sl], gm, preferred_element_type=F32))
    return outs[0] if nbk == 1 else jnp.concatenate(outs, axis=1)


class _RowReader:
    def __init__(self, ref):
        self.ref = ref

    def __getitem__(self, rows):
        return self.ref[rows, :]


def _shifted(ext_ref, cur, before8, after8, downs=(), ups=()):
    r = cur.shape[0]
    if downs:
        ext_ref[0:8, :] = before8
    ext_ref[8:8 + r, :] = cur
    if ups:
        ext_ref[8 + r:16 + r, :] = after8
    return [ext_ref[pl.ds(8 - j, r), :] for j in downs] + [ext_ref[pl.ds(8 + j, r), :] for j in ups]


def _lru_gates(xc, pv, wa_ref, wx_ref):
    xcb = xc.astype(MXU_DTYPE)
    ga = jax.nn.sigmoid(_block_mm(xcb, wa_ref, False) + pv[5:6])
    gx = jax.nn.sigmoid(_block_mm(xcb, wx_ref, False) + pv[6:7])
    sp = _softplus(-pv[7:8])
    log_a = -LRU_C * ga * sp
    a = jnp.exp(log_a)
    e2 = _expm1(2.0 * log_a)
    mult = jnp.sqrt(-e2)
    return xcb, ga, gx, sp, a, e2, mult


def _gelu_parts(y):
    th = jnp.tanh(GELU_K * (y + GELU_C * y * y * y))
    return 0.5 * y * (1.0 + th), th


def _scan_block(a, u, sa_ref, su_ref, carry_ref, out_ref, reverse):
    r, c = a.shape
    n = r // 8
    a3 = a.reshape(n, 8, c)
    u3 = u.reshape(n, 8, c)
    sub = lax.broadcasted_iota(jnp.int32, (n, 8, c), 1)
    for dlt in (1, 2, 4):
        keep = (sub < 8 - dlt) if reverse else (sub >= dlt)
        shift = 8 - dlt if reverse else dlt
        sh_a = pltpu.roll(a3, shift, axis=1)
        sh_u = pltpu.roll(u3, shift, axis=1)
        u3 = u3 + a3 * jnp.where(keep, sh_u, 0.0)
        a3 = a3 * jnp.where(keep, sh_a, 1.0)
    sa_ref[...] = a3.reshape(r, c)
    su_ref[...] = u3.reshape(r, c)
    for k in (range(n - 1, -1, -1) if reverse else range(n)):
        rows = pl.ds(8 * k, 8)
        out_ref[rows, :] = su_ref[rows, :] + sa_ref[rows, :] * carry_ref[...]
        carry_ref[...] = out_ref[pl.ds(8 * k if reverse else 8 * k + 7, 1), :]


def _mixer_fwd(z, pv, wa, wx, gm, pad, name, carried=()):
    m = z.shape[0]
    c = pv.shape[1]
    r = MIX_ROWS
    nb = m // r

    def body(z_ref, pv_ref, wa_ref, wx_ref, gm_ref, mixed_ref, hs_ref, ext_ref, tailx_ref, tailc_ref, carry_ref,
             sa_ref, su_ref):
        b = pl.program_id(0)

        @pl.when(b == 0)
        def _():
            tailx_ref[...] = jnp.zeros_like(tailx_ref)
            tailc_ref[...] = jnp.zeros_like(tailc_ref)
            carry_ref[...] = jnp.zeros_like(carry_ref)

        pv = _RowReader(pv_ref)
        row = b * r + lax.broadcasted_iota(jnp.int32, (r, 1), 0)
        maskf = (row >= pad).astype(F32)
        y = z_ref[:, 0:c]
        xl = z_ref[:, c:2 * c]
        bs = z_ref[:, 2 * c:3 * c]
        cv = z_ref[:, 3 * c:4 * c] * z_ref[:, 4 * c:5 * c]

        x1, x2, x3 = _shifted(ext_ref, xl, tailx_ref[...], None, downs=(1, 2, 3))
        tailx_ref[...] = z_ref[pl.ds(r - 8, 8), c:2 * c]
        xc = pv[4:5] + pv[3:4] * xl + pv[2:3] * x1 + pv[1:2] * x2 + pv[0:1] * x3
        _, _, gx, _, a, _, mult = _lru_gates(xc, pv, wa_ref, wx_ref)
        uu = mult * (gx * xc) * maskf

        _scan_block(a, uu, sa_ref, su_ref, carry_ref, hs_ref, reverse=False)
        hs = hs_ref[...]

        gelu_y, _ = _gelu_parts(y)
        lru_out = hs * gelu_y
        c1, c2 = _shifted(ext_ref, cv, tailc_ref[...], None, downs=(1, 2))
        tailc_ref[...] = cv[r - 8:r]
        sc_out = bs * (pv[10:11] * cv + pv[9:10] * c1 + pv[8:9] * c2)

        rl = lax.rsqrt(_group_mean(lru_out * lru_out, gm_ref) + EPS)
        rs = lax.rsqrt(_group_mean(sc_out * sc_out, gm_ref) + EPS)
        mixed_ref[:, 0:c] = (lru_out * rl * pv[11:12]).astype(mixed_ref.dtype)
        mixed_ref[:, c:2 * c] = (sc_out * rs * pv[12:13]).astype(mixed_ref.dtype)

    full = lambda shape: pl.BlockSpec(shape, lambda b: (0,) * len(shape))
    return _call(
        body, grid=(nb,),
        in_specs=[pl.BlockSpec((r, 5 * c), lambda b: (b, 0)), full(pv.shape), full(wa.shape), full(wx.shape), full(gm.shape)],
        out_specs=[pl.BlockSpec((r, 2 * c), lambda b: (b, 0)), pl.BlockSpec((r, c), lambda b: (b, 0))],
        out_shape=[jax.ShapeDtypeStruct((m, 2 * c), MXU_DTYPE), jax.ShapeDtypeStruct((m, c), F32)],
        scratch_shapes=[pltpu.VMEM((r + 16, c), F32), pltpu.VMEM((8, c), F32), pltpu.VMEM((8, c), F32),
                        pltpu.VMEM((1, c), F32), pltpu.VMEM((r, c), F32), pltpu.VMEM((r, c), F32)],
        name=name, args=(z, pv, wa, wx, gm), carried=carried)


def _mixer_bwd(z, hs, dmixed, pv, wa, wx, gm, pad, name, carried=()):
    m = z.shape[0]
    c = pv.shape[1]
    r = MIX_ROWS
    nb = m // r
    r8 = r // 8
    assert pad <= r and pad % SUBLANE_BF16 == 0

    def body(z_ref, zp_ref, hs_ref, hsp_ref, dm_ref, pv_ref, wa_ref, wx_ref, gm_ref,
             dz_ref, dpv_ref, dwa_ref, dwx_ref, ext_ref, hxc_ref, hsc_ref, hp_ref, pc_ref, sa_ref, su_ref, p_ref):
        i = pl.program_id(0)
        b = nb - 1 - i

        @pl.when(i == 0)
        def _():
            hxc_ref[...] = jnp.zeros_like(hxc_ref)
            hsc_ref[...] = jnp.zeros_like(hsc_ref)
            hp_ref[...] = jnp.zeros_like(hp_ref)
            pc_ref[...] = jnp.zeros_like(pc_ref)
            dpv_ref[...] = jnp.zeros_like(dpv_ref)
            dwa_ref[...] = jnp.zeros_like(dwa_ref)
            dwx_ref[...] = jnp.zeros_like(dwx_ref)

        pv = _RowReader(pv_ref)
        row = b * r + lax.broadcasted_iota(jnp.int32, (r, 1), 0)
        maskf = (row >= pad).astype(F32)
        has_prev = (b > 0).astype(F32)
        y = z_ref[:, 0:c]
        xl = z_ref[:, c:2 * c]
        bs = z_ref[:, 2 * c:3 * c]
        cs = z_ref[:, 3 * c:4 * c]
        vs = z_ref[:, 4 * c:5 * c]
        cv = cs * vs
        xl_prev = zp_ref[:, c:2 * c] * has_prev
        cv_prev = zp_ref[:, 3 * c:4 * c] * zp_ref[:, 4 * c:5 * c] * has_prev
        hs = hs_ref[...]

        x1, x2, x3 = _shifted(ext_ref, xl, xl_prev, None, downs=(1, 2, 3))
        xc = pv[4:5] + pv[3:4] * xl + pv[2:3] * x1 + pv[1:2] * x2 + pv[0:1] * x3
        xcb, ga, gx, sp, a, e2, mult = _lru_gates(xc, pv, wa_ref, wx_ref)
        gxx = gx * xc
        gelu_y, th = _gelu_parts(y)
        lru_out = hs * gelu_y
        c1, c2 = _shifted(ext_ref, cv, cv_prev, None, downs=(1, 2))
        sc = pv[10:11] * cv + pv[9:10] * c1 + pv[8:9] * c2
        sc_out = bs * sc

        def group_norm_bwd(v, dm, gain):
            rr = lax.rsqrt(_group_mean(v * v, gm_ref) + EPS)
            vh = v * rr
            dvh = dm * gain
            dv = rr * (dvh - vh * _group_mean(dvh * vh, gm_ref))
            return dv, jnp.sum(dm * vh, axis=0, keepdims=True)

        d_lru_out, d_og = group_norm_bwd(lru_out, dm_ref[:, 0:c], pv[11:12])
        d_sc_out, d_sg = group_norm_bwd(sc_out, dm_ref[:, c:2 * c], pv[12:13])
        dpv_ref[11:12, :] += d_og
        dpv_ref[12:13, :] += d_sg

        dhs = d_lru_out * gelu_y
        dgelu = 0.5 * (1.0 + th) + 0.5 * y * (1.0 - th * th) * GELU_K * (1.0 + 3.0 * GELU_C * y * y)
        dy = d_lru_out * hs * dgelu

        _scan_block(a, a * dhs, sa_ref, su_ref, pc_ref, p_ref, reverse=True)
        (p_next,) = _shifted(ext_ref, p_ref[...], None, hp_ref[...], ups=(1,))
        hp_ref[...] = p_ref[0:8, :]
        q = dhs + p_next
        (hs_prev,) = _shifted(ext_ref, hs, hsp_ref[...] * has_prev, None, downs=(1,))
        duu = q * maskf
        da = q * hs_prev

        dmult = duu * gxx
        dgxx = duu * mult
        dgx = dgxx * xc
        dxc = dgxx * gx
        dlog_a = da * a - dmult * ((1.0 + e2) / mult)
        dga = dlog_a * (-LRU_C * sp)
        dsp = jnp.sum(dlog_a * (-LRU_C * ga), axis=0, keepdims=True)
        dpv_ref[7:8, :] += dsp * (-jax.nn.sigmoid(-pv[7:8]))
        dga_pre = dga * ga * (1.0 - ga)
        dgx_pre = dgx * gx * (1.0 - gx)
        dpv_ref[5:6, :] += jnp.sum(dga_pre, axis=0, keepdims=True)
        dpv_ref[6:7, :] += jnp.sum(dgx_pre, axis=0, keepdims=True)
        dga_b = dga_pre.astype(MXU_DTYPE)
        dgx_b = dgx_pre.astype(MXU_DTYPE)
        dxc = dxc + _block_mm(dga_b, wa_ref, True) + _block_mm(dgx_b, wx_ref, True)
        for j in range(c // BD):
            sl = slice(j * BD, (j + 1) * BD)
            dwa_ref[j] += _dot_tn(xcb[:, sl], dga_b[:, sl])
            dwx_ref[j] += _dot_tn(xcb[:, sl], dgx_b[:, sl])

        dpv_ref[4:5, :] += jnp.sum(dxc, axis=0, keepdims=True)
        dpv_ref[3:4, :] += jnp.sum(dxc * xl, axis=0, keepdims=True)
        dpv_ref[2:3, :] += jnp.sum(dxc * x1, axis=0, keepdims=True)
        dpv_ref[1:2, :] += jnp.sum(dxc * x2, axis=0, keepdims=True)
        dpv_ref[0:1, :] += jnp.sum(dxc * x3, axis=0, keepdims=True)
        u1, u2, u3 = _shifted(ext_ref, dxc, None, hxc_ref[...], ups=(1, 2, 3))
        hxc_ref[...] = dxc[0:8]
        dxl = pv[3:4] * dxc + pv[2:3] * u1 + pv[1:2] * u2 + pv[0:1] * u3

        dbs = d_sc_out * sc
        dsc = d_sc_out * bs
        dpv_ref[10:11, :] += jnp.sum(dsc * cv, axis=0, keepdims=True)
        dpv_ref[9:10, :] += jnp.sum(dsc * c1, axis=0, keepdims=True)
        dpv_ref[8:9, :] += jnp.sum(dsc * c2, axis=0, keepdims=True)
        s1, s2 = _shifted(ext_ref, dsc, None, hsc_ref[...], ups=(1, 2))
        hsc_ref[...] = dsc[0:8]
        dcv = pv[10:11] * dsc + pv[9:10] * s1 + pv[8:9] * s2

        dz_ref[:, 0:c] = dy.astype(dz_ref.dtype)
        dz_ref[:, c:2 * c] = dxl.astype(dz_ref.dtype)
        dz_ref[:, 2 * c:3 * c] = dbs.astype(dz_ref.dtype)
        dz_ref[:, 3 * c:4 * c] = (dcv * vs).astype(dz_ref.dtype)
        dz_ref[:, 4 * c:5 * c] = (dcv * cs).astype(dz_ref.dtype)

        if pad:
            @pl.when(b == 0)
            def _():
                dz_ref[0:pad, :] = jnp.zeros((pad, 5 * c), dz_ref.dtype)

    full = lambda shape: pl.BlockSpec(shape, lambda i: (0,) * len(shape))
    cur = lambda width: pl.BlockSpec((r, width), lambda i: (nb - 1 - i, 0))
    prev8 = lambda width: pl.BlockSpec((8, width), lambda i: (jnp.maximum((nb - 1 - i) * r8 - 1, 0), 0))
    return _call(
        body, grid=(nb,),
        in_specs=[cur(5 * c), prev8(5 * c), cur(c), prev8(c), cur(2 * c),
                  full(pv.shape), full(wa.shape), full(wx.shape), full(gm.shape)],
        out_specs=[cur(5 * c), full(pv.shape), full(wa.shape), full(wx.shape)],
        out_shape=[jax.ShapeDtypeStruct((m, 5 * c), MXU_DTYPE), jax.ShapeDtypeStruct(pv.shape, F32),
                   jax.ShapeDtypeStruct(wa.shape, F32), jax.ShapeDtypeStruct(wx.shape, F32)],
        scratch_shapes=[pltpu.VMEM((r + 16, c), F32), pltpu.VMEM((8, c), F32), pltpu.VMEM((8, c), F32),
                        pltpu.VMEM((8, c), F32), pltpu.VMEM((1, c), F32), pltpu.VMEM((r, c), F32),
                        pltpu.VMEM((r, c), F32), pltpu.VMEM((r, c), F32)],
        name=name, args=(z, z, hs, hs, dmixed, pv, wa, wx, gm), carried=carried)


def _position():
    return lax.axis_index("x"), lax.axis_index("y"), lax.axis_index("c")


def _block_of(px, py, pc):
    return 4 * px + 2 * py + pc


class _TwoLevelGather:
    def __init__(self, n_arrays, rows_of, src_of, send_sems, recv_sems):
        x, y, c = _position()
        self.n, self.rows_of, self.src_of = n_arrays, rows_of, src_of
        self.send_sems, self.recv_sems = send_sems, recv_sems
        self.c, self.me, self.sibling = c, (x, y, c), (x, y, 1 - c)
        self.chips = [(1 - x, y), (x, 1 - y), (1 - x, 1 - y)]

    def _copy(self, i, k, block, to, src=None):
        return pltpu.make_async_remote_copy(
            src_ref=self.rows_of(i, *block) if src is None else src, dst_ref=self.rows_of(i, *block),
            send_sem=self.send_sems.at[7 * i + k], recv_sem=self.recv_sems.at[7 * i + k],
            device_id=to, device_id_type=MESH)

    def _first(self, i):
        own = [self._copy(i, 0, self.me, self.sibling, src=self.src_of(i))]
        return own + [self._copy(i, 1 + j, self.me, (*chip, self.c), src=self.src_of(i))
                      for j, chip in enumerate(self.chips)]

    def _passed(self, i, j):
        return self._copy(i, 4 + j, (*self.chips[j], self.c), self.sibling)

    def start(self):
        for i in range(self.n):
            for cp in self._first(i):
                cp.start()

    def forward(self):
        for i in range(self.n):
            for j, chip in enumerate(self.chips):
                self._copy(i, 1 + j, (*chip, self.c), self.me).wait_recv()
                self._passed(i, j).start()

    def drain(self):
        for i in range(self.n):
            self._copy(i, 0, self.sibling, self.me).wait_recv()
            for j, chip in enumerate(self.chips):
                self._copy(i, 4 + j, (*chip, 1 - self.c), self.me).wait_recv()
        for i in range(self.n):
            for cp in self._first(i) + [self._passed(i, j) for j in range(3)]:
                cp.wait_send()


class _RelayGather:
    def __init__(self, n_arrays, rows_of, src_of, send_sems, recv_sems):
        x, y, c = _position()
        self.n, self.rows_of, self.src_of = n_arrays, rows_of, src_of
        self.send_sems, self.recv_sems = send_sems, recv_sems
        self.me, self.sibling = (x, y, c), (x, y, 1 - c)
        self.xn, self.yn, self.dg = (1 - x, y, c), (x, 1 - y, c), (1 - x, 1 - y, c)

    def _copy(self, i, k, block, to, half=None, src=None):
        rows = self.rows_of(i, *block, half)
        return pltpu.make_async_remote_copy(
            src_ref=rows if src is None else src, dst_ref=rows,
            send_sem=self.send_sems.at[8 * i + k], recv_sem=self.recv_sems.at[8 * i + k],
            device_id=to, device_id_type=MESH)

    def _sends(self, i):
        own = self.src_of(i)
        return [self._copy(i, 0, self.me, self.sibling, src=own), self._copy(i, 1, self.me, self.xn, src=own),
                self._copy(i, 2, self.me, self.yn, src=own),
                self._copy(i, 3, self.xn, self.yn, half=0), self._copy(i, 4, self.yn, self.xn, half=1),
                self._copy(i, 5, self.xn, self.sibling), self._copy(i, 6, self.yn, self.sibling),
                self._copy(i, 7, self.dg, self.sibling)]

    def start(self):
        for i in range(self.n):
            for cp in self._sends(i)[0:3]:
                cp.start()

    def forward(self):
        for i in range(self.n):
            self._copy(i, 1, self.xn, self.me).wait_recv()
            self._copy(i, 2, self.yn, self.me).wait_recv()
            for cp in self._sends(i)[3:7]:
                cp.start()

    def drain(self):
        x, y, c = self.me
        for i in range(self.n):
            self._copy(i, 3, self.dg, self.me, half=0).wait_recv()
            self._copy(i, 4, self.dg, self.me, half=1).wait_recv()
            self._sends(i)[7].start()
        for i in range(self.n):
            self._copy(i, 0, self.sibling, self.me).wait_recv()
            self._copy(i, 5, (1 - x, y, 1 - c), self.me).wait_recv()
            self._copy(i, 6, (x, 1 - y, 1 - c), self.me).wait_recv()
            self._copy(i, 7, (1 - x, 1 - y, 1 - c), self.me).wait_recv()
        for i in range(self.n):
            for cp in self._sends(i):
                cp.wait_send()


class _CarriedGather:
    def __init__(self, shards, padded_rows, zeros, forward_at, part=None, into=None):
        d = shards[0].shape[1]
        self.forward_at = forward_at
        self.n = len(shards)
        self.rows = [s.shape[0] for s in shards]
        self.pads = [p - N_DEV * r for r, p in zip(self.rows, padded_rows)]
        assert max(self.pads) <= zeros.shape[0] and zeros.shape[1] == d
        self.part = part if part is not None else (0, self.rows[0])
        assert (part is None and into is None) or self.n == 1
        assert self.part[0] % SUBLANE_BF16 == 0 and self.part[1] % SUBLANE_BF16 == 0
        self.arrays = list(shards) + [zeros] + ([into] if into is not None else [])
        self.out_shapes = [jax.ShapeDtypeStruct((p, d), s.dtype) for s, p in zip(shards, padded_rows)]
        self.aliases = {self.n + 1: 0} if into is not None else {}
        if into is not None:
            self.pads = [0] * self.n
        self.n_remote, self.n_local = 8 * self.n, 2 * self.n
        self.results = None

    def _rows_of(self, outs):
        def rows_of(i, px, py, pc, half):
            first, count = (self.part if self.n == 1 else (0, self.rows[i]))
            head = _round_up(count // 2, SUBLANE_BF16)
            if half == 0:
                count = head
            elif half == 1:
                first, count = first + head, count - head
            first = _block_of(px, py, pc) * self.rows[i] + first
            return outs[i].at[pl.ds(pl.multiple_of(first, SUBLANE_BF16), count), :]
        return rows_of

    def _own(self, ins, i):
        return ins[i].at[pl.ds(self.part[0], self.part[1]), :] if self.n == 1 else ins[i]

    def _gather(self, ins, outs, send_sems, recv_sems):
        return _RelayGather(self.n, self._rows_of(outs), functools.partial(self._own, ins), send_sems, recv_sems)

    def _local(self, ins, outs, local_sems):
        x, y, c = _position()
        rows_of = self._rows_of(outs)
        cps = []
        for i in range(self.n):
            cps.append(pltpu.make_async_copy(self._own(ins, i), rows_of(i, x, y, c, None), local_sems.at[2 * i]))
            if self.pads[i]:
                cps.append(pltpu.make_async_copy(ins[self.n].at[pl.ds(0, self.pads[i]), :],
                                                 outs[i].at[pl.ds(N_DEV * self.rows[i], self.pads[i]), :],
                                                 local_sems.at[2 * i + 1]))
        return cps

    def start(self, ins, outs, send_sems, recv_sems, local_sems):
        for cp in self._local(ins, outs, local_sems):
            cp.start()
        self._gather(ins, outs, send_sems, recv_sems).start()

    def forward(self, ins, outs, send_sems, recv_sems, local_sems):
        self._gather(ins, outs, send_sems, recv_sems).forward()

    def finish(self, ins, outs, send_sems, recv_sems, local_sems):
        self._gather(ins, outs, send_sems, recv_sems).drain()
        for cp in self._local(ins, outs, local_sems):
            cp.wait()


class _CarriedSwap:
    def __init__(self, grads, shard_rows):
        d = grads[0].shape[1]
        self.n, self.rows = len(grads), list(shard_rows)
        self.arrays = list(grads)
        self.out_shapes = [jax.ShapeDtypeStruct((4, s, d), g.dtype) for g, s in zip(grads, shard_rows)]
        self.aliases = {}
        self.n_remote, self.n_local = 4 * self.n, 0
        self.forward_at = 1.0
        self.results = None

    def _copies(self, ins, outs, send_sems, recv_sems):
        x, y, c = _position()
        cps = []
        for i in range(self.n):
            s = self.rows[i]
            for k in range(4):
                blk = _block_of(k >> 1, k & 1, 1 - c)
                cps.append(pltpu.make_async_remote_copy(
                    src_ref=ins[i].at[pl.ds(pl.multiple_of(blk * s, SUBLANE_BF16), s), :], dst_ref=outs[i].at[k],
                    send_sem=send_sems.at[4 * i + k], recv_sem=recv_sems.at[4 * i + k],
                    device_id=(x, y, 1 - c), device_id_type=MESH))
        return cps

    def start(self, ins, outs, send_sems, recv_sems, local_sems):
        for cp in self._copies(ins, outs, send_sems, recv_sems):
            cp.start()

    def forward(self, *_):
        pass

    def finish(self, ins, outs, send_sems, recv_sems, local_sems):
        for cp in self._copies(ins, outs, send_sems, recv_sems):
            cp.wait()


class _CarriedChipExchange:
    def __init__(self, presums, part=None, into=None):
        self.n = len(presums)
        assert (part is None and into is None) or self.n == 1
        self.part = part if part is not None else (0, presums[0].shape[1])
        assert self.part[0] % SUBLANE_BF16 == 0 and self.part[1] % SUBLANE_BF16 == 0
        self.arrays = list(presums) + ([into] if into is not None else [])
        self.out_shapes = [jax.ShapeDtypeStruct(p.shape, p.dtype) for p in presums]
        self.aliases = {self.n: 0} if into is not None else {}
        self.n_remote, self.n_local = 3 * self.n, 0
        self.forward_at = 1.0
        self.results = None

    def _copies(self, ins, outs, send_sems, recv_sems):
        x, y, c = _position()
        cps = []
        for i in range(self.n):
            rows = pl.ds(*self.part) if self.n == 1 else pl.ds(0, self.arrays[i].shape[1])
            for r in range(1, 4):
                cps.append(pltpu.make_async_remote_copy(
                    src_ref=ins[i].at[r - 1, rows, :], dst_ref=outs[i].at[r - 1, rows, :],
                    send_sem=send_sems.at[3 * i + r - 1], recv_sem=recv_sems.at[3 * i + r - 1],
                    device_id=(x ^ (r >> 1), y ^ (r & 1), c), device_id_type=MESH))
        return cps

    def start(self, ins, outs, send_sems, recv_sems, local_sems):
        for cp in self._copies(ins, outs, send_sems, recv_sems):
            cp.start()

    def forward(self, *_):
        pass

    def finish(self, ins, outs, send_sems, recv_sems, local_sems):
        for cp in self._copies(ins, outs, send_sems, recv_sems):
            cp.wait()


def _gather_small(block, reduce, name):
    rr, nn = block.shape

    def body(x_ref, out_ref, *rest):
        if reduce:
            stack_ref, send_sems, recv_sems, local_sem = rest
        else:
            send_sems, recv_sems, local_sem = rest
            stack_ref = out_ref
        x, y, c = _position()

        def rows_of(i, px, py, pc):
            return stack_ref.at[pl.ds(pl.multiple_of(_block_of(px, py, pc) * rr, 8), rr), :]

        own = pltpu.make_async_copy(x_ref, rows_of(0, x, y, c), local_sem)
        own.start()
        gather = _TwoLevelGather(1, rows_of, lambda i: x_ref, send_sems, recv_sems)
        gather.start()
        gather.forward()
        gather.drain()
        own.wait()
        if reduce:
            acc = stack_ref[0:rr, :]
            for k in range(1, N_DEV):
                acc = acc + stack_ref[k * rr:(k + 1) * rr, :]
            out_ref[...] = acc

    vmem = pl.BlockSpec(memory_space=pltpu.VMEM)
    scratch = [pltpu.SemaphoreType.DMA((7,)), pltpu.SemaphoreType.DMA((7,)), pltpu.SemaphoreType.DMA]
    if reduce:
        scratch = [pltpu.VMEM((N_DEV * rr, nn), F32)] + scratch
    out_rows = rr if reduce else N_DEV * rr
    return pl.pallas_call(
        body, in_specs=[vmem], out_specs=vmem, out_shape=jax.ShapeDtypeStruct((out_rows, nn), F32),
        scratch_shapes=scratch, name=name, compiler_params=_params())(block)


def _sum_stack(stack, name):
    rr = stack.shape[0] // N_DEV

    def body(s_ref, o_ref):
        acc = s_ref[0:rr, :]
        for k in range(1, N_DEV):
            acc = acc + s_ref[k * rr:(k + 1) * rr, :]
        o_ref[...] = acc

    vmem = pl.BlockSpec(memory_space=pltpu.VMEM)
    return pl.pallas_call(body, in_specs=[vmem], out_specs=vmem,
                          out_shape=jax.ShapeDtypeStruct((rr, stack.shape[1]), F32), name=name,
                          compiler_params=_params())(stack)


def _presum(where, grad, swapped, name):
    s, d = swapped.shape[1], swapped.shape[2]
    tc = _tile(d, 2048, LANE)

    def body(where_ref, g_ref, sw_ref, o_ref):
        o_ref[0] = (g_ref[...].astype(F32) + sw_ref[0].astype(F32)).astype(o_ref.dtype)

    return _call(
        body, grid=(3, d // tc),
        in_specs=[pl.BlockSpec((s, tc), lambda r, j, where: (where[1 + r], j)),
                  pl.BlockSpec((1, s, tc), lambda r, j, where: (where[5 + r], 0, j))],
        out_specs=pl.BlockSpec((1, s, tc), lambda r, j, where: (r, 0, j)),
        out_shape=jax.ShapeDtypeStruct((3, s, d), WIRE_DTYPE), name=name, args=(grad, swapped), prefetch=(where,))


def _final_sum(where, grad, swapped, received, name, carried=()):
    s, d = swapped.shape[1], swapped.shape[2]
    tc = _tile(d, 512, LANE)

    def body(where_ref, g_ref, sw_ref, r_ref, o_ref):
        acc = g_ref[...].astype(F32) + sw_ref[0].astype(F32)
        for k in range(3):
            acc = acc + r_ref[k].astype(F32)
        o_ref[...] = acc

    return _call(
        body, grid=(d // tc,),
        in_specs=[pl.BlockSpec((s, tc), lambda j, where: (where[0], j)),
                  pl.BlockSpec((1, s, tc), lambda j, where: (where[4], 0, j)),
                  pl.BlockSpec((3, s, tc), lambda j, where: (0, 0, j))],
        out_specs=pl.BlockSpec((s, tc), lambda j, where: (0, j)),
        out_shape=jax.ShapeDtypeStruct((s, d), F32), name=name, args=(grad, swapped, received),
        prefetch=(where,), carried=carried)


class _GradReduction:
    def __init__(self, key, grad, shard_rows, where):
        self.key, self.grad, self.rows, self.where = key, grad, shard_rows, where
        self._presum = self._exchange = None

    def swap(self):
        self._swap = _CarriedSwap([self.grad], [self.rows])
        return self._swap

    def exchange(self, part=None):
        if self._presum is None:
            self._presum = _presum(self.where, self.grad, self._swap.results[0], "presum_" + self.key)
        rows = None
        if part is not None:
            half = _round_up(self.rows // 2, SUBLANE_BF16)
            rows = (0, half) if part == 0 else (half, self.rows - half)
        into = self._exchange.results[0] if part == 1 else None
        self._exchange = _CarriedChipExchange([self._presum], rows, into)
        return self._exchange

    def total(self, carried=()):
        return _final_sum(self.where, self.grad, self._swap.results[0], self._exchange.results[0],
                          "sum_" + self.key, carried)

    def total_and_update(self, w, m, v):
        return _sum_adamw(self.where, self.grad, self._swap.results[0], self._exchange.results[0], w, m, v,
                          "update_" + self.key)


def _adamw_math(w, g, m, v):
    nm = ADAM_B1 * m + (1.0 - ADAM_B1) * g
    nv = ADAM_B2 * v + (1.0 - ADAM_B2) * (g * g)
    m_hat = nm / (1.0 - ADAM_B1 ** ADAM_STEP)
    v_hat = nv / (1.0 - ADAM_B2 ** ADAM_STEP)
    return -ADAM_LR * (m_hat / (jnp.sqrt(v_hat) + ADAM_EPS) + ADAM_WD * w), nm, nv


def _sum_adamw(where, grad, swapped, received, w, m, v, name):
    s, d = swapped.shape[1], swapped.shape[2]
    tc = _tile(d, 512, LANE)

    def body(where_ref, g_ref, sw_ref, r_ref, w_ref, m_ref, v_ref, gs_ref, d_ref, nm_ref, nv_ref):
        g = g_ref[...].astype(F32) + sw_ref[0].astype(F32)
        for k in range(3):
            g = g + r_ref[k].astype(F32)
        gs_ref[...] = g
        d_ref[...], nm_ref[...], nv_ref[...] = _adamw_math(w_ref[...], g, m_ref[...], v_ref[...])

    blk = pl.BlockSpec((s, tc), lambda j, where: (0, j))
    return _call(
        body, grid=(d // tc,),
        in_specs=[pl.BlockSpec((s, tc), lambda j, where: (where[0], j)),
                  pl.BlockSpec((1, s, tc), lambda j, where: (where[4], 0, j)),
                  pl.BlockSpec((3, s, tc), lambda j, where: (0, 0, j)), blk, blk, blk],
        out_specs=[blk] * 4, out_shape=[jax.ShapeDtypeStruct((s, d), F32)] * 4, name=name,
        args=(grad, swapped, received, w, m, v), prefetch=(where,))


def _adamw(w, g, m, v, name):
    rows, cols = w.shape
    tr = _tile(rows, 256, 8)

    def body(w_ref, g_ref, m_ref, v_ref, d_ref, nm_ref, nv_ref):
        d_ref[...], nm_ref[...], nv_ref[...] = _adamw_math(w_ref[...], g_ref[...], m_ref[...], v_ref[...])

    spec = pl.BlockSpec((tr, cols), lambda i: (i, 0))
    return pl.pallas_call(
        body, grid=(rows // tr,), in_specs=[spec] * 4, out_specs=[spec] * 3,
        out_shape=[jax.ShapeDtypeStruct((rows, cols), F32)] * 3, name=name, compiler_params=_params())(w, g, m, v)


def _pack_rows(arrays, width, row_quantum=8):
    flat = jnp.concatenate([a.reshape(-1) for a in arrays])
    total = _round_up(flat.shape[0], row_quantum * width)
    flat = jnp.pad(flat, (0, total - flat.shape[0]))
    return flat.reshape(-1, width)


def _unpack_rows(packed, shapes):
    flat = packed.reshape(-1)
    out = []
    off = 0
    for shp in shapes:
        size = 1
        for s in shp:
            size *= s
        out.append(flat[off:off + size].reshape(shp))
        off += size
    return out


def _block_diag(w):
    h, hb, _ = w.shape
    per = BD // hb
    w4 = w.reshape(h // per, per, hb, hb)
    eye = jnp.eye(per, dtype=w.dtype)
    return jnp.einsum('npij,pq->npiqj', w4, eye).reshape(h // per, BD, BD)


def _block_diag_extract(bd, hb):
    nbk = bd.shape[0]
    per = BD // hb
    b5 = bd.reshape(nbk, per, hb, per, hb)
    eye = jnp.eye(per, dtype=bd.dtype)
    return jnp.einsum('npiqj,pq->npij', b5, eye).reshape(nbk * per, hb, hb)


def kernel(x, meta_tokens, ffn1_pre_g, ffn1_w_gate, ffn1_w_up, ffn1_w_down, ffn1_post_g, mix_pre_g, w_in, lru_conv_w, lru_conv_b, lru_w_a, lru_b_a, lru_w_x, lru_b_x, lru_lambda, sconv_w, lru_out_g, sconv_out_g, w_out, mix_post_g, ffn2_pre_g, ffn2_w_gate, ffn2_w_up, ffn2_w_down, ffn2_post_g, loss_target, m_meta_tokens, m_ffn1_pre_g, m_ffn1_w_gate, m_ffn1_w_up, m_ffn1_w_down, m_ffn1_post_g, m_mix_pre_g, m_w_in, m_lru_conv_w, m_lru_conv_b, m_lru_w_a, m_lru_b_a, m_lru_w_x, m_lru_b_x, m_lru_lambda, m_sconv_w, m_lru_out_g, m_sconv_out_g, m_w_out, m_mix_post_g, m_ffn2_pre_g, m_ffn2_w_gate, m_ffn2_w_up, m_ffn2_w_down, m_ffn2_post_g, v_meta_tokens, v_ffn1_pre_g, v_ffn1_w_gate, v_ffn1_w_up, v_ffn1_w_down, v_ffn1_post_g, v_mix_pre_g, v_w_in, v_lru_conv_w, v_lru_conv_b, v_lru_w_a, v_lru_b_a, v_lru_w_x, v_lru_b_x, v_lru_lambda, v_sconv_w, v_lru_out_g, v_sconv_out_g, v_w_out, v_mix_post_g, v_ffn2_pre_g, v_ffn2_w_gate, v_ffn2_w_up, v_ffn2_w_down, v_ffn2_post_g):
    given = dict(locals())
    wts = {n: given[n] for n in WEIGHT_NAMES}
    mom = {n: given["m_" + n] for n in WEIGHT_NAMES}
    var = {n: given["v_" + n] for n in WEIGHT_NAMES}

    xi, yi, ci = _position()
    me = _block_of(xi, yi, ci)
    x2 = x[0]
    seq, d = x2.shape
    n_meta = meta_tokens.shape[0]
    m_rows = _round_up(n_meta + seq, ROW_ALIGN)
    pad = m_rows - n_meta - seq
    lead = pad + n_meta
    c = lru_conv_b.shape[1]
    hb = lru_w_a.shape[-1]
    dm = meta_tokens.shape[1]
    cs_ = lru_conv_w.shape[2]
    kw4, kw3 = lru_conv_w.shape[1], sconv_w.shape[1]
    assert d == 2 * c and c % BD == 0 and BD % hb == 0 and cs_ <= dm and kw4 == 4 and kw3 == 3

    small = jnp.zeros((_round_up(n_meta + kw4 + kw3, 8), dm), F32)
    small = small.at[0:n_meta].set(meta_tokens)
    small = small.at[n_meta:n_meta + kw4, 0:cs_].set(lru_conv_w[0])
    small = small.at[n_meta + kw4:n_meta + kw4 + kw3, 0:cs_].set(sconv_w[0])
    sr = small.shape[0]
    small_all = _gather_small(small, False, "gather_small").reshape(N_DEV, sr, dm)
    meta_full = small_all[:, 0:n_meta, :].transpose(1, 0, 2).reshape(n_meta, d)
    conv_w_full = small_all[:, n_meta:n_meta + kw4, 0:cs_].transpose(1, 0, 2).reshape(kw4, c)
    sconv_w_full = small_all[:, n_meta + kw4:n_meta + kw4 + kw3, 0:cs_].transpose(1, 0, 2).reshape(kw3, c)

    big = ['ffn1_w_gate', 'ffn1_w_up', 'ffn1_w_down', 'w_in', 'w_out', 'ffn2_w_gate', 'ffn2_w_up', 'ffn2_w_down']
    col_sharded = {'ffn1_w_gate', 'ffn1_w_up', 'w_in', 'ffn2_w_gate', 'ffn2_w_up'}
    shards = []
    for nme in big:
        w = wts[nme][0].astype(WIRE_DTYPE)
        shards.append(w.T if nme in col_sharded else w)
    shard_rows = dict(zip(big, [s.shape[0] for s in shards]))
    zeros = jnp.zeros((F_ALIGN, d), WIRE_DTYPE)

    def gather(forward_at, *names, part=None, into=None):
        sel = [shards[big.index(nme)] for nme in names]
        padded = [_round_up(N_DEV * shard_rows[nme], LANE if nme in ('w_in', 'w_out') else F_ALIGN) for nme in names]
        return _CarriedGather(sel, padded, zeros, forward_at, part, into)

    pv = jnp.zeros((16, c), F32)
    pv = pv.at[0:4].set(conv_w_full).at[4].set(lru_conv_b[0]).at[5].set(lru_b_a[0]).at[6].set(lru_b_x[0])
    pv = pv.at[7].set(lru_lambda[0]).at[8:11].set(sconv_w_full).at[11].set(lru_out_g[0]).at[12].set(sconv_out_g[0])
    wa_bd = _block_diag(lru_w_a[0]).astype(MXU_DTYPE)
    wx_bd = _block_diag(lru_w_x[0]).astype(MXU_DTYPE)
    gs = c // N_GROUPS
    gidx = jnp.arange(BD) // gs
    gm = jnp.where(gidx[:, None] == gidx[None, :], 1.0 / gs, 0.0).astype(MXU_DTYPE)

    ride = gather(0.3, 'ffn1_w_gate')
    h0, n1, target = _embed(x2, meta_full, loss_target[0], ffn1_pre_g, pad, "embed_prenorm", carried=[ride])
    (wg1,) = ride.results
    ride = gather(0.6, 'ffn1_w_up')
    g1 = _mm_nt(n1, wg1, "ffn1_gate", carried=[ride], out_dtype=MXU_DTYPE)
    (wu1,) = ride.results
    ride = gather(0.6, 'ffn1_w_down')
    u1, a1 = _ffn_up_act(n1, wu1, g1, "ffn1_up_act", carried=[ride])
    (wd1,) = ride.results
    ride = gather(0.6, 'w_in')
    fo1, h1, un = _mm_residual_norm(a1, wd1, h0, ffn1_post_g, 0.5, mix_pre_g, "ffn1_down", carried=[ride])
    (win_t,) = ride.results
    s2 = shard_rows['ffn2_w_gate']
    quarter = _round_up(s2 // 4, SUBLANE_BF16)
    ride_g = gather(0.5, 'ffn2_w_gate', part=(0, 3 * quarter))
    ride_o = gather(0.5, 'w_out')
    z = _mm_nt(un, win_t, "mix_in_proj", carried=[ride_g, ride_o])
    (wout,) = ride_o.results
    ride_g = gather(0.5, 'ffn2_w_gate', part=(3 * quarter, s2 - 3 * quarter), into=ride_g.results[0])
    ride_u = gather(0.5, 'ffn2_w_up', part=(0, quarter))
    mixed, hs = _mixer_fwd(z, pv, wa_bd, wx_bd, gm, pad, "mixer_fwd", carried=[ride_g, ride_u])
    (wg2,) = ride_g.results
    ride_u = gather(0.5, 'ffn2_w_up', part=(quarter, s2 - quarter), into=ride_u.results[0])
    o_mix, h2, n2 = _mm_residual_norm(mixed, wout, h1, mix_post_g, 1.0, ffn2_pre_g, "mix_out_proj", carried=[ride_u])
    (wu2,) = ride_u.results
    ride = gather(0.75, 'ffn2_w_down')
    g2, u2, a2 = _ffn_gate_up(n2, wg2, wu2, "ffn2_gate_up", carried=[ride])
    (wd2,) = ride.results
    dh3, dfo2, d_post2, loss_part = _mm_residual_loss(a2, wd2, h2, ffn2_post_g, 0.5, target, lead, "ffn2_down_loss")

    chip_rel = [2 * (xi ^ (r >> 1)) + (yi ^ (r & 1)) for r in range(4)]
    where = jnp.stack([2 * k + ci for k in chip_rel] + chip_rel).astype(jnp.int32)
    red = {}

    def reduction(nme, grad):
        red[nme] = _GradReduction(nme, grad, shard_rows[nme], where)
        return red[nme]

    r_wd2 = reduction('ffn2_w_down', _mm_tn(a2, dfo2, "ffn2_dw_down"))
    dg2, du2 = _ffn_hidden_bwd(dfo2, wd2, g2, u2, "ffn2_hidden_bwd", carried=[r_wd2.swap()])
    r_wg2 = reduction('ffn2_w_gate', _mm_tn(dg2, n2, "ffn2_dw_gate", carried=[r_wd2.exchange(part=0)]))
    r_wu2 = reduction('ffn2_w_up', _mm_tn(du2, n2, "ffn2_dw_up", carried=[r_wd2.exchange(part=1), r_wg2.swap()]))
    dh2, d_pre2 = _mm_norm_bwd([(dg2, wg2), (du2, wu2)], h2, ffn2_pre_g, dh3, "ffn2_dx",
                               carried=[r_wg2.exchange(), r_wu2.swap()])
    do_mix, d_mix_post, dmixed = _norm_bwd_mm_nt(o_mix, mix_post_g, dh2, 1.0, wout, "mix_out_proj_bwd")
    r_wout = reduction('w_out', _mm_tn(mixed, do_mix, "mix_dw_out"))
    dz, dpv, dwa_bd, dwx_bd = _mixer_bwd(z, hs, dmixed, pv, wa_bd, wx_bd, gm, pad, "mixer_bwd",
                                         carried=[r_wu2.exchange(), r_wout.swap()])
    r_win = reduction('w_in', _mm_tn(dz, un, "mix_dw_in", carried=[r_wout.exchange()]))
    dh1, d_mix_pre, dfo1, d_post1 = _mm_norm_bwd([(dz, win_t)], h1, mix_pre_g, dh2, "mix_dx", carried=[r_win.swap()],
                                                 post=(fo1, ffn1_post_g, 0.5))
    r_wd1 = reduction('ffn1_w_down', _mm_tn(a1, dfo1, "ffn1_dw_down", carried=[r_win.exchange(part=0)]))
    early_names = ['mix_pre_g', 'mix_post_g', 'ffn2_pre_g', 'ffn2_post_g', 'ffn1_post_g',
                   'lru_conv_b', 'lru_b_a', 'lru_b_x', 'lru_lambda', 'lru_out_g', 'sconv_out_g',
                   'lru_conv_w', 'sconv_w', 'lru_w_a', 'lru_w_x']
    early_parts = [d_mix_pre, d_mix_post, d_pre2, d_post2, d_post1,
                   dpv[4:5], dpv[5:6], dpv[6:7], dpv[7:8], dpv[11:12], dpv[12:13],
                   dpv[0:4], dpv[8:11], _block_diag_extract(dwa_bd, hb), _block_diag_extract(dwx_bd, hb)]
    early_packed = _pack_rows(early_parts, d, SUBLANE_BF16)
    early_ride = _CarriedGather([early_packed], [N_DEV * early_packed.shape[0]], zeros, 0.75)
    dg1, du1 = _ffn_hidden_bwd(dfo1, wd1, g1, u1, "ffn1_hidden_bwd",
                               carried=[r_win.exchange(part=1), r_wd1.swap(), early_ride])
    early_sum = _sum_stack(early_ride.results[0], "sum_small_early")
    r_wg1 = reduction('ffn1_w_gate', _mm_tn(dg1, n1, "ffn1_dw_gate", carried=[r_wd1.exchange(part=0)]))
    r_wu1 = reduction('ffn1_w_up', _mm_tn(du1, n1, "ffn1_dw_up", carried=[r_wd1.exchange(part=1), r_wg1.swap()]))
    row_tile = _norm_bwd_row_tile(m_rows)
    n_tiles = m_rows // row_tile
    half = n_tiles // 2
    assert half >= 1 and half * row_tile >= lead
    dh0_a, d_pre1_a = _mm_norm_bwd([(dg1, wg1), (du1, wu1)], h0, ffn1_pre_g, dh1, "ffn1_dx_a",
                                   carried=[r_wg1.exchange(), r_wu1.swap()], row_tiles=(0, half))
    dh0_b, d_pre1 = _mm_norm_bwd([(dg1, wg1), (du1, wu1)], h0, ffn1_pre_g, dh1, "ffn1_dx_b",
                                 carried=[r_wu1.exchange()], row_tiles=(half, n_tiles - half), dg_init=d_pre1_a)
    grad_x = jnp.concatenate([dh0_a[lead:], dh0_b], axis=0)[None]
    d_meta = dh0_a[pad:lead]

    grads, delta, new_m, new_v = {}, {}, {}, {}
    for nme in big:
        in_shard_layout = nme not in col_sharded or shard_rows[nme] % LANE != 0
        if in_shard_layout:
            view = (lambda t: t[0].T) if nme in col_sharded else (lambda t: t[0])
            back = (lambda t: t.T[None]) if nme in col_sharded else (lambda t: t[None])
            outs = red[nme].total_and_update(view(wts[nme]), view(mom[nme]), view(var[nme]))
            grads[nme], delta[nme], new_m[nme], new_v[nme] = [back(t) for t in outs]
        else:
            grads[nme] = red[nme].total().T[None]
            outs = _adamw(wts[nme][0], grads[nme][0], mom[nme][0], var[nme][0], "adamw_" + nme)
            delta[nme], new_m[nme], new_v[nme] = [t[None] for t in outs]

    late_names = ['ffn1_pre_g', 'meta_tokens']
    late_parts = [d_pre1, d_meta, loss_part]
    late_sum = _gather_small(_pack_rows(late_parts, d), True, "reduce_small_late")
    small_sums = (_unpack_rows(early_sum, [p.shape for p in early_parts])
                  + _unpack_rows(late_sum, [p.shape for p in late_parts]))
    loss = small_sums.pop()[0, 0]
    for nme, gsm in zip(early_names + late_names, small_sums):
        if nme == 'meta_tokens':
            grads[nme] = lax.dynamic_slice_in_dim(gsm, me * dm, dm, axis=1)
        elif nme in ('lru_conv_w', 'sconv_w'):
            grads[nme] = lax.dynamic_slice_in_dim(gsm, me * cs_, cs_, axis=1)[None]
        else:
            grads[nme] = gsm.reshape(wts[nme].shape)

    rest = [n for n in WEIGHT_NAMES if n not in big]
    rest_shapes = [wts[n].shape for n in rest]
    packed = [_pack_rows([src[n] for n in rest], LANE, 256) for src in (wts, grads, mom, var)]
    for out, packed_out in zip((delta, new_m, new_v), _adamw(*packed, "adamw_small")):
        for nme, arr in zip(rest, _unpack_rows(packed_out, rest_shapes)):
            out[nme] = arr

    return (loss, grad_x, *[grads[n] for n in WEIGHT_NAMES], *[delta[n] for n in WEIGHT_NAMES],
            *[new_m[n] for n in WEIGHT_NAMES], *[new_v[n] for n in WEIGHT_NAMES])
```

```python
import functools

import jax
import jax.numpy as jnp
from jax import lax
from jax.experimental import pallas as pl
from jax.experimental.pallas import tpu as pltpu

F32 = jnp.float32
MXU_DTYPE = jnp.bfloat16
WIRE_DTYPE = jnp.bfloat16
MESH = pl.DeviceIdType.MESH

EPS = 1e-6
LRU_C = 8.0
N_GROUPS = 16
ADAM_LR = 0.001
ADAM_B1 = 0.9
ADAM_B2 = 0.999
ADAM_EPS = 1e-08
ADAM_WD = 0.01
ADAM_STEP = 10

N_DEV = 8
LANE = 128
SUBLANE_BF16 = 16
ROW_ALIGN = 128
F_ALIGN = 512
BD = 256
K_TILE = 512
ACC_ROWS = 528
ACC_GROUP = 1
MIX_ROWS = 128
VMEM_LIMIT_MB = 56

WEIGHT_NAMES = ['meta_tokens', 'ffn1_pre_g', 'ffn1_w_gate', 'ffn1_w_up', 'ffn1_w_down', 'ffn1_post_g',
                'mix_pre_g', 'w_in', 'lru_conv_w', 'lru_conv_b', 'lru_w_a', 'lru_b_a', 'lru_w_x', 'lru_b_x',
                'lru_lambda', 'sconv_w', 'lru_out_g', 'sconv_out_g', 'w_out', 'mix_post_g', 'ffn2_pre_g',
                'ffn2_w_gate', 'ffn2_w_up', 'ffn2_w_down', 'ffn2_post_g']


def _round_up(n, q):
    return (n + q - 1) // q * q


def _tile(n, target, q):
    best = None
    t = q
    while t <= min(n, target):
        if n % t == 0:
            best = t
        t += q
    assert best is not None, (n, target, q)
    return best


def _params(**kw):
    return pltpu.CompilerParams(vmem_limit_bytes=VMEM_LIMIT_MB << 20, **kw)


def _call(body, *, grid, in_specs, out_specs, out_shape, name, args, scratch_shapes=(), carried=(), prefetch=()):
    carried = list(carried)
    n_pf = len(prefetch)

    def launch(fn, in_specs_, out_specs_, out_shape_, scratch_, operands, aliases_):
        if n_pf:
            spec = pltpu.PrefetchScalarGridSpec(num_scalar_prefetch=n_pf, grid=grid, in_specs=in_specs_,
                                                out_specs=out_specs_, scratch_shapes=scratch_)
            return pl.pallas_call(fn, grid_spec=spec, out_shape=out_shape_, input_output_aliases=aliases_,
                                  name=name, compiler_params=_params())(*prefetch, *operands)
        return pl.pallas_call(fn, grid=grid, in_specs=in_specs_, out_specs=out_specs_, out_shape=out_shape_,
                              scratch_shapes=scratch_, input_output_aliases=aliases_, name=name,
                              compiler_params=_params())(*operands)

    if not carried:
        return launch(body, in_specs, out_specs, out_shape, list(scratch_shapes), args, {})
    single = not isinstance(out_shape, (list, tuple))
    out_specs_l = [out_specs] if single else list(out_specs)
    out_shape_l = [out_shape] if single else list(out_shape)
    n_in, n_out, n_scr = len(in_specs), len(out_specs_l), len(scratch_shapes)
    hbm = pl.BlockSpec(memory_space=pl.ANY)
    c_in = [a for cm in carried for a in cm.arrays]
    c_out = [s for cm in carried for s in cm.out_shapes]
    c_scr = []
    aliases = {}
    in_off, out_off = n_pf + n_in, n_out
    for cm in carried:
        c_scr += [pltpu.SemaphoreType.DMA((cm.n_remote,)), pltpu.SemaphoreType.DMA((cm.n_remote,)),
                  pltpu.SemaphoreType.DMA((max(cm.n_local, 1),))]
        for k, v in cm.aliases.items():
            aliases[in_off + k] = out_off + v
        in_off += len(cm.arrays)
        out_off += len(cm.out_shapes)
    steps = 1
    for g in grid:
        steps *= g
    forward_steps = [min(int(cm.forward_at * steps), steps - 1) for cm in carried]

    def wrapped(*refs):
        pf = refs[:n_pf]
        p = n_pf
        ins = refs[p:p + n_in]
        p += n_in
        cins = refs[p:p + len(c_in)]
        p += len(c_in)
        outs = refs[p:p + n_out]
        p += n_out
        couts = refs[p:p + len(c_out)]
        p += len(c_out)
        scr = refs[p:p + n_scr]
        csem = refs[p + n_scr:]
        lin = 0
        for axis, g in enumerate(grid):
            lin = lin * g + pl.program_id(axis)
        views = []
        io = oo = 0
        for j, cm in enumerate(carried):
            views.append((cins[io:io + len(cm.arrays)], couts[oo:oo + len(cm.out_shapes)],
                          csem[3 * j], csem[3 * j + 1], csem[3 * j + 2]))
            io += len(cm.arrays)
            oo += len(cm.out_shapes)

        @pl.when(lin == 0)
        def _():
            for cm, v in zip(carried, views):
                cm.start(*v)

        body(*pf, *ins, *outs, *scr)

        for cm, v, step in zip(carried, views, forward_steps):
            pl.when(lin == step)(functools.partial(cm.forward, *v))

        @pl.when(lin == steps - 1)
        def _():
            for cm, v in zip(carried, views):
                cm.finish(*v)

    res = launch(wrapped, list(in_specs) + [hbm] * len(c_in), out_specs_l + [hbm] * len(c_out),
                 out_shape_l + c_out, list(scratch_shapes) + c_scr, (*args, *c_in), aliases)
    oo = n_out
    for cm in carried:
        cm.results = list(res[oo:oo + len(cm.out_shapes)])
        oo += len(cm.out_shapes)
    return res[0] if single else list(res[:n_out])


def _embed(x, meta, target, g, pad, name, carried=()):
    seq, d = x.shape
    n_meta = meta.shape[0]
    lead = pad + n_meta
    m = lead + seq
    tr = ROW_ALIGN
    lead_blocks = lead // tr
    meta_row = pad - (lead_blocks - 1) * tr
    assert lead % tr == 0 and seq % tr == 0 and 0 <= meta_row and meta_row % 8 == 0

    def body(x_ref, meta_ref, t_ref, g_ref, h_ref, n_ref, tp_ref):
        i = pl.program_id(0)

        @pl.when(i < lead_blocks)
        def _():
            h_ref[...] = jnp.zeros_like(h_ref)
            tp_ref[...] = jnp.zeros_like(tp_ref)

        @pl.when(i == lead_blocks - 1)
        def _():
            h_ref[pl.ds(meta_row, n_meta), :] = meta_ref[...]

        @pl.when(i >= lead_blocks)
        def _():
            h_ref[...] = x_ref[...]
            tp_ref[...] = t_ref[...]

        h = h_ref[...]
        r = lax.rsqrt(jnp.mean(h * h, axis=-1, keepdims=True) + EPS)
        n_ref[...] = (h * r * g_ref[...]).astype(n_ref.dtype)

    tokens = pl.BlockSpec((tr, d), lambda i: (jnp.maximum(i - lead_blocks, 0), 0))
    rows = pl.BlockSpec((tr, d), lambda i: (i, 0))
    return _call(
        body, grid=(m // tr,),
        in_specs=[tokens, pl.BlockSpec((n_meta, d), lambda i: (0, 0)), tokens, pl.BlockSpec((1, d), lambda i: (0, 0))],
        out_specs=[rows, rows, rows],
        out_shape=[jax.ShapeDtypeStruct((m, d), F32), jax.ShapeDtypeStruct((m, d), MXU_DTYPE),
                   jax.ShapeDtypeStruct((m, d), F32)],
        name=name, args=(x, meta, target, g), carried=carried)


def _rmsnorm_bwd_rows(x, g, dy):
    r = lax.rsqrt(jnp.mean(x * x, axis=-1, keepdims=True) + EPS)
    xh = x * r
    dyh = dy * g
    dx = r * (dyh - xh * jnp.mean(dyh * xh, axis=-1, keepdims=True))
    return dx, dy * xh


def _dot_nt(a, b):
    return lax.dot_general(a, b, (((1,), (1,)), ((), ())), preferred_element_type=F32)


def _dot_tn(a, b):
    return lax.dot_general(a, b, (((0,), (0,)), ((), ())), preferred_element_type=F32)


def _mm_nt(a, w, name, carried=(), out_dtype=F32):
    m, k = a.shape
    n = w.shape[0]
    tm = _tile(m, 1056, SUBLANE_BF16)
    tn = _tile(n, 512, LANE)

    def body(a_ref, w_ref, o_ref):
        o_ref[...] = _dot_nt(a_ref[...], w_ref[...]).astype(o_ref.dtype)

    return _call(
        body, grid=(m // tm, n // tn),
        in_specs=[pl.BlockSpec((tm, k), lambda i, j: (i, 0)), pl.BlockSpec((tn, k), lambda i, j: (j, 0))],
        out_specs=pl.BlockSpec((tm, tn), lambda i, j: (i, j)),
        out_shape=jax.ShapeDtypeStruct((m, n), out_dtype), name=name, args=(a, w), carried=carried)


def _norm_bwd_mm_nt(x, g, dy, scale, w, name, carried=()):
    m, d = x.shape
    n = w.shape[0]
    tm = _tile(m, 528, SUBLANE_BF16)

    def body(x_ref, g_ref, dy_ref, w_ref, dx_ref, dg_ref, o_ref):
        @pl.when(pl.program_id(0) == 0)
        def _():
            dg_ref[...] = jnp.zeros_like(dg_ref)

        dx, dgr = _rmsnorm_bwd_rows(x_ref[...], g_ref[...], scale * dy_ref[...])
        dxb = dx.astype(dx_ref.dtype)
        dx_ref[...] = dxb
        dg_ref[...] += jnp.sum(dgr, axis=0, keepdims=True)
        o_ref[...] = _dot_nt(dxb, w_ref[...])

    row = pl.BlockSpec((tm, d), lambda i: (i, 0))
    vec = pl.BlockSpec((1, d), lambda i: (0, 0))
    return _call(
        body, grid=(m // tm,),
        in_specs=[row, vec, row, pl.BlockSpec((n, d), lambda i: (0, 0), pipeline_mode=pl.Buffered(1))],
        out_specs=[row, vec, pl.BlockSpec((tm, n), lambda i: (i, 0))],
        out_shape=[jax.ShapeDtypeStruct((m, d), MXU_DTYPE), jax.ShapeDtypeStruct((1, d), F32),
                   jax.ShapeDtypeStruct((m, n), F32)],
        name=name, args=(x, g, dy, w), carried=carried)


def _ffn_up_act(n_act, wu_t, g_act, name, carried=()):
    m, d = n_act.shape
    fp = wu_t.shape[0]
    tm = _tile(m, 1056, SUBLANE_BF16)
    tn = _tile(fp, 512, LANE)

    def body(n_ref, wu_ref, g_ref, u_ref, a_ref):
        u = _dot_nt(n_ref[...], wu_ref[...])
        g = g_ref[...].astype(F32)
        u_ref[...] = u.astype(u_ref.dtype)
        a_ref[...] = (g * jax.nn.sigmoid(g) * u).astype(a_ref.dtype)

    act = pl.BlockSpec((tm, tn), lambda i, j: (i, j))
    return _call(
        body, grid=(m // tm, fp // tn),
        in_specs=[pl.BlockSpec((tm, d), lambda i, j: (i, 0)), pl.BlockSpec((tn, d), lambda i, j: (j, 0)), act],
        out_specs=[act, act],
        out_shape=[jax.ShapeDtypeStruct((m, fp), MXU_DTYPE)] * 2, name=name, args=(n_act, wu_t, g_act), carried=carried)


def _ffn_gate_up(n_act, wg_t, wu_t, name, carried=()):
    m, d = n_act.shape
    fp = wg_t.shape[0]
    tm = _tile(m, 1056, SUBLANE_BF16)
    tn = _tile(fp, 512, LANE)

    def body(n_ref, wg_ref, wu_ref, g_ref, u_ref, a_ref):
        n = n_ref[...]
        g = _dot_nt(n, wg_ref[...])
        u = _dot_nt(n, wu_ref[...])
        g_ref[...] = g.astype(g_ref.dtype)
        u_ref[...] = u.astype(u_ref.dtype)
        a_ref[...] = (g * jax.nn.sigmoid(g) * u).astype(a_ref.dtype)

    act = pl.BlockSpec((tm, tn), lambda i, j: (i, j))
    wsp = pl.BlockSpec((tn, d), lambda i, j: (j, 0))
    return _call(
        body, grid=(m // tm, fp // tn),
        in_specs=[pl.BlockSpec((tm, d), lambda i, j: (i, 0)), wsp, wsp],
        out_specs=[act, act, act],
        out_shape=[jax.ShapeDtypeStruct((m, fp), MXU_DTYPE)] * 3, name=name, args=(n_act, wg_t, wu_t), carried=carried)


def _ffn_hidden_bwd(dfo, wd, g_act, u_act, name, carried=()):
    m, d = dfo.shape
    fp = wd.shape[0]
    tm = _tile(m, 1056, SUBLANE_BF16)
    tn = _tile(fp, 512, LANE)

    def body(df_ref, wd_ref, g_ref, u_ref, dg_ref, du_ref):
        da = _dot_nt(df_ref[...], wd_ref[...]).astype(dg_ref.dtype)
        g = g_ref[...]
        u = u_ref[...]
        s = jax.nn.sigmoid(g)
        du_ref[...] = da * (g * s)
        dg_ref[...] = da * (u * (s * (1.0 + g * (1.0 - s))))

    act = pl.BlockSpec((tm, tn), lambda i, j: (i, j))
    return _call(
        body, grid=(m // tm, fp // tn),
        in_specs=[pl.BlockSpec((tm, d), lambda i, j: (i, 0)), pl.BlockSpec((tn, d), lambda i, j: (j, 0)), act, act],
        out_specs=[act, act],
        out_shape=[jax.ShapeDtypeStruct((m, fp), MXU_DTYPE)] * 2, name=name, args=(dfo, wd, g_act, u_act),
        carried=carried)


def _row_groups(n_tiles, max_group, nk):
    gsz = max(q for q in range(1, max_group + 1) if n_tiles % q == 0)

    def epilogue_row(grp, kk, i):
        return grp * gsz + jnp.where(kk == nk - 1, i, 0)

    return gsz, epilogue_row


def _mm_residual_norm(a, w, h, g, scale, next_g, name, carried=()):
    m, k = a.shape
    d = w.shape[1]
    tm = _tile(m, ACC_ROWS, SUBLANE_BF16)
    tk = _tile(k, K_TILE, LANE)
    nk = k // tk
    gsz, epilogue_row = _row_groups(m // tm, ACC_GROUP, nk)

    def body(a_ref, w_ref, h_ref, g_ref, ng_ref, fo_ref, hn_ref, nn_ref, acc_ref):
        kk, i = pl.program_id(1), pl.program_id(2)

        @pl.when(kk == 0)
        def _():
            acc_ref[i] = jnp.zeros((tm, d), F32)

        acc_ref[i] += jnp.dot(a_ref[...], w_ref[...], preferred_element_type=F32)

        @pl.when(kk == nk - 1)
        def _():
            fo = acc_ref[i]
            fo_ref[...] = fo
            r = lax.rsqrt(jnp.mean(fo * fo, axis=-1, keepdims=True) + EPS)
            hn = h_ref[...] + scale * (fo * r * g_ref[...])
            hn_ref[...] = hn
            rn = lax.rsqrt(jnp.mean(hn * hn, axis=-1, keepdims=True) + EPS)
            nn_ref[...] = (hn * rn * ng_ref[...]).astype(nn_ref.dtype)

    row = pl.BlockSpec((tm, d), lambda grp, kk, i: (epilogue_row(grp, kk, i), 0))
    vec = pl.BlockSpec((1, d), lambda grp, kk, i: (0, 0))
    return _call(
        body, grid=(m // tm // gsz, nk, gsz),
        in_specs=[pl.BlockSpec((tm, tk), lambda grp, kk, i: (grp * gsz + i, kk)),
                  pl.BlockSpec((tk, d), lambda grp, kk, i: (kk, 0)), row, vec, vec],
        out_specs=[row, row, row],
        out_shape=[jax.ShapeDtypeStruct((m, d), F32)] * 2 + [jax.ShapeDtypeStruct((m, d), MXU_DTYPE)],
        scratch_shapes=[pltpu.VMEM((gsz, tm, d), F32)], name=name, args=(a, w, h, g, next_g), carried=carried)


def _mm_residual_loss(a, w, h, g, scale, target, lead, name, carried=()):
    m, k = a.shape
    d = w.shape[1]
    tm = _tile(m, ACC_ROWS, SUBLANE_BF16)
    tk = _tile(k, K_TILE, LANE)
    nk = k // tk
    gsz, epilogue_row = _row_groups(m // tm, ACC_GROUP, nk)

    def body(a_ref, w_ref, h_ref, g_ref, t_ref, dy_ref, dfo_ref, dg_ref, l_ref, acc_ref):
        grp, kk, i = pl.program_id(0), pl.program_id(1), pl.program_id(2)

        @pl.when(jnp.logical_and(jnp.logical_and(grp == 0, kk == 0), i == 0))
        def _():
            dg_ref[...] = jnp.zeros_like(dg_ref)
            l_ref[...] = jnp.zeros_like(l_ref)

        @pl.when(kk == 0)
        def _():
            acc_ref[i] = jnp.zeros((tm, d), F32)

        acc_ref[i] += jnp.dot(a_ref[...], w_ref[...], preferred_element_type=F32)

        @pl.when(kk == nk - 1)
        def _():
            fo = acc_ref[i]
            gain = g_ref[...]
            r = lax.rsqrt(jnp.mean(fo * fo, axis=-1, keepdims=True) + EPS)
            xh = fo * r
            y = h_ref[...] + scale * (xh * gain)
            row = (grp * gsz + i) * tm + lax.broadcasted_iota(jnp.int32, (tm, 1), 0)
            e = jnp.where(row >= lead, y - t_ref[...], 0.0)
            dy = e * (1.0 / d)
            dy_ref[...] = dy
            l_ref[...] += 0.5 * jnp.sum(jnp.sum(e * e, axis=-1, keepdims=True) * (1.0 / d), axis=0, keepdims=True)
            dn = scale * dy
            dyh = dn * gain
            dfo_ref[...] = (r * (dyh - xh * jnp.mean(dyh * xh, axis=-1, keepdims=True))).astype(dfo_ref.dtype)
            dg_ref[...] += jnp.sum(dn * xh, axis=0, keepdims=True)

    row = pl.BlockSpec((tm, d), lambda grp, kk, i: (epilogue_row(grp, kk, i), 0))
    vec = pl.BlockSpec((1, d), lambda grp, kk, i: (0, 0))
    return _call(
        body, grid=(m // tm // gsz, nk, gsz),
        in_specs=[pl.BlockSpec((tm, tk), lambda grp, kk, i: (grp * gsz + i, kk)),
                  pl.BlockSpec((tk, d), lambda grp, kk, i: (kk, 0)), row, vec, row],
        out_specs=[row, row, vec, pl.BlockSpec((1, 1), lambda grp, kk, i: (0, 0))],
        out_shape=[jax.ShapeDtypeStruct((m, d), F32), jax.ShapeDtypeStruct((m, d), MXU_DTYPE),
                   jax.ShapeDtypeStruct((1, d), F32), jax.ShapeDtypeStruct((1, 1), F32)],
        scratch_shapes=[pltpu.VMEM((gsz, tm, d), F32)], name=name, args=(a, w, h, g, target), carried=carried)


def _norm_bwd_row_tile(m):
    return _tile(m, ACC_ROWS, SUBLANE_BF16)


def _mm_norm_bwd(pairs, h, g, dh_up, name, carried=(), row_tiles=None, dg_init=None, post=None):
    n_pairs = len(pairs)
    m, k = pairs[0][0].shape
    d = h.shape[1]
    tm = _norm_bwd_row_tile(m)
    tk = _tile(k, K_TILE, LANE)
    nk = k // tk
    t0, nt = row_tiles if row_tiles is not None else (0, m // tm)
    gsz, epilogue_row = _row_groups(nt, ACC_GROUP, nk)
    if dg_init is None:
        dg_init = jnp.zeros((1, d), F32)

    n_post = 0 if post is None else 2

    def body(*refs):
        ops = refs[:2 * n_pairs]
        h_ref, g_ref, up_ref, init_ref = refs[2 * n_pairs:2 * n_pairs + 4]
        post_in = refs[2 * n_pairs + 4:2 * n_pairs + 4 + n_post]
        dh_ref, dg_ref = refs[2 * n_pairs + 4 + n_post:2 * n_pairs + 6 + n_post]
        post_out = refs[2 * n_pairs + 6 + n_post:2 * n_pairs + 6 + 2 * n_post]
        acc_ref = refs[-1]
        grp, kk, i = pl.program_id(0), pl.program_id(1), pl.program_id(2)

        @pl.when(jnp.logical_and(jnp.logical_and(grp == 0, kk == 0), i == 0))
        def _():
            dg_ref[...] = init_ref[...]
            if post is not None:
                post_out[1][...] = jnp.zeros_like(post_out[1])

        @pl.when(kk == 0)
        def _():
            acc_ref[i] = jnp.zeros((tm, d), F32)

        for p in range(n_pairs):
            acc_ref[i] += jnp.dot(ops[2 * p][...], ops[2 * p + 1][...], preferred_element_type=F32)

        @pl.when(kk == nk - 1)
        def _():
            dx, dgr = _rmsnorm_bwd_rows(h_ref[...], g_ref[...], acc_ref[i])
            dh = up_ref[...] + dx
            dh_ref[...] = dh
            dg_ref[...] += jnp.sum(dgr, axis=0, keepdims=True)
            if post is not None:
                dfo, dpr = _rmsnorm_bwd_rows(post_in[0][...], post_in[1][...], post[2] * dh)
                post_out[0][...] = dfo.astype(post_out[0].dtype)
                post_out[1][...] += jnp.sum(dpr, axis=0, keepdims=True)

    row_in = pl.BlockSpec((tm, d), lambda grp, kk, i: (t0 + epilogue_row(grp, kk, i), 0))
    row_out = pl.BlockSpec((tm, d), lambda grp, kk, i: (epilogue_row(grp, kk, i), 0))
    vec = pl.BlockSpec((1, d), lambda grp, kk, i: (0, 0))
    in_specs = []
    args = []
    for a, w in pairs:
        in_specs += [pl.BlockSpec((tm, tk), lambda grp, kk, i: (t0 + grp * gsz + i, kk)),
                     pl.BlockSpec((tk, d), lambda grp, kk, i: (kk, 0))]
        args += [a, w]
    in_specs += [row_in, vec, row_in, vec]
    args += [h, g, dh_up, dg_init]
    out_specs = [row_out, vec]
    out_shape = [jax.ShapeDtypeStruct((nt * tm, d), F32), jax.ShapeDtypeStruct((1, d), F32)]
    if post is not None:
        in_specs += [row_in, vec]
        args += [post[0], post[1]]
        out_specs += [row_out, vec]
        out_shape += [jax.ShapeDtypeStruct((nt * tm, d), MXU_DTYPE), jax.ShapeDtypeStruct((1, d), F32)]
    return _call(
        body, grid=(nt // gsz, nk, gsz), in_specs=in_specs, out_specs=out_specs, out_shape=out_shape,
        scratch_shapes=[pltpu.VMEM((gsz, tm, d), F32)], name=name, args=tuple(args), carried=carried)


def _mm_tn(a, b, name, carried=()):
    m, ka = a.shape
    d = b.shape[1]
    tf = _tile(ka, 512, LANE)

    def body(a_ref, b_ref, o_ref):
        o_ref[...] = _dot_tn(a_ref[...], b_ref[...]).astype(o_ref.dtype)

    return _call(
        body, grid=(ka // tf,),
        in_specs=[pl.BlockSpec((m, tf), lambda j: (0, j)),
                  pl.BlockSpec((m, d), lambda j: (0, 0), pipeline_mode=pl.Buffered(1))],
        out_specs=pl.BlockSpec((tf, d), lambda j: (j, 0)),
        out_shape=jax.ShapeDtypeStruct((ka, d), WIRE_DTYPE), name=name, args=(a, b), carried=carried)


GELU_K = 0.7978845608028654
GELU_C = 0.044715


def _expm1(x):
    series = x * (1.0 + x * (1.0 / 2 + x * (1.0 / 6 + x * (1.0 / 24 + x * (1.0 / 120)))))
    return jnp.where(jnp.abs(x) < 0.1, series, jnp.exp(x) - 1.0)


def _softplus(x):
    return jnp.maximum(x, 0.0) + jnp.log1p(jnp.exp(-jnp.abs(x)))


def _block_mm(v, w_ref, transposed):
    nbk = w_ref.shape[0]
    outs = []
    for j in range(nbk):
        vj = v[:, j * BD:(j + 1) * BD]
        outs.append(_dot_nt(vj, w_ref[j]) if transposed else jnp.dot(vj, w_ref[j], preferred_element_type=F32))
    return outs[0] if nbk == 1 else jnp.concatenate(outs, axis=1)


def _group_mean(q, gm_ref):
    hi = q.astype(MXU_DTYPE)
    lo = (q - hi.astype(F32)).astype(MXU_DTYPE)
    nbk = q.shape[1] // BD
    gm = gm_ref[...]
    outs = []
    for j in range(nbk):
        sl = slice(j * BD, (j + 1) * BD)
        outs.append(jnp.dot(hi[:, sl], gm, preferred_element_type=F32) + jnp.dot(lo[:, sl], gm, preferred_element_type=F32))
    return outs[0] if nbk == 1 else jnp.concatenate(outs, axis=1)


class _RowReader:
    def __init__(self, ref):
        self.ref = ref

    def __getitem__(self, rows):
        return self.ref[rows, :]


def _shifted(ext_ref, cur, before8, after8, downs=(), ups=()):
    r = cur.shape[0]
    if downs:
        ext_ref[0:8, :] = before8
    ext_ref[8:8 + r, :] = cur
    if ups:
        ext_ref[8 + r:16 + r, :] = after8
    return [ext_ref[pl.ds(8 - j, r), :] for j in downs] + [ext_ref[pl.ds(8 + j, r), :] for j in ups]


def _lru_gates(xc, pv, wa_ref, wx_ref):
    xcb = xc.astype(MXU_DTYPE)
    ga = jax.nn.sigmoid(_block_mm(xcb, wa_ref, False) + pv[5:6])
    gx = jax.nn.sigmoid(_block_mm(xcb, wx_ref, False) + pv[6:7])
    sp = _softplus(-pv[7:8])
    log_a = -LRU_C * ga * sp
    a = jnp.exp(log_a)
    e2 = _expm1(2.0 * log_a)
    mult = jnp.sqrt(-e2)
    return xcb, ga, gx, sp, a, e2, mult


def _gelu_parts(y):
    th = jnp.tanh(GELU_K * (y + GELU_C * y * y * y))
    return 0.5 * y * (1.0 + th), th


def _scan_block(a, u, sa_ref, su_ref, carry_ref, out_ref, reverse):
    r, c = a.shape
    n = r // 8
    a3 = a.reshape(n, 8, c)
    u3 = u.reshape(n, 8, c)
    sub = lax.broadcasted_iota(jnp.int32, (n, 8, c), 1)
    for dlt in (1, 2, 4):
        keep = (sub < 8 - dlt) if reverse else (sub >= dlt)
        shift = 8 - dlt if reverse else dlt
        sh_a = pltpu.roll(a3, shift, axis=1)
        sh_u = pltpu.roll(u3, shift, axis=1)
        u3 = u3 + a3 * jnp.where(keep, sh_u, 0.0)
        a3 = a3 * jnp.where(keep, sh_a, 1.0)
    sa_ref[...] = a3.reshape(r, c)
    su_ref[...] = u3.reshape(r, c)
    for k in (range(n - 1, -1, -1) if reverse else range(n)):
        rows = pl.ds(8 * k, 8)
        out_ref[rows, :] = su_ref[rows, :] + sa_ref[rows, :] * carry_ref[...]
        carry_ref[...] = out_ref[pl.ds(8 * k if reverse else 8 * k + 7, 1), :]


def _mixer_fwd(z, pv, wa, wx, gm, pad, name, carried=()):
    m = z.shape[0]
    c = pv.shape[1]
    r = MIX_ROWS
    nb = m // r

    def body(z_ref, pv_ref, wa_ref, wx_ref, gm_ref, mixed_ref, hs_ref, ext_ref, tailx_ref, tailc_ref, carry_ref,
             sa_ref, su_ref):
        b = pl.program_id(0)

        @pl.when(b == 0)
        def _():
            tailx_ref[...] = jnp.zeros_like(tailx_ref)
            tailc_ref[...] = jnp.zeros_like(tailc_ref)
            carry_ref[...] = jnp.zeros_like(carry_ref)

        pv = _RowReader(pv_ref)
        row = b * r + lax.broadcasted_iota(jnp.int32, (r, 1), 0)
        maskf = (row >= pad).astype(F32)
        y = z_ref[:, 0:c]
        xl = z_ref[:, c:2 * c]
        bs = z_ref[:, 2 * c:3 * c]
        cv = z_ref[:, 3 * c:4 * c] * z_ref[:, 4 * c:5 * c]

        x1, x2, x3 = _shifted(ext_ref, xl, tailx_ref[...], None, downs=(1, 2, 3))
        tailx_ref[...] = z_ref[pl.ds(r - 8, 8), c:2 * c]
        xc = pv[4:5] + pv[3:4] * xl + pv[2:3] * x1 + pv[1:2] * x2 + pv[0:1] * x3
        _, _, gx, _, a, _, mult = _lru_gates(xc, pv, wa_ref, wx_ref)
        uu = mult * (gx * xc) * maskf

        _scan_block(a, uu, sa_ref, su_ref, carry_ref, hs_ref, reverse=False)
        hs = hs_ref[...]

        gelu_y, _ = _gelu_parts(y)
        lru_out = hs * gelu_y
        c1, c2 = _shifted(ext_ref, cv, tailc_ref[...], None, downs=(1, 2))
        tailc_ref[...] = cv[r - 8:r]
        sc_out = bs * (pv[10:11] * cv + pv[9:10] * c1 + pv[8:9] * c2)

        rl = lax.rsqrt(_group_mean(lru_out * lru_out, gm_ref) + EPS)
        rs = lax.rsqrt(_group_mean(sc_out * sc_out, gm_ref) + EPS)
        mixed_ref[:, 0:c] = (lru_out * rl * pv[11:12]).astype(mixed_ref.dtype)
        mixed_ref[:, c:2 * c] = (sc_out * rs * pv[12:13]).astype(mixed_ref.dtype)

    full = lambda shape: pl.BlockSpec(shape, lambda b: (0,) * len(shape))
    return _call(
        body, grid=(nb,),
        in_specs=[pl.BlockSpec((r, 5 * c), lambda b: (b, 0)), full(pv.shape), full(wa.shape), full(wx.shape), full(gm.shape)],
        out_specs=[pl.BlockSpec((r, 2 * c), lambda b: (b, 0)), pl.BlockSpec((r, c), lambda b: (b, 0))],
        out_shape=[jax.ShapeDtypeStruct((m, 2 * c), MXU_DTYPE), jax.ShapeDtypeStruct((m, c), F32)],
        scratch_shapes=[pltpu.VMEM((r + 16, c), F32), pltpu.VMEM((8, c), F32), pltpu.VMEM((8, c), F32),
                        pltpu.VMEM((1, c), F32), pltpu.VMEM((r, c), F32), pltpu.VMEM((r, c), F32)],
        name=name, args=(z, pv, wa, wx, gm), carried=carried)


def _mixer_bwd(z, hs, dmixed, pv, wa, wx, gm, pad, name, carried=()):
    m = z.shape[0]
    c = pv.shape[1]
    r = MIX_ROWS
    nb = m // r
    r8 = r // 8
    assert pad <= r and pad % SUBLANE_BF16 == 0

    def body(z_ref, zp_ref, hs_ref, hsp_ref, dm_ref, pv_ref, wa_ref, wx_ref, gm_ref,
             dz_ref, dpv_ref, dwa_ref, dwx_ref, ext_ref, hxc_ref, hsc_ref, hp_ref, pc_ref, sa_ref, su_ref, p_ref):
        i = pl.program_id(0)
        b = nb - 1 - i

        @pl.when(i == 0)
        def _():
            hxc_ref[...] = jnp.zeros_like(hxc_ref)
            hsc_ref[...] = jnp.zeros_like(hsc_ref)
            hp_ref[...] = jnp.zeros_like(hp_ref)
            pc_ref[...] = jnp.zeros_like(pc_ref)
            dpv_ref[...] = jnp.zeros_like(dpv_ref)
            dwa_ref[...] = jnp.zeros_like(dwa_ref)
            dwx_ref[...] = jnp.zeros_like(dwx_ref)

        pv = _RowReader(pv_ref)
        row = b * r + lax.broadcasted_iota(jnp.int32, (r, 1), 0)
        maskf = (row >= pad).astype(F32)
        has_prev = (b > 0).astype(F32)
        y = z_ref[:, 0:c]
        xl = z_ref[:, c:2 * c]
        bs = z_ref[:, 2 * c:3 * c]
        cs = z_ref[:, 3 * c:4 * c]
        vs = z_ref[:, 4 * c:5 * c]
        cv = cs * vs
        xl_prev = zp_ref[:, c:2 * c] * has_prev
        cv_prev = zp_ref[:, 3 * c:4 * c] * zp_ref[:, 4 * c:5 * c] * has_prev
        hs = hs_ref[...]

        x1, x2, x3 = _shifted(ext_ref, xl, xl_prev, None, downs=(1, 2, 3))
        xc = pv[4:5] + pv[3:4] * xl + pv[2:3] * x1 + pv[1:2] * x2 + pv[0:1] * x3
        xcb, ga, gx, sp, a, e2, mult = _lru_gates(xc, pv, wa_ref, wx_ref)
        gxx = gx * xc
        gelu_y, th = _gelu_parts(y)
        lru_out = hs * gelu_y
        c1, c2 = _shifted(ext_ref, cv, cv_prev, None, downs=(1, 2))
        sc = pv[10:11] * cv + pv[9:10] * c1 + pv[8:9] * c2
        sc_out = bs * sc

        def group_norm_bwd(v, dm, gain):
            rr = lax.rsqrt(_group_mean(v * v, gm_ref) + EPS)
            vh = v * rr
            dvh = dm * gain
            dv = rr * (dvh - vh * _group_mean(dvh * vh, gm_ref))
            return dv, jnp.sum(dm * vh, axis=0, keepdims=True)

        d_lru_out, d_og = group_norm_bwd(lru_out, dm_ref[:, 0:c], pv[11:12])
        d_sc_out, d_sg = group_norm_bwd(sc_out, dm_ref[:, c:2 * c], pv[12:13])
        dpv_ref[11:12, :] += d_og
        dpv_ref[12:13, :] += d_sg

        dhs = d_lru_out * gelu_y
        dgelu = 0.5 * (1.0 + th) + 0.5 * y * (1.0 - th * th) * GELU_K * (1.0 + 3.0 * GELU_C * y * y)
        dy = d_lru_out * hs * dgelu

        _scan_block(a, a * dhs, sa_ref, su_ref, pc_ref, p_ref, reverse=True)
        (p_next,) = _shifted(ext_ref, p_ref[...], None, hp_ref[...], ups=(1,))
        hp_ref[...] = p_ref[0:8, :]
        q = dhs + p_next
        (hs_prev,) = _shifted(ext_ref, hs, hsp_ref[...] * has_prev, None, downs=(1,))
        duu = q * maskf
        da = q * hs_prev

        dmult = duu * gxx
        dgxx = duu * mult
        dgx = dgxx * xc
        dxc = dgxx * gx
        dlog_a = da * a - dmult * ((1.0 + e2) / mult)
        dga = dlog_a * (-LRU_C * sp)
        dsp = jnp.sum(dlog_a * (-LRU_C * ga), axis=0, keepdims=True)
        dpv_ref[7:8, :] += dsp * (-jax.nn.sigmoid(-pv[7:8]))
        dga_pre = dga * ga * (1.0 - ga)
        dgx_pre = dgx * gx * (1.0 - gx)
        dpv_ref[5:6, :] += jnp.sum(dga_pre, axis=0, keepdims=True)
        dpv_ref[6:7, :] += jnp.sum(dgx_pre, axis=0, keepdims=True)
        dga_b = dga_pre.astype(MXU_DTYPE)
        dgx_b = dgx_pre.astype(MXU_DTYPE)
        dxc = dxc + _block_mm(dga_b, wa_ref, True) + _block_mm(dgx_b, wx_ref, True)
        for j in range(c // BD):
            sl = slice(j * BD, (j + 1) * BD)
            dwa_ref[j] += _dot_tn(xcb[:, sl], dga_b[:, sl])
            dwx_ref[j] += _dot_tn(xcb[:, sl], dgx_b[:, sl])

        dpv_ref[4:5, :] += jnp.sum(dxc, axis=0, keepdims=True)
        dpv_ref[3:4, :] += jnp.sum(dxc * xl, axis=0, keepdims=True)
        dpv_ref[2:3, :] += jnp.sum(dxc * x1, axis=0, keepdims=True)
        dpv_ref[1:2, :] += jnp.sum(dxc * x2, axis=0, keepdims=True)
        dpv_ref[0:1, :] += jnp.sum(dxc * x3, axis=0, keepdims=True)
        u1, u2, u3 = _shifted(ext_ref, dxc, None, hxc_ref[...], ups=(1, 2, 3))
        hxc_ref[...] = dxc[0:8]
        dxl = pv[3:4] * dxc + pv[2:3] * u1 + pv[1:2] * u2 + pv[0:1] * u3

        dbs = d_sc_out * sc
        dsc = d_sc_out * bs
        dpv_ref[10:11, :] += jnp.sum(dsc * cv, axis=0, keepdims=True)
        dpv_ref[9:10, :] += jnp.sum(dsc * c1, axis=0, keepdims=True)
        dpv_ref[8:9, :] += jnp.sum(dsc * c2, axis=0, keepdims=True)
        s1, s2 = _shifted(ext_ref, dsc, None, hsc_ref[...], ups=(1, 2))
        hsc_ref[...] = dsc[0:8]
        dcv = pv[10:11] * dsc + pv[9:10] * s1 + pv[8:9] * s2

        dz_ref[:, 0:c] = dy.astype(dz_ref.dtype)
        dz_ref[:, c:2 * c] = dxl.astype(dz_ref.dtype)
        dz_ref[:, 2 * c:3 * c] = dbs.astype(dz_ref.dtype)
        dz_ref[:, 3 * c:4 * c] = (dcv * vs).astype(dz_ref.dtype)
        dz_ref[:, 4 * c:5 * c] = (dcv * cs).astype(dz_ref.dtype)

        if pad:
            @pl.when(b == 0)
            def _():
                dz_ref[0:pad, :] = jnp.zeros((pad, 5 * c), dz_ref.dtype)

    full = lambda shape: pl.BlockSpec(shape, lambda i: (0,) * len(shape))
    cur = lambda width: pl.BlockSpec((r, width), lambda i: (nb - 1 - i, 0))
    prev8 = lambda width: pl.BlockSpec((8, width), lambda i: (jnp.maximum((nb - 1 - i) * r8 - 1, 0), 0))
    return _call(
        body, grid=(nb,),
        in_specs=[cur(5 * c), prev8(5 * c), cur(c), prev8(c), cur(2 * c),
                  full(pv.shape), full(wa.shape), full(wx.shape), full(gm.shape)],
        out_specs=[cur(5 * c), full(pv.shape), full(wa.shape), full(wx.shape)],
        out_shape=[jax.ShapeDtypeStruct((m, 5 * c), MXU_DTYPE), jax.ShapeDtypeStruct(pv.shape, F32),
                   jax.ShapeDtypeStruct(wa.shape, F32), jax.ShapeDtypeStruct(wx.shape, F32)],
        scratch_shapes=[pltpu.VMEM((r + 16, c), F32), pltpu.VMEM((8, c), F32), pltpu.VMEM((8, c), F32),
                        pltpu.VMEM((8, c), F32), pltpu.VMEM((1, c), F32), pltpu.VMEM((r, c), F32),
                        pltpu.VMEM((r, c), F32), pltpu.VMEM((r, c), F32)],
        name=name, args=(z, z, hs, hs, dmixed, pv, wa, wx, gm), carried=carried)


def _position():
    return lax.axis_index("x"), lax.axis_index("y"), lax.axis_index("c")


def _block_of(px, py, pc):
    return 4 * px + 2 * py + pc


class _TwoLevelGather:
    def __init__(self, n_arrays, rows_of, src_of, send_sems, recv_sems):
        x, y, c = _position()
        self.n, self.rows_of, self.src_of = n_arrays, rows_of, src_of
        self.send_sems, self.recv_sems = send_sems, recv_sems
        self.c, self.me, self.sibling = c, (x, y, c), (x, y, 1 - c)
        self.chips = [(1 - x, y), (x, 1 - y), (1 - x, 1 - y)]

    def _copy(self, i, k, block, to, src=None):
        return pltpu.make_async_remote_copy(
            src_ref=self.rows_of(i, *block) if src is None else src, dst_ref=self.rows_of(i, *block),
            send_sem=self.send_sems.at[7 * i + k], recv_sem=self.recv_sems.at[7 * i + k],
            device_id=to, device_id_type=MESH)

    def _first(self, i):
        own = [self._copy(i, 0, self.me, self.sibling, src=self.src_of(i))]
        return own + [self._copy(i, 1 + j, self.me, (*chip, self.c), src=self.src_of(i))
                      for j, chip in enumerate(self.chips)]

    def _passed(self, i, j):
        return self._copy(i, 4 + j, (*self.chips[j], self.c), self.sibling)

    def start(self):
        for i in range(self.n):
            for cp in self._first(i):
                cp.start()

    def forward(self):
        for i in range(self.n):
            for j, chip in enumerate(self.chips):
                self._copy(i, 1 + j, (*chip, self.c), self.me).wait_recv()
                self._passed(i, j).start()

    def drain(self):
        for i in range(self.n):
            self._copy(i, 0, self.sibling, self.me).wait_recv()
            for j, chip in enumerate(self.chips):
                self._copy(i, 4 + j, (*chip, 1 - self.c), self.me).wait_recv()
        for i in range(self.n):
            for cp in self._first(i) + [self._passed(i, j) for j in range(3)]:
                cp.wait_send()


class _RelayGather:
    def __init__(self, n_arrays, rows_of, src_of, send_sems, recv_sems):
        x, y, c = _position()
        self.n, self.rows_of, self.src_of = n_arrays, rows_of, src_of
        self.send_sems, self.recv_sems = send_sems, recv_sems
        self.me, self.sibling = (x, y, c), (x, y, 1 - c)
        self.xn, self.yn, self.dg = (1 - x, y, c), (x, 1 - y, c), (1 - x, 1 - y, c)

    def _copy(self, i, k, block, to, half=None, src=None):
        rows = self.rows_of(i, *block, half)
        return pltpu.make_async_remote_copy(
            src_ref=rows if src is None else src, dst_ref=rows,
            send_sem=self.send_sems.at[8 * i + k], recv_sem=self.recv_sems.at[8 * i + k],
            device_id=to, device_id_type=MESH)

    def _sends(self, i):
        own = self.src_of(i)
        return [self._copy(i, 0, self.me, self.sibling, src=own), self._copy(i, 1, self.me, self.xn, src=own),
                self._copy(i, 2, self.me, self.yn, src=own),
                self._copy(i, 3, self.xn, self.yn, half=0), self._copy(i, 4, self.yn, self.xn, half=1),
                self._copy(i, 5, self.xn, self.sibling), self._copy(i, 6, self.yn, self.sibling),
                self._copy(i, 7, self.dg, self.sibling)]

    def start(self):
        for i in range(self.n):
            for cp in self._sends(i)[0:3]:
                cp.start()

    def forward(self):
        for i in range(self.n):
            self._copy(i, 1, self.xn, self.me).wait_recv()
            self._copy(i, 2, self.yn, self.me).wait_recv()
            for cp in self._sends(i)[3:7]:
                cp.start()

    def drain(self):
        x, y, c = self.me
        for i in range(self.n):
            self._copy(i, 3, self.dg, self.me, half=0).wait_recv()
            self._copy(i, 4, self.dg, self.me, half=1).wait_recv()
            self._sends(i)[7].start()
        for i in range(self.n):
            self._copy(i, 0, self.sibling, self.me).wait_recv()
            self._copy(i, 5, (1 - x, y, 1 - c), self.me).wait_recv()
            self._copy(i, 6, (x, 1 - y, 1 - c), self.me).wait_recv()
            self._copy(i, 7, (1 - x, 1 - y, 1 - c), self.me).wait_recv()
        for i in range(self.n):
            for cp in self._sends(i):
                cp.wait_send()


class _CarriedGather:
    def __init__(self, shards, padded_rows, zeros, forward_at, part=None, into=None):
        d = shards[0].shape[1]
        self.forward_at = forward_at
        self.n = len(shards)
        self.rows = [s.shape[0] for s in shards]
        self.pads = [p - N_DEV * r for r, p in zip(self.rows, padded_rows)]
        assert max(self.pads) <= zeros.shape[0] and zeros.shape[1] == d
        self.part = part if part is not None else (0, self.rows[0])
        assert (part is None and into is None) or self.n == 1
        assert self.part[0] % SUBLANE_BF16 == 0 and self.part[1] % SUBLANE_BF16 == 0
        self.arrays = list(shards) + [zeros] + ([into] if into is not None else [])
        self.out_shapes = [jax.ShapeDtypeStruct((p, d), s.dtype) for s, p in zip(shards, padded_rows)]
        self.aliases = {self.n + 1: 0} if into is not None else {}
        if into is not None:
            self.pads = [0] * self.n
        self.n_remote, self.n_local = 8 * self.n, 2 * self.n
        self.results = None

    def _rows_of(self, outs):
        def rows_of(i, px, py, pc, half):
            first, count = (self.part if self.n == 1 else (0, self.rows[i]))
            head = _round_up(count // 2, SUBLANE_BF16)
            if half == 0:
                count = head
            elif half == 1:
                first, count = first + head, count - head
            first = _block_of(px, py, pc) * self.rows[i] + first
            return outs[i].at[pl.ds(pl.multiple_of(first, SUBLANE_BF16), count), :]
        return rows_of

    def _own(self, ins, i):
        return ins[i].at[pl.ds(self.part[0], self.part[1]), :] if self.n == 1 else ins[i]

    def _gather(self, ins, outs, send_sems, recv_sems):
        return _RelayGather(self.n, self._rows_of(outs), functools.partial(self._own, ins), send_sems, recv_sems)

    def _local(self, ins, outs, local_sems):
        x, y, c = _position()
        rows_of = self._rows_of(outs)
        cps = []
        for i in range(self.n):
            cps.append(pltpu.make_async_copy(self._own(ins, i), rows_of(i, x, y, c, None), local_sems.at[2 * i]))
            if self.pads[i]:
                cps.append(pltpu.make_async_copy(ins[self.n].at[pl.ds(0, self.pads[i]), :],
                                                 outs[i].at[pl.ds(N_DEV * self.rows[i], self.pads[i]), :],
                                                 local_sems.at[2 * i + 1]))
        return cps

    def start(self, ins, outs, send_sems, recv_sems, local_sems):
        for cp in self._local(ins, outs, local_sems):
            cp.start()
        self._gather(ins, outs, send_sems, recv_sems).start()

    def forward(self, ins, outs, send_sems, recv_sems, local_sems):
        self._gather(ins, outs, send_sems, recv_sems).forward()

    def finish(self, ins, outs, send_sems, recv_sems, local_sems):
        self._gather(ins, outs, send_sems, recv_sems).drain()
        for cp in self._local(ins, outs, local_sems):
            cp.wait()


class _CarriedSwap:
    def __init__(self, grads, shard_rows):
        d = grads[0].shape[1]
        self.n, self.rows = len(grads), list(shard_rows)
        self.arrays = list(grads)
        self.out_shapes = [jax.ShapeDtypeStruct((4, s, d), g.dtype) for g, s in zip(grads, shard_rows)]
        self.aliases = {}
        self.n_remote, self.n_local = 4 * self.n, 0
        self.forward_at = 1.0
        self.results = None

    def _copies(self, ins, outs, send_sems, recv_sems):
        x, y, c = _position()
        cps = []
        for i in range(self.n):
            s = self.rows[i]
            for k in range(4):
                blk = _block_of(k >> 1, k & 1, 1 - c)
                cps.append(pltpu.make_async_remote_copy(
                    src_ref=ins[i].at[pl.ds(pl.multiple_of(blk * s, SUBLANE_BF16), s), :], dst_ref=outs[i].at[k],
                    send_sem=send_sems.at[4 * i + k], recv_sem=recv_sems.at[4 * i + k],
                    device_id=(x, y, 1 - c), device_id_type=MESH))
        return cps

    def start(self, ins, outs, send_sems, recv_sems, local_sems):
        for cp in self._copies(ins, outs, send_sems, recv_sems):
            cp.start()

    def forward(self, *_):
        pass

    def finish(self, ins, outs, send_sems, recv_sems, local_sems):
        for cp in self._copies(ins, outs, send_sems, recv_sems):
            cp.wait()


class _CarriedChipExchange:
    def __init__(self, presums, part=None, into=None):
        self.n = len(presums)
        assert (part is None and into is None) or self.n == 1
        self.part = part if part is not None else (0, presums[0].shape[1])
        assert self.part[0] % SUBLANE_BF16 == 0 and self.part[1] % SUBLANE_BF16 == 0
        self.arrays = list(presums) + ([into] if into is not None else [])
        self.out_shapes = [jax.ShapeDtypeStruct(p.shape, p.dtype) for p in presums]
        self.aliases = {self.n: 0} if into is not None else {}
        self.n_remote, self.n_local = 3 * self.n, 0
        self.forward_at = 1.0
        self.results = None

    def _copies(self, ins, outs, send_sems, recv_sems):
        x, y, c = _position()
        cps = []
        for i in range(self.n):
            rows = pl.ds(*self.part) if self.n == 1 else pl.ds(0, self.arrays[i].shape[1])
            for r in range(1, 4):
                cps.append(pltpu.make_async_remote_copy(
                    src_ref=ins[i].at[r - 1, rows, :], dst_ref=outs[i].at[r - 1, rows, :],
                    send_sem=send_sems.at[3 * i + r - 1], recv_sem=recv_sems.at[3 * i + r - 1],
                    device_id=(x ^ (r >> 1), y ^ (r & 1), c), device_id_type=MESH))
        return cps

    def start(self, ins, outs, send_sems, recv_sems, local_sems):
        for cp in self._copies(ins, outs, send_sems, recv_sems):
            cp.start()

    def forward(self, *_):
        pass

    def finish(self, ins, outs, send_sems, recv_sems, local_sems):
        for cp in self._copies(ins, outs, send_sems, recv_sems):
            cp.wait()


def _gather_small(block, reduce, name):
    rr, nn = block.shape

    def body(x_ref, out_ref, *rest):
        if reduce:
            stack_ref, send_sems, recv_sems, local_sem = rest
        else:
            send_sems, recv_sems, local_sem = rest
            stack_ref = out_ref
        x, y, c = _position()

        def rows_of(i, px, py, pc):
            return stack_ref.at[pl.ds(pl.multiple_of(_block_of(px, py, pc) * rr, 8), rr), :]

        own = pltpu.make_async_copy(x_ref, rows_of(0, x, y, c), local_sem)
        own.start()
        gather = _TwoLevelGather(1, rows_of, lambda i: x_ref, send_sems, recv_sems)
        gather.start()
        gather.forward()
        gather.drain()
        own.wait()
        if reduce:
            acc = stack_ref[0:rr, :]
            for k in range(1, N_DEV):
                acc = acc + stack_ref[k * rr:(k + 1) * rr, :]
            out_ref[...] = acc

    vmem = pl.BlockSpec(memory_space=pltpu.VMEM)
    scratch = [pltpu.SemaphoreType.DMA((7,)), pltpu.SemaphoreType.DMA((7,)), pltpu.SemaphoreType.DMA]
    if reduce:
        scratch = [pltpu.VMEM((N_DEV * rr, nn), F32)] + scratch
    out_rows = rr if reduce else N_DEV * rr
    return pl.pallas_call(
        body, in_specs=[vmem], out_specs=vmem, out_shape=jax.ShapeDtypeStruct((out_rows, nn), F32),
        scratch_shapes=scratch, name=name, compiler_params=_params())(block)


def _sum_stack(stack, name):
    rr = stack.shape[0] // N_DEV

    def body(s_ref, o_ref):
        acc = s_ref[0:rr, :]
        for k in range(1, N_DEV):
            acc = acc + s_ref[k * rr:(k + 1) * rr, :]
        o_ref[...] = acc

    vmem = pl.BlockSpec(memory_space=pltpu.VMEM)
    return pl.pallas_call(body, in_specs=[vmem], out_specs=vmem,
                          out_shape=jax.ShapeDtypeStruct((rr, stack.shape[1]), F32), name=name,
                          compiler_params=_params())(stack)


def _presum(where, grad, swapped, name):
    s, d = swapped.shape[1], swapped.shape[2]
    tc = _tile(d, 2048, LANE)

    def body(where_ref, g_ref, sw_ref, o_ref):
        o_ref[0] = (g_ref[...].astype(F32) + sw_ref[0].astype(F32)).astype(o_ref.dtype)

    return _call(
        body, grid=(3, d // tc),
        in_specs=[pl.BlockSpec((s, tc), lambda r, j, where: (where[1 + r], j)),
                  pl.BlockSpec((1, s, tc), lambda r, j, where: (where[5 + r], 0, j))],
        out_specs=pl.BlockSpec((1, s, tc), lambda r, j, where: (r, 0, j)),
        out_shape=jax.ShapeDtypeStruct((3, s, d), WIRE_DTYPE), name=name, args=(grad, swapped), prefetch=(where,))


def _final_sum(where, grad, swapped, received, name, carried=()):
    s, d = swapped.shape[1], swapped.shape[2]
    tc = _tile(d, 512, LANE)

    def body(where_ref, g_ref, sw_ref, r_ref, o_ref):
        acc = g_ref[...].astype(F32) + sw_ref[0].astype(F32)
        for k in range(3):
            acc = acc + r_ref[k].astype(F32)
        o_ref[...] = acc

    return _call(
        body, grid=(d // tc,),
        in_specs=[pl.BlockSpec((s, tc), lambda j, where: (where[0], j)),
                  pl.BlockSpec((1, s, tc), lambda j, where: (where[4], 0, j)),
                  pl.BlockSpec((3, s, tc), lambda j, where: (0, 0, j))],
        out_specs=pl.BlockSpec((s, tc), lambda j, where: (0, j)),
        out_shape=jax.ShapeDtypeStruct((s, d), F32), name=name, args=(grad, swapped, received),
        prefetch=(where,), carried=carried)


class _GradReduction:
    def __init__(self, key, grad, shard_rows, where):
        self.key, self.grad, self.rows, self.where = key, grad, shard_rows, where
        self._presum = self._exchange = None

    def swap(self):
        self._swap = _CarriedSwap([self.grad], [self.rows])
        return self._swap

    def exchange(self, part=None):
        if self._presum is None:
            self._presum = _presum(self.where, self.grad, self._swap.results[0], "presum_" + self.key)
        rows = None
        if part is not None:
            half = _round_up(self.rows // 2, SUBLANE_BF16)
            rows = (0, half) if part == 0 else (half, self.rows - half)
        into = self._exchange.results[0] if part == 1 else None
        self._exchange = _CarriedChipExchange([self._presum], rows, into)
        return self._exchange

    def total(self, carried=()):
        return _final_sum(self.where, self.grad, self._swap.results[0], self._exchange.results[0],
                          "sum_" + self.key, carried)

    def total_and_update(self, w, m, v):
        return _sum_adamw(self.where, self.grad, self._swap.results[0], self._exchange.results[0], w, m, v,
                          "update_" + self.key)


def _adamw_math(w, g, m, v):
    nm = ADAM_B1 * m + (1.0 - ADAM_B1) * g
    nv = ADAM_B2 * v + (1.0 - ADAM_B2) * (g * g)
    m_hat = nm / (1.0 - ADAM_B1 ** ADAM_STEP)
    v_hat = nv / (1.0 - ADAM_B2 ** ADAM_STEP)
    return -ADAM_LR * (m_hat / (jnp.sqrt(v_hat) + ADAM_EPS) + ADAM_WD * w), nm, nv


def _sum_adamw(where, grad, swapped, received, w, m, v, name):
    s, d = swapped.shape[1], swapped.shape[2]
    tc = _tile(d, 512, LANE)

    def body(where_ref, g_ref, sw_ref, r_ref, w_ref, m_ref, v_ref, gs_ref, d_ref, nm_ref, nv_ref):
        g = g_ref[...].astype(F32) + sw_ref[0].astype(F32)
        for k in range(3):
            g = g + r_ref[k].astype(F32)
        gs_ref[...] = g
        d_ref[...], nm_ref[...], nv_ref[...] = _adamw_math(w_ref[...], g, m_ref[...], v_ref[...])

    blk = pl.BlockSpec((s, tc), lambda j, where: (0, j))
    return _call(
        body, grid=(d // tc,),
        in_specs=[pl.BlockSpec((s, tc), lambda j, where: (where[0], j)),
                  pl.BlockSpec((1, s, tc), lambda j, where: (where[4], 0, j)),
                  pl.BlockSpec((3, s, tc), lambda j, where: (0, 0, j)), blk, blk, blk],
        out_specs=[blk] * 4, out_shape=[jax.ShapeDtypeStruct((s, d), F32)] * 4, name=name,
        args=(grad, swapped, received, w, m, v), prefetch=(where,))


def _adamw(w, g, m, v, name):
    rows, cols = w.shape
    tr = _tile(rows, 256, 8)

    def body(w_ref, g_ref, m_ref, v_ref, d_ref, nm_ref, nv_ref):
        d_ref[...], nm_ref[...], nv_ref[...] = _adamw_math(w_ref[...], g_ref[...], m_ref[...], v_ref[...])

    spec = pl.BlockSpec((tr, cols), lambda i: (i, 0))
    return pl.pallas_call(
        body, grid=(rows // tr,), in_specs=[spec] * 4, out_specs=[spec] * 3,
        out_shape=[jax.ShapeDtypeStruct((rows, cols), F32)] * 3, name=name, compiler_params=_params())(w, g, m, v)


def _pack_rows(arrays, width, row_quantum=8):
    flat = jnp.concatenate([a.reshape(-1) for a in arrays])
    total = _round_up(flat.shape[0], row_quantum * width)
    flat = jnp.pad(flat, (0, total - flat.shape[0]))
    return flat.reshape(-1, width)


def _unpack_rows(packed, shapes):
    flat = packed.reshape(-1)
    out = []
    off = 0
    for shp in shapes:
        size = 1
        for s in shp:
            size *= s
        out.append(flat[off:off + size].reshape(shp))
        off += size
    return out


def _block_diag(w):
    h, hb, _ = w.shape
    per = BD // hb
    w4 = w.reshape(h // per, per, hb, hb)
    eye = jnp.eye(per, dtype=w.dtype)
    return jnp.einsum('npij,pq->npiqj', w4, eye).reshape(h // per, BD, BD)


def _block_diag_extract(bd, hb):
    nbk = bd.shape[0]
    per = BD // hb
    b5 = bd.reshape(nbk, per, hb, per, hb)
    eye = jnp.eye(per, dtype=bd.dtype)
    return jnp.einsum('npiqj,pq->npij', b5, eye).reshape(nbk * per, hb, hb)


def kernel(x, meta_tokens, ffn1_pre_g, ffn1_w_gate, ffn1_w_up, ffn1_w_down, ffn1_post_g, mix_pre_g, w_in, lru_conv_w, lru_conv_b, lru_w_a, lru_b_a, lru_w_x, lru_b_x, lru_lambda, sconv_w, lru_out_g, sconv_out_g, w_out, mix_post_g, ffn2_pre_g, ffn2_w_gate, ffn2_w_up, ffn2_w_down, ffn2_post_g, loss_target, m_meta_tokens, m_ffn1_pre_g, m_ffn1_w_gate, m_ffn1_w_up, m_ffn1_w_down, m_ffn1_post_g, m_mix_pre_g, m_w_in, m_lru_conv_w, m_lru_conv_b, m_lru_w_a, m_lru_b_a, m_lru_w_x, m_lru_b_x, m_lru_lambda, m_sconv_w, m_lru_out_g, m_sconv_out_g, m_w_out, m_mix_post_g, m_ffn2_pre_g, m_ffn2_w_gate, m_ffn2_w_up, m_ffn2_w_down, m_ffn2_post_g, v_meta_tokens, v_ffn1_pre_g, v_ffn1_w_gate, v_ffn1_w_up, v_ffn1_w_down, v_ffn1_post_g, v_mix_pre_g, v_w_in, v_lru_conv_w, v_lru_conv_b, v_lru_w_a, v_lru_b_a, v_lru_w_x, v_lru_b_x, v_lru_lambda, v_sconv_w, v_lru_out_g, v_sconv_out_g, v_w_out, v_mix_post_g, v_ffn2_pre_g, v_ffn2_w_gate, v_ffn2_w_up, v_ffn2_w_down, v_ffn2_post_g):
    given = dict(locals())
    wts = {n: given[n] for n in WEIGHT_NAMES}
    mom = {n: given["m_" + n] for n in WEIGHT_NAMES}
    var = {n: given["v_" + n] for n in WEIGHT_NAMES}

    xi, yi, ci = _position()
    me = _block_of(xi, yi, ci)
    x2 = x[0]
    seq, d = x2.shape
    n_meta = meta_tokens.shape[0]
    m_rows = _round_up(n_meta + seq, ROW_ALIGN)
    pad = m_rows - n_meta - seq
    lead = pad + n_meta
    c = lru_conv_b.shape[1]
    hb = lru_w_a.shape[-1]
    dm = meta_tokens.shape[1]
    cs_ = lru_conv_w.shape[2]
    kw4, kw3 = lru_conv_w.shape[1], sconv_w.shape[1]
    assert d == 2 * c and c % BD == 0 and BD % hb == 0 and cs_ <= dm and kw4 == 4 and kw3 == 3

    small = jnp.zeros((_round_up(n_meta + kw4 + kw3, 8), dm), F32)
    small = small.at[0:n_meta].set(meta_tokens)
    small = small.at[n_meta:n_meta + kw4, 0:cs_].set(lru_conv_w[0])
    small = small.at[n_meta + kw4:n_meta + kw4 + kw3, 0:cs_].set(sconv_w[0])
    sr = small.shape[0]
    small_all = _gather_small(small, False, "gather_small").reshape(N_DEV, sr, dm)
    meta_full = small_all[:, 0:n_meta, :].transpose(1, 0, 2).reshape(n_meta, d)
    conv_w_full = small_all[:, n_meta:n_meta + kw4, 0:cs_].transpose(1, 0, 2).reshape(kw4, c)
    sconv_w_full = small_all[:, n_meta + kw4:n_meta + kw4 + kw3, 0:cs_].transpose(1, 0, 2).reshape(kw3, c)

    big = ['ffn1_w_gate', 'ffn1_w_up', 'ffn1_w_down', 'w_in', 'w_out', 'ffn2_w_gate', 'ffn2_w_up', 'ffn2_w_down']
    col_sharded = {'ffn1_w_gate', 'ffn1_w_up', 'w_in', 'ffn2_w_gate', 'ffn2_w_up'}
    shards = []
    for nme in big:
        w = wts[nme][0].astype(WIRE_DTYPE)
        shards.append(w.T if nme in col_sharded else w)
    shard_rows = dict(zip(big, [s.shape[0] for s in shards]))
    zeros = jnp.zeros((F_ALIGN, d), WIRE_DTYPE)

    def gather(forward_at, *names, part=None, into=None):
        sel = [shards[big.index(nme)] for nme in names]
        padded = [_round_up(N_DEV * shard_rows[nme], LANE if nme in ('w_in', 'w_out') else F_ALIGN) for nme in names]
        return _CarriedGather(sel, padded, zeros, forward_at, part, into)

    pv = jnp.zeros((16, c), F32)
    pv = pv.at[0:4].set(conv_w_full).at[4].set(lru_conv_b[0]).at[5].set(lru_b_a[0]).at[6].set(lru_b_x[0])
    pv = pv.at[7].set(lru_lambda[0]).at[8:11].set(sconv_w_full).at[11].set(lru_out_g[0]).at[12].set(sconv_out_g[0])
    wa_bd = _block_diag(lru_w_a[0]).astype(MXU_DTYPE)
    wx_bd = _block_diag(lru_w_x[0]).astype(MXU_DTYPE)
    gs = c // N_GROUPS
    gidx = jnp.arange(BD) // gs
    gm = jnp.where(gidx[:, None] == gidx[None, :], 1.0 / gs, 0.0).astype(MXU_DTYPE)

    ride = gather(0.3, 'ffn1_w_gate')
    h0, n1, target = _embed(x2, meta_full, loss_target[0], ffn1_pre_g, pad, "embed_prenorm", carried=[ride])
    (wg1,) = ride.results
    ride = gather(0.6, 'ffn1_w_up')
    g1 = _mm_nt(n1, wg1, "ffn1_gate", carried=[ride], out_dtype=MXU_DTYPE)
    (wu1,) = ride.results
    ride = gather(0.5, 'ffn1_w_down')
    u1, a1 = _ffn_up_act(n1, wu1, g1, "ffn1_up_act", carried=[ride])
    (wd1,) = ride.results
    ride = gather(0.6, 'w_in', 'w_out')
    fo1, h1, un = _mm_residual_norm(a1, wd1, h0, ffn1_post_g, 0.5, mix_pre_g, "ffn1_down", carried=[ride])
    win_t, wout = ride.results
    s2 = shard_rows['ffn2_w_gate']
    quarter = _round_up(s2 // 4, SUBLANE_BF16)
    ride_g = gather(0.5, 'ffn2_w_gate', part=(0, 3 * quarter))
    z = _mm_nt(un, win_t, "mix_in_proj", carried=[ride_g])
    ride_g = gather(0.5, 'ffn2_w_gate', part=(3 * quarter, s2 - 3 * quarter), into=ride_g.results[0])
    ride_u = gather(0.5, 'ffn2_w_up', part=(0, quarter))
    mixed, hs = _mixer_fwd(z, pv, wa_bd, wx_bd, gm, pad, "mixer_fwd", carried=[ride_g, ride_u])
    (wg2,) = ride_g.results
    ride_u = gather(0.5, 'ffn2_w_up', part=(quarter, s2 - quarter), into=ride_u.results[0])
    o_mix, h2, n2 = _mm_residual_norm(mixed, wout, h1, mix_post_g, 1.0, ffn2_pre_g, "mix_out_proj", carried=[ride_u])
    (wu2,) = ride_u.results
    ride = gather(0.75, 'ffn2_w_down')
    g2, u2, a2 = _ffn_gate_up(n2, wg2, wu2, "ffn2_gate_up", carried=[ride])
    (wd2,) = ride.results
    dh3, dfo2, d_post2, loss_part = _mm_residual_loss(a2, wd2, h2, ffn2_post_g, 0.5, target, lead, "ffn2_down_loss")

    chip_rel = [2 * (xi ^ (r >> 1)) + (yi ^ (r & 1)) for r in range(4)]
    where = jnp.stack([2 * k + ci for k in chip_rel] + chip_rel).astype(jnp.int32)
    red = {}

    def reduction(nme, grad):
        red[nme] = _GradReduction(nme, grad, shard_rows[nme], where)
        return red[nme]

    r_wd2 = reduction('ffn2_w_down', _mm_tn(a2, dfo2, "ffn2_dw_down"))
    dg2, du2 = _ffn_hidden_bwd(dfo2, wd2, g2, u2, "ffn2_hidden_bwd", carried=[r_wd2.swap()])
    r_wg2 = reduction('ffn2_w_gate', _mm_tn(dg2, n2, "ffn2_dw_gate", carried=[r_wd2.exchange(part=0)]))
    r_wu2 = reduction('ffn2_w_up', _mm_tn(du2, n2, "ffn2_dw_up", carried=[r_wd2.exchange(part=1), r_wg2.swap()]))
    dh2, d_pre2 = _mm_norm_bwd([(dg2, wg2), (du2, wu2)], h2, ffn2_pre_g, dh3, "ffn2_dx",
                               carried=[r_wg2.exchange(), r_wu2.swap()])
    do_mix, d_mix_post, dmixed = _norm_bwd_mm_nt(o_mix, mix_post_g, dh2, 1.0, wout, "mix_out_proj_bwd")
    r_wout = reduction('w_out', _mm_tn(mixed, do_mix, "mix_dw_out"))
    dz, dpv, dwa_bd, dwx_bd = _mixer_bwd(z, hs, dmixed, pv, wa_bd, wx_bd, gm, pad, "mixer_bwd",
                                         carried=[r_wu2.exchange(), r_wout.swap()])
    r_win = reduction('w_in', _mm_tn(dz, un, "mix_dw_in", carried=[r_wout.exchange()]))
    dh1, d_mix_pre, dfo1, d_post1 = _mm_norm_bwd([(dz, win_t)], h1, mix_pre_g, dh2, "mix_dx", carried=[r_win.swap()],
                                                 post=(fo1, ffn1_post_g, 0.5))
    r_wd1 = reduction('ffn1_w_down', _mm_tn(a1, dfo1, "ffn1_dw_down", carried=[r_win.exchange(part=0)]))
    early_names = ['mix_pre_g', 'mix_post_g', 'ffn2_pre_g', 'ffn2_post_g', 'ffn1_post_g',
                   'lru_conv_b', 'lru_b_a', 'lru_b_x', 'lru_lambda', 'lru_out_g', 'sconv_out_g',
                   'lru_conv_w', 'sconv_w', 'lru_w_a', 'lru_w_x']
    early_parts = [d_mix_pre, d_mix_post, d_pre2, d_post2, d_post1,
                   dpv[4:5], dpv[5:6], dpv[6:7], dpv[7:8], dpv[11:12], dpv[12:13],
                   dpv[0:4], dpv[8:11], _block_diag_extract(dwa_bd, hb), _block_diag_extract(dwx_bd, hb)]
    early_packed = _pack_rows(early_parts, d, SUBLANE_BF16)
    early_ride = _CarriedGather([early_packed], [N_DEV * early_packed.shape[0]], zeros, 0.75)
    dg1, du1 = _ffn_hidden_bwd(dfo1, wd1, g1, u1, "ffn1_hidden_bwd",
                               carried=[r_win.exchange(part=1), r_wd1.swap(), early_ride])
    early_sum = _sum_stack(early_ride.results[0], "sum_small_early")
    r_wg1 = reduction('ffn1_w_gate', _mm_tn(dg1, n1, "ffn1_dw_gate", carried=[r_wd1.exchange(part=0)]))
    r_wu1 = reduction('ffn1_w_up', _mm_tn(du1, n1, "ffn1_dw_up", carried=[r_wd1.exchange(part=1), r_wg1.swap()]))
    row_tile = _norm_bwd_row_tile(m_rows)
    n_tiles = m_rows // row_tile
    half = n_tiles // 2
    assert half >= 1 and half * row_tile >= lead
    dh0_a, d_pre1_a = _mm_norm_bwd([(dg1, wg1), (du1, wu1)], h0, ffn1_pre_g, dh1, "ffn1_dx_a",
                                   carried=[r_wg1.exchange(), r_wu1.swap()], row_tiles=(0, half))
    dh0_b, d_pre1 = _mm_norm_bwd([(dg1, wg1), (du1, wu1)], h0, ffn1_pre_g, dh1, "ffn1_dx_b",
                                 carried=[r_wu1.exchange()], row_tiles=(half, n_tiles - half), dg_init=d_pre1_a)
    grad_x = jnp.concatenate([dh0_a[lead:], dh0_b], axis=0)[None]
    d_meta = dh0_a[pad:lead]

    grads, delta, new_m, new_v = {}, {}, {}, {}
    for nme in big:
        in_shard_layout = nme not in col_sharded or shard_rows[nme] % LANE != 0
        if in_shard_layout:
            view = (lambda t: t[0].T) if nme in col_sharded else (lambda t: t[0])
            back = (lambda t: t.T[None]) if nme in col_sharded else (lambda t: t[None])
            outs = red[nme].total_and_update(view(wts[nme]), view(mom[nme]), view(var[nme]))
            grads[nme], delta[nme], new_m[nme], new_v[nme] = [back(t) for t in outs]
        else:
            grads[nme] = red[nme].total().T[None]
            outs = _adamw(wts[nme][0], grads[nme][0], mom[nme][0], var[nme][0], "adamw_" + nme)
            delta[nme], new_m[nme], new_v[nme] = [t[None] for t in outs]

    late_names = ['ffn1_pre_g', 'meta_tokens']
    late_parts = [d_pre1, d_meta, loss_part]
    late_sum = _gather_small(_pack_rows(late_parts, d), True, "reduce_small_late")
    small_sums = (_unpack_rows(early_sum, [p.shape for p in early_parts])
                  + _unpack_rows(late_sum, [p.shape for p in late_parts]))
    loss = small_sums.pop()[0, 0]
    for nme, gsm in zip(early_names + late_names, small_sums):
        if nme == 'meta_tokens':
            grads[nme] = lax.dynamic_slice_in_dim(gsm, me * dm, dm, axis=1)
        elif nme in ('lru_conv_w', 'sconv_w'):
            grads[nme] = lax.dynamic_slice_in_dim(gsm, me * cs_, cs_, axis=1)[None]
        else:
            grads[nme] = gsm.reshape(wts[nme].shape)

    rest = [n for n in WEIGHT_NAMES if n not in big]
    rest_shapes = [wts[n].shape for n in rest]
    packed = [_pack_rows([src[n] for n in rest], LANE, 256) for src in (wts, grads, mom, var)]
    for out, packed_out in zip((delta, new_m, new_v), _adamw(*packed, "adamw_small")):
        for nme, arr in zip(rest, _unpack_rows(packed_out, rest_shapes)):
            out[nme] = arr

    return (loss, grad_x, *[grads[n] for n in WEIGHT_NAMES], *[delta[n] for n in WEIGHT_NAMES],
            *[new_m[n] for n in WEIGHT_NAMES], *[new_v[n] for n in WEIGHT_NAMES])
```

```python
import functools

import jax
import jax.numpy as jnp
from jax import lax
from jax.experimental import pallas as pl
from jax.experimental.pallas import tpu as pltpu

F32 = jnp.float32
MXU_DTYPE = jnp.bfloat16
WIRE_DTYPE = jnp.bfloat16
MESH = pl.DeviceIdType.MESH

EPS = 1e-6
LRU_C = 8.0
N_GROUPS = 16
ADAM_LR = 0.001
ADAM_B1 = 0.9
ADAM_B2 = 0.999
ADAM_EPS = 1e-08
ADAM_WD = 0.01
ADAM_STEP = 10

N_DEV = 8
LANE = 128
SUBLANE_BF16 = 16
ROW_ALIGN = 128
F_ALIGN = 512
BD = 256
K_TILE = 512
ACC_ROWS = 528
ACC_GROUP = 1
MIX_ROWS = 128
VMEM_LIMIT_MB = 56

WEIGHT_NAMES = ['meta_tokens', 'ffn1_pre_g', 'ffn1_w_gate', 'ffn1_w_up', 'ffn1_w_down', 'ffn1_post_g',
                'mix_pre_g', 'w_in', 'lru_conv_w', 'lru_conv_b', 'lru_w_a', 'lru_b_a', 'lru_w_x', 'lru_b_x',
                'lru_lambda', 'sconv_w', 'lru_out_g', 'sconv_out_g', 'w_out', 'mix_post_g', 'ffn2_pre_g',
                'ffn2_w_gate', 'ffn2_w_up', 'ffn2_w_down', 'ffn2_post_g']


def _round_up(n, q):
    return (n + q - 1) // q * q


def _tile(n, target, q):
    best = None
    t = q
    while t <= min(n, target):
        if n % t == 0:
            best = t
        t += q
    assert best is not None, (n, target, q)
    return best


def _params(**kw):
    return pltpu.CompilerParams(vmem_limit_bytes=VMEM_LIMIT_MB << 20, **kw)


def _call(body, *, grid, in_specs, out_specs, out_shape, name, args, scratch_shapes=(), carried=(), prefetch=()):
    carried = list(carried)
    n_pf = len(prefetch)

    def launch(fn, in_specs_, out_specs_, out_shape_, scratch_, operands, aliases_):
        if n_pf:
            spec = pltpu.PrefetchScalarGridSpec(num_scalar_prefetch=n_pf, grid=grid, in_specs=in_specs_,
                                                out_specs=out_specs_, scratch_shapes=scratch_)
            return pl.pallas_call(fn, grid_spec=spec, out_shape=out_shape_, input_output_aliases=aliases_,
                                  name=name, compiler_params=_params())(*prefetch, *operands)
        return pl.pallas_call(fn, grid=grid, in_specs=in_specs_, out_specs=out_specs_, out_shape=out_shape_,
                              scratch_shapes=scratch_, input_output_aliases=aliases_, name=name,
                              compiler_params=_params())(*operands)

    if not carried:
        return launch(body, in_specs, out_specs, out_shape, list(scratch_shapes), args, {})
    single = not isinstance(out_shape, (list, tuple))
    out_specs_l = [out_specs] if single else list(out_specs)
    out_shape_l = [out_shape] if single else list(out_shape)
    n_in, n_out, n_scr = len(in_specs), len(out_specs_l), len(scratch_shapes)
    hbm = pl.BlockSpec(memory_space=pl.ANY)
    c_in = [a for cm in carried for a in cm.arrays]
    c_out = [s for cm in carried for s in cm.out_shapes]
    c_scr = []
    aliases = {}
    in_off, out_off = n_pf + n_in, n_out
    for cm in carried:
        c_scr += [pltpu.SemaphoreType.DMA((cm.n_remote,)), pltpu.SemaphoreType.DMA((cm.n_remote,)),
                  pltpu.SemaphoreType.DMA((max(cm.n_local, 1),))]
        for k, v in cm.aliases.items():
            aliases[in_off + k] = out_off + v
        in_off += len(cm.arrays)
        out_off += len(cm.out_shapes)
    steps = 1
    for g in grid:
        steps *= g
    forward_steps = [min(int(cm.forward_at * steps), steps - 1) for cm in carried]

    def wrapped(*refs):
        pf = refs[:n_pf]
        p = n_pf
        ins = refs[p:p + n_in]
        p += n_in
        cins = refs[p:p + len(c_in)]
        p += len(c_in)
        outs = refs[p:p + n_out]
        p += n_out
        couts = refs[p:p + len(c_out)]
        p += len(c_out)
        scr = refs[p:p + n_scr]
        csem = refs[p + n_scr:]
        lin = 0
        for axis, g in enumerate(grid):
            lin = lin * g + pl.program_id(axis)
        views = []
        io = oo = 0
        for j, cm in enumerate(carried):
            views.append((cins[io:io + len(cm.arrays)], couts[oo:oo + len(cm.out_shapes)],
                          csem[3 * j], csem[3 * j + 1], csem[3 * j + 2]))
            io += len(cm.arrays)
            oo += len(cm.out_shapes)

        @pl.when(lin == 0)
        def _():
            for cm, v in zip(carried, views):
                cm.start(*v)

        body(*pf, *ins, *outs, *scr)

        for cm, v, step in zip(carried, views, forward_steps):
            pl.when(lin == step)(functools.partial(cm.forward, *v))

        @pl.when(lin == steps - 1)
        def _():
            for cm, v in zip(carried, views):
                cm.finish(*v)

    res = launch(wrapped, list(in_specs) + [hbm] * len(c_in), out_specs_l + [hbm] * len(c_out),
                 out_shape_l + c_out, list(scratch_shapes) + c_scr, (*args, *c_in), aliases)
    oo = n_out
    for cm in carried:
        cm.results = list(res[oo:oo + len(cm.out_shapes)])
        oo += len(cm.out_shapes)
    return res[0] if single else list(res[:n_out])


def _embed(x, meta, target, g, pad, name, carried=()):
    seq, d = x.shape
    n_meta = meta.shape[0]
    lead = pad + n_meta
    m = lead + seq
    tr = ROW_ALIGN
    lead_blocks = lead // tr
    meta_row = pad - (lead_blocks - 1) * tr
    assert lead % tr == 0 and seq % tr == 0 and 0 <= meta_row and meta_row % 8 == 0

    def body(x_ref, meta_ref, t_ref, g_ref, h_ref, n_ref, tp_ref):
        i = pl.program_id(0)

        @pl.when(i < lead_blocks)
        def _():
            h_ref[...] = jnp.zeros_like(h_ref)
            tp_ref[...] = jnp.zeros_like(tp_ref)

        @pl.when(i == lead_blocks - 1)
        def _():
            h_ref[pl.ds(meta_row, n_meta), :] = meta_ref[...]

        @pl.when(i >= lead_blocks)
        def _():
            h_ref[...] = x_ref[...]
            tp_ref[...] = t_ref[...]

        h = h_ref[...]
        r = lax.rsqrt(jnp.mean(h * h, axis=-1, keepdims=True) + EPS)
        n_ref[...] = (h * r * g_ref[...]).astype(n_ref.dtype)

    tokens = pl.BlockSpec((tr, d), lambda i: (jnp.maximum(i - lead_blocks, 0), 0))
    rows = pl.BlockSpec((tr, d), lambda i: (i, 0))
    return _call(
        body, grid=(m // tr,),
        in_specs=[tokens, pl.BlockSpec((n_meta, d), lambda i: (0, 0)), tokens, pl.BlockSpec((1, d), lambda i: (0, 0))],
        out_specs=[rows, rows, rows],
        out_shape=[jax.ShapeDtypeStruct((m, d), F32), jax.ShapeDtypeStruct((m, d), MXU_DTYPE),
                   jax.ShapeDtypeStruct((m, d), F32)],
        name=name, args=(x, meta, target, g), carried=carried)


def _rmsnorm_bwd_rows(x, g, dy):
    r = lax.rsqrt(jnp.mean(x * x, axis=-1, keepdims=True) + EPS)
    xh = x * r
    dyh = dy * g
    dx = r * (dyh - xh * jnp.mean(dyh * xh, axis=-1, keepdims=True))
    return dx, dy * xh


def _dot_nt(a, b):
    return lax.dot_general(a, b, (((1,), (1,)), ((), ())), preferred_element_type=F32)


def _dot_tn(a, b):
    return lax.dot_general(a, b, (((0,), (0,)), ((), ())), preferred_element_type=F32)


def _mm_nt(a, w, name, carried=(), out_dtype=F32):
    m, k = a.shape
    n = w.shape[0]
    tm = _tile(m, 1056, SUBLANE_BF16)
    tn = _tile(n, 512, LANE)

    def body(a_ref, w_ref, o_ref):
        o_ref[...] = _dot_nt(a_ref[...], w_ref[...]).astype(o_ref.dtype)

    return _call(
        body, grid=(m // tm, n // tn),
        in_specs=[pl.BlockSpec((tm, k), lambda i, j: (i, 0)), pl.BlockSpec((tn, k), lambda i, j: (j, 0))],
        out_specs=pl.BlockSpec((tm, tn), lambda i, j: (i, j)),
        out_shape=jax.ShapeDtypeStruct((m, n), out_dtype), name=name, args=(a, w), carried=carried)


def _norm_bwd_mm_nt(x, g, dy, scale, w, name, carried=()):
    m, d = x.shape
    n = w.shape[0]
    tm = _tile(m, 528, SUBLANE_BF16)

    def body(x_ref, g_ref, dy_ref, w_ref, dx_ref, dg_ref, o_ref):
        @pl.when(pl.program_id(0) == 0)
        def _():
            dg_ref[...] = jnp.zeros_like(dg_ref)

        dx, dgr = _rmsnorm_bwd_rows(x_ref[...], g_ref[...], scale * dy_ref[...])
        dxb = dx.astype(dx_ref.dtype)
        dx_ref[...] = dxb
        dg_ref[...] += jnp.sum(dgr, axis=0, keepdims=True)
        o_ref[...] = _dot_nt(dxb, w_ref[...])

    row = pl.BlockSpec((tm, d), lambda i: (i, 0))
    vec = pl.BlockSpec((1, d), lambda i: (0, 0))
    return _call(
        body, grid=(m // tm,),
        in_specs=[row, vec, row, pl.BlockSpec((n, d), lambda i: (0, 0), pipeline_mode=pl.Buffered(1))],
        out_specs=[row, vec, pl.BlockSpec((tm, n), lambda i: (i, 0))],
        out_shape=[jax.ShapeDtypeStruct((m, d), MXU_DTYPE), jax.ShapeDtypeStruct((1, d), F32),
                   jax.ShapeDtypeStruct((m, n), F32)],
        name=name, args=(x, g, dy, w), carried=carried)


def _ffn_up_act(n_act, wu_t, g_act, name, carried=()):
    m, d = n_act.shape
    fp = wu_t.shape[0]
    tm = _tile(m, 1056, SUBLANE_BF16)
    tn = _tile(fp, 512, LANE)

    def body(n_ref, wu_ref, g_ref, u_ref, a_ref):
        u = _dot_nt(n_ref[...], wu_ref[...])
        g = g_ref[...].astype(F32)
        u_ref[...] = u.astype(u_ref.dtype)
        a_ref[...] = (g * jax.nn.sigmoid(g) * u).astype(a_ref.dtype)

    act = pl.BlockSpec((tm, tn), lambda i, j: (i, j))
    return _call(
        body, grid=(m // tm, fp // tn),
        in_specs=[pl.BlockSpec((tm, d), lambda i, j: (i, 0)), pl.BlockSpec((tn, d), lambda i, j: (j, 0)), act],
        out_specs=[act, act],
        out_shape=[jax.ShapeDtypeStruct((m, fp), MXU_DTYPE)] * 2, name=name, args=(n_act, wu_t, g_act), carried=carried)


def _ffn_gate_up(n_act, wg_t, wu_t, name, carried=()):
    m, d = n_act.shape
    fp = wg_t.shape[0]
    tm = _tile(m, 1056, SUBLANE_BF16)
    tn = _tile(fp, 512, LANE)

    def body(n_ref, wg_ref, wu_ref, g_ref, u_ref, a_ref):
        n = n_ref[...]
        g = _dot_nt(n, wg_ref[...])
        u = _dot_nt(n, wu_ref[...])
        g_ref[...] = g.astype(g_ref.dtype)
        u_ref[...] = u.astype(u_ref.dtype)
        a_ref[...] = (g * jax.nn.sigmoid(g) * u).astype(a_ref.dtype)

    act = pl.BlockSpec((tm, tn), lambda i, j: (i, j))
    wsp = pl.BlockSpec((tn, d), lambda i, j: (j, 0))
    return _call(
        body, grid=(m // tm, fp // tn),
        in_specs=[pl.BlockSpec((tm, d), lambda i, j: (i, 0)), wsp, wsp],
        out_specs=[act, act, act],
        out_shape=[jax.ShapeDtypeStruct((m, fp), MXU_DTYPE)] * 3, name=name, args=(n_act, wg_t, wu_t), carried=carried)


def _ffn_hidden_bwd(dfo, wd, g_act, u_act, name, carried=()):
    m, d = dfo.shape
    fp = wd.shape[0]
    tm = _tile(m, 1056, SUBLANE_BF16)
    tn = _tile(fp, 512, LANE)

    def body(df_ref, wd_ref, g_ref, u_ref, dg_ref, du_ref):
        da = _dot_nt(df_ref[...], wd_ref[...]).astype(dg_ref.dtype)
        g = g_ref[...]
        u = u_ref[...]
        s = jax.nn.sigmoid(g)
        du_ref[...] = da * (g * s)
        dg_ref[...] = da * (u * (s * (1.0 + g * (1.0 - s))))

    act = pl.BlockSpec((tm, tn), lambda i, j: (i, j))
    return _call(
        body, grid=(m // tm, fp // tn),
        in_specs=[pl.BlockSpec((tm, d), lambda i, j: (i, 0)), pl.BlockSpec((tn, d), lambda i, j: (j, 0)), act, act],
        out_specs=[act, act],
        out_shape=[jax.ShapeDtypeStruct((m, fp), MXU_DTYPE)] * 2, name=name, args=(dfo, wd, g_act, u_act),
        carried=carried)


def _row_groups(n_tiles, max_group, nk):
    gsz = max(q for q in range(1, max_group + 1) if n_tiles % q == 0)

    def epilogue_row(grp, kk, i):
        return grp * gsz + jnp.where(kk == nk - 1, i, 0)

    return gsz, epilogue_row


def _mm_residual_norm(a, w, h, g, scale, next_g, name, carried=()):
    m, k = a.shape
    d = w.shape[1]
    tm = _tile(m, ACC_ROWS, SUBLANE_BF16)
    tk = _tile(k, K_TILE, LANE)
    nk = k // tk
    gsz, epilogue_row = _row_groups(m // tm, ACC_GROUP, nk)

    def body(a_ref, w_ref, h_ref, g_ref, ng_ref, fo_ref, hn_ref, nn_ref, acc_ref):
        kk, i = pl.program_id(1), pl.program_id(2)

        @pl.when(kk == 0)
        def _():
            acc_ref[i] = jnp.zeros((tm, d), F32)

        acc_ref[i] += jnp.dot(a_ref[...], w_ref[...], preferred_element_type=F32)

        @pl.when(kk == nk - 1)
        def _():
            fo = acc_ref[i]
            fo_ref[...] = fo
            r = lax.rsqrt(jnp.mean(fo * fo, axis=-1, keepdims=True) + EPS)
            hn = h_ref[...] + scale * (fo * r * g_ref[...])
            hn_ref[...] = hn
            rn = lax.rsqrt(jnp.mean(hn * hn, axis=-1, keepdims=True) + EPS)
            nn_ref[...] = (hn * rn * ng_ref[...]).astype(nn_ref.dtype)

    row = pl.BlockSpec((tm, d), lambda grp, kk, i: (epilogue_row(grp, kk, i), 0))
    vec = pl.BlockSpec((1, d), lambda grp, kk, i: (0, 0))
    return _call(
        body, grid=(m // tm // gsz, nk, gsz),
        in_specs=[pl.BlockSpec((tm, tk), lambda grp, kk, i: (grp * gsz + i, kk)),
                  pl.BlockSpec((tk, d), lambda grp, kk, i: (kk, 0)), row, vec, vec],
        out_specs=[row, row, row],
        out_shape=[jax.ShapeDtypeStruct((m, d), F32)] * 2 + [jax.ShapeDtypeStruct((m, d), MXU_DTYPE)],
        scratch_shapes=[pltpu.VMEM((gsz, tm, d), F32)], name=name, args=(a, w, h, g, next_g), carried=carried)


def _mm_residual_loss(a, w, h, g, scale, target, lead, name, carried=()):
    m, k = a.shape
    d = w.shape[1]
    tm = _tile(m, ACC_ROWS, SUBLANE_BF16)
    tk = _tile(k, K_TILE, LANE)
    nk = k // tk
    gsz, epilogue_row = _row_groups(m // tm, ACC_GROUP, nk)

    def body(a_ref, w_ref, h_ref, g_ref, t_ref, dy_ref, dfo_ref, dg_ref, l_ref, acc_ref):
        grp, kk, i = pl.program_id(0), pl.program_id(1), pl.program_id(2)

        @pl.when(jnp.logical_and(jnp.logical_and(grp == 0, kk == 0), i == 0))
        def _():
            dg_ref[...] = jnp.zeros_like(dg_ref)
            l_ref[...] = jnp.zeros_like(l_ref)

        @pl.when(kk == 0)
        def _():
            acc_ref[i] = jnp.zeros((tm, d), F32)

        acc_ref[i] += jnp.dot(a_ref[...], w_ref[...], preferred_element_type=F32)

        @pl.when(kk == nk - 1)
        def _():
            fo = acc_ref[i]
            gain = g_ref[...]
            r = lax.rsqrt(jnp.mean(fo * fo, axis=-1, keepdims=True) + EPS)
            xh = fo * r
            y = h_ref[...] + scale * (xh * gain)
            row = (grp * gsz + i) * tm + lax.broadcasted_iota(jnp.int32, (tm, 1), 0)
            e = jnp.where(row >= lead, y - t_ref[...], 0.0)
            dy = e * (1.0 / d)
            dy_ref[...] = dy
            l_ref[...] += 0.5 * jnp.sum(jnp.sum(e * e, axis=-1, keepdims=True) * (1.0 / d), axis=0, keepdims=True)
            dn = scale * dy
            dyh = dn * gain
            dfo_ref[...] = (r * (dyh - xh * jnp.mean(dyh * xh, axis=-1, keepdims=True))).astype(dfo_ref.dtype)
            dg_ref[...] += jnp.sum(dn * xh, axis=0, keepdims=True)

    row = pl.BlockSpec((tm, d), lambda grp, kk, i: (epilogue_row(grp, kk, i), 0))
    vec = pl.BlockSpec((1, d), lambda grp, kk, i: (0, 0))
    return _call(
        body, grid=(m // tm // gsz, nk, gsz),
        in_specs=[pl.BlockSpec((tm, tk), lambda grp, kk, i: (grp * gsz + i, kk)),
                  pl.BlockSpec((tk, d), lambda grp, kk, i: (kk, 0)), row, vec, row],
        out_specs=[row, row, vec, pl.BlockSpec((1, 1), lambda grp, kk, i: (0, 0))],
        out_shape=[jax.ShapeDtypeStruct((m, d), F32), jax.ShapeDtypeStruct((m, d), MXU_DTYPE),
                   jax.ShapeDtypeStruct((1, d), F32), jax.ShapeDtypeStruct((1, 1), F32)],
        scratch_shapes=[pltpu.VMEM((gsz, tm, d), F32)], name=name, args=(a, w, h, g, target), carried=carried)


def _norm_bwd_row_tile(m):
    return _tile(m, ACC_ROWS, SUBLANE_BF16)


def _mm_norm_bwd(pairs, h, g, dh_up, name, carried=(), row_tiles=None, dg_init=None, post=None):
    n_pairs = len(pairs)
    m, k = pairs[0][0].shape
    d = h.shape[1]
    tm = _norm_bwd_row_tile(m)
    tk = _tile(k, K_TILE, LANE)
    nk = k // tk
    t0, nt = row_tiles if row_tiles is not None else (0, m // tm)
    gsz, epilogue_row = _row_groups(nt, ACC_GROUP, nk)
    if dg_init is None:
        dg_init = jnp.zeros((1, d), F32)

    n_post = 0 if post is None else 2

    def body(*refs):
        ops = refs[:2 * n_pairs]
        h_ref, g_ref, up_ref, init_ref = refs[2 * n_pairs:2 * n_pairs + 4]
        post_in = refs[2 * n_pairs + 4:2 * n_pairs + 4 + n_post]
        dh_ref, dg_ref = refs[2 * n_pairs + 4 + n_post:2 * n_pairs + 6 + n_post]
        post_out = refs[2 * n_pairs + 6 + n_post:2 * n_pairs + 6 + 2 * n_post]
        acc_ref = refs[-1]
        grp, kk, i = pl.program_id(0), pl.program_id(1), pl.program_id(2)

        @pl.when(jnp.logical_and(jnp.logical_and(grp == 0, kk == 0), i == 0))
        def _():
            dg_ref[...] = init_ref[...]
            if post is not None:
                post_out[1][...] = jnp.zeros_like(post_out[1])

        @pl.when(kk == 0)
        def _():
            acc_ref[i] = jnp.zeros((tm, d), F32)

        for p in range(n_pairs):
            acc_ref[i] += jnp.dot(ops[2 * p][...], ops[2 * p + 1][...], preferred_element_type=F32)

        @pl.when(kk == nk - 1)
        def _():
            dx, dgr = _rmsnorm_bwd_rows(h_ref[...], g_ref[...], acc_ref[i])
            dh = up_ref[...] + dx
            dh_ref[...] = dh
            dg_ref[...] += jnp.sum(dgr, axis=0, keepdims=True)
            if post is not None:
                dfo, dpr = _rmsnorm_bwd_rows(post_in[0][...], post_in[1][...], post[2] * dh)
                post_out[0][...] = dfo.astype(post_out[0].dtype)
                post_out[1][...] += jnp.sum(dpr, axis=0, keepdims=True)

    row_in = pl.BlockSpec((tm, d), lambda grp, kk, i: (t0 + epilogue_row(grp, kk, i), 0))
    row_out = pl.BlockSpec((tm, d), lambda grp, kk, i: (epilogue_row(grp, kk, i), 0))
    vec = pl.BlockSpec((1, d), lambda grp, kk, i: (0, 0))
    in_specs = []
    args = []
    for a, w in pairs:
        in_specs += [pl.BlockSpec((tm, tk), lambda grp, kk, i: (t0 + grp * gsz + i, kk)),
                     pl.BlockSpec((tk, d), lambda grp, kk, i: (kk, 0))]
        args += [a, w]
    in_specs += [row_in, vec, row_in, vec]
    args += [h, g, dh_up, dg_init]
    out_specs = [row_out, vec]
    out_shape = [jax.ShapeDtypeStruct((nt * tm, d), F32), jax.ShapeDtypeStruct((1, d), F32)]
    if post is not None:
        in_specs += [row_in, vec]
        args += [post[0], post[1]]
        out_specs += [row_out, vec]
        out_shape += [jax.ShapeDtypeStruct((nt * tm, d), MXU_DTYPE), jax.ShapeDtypeStruct((1, d), F32)]
    return _call(
        body, grid=(nt // gsz, nk, gsz), in_specs=in_specs, out_specs=out_specs, out_shape=out_shape,
        scratch_shapes=[pltpu.VMEM((gsz, tm, d), F32)], name=name, args=tuple(args), carried=carried)


def _mm_tn(a, b, name, carried=()):
    m, ka = a.shape
    d = b.shape[1]
    tf = _tile(ka, 512, LANE)

    def body(a_ref, b_ref, o_ref):
        o_ref[...] = _dot_tn(a_ref[...], b_ref[...]).astype(o_ref.dtype)

    return _call(
        body, grid=(ka // tf,),
        in_specs=[pl.BlockSpec((m, tf), lambda j: (0, j)),
                  pl.BlockSpec((m, d), lambda j: (0, 0), pipeline_mode=pl.Buffered(1))],
        out_specs=pl.BlockSpec((tf, d), lambda j: (j, 0)),
        out_shape=jax.ShapeDtypeStruct((ka, d), WIRE_DTYPE), name=name, args=(a, b), carried=carried)


GELU_K = 0.7978845608028654
GELU_C = 0.044715


def _expm1(x):
    series = x * (1.0 + x * (1.0 / 2 + x * (1.0 / 6 + x * (1.0 / 24 + x * (1.0 / 120)))))
    return jnp.where(jnp.abs(x) < 0.1, series, jnp.exp(x) - 1.0)


def _softplus(x):
    return jnp.maximum(x, 0.0) + jnp.log1p(jnp.exp(-jnp.abs(x)))


def _block_mm(v, w_ref, transposed):
    nbk = w_ref.shape[0]
    outs = []
    for j in range(nbk):
        vj = v[:, j * BD:(j + 1) * BD]
        outs.append(_dot_nt(vj, w_ref[j]) if transposed else jnp.dot(vj, w_ref[j], preferred_element_type=F32))
    return outs[0] if nbk == 1 else jnp.concatenate(outs, axis=1)


def _group_mean(q, gm_ref):
    hi = q.astype(MXU_DTYPE)
    lo = (q - hi.astype(F32)).astype(MXU_DTYPE)
    nbk = q.shape[1] // BD
    gm = gm_ref[...]
    outs = []
    for j in range(nbk):
        sl = slice(j * BD, (j + 1) * BD)
        outs.append(jnp.dot(hi[:, sl], gm, preferred_element_type=F32) + jnp.dot(lo[:, sl], gm, preferred_element_type=F32))
    return outs[0] if nbk == 1 else jnp.concatenate(outs, axis=1)


class _RowReader:
    def __init__(self, ref):
        self.ref = ref

    def __getitem__(self, rows):
        return self.ref[rows, :]


def _shifted(ext_ref, cur, before8, after8, downs=(), ups=()):
    r = cur.shape[0]
    if downs:
        ext_ref[0:8, :] = before8
    ext_ref[8:8 + r, :] = cur
    if ups:
        ext_ref[8 + r:16 + r, :] = after8
    return [ext_ref[pl.ds(8 - j, r), :] for j in downs] + [ext_ref[pl.ds(8 + j, r), :] for j in ups]


def _lru_gates(xc, pv, wa_ref, wx_ref):
    xcb = xc.astype(MXU_DTYPE)
    ga = jax.nn.sigmoid(_block_mm(xcb, wa_ref, False) + pv[5:6])
    gx = jax.nn.sigmoid(_block_mm(xcb, wx_ref, False) + pv[6:7])
    sp = _softplus(-pv[7:8])
    log_a = -LRU_C * ga * sp
    a = jnp.exp(log_a)
    e2 = _expm1(2.0 * log_a)
    mult = jnp.sqrt(-e2)
    return xcb, ga, gx, sp, a, e2, mult


def _gelu_parts(y):
    th = jnp.tanh(GELU_K * (y + GELU_C * y * y * y))
    return 0.5 * y * (1.0 + th), th


def _scan_block(a, u, sa_ref, su_ref, carry_ref, out_ref, reverse):
    r, c = a.shape
    n = r // 8
    a3 = a.reshape(n, 8, c)
    u3 = u.reshape(n, 8, c)
    sub = lax.broadcasted_iota(jnp.int32, (n, 8, c), 1)
    for dlt in (1, 2, 4):
        keep = (sub < 8 - dlt) if reverse else (sub >= dlt)
        shift = 8 - dlt if reverse else dlt
        sh_a = pltpu.roll(a3, shift, axis=1)
        sh_u = pltpu.roll(u3, shift, axis=1)
        u3 = u3 + a3 * jnp.where(keep, sh_u, 0.0)
        a3 = a3 * jnp.where(keep, sh_a, 1.0)
    sa_ref[...] = a3.reshape(r, c)
    su_ref[...] = u3.reshape(r, c)
    for k in (range(n - 1, -1, -1) if reverse else range(n)):
        rows = pl.ds(8 * k, 8)
        out_ref[rows, :] = su_ref[rows, :] + sa_ref[rows, :] * carry_ref[...]
        carry_ref[...] = out_ref[pl.ds(8 * k if reverse else 8 * k + 7, 1), :]


def _mixer_fwd(z, pv, wa, wx, gm, pad, name, carried=()):
    m = z.shape[0]
    c = pv.shape[1]
    r = MIX_ROWS
    nb = m // r

    def body(z_ref, pv_ref, wa_ref, wx_ref, gm_ref, mixed_ref, hs_ref, ext_ref, tailx_ref, tailc_ref, carry_ref,
             sa_ref, su_ref):
        b = pl.program_id(0)

        @pl.when(b == 0)
        def _():
            tailx_ref[...] = jnp.zeros_like(tailx_ref)
            tailc_ref[...] = jnp.zeros_like(tailc_ref)
            carry_ref[...] = jnp.zeros_like(carry_ref)

        pv = _RowReader(pv_ref)
        row = b * r + lax.broadcasted_iota(jnp.int32, (r, 1), 0)
        maskf = (row >= pad).astype(F32)
        y = z_ref[:, 0:c]
        xl = z_ref[:, c:2 * c]
        bs = z_ref[:, 2 * c:3 * c]
        cv = z_ref[:, 3 * c:4 * c] * z_ref[:, 4 * c:5 * c]

        x1, x2, x3 = _shifted(ext_ref, xl, tailx_ref[...], None, downs=(1, 2, 3))
        tailx_ref[...] = z_ref[pl.ds(r - 8, 8), c:2 * c]
        xc = pv[4:5] + pv[3:4] * xl + pv[2:3] * x1 + pv[1:2] * x2 + pv[0:1] * x3
        _, _, gx, _, a, _, mult = _lru_gates(xc, pv, wa_ref, wx_ref)
        uu = mult * (gx * xc) * maskf

        _scan_block(a, uu, sa_ref, su_ref, carry_ref, hs_ref, reverse=False)
        hs = hs_ref[...]

        gelu_y, _ = _gelu_parts(y)
        lru_out = hs * gelu_y
        c1, c2 = _shifted(ext_ref, cv, tailc_ref[...], None, downs=(1, 2))
        tailc_ref[...] = cv[r - 8:r]
        sc_out = bs * (pv[10:11] * cv + pv[9:10] * c1 + pv[8:9] * c2)

        rl = lax.rsqrt(_group_mean(lru_out * lru_out, gm_ref) + EPS)
        rs = lax.rsqrt(_group_mean(sc_out * sc_out, gm_ref) + EPS)
        mixed_ref[:, 0:c] = (lru_out * rl * pv[11:12]).astype(mixed_ref.dtype)
        mixed_ref[:, c:2 * c] = (sc_out * rs * pv[12:13]).astype(mixed_ref.dtype)

    full = lambda shape: pl.BlockSpec(shape, lambda b: (0,) * len(shape))
    return _call(
        body, grid=(nb,),
        in_specs=[pl.BlockSpec((r, 5 * c), lambda b: (b, 0)), full(pv.shape), full(wa.shape), full(wx.shape), full(gm.shape)],
        out_specs=[pl.BlockSpec((r, 2 * c), lambda b: (b, 0)), pl.BlockSpec((r, c), lambda b: (b, 0))],
        out_shape=[jax.ShapeDtypeStruct((m, 2 * c), MXU_DTYPE), jax.ShapeDtypeStruct((m, c), F32)],
        scratch_shapes=[pltpu.VMEM((r + 16, c), F32), pltpu.VMEM((8, c), F32), pltpu.VMEM((8, c), F32),
                        pltpu.VMEM((1, c), F32), pltpu.VMEM((r, c), F32), pltpu.VMEM((r, c), F32)],
        name=name, args=(z, pv, wa, wx, gm), carried=carried)


def _mixer_bwd(z, hs, dmixed, pv, wa, wx, gm, pad, name, carried=()):
    m = z.shape[0]
    c = pv.shape[1]
    r = MIX_ROWS
    nb = m // r
    r8 = r // 8
    assert pad <= r and pad % SUBLANE_BF16 == 0

    def body(z_ref, zp_ref, hs_ref, hsp_ref, dm_ref, pv_ref, wa_ref, wx_ref, gm_ref,
             dz_ref, dpv_ref, dwa_ref, dwx_ref, ext_ref, hxc_ref, hsc_ref, hp_ref, pc_ref, sa_ref, su_ref, p_ref):
        i = pl.program_id(0)
        b = nb - 1 - i

        @pl.when(i == 0)
        def _():
            hxc_ref[...] = jnp.zeros_like(hxc_ref)
            hsc_ref[...] = jnp.zeros_like(hsc_ref)
            hp_ref[...] = jnp.zeros_like(hp_ref)
            pc_ref[...] = jnp.zeros_like(pc_ref)
            dpv_ref[...] = jnp.zeros_like(dpv_ref)
            dwa_ref[...] = jnp.zeros_like(dwa_ref)
            dwx_ref[...] = jnp.zeros_like(dwx_ref)

        pv = _RowReader(pv_ref)
        row = b * r + lax.broadcasted_iota(jnp.int32, (r, 1), 0)
        maskf = (row >= pad).astype(F32)
        has_prev = (b > 0).astype(F32)
        y = z_ref[:, 0:c]
        xl = z_ref[:, c:2 * c]
        bs = z_ref[:, 2 * c:3 * c]
        cs = z_ref[:, 3 * c:4 * c]
        vs = z_ref[:, 4 * c:5 * c]
        cv = cs * vs
        xl_prev = zp_ref[:, c:2 * c] * has_prev
        cv_prev = zp_ref[:, 3 * c:4 * c] * zp_ref[:, 4 * c:5 * c] * has_prev
        hs = hs_ref[...]

        x1, x2, x3 = _shifted(ext_ref, xl, xl_prev, None, downs=(1, 2, 3))
        xc = pv[4:5] + pv[3:4] * xl + pv[2:3] * x1 + pv[1:2] * x2 + pv[0:1] * x3
        xcb, ga, gx, sp, a, e2, mult = _lru_gates(xc, pv, wa_ref, wx_ref)
        gxx = gx * xc
        gelu_y, th = _gelu_parts(y)
        lru_out = hs * gelu_y
        c1, c2 = _shifted(ext_ref, cv, cv_prev, None, downs=(1, 2))
        sc = pv[10:11] * cv + pv[9:10] * c1 + pv[8:9] * c2
        sc_out = bs * sc

        def group_norm_bwd(v, dm, gain):
            rr = lax.rsqrt(_group_mean(v * v, gm_ref) + EPS)
            vh = v * rr
            dvh = dm * gain
            dv = rr * (dvh - vh * _group_mean(dvh * vh, gm_ref))
            return dv, jnp.sum(dm * vh, axis=0, keepdims=True)

        d_lru_out, d_og = group_norm_bwd(lru_out, dm_ref[:, 0:c], pv[11:12])
        d_sc_out, d_sg = group_norm_bwd(sc_out, dm_ref[:, c:2 * c], pv[12:13])
        dpv_ref[11:12, :] += d_og
        dpv_ref[12:13, :] += d_sg

        dhs = d_lru_out * gelu_y
        dgelu = 0.5 * (1.0 + th) + 0.5 * y * (1.0 - th * th) * GELU_K * (1.0 + 3.0 * GELU_C * y * y)
        dy = d_lru_out * hs * dgelu

        _scan_block(a, a * dhs, sa_ref, su_ref, pc_ref, p_ref, reverse=True)
        (p_next,) = _shifted(ext_ref, p_ref[...], None, hp_ref[...], ups=(1,))
        hp_ref[...] = p_ref[0:8, :]
        q = dhs + p_next
        (hs_prev,) = _shifted(ext_ref, hs, hsp_ref[...] * has_prev, None, downs=(1,))
        duu = q * maskf
        da = q * hs_prev

        dmult = duu * gxx
        dgxx = duu * mult
        dgx = dgxx * xc
        dxc = dgxx * gx
        dlog_a = da * a - dmult * ((1.0 + e2) / mult)
        dga = dlog_a * (-LRU_C * sp)
        dsp = jnp.sum(dlog_a * (-LRU_C * ga), axis=0, keepdims=True)
        dpv_ref[7:8, :] += dsp * (-jax.nn.sigmoid(-pv[7:8]))
        dga_pre = dga * ga * (1.0 - ga)
        dgx_pre = dgx * gx * (1.0 - gx)
        dpv_ref[5:6, :] += jnp.sum(dga_pre, axis=0, keepdims=True)
        dpv_ref[6:7, :] += jnp.sum(dgx_pre, axis=0, keepdims=True)
        dga_b = dga_pre.astype(MXU_DTYPE)
        dgx_b = dgx_pre.astype(MXU_DTYPE)
        dxc = dxc + _block_mm(dga_b, wa_ref, True) + _block_mm(dgx_b, wx_ref, True)
        for j in range(c // BD):
            sl = slice(j * BD, (j + 1) * BD)
            dwa_ref[j] += _dot_tn(xcb[:, sl], dga_b[:, sl])
            dwx_ref[j] += _dot_tn(xcb[:, sl], dgx_b[:, sl])

        dpv_ref[4:5, :] += jnp.sum(dxc, axis=0, keepdims=True)
        dpv_ref[3:4, :] += jnp.sum(dxc * xl, axis=0, keepdims=True)
        dpv_ref[2:3, :] += jnp.sum(dxc * x1, axis=0, keepdims=True)
        dpv_ref[1:2, :] += jnp.sum(dxc * x2, axis=0, keepdims=True)
        dpv_ref[0:1, :] += jnp.sum(dxc * x3, axis=0, keepdims=True)
        u1, u2, u3 = _shifted(ext_ref, dxc, None, hxc_ref[...], ups=(1, 2, 3))
        hxc_ref[...] = dxc[0:8]
        dxl = pv[3:4] * dxc + pv[2:3] * u1 + pv[1:2] * u2 + pv[0:1] * u3

        dbs = d_sc_out * sc
        dsc = d_sc_out * bs
        dpv_ref[10:11, :] += jnp.sum(dsc * cv, axis=0, keepdims=True)
        dpv_ref[9:10, :] += jnp.sum(dsc * c1, axis=0, keepdims=True)
        dpv_ref[8:9, :] += jnp.sum(dsc * c2, axis=0, keepdims=True)
        s1, s2 = _shifted(ext_ref, dsc, None, hsc_ref[...], ups=(1, 2))
        hsc_ref[...] = dsc[0:8]
        dcv = pv[10:11] * dsc + pv[9:10] * s1 + pv[8:9] * s2

        dz_ref[:, 0:c] = dy.astype(dz_ref.dtype)
        dz_ref[:, c:2 * c] = dxl.astype(dz_ref.dtype)
        dz_ref[:, 2 * c:3 * c] = dbs.astype(dz_ref.dtype)
        dz_ref[:, 3 * c:4 * c] = (dcv * vs).astype(dz_ref.dtype)
        dz_ref[:, 4 * c:5 * c] = (dcv * cs).astype(dz_ref.dtype)

        if pad:
            @pl.when(b == 0)
            def _():
                dz_ref[0:pad, :] = jnp.zeros((pad, 5 * c), dz_ref.dtype)

    full = lambda shape: pl.BlockSpec(shape, lambda i: (0,) * len(shape))
    cur = lambda width: pl.BlockSpec((r, width), lambda i: (nb - 1 - i, 0))
    prev8 = lambda width: pl.BlockSpec((8, width), lambda i: (jnp.maximum((nb - 1 - i) * r8 - 1, 0), 0))
    return _call(
        body, grid=(nb,),
        in_specs=[cur(5 * c), prev8(5 * c), cur(c), prev8(c), cur(2 * c),
                  full(pv.shape), full(wa.shape), full(wx.shape), full(gm.shape)],
        out_specs=[cur(5 * c), full(pv.shape), full(wa.shape), full(wx.shape)],
        out_shape=[jax.ShapeDtypeStruct((m, 5 * c), MXU_DTYPE), jax.ShapeDtypeStruct(pv.shape, F32),
                   jax.ShapeDtypeStruct(wa.shape, F32), jax.ShapeDtypeStruct(wx.shape, F32)],
        scratch_shapes=[pltpu.VMEM((r + 16, c), F32), pltpu.VMEM((8, c), F32), pltpu.VMEM((8, c), F32),
                        pltpu.VMEM((8, c), F32), pltpu.VMEM((1, c), F32), pltpu.VMEM((r, c), F32),
                        pltpu.VMEM((r, c), F32), pltpu.VMEM((r, c), F32)],
        name=name, args=(z, z, hs, hs, dmixed, pv, wa, wx, gm), carried=carried)


def _position():
    return lax.axis_index("x"), lax.axis_index("y"), lax.axis_index("c")


def _block_of(px, py, pc):
    return 4 * px + 2 * py + pc


class _TwoLevelGather:
    def __init__(self, n_arrays, rows_of, src_of, send_sems, recv_sems):
        x, y, c = _position()
        self.n, self.rows_of, self.src_of = n_arrays, rows_of, src_of
        self.send_sems, self.recv_sems = send_sems, recv_sems
        self.c, self.me, self.sibling = c, (x, y, c), (x, y, 1 - c)
        self.chips = [(1 - x, y), (x, 1 - y), (1 - x, 1 - y)]

    def _copy(self, i, k, block, to, src=None):
        return pltpu.make_async_remote_copy(
            src_ref=self.rows_of(i, *block) if src is None else src, dst_ref=self.rows_of(i, *block),
            send_sem=self.send_sems.at[7 * i + k], recv_sem=self.recv_sems.at[7 * i + k],
            device_id=to, device_id_type=MESH)

    def _first(self, i):
        own = [self._copy(i, 0, self.me, self.sibling, src=self.src_of(i))]
        return own + [self._copy(i, 1 + j, self.me, (*chip, self.c), src=self.src_of(i))
                      for j, chip in enumerate(self.chips)]

    def _passed(self, i, j):
        return self._copy(i, 4 + j, (*self.chips[j], self.c), self.sibling)

    def start(self):
        for i in range(self.n):
            for cp in self._first(i):
                cp.start()

    def forward(self):
        for i in range(self.n):
            for j, chip in enumerate(self.chips):
                self._copy(i, 1 + j, (*chip, self.c), self.me).wait_recv()
                self._passed(i, j).start()

    def drain(self):
        for i in range(self.n):
            self._copy(i, 0, self.sibling, self.me).wait_recv()
            for j, chip in enumerate(self.chips):
                self._copy(i, 4 + j, (*chip, 1 - self.c), self.me).wait_recv()
        for i in range(self.n):
            for cp in self._first(i) + [self._passed(i, j) for j in range(3)]:
                cp.wait_send()


class _RelayGather:
    def __init__(self, n_arrays, rows_of, src_of, send_sems, recv_sems):
        x, y, c = _position()
        self.n, self.rows_of, self.src_of = n_arrays, rows_of, src_of
        self.send_sems, self.recv_sems = send_sems, recv_sems
        self.me, self.sibling = (x, y, c), (x, y, 1 - c)
        self.xn, self.yn, self.dg = (1 - x, y, c), (x, 1 - y, c), (1 - x, 1 - y, c)

    def _copy(self, i, k, block, to, half=None, src=None):
        rows = self.rows_of(i, *block, half)
        return pltpu.make_async_remote_copy(
            src_ref=rows if src is None else src, dst_ref=rows,
            send_sem=self.send_sems.at[8 * i + k], recv_sem=self.recv_sems.at[8 * i + k],
            device_id=to, device_id_type=MESH)

    def _sends(self, i):
        own = self.src_of(i)
        return [self._copy(i, 0, self.me, self.sibling, src=own), self._copy(i, 1, self.me, self.xn, src=own),
                self._copy(i, 2, self.me, self.yn, src=own),
                self._copy(i, 3, self.xn, self.yn, half=0), self._copy(i, 4, self.yn, self.xn, half=1),
                self._copy(i, 5, self.xn, self.sibling), self._copy(i, 6, self.yn, self.sibling),
                self._copy(i, 7, self.dg, self.sibling)]

    def start(self):
        for i in range(self.n):
            for cp in self._sends(i)[0:3]:
                cp.start()

    def forward(self):
        for i in range(self.n):
            self._copy(i, 1, self.xn, self.me).wait_recv()
            self._copy(i, 2, self.yn, self.me).wait_recv()
            for cp in self._sends(i)[3:7]:
                cp.start()

    def drain(self):
        x, y, c = self.me
        for i in range(self.n):
            self._copy(i, 3, self.dg, self.me, half=0).wait_recv()
            self._copy(i, 4, self.dg, self.me, half=1).wait_recv()
            self._sends(i)[7].start()
        for i in range(self.n):
            self._copy(i, 0, self.sibling, self.me).wait_recv()
            self._copy(i, 5, (1 - x, y, 1 - c), self.me).wait_recv()
            self._copy(i, 6, (x, 1 - y, 1 - c), self.me).wait_recv()
            self._copy(i, 7, (1 - x, 1 - y, 1 - c), self.me).wait_recv()
        for i in range(self.n):
            for cp in self._sends(i):
                cp.wait_send()


class _CarriedGather:
    def __init__(self, shards, padded_rows, zeros, forward_at, part=None, into=None):
        d = shards[0].shape[1]
        self.forward_at = forward_at
        self.n = len(shards)
        self.rows = [s.shape[0] for s in shards]
        self.pads = [p - N_DEV * r for r, p in zip(self.rows, padded_rows)]
        assert max(self.pads) <= zeros.shape[0] and zeros.shape[1] == d
        self.part = part if part is not None else (0, self.rows[0])
        assert (part is None and into is None) or self.n == 1
        assert self.part[0] % SUBLANE_BF16 == 0 and self.part[1] % SUBLANE_BF16 == 0
        self.arrays = list(shards) + [zeros] + ([into] if into is not None else [])
        self.out_shapes = [jax.ShapeDtypeStruct((p, d), s.dtype) for s, p in zip(shards, padded_rows)]
        self.aliases = {self.n + 1: 0} if into is not None else {}
        if into is not None:
            self.pads = [0] * self.n
        self.n_remote, self.n_local = 8 * self.n, 2 * self.n
        self.results = None

    def _rows_of(self, outs):
        def rows_of(i, px, py, pc, half):
            first, count = (self.part if self.n == 1 else (0, self.rows[i]))
            head = _round_up(count // 2, SUBLANE_BF16)
            if half == 0:
                count = head
            elif half == 1:
                first, count = first + head, count - head
            first = _block_of(px, py, pc) * self.rows[i] + first
            return outs[i].at[pl.ds(pl.multiple_of(first, SUBLANE_BF16), count), :]
        return rows_of

    def _own(self, ins, i):
        return ins[i].at[pl.ds(self.part[0], self.part[1]), :] if self.n == 1 else ins[i]

    def _gather(self, ins, outs, send_sems, recv_sems):
        return _RelayGather(self.n, self._rows_of(outs), functools.partial(self._own, ins), send_sems, recv_sems)

    def _local(self, ins, outs, local_sems):
        x, y, c = _position()
        rows_of = self._rows_of(outs)
        cps = []
        for i in range(self.n):
            cps.append(pltpu.make_async_copy(self._own(ins, i), rows_of(i, x, y, c, None), local_sems.at[2 * i]))
            if self.pads[i]:
                cps.append(pltpu.make_async_copy(ins[self.n].at[pl.ds(0, self.pads[i]), :],
                                                 outs[i].at[pl.ds(N_DEV * self.rows[i], self.pads[i]), :],
                                                 local_sems.at[2 * i + 1]))
        return cps

    def start(self, ins, outs, send_sems, recv_sems, local_sems):
        for cp in self._local(ins, outs, local_sems):
            cp.start()
        self._gather(ins, outs, send_sems, recv_sems).start()

    def forward(self, ins, outs, send_sems, recv_sems, local_sems):
        self._gather(ins, outs, send_sems, recv_sems).forward()

    def finish(self, ins, outs, send_sems, recv_sems, local_sems):
        self._gather(ins, outs, send_sems, recv_sems).drain()
        for cp in self._local(ins, outs, local_sems):
            cp.wait()


class _CarriedSwap:
    def __init__(self, grads, shard_rows):
        d = grads[0].shape[1]
        self.n, self.rows = len(grads), list(shard_rows)
        self.arrays = list(grads)
        self.out_shapes = [jax.ShapeDtypeStruct((4, s, d), g.dtype) for g, s in zip(grads, shard_rows)]
        self.aliases = {}
        self.n_remote, self.n_local = 4 * self.n, 0
        self.forward_at = 1.0
        self.results = None

    def _copies(self, ins, outs, send_sems, recv_sems):
        x, y, c = _position()
        cps = []
        for i in range(self.n):
            s = self.rows[i]
            for k in range(4):
                blk = _block_of(k >> 1, k & 1, 1 - c)
                cps.append(pltpu.make_async_remote_copy(
                    src_ref=ins[i].at[pl.ds(pl.multiple_of(blk * s, SUBLANE_BF16), s), :], dst_ref=outs[i].at[k],
                    send_sem=send_sems.at[4 * i + k], recv_sem=recv_sems.at[4 * i + k],
                    device_id=(x, y, 1 - c), device_id_type=MESH))
        return cps

    def start(self, ins, outs, send_sems, recv_sems, local_sems):
        for cp in self._copies(ins, outs, send_sems, recv_sems):
            cp.start()

    def forward(self, *_):
        pass

    def finish(self, ins, outs, send_sems, recv_sems, local_sems):
        for cp in self._copies(ins, outs, send_sems, recv_sems):
            cp.wait()


class _CarriedChipExchange:
    def __init__(self, presums, part=None, into=None):
        self.n = len(presums)
        assert (part is None and into is None) or self.n == 1
        self.part = part if part is not None else (0, presums[0].shape[1])
        assert self.part[0] % SUBLANE_BF16 == 0 and self.part[1] % SUBLANE_BF16 == 0
        self.arrays = list(presums) + ([into] if into is not None else [])
        self.out_shapes = [jax.ShapeDtypeStruct(p.shape, p.dtype) for p in presums]
        self.aliases = {self.n: 0} if into is not None else {}
        self.n_remote, self.n_local = 3 * self.n, 0
        self.forward_at = 1.0
        self.results = None

    def _copies(self, ins, outs, send_sems, recv_sems):
        x, y, c = _position()
        cps = []
        for i in range(self.n):
            rows = pl.ds(*self.part) if self.n == 1 else pl.ds(0, self.arrays[i].shape[1])
            for r in range(1, 4):
                cps.append(pltpu.make_async_remote_copy(
                    src_ref=ins[i].at[r - 1, rows, :], dst_ref=outs[i].at[r - 1, rows, :],
                    send_sem=send_sems.at[3 * i + r - 1], recv_sem=recv_sems.at[3 * i + r - 1],
                    device_id=(x ^ (r >> 1), y ^ (r & 1), c), device_id_type=MESH))
        return cps

    def start(self, ins, outs, send_sems, recv_sems, local_sems):
        for cp in self._copies(ins, outs, send_sems, recv_sems):
            cp.start()

    def forward(self, *_):
        pass

    def finish(self, ins, outs, send_sems, recv_sems, local_sems):
        for cp in self._copies(ins, outs, send_sems, recv_sems):
            cp.wait()


def _gather_small(block, reduce, name):
    rr, nn = block.shape

    def body(x_ref, out_ref, *rest):
        if reduce:
            stack_ref, send_sems, recv_sems, local_sem = rest
        else:
            send_sems, recv_sems, local_sem = rest
            stack_ref = out_ref
        x, y, c = _position()

        def rows_of(i, px, py, pc):
            return stack_ref.at[pl.ds(pl.multiple_of(_block_of(px, py, pc) * rr, 8), rr), :]

        own = pltpu.make_async_copy(x_ref, rows_of(0, x, y, c), local_sem)
        own.start()
        gather = _TwoLevelGather(1, rows_of, lambda i: x_ref, send_sems, recv_sems)
        gather.start()
        gather.forward()
        gather.drain()
        own.wait()
        if reduce:
            acc = stack_ref[0:rr, :]
            for k in range(1, N_DEV):
                acc = acc + stack_ref[k * rr:(k + 1) * rr, :]
            out_ref[...] = acc

    vmem = pl.BlockSpec(memory_space=pltpu.VMEM)
    scratch = [pltpu.SemaphoreType.DMA((7,)), pltpu.SemaphoreType.DMA((7,)), pltpu.SemaphoreType.DMA]
    if reduce:
        scratch = [pltpu.VMEM((N_DEV * rr, nn), F32)] + scratch
    out_rows = rr if reduce else N_DEV * rr
    return pl.pallas_call(
        body, in_specs=[vmem], out_specs=vmem, out_shape=jax.ShapeDtypeStruct((out_rows, nn), F32),
        scratch_shapes=scratch, name=name, compiler_params=_params())(block)


def _sum_stack(stack, name):
    rr = stack.shape[0] // N_DEV

    def body(s_ref, o_ref):
        acc = s_ref[0:rr, :]
        for k in range(1, N_DEV):
            acc = acc + s_ref[k * rr:(k + 1) * rr, :]
        o_ref[...] = acc

    vmem = pl.BlockSpec(memory_space=pltpu.VMEM)
    return pl.pallas_call(body, in_specs=[vmem], out_specs=vmem,
                          out_shape=jax.ShapeDtypeStruct((rr, stack.shape[1]), F32), name=name,
                          compiler_params=_params())(stack)


def _presum(where, grad, swapped, name):
    s, d = swapped.shape[1], swapped.shape[2]
    tc = _tile(d, 2048, LANE)

    def body(where_ref, g_ref, sw_ref, o_ref):
        o_ref[0] = (g_ref[...].astype(F32) + sw_ref[0].astype(F32)).astype(o_ref.dtype)

    return _call(
        body, grid=(3, d // tc),
        in_specs=[pl.BlockSpec((s, tc), lambda r, j, where: (where[1 + r], j)),
                  pl.BlockSpec((1, s, tc), lambda r, j, where: (where[5 + r], 0, j))],
        out_specs=pl.BlockSpec((1, s, tc), lambda r, j, where: (r, 0, j)),
        out_shape=jax.ShapeDtypeStruct((3, s, d), WIRE_DTYPE), name=name, args=(grad, swapped), prefetch=(where,))


def _final_sum(where, grad, swapped, received, name, carried=()):
    s, d = swapped.shape[1], swapped.shape[2]
    tc = _tile(d, 512, LANE)

    def body(where_ref, g_ref, sw_ref, r_ref, o_ref):
        acc = g_ref[...].astype(F32) + sw_ref[0].astype(F32)
        for k in range(3):
            acc = acc + r_ref[k].astype(F32)
        o_ref[...] = acc

    return _call(
        body, grid=(d // tc,),
        in_specs=[pl.BlockSpec((s, tc), lambda j, where: (where[0], j)),
                  pl.BlockSpec((1, s, tc), lambda j, where: (where[4], 0, j)),
                  pl.BlockSpec((3, s, tc), lambda j, where: (0, 0, j))],
        out_specs=pl.BlockSpec((s, tc), lambda j, where: (0, j)),
        out_shape=jax.ShapeDtypeStruct((s, d), F32), name=name, args=(grad, swapped, received),
        prefetch=(where,), carried=carried)


class _GradReduction:
    def __init__(self, key, grad, shard_rows, where):
        self.key, self.grad, self.rows, self.where = key, grad, shard_rows, where
        self._presum = self._exchange = None

    def swap(self):
        self._swap = _CarriedSwap([self.grad], [self.rows])
        return self._swap

    def exchange(self, part=None):
        if self._presum is None:
            self._presum = _presum(self.where, self.grad, self._swap.results[0], "presum_" + self.key)
        rows = None
        if part is not None:
            half = _round_up(self.rows // 2, SUBLANE_BF16)
            rows = (0, half) if part == 0 else (half, self.rows - half)
        into = self._exchange.results[0] if part == 1 else None
        self._exchange = _CarriedChipExchange([self._presum], rows, into)
        return self._exchange

    def total(self, carried=()):
        return _final_sum(self.where, self.grad, self._swap.results[0], self._exchange.results[0],
                          "sum_" + self.key, carried)

    def total_and_update(self, w, m, v):
        return _sum_adamw(self.where, self.grad, self._swap.results[0], self._exchange.results[0], w, m, v,
                          "update_" + self.key)


def _adamw_math(w, g, m, v):
    nm = ADAM_B1 * m + (1.0 - ADAM_B1) * g
    nv = ADAM_B2 * v + (1.0 - ADAM_B2) * (g * g)
    m_hat = nm / (1.0 - ADAM_B1 ** ADAM_STEP)
    v_hat = nv / (1.0 - ADAM_B2 ** ADAM_STEP)
    return -ADAM_LR * (m_hat / (jnp.sqrt(v_hat) + ADAM_EPS) + ADAM_WD * w), nm, nv


def _sum_adamw(where, grad, swapped, received, w, m, v, name):
    s, d = swapped.shape[1], swapped.shape[2]
    tc = _tile(d, 512, LANE)

    def body(where_ref, g_ref, sw_ref, r_ref, w_ref, m_ref, v_ref, gs_ref, d_ref, nm_ref, nv_ref):
        g = g_ref[...].astype(F32) + sw_ref[0].astype(F32)
        for k in range(3):
            g = g + r_ref[k].astype(F32)
        gs_ref[...] = g
        d_ref[...], nm_ref[...], nv_ref[...] = _adamw_math(w_ref[...], g, m_ref[...], v_ref[...])

    blk = pl.BlockSpec((s, tc), lambda j, where: (0, j))
    return _call(
        body, grid=(d // tc,),
        in_specs=[pl.BlockSpec((s, tc), lambda j, where: (where[0], j)),
                  pl.BlockSpec((1, s, tc), lambda j, where: (where[4], 0, j)),
                  pl.BlockSpec((3, s, tc), lambda j, where: (0, 0, j)), blk, blk, blk],
        out_specs=[blk] * 4, out_shape=[jax.ShapeDtypeStruct((s, d), F32)] * 4, name=name,
        args=(grad, swapped, received, w, m, v), prefetch=(where,))


def _adamw(w, g, m, v, name):
    rows, cols = w.shape
    tr = _tile(rows, 256, 8)

    def body(w_ref, g_ref, m_ref, v_ref, d_ref, nm_ref, nv_ref):
        d_ref[...], nm_ref[...], nv_ref[...] = _adamw_math(w_ref[...], g_ref[...], m_ref[...], v_ref[...])

    spec = pl.BlockSpec((tr, cols), lambda i: (i, 0))
    return pl.pallas_call(
        body, grid=(rows // tr,), in_specs=[spec] * 4, out_specs=[spec] * 3,
        out_shape=[jax.ShapeDtypeStruct((rows, cols), F32)] * 3, name=name, compiler_params=_params())(w, g, m, v)


def _pack_rows(arrays, width, row_quantum=8):
    flat = jnp.concatenate([a.reshape(-1) for a in arrays])
    total = _round_up(flat.shape[0], row_quantum * width)
    flat = jnp.pad(flat, (0, total - flat.shape[0]))
    return flat.reshape(-1, width)


def _unpack_rows(packed, shapes):
    flat = packed.reshape(-1)
    out = []
    off = 0
    for shp in shapes:
        size = 1
        for s in shp:
            size *= s
        out.append(flat[off:off + size].reshape(shp))
        off += size
    return out


def _block_diag(w):
    h, hb, _ = w.shape
    per = BD // hb
    w4 = w.reshape(h // per, per, hb, hb)
    eye = jnp.eye(per, dtype=w.dtype)
    return jnp.einsum('npij,pq->npiqj', w4, eye).reshape(h // per, BD, BD)


def _block_diag_extract(bd, hb):
    nbk = bd.shape[0]
    per = BD // hb
    b5 = bd.reshape(nbk, per, hb, per, hb)
    eye = jnp.eye(per, dtype=bd.dtype)
    return jnp.einsum('npiqj,pq->npij', b5, eye).reshape(nbk * per, hb, hb)


def kernel(x, meta_tokens, ffn1_pre_g, ffn1_w_gate, ffn1_w_up, ffn1_w_down, ffn1_post_g, mix_pre_g, w_in, lru_conv_w, lru_conv_b, lru_w_a, lru_b_a, lru_w_x, lru_b_x, lru_lambda, sconv_w, lru_out_g, sconv_out_g, w_out, mix_post_g, ffn2_pre_g, ffn2_w_gate, ffn2_w_up, ffn2_w_down, ffn2_post_g, loss_target, m_meta_tokens, m_ffn1_pre_g, m_ffn1_w_gate, m_ffn1_w_up, m_ffn1_w_down, m_ffn1_post_g, m_mix_pre_g, m_w_in, m_lru_conv_w, m_lru_conv_b, m_lru_w_a, m_lru_b_a, m_lru_w_x, m_lru_b_x, m_lru_lambda, m_sconv_w, m_lru_out_g, m_sconv_out_g, m_w_out, m_mix_post_g, m_ffn2_pre_g, m_ffn2_w_gate, m_ffn2_w_up, m_ffn2_w_down, m_ffn2_post_g, v_meta_tokens, v_ffn1_pre_g, v_ffn1_w_gate, v_ffn1_w_up, v_ffn1_w_down, v_ffn1_post_g, v_mix_pre_g, v_w_in, v_lru_conv_w, v_lru_conv_b, v_lru_w_a, v_lru_b_a, v_lru_w_x, v_lru_b_x, v_lru_lambda, v_sconv_w, v_lru_out_g, v_sconv_out_g, v_w_out, v_mix_post_g, v_ffn2_pre_g, v_ffn2_w_gate, v_ffn2_w_up, v_ffn2_w_down, v_ffn2_post_g):
    given = dict(locals())
    wts = {n: given[n] for n in WEIGHT_NAMES}
    mom = {n: given["m_" + n] for n in WEIGHT_NAMES}
    var = {n: given["v_" + n] for n in WEIGHT_NAMES}

    xi, yi, ci = _position()
    me = _block_of(xi, yi, ci)
    x2 = x[0]
    seq, d = x2.shape
    n_meta = meta_tokens.shape[0]
    m_rows = _round_up(n_meta + seq, ROW_ALIGN)
    pad = m_rows - n_meta - seq
    lead = pad + n_meta
    c = lru_conv_b.shape[1]
    hb = lru_w_a.shape[-1]
    dm = meta_tokens.shape[1]
    cs_ = lru_conv_w.shape[2]
    kw4, kw3 = lru_conv_w.shape[1], sconv_w.shape[1]
    assert d == 2 * c and c % BD == 0 and BD % hb == 0 and cs_ <= dm and kw4 == 4 and kw3 == 3

    small = jnp.zeros((_round_up(n_meta + kw4 + kw3, 8), dm), F32)
    small = small.at[0:n_meta].set(meta_tokens)
    small = small.at[n_meta:n_meta + kw4, 0:cs_].set(lru_conv_w[0])
    small = small.at[n_meta + kw4:n_meta + kw4 + kw3, 0:cs_].set(sconv_w[0])
    sr = small.shape[0]
    small_all = _gather_small(small, False, "gather_small").reshape(N_DEV, sr, dm)
    meta_full = small_all[:, 0:n_meta, :].transpose(1, 0, 2).reshape(n_meta, d)
    conv_w_full = small_all[:, n_meta:n_meta + kw4, 0:cs_].transpose(1, 0, 2).reshape(kw4, c)
    sconv_w_full = small_all[:, n_meta + kw4:n_meta + kw4 + kw3, 0:cs_].transpose(1, 0, 2).reshape(kw3, c)

    big = ['ffn1_w_gate', 'ffn1_w_up', 'ffn1_w_down', 'w_in', 'w_out', 'ffn2_w_gate', 'ffn2_w_up', 'ffn2_w_down']
    col_sharded = {'ffn1_w_gate', 'ffn1_w_up', 'w_in', 'ffn2_w_gate', 'ffn2_w_up'}
    shards = []
    for nme in big:
        w = wts[nme][0].astype(WIRE_DTYPE)
        shards.append(w.T if nme in col_sharded else w)
    shard_rows = dict(zip(big, [s.shape[0] for s in shards]))
    zeros = jnp.zeros((F_ALIGN, d), WIRE_DTYPE)

    def gather(forward_at, *names, part=None, into=None):
        sel = [shards[big.index(nme)] for nme in names]
        padded = [_round_up(N_DEV * shard_rows[nme], LANE if nme in ('w_in', 'w_out') else F_ALIGN) for nme in names]
        return _CarriedGather(sel, padded, zeros, forward_at, part, into)

    pv = jnp.zeros((16, c), F32)
    pv = pv.at[0:4].set(conv_w_full).at[4].set(lru_conv_b[0]).at[5].set(lru_b_a[0]).at[6].set(lru_b_x[0])
    pv = pv.at[7].set(lru_lambda[0]).at[8:11].set(sconv_w_full).at[11].set(lru_out_g[0]).at[12].set(sconv_out_g[0])
    wa_bd = _block_diag(lru_w_a[0]).astype(MXU_DTYPE)
    wx_bd = _block_diag(lru_w_x[0]).astype(MXU_DTYPE)
    gs = c // N_GROUPS
    gidx = jnp.arange(BD) // gs
    gm = jnp.where(gidx[:, None] == gidx[None, :], 1.0 / gs, 0.0).astype(MXU_DTYPE)

    ride = gather(0.3, 'ffn1_w_gate')
    h0, n1, target = _embed(x2, meta_full, loss_target[0], ffn1_pre_g, pad, "embed_prenorm", carried=[ride])
    (wg1,) = ride.results
    ride = gather(0.5, 'ffn1_w_up')
    g1 = _mm_nt(n1, wg1, "ffn1_gate", carried=[ride], out_dtype=MXU_DTYPE)
    (wu1,) = ride.results
    ride = gather(0.5, 'ffn1_w_down')
    u1, a1 = _ffn_up_act(n1, wu1, g1, "ffn1_up_act", carried=[ride])
    (wd1,) = ride.results
    ride = gather(0.6, 'w_in', 'w_out')
    fo1, h1, un = _mm_residual_norm(a1, wd1, h0, ffn1_post_g, 0.5, mix_pre_g, "ffn1_down", carried=[ride])
    win_t, wout = ride.results
    s2 = shard_rows['ffn2_w_gate']
    quarter = _round_up(s2 // 4, SUBLANE_BF16)
    ride_g = gather(0.5, 'ffn2_w_gate', part=(0, 3 * quarter))
    z = _mm_nt(un, win_t, "mix_in_proj", carried=[ride_g])
    ride_g = gather(0.5, 'ffn2_w_gate', part=(3 * quarter, s2 - 3 * quarter), into=ride_g.results[0])
    ride_u = gather(0.5, 'ffn2_w_up', part=(0, quarter))
    mixed, hs = _mixer_fwd(z, pv, wa_bd, wx_bd, gm, pad, "mixer_fwd", carried=[ride_g, ride_u])
    (wg2,) = ride_g.results
    ride_u = gather(0.5, 'ffn2_w_up', part=(quarter, s2 - quarter), into=ride_u.results[0])
    o_mix, h2, n2 = _mm_residual_norm(mixed, wout, h1, mix_post_g, 1.0, ffn2_pre_g, "mix_out_proj", carried=[ride_u])
    (wu2,) = ride_u.results
    ride = gather(0.5, 'ffn2_w_down')
    g2, u2, a2 = _ffn_gate_up(n2, wg2, wu2, "ffn2_gate_up", carried=[ride])
    (wd2,) = ride.results
    dh3, dfo2, d_post2, loss_part = _mm_residual_loss(a2, wd2, h2, ffn2_post_g, 0.5, target, lead, "ffn2_down_loss")

    chip_rel = [2 * (xi ^ (r >> 1)) + (yi ^ (r & 1)) for r in range(4)]
    where = jnp.stack([2 * k + ci for k in chip_rel] + chip_rel).astype(jnp.int32)
    red = {}

    def reduction(nme, grad):
        red[nme] = _GradReduction(nme, grad, shard_rows[nme], where)
        return red[nme]

    r_wd2 = reduction('ffn2_w_down', _mm_tn(a2, dfo2, "ffn2_dw_down"))
    dg2, du2 = _ffn_hidden_bwd(dfo2, wd2, g2, u2, "ffn2_hidden_bwd", carried=[r_wd2.swap()])
    r_wg2 = reduction('ffn2_w_gate', _mm_tn(dg2, n2, "ffn2_dw_gate", carried=[r_wd2.exchange(part=0)]))
    r_wu2 = reduction('ffn2_w_up', _mm_tn(du2, n2, "ffn2_dw_up", carried=[r_wd2.exchange(part=1), r_wg2.swap()]))
    dh2, d_pre2 = _mm_norm_bwd([(dg2, wg2), (du2, wu2)], h2, ffn2_pre_g, dh3, "ffn2_dx",
                               carried=[r_wg2.exchange(), r_wu2.swap()])
    do_mix, d_mix_post, dmixed = _norm_bwd_mm_nt(o_mix, mix_post_g, dh2, 1.0, wout, "mix_out_proj_bwd")
    r_wout = reduction('w_out', _mm_tn(mixed, do_mix, "mix_dw_out"))
    dz, dpv, dwa_bd, dwx_bd = _mixer_bwd(z, hs, dmixed, pv, wa_bd, wx_bd, gm, pad, "mixer_bwd",
                                         carried=[r_wu2.exchange(), r_wout.swap()])
    r_win = reduction('w_in', _mm_tn(dz, un, "mix_dw_in", carried=[r_wout.exchange()]))
    dh1, d_mix_pre, dfo1, d_post1 = _mm_norm_bwd([(dz, win_t)], h1, mix_pre_g, dh2, "mix_dx", carried=[r_win.swap()],
                                                 post=(fo1, ffn1_post_g, 0.5))
    r_wd1 = reduction('ffn1_w_down', _mm_tn(a1, dfo1, "ffn1_dw_down", carried=[r_win.exchange(part=0)]))
    early_names = ['mix_pre_g', 'mix_post_g', 'ffn2_pre_g', 'ffn2_post_g', 'ffn1_post_g',
                   'lru_conv_b', 'lru_b_a', 'lru_b_x', 'lru_lambda', 'lru_out_g', 'sconv_out_g',
                   'lru_conv_w', 'sconv_w', 'lru_w_a', 'lru_w_x']
    early_parts = [d_mix_pre, d_mix_post, d_pre2, d_post2, d_post1,
                   dpv[4:5], dpv[5:6], dpv[6:7], dpv[7:8], dpv[11:12], dpv[12:13],
                   dpv[0:4], dpv[8:11], _block_diag_extract(dwa_bd, hb), _block_diag_extract(dwx_bd, hb)]
    early_packed = _pack_rows(early_parts, d, SUBLANE_BF16)
    early_ride = _CarriedGather([early_packed], [N_DEV * early_packed.shape[0]], zeros, 0.75)
    dg1, du1 = _ffn_hidden_bwd(dfo1, wd1, g1, u1, "ffn1_hidden_bwd",
                               carried=[r_win.exchange(part=1), r_wd1.swap(), early_ride])
    early_sum = _sum_stack(early_ride.results[0], "sum_small_early")
    r_wg1 = reduction('ffn1_w_gate', _mm_tn(dg1, n1, "ffn1_dw_gate", carried=[r_wd1.exchange(part=0)]))
    r_wu1 = reduction('ffn1_w_up', _mm_tn(du1, n1, "ffn1_dw_up", carried=[r_wd1.exchange(part=1), r_wg1.swap()]))
    row_tile = _norm_bwd_row_tile(m_rows)
    n_tiles = m_rows // row_tile
    half = n_tiles // 2
    assert half >= 1 and half * row_tile >= lead
    dh0_a, d_pre1_a = _mm_norm_bwd([(dg1, wg1), (du1, wu1)], h0, ffn1_pre_g, dh1, "ffn1_dx_a",
                                   carried=[r_wg1.exchange(), r_wu1.swap()], row_tiles=(0, half))
    dh0_b, d_pre1 = _mm_norm_bwd([(dg1, wg1), (du1, wu1)], h0, ffn1_pre_g, dh1, "ffn1_dx_b",
                                 carried=[r_wu1.exchange()], row_tiles=(half, n_tiles - half), dg_init=d_pre1_a)
    grad_x = jnp.concatenate([dh0_a[lead:], dh0_b], axis=0)[None]
    d_meta = dh0_a[pad:lead]

    grads, delta, new_m, new_v = {}, {}, {}, {}
    for nme in big:
        in_shard_layout = nme not in col_sharded or shard_rows[nme] % LANE != 0
        if in_shard_layout:
            view = (lambda t: t[0].T) if nme in col_sharded else (lambda t: t[0])
            back = (lambda t: t.T[None]) if nme in col_sharded else (lambda t: t[None])
            outs = red[nme].total_and_update(view(wts[nme]), view(mom[nme]), view(var[nme]))
            grads[nme], delta[nme], new_m[nme], new_v[nme] = [back(t) for t in outs]
        else:
            grads[nme] = red[nme].total().T[None]
            outs = _adamw(wts[nme][0], grads[nme][0], mom[nme][0], var[nme][0], "adamw_" + nme)
            delta[nme], new_m[nme], new_v[nme] = [t[None] for t in outs]

    late_names = ['ffn1_pre_g', 'meta_tokens']
    late_parts = [d_pre1, d_meta, loss_part]
    late_sum = _gather_small(_pack_rows(late_parts, d), True, "reduce_small_late")
    small_sums = (_unpack_rows(early_sum, [p.shape for p in early_parts])
                  + _unpack_rows(late_sum, [p.shape for p in late_parts]))
    loss = small_sums.pop()[0, 0]
    for nme, gsm in zip(early_names + late_names, small_sums):
        if nme == 'meta_tokens':
            grads[nme] = lax.dynamic_slice_in_dim(gsm, me * dm, dm, axis=1)
        elif nme in ('lru_conv_w', 'sconv_w'):
            grads[nme] = lax.dynamic_slice_in_dim(gsm, me * cs_, cs_, axis=1)[None]
        else:
            grads[nme] = gsm.reshape(wts[nme].shape)

    rest = [n for n in WEIGHT_NAMES if n not in big]
    rest_shapes = [wts[n].shape for n in rest]
    packed = [_pack_rows([src[n] for n in rest], LANE, 256) for src in (wts, grads, mom, var)]
    for out, packed_out in zip((delta, new_m, new_v), _adamw(*packed, "adamw_small")):
        for nme, arr in zip(rest, _unpack_rows(packed_out, rest_shapes)):
            out[nme] = arr

    return (loss, grad_x, *[grads[n] for n in WEIGHT_NAMES], *[delta[n] for n in WEIGHT_NAMES],
            *[new_m[n] for n in WEIGHT_NAMES], *[new_v[n] for n in WEIGHT_NAMES])
```

```python
import functools

import jax
import jax.numpy as jnp
from jax import lax
from jax.experimental import pallas as pl
from jax.experimental.pallas import tpu as pltpu

F32 = jnp.float32
MXU_DTYPE = jnp.bfloat16
WIRE_DTYPE = jnp.bfloat16
MESH = pl.DeviceIdType.MESH

EPS = 1e-6
LRU_C = 8.0
N_GROUPS = 16
ADAM_LR = 0.001
ADAM_B1 = 0.9
ADAM_B2 = 0.999
ADAM_EPS = 1e-08
ADAM_WD = 0.01
ADAM_STEP = 10

N_DEV = 8
LANE = 128
SUBLANE_BF16 = 16
ROW_ALIGN = 128
F_ALIGN = 512
BD = 256
K_TILE = 512
ACC_ROWS = 528
ACC_GROUP = 1
MIX_ROWS = 128
VMEM_LIMIT_MB = 56

WEIGHT_NAMES = ['meta_tokens', 'ffn1_pre_g', 'ffn1_w_gate', 'ffn1_w_up', 'ffn1_w_down', 'ffn1_post_g',
                'mix_pre_g', 'w_in', 'lru_conv_w', 'lru_conv_b', 'lru_w_a', 'lru_b_a', 'lru_w_x', 'lru_b_x',
                'lru_lambda', 'sconv_w', 'lru_out_g', 'sconv_out_g', 'w_out', 'mix_post_g', 'ffn2_pre_g',
                'ffn2_w_gate', 'ffn2_w_up', 'ffn2_w_down', 'ffn2_post_g']


def _round_up(n, q):
    return (n + q - 1) // q * q


def _tile(n, target, q):
    best = None
    t = q
    while t <= min(n, target):
        if n % t == 0:
            best = t
        t += q
    assert best is not None, (n, target, q)
    return best


def _params(**kw):
    return pltpu.CompilerParams(vmem_limit_bytes=VMEM_LIMIT_MB << 20, **kw)


def _call(body, *, grid, in_specs, out_specs, out_shape, name, args, scratch_shapes=(), carried=(), prefetch=()):
    carried = list(carried)
    n_pf = len(prefetch)

    def launch(fn, in_specs_, out_specs_, out_shape_, scratch_, operands, aliases_):
        if n_pf:
            spec = pltpu.PrefetchScalarGridSpec(num_scalar_prefetch=n_pf, grid=grid, in_specs=in_specs_,
                                                out_specs=out_specs_, scratch_shapes=scratch_)
            return pl.pallas_call(fn, grid_spec=spec, out_shape=out_shape_, input_output_aliases=aliases_,
                                  name=name, compiler_params=_params())(*prefetch, *operands)
        return pl.pallas_call(fn, grid=grid, in_specs=in_specs_, out_specs=out_specs_, out_shape=out_shape_,
                              scratch_shapes=scratch_, input_output_aliases=aliases_, name=name,
                              compiler_params=_params())(*operands)

    if not carried:
        return launch(body, in_specs, out_specs, out_shape, list(scratch_shapes), args, {})
    single = not isinstance(out_shape, (list, tuple))
    out_specs_l = [out_specs] if single else list(out_specs)
    out_shape_l = [out_shape] if single else list(out_shape)
    n_in, n_out, n_scr = len(in_specs), len(out_specs_l), len(scratch_shapes)
    hbm = pl.BlockSpec(memory_space=pl.ANY)
    c_in = [a for cm in carried for a in cm.arrays]
    c_out = [s for cm in carried for s in cm.out_shapes]
    c_scr = []
    aliases = {}
    in_off, out_off = n_pf + n_in, n_out
    for cm in carried:
        c_scr += [pltpu.SemaphoreType.DMA((cm.n_remote,)), pltpu.SemaphoreType.DMA((cm.n_remote,)),
                  pltpu.SemaphoreType.DMA((max(cm.n_local, 1),))]
        for k, v in cm.aliases.items():
            aliases[in_off + k] = out_off + v
        in_off += len(cm.arrays)
        out_off += len(cm.out_shapes)
    steps = 1
    for g in grid:
        steps *= g
    forward_steps = [min(int(cm.forward_at * steps), steps - 1) for cm in carried]

    def wrapped(*refs):
        pf = refs[:n_pf]
        p = n_pf
        ins = refs[p:p + n_in]
        p += n_in
        cins = refs[p:p + len(c_in)]
        p += len(c_in)
        outs = refs[p:p + n_out]
        p += n_out
        couts = refs[p:p + len(c_out)]
        p += len(c_out)
        scr = refs[p:p + n_scr]
        csem = refs[p + n_scr:]
        lin = 0
        for axis, g in enumerate(grid):
            lin = lin * g + pl.program_id(axis)
        views = []
        io = oo = 0
        for j, cm in enumerate(carried):
            views.append((cins[io:io + len(cm.arrays)], couts[oo:oo + len(cm.out_shapes)],
                          csem[3 * j], csem[3 * j + 1], csem[3 * j + 2]))
            io += len(cm.arrays)
            oo += len(cm.out_shapes)

        @pl.when(lin == 0)
        def _():
            for cm, v in zip(carried, views):
                cm.start(*v)

        body(*pf, *ins, *outs, *scr)

        for cm, v, step in zip(carried, views, forward_steps):
            pl.when(lin == step)(functools.partial(cm.forward, *v))

        @pl.when(lin == steps - 1)
        def _():
            for cm, v in zip(carried, views):
                cm.finish(*v)

    res = launch(wrapped, list(in_specs) + [hbm] * len(c_in), out_specs_l + [hbm] * len(c_out),
                 out_shape_l + c_out, list(scratch_shapes) + c_scr, (*args, *c_in), aliases)
    oo = n_out
    for cm in carried:
        cm.results = list(res[oo:oo + len(cm.out_shapes)])
        oo += len(cm.out_shapes)
    return res[0] if single else list(res[:n_out])


def _embed(x, meta, target, g, pad, name, carried=()):
    seq, d = x.shape
    n_meta = meta.shape[0]
    lead = pad + n_meta
    m = lead + seq
    tr = ROW_ALIGN
    lead_blocks = lead // tr
    meta_row = pad - (lead_blocks - 1) * tr
    assert lead % tr == 0 and seq % tr == 0 and 0 <= meta_row and meta_row % 8 == 0

    def body(x_ref, meta_ref, t_ref, g_ref, h_ref, n_ref, tp_ref):
        i = pl.program_id(0)

        @pl.when(i < lead_blocks)
        def _():
            h_ref[...] = jnp.zeros_like(h_ref)
            tp_ref[...] = jnp.zeros_like(tp_ref)

        @pl.when(i == lead_blocks - 1)
        def _():
            h_ref[pl.ds(meta_row, n_meta), :] = meta_ref[...]

        @pl.when(i >= lead_blocks)
        def _():
            h_ref[...] = x_ref[...]
            tp_ref[...] = t_ref[...]

        h = h_ref[...]
        r = lax.rsqrt(jnp.mean(h * h, axis=-1, keepdims=True) + EPS)
        n_ref[...] = (h * r * g_ref[...]).astype(n_ref.dtype)

    tokens = pl.BlockSpec((tr, d), lambda i: (jnp.maximum(i - lead_blocks, 0), 0))
    rows = pl.BlockSpec((tr, d), lambda i: (i, 0))
    return _call(
        body, grid=(m // tr,),
        in_specs=[tokens, pl.BlockSpec((n_meta, d), lambda i: (0, 0)), tokens, pl.BlockSpec((1, d), lambda i: (0, 0))],
        out_specs=[rows, rows, rows],
        out_shape=[jax.ShapeDtypeStruct((m, d), F32), jax.ShapeDtypeStruct((m, d), MXU_DTYPE),
                   jax.ShapeDtypeStruct((m, d), F32)],
        name=name, args=(x, meta, target, g), carried=carried)


def _rmsnorm_bwd_rows(x, g, dy):
    r = lax.rsqrt(jnp.mean(x * x, axis=-1, keepdims=True) + EPS)
    xh = x * r
    dyh = dy * g
    dx = r * (dyh - xh * jnp.mean(dyh * xh, axis=-1, keepdims=True))
    return dx, dy * xh


def _dot_nt(a, b):
    return lax.dot_general(a, b, (((1,), (1,)), ((), ())), preferred_element_type=F32)


def _dot_tn(a, b):
    return lax.dot_general(a, b, (((0,), (0,)), ((), ())), preferred_element_type=F32)


def _mm_nt(a, w, name, carried=(), out_dtype=F32):
    m, k = a.shape
    n = w.shape[0]
    tm = _tile(m, 1056, SUBLANE_BF16)
    tn = _tile(n, 512, LANE)

    def body(a_ref, w_ref, o_ref):
        o_ref[...] = _dot_nt(a_ref[...], w_ref[...]).astype(o_ref.dtype)

    return _call(
        body, grid=(m // tm, n // tn),
        in_specs=[pl.BlockSpec((tm, k), lambda i, j: (i, 0)), pl.BlockSpec((tn, k), lambda i, j: (j, 0))],
        out_specs=pl.BlockSpec((tm, tn), lambda i, j: (i, j)),
        out_shape=jax.ShapeDtypeStruct((m, n), out_dtype), name=name, args=(a, w), carried=carried)


def _norm_bwd_mm_nt(x, g, dy, scale, w, name, carried=()):
    m, d = x.shape
    n = w.shape[0]
    tm = _tile(m, 528, SUBLANE_BF16)

    def body(x_ref, g_ref, dy_ref, w_ref, dx_ref, dg_ref, o_ref):
        @pl.when(pl.program_id(0) == 0)
        def _():
            dg_ref[...] = jnp.zeros_like(dg_ref)

        dx, dgr = _rmsnorm_bwd_rows(x_ref[...], g_ref[...], scale * dy_ref[...])
        dxb = dx.astype(dx_ref.dtype)
        dx_ref[...] = dxb
        dg_ref[...] += jnp.sum(dgr, axis=0, keepdims=True)
        o_ref[...] = _dot_nt(dxb, w_ref[...])

    row = pl.BlockSpec((tm, d), lambda i: (i, 0))
    vec = pl.BlockSpec((1, d), lambda i: (0, 0))
    return _call(
        body, grid=(m // tm,),
        in_specs=[row, vec, row, pl.BlockSpec((n, d), lambda i: (0, 0), pipeline_mode=pl.Buffered(1))],
        out_specs=[row, vec, pl.BlockSpec((tm, n), lambda i: (i, 0))],
        out_shape=[jax.ShapeDtypeStruct((m, d), MXU_DTYPE), jax.ShapeDtypeStruct((1, d), F32),
                   jax.ShapeDtypeStruct((m, n), F32)],
        name=name, args=(x, g, dy, w), carried=carried)


def _ffn_up_act(n_act, wu_t, g_act, name, carried=()):
    m, d = n_act.shape
    fp = wu_t.shape[0]
    tm = _tile(m, 1056, SUBLANE_BF16)
    tn = _tile(fp, 512, LANE)

    def body(n_ref, wu_ref, g_ref, u_ref, a_ref):
        u = _dot_nt(n_ref[...], wu_ref[...])
        g = g_ref[...].astype(F32)
        u_ref[...] = u.astype(u_ref.dtype)
        a_ref[...] = (g * jax.nn.sigmoid(g) * u).astype(a_ref.dtype)

    act = pl.BlockSpec((tm, tn), lambda i, j: (i, j))
    return _call(
        body, grid=(m // tm, fp // tn),
        in_specs=[pl.BlockSpec((tm, d), lambda i, j: (i, 0)), pl.BlockSpec((tn, d), lambda i, j: (j, 0)), act],
        out_specs=[act, act],
        out_shape=[jax.ShapeDtypeStruct((m, fp), MXU_DTYPE)] * 2, name=name, args=(n_act, wu_t, g_act), carried=carried)


def _ffn_gate_up(n_act, wg_t, wu_t, name, carried=()):
    m, d = n_act.shape
    fp = wg_t.shape[0]
    tm = _tile(m, 1056, SUBLANE_BF16)
    tn = _tile(fp, 512, LANE)

    def body(n_ref, wg_ref, wu_ref, g_ref, u_ref, a_ref):
        n = n_ref[...]
        g = _dot_nt(n, wg_ref[...])
        u = _dot_nt(n, wu_ref[...])
        g_ref[...] = g.astype(g_ref.dtype)
        u_ref[...] = u.astype(u_ref.dtype)
        a_ref[...] = (g * jax.nn.sigmoid(g) * u).astype(a_ref.dtype)

    act = pl.BlockSpec((tm, tn), lambda i, j: (i, j))
    wsp = pl.BlockSpec((tn, d), lambda i, j: (j, 0))
    return _call(
        body, grid=(m // tm, fp // tn),
        in_specs=[pl.BlockSpec((tm, d), lambda i, j: (i, 0)), wsp, wsp],
        out_specs=[act, act, act],
        out_shape=[jax.ShapeDtypeStruct((m, fp), MXU_DTYPE)] * 3, name=name, args=(n_act, wg_t, wu_t), carried=carried)


def _ffn_hidden_bwd(dfo, wd, g_act, u_act, name, carried=()):
    m, d = dfo.shape
    fp = wd.shape[0]
    tm = _tile(m, 1056, SUBLANE_BF16)
    tn = _tile(fp, 512, LANE)

    def body(df_ref, wd_ref, g_ref, u_ref, dg_ref, du_ref):
        da = _dot_nt(df_ref[...], wd_ref[...]).astype(dg_ref.dtype)
        g = g_ref[...]
        u = u_ref[...]
        s = jax.nn.sigmoid(g)
        du_ref[...] = da * (g * s)
        dg_ref[...] = da * (u * (s * (1.0 + g * (1.0 - s))))

    act = pl.BlockSpec((tm, tn), lambda i, j: (i, j))
    return _call(
        body, grid=(m // tm, fp // tn),
        in_specs=[pl.BlockSpec((tm, d), lambda i, j: (i, 0)), pl.BlockSpec((tn, d), lambda i, j: (j, 0)), act, act],
        out_specs=[act, act],
        out_shape=[jax.ShapeDtypeStruct((m, fp), MXU_DTYPE)] * 2, name=name, args=(dfo, wd, g_act, u_act),
        carried=carried)


def _row_groups(n_tiles, max_group, nk):
    gsz = max(q for q in range(1, max_group + 1) if n_tiles % q == 0)

    def epilogue_row(grp, kk, i):
        return grp * gsz + jnp.where(kk == nk - 1, i, 0)

    return gsz, epilogue_row


def _mm_residual_norm(a, w, h, g, scale, next_g, name, carried=()):
    m, k = a.shape
    d = w.shape[1]
    tm = _tile(m, ACC_ROWS, SUBLANE_BF16)
    tk = _tile(k, K_TILE, LANE)
    nk = k // tk
    gsz, epilogue_row = _row_groups(m // tm, ACC_GROUP, nk)

    def body(a_ref, w_ref, h_ref, g_ref, ng_ref, fo_ref, hn_ref, nn_ref, acc_ref):
        kk, i = pl.program_id(1), pl.program_id(2)

        @pl.when(kk == 0)
        def _():
            acc_ref[i] = jnp.zeros((tm, d), F32)

        acc_ref[i] += jnp.dot(a_ref[...], w_ref[...], preferred_element_type=F32)

        @pl.when(kk == nk - 1)
        def _():
            fo = acc_ref[i]
            fo_ref[...] = fo
            r = lax.rsqrt(jnp.mean(fo * fo, axis=-1, keepdims=True) + EPS)
            hn = h_ref[...] + scale * (fo * r * g_ref[...])
            hn_ref[...] = hn
            rn = lax.rsqrt(jnp.mean(hn * hn, axis=-1, keepdims=True) + EPS)
            nn_ref[...] = (hn * rn * ng_ref[...]).astype(nn_ref.dtype)

    row = pl.BlockSpec((tm, d), lambda grp, kk, i: (epilogue_row(grp, kk, i), 0))
    vec = pl.BlockSpec((1, d), lambda grp, kk, i: (0, 0))
    return _call(
        body, grid=(m // tm // gsz, nk, gsz),
        in_specs=[pl.BlockSpec((tm, tk), lambda grp, kk, i: (grp * gsz + i, kk)),
                  pl.BlockSpec((tk, d), lambda grp, kk, i: (kk, 0)), row, vec, vec],
        out_specs=[row, row, row],
        out_shape=[jax.ShapeDtypeStruct((m, d), F32)] * 2 + [jax.ShapeDtypeStruct((m, d), MXU_DTYPE)],
        scratch_shapes=[pltpu.VMEM((gsz, tm, d), F32)], name=name, args=(a, w, h, g, next_g), carried=carried)


def _mm_residual_loss(a, w, h, g, scale, target, lead, name, carried=()):
    m, k = a.shape
    d = w.shape[1]
    tm = _tile(m, ACC_ROWS, SUBLANE_BF16)
    tk = _tile(k, K_TILE, LANE)
    nk = k // tk
    gsz, epilogue_row = _row_groups(m // tm, ACC_GROUP, nk)

    def body(a_ref, w_ref, h_ref, g_ref, t_ref, dy_ref, dfo_ref, dg_ref, l_ref, acc_ref):
        grp, kk, i = pl.program_id(0), pl.program_id(1), pl.program_id(2)

        @pl.when(jnp.logical_and(jnp.logical_and(grp == 0, kk == 0), i == 0))
        def _():
            dg_ref[...] = jnp.zeros_like(dg_ref)
            l_ref[...] = jnp.zeros_like(l_ref)

        @pl.when(kk == 0)
        def _():
            acc_ref[i] = jnp.zeros((tm, d), F32)

        acc_ref[i] += jnp.dot(a_ref[...], w_ref[...], preferred_element_type=F32)

        @pl.when(kk == nk - 1)
        def _():
            fo = acc_ref[i]
            gain = g_ref[...]
            r = lax.rsqrt(jnp.mean(fo * fo, axis=-1, keepdims=True) + EPS)
            xh = fo * r
            y = h_ref[...] + scale * (xh * gain)
            row = (grp * gsz + i) * tm + lax.broadcasted_iota(jnp.int32, (tm, 1), 0)
            e = jnp.where(row >= lead, y - t_ref[...], 0.0)
            dy = e * (1.0 / d)
            dy_ref[...] = dy
            l_ref[...] += 0.5 * jnp.sum(jnp.sum(e * e, axis=-1, keepdims=True) * (1.0 / d), axis=0, keepdims=True)
            dn = scale * dy
            dyh = dn * gain
            dfo_ref[...] = (r * (dyh - xh * jnp.mean(dyh * xh, axis=-1, keepdims=True))).astype(dfo_ref.dtype)
            dg_ref[...] += jnp.sum(dn * xh, axis=0, keepdims=True)

    row = pl.BlockSpec((tm, d), lambda grp, kk, i: (epilogue_row(grp, kk, i), 0))
    vec = pl.BlockSpec((1, d), lambda grp, kk, i: (0, 0))
    return _call(
        body, grid=(m // tm // gsz, nk, gsz),
        in_specs=[pl.BlockSpec((tm, tk), lambda grp, kk, i: (grp * gsz + i, kk)),
                  pl.BlockSpec((tk, d), lambda grp, kk, i: (kk, 0)), row, vec, row],
        out_specs=[row, row, vec, pl.BlockSpec((1, 1), lambda grp, kk, i: (0, 0))],
        out_shape=[jax.ShapeDtypeStruct((m, d), F32), jax.ShapeDtypeStruct((m, d), MXU_DTYPE),
                   jax.ShapeDtypeStruct((1, d), F32), jax.ShapeDtypeStruct((1, 1), F32)],
        scratch_shapes=[pltpu.VMEM((gsz, tm, d), F32)], name=name, args=(a, w, h, g, target), carried=carried)


def _norm_bwd_row_tile(m):
    return _tile(m, ACC_ROWS, SUBLANE_BF16)


def _mm_norm_bwd(pairs, h, g, dh_up, name, carried=(), row_tiles=None, dg_init=None, post=None):
    n_pairs = len(pairs)
    m, k = pairs[0][0].shape
    d = h.shape[1]
    tm = _norm_bwd_row_tile(m)
    tk = _tile(k, K_TILE, LANE)
    nk = k // tk
    t0, nt = row_tiles if row_tiles is not None else (0, m // tm)
    gsz, epilogue_row = _row_groups(nt, ACC_GROUP, nk)
    if dg_init is None:
        dg_init = jnp.zeros((1, d), F32)

    n_post = 0 if post is None else 2

    def body(*refs):
        ops = refs[:2 * n_pairs]
        h_ref, g_ref, up_ref, init_ref = refs[2 * n_pairs:2 * n_pairs + 4]
        post_in = refs[2 * n_pairs + 4:2 * n_pairs + 4 + n_post]
        dh_ref, dg_ref = refs[2 * n_pairs + 4 + n_post:2 * n_pairs + 6 + n_post]
        post_out = refs[2 * n_pairs + 6 + n_post:2 * n_pairs + 6 + 2 * n_post]
        acc_ref = refs[-1]
        grp, kk, i = pl.program_id(0), pl.program_id(1), pl.program_id(2)

        @pl.when(jnp.logical_and(jnp.logical_and(grp == 0, kk == 0), i == 0))
        def _():
            dg_ref[...] = init_ref[...]
            if post is not None:
                post_out[1][...] = jnp.zeros_like(post_out[1])

        @pl.when(kk == 0)
        def _():
            acc_ref[i] = jnp.zeros((tm, d), F32)

        for p in range(n_pairs):
            acc_ref[i] += jnp.dot(ops[2 * p][...], ops[2 * p + 1][...], preferred_element_type=F32)

        @pl.when(kk == nk - 1)
        def _():
            dx, dgr = _rmsnorm_bwd_rows(h_ref[...], g_ref[...], acc_ref[i])
            dh = up_ref[...] + dx
            dh_ref[...] = dh
            dg_ref[...] += jnp.sum(dgr, axis=0, keepdims=True)
            if post is not None:
                dfo, dpr = _rmsnorm_bwd_rows(post_in[0][...], post_in[1][...], post[2] * dh)
                post_out[0][...] = dfo.astype(post_out[0].dtype)
                post_out[1][...] += jnp.sum(dpr, axis=0, keepdims=True)

    row_in = pl.BlockSpec((tm, d), lambda grp, kk, i: (t0 + epilogue_row(grp, kk, i), 0))
    row_out = pl.BlockSpec((tm, d), lambda grp, kk, i: (epilogue_row(grp, kk, i), 0))
    vec = pl.BlockSpec((1, d), lambda grp, kk, i: (0, 0))
    in_specs = []
    args = []
    for a, w in pairs:
        in_specs += [pl.BlockSpec((tm, tk), lambda grp, kk, i: (t0 + grp * gsz + i, kk)),
                     pl.BlockSpec((tk, d), lambda grp, kk, i: (kk, 0))]
        args += [a, w]
    in_specs += [row_in, vec, row_in, vec]
    args += [h, g, dh_up, dg_init]
    out_specs = [row_out, vec]
    out_shape = [jax.ShapeDtypeStruct((nt * tm, d), F32), jax.ShapeDtypeStruct((1, d), F32)]
    if post is not None:
        in_specs += [row_in, vec]
        args += [post[0], post[1]]
        out_specs += [row_out, vec]
        out_shape += [jax.ShapeDtypeStruct((nt * tm, d), MXU_DTYPE), jax.ShapeDtypeStruct((1, d), F32)]
    return _call(
        body, grid=(nt // gsz, nk, gsz), in_specs=in_specs, out_specs=out_specs, out_shape=out_shape,
        scratch_shapes=[pltpu.VMEM((gsz, tm, d), F32)], name=name, args=tuple(args), carried=carried)


def _mm_tn(a, b, name, carried=()):
    m, ka = a.shape
    d = b.shape[1]
    tf = _tile(ka, 512, LANE)

    def body(a_ref, b_ref, o_ref):
        o_ref[...] = _dot_tn(a_ref[...], b_ref[...]).astype(o_ref.dtype)

    return _call(
        body, grid=(ka // tf,),
        in_specs=[pl.BlockSpec((m, tf), lambda j: (0, j)),
                  pl.BlockSpec((m, d), lambda j: (0, 0), pipeline_mode=pl.Buffered(1))],
        out_specs=pl.BlockSpec((tf, d), lambda j: (j, 0)),
        out_shape=jax.ShapeDtypeStruct((ka, d), WIRE_DTYPE), name=name, args=(a, b), carried=carried)


GELU_K = 0.7978845608028654
GELU_C = 0.044715


def _expm1(x):
    series = x * (1.0 + x * (1.0 / 2 + x * (1.0 / 6 + x * (1.0 / 24 + x * (1.0 / 120)))))
    return jnp.where(jnp.abs(x) < 0.1, series, jnp.exp(x) - 1.0)


def _softplus(x):
    return jnp.maximum(x, 0.0) + jnp.log1p(jnp.exp(-jnp.abs(x)))


def _block_mm(v, w_ref, transposed):
    nbk = w_ref.shape[0]
    outs = []
    for j in range(nbk):
        vj = v[:, j * BD:(j + 1) * BD]
        outs.append(_dot_nt(vj, w_ref[j]) if transposed else jnp.dot(vj, w_ref[j], preferred_element_type=F32))
    return outs[0] if nbk == 1 else jnp.concatenate(outs, axis=1)


def _group_mean(q, gm_ref):
    hi = q.astype(MXU_DTYPE)
    lo = (q - hi.astype(F32)).astype(MXU_DTYPE)
    nbk = q.shape[1] // BD
    gm = gm_ref[...]
    outs = []
    for j in range(nbk):
        sl = slice(j * BD, (j + 1) * BD)
        outs.append(jnp.dot(hi[:, sl], gm, preferred_element_type=F32) + jnp.dot(lo[:, sl], gm, preferred_element_type=F32))
    return outs[0] if nbk == 1 else jnp.concatenate(outs, axis=1)


class _RowReader:
    def __init__(self, ref):
        self.ref = ref

    def __getitem__(self, rows):
        return self.ref[rows, :]


def _shifted(ext_ref, cur, before8, after8, downs=(), ups=()):
    r = cur.shape[0]
    if downs:
        ext_ref[0:8, :] = before8
    ext_ref[8:8 + r, :] = cur
    if ups:
        ext_ref[8 + r:16 + r, :] = after8
    return [ext_ref[pl.ds(8 - j, r), :] for j in downs] + [ext_ref[pl.ds(8 + j, r), :] for j in ups]


def _lru_gates(xc, pv, wa_ref, wx_ref):
    xcb = xc.astype(MXU_DTYPE)
    ga = jax.nn.sigmoid(_block_mm(xcb, wa_ref, False) + pv[5:6])
    gx = jax.nn.sigmoid(_block_mm(xcb, wx_ref, False) + pv[6:7])
    sp = _softplus(-pv[7:8])
    log_a = -LRU_C * ga * sp
    a = jnp.exp(log_a)
    e2 = _expm1(2.0 * log_a)
    mult = jnp.sqrt(-e2)
    return xcb, ga, gx, sp, a, e2, mult


def _gelu_parts(y):
    th = jnp.tanh(GELU_K * (y + GELU_C * y * y * y))
    return 0.5 * y * (1.0 + th), th


def _scan_block(a, u, sa_ref, su_ref, carry_ref, out_ref, reverse):
    r, c = a.shape
    n = r // 8
    a3 = a.reshape(n, 8, c)
    u3 = u.reshape(n, 8, c)
    sub = lax.broadcasted_iota(jnp.int32, (n, 8, c), 1)
    for dlt in (1, 2, 4):
        keep = (sub < 8 - dlt) if reverse else (sub >= dlt)
        shift = 8 - dlt if reverse else dlt
        sh_a = pltpu.roll(a3, shift, axis=1)
        sh_u = pltpu.roll(u3, shift, axis=1)
        u3 = u3 + a3 * jnp.where(keep, sh_u, 0.0)
        a3 = a3 * jnp.where(keep, sh_a, 1.0)
    sa_ref[...] = a3.reshape(r, c)
    su_ref[...] = u3.reshape(r, c)
    for k in (range(n - 1, -1, -1) if reverse else range(n)):
        rows = pl.ds(8 * k, 8)
        out_ref[rows, :] = su_ref[rows, :] + sa_ref[rows, :] * carry_ref[...]
        carry_ref[...] = out_ref[pl.ds(8 * k if reverse else 8 * k + 7, 1), :]


def _mixer_fwd(z, pv, wa, wx, gm, pad, name, carried=()):
    m = z.shape[0]
    c = pv.shape[1]
    r = MIX_ROWS
    nb = m // r

    def body(z_ref, pv_ref, wa_ref, wx_ref, gm_ref, mixed_ref, hs_ref, ext_ref, tailx_ref, tailc_ref, carry_ref,
             sa_ref, su_ref):
        b = pl.program_id(0)

        @pl.when(b == 0)
        def _():
            tailx_ref[...] = jnp.zeros_like(tailx_ref)
            tailc_ref[...] = jnp.zeros_like(tailc_ref)
            carry_ref[...] = jnp.zeros_like(carry_ref)

        pv = _RowReader(pv_ref)
        row = b * r + lax.broadcasted_iota(jnp.int32, (r, 1), 0)
        maskf = (row >= pad).astype(F32)
        y = z_ref[:, 0:c]
        xl = z_ref[:, c:2 * c]
        bs = z_ref[:, 2 * c:3 * c]
        cv = z_ref[:, 3 * c:4 * c] * z_ref[:, 4 * c:5 * c]

        x1, x2, x3 = _shifted(ext_ref, xl, tailx_ref[...], None, downs=(1, 2, 3))
        tailx_ref[...] = z_ref[pl.ds(r - 8, 8), c:2 * c]
        xc = pv[4:5] + pv[3:4] * xl + pv[2:3] * x1 + pv[1:2] * x2 + pv[0:1] * x3
        _, _, gx, _, a, _, mult = _lru_gates(xc, pv, wa_ref, wx_ref)
        uu = mult * (gx * xc) * maskf

        _scan_block(a, uu, sa_ref, su_ref, carry_ref, hs_ref, reverse=False)
        hs = hs_ref[...]

        gelu_y, _ = _gelu_parts(y)
        lru_out = hs * gelu_y
        c1, c2 = _shifted(ext_ref, cv, tailc_ref[...], None, downs=(1, 2))
        tailc_ref[...] = cv[r - 8:r]
        sc_out = bs * (pv[10:11] * cv + pv[9:10] * c1 + pv[8:9] * c2)

        rl = lax.rsqrt(_group_mean(lru_out * lru_out, gm_ref) + EPS)
        rs = lax.rsqrt(_group_mean(sc_out * sc_out, gm_ref) + EPS)
        mixed_ref[:, 0:c] = (lru_out * rl * pv[11:12]).astype(mixed_ref.dtype)
        mixed_ref[:, c:2 * c] = (sc_out * rs * pv[12:13]).astype(mixed_ref.dtype)

    full = lambda shape: pl.BlockSpec(shape, lambda b: (0,) * len(shape))
    return _call(
        body, grid=(nb,),
        in_specs=[pl.BlockSpec((r, 5 * c), lambda b: (b, 0)), full(pv.shape), full(wa.shape), full(wx.shape), full(gm.shape)],
        out_specs=[pl.BlockSpec((r, 2 * c), lambda b: (b, 0)), pl.BlockSpec((r, c), lambda b: (b, 0))],
        out_shape=[jax.ShapeDtypeStruct((m, 2 * c), MXU_DTYPE), jax.ShapeDtypeStruct((m, c), F32)],
        scratch_shapes=[pltpu.VMEM((r + 16, c), F32), pltpu.VMEM((8, c), F32), pltpu.VMEM((8, c), F32),
                        pltpu.VMEM((1, c), F32), pltpu.VMEM((r, c), F32), pltpu.VMEM((r, c), F32)],
        name=name, args=(z, pv, wa, wx, gm), carried=carried)


def _mixer_bwd(z, hs, dmixed, pv, wa, wx, gm, pad, name, carried=()):
    m = z.shape[0]
    c = pv.shape[1]
    r = MIX_ROWS
    nb = m // r
    r8 = r // 8
    assert pad <= r and pad % SUBLANE_BF16 == 0

    def body(z_ref, zp_ref, hs_ref, hsp_ref, dm_ref, pv_ref, wa_ref, wx_ref, gm_ref,
             dz_ref, dpv_ref, dwa_ref, dwx_ref, ext_ref, hxc_ref, hsc_ref, hp_ref, pc_ref, sa_ref, su_ref, p_ref):
        i = pl.program_id(0)
        b = nb - 1 - i

        @pl.when(i == 0)
        def _():
            hxc_ref[...] = jnp.zeros_like(hxc_ref)
            hsc_ref[...] = jnp.zeros_like(hsc_ref)
            hp_ref[...] = jnp.zeros_like(hp_ref)
            pc_ref[...] = jnp.zeros_like(pc_ref)
            dpv_ref[...] = jnp.zeros_like(dpv_ref)
            dwa_ref[...] = jnp.zeros_like(dwa_ref)
            dwx_ref[...] = jnp.zeros_like(dwx_ref)

        pv = _RowReader(pv_ref)
        row = b * r + lax.broadcasted_iota(jnp.int32, (r, 1), 0)
        maskf = (row >= pad).astype(F32)
        has_prev = (b > 0).astype(F32)
        y = z_ref[:, 0:c]
        xl = z_ref[:, c:2 * c]
        bs = z_ref[:, 2 * c:3 * c]
        cs = z_ref[:, 3 * c:4 * c]
        vs = z_ref[:, 4 * c:5 * c]
        cv = cs * vs
        xl_prev = zp_ref[:, c:2 * c] * has_prev
        cv_prev = zp_ref[:, 3 * c:4 * c] * zp_ref[:, 4 * c:5 * c] * has_prev
        hs = hs_ref[...]

        x1, x2, x3 = _shifted(ext_ref, xl, xl_prev, None, downs=(1, 2, 3))
        xc = pv[4:5] + pv[3:4] * xl + pv[2:3] * x1 + pv[1:2] * x2 + pv[0:1] * x3
        xcb, ga, gx, sp, a, e2, mult = _lru_gates(xc, pv, wa_ref, wx_ref)
        gxx = gx * xc
        gelu_y, th = _gelu_parts(y)
        lru_out = hs * gelu_y
        c1, c2 = _shifted(ext_ref, cv, cv_prev, None, downs=(1, 2))
        sc = pv[10:11] * cv + pv[9:10] * c1 + pv[8:9] * c2
        sc_out = bs * sc

        def group_norm_bwd(v, dm, gain):
            rr = lax.rsqrt(_group_mean(v * v, gm_ref) + EPS)
            vh = v * rr
            dvh = dm * gain
            dv = rr * (dvh - vh * _group_mean(dvh * vh, gm_ref))
            return dv, jnp.sum(dm * vh, axis=0, keepdims=True)

        d_lru_out, d_og = group_norm_bwd(lru_out, dm_ref[:, 0:c], pv[11:12])
        d_sc_out, d_sg = group_norm_bwd(sc_out, dm_ref[:, c:2 * c], pv[12:13])
        dpv_ref[11:12, :] += d_og
        dpv_ref[12:13, :] += d_sg

        dhs = d_lru_out * gelu_y
        dgelu = 0.5 * (1.0 + th) + 0.5 * y * (1.0 - th * th) * GELU_K * (1.0 + 3.0 * GELU_C * y * y)
        dy = d_lru_out * hs * dgelu

        _scan_block(a, a * dhs, sa_ref, su_ref, pc_ref, p_ref, reverse=True)
        (p_next,) = _shifted(ext_ref, p_ref[...], None, hp_ref[...], ups=(1,))
        hp_ref[...] = p_ref[0:8, :]
        q = dhs + p_next
        (hs_prev,) = _shifted(ext_ref, hs, hsp_ref[...] * has_prev, None, downs=(1,))
        duu = q * maskf
        da = q * hs_prev

        dmult = duu * gxx
        dgxx = duu * mult
        dgx = dgxx * xc
        dxc = dgxx * gx
        dlog_a = da * a - dmult * ((1.0 + e2) / mult)
        dga = dlog_a * (-LRU_C * sp)
        dsp = jnp.sum(dlog_a * (-LRU_C * ga), axis=0, keepdims=True)
        dpv_ref[7:8, :] += dsp * (-jax.nn.sigmoid(-pv[7:8]))
        dga_pre = dga * ga * (1.0 - ga)
        dgx_pre = dgx * gx * (1.0 - gx)
        dpv_ref[5:6, :] += jnp.sum(dga_pre, axis=0, keepdims=True)
        dpv_ref[6:7, :] += jnp.sum(dgx_pre, axis=0, keepdims=True)
        dga_b = dga_pre.astype(MXU_DTYPE)
        dgx_b = dgx_pre.astype(MXU_DTYPE)
        dxc = dxc + _block_mm(dga_b, wa_ref, True) + _block_mm(dgx_b, wx_ref, True)
        for j in range(c // BD):
            sl = slice(j * BD, (j + 1) * BD)
            dwa_ref[j] += _dot_tn(xcb[:, sl], dga_b[:, sl])
            dwx_ref[j] += _dot_tn(xcb[:, sl], dgx_b[:, sl])

        dpv_ref[4:5, :] += jnp.sum(dxc, axis=0, keepdims=True)
        dpv_ref[3:4, :] += jnp.sum(dxc * xl, axis=0, keepdims=True)
        dpv_ref[2:3, :] += jnp.sum(dxc * x1, axis=0, keepdims=True)
        dpv_ref[1:2, :] += jnp.sum(dxc * x2, axis=0, keepdims=True)
        dpv_ref[0:1, :] += jnp.sum(dxc * x3, axis=0, keepdims=True)
        u1, u2, u3 = _shifted(ext_ref, dxc, None, hxc_ref[...], ups=(1, 2, 3))
        hxc_ref[...] = dxc[0:8]
        dxl = pv[3:4] * dxc + pv[2:3] * u1 + pv[1:2] * u2 + pv[0:1] * u3

        dbs = d_sc_out * sc
        dsc = d_sc_out * bs
        dpv_ref[10:11, :] += jnp.sum(dsc * cv, axis=0, keepdims=True)
        dpv_ref[9:10, :] += jnp.sum(dsc * c1, axis=0, keepdims=True)
        dpv_ref[8:9, :] += jnp.sum(dsc * c2, axis=0, keepdims=True)
        s1, s2 = _shifted(ext_ref, dsc, None, hsc_ref[...], ups=(1, 2))
        hsc_ref[...] = dsc[0:8]
        dcv = pv[10:11] * dsc + pv[9:10] * s1 + pv[8:9] * s2

        dz_ref[:, 0:c] = dy.astype(dz_ref.dtype)
        dz_ref[:, c:2 * c] = dxl.astype(dz_ref.dtype)
        dz_ref[:, 2 * c:3 * c] = dbs.astype(dz_ref.dtype)
        dz_ref[:, 3 * c:4 * c] = (dcv * vs).astype(dz_ref.dtype)
        dz_ref[:, 4 * c:5 * c] = (dcv * cs).astype(dz_ref.dtype)

        if pad:
            @pl.when(b == 0)
            def _():
                dz_ref[0:pad, :] = jnp.zeros((pad, 5 * c), dz_ref.dtype)

    full = lambda shape: pl.BlockSpec(shape, lambda i: (0,) * len(shape))
    cur = lambda width: pl.BlockSpec((r, width), lambda i: (nb - 1 - i, 0))
    prev8 = lambda width: pl.BlockSpec((8, width), lambda i: (jnp.maximum((nb - 1 - i) * r8 - 1, 0), 0))
    return _call(
        body, grid=(nb,),
        in_specs=[cur(5 * c), prev8(5 * c), cur(c), prev8(c), cur(2 * c),
                  full(pv.shape), full(wa.shape), full(wx.shape), full(gm.shape)],
        out_specs=[cur(5 * c), full(pv.shape), full(wa.shape), full(wx.shape)],
        out_shape=[jax.ShapeDtypeStruct((m, 5 * c), MXU_DTYPE), jax.ShapeDtypeStruct(pv.shape, F32),
                   jax.ShapeDtypeStruct(wa.shape, F32), jax.ShapeDtypeStruct(wx.shape, F32)],
        scratch_shapes=[pltpu.VMEM((r + 16, c), F32), pltpu.VMEM((8, c), F32), pltpu.VMEM((8, c), F32),
                        pltpu.VMEM((8, c), F32), pltpu.VMEM((1, c), F32), pltpu.VMEM((r, c), F32),
                        pltpu.VMEM((r, c), F32), pltpu.VMEM((r, c), F32)],
        name=name, args=(z, z, hs, hs, dmixed, pv, wa, wx, gm), carried=carried)


def _position():
    return lax.axis_index("x"), lax.axis_index("y"), lax.axis_index("c")


def _block_of(px, py, pc):
    return 4 * px + 2 * py + pc


class _TwoLevelGather:
    def __init__(self, n_arrays, rows_of, src_of, send_sems, recv_sems):
        x, y, c = _position()
        self.n, self.rows_of, self.src_of = n_arrays, rows_of, src_of
        self.send_sems, self.recv_sems = send_sems, recv_sems
        self.c, self.me, self.sibling = c, (x, y, c), (x, y, 1 - c)
        self.chips = [(1 - x, y), (x, 1 - y), (1 - x, 1 - y)]

    def _copy(self, i, k, block, to, src=None):
        return pltpu.make_async_remote_copy(
            src_ref=self.rows_of(i, *block) if src is None else src, dst_ref=self.rows_of(i, *block),
            send_sem=self.send_sems.at[7 * i + k], recv_sem=self.recv_sems.at[7 * i + k],
            device_id=to, device_id_type=MESH)

    def _first(self, i):
        own = [self._copy(i, 0, self.me, self.sibling, src=self.src_of(i))]
        return own + [self._copy(i, 1 + j, self.me, (*chip, self.c), src=self.src_of(i))
                      for j, chip in enumerate(self.chips)]

    def _passed(self, i, j):
        return self._copy(i, 4 + j, (*self.chips[j], self.c), self.sibling)

    def start(self):
        for i in range(self.n):
            for cp in self._first(i):
                cp.start()

    def forward(self):
        for i in range(self.n):
            for j, chip in enumerate(self.chips):
                self._copy(i, 1 + j, (*chip, self.c), self.me).wait_recv()
                self._passed(i, j).start()

    def drain(self):
        for i in range(self.n):
            self._copy(i, 0, self.sibling, self.me).wait_recv()
            for j, chip in enumerate(self.chips):
                self._copy(i, 4 + j, (*chip, 1 - self.c), self.me).wait_recv()
        for i in range(self.n):
            for cp in self._first(i) + [self._passed(i, j) for j in range(3)]:
                cp.wait_send()


class _RelayGather:
    def __init__(self, n_arrays, rows_of, src_of, send_sems, recv_sems):
        x, y, c = _position()
        self.n, self.rows_of, self.src_of = n_arrays, rows_of, src_of
        self.send_sems, self.recv_sems = send_sems, recv_sems
        self.me, self.sibling = (x, y, c), (x, y, 1 - c)
        self.xn, self.yn, self.dg = (1 - x, y, c), (x, 1 - y, c), (1 - x, 1 - y, c)

    def _copy(self, i, k, block, to, half=None, src=None):
        rows = self.rows_of(i, *block, half)
        return pltpu.make_async_remote_copy(
            src_ref=rows if src is None else src, dst_ref=rows,
            send_sem=self.send_sems.at[8 * i + k], recv_sem=self.recv_sems.at[8 * i + k],
            device_id=to, device_id_type=MESH)

    def _sends(self, i):
        own = self.src_of(i)
        return [self._copy(i, 0, self.me, self.sibling, src=own), self._copy(i, 1, self.me, self.xn, src=own),
                self._copy(i, 2, self.me, self.yn, src=own),
                self._copy(i, 3, self.xn, self.yn, half=0), self._copy(i, 4, self.yn, self.xn, half=1),
                self._copy(i, 5, self.xn, self.sibling), self._copy(i, 6, self.yn, self.sibling),
                self._copy(i, 7, self.dg, self.sibling)]

    def start(self):
        for i in range(self.n):
            for cp in self._sends(i)[0:3]:
                cp.start()

    def forward(self):
        for i in range(self.n):
            self._copy(i, 1, self.xn, self.me).wait_recv()
            self._copy(i, 2, self.yn, self.me).wait_recv()
            for cp in self._sends(i)[3:7]:
                cp.start()

    def drain(self):
        x, y, c = self.me
        for i in range(self.n):
            self._copy(i, 3, self.dg, self.me, half=0).wait_recv()
            self._copy(i, 4, self.dg, self.me, half=1).wait_recv()
            self._sends(i)[7].start()
        for i in range(self.n):
            self._copy(i, 0, self.sibling, self.me).wait_recv()
            self._copy(i, 5, (1 - x, y, 1 - c), self.me).wait_recv()
            self._copy(i, 6, (x, 1 - y, 1 - c), self.me).wait_recv()
            self._copy(i, 7, (1 - x, 1 - y, 1 - c), self.me).wait_recv()
        for i in range(self.n):
            for cp in self._sends(i):
                cp.wait_send()


class _CarriedGather:
    def __init__(self, shards, padded_rows, zeros, forward_at, part=None, into=None):
        d = shards[0].shape[1]
        self.forward_at = forward_at
        self.n = len(shards)
        self.rows = [s.shape[0] for s in shards]
        self.pads = [p - N_DEV * r for r, p in zip(self.rows, padded_rows)]
        assert max(self.pads) <= zeros.shape[0] and zeros.shape[1] == d
        self.part = part if part is not None else (0, self.rows[0])
        assert (part is None and into is None) or self.n == 1
        assert self.part[0] % SUBLANE_BF16 == 0 and self.part[1] % SUBLANE_BF16 == 0
        self.arrays = list(shards) + [zeros] + ([into] if into is not None else [])
        self.out_shapes = [jax.ShapeDtypeStruct((p, d), s.dtype) for s, p in zip(shards, padded_rows)]
        self.aliases = {self.n + 1: 0} if into is not None else {}
        if into is not None:
            self.pads = [0] * self.n
        self.n_remote, self.n_local = 8 * self.n, 2 * self.n
        self.results = None

    def _rows_of(self, outs):
        def rows_of(i, px, py, pc, half):
            first, count = (self.part if self.n == 1 else (0, self.rows[i]))
            head = _round_up(count // 2, SUBLANE_BF16)
            if half == 0:
                count = head
            elif half == 1:
                first, count = first + head, count - head
            first = _block_of(px, py, pc) * self.rows[i] + first
            return outs[i].at[pl.ds(pl.multiple_of(first, SUBLANE_BF16), count), :]
        return rows_of

    def _own(self, ins, i):
        return ins[i].at[pl.ds(self.part[0], self.part[1]), :] if self.n == 1 else ins[i]

    def _gather(self, ins, outs, send_sems, recv_sems):
        return _RelayGather(self.n, self._rows_of(outs), functools.partial(self._own, ins), send_sems, recv_sems)

    def _local(self, ins, outs, local_sems):
        x, y, c = _position()
        rows_of = self._rows_of(outs)
        cps = []
        for i in range(self.n):
            cps.append(pltpu.make_async_copy(self._own(ins, i), rows_of(i, x, y, c, None), local_sems.at[2 * i]))
            if self.pads[i]:
                cps.append(pltpu.make_async_copy(ins[self.n].at[pl.ds(0, self.pads[i]), :],
                                                 outs[i].at[pl.ds(N_DEV * self.rows[i], self.pads[i]), :],
                                                 local_sems.at[2 * i + 1]))
        return cps

    def start(self, ins, outs, send_sems, recv_sems, local_sems):
        for cp in self._local(ins, outs, local_sems):
            cp.start()
        self._gather(ins, outs, send_sems, recv_sems).start()

    def forward(self, ins, outs, send_sems, recv_sems, local_sems):
        self._gather(ins, outs, send_sems, recv_sems).forward()

    def finish(self, ins, outs, send_sems, recv_sems, local_sems):
        self._gather(ins, outs, send_sems, recv_sems).drain()
        for cp in self._local(ins, outs, local_sems):
            cp.wait()


class _CarriedSwap:
    def __init__(self, grads, shard_rows):
        d = grads[0].shape[1]
        self.n, self.rows = len(grads), list(shard_rows)
        self.arrays = list(grads)
        self.out_shapes = [jax.ShapeDtypeStruct((4, s, d), g.dtype) for g, s in zip(grads, shard_rows)]
        self.aliases = {}
        self.n_remote, self.n_local = 4 * self.n, 0
        self.forward_at = 1.0
        self.results = None

    def _copies(self, ins, outs, send_sems, recv_sems):
        x, y, c = _position()
        cps = []
        for i in range(self.n):
            s = self.rows[i]
            for k in range(4):
                blk = _block_of(k >> 1, k & 1, 1 - c)
                cps.append(pltpu.make_async_remote_copy(
                    src_ref=ins[i].at[pl.ds(pl.multiple_of(blk * s, SUBLANE_BF16), s), :], dst_ref=outs[i].at[k],
                    send_sem=send_sems.at[4 * i + k], recv_sem=recv_sems.at[4 * i + k],
                    device_id=(x, y, 1 - c), device_id_type=MESH))
        return cps

    def start(self, ins, outs, send_sems, recv_sems, local_sems):
        for cp in self._copies(ins, outs, send_sems, recv_sems):
            cp.start()

    def forward(self, *_):
        pass

    def finish(self, ins, outs, send_sems, recv_sems, local_sems):
        for cp in self._copies(ins, outs, send_sems, recv_sems):
            cp.wait()


class _CarriedChipExchange:
    def __init__(self, presums, part=None, into=None):
        self.n = len(presums)
        assert (part is None and into is None) or self.n == 1
        self.part = part if part is not None else (0, presums[0].shape[1])
        assert self.part[0] % SUBLANE_BF16 == 0 and self.part[1] % SUBLANE_BF16 == 0
        self.arrays = list(presums) + ([into] if into is not None else [])
        self.out_shapes = [jax.ShapeDtypeStruct(p.shape, p.dtype) for p in presums]
        self.aliases = {self.n: 0} if into is not None else {}
        self.n_remote, self.n_local = 3 * self.n, 0
        self.forward_at = 1.0
        self.results = None

    def _copies(self, ins, outs, send_sems, recv_sems):
        x, y, c = _position()
        cps = []
        for i in range(self.n):
            rows = pl.ds(*self.part) if self.n == 1 else pl.ds(0, self.arrays[i].shape[1])
            for r in range(1, 4):
                cps.append(pltpu.make_async_remote_copy(
                    src_ref=ins[i].at[r - 1, rows, :], dst_ref=outs[i].at[r - 1, rows, :],
                    send_sem=send_sems.at[3 * i + r - 1], recv_sem=recv_sems.at[3 * i + r - 1],
                    device_id=(x ^ (r >> 1), y ^ (r & 1), c), device_id_type=MESH))
        return cps

    def start(self, ins, outs, send_sems, recv_sems, local_sems):
        for cp in self._copies(ins, outs, send_sems, recv_sems):
            cp.start()

    def forward(self, *_):
        pass

    def finish(self, ins, outs, send_sems, recv_sems, local_sems):
        for cp in self._copies(ins, outs, send_sems, recv_sems):
            cp.wait()


def _gather_small(block, reduce, name):
    rr, nn = block.shape

    def body(x_ref, out_ref, *rest):
        if reduce:
            stack_ref, send_sems, recv_sems, local_sem = rest
        else:
            send_sems, recv_sems, local_sem = rest
            stack_ref = out_ref
        x, y, c = _position()

        def rows_of(i, px, py, pc):
            return stack_ref.at[pl.ds(pl.multiple_of(_block_of(px, py, pc) * rr, 8), rr), :]

        own = pltpu.make_async_copy(x_ref, rows_of(0, x, y, c), local_sem)
        own.start()
        gather = _TwoLevelGather(1, rows_of, lambda i: x_ref, send_sems, recv_sems)
        gather.start()
        gather.forward()
        gather.drain()
        own.wait()
        if reduce:
            acc = stack_ref[0:rr, :]
            for k in range(1, N_DEV):
                acc = acc + stack_ref[k * rr:(k + 1) * rr, :]
            out_ref[...] = acc

    vmem = pl.BlockSpec(memory_space=pltpu.VMEM)
    scratch = [pltpu.SemaphoreType.DMA((7,)), pltpu.SemaphoreType.DMA((7,)), pltpu.SemaphoreType.DMA]
    if reduce:
        scratch = [pltpu.VMEM((N_DEV * rr, nn), F32)] + scratch
    out_rows = rr if reduce else N_DEV * rr
    return pl.pallas_call(
        body, in_specs=[vmem], out_specs=vmem, out_shape=jax.ShapeDtypeStruct((out_rows, nn), F32),
        scratch_shapes=scratch, name=name, compiler_params=_params())(block)


def _sum_stack(stack, name):
    rr = stack.shape[0] // N_DEV

    def body(s_ref, o_ref):
        acc = s_ref[0:rr, :]
        for k in range(1, N_DEV):
            acc = acc + s_ref[k * rr:(k + 1) * rr, :]
        o_ref[...] = acc

    vmem = pl.BlockSpec(memory_space=pltpu.VMEM)
    return pl.pallas_call(body, in_specs=[vmem], out_specs=vmem,
                          out_shape=jax.ShapeDtypeStruct((rr, stack.shape[1]), F32), name=name,
                          compiler_params=_params())(stack)


def _presum(where, grad, swapped, name):
    s, d = swapped.shape[1], swapped.shape[2]
    tc = _tile(d, 2048, LANE)

    def body(where_ref, g_ref, sw_ref, o_ref):
        o_ref[0] = (g_ref[...].astype(F32) + sw_ref[0].astype(F32)).astype(o_ref.dtype)

    return _call(
        body, grid=(3, d // tc),
        in_specs=[pl.BlockSpec((s, tc), lambda r, j, where: (where[1 + r], j)),
                  pl.BlockSpec((1, s, tc), lambda r, j, where: (where[5 + r], 0, j))],
        out_specs=pl.BlockSpec((1, s, tc), lambda r, j, where: (r, 0, j)),
        out_shape=jax.ShapeDtypeStruct((3, s, d), WIRE_DTYPE), name=name, args=(grad, swapped), prefetch=(where,))


def _final_sum(where, grad, swapped, received, name, carried=()):
    s, d = swapped.shape[1], swapped.shape[2]
    tc = _tile(d, 512, LANE)

    def body(where_ref, g_ref, sw_ref, r_ref, o_ref):
        acc = g_ref[...].astype(F32) + sw_ref[0].astype(F32)
        for k in range(3):
            acc = acc + r_ref[k].astype(F32)
        o_ref[...] = acc

    return _call(
        body, grid=(d // tc,),
        in_specs=[pl.BlockSpec((s, tc), lambda j, where: (where[0], j)),
                  pl.BlockSpec((1, s, tc), lambda j, where: (where[4], 0, j)),
                  pl.BlockSpec((3, s, tc), lambda j, where: (0, 0, j))],
        out_specs=pl.BlockSpec((s, tc), lambda j, where: (0, j)),
        out_shape=jax.ShapeDtypeStruct((s, d), F32), name=name, args=(grad, swapped, received),
        prefetch=(where,), carried=carried)


class _GradReduction:
    def __init__(self, key, grad, shard_rows, where):
        self.key, self.grad, self.rows, self.where = key, grad, shard_rows, where
        self._presum = self._exchange = None

    def swap(self):
        self._swap = _CarriedSwap([self.grad], [self.rows])
        return self._swap

    def exchange(self, part=None):
        if self._presum is None:
            self._presum = _presum(self.where, self.grad, self._swap.results[0], "presum_" + self.key)
        rows = None
        if part is not None:
            half = _round_up(self.rows // 2, SUBLANE_BF16)
            rows = (0, half) if part == 0 else (half, self.rows - half)
        into = self._exchange.results[0] if part == 1 else None
        self._exchange = _CarriedChipExchange([self._presum], rows, into)
        return self._exchange

    def total(self, carried=()):
        return _final_sum(self.where, self.grad, self._swap.results[0], self._exchange.results[0],
                          "sum_" + self.key, carried)

    def total_and_update(self, w, m, v):
        return _sum_adamw(self.where, self.grad, self._swap.results[0], self._exchange.results[0], w, m, v,
                          "update_" + self.key)


def _adamw_math(w, g, m, v):
    nm = ADAM_B1 * m + (1.0 - ADAM_B1) * g
    nv = ADAM_B2 * v + (1.0 - ADAM_B2) * (g * g)
    m_hat = nm / (1.0 - ADAM_B1 ** ADAM_STEP)
    v_hat = nv / (1.0 - ADAM_B2 ** ADAM_STEP)
    return -ADAM_LR * (m_hat / (jnp.sqrt(v_hat) + ADAM_EPS) + ADAM_WD * w), nm, nv


def _sum_adamw(where, grad, swapped, received, w, m, v, name):
    s, d = swapped.shape[1], swapped.shape[2]
    tc = _tile(d, 512, LANE)

    def body(where_ref, g_ref, sw_ref, r_ref, w_ref, m_ref, v_ref, gs_ref, d_ref, nm_ref, nv_ref):
        g = g_ref[...].astype(F32) + sw_ref[0].astype(F32)
        for k in range(3):
            g = g + r_ref[k].astype(F32)
        gs_ref[...] = g
        d_ref[...], nm_ref[...], nv_ref[...] = _adamw_math(w_ref[...], g, m_ref[...], v_ref[...])

    blk = pl.BlockSpec((s, tc), lambda j, where: (0, j))
    return _call(
        body, grid=(d // tc,),
        in_specs=[pl.BlockSpec((s, tc), lambda j, where: (where[0], j)),
                  pl.BlockSpec((1, s, tc), lambda j, where: (where[4], 0, j)),
                  pl.BlockSpec((3, s, tc), lambda j, where: (0, 0, j)), blk, blk, blk],
        out_specs=[blk] * 4, out_shape=[jax.ShapeDtypeStruct((s, d), F32)] * 4, name=name,
        args=(grad, swapped, received, w, m, v), prefetch=(where,))


def _adamw(w, g, m, v, name):
    rows, cols = w.shape
    tr = _tile(rows, 256, 8)

    def body(w_ref, g_ref, m_ref, v_ref, d_ref, nm_ref, nv_ref):
        d_ref[...], nm_ref[...], nv_ref[...] = _adamw_math(w_ref[...], g_ref[...], m_ref[...], v_ref[...])

    spec = pl.BlockSpec((tr, cols), lambda i: (i, 0))
    return pl.pallas_call(
        body, grid=(rows // tr,), in_specs=[spec] * 4, out_specs=[spec] * 3,
        out_shape=[jax.ShapeDtypeStruct((rows, cols), F32)] * 3, name=name, compiler_params=_params())(w, g, m, v)


def _pack_rows(arrays, width, row_quantum=8):
    flat = jnp.concatenate([a.reshape(-1) for a in arrays])
    total = _round_up(flat.shape[0], row_quantum * width)
    flat = jnp.pad(flat, (0, total - flat.shape[0]))
    return flat.reshape(-1, width)


def _unpack_rows(packed, shapes):
    flat = packed.reshape(-1)
    out = []
    off = 0
    for shp in shapes:
        size = 1
        for s in shp:
            size *= s
        out.append(flat[off:off + size].reshape(shp))
        off += size
    return out


def _block_diag(w):
    h, hb, _ = w.shape
    per = BD // hb
    w4 = w.reshape(h // per, per, hb, hb)
    eye = jnp.eye(per, dtype=w.dtype)
    return jnp.einsum('npij,pq->npiqj', w4, eye).reshape(h // per, BD, BD)


def _block_diag_extract(bd, hb):
    nbk = bd.shape[0]
    per = BD // hb
    b5 = bd.reshape(nbk, per, hb, per, hb)
    eye = jnp.eye(per, dtype=bd.dtype)
    return jnp.einsum('npiqj,pq->npij', b5, eye).reshape(nbk * per, hb, hb)


def kernel(x, meta_tokens, ffn1_pre_g, ffn1_w_gate, ffn1_w_up, ffn1_w_down, ffn1_post_g, mix_pre_g, w_in, lru_conv_w, lru_conv_b, lru_w_a, lru_b_a, lru_w_x, lru_b_x, lru_lambda, sconv_w, lru_out_g, sconv_out_g, w_out, mix_post_g, ffn2_pre_g, ffn2_w_gate, ffn2_w_up, ffn2_w_down, ffn2_post_g, loss_target, m_meta_tokens, m_ffn1_pre_g, m_ffn1_w_gate, m_ffn1_w_up, m_ffn1_w_down, m_ffn1_post_g, m_mix_pre_g, m_w_in, m_lru_conv_w, m_lru_conv_b, m_lru_w_a, m_lru_b_a, m_lru_w_x, m_lru_b_x, m_lru_lambda, m_sconv_w, m_lru_out_g, m_sconv_out_g, m_w_out, m_mix_post_g, m_ffn2_pre_g, m_ffn2_w_gate, m_ffn2_w_up, m_ffn2_w_down, m_ffn2_post_g, v_meta_tokens, v_ffn1_pre_g, v_ffn1_w_gate, v_ffn1_w_up, v_ffn1_w_down, v_ffn1_post_g, v_mix_pre_g, v_w_in, v_lru_conv_w, v_lru_conv_b, v_lru_w_a, v_lru_b_a, v_lru_w_x, v_lru_b_x, v_lru_lambda, v_sconv_w, v_lru_out_g, v_sconv_out_g, v_w_out, v_mix_post_g, v_ffn2_pre_g, v_ffn2_w_gate, v_ffn2_w_up, v_ffn2_w_down, v_ffn2_post_g):
    given = dict(locals())
    wts = {n: given[n] for n in WEIGHT_NAMES}
    mom = {n: given["m_" + n] for n in WEIGHT_NAMES}
    var = {n: given["v_" + n] for n in WEIGHT_NAMES}

    xi, yi, ci = _position()
    me = _block_of(xi, yi, ci)
    x2 = x[0]
    seq, d = x2.shape
    n_meta = meta_tokens.shape[0]
    m_rows = _round_up(n_meta + seq, ROW_ALIGN)
    pad = m_rows - n_meta - seq
    lead = pad + n_meta
    c = lru_conv_b.shape[1]
    hb = lru_w_a.shape[-1]
    dm = meta_tokens.shape[1]
    cs_ = lru_conv_w.shape[2]
    kw4, kw3 = lru_conv_w.shape[1], sconv_w.shape[1]
    assert d == 2 * c and c % BD == 0 and BD % hb == 0 and cs_ <= dm and kw4 == 4 and kw3 == 3

    small = jnp.zeros((_round_up(n_meta + kw4 + kw3, 8), dm), F32)
    small = small.at[0:n_meta].set(meta_tokens)
    small = small.at[n_meta:n_meta + kw4, 0:cs_].set(lru_conv_w[0])
    small = small.at[n_meta + kw4:n_meta + kw4 + kw3, 0:cs_].set(sconv_w[0])
    sr = small.shape[0]
    small_all = _gather_small(small, False, "gather_small").reshape(N_DEV, sr, dm)
    meta_full = small_all[:, 0:n_meta, :].transpose(1, 0, 2).reshape(n_meta, d)
    conv_w_full = small_all[:, n_meta:n_meta + kw4, 0:cs_].transpose(1, 0, 2).reshape(kw4, c)
    sconv_w_full = small_all[:, n_meta + kw4:n_meta + kw4 + kw3, 0:cs_].transpose(1, 0, 2).reshape(kw3, c)

    big = ['ffn1_w_gate', 'ffn1_w_up', 'ffn1_w_down', 'w_in', 'w_out', 'ffn2_w_gate', 'ffn2_w_up', 'ffn2_w_down']
    col_sharded = {'ffn1_w_gate', 'ffn1_w_up', 'w_in', 'ffn2_w_gate', 'ffn2_w_up'}
    shards = []
    for nme in big:
        w = wts[nme][0].astype(WIRE_DTYPE)
        shards.append(w.T if nme in col_sharded else w)
    shard_rows = dict(zip(big, [s.shape[0] for s in shards]))
    zeros = jnp.zeros((F_ALIGN, d), WIRE_DTYPE)

    def gather(forward_at, *names, part=None, into=None):
        sel = [shards[big.index(nme)] for nme in names]
        padded = [_round_up(N_DEV * shard_rows[nme], LANE if nme in ('w_in', 'w_out') else F_ALIGN) for nme in names]
        return _CarriedGather(sel, padded, zeros, forward_at, part, into)

    pv = jnp.zeros((16, c), F32)
    pv = pv.at[0:4].set(conv_w_full).at[4].set(lru_conv_b[0]).at[5].set(lru_b_a[0]).at[6].set(lru_b_x[0])
    pv = pv.at[7].set(lru_lambda[0]).at[8:11].set(sconv_w_full).at[11].set(lru_out_g[0]).at[12].set(sconv_out_g[0])
    wa_bd = _block_diag(lru_w_a[0]).astype(MXU_DTYPE)
    wx_bd = _block_diag(lru_w_x[0]).astype(MXU_DTYPE)
    gs = c // N_GROUPS
    gidx = jnp.arange(BD) // gs
    gm = jnp.where(gidx[:, None] == gidx[None, :], 1.0 / gs, 0.0).astype(MXU_DTYPE)

    ride = gather(0.3, 'ffn1_w_gate')
    h0, n1, target = _embed(x2, meta_full, loss_target[0], ffn1_pre_g, pad, "embed_prenorm", carried=[ride])
    (wg1,) = ride.results
    ride = gather(0.6, 'ffn1_w_up')
    g1 = _mm_nt(n1, wg1, "ffn1_gate", carried=[ride], out_dtype=MXU_DTYPE)
    (wu1,) = ride.results
    ride = gather(0.5, 'ffn1_w_down')
    u1, a1 = _ffn_up_act(n1, wu1, g1, "ffn1_up_act", carried=[ride])
    (wd1,) = ride.results
    ride = gather(0.6, 'w_in', 'w_out')
    fo1, h1, un = _mm_residual_norm(a1, wd1, h0, ffn1_post_g, 0.5, mix_pre_g, "ffn1_down", carried=[ride])
    win_t, wout = ride.results
    s2 = shard_rows['ffn2_w_gate']
    quarter = _round_up(s2 // 4, SUBLANE_BF16)
    ride_g = gather(0.4, 'ffn2_w_gate', part=(0, 3 * quarter))
    z = _mm_nt(un, win_t, "mix_in_proj", carried=[ride_g])
    ride_g = gather(0.5, 'ffn2_w_gate', part=(3 * quarter, s2 - 3 * quarter), into=ride_g.results[0])
    ride_u = gather(0.5, 'ffn2_w_up', part=(0, quarter))
    mixed, hs = _mixer_fwd(z, pv, wa_bd, wx_bd, gm, pad, "mixer_fwd", carried=[ride_g, ride_u])
    (wg2,) = ride_g.results
    ride_u = gather(0.4, 'ffn2_w_up', part=(quarter, s2 - quarter), into=ride_u.results[0])
    o_mix, h2, n2 = _mm_residual_norm(mixed, wout, h1, mix_post_g, 1.0, ffn2_pre_g, "mix_out_proj", carried=[ride_u])
    (wu2,) = ride_u.results
    ride = gather(0.75, 'ffn2_w_down')
    g2, u2, a2 = _ffn_gate_up(n2, wg2, wu2, "ffn2_gate_up", carried=[ride])
    (wd2,) = ride.results
    dh3, dfo2, d_post2, loss_part = _mm_residual_loss(a2, wd2, h2, ffn2_post_g, 0.5, target, lead, "ffn2_down_loss")

    chip_rel = [2 * (xi ^ (r >> 1)) + (yi ^ (r & 1)) for r in range(4)]
    where = jnp.stack([2 * k + ci for k in chip_rel] + chip_rel).astype(jnp.int32)
    red = {}

    def reduction(nme, grad):
        red[nme] = _GradReduction(nme, grad, shard_rows[nme], where)
        return red[nme]

    r_wd2 = reduction('ffn2_w_down', _mm_tn(a2, dfo2, "ffn2_dw_down"))
    dg2, du2 = _ffn_hidden_bwd(dfo2, wd2, g2, u2, "ffn2_hidden_bwd", carried=[r_wd2.swap()])
    r_wg2 = reduction('ffn2_w_gate', _mm_tn(dg2, n2, "ffn2_dw_gate", carried=[r_wd2.exchange(part=0)]))
    r_wu2 = reduction('ffn2_w_up', _mm_tn(du2, n2, "ffn2_dw_up", carried=[r_wd2.exchange(part=1), r_wg2.swap()]))
    dh2, d_pre2 = _mm_norm_bwd([(dg2, wg2), (du2, wu2)], h2, ffn2_pre_g, dh3, "ffn2_dx",
                               carried=[r_wg2.exchange(), r_wu2.swap()])
    do_mix, d_mix_post, dmixed = _norm_bwd_mm_nt(o_mix, mix_post_g, dh2, 1.0, wout, "mix_out_proj_bwd")
    r_wout = reduction('w_out', _mm_tn(mixed, do_mix, "mix_dw_out"))
    dz, dpv, dwa_bd, dwx_bd = _mixer_bwd(z, hs, dmixed, pv, wa_bd, wx_bd, gm, pad, "mixer_bwd",
                                         carried=[r_wu2.exchange(), r_wout.swap()])
    r_win = reduction('w_in', _mm_tn(dz, un, "mix_dw_in", carried=[r_wout.exchange()]))
    dh1, d_mix_pre, dfo1, d_post1 = _mm_norm_bwd([(dz, win_t)], h1, mix_pre_g, dh2, "mix_dx", carried=[r_win.swap()],
                                                 post=(fo1, ffn1_post_g, 0.5))
    r_wd1 = reduction('ffn1_w_down', _mm_tn(a1, dfo1, "ffn1_dw_down", carried=[r_win.exchange(part=0)]))
    early_names = ['mix_pre_g', 'mix_post_g', 'ffn2_pre_g', 'ffn2_post_g', 'ffn1_post_g',
                   'lru_conv_b', 'lru_b_a', 'lru_b_x', 'lru_lambda', 'lru_out_g', 'sconv_out_g',
                   'lru_conv_w', 'sconv_w', 'lru_w_a', 'lru_w_x']
    early_parts = [d_mix_pre, d_mix_post, d_pre2, d_post2, d_post1,
                   dpv[4:5], dpv[5:6], dpv[6:7], dpv[7:8], dpv[11:12], dpv[12:13],
                   dpv[0:4], dpv[8:11], _block_diag_extract(dwa_bd, hb), _block_diag_extract(dwx_bd, hb)]
    early_packed = _pack_rows(early_parts, d, SUBLANE_BF16)
    early_ride = _CarriedGather([early_packed], [N_DEV * early_packed.shape[0]], zeros, 0.75)
    dg1, du1 = _ffn_hidden_bwd(dfo1, wd1, g1, u1, "ffn1_hidden_bwd",
                               carried=[r_win.exchange(part=1), r_wd1.swap(), early_ride])
    early_sum = _sum_stack(early_ride.results[0], "sum_small_early")
    r_wg1 = reduction('ffn1_w_gate', _mm_tn(dg1, n1, "ffn1_dw_gate", carried=[r_wd1.exchange(part=0)]))
    r_wu1 = reduction('ffn1_w_up', _mm_tn(du1, n1, "ffn1_dw_up", carried=[r_wd1.exchange(part=1), r_wg1.swap()]))
    row_tile = _norm_bwd_row_tile(m_rows)
    n_tiles = m_rows // row_tile
    half = n_tiles // 2
    assert half >= 1 and half * row_tile >= lead
    dh0_a, d_pre1_a = _mm_norm_bwd([(dg1, wg1), (du1, wu1)], h0, ffn1_pre_g, dh1, "ffn1_dx_a",
                                   carried=[r_wg1.exchange(), r_wu1.swap()], row_tiles=(0, half))
    dh0_b, d_pre1 = _mm_norm_bwd([(dg1, wg1), (du1, wu1)], h0, ffn1_pre_g, dh1, "ffn1_dx_b",
                                 carried=[r_wu1.exchange()], row_tiles=(half, n_tiles - half), dg_init=d_pre1_a)
    grad_x = jnp.concatenate([dh0_a[lead:], dh0_b], axis=0)[None]
    d_meta = dh0_a[pad:lead]

    grads, delta, new_m, new_v = {}, {}, {}, {}
    for nme in big:
        in_shard_layout = nme not in col_sharded or shard_rows[nme] % LANE != 0
        if in_shard_layout:
            view = (lambda t: t[0].T) if nme in col_sharded else (lambda t: t[0])
            back = (lambda t: t.T[None]) if nme in col_sharded else (lambda t: t[None])
            outs = red[nme].total_and_update(view(wts[nme]), view(mom[nme]), view(var[nme]))
            grads[nme], delta[nme], new_m[nme], new_v[nme] = [back(t) for t in outs]
        else:
            grads[nme] = red[nme].total().T[None]
            outs = _adamw(wts[nme][0], grads[nme][0], mom[nme][0], var[nme][0], "adamw_" + nme)
            delta[nme], new_m[nme], new_v[nme] = [t[None] for t in outs]

    late_names = ['ffn1_pre_g', 'meta_tokens']
    late_parts = [d_pre1, d_meta, loss_part]
    late_sum = _gather_small(_pack_rows(late_parts, d), True, "reduce_small_late")
    small_sums = (_unpack_rows(early_sum, [p.shape for p in early_parts])
                  + _unpack_rows(late_sum, [p.shape for p in late_parts]))
    loss = small_sums.pop()[0, 0]
    for nme, gsm in zip(early_names + late_names, small_sums):
        if nme == 'meta_tokens':
            grads[nme] = lax.dynamic_slice_in_dim(gsm, me * dm, dm, axis=1)
        elif nme in ('lru_conv_w', 'sconv_w'):
            grads[nme] = lax.dynamic_slice_in_dim(gsm, me * cs_, cs_, axis=1)[None]
        else:
            grads[nme] = gsm.reshape(wts[nme].shape)

    rest = [n for n in WEIGHT_NAMES if n not in big]
    rest_shapes = [wts[n].shape for n in rest]
    packed = [_pack_rows([src[n] for n in rest], LANE, 256) for src in (wts, grads, mom, var)]
    for out, packed_out in zip((delta, new_m, new_v), _adamw(*packed, "adamw_small")):
        for nme, arr in zip(rest, _unpack_rows(packed_out, rest_shapes)):
            out[nme] = arr

    return (loss, grad_x, *[grads[n] for n in WEIGHT_NAMES], *[delta[n] for n in WEIGHT_NAMES],
            *[new_m[n] for n in WEIGHT_NAMES], *[new_v[n] for n in WEIGHT_NAMES])
```

```python
import functools

import jax
import jax.numpy as jnp
from jax import lax
from jax.experimental import pallas as pl
from jax.experimental.pallas import tpu as pltpu

F32 = jnp.float32
MXU_DTYPE = jnp.bfloat16
WIRE_DTYPE = jnp.bfloat16
MESH = pl.DeviceIdType.MESH

EPS = 1e-6
LRU_C = 8.0
N_GROUPS = 16
ADAM_LR = 0.001
ADAM_B1 = 0.9
ADAM_B2 = 0.999
ADAM_EPS = 1e-08
ADAM_WD = 0.01
ADAM_STEP = 10

N_DEV = 8
LANE = 128
SUBLANE_BF16 = 16
ROW_ALIGN = 128
F_ALIGN = 512
BD = 256
K_TILE = 512
ACC_ROWS = 528
ACC_GROUP = 1
MIX_ROWS = 192
VMEM_LIMIT_MB = 56

WEIGHT_NAMES = ['meta_tokens', 'ffn1_pre_g', 'ffn1_w_gate', 'ffn1_w_up', 'ffn1_w_down', 'ffn1_post_g',
                'mix_pre_g', 'w_in', 'lru_conv_w', 'lru_conv_b', 'lru_w_a', 'lru_b_a', 'lru_w_x', 'lru_b_x',
                'lru_lambda', 'sconv_w', 'lru_out_g', 'sconv_out_g', 'w_out', 'mix_post_g', 'ffn2_pre_g',
                'ffn2_w_gate', 'ffn2_w_up', 'ffn2_w_down', 'ffn2_post_g']


def _round_up(n, q):
    return (n + q - 1) // q * q


def _tile(n, target, q):
    best = None
    t = q
    while t <= min(n, target):
        if n % t == 0:
            best = t
        t += q
    assert best is not None, (n, target, q)
    return best


def _params(**kw):
    return pltpu.CompilerParams(vmem_limit_bytes=VMEM_LIMIT_MB << 20, **kw)


def _call(body, *, grid, in_specs, out_specs, out_shape, name, args, scratch_shapes=(), carried=(), prefetch=()):
    carried = list(carried)
    n_pf = len(prefetch)

    def launch(fn, in_specs_, out_specs_, out_shape_, scratch_, operands, aliases_):
        if n_pf:
            spec = pltpu.PrefetchScalarGridSpec(num_scalar_prefetch=n_pf, grid=grid, in_specs=in_specs_,
                                                out_specs=out_specs_, scratch_shapes=scratch_)
            return pl.pallas_call(fn, grid_spec=spec, out_shape=out_shape_, input_output_aliases=aliases_,
                                  name=name, compiler_params=_params())(*prefetch, *operands)
        return pl.pallas_call(fn, grid=grid, in_specs=in_specs_, out_specs=out_specs_, out_shape=out_shape_,
                              scratch_shapes=scratch_, input_output_aliases=aliases_, name=name,
                              compiler_params=_params())(*operands)

    if not carried:
        return launch(body, in_specs, out_specs, out_shape, list(scratch_shapes), args, {})
    single = not isinstance(out_shape, (list, tuple))
    out_specs_l = [out_specs] if single else list(out_specs)
    out_shape_l = [out_shape] if single else list(out_shape)
    n_in, n_out, n_scr = len(in_specs), len(out_specs_l), len(scratch_shapes)
    hbm = pl.BlockSpec(memory_space=pl.ANY)
    c_in = [a for cm in carried for a in cm.arrays]
    c_out = [s for cm in carried for s in cm.out_shapes]
    c_scr = []
    aliases = {}
    in_off, out_off = n_pf + n_in, n_out
    for cm in carried:
        c_scr += [pltpu.SemaphoreType.DMA((cm.n_remote,)), pltpu.SemaphoreType.DMA((cm.n_remote,)),
                  pltpu.SemaphoreType.DMA((max(cm.n_local, 1),))]
        for k, v in cm.aliases.items():
            aliases[in_off + k] = out_off + v
        in_off += len(cm.arrays)
        out_off += len(cm.out_shapes)
    steps = 1
    for g in grid:
        steps *= g
    forward_steps = [min(int(cm.forward_at * steps), steps - 1) for cm in carried]

    def wrapped(*refs):
        pf = refs[:n_pf]
        p = n_pf
        ins = refs[p:p + n_in]
        p += n_in
        cins = refs[p:p + len(c_in)]
        p += len(c_in)
        outs = refs[p:p + n_out]
        p += n_out
        couts = refs[p:p + len(c_out)]
        p += len(c_out)
        scr = refs[p:p + n_scr]
        csem = refs[p + n_scr:]
        lin = 0
        for axis, g in enumerate(grid):
            lin = lin * g + pl.program_id(axis)
        views = []
        io = oo = 0
        for j, cm in enumerate(carried):
            views.append((cins[io:io + len(cm.arrays)], couts[oo:oo + len(cm.out_shapes)],
                          csem[3 * j], csem[3 * j + 1], csem[3 * j + 2]))
            io += len(cm.arrays)
            oo += len(cm.out_shapes)

        @pl.when(lin == 0)
        def _():
            for cm, v in zip(carried, views):
                cm.start(*v)

        body(*pf, *ins, *outs, *scr)

        for cm, v, step in zip(carried, views, forward_steps):
            pl.when(lin == step)(functools.partial(cm.forward, *v))

        @pl.when(lin == steps - 1)
        def _():
            for cm, v in zip(carried, views):
                cm.finish(*v)

    res = launch(wrapped, list(in_specs) + [hbm] * len(c_in), out_specs_l + [hbm] * len(c_out),
                 out_shape_l + c_out, list(scratch_shapes) + c_scr, (*args, *c_in), aliases)
    oo = n_out
    for cm in carried:
        cm.results = list(res[oo:oo + len(cm.out_shapes)])
        oo += len(cm.out_shapes)
    return res[0] if single else list(res[:n_out])


def _embed(x, meta, target, g, pad, name, carried=()):
    seq, d = x.shape
    n_meta = meta.shape[0]
    lead = pad + n_meta
    m = lead + seq
    tr = ROW_ALIGN
    lead_blocks = lead // tr
    meta_row = pad - (lead_blocks - 1) * tr
    assert lead % tr == 0 and seq % tr == 0 and 0 <= meta_row and meta_row % 8 == 0

    def body(x_ref, meta_ref, t_ref, g_ref, h_ref, n_ref, tp_ref):
        i = pl.program_id(0)

        @pl.when(i < lead_blocks)
        def _():
            h_ref[...] = jnp.zeros_like(h_ref)
            tp_ref[...] = jnp.zeros_like(tp_ref)

        @pl.when(i == lead_blocks - 1)
        def _():
            h_ref[pl.ds(meta_row, n_meta), :] = meta_ref[...]

        @pl.when(i >= lead_blocks)
        def _():
            h_ref[...] = x_ref[...]
            tp_ref[...] = t_ref[...]

        h = h_ref[...]
        r = lax.rsqrt(jnp.mean(h * h, axis=-1, keepdims=True) + EPS)
        n_ref[...] = (h * r * g_ref[...]).astype(n_ref.dtype)

    tokens = pl.BlockSpec((tr, d), lambda i: (jnp.maximum(i - lead_blocks, 0), 0))
    rows = pl.BlockSpec((tr, d), lambda i: (i, 0))
    return _call(
        body, grid=(m // tr,),
        in_specs=[tokens, pl.BlockSpec((n_meta, d), lambda i: (0, 0)), tokens, pl.BlockSpec((1, d), lambda i: (0, 0))],
        out_specs=[rows, rows, rows],
        out_shape=[jax.ShapeDtypeStruct((m, d), F32), jax.ShapeDtypeStruct((m, d), MXU_DTYPE),
                   jax.ShapeDtypeStruct((m, d), F32)],
        name=name, args=(x, meta, target, g), carried=carried)


def _rmsnorm_bwd_rows(x, g, dy):
    r = lax.rsqrt(jnp.mean(x * x, axis=-1, keepdims=True) + EPS)
    xh = x * r
    dyh = dy * g
    dx = r * (dyh - xh * jnp.mean(dyh * xh, axis=-1, keepdims=True))
    return dx, dy * xh


def _dot_nt(a, b):
    return lax.dot_general(a, b, (((1,), (1,)), ((), ())), preferred_element_type=F32)


def _dot_tn(a, b):
    return lax.dot_general(a, b, (((0,), (0,)), ((), ())), preferred_element_type=F32)


def _mm_nt(a, w, name, carried=(), out_dtype=F32):
    m, k = a.shape
    n = w.shape[0]
    tm = _tile(m, 1056, SUBLANE_BF16)
    tn = _tile(n, 512, LANE)

    def body(a_ref, w_ref, o_ref):
        o_ref[...] = _dot_nt(a_ref[...], w_ref[...]).astype(o_ref.dtype)

    return _call(
        body, grid=(m // tm, n // tn),
        in_specs=[pl.BlockSpec((tm, k), lambda i, j: (i, 0)), pl.BlockSpec((tn, k), lambda i, j: (j, 0))],
        out_specs=pl.BlockSpec((tm, tn), lambda i, j: (i, j)),
        out_shape=jax.ShapeDtypeStruct((m, n), out_dtype), name=name, args=(a, w), carried=carried)


def _norm_bwd_mm_nt(x, g, dy, scale, w, name, carried=()):
    m, d = x.shape
    n = w.shape[0]
    tm = _tile(m, 528, SUBLANE_BF16)

    def body(x_ref, g_ref, dy_ref, w_ref, dx_ref, dg_ref, o_ref):
        @pl.when(pl.program_id(0) == 0)
        def _():
            dg_ref[...] = jnp.zeros_like(dg_ref)

        dx, dgr = _rmsnorm_bwd_rows(x_ref[...], g_ref[...], scale * dy_ref[...])
        dxb = dx.astype(dx_ref.dtype)
        dx_ref[...] = dxb
        dg_ref[...] += jnp.sum(dgr, axis=0, keepdims=True)
        o_ref[...] = _dot_nt(dxb, w_ref[...])

    row = pl.BlockSpec((tm, d), lambda i: (i, 0))
    vec = pl.BlockSpec((1, d), lambda i: (0, 0))
    return _call(
        body, grid=(m // tm,),
        in_specs=[row, vec, row, pl.BlockSpec((n, d), lambda i: (0, 0), pipeline_mode=pl.Buffered(1))],
        out_specs=[row, vec, pl.BlockSpec((tm, n), lambda i: (i, 0))],
        out_shape=[jax.ShapeDtypeStruct((m, d), MXU_DTYPE), jax.ShapeDtypeStruct((1, d), F32),
                   jax.ShapeDtypeStruct((m, n), F32)],
        name=name, args=(x, g, dy, w), carried=carried)


def _ffn_up_act(n_act, wu_t, g_act, name, carried=()):
    m, d = n_act.shape
    fp = wu_t.shape[0]
    tm = _tile(m, 1056, SUBLANE_BF16)
    tn = _tile(fp, 512, LANE)

    def body(n_ref, wu_ref, g_ref, u_ref, a_ref):
        u = _dot_nt(n_ref[...], wu_ref[...])
        g = g_ref[...].astype(F32)
        u_ref[...] = u.astype(u_ref.dtype)
        a_ref[...] = (g * jax.nn.sigmoid(g) * u).astype(a_ref.dtype)

    act = pl.BlockSpec((tm, tn), lambda i, j: (i, j))
    return _call(
        body, grid=(m // tm, fp // tn),
        in_specs=[pl.BlockSpec((tm, d), lambda i, j: (i, 0)), pl.BlockSpec((tn, d), lambda i, j: (j, 0)), act],
        out_specs=[act, act],
        out_shape=[jax.ShapeDtypeStruct((m, fp), MXU_DTYPE)] * 2, name=name, args=(n_act, wu_t, g_act), carried=carried)


def _ffn_gate_up(n_act, wg_t, wu_t, name, carried=()):
    m, d = n_act.shape
    fp = wg_t.shape[0]
    tm = _tile(m, 1056, SUBLANE_BF16)
    tn = _tile(fp, 512, LANE)

    def body(n_ref, wg_ref, wu_ref, g_ref, u_ref, a_ref):
        n = n_ref[...]
        g = _dot_nt(n, wg_ref[...])
        u = _dot_nt(n, wu_ref[...])
        g_ref[...] = g.astype(g_ref.dtype)
        u_ref[...] = u.astype(u_ref.dtype)
        a_ref[...] = (g * jax.nn.sigmoid(g) * u).astype(a_ref.dtype)

    act = pl.BlockSpec((tm, tn), lambda i, j: (i, j))
    wsp = pl.BlockSpec((tn, d), lambda i, j: (j, 0))
    return _call(
        body, grid=(m // tm, fp // tn),
        in_specs=[pl.BlockSpec((tm, d), lambda i, j: (i, 0)), wsp, wsp],
        out_specs=[act, act, act],
        out_shape=[jax.ShapeDtypeStruct((m, fp), MXU_DTYPE)] * 3, name=name, args=(n_act, wg_t, wu_t), carried=carried)


def _ffn_hidden_bwd(dfo, wd, g_act, u_act, name, carried=()):
    m, d = dfo.shape
    fp = wd.shape[0]
    tm = _tile(m, 1056, SUBLANE_BF16)
    tn = _tile(fp, 512, LANE)

    def body(df_ref, wd_ref, g_ref, u_ref, dg_ref, du_ref):
        da = _dot_nt(df_ref[...], wd_ref[...]).astype(dg_ref.dtype)
        g = g_ref[...]
        u = u_ref[...]
        s = jax.nn.sigmoid(g)
        du_ref[...] = da * (g * s)
        dg_ref[...] = da * (u * (s * (1.0 + g * (1.0 - s))))

    act = pl.BlockSpec((tm, tn), lambda i, j: (i, j))
    return _call(
        body, grid=(m // tm, fp // tn),
        in_specs=[pl.BlockSpec((tm, d), lambda i, j: (i, 0)), pl.BlockSpec((tn, d), lambda i, j: (j, 0)), act, act],
        out_specs=[act, act],
        out_shape=[jax.ShapeDtypeStruct((m, fp), MXU_DTYPE)] * 2, name=name, args=(dfo, wd, g_act, u_act),
        carried=carried)


def _row_groups(n_tiles, max_group, nk):
    gsz = max(q for q in range(1, max_group + 1) if n_tiles % q == 0)

    def epilogue_row(grp, kk, i):
        return grp * gsz + jnp.where(kk == nk - 1, i, 0)

    return gsz, epilogue_row


def _mm_residual_norm(a, w, h, g, scale, next_g, name, carried=()):
    m, k = a.shape
    d = w.shape[1]
    tm = _tile(m, ACC_ROWS, SUBLANE_BF16)
    tk = _tile(k, K_TILE, LANE)
    nk = k // tk
    gsz, epilogue_row = _row_groups(m // tm, ACC_GROUP, nk)

    def body(a_ref, w_ref, h_ref, g_ref, ng_ref, fo_ref, hn_ref, nn_ref, acc_ref):
        kk, i = pl.program_id(1), pl.program_id(2)

        @pl.when(kk == 0)
        def _():
            acc_ref[i] = jnp.zeros((tm, d), F32)

        acc_ref[i] += jnp.dot(a_ref[...], w_ref[...], preferred_element_type=F32)

        @pl.when(kk == nk - 1)
        def _():
            fo = acc_ref[i]
            fo_ref[...] = fo
            r = lax.rsqrt(jnp.mean(fo * fo, axis=-1, keepdims=True) + EPS)
            hn = h_ref[...] + scale * (fo * r * g_ref[...])
            hn_ref[...] = hn
            rn = lax.rsqrt(jnp.mean(hn * hn, axis=-1, keepdims=True) + EPS)
            nn_ref[...] = (hn * rn * ng_ref[...]).astype(nn_ref.dtype)

    row = pl.BlockSpec((tm, d), lambda grp, kk, i: (epilogue_row(grp, kk, i), 0))
    vec = pl.BlockSpec((1, d), lambda grp, kk, i: (0, 0))
    return _call(
        body, grid=(m // tm // gsz, nk, gsz),
        in_specs=[pl.BlockSpec((tm, tk), lambda grp, kk, i: (grp * gsz + i, kk)),
                  pl.BlockSpec((tk, d), lambda grp, kk, i: (kk, 0)), row, vec, vec],
        out_specs=[row, row, row],
        out_shape=[jax.ShapeDtypeStruct((m, d), F32)] * 2 + [jax.ShapeDtypeStruct((m, d), MXU_DTYPE)],
        scratch_shapes=[pltpu.VMEM((gsz, tm, d), F32)], name=name, args=(a, w, h, g, next_g), carried=carried)


def _mm_residual_loss(a, w, h, g, scale, target, lead, name, carried=()):
    m, k = a.shape
    d = w.shape[1]
    tm = _tile(m, ACC_ROWS, SUBLANE_BF16)
    tk = _tile(k, K_TILE, LANE)
    nk = k // tk
    gsz, epilogue_row = _row_groups(m // tm, ACC_GROUP, nk)

    def body(a_ref, w_ref, h_ref, g_ref, t_ref, dy_ref, dfo_ref, dg_ref, l_ref, acc_ref):
        grp, kk, i = pl.program_id(0), pl.program_id(1), pl.program_id(2)

        @pl.when(jnp.logical_and(jnp.logical_and(grp == 0, kk == 0), i == 0))
        def _():
            dg_ref[...] = jnp.zeros_like(dg_ref)
            l_ref[...] = jnp.zeros_like(l_ref)

        @pl.when(kk == 0)
        def _():
            acc_ref[i] = jnp.zeros((tm, d), F32)

        acc_ref[i] += jnp.dot(a_ref[...], w_ref[...], preferred_element_type=F32)

        @pl.when(kk == nk - 1)
        def _():
            fo = acc_ref[i]
            gain = g_ref[...]
            r = lax.rsqrt(jnp.mean(fo * fo, axis=-1, keepdims=True) + EPS)
            xh = fo * r
            y = h_ref[...] + scale * (xh * gain)
            row = (grp * gsz + i) * tm + lax.broadcasted_iota(jnp.int32, (tm, 1), 0)
            e = jnp.where(row >= lead, y - t_ref[...], 0.0)
            dy = e * (1.0 / d)
            dy_ref[...] = dy
            l_ref[...] += 0.5 * jnp.sum(jnp.sum(e * e, axis=-1, keepdims=True) * (1.0 / d), axis=0, keepdims=True)
            dn = scale * dy
            dyh = dn * gain
            dfo_ref[...] = (r * (dyh - xh * jnp.mean(dyh * xh, axis=-1, keepdims=True))).astype(dfo_ref.dtype)
            dg_ref[...] += jnp.sum(dn * xh, axis=0, keepdims=True)

    row = pl.BlockSpec((tm, d), lambda grp, kk, i: (epilogue_row(grp, kk, i), 0))
    vec = pl.BlockSpec((1, d), lambda grp, kk, i: (0, 0))
    return _call(
        body, grid=(m // tm // gsz, nk, gsz),
        in_specs=[pl.BlockSpec((tm, tk), lambda grp, kk, i: (grp * gsz + i, kk)),
                  pl.BlockSpec((tk, d), lambda grp, kk, i: (kk, 0)), row, vec, row],
        out_specs=[row, row, vec, pl.BlockSpec((1, 1), lambda grp, kk, i: (0, 0))],
        out_shape=[jax.ShapeDtypeStruct((m, d), F32), jax.ShapeDtypeStruct((m, d), MXU_DTYPE),
                   jax.ShapeDtypeStruct((1, d), F32), jax.ShapeDtypeStruct((1, 1), F32)],
        scratch_shapes=[pltpu.VMEM((gsz, tm, d), F32)], name=name, args=(a, w, h, g, target), carried=carried)


def _norm_bwd_row_tile(m):
    return _tile(m, ACC_ROWS, SUBLANE_BF16)


def _mm_norm_bwd(pairs, h, g, dh_up, name, carried=(), row_tiles=None, dg_init=None, post=None):
    n_pairs = len(pairs)
    m, k = pairs[0][0].shape
    d = h.shape[1]
    tm = _norm_bwd_row_tile(m)
    tk = _tile(k, K_TILE, LANE)
    nk = k // tk
    t0, nt = row_tiles if row_tiles is not None else (0, m // tm)
    gsz, epilogue_row = _row_groups(nt, ACC_GROUP, nk)
    if dg_init is None:
        dg_init = jnp.zeros((1, d), F32)

    n_post = 0 if post is None else 2

    def body(*refs):
        ops = refs[:2 * n_pairs]
        h_ref, g_ref, up_ref, init_ref = refs[2 * n_pairs:2 * n_pairs + 4]
        post_in = refs[2 * n_pairs + 4:2 * n_pairs + 4 + n_post]
        dh_ref, dg_ref = refs[2 * n_pairs + 4 + n_post:2 * n_pairs + 6 + n_post]
        post_out = refs[2 * n_pairs + 6 + n_post:2 * n_pairs + 6 + 2 * n_post]
        acc_ref = refs[-1]
        grp, kk, i = pl.program_id(0), pl.program_id(1), pl.program_id(2)

        @pl.when(jnp.logical_and(jnp.logical_and(grp == 0, kk == 0), i == 0))
        def _():
            dg_ref[...] = init_ref[...]
            if post is not None:
                post_out[1][...] = jnp.zeros_like(post_out[1])

        @pl.when(kk == 0)
        def _():
            acc_ref[i] = jnp.zeros((tm, d), F32)

        for p in range(n_pairs):
            acc_ref[i] += jnp.dot(ops[2 * p][...], ops[2 * p + 1][...], preferred_element_type=F32)

        @pl.when(kk == nk - 1)
        def _():
            dx, dgr = _rmsnorm_bwd_rows(h_ref[...], g_ref[...], acc_ref[i])
            dh = up_ref[...] + dx
            dh_ref[...] = dh
            dg_ref[...] += jnp.sum(dgr, axis=0, keepdims=True)
            if post is not None:
                dfo, dpr = _rmsnorm_bwd_rows(post_in[0][...], post_in[1][...], post[2] * dh)
                post_out[0][...] = dfo.astype(post_out[0].dtype)
                post_out[1][...] += jnp.sum(dpr, axis=0, keepdims=True)

    row_in = pl.BlockSpec((tm, d), lambda grp, kk, i: (t0 + epilogue_row(grp, kk, i), 0))
    row_out = pl.BlockSpec((tm, d), lambda grp, kk, i: (epilogue_row(grp, kk, i), 0))
    vec = pl.BlockSpec((1, d), lambda grp, kk, i: (0, 0))
    in_specs = []
    args = []
    for a, w in pairs:
        in_specs += [pl.BlockSpec((tm, tk), lambda grp, kk, i: (t0 + grp * gsz + i, kk)),
                     pl.BlockSpec((tk, d), lambda grp, kk, i: (kk, 0))]
        args += [a, w]
    in_specs += [row_in, vec, row_in, vec]
    args += [h, g, dh_up, dg_init]
    out_specs = [row_out, vec]
    out_shape = [jax.ShapeDtypeStruct((nt * tm, d), F32), jax.ShapeDtypeStruct((1, d), F32)]
    if post is not None:
        in_specs += [row_in, vec]
        args += [post[0], post[1]]
        out_specs += [row_out, vec]
        out_shape += [jax.ShapeDtypeStruct((nt * tm, d), MXU_DTYPE), jax.ShapeDtypeStruct((1, d), F32)]
    return _call(
        body, grid=(nt // gsz, nk, gsz), in_specs=in_specs, out_specs=out_specs, out_shape=out_shape,
        scratch_shapes=[pltpu.VMEM((gsz, tm, d), F32)], name=name, args=tuple(args), carried=carried)


def _mm_tn(a, b, name, carried=()):
    m, ka = a.shape
    d = b.shape[1]
    tf = _tile(ka, 512, LANE)

    def body(a_ref, b_ref, o_ref):
        o_ref[...] = _dot_tn(a_ref[...], b_ref[...]).astype(o_ref.dtype)

    return _call(
        body, grid=(ka // tf,),
        in_specs=[pl.BlockSpec((m, tf), lambda j: (0, j)),
                  pl.BlockSpec((m, d), lambda j: (0, 0), pipeline_mode=pl.Buffered(1))],
        out_specs=pl.BlockSpec((tf, d), lambda j: (j, 0)),
        out_shape=jax.ShapeDtypeStruct((ka, d), WIRE_DTYPE), name=name, args=(a, b), carried=carried)


GELU_K = 0.7978845608028654
GELU_C = 0.044715


def _expm1(x):
    series = x * (1.0 + x * (1.0 / 2 + x * (1.0 / 6 + x * (1.0 / 24 + x * (1.0 / 120)))))
    return jnp.where(jnp.abs(x) < 0.1, series, jnp.exp(x) - 1.0)


def _softplus(x):
    return jnp.maximum(x, 0.0) + jnp.log1p(jnp.exp(-jnp.abs(x)))


def _block_mm(v, w_ref, transposed):
    nbk = w_ref.shape[0]
    outs = []
    for j in range(nbk):
        vj = v[:, j * BD:(j + 1) * BD]
        outs.append(_dot_nt(vj, w_ref[j]) if transposed else jnp.dot(vj, w_ref[j], preferred_element_type=F32))
    return outs[0] if nbk == 1 else jnp.concatenate(outs, axis=1)


def _group_mean(q, gm_ref):
    hi = q.astype(MXU_DTYPE)
    lo = (q - hi.astype(F32)).astype(MXU_DTYPE)
    nbk = q.shape[1] // BD
    gm = gm_ref[...]
    outs = []
    for j in range(nbk):
        sl = slice(j * BD, (j + 1) * BD)
        outs.append(jnp.dot(hi[:, sl], gm, preferred_element_type=F32) + jnp.dot(lo[:, sl], gm, preferred_element_type=F32))
    return outs[0] if nbk == 1 else jnp.concatenate(outs, axis=1)


class _RowReader:
    def __init__(self, ref):
        self.ref = ref

    def __getitem__(self, rows):
        return self.ref[rows, :]


def _shifted(ext_ref, cur, before8, after8, downs=(), ups=()):
    r = cur.shape[0]
    if downs:
        ext_ref[0:8, :] = before8
    ext_ref[8:8 + r, :] = cur
    if ups:
        ext_ref[8 + r:16 + r, :] = after8
    return [ext_ref[pl.ds(8 - j, r), :] for j in downs] + [ext_ref[pl.ds(8 + j, r), :] for j in ups]


def _lru_gates(xc, pv, wa_ref, wx_ref):
    xcb = xc.astype(MXU_DTYPE)
    ga = jax.nn.sigmoid(_block_mm(xcb, wa_ref, False) + pv[5:6])
    gx = jax.nn.sigmoid(_block_mm(xcb, wx_ref, False) + pv[6:7])
    sp = _softplus(-pv[7:8])
    log_a = -LRU_C * ga * sp
    a = jnp.exp(log_a)
    e2 = _expm1(2.0 * log_a)
    mult = jnp.sqrt(-e2)
    return xcb, ga, gx, sp, a, e2, mult


def _gelu_parts(y):
    th = jnp.tanh(GELU_K * (y + GELU_C * y * y * y))
    return 0.5 * y * (1.0 + th), th


def _scan_block(a, u, sa_ref, su_ref, carry_ref, out_ref, reverse):
    r, c = a.shape
    n = r // 8
    a3 = a.reshape(n, 8, c)
    u3 = u.reshape(n, 8, c)
    sub = lax.broadcasted_iota(jnp.int32, (n, 8, c), 1)
    for dlt in (1, 2, 4):
        keep = (sub < 8 - dlt) if reverse else (sub >= dlt)
        shift = 8 - dlt if reverse else dlt
        sh_a = pltpu.roll(a3, shift, axis=1)
        sh_u = pltpu.roll(u3, shift, axis=1)
        u3 = u3 + a3 * jnp.where(keep, sh_u, 0.0)
        a3 = a3 * jnp.where(keep, sh_a, 1.0)
    sa_ref[...] = a3.reshape(r, c)
    su_ref[...] = u3.reshape(r, c)
    for k in (range(n - 1, -1, -1) if reverse else range(n)):
        rows = pl.ds(8 * k, 8)
        out_ref[rows, :] = su_ref[rows, :] + sa_ref[rows, :] * carry_ref[...]
        carry_ref[...] = out_ref[pl.ds(8 * k if reverse else 8 * k + 7, 1), :]


def _mixer_fwd(z, pv, wa, wx, gm, pad, name, carried=()):
    m = z.shape[0]
    c = pv.shape[1]
    r = MIX_ROWS
    nb = m // r

    def body(z_ref, pv_ref, wa_ref, wx_ref, gm_ref, mixed_ref, hs_ref, ext_ref, tailx_ref, tailc_ref, carry_ref,
             sa_ref, su_ref):
        b = pl.program_id(0)

        @pl.when(b == 0)
        def _():
            tailx_ref[...] = jnp.zeros_like(tailx_ref)
            tailc_ref[...] = jnp.zeros_like(tailc_ref)
            carry_ref[...] = jnp.zeros_like(carry_ref)

        pv = _RowReader(pv_ref)
        row = b * r + lax.broadcasted_iota(jnp.int32, (r, 1), 0)
        maskf = (row >= pad).astype(F32)
        y = z_ref[:, 0:c]
        xl = z_ref[:, c:2 * c]
        bs = z_ref[:, 2 * c:3 * c]
        cv = z_ref[:, 3 * c:4 * c] * z_ref[:, 4 * c:5 * c]

        x1, x2, x3 = _shifted(ext_ref, xl, tailx_ref[...], None, downs=(1, 2, 3))
        tailx_ref[...] = z_ref[pl.ds(r - 8, 8), c:2 * c]
        xc = pv[4:5] + pv[3:4] * xl + pv[2:3] * x1 + pv[1:2] * x2 + pv[0:1] * x3
        _, _, gx, _, a, _, mult = _lru_gates(xc, pv, wa_ref, wx_ref)
        uu = mult * (gx * xc) * maskf

        _scan_block(a, uu, sa_ref, su_ref, carry_ref, hs_ref, reverse=False)
        hs = hs_ref[...]

        gelu_y, _ = _gelu_parts(y)
        lru_out = hs * gelu_y
        c1, c2 = _shifted(ext_ref, cv, tailc_ref[...], None, downs=(1, 2))
        tailc_ref[...] = cv[r - 8:r]
        sc_out = bs * (pv[10:11] * cv + pv[9:10] * c1 + pv[8:9] * c2)

        rl = lax.rsqrt(_group_mean(lru_out * lru_out, gm_ref) + EPS)
        rs = lax.rsqrt(_group_mean(sc_out * sc_out, gm_ref) + EPS)
        mixed_ref[:, 0:c] = (lru_out * rl * pv[11:12]).astype(mixed_ref.dtype)
        mixed_ref[:, c:2 * c] = (sc_out * rs * pv[12:13]).astype(mixed_ref.dtype)

    full = lambda shape: pl.BlockSpec(shape, lambda b: (0,) * len(shape))
    return _call(
        body, grid=(nb,),
        in_specs=[pl.BlockSpec((r, 5 * c), lambda b: (b, 0)), full(pv.shape), full(wa.shape), full(wx.shape), full(gm.shape)],
        out_specs=[pl.BlockSpec((r, 2 * c), lambda b: (b, 0)), pl.BlockSpec((r, c), lambda b: (b, 0))],
        out_shape=[jax.ShapeDtypeStruct((m, 2 * c), MXU_DTYPE), jax.ShapeDtypeStruct((m, c), F32)],
        scratch_shapes=[pltpu.VMEM((r + 16, c), F32), pltpu.VMEM((8, c), F32), pltpu.VMEM((8, c), F32),
                        pltpu.VMEM((1, c), F32), pltpu.VMEM((r, c), F32), pltpu.VMEM((r, c), F32)],
        name=name, args=(z, pv, wa, wx, gm), carried=carried)


def _mixer_bwd(z, hs, dmixed, pv, wa, wx, gm, pad, name, carried=()):
    m = z.shape[0]
    c = pv.shape[1]
    r = MIX_ROWS
    nb = m // r
    r8 = r // 8
    assert pad % SUBLANE_BF16 == 0 and r % SUBLANE_BF16 == 0

    def body(z_ref, zp_ref, hs_ref, hsp_ref, dm_ref, pv_ref, wa_ref, wx_ref, gm_ref,
             dz_ref, dpv_ref, dwa_ref, dwx_ref, ext_ref, hxc_ref, hsc_ref, hp_ref, pc_ref, sa_ref, su_ref, p_ref):
        i = pl.program_id(0)
        b = nb - 1 - i

        @pl.when(i == 0)
        def _():
            hxc_ref[...] = jnp.zeros_like(hxc_ref)
            hsc_ref[...] = jnp.zeros_like(hsc_ref)
            hp_ref[...] = jnp.zeros_like(hp_ref)
            pc_ref[...] = jnp.zeros_like(pc_ref)
            dpv_ref[...] = jnp.zeros_like(dpv_ref)
            dwa_ref[...] = jnp.zeros_like(dwa_ref)
            dwx_ref[...] = jnp.zeros_like(dwx_ref)

        pv = _RowReader(pv_ref)
        row = b * r + lax.broadcasted_iota(jnp.int32, (r, 1), 0)
        maskf = (row >= pad).astype(F32)
        has_prev = (b > 0).astype(F32)
        y = z_ref[:, 0:c]
        xl = z_ref[:, c:2 * c]
        bs = z_ref[:, 2 * c:3 * c]
        cs = z_ref[:, 3 * c:4 * c]
        vs = z_ref[:, 4 * c:5 * c]
        cv = cs * vs
        xl_prev = zp_ref[:, c:2 * c] * has_prev
        cv_prev = zp_ref[:, 3 * c:4 * c] * zp_ref[:, 4 * c:5 * c] * has_prev
        hs = hs_ref[...]

        x1, x2, x3 = _shifted(ext_ref, xl, xl_prev, None, downs=(1, 2, 3))
        xc = pv[4:5] + pv[3:4] * xl + pv[2:3] * x1 + pv[1:2] * x2 + pv[0:1] * x3
        xcb, ga, gx, sp, a, e2, mult = _lru_gates(xc, pv, wa_ref, wx_ref)
        gxx = gx * xc
        gelu_y, th = _gelu_parts(y)
        lru_out = hs * gelu_y
        c1, c2 = _shifted(ext_ref, cv, cv_prev, None, downs=(1, 2))
        sc = pv[10:11] * cv + pv[9:10] * c1 + pv[8:9] * c2
        sc_out = bs * sc

        def group_norm_bwd(v, dm, gain):
            rr = lax.rsqrt(_group_mean(v * v, gm_ref) + EPS)
            vh = v * rr
            dvh = dm * gain
            dv = rr * (dvh - vh * _group_mean(dvh * vh, gm_ref))
            return dv, jnp.sum(dm * vh, axis=0, keepdims=True)

        d_lru_out, d_og = group_norm_bwd(lru_out, dm_ref[:, 0:c], pv[11:12])
        d_sc_out, d_sg = group_norm_bwd(sc_out, dm_ref[:, c:2 * c], pv[12:13])
        dpv_ref[11:12, :] += d_og
        dpv_ref[12:13, :] += d_sg

        dhs = d_lru_out * gelu_y
        dgelu = 0.5 * (1.0 + th) + 0.5 * y * (1.0 - th * th) * GELU_K * (1.0 + 3.0 * GELU_C * y * y)
        dy = d_lru_out * hs * dgelu

        _scan_block(a, a * dhs, sa_ref, su_ref, pc_ref, p_ref, reverse=True)
        (p_next,) = _shifted(ext_ref, p_ref[...], None, hp_ref[...], ups=(1,))
        hp_ref[...] = p_ref[0:8, :]
        q = dhs + p_next
        (hs_prev,) = _shifted(ext_ref, hs, hsp_ref[...] * has_prev, None, downs=(1,))
        duu = q * maskf
        da = q * hs_prev

        dmult = duu * gxx
        dgxx = duu * mult
        dgx = dgxx * xc
        dxc = dgxx * gx
        dlog_a = da * a - dmult * ((1.0 + e2) / mult)
        dga = dlog_a * (-LRU_C * sp)
        dsp = jnp.sum(dlog_a * (-LRU_C * ga), axis=0, keepdims=True)
        dpv_ref[7:8, :] += dsp * (-jax.nn.sigmoid(-pv[7:8]))
        dga_pre = dga * ga * (1.0 - ga)
        dgx_pre = dgx * gx * (1.0 - gx)
        dpv_ref[5:6, :] += jnp.sum(dga_pre, axis=0, keepdims=True)
        dpv_ref[6:7, :] += jnp.sum(dgx_pre, axis=0, keepdims=True)
        dga_b = dga_pre.astype(MXU_DTYPE)
        dgx_b = dgx_pre.astype(MXU_DTYPE)
        dxc = dxc + _block_mm(dga_b, wa_ref, True) + _block_mm(dgx_b, wx_ref, True)
        for j in range(c // BD):
            sl = slice(j * BD, (j + 1) * BD)
            dwa_ref[j] += _dot_tn(xcb[:, sl], dga_b[:, sl])
            dwx_ref[j] += _dot_tn(xcb[:, sl], dgx_b[:, sl])

        dpv_ref[4:5, :] += jnp.sum(dxc, axis=0, keepdims=True)
        dpv_ref[3:4, :] += jnp.sum(dxc * xl, axis=0, keepdims=True)
        dpv_ref[2:3, :] += jnp.sum(dxc * x1, axis=0, keepdims=True)
        dpv_ref[1:2, :] += jnp.sum(dxc * x2, axis=0, keepdims=True)
        dpv_ref[0:1, :] += jnp.sum(dxc * x3, axis=0, keepdims=True)
        u1, u2, u3 = _shifted(ext_ref, dxc, None, hxc_ref[...], ups=(1, 2, 3))
        hxc_ref[...] = dxc[0:8]
        dxl = pv[3:4] * dxc + pv[2:3] * u1 + pv[1:2] * u2 + pv[0:1] * u3

        dbs = d_sc_out * sc
        dsc = d_sc_out * bs
        dpv_ref[10:11, :] += jnp.sum(dsc * cv, axis=0, keepdims=True)
        dpv_ref[9:10, :] += jnp.sum(dsc * c1, axis=0, keepdims=True)
        dpv_ref[8:9, :] += jnp.sum(dsc * c2, axis=0, keepdims=True)
        s1, s2 = _shifted(ext_ref, dsc, None, hsc_ref[...], ups=(1, 2))
        hsc_ref[...] = dsc[0:8]
        dcv = pv[10:11] * dsc + pv[9:10] * s1 + pv[8:9] * s2

        dz_ref[:, 0:c] = dy.astype(dz_ref.dtype)
        dz_ref[:, c:2 * c] = dxl.astype(dz_ref.dtype)
        dz_ref[:, 2 * c:3 * c] = dbs.astype(dz_ref.dtype)
        dz_ref[:, 3 * c:4 * c] = (dcv * vs).astype(dz_ref.dtype)
        dz_ref[:, 4 * c:5 * c] = (dcv * cs).astype(dz_ref.dtype)

        for k in range(-(-pad // r)):
            count = min(r, pad - k * r)

            def clear(count=count):
                dz_ref[0:count, :] = jnp.zeros((count, 5 * c), dz_ref.dtype)

            pl.when(b == k)(clear)

    full = lambda shape: pl.BlockSpec(shape, lambda i: (0,) * len(shape))
    cur = lambda width: pl.BlockSpec((r, width), lambda i: (nb - 1 - i, 0))
    prev8 = lambda width: pl.BlockSpec((8, width), lambda i: (jnp.maximum((nb - 1 - i) * r8 - 1, 0), 0))
    return _call(
        body, grid=(nb,),
        in_specs=[cur(5 * c), prev8(5 * c), cur(c), prev8(c), cur(2 * c),
                  full(pv.shape), full(wa.shape), full(wx.shape), full(gm.shape)],
        out_specs=[cur(5 * c), full(pv.shape), full(wa.shape), full(wx.shape)],
        out_shape=[jax.ShapeDtypeStruct((m, 5 * c), MXU_DTYPE), jax.ShapeDtypeStruct(pv.shape, F32),
                   jax.ShapeDtypeStruct(wa.shape, F32), jax.ShapeDtypeStruct(wx.shape, F32)],
        scratch_shapes=[pltpu.VMEM((r + 16, c), F32), pltpu.VMEM((8, c), F32), pltpu.VMEM((8, c), F32),
                        pltpu.VMEM((8, c), F32), pltpu.VMEM((1, c), F32), pltpu.VMEM((r, c), F32),
                        pltpu.VMEM((r, c), F32), pltpu.VMEM((r, c), F32)],
        name=name, args=(z, z, hs, hs, dmixed, pv, wa, wx, gm), carried=carried)


def _position():
    return lax.axis_index("x"), lax.axis_index("y"), lax.axis_index("c")


def _block_of(px, py, pc):
    return 4 * px + 2 * py + pc


class _TwoLevelGather:
    def __init__(self, n_arrays, rows_of, src_of, send_sems, recv_sems):
        x, y, c = _position()
        self.n, self.rows_of, self.src_of = n_arrays, rows_of, src_of
        self.send_sems, self.recv_sems = send_sems, recv_sems
        self.c, self.me, self.sibling = c, (x, y, c), (x, y, 1 - c)
        self.chips = [(1 - x, y), (x, 1 - y), (1 - x, 1 - y)]

    def _copy(self, i, k, block, to, src=None):
        return pltpu.make_async_remote_copy(
            src_ref=self.rows_of(i, *block) if src is None else src, dst_ref=self.rows_of(i, *block),
            send_sem=self.send_sems.at[7 * i + k], recv_sem=self.recv_sems.at[7 * i + k],
            device_id=to, device_id_type=MESH)

    def _first(self, i):
        own = [self._copy(i, 0, self.me, self.sibling, src=self.src_of(i))]
        return own + [self._copy(i, 1 + j, self.me, (*chip, self.c), src=self.src_of(i))
                      for j, chip in enumerate(self.chips)]

    def _passed(self, i, j):
        return self._copy(i, 4 + j, (*self.chips[j], self.c), self.sibling)

    def start(self):
        for i in range(self.n):
            for cp in self._first(i):
                cp.start()

    def forward(self):
        for i in range(self.n):
            for j, chip in enumerate(self.chips):
                self._copy(i, 1 + j, (*chip, self.c), self.me).wait_recv()
                self._passed(i, j).start()

    def drain(self):
        for i in range(self.n):
            self._copy(i, 0, self.sibling, self.me).wait_recv()
            for j, chip in enumerate(self.chips):
                self._copy(i, 4 + j, (*chip, 1 - self.c), self.me).wait_recv()
        for i in range(self.n):
            for cp in self._first(i) + [self._passed(i, j) for j in range(3)]:
                cp.wait_send()


class _RelayGather:
    def __init__(self, n_arrays, rows_of, src_of, send_sems, recv_sems):
        x, y, c = _position()
        self.n, self.rows_of, self.src_of = n_arrays, rows_of, src_of
        self.send_sems, self.recv_sems = send_sems, recv_sems
        self.me, self.sibling = (x, y, c), (x, y, 1 - c)
        self.xn, self.yn, self.dg = (1 - x, y, c), (x, 1 - y, c), (1 - x, 1 - y, c)

    def _copy(self, i, k, block, to, half=None, src=None):
        rows = self.rows_of(i, *block, half)
        return pltpu.make_async_remote_copy(
            src_ref=rows if src is None else src, dst_ref=rows,
            send_sem=self.send_sems.at[8 * i + k], recv_sem=self.recv_sems.at[8 * i + k],
            device_id=to, device_id_type=MESH)

    def _sends(self, i):
        own = self.src_of(i)
        return [self._copy(i, 0, self.me, self.sibling, src=own), self._copy(i, 1, self.me, self.xn, src=own),
                self._copy(i, 2, self.me, self.yn, src=own),
                self._copy(i, 3, self.xn, self.yn, half=0), self._copy(i, 4, self.yn, self.xn, half=1),
                self._copy(i, 5, self.xn, self.sibling), self._copy(i, 6, self.yn, self.sibling),
                self._copy(i, 7, self.dg, self.sibling)]

    def start(self):
        for i in range(self.n):
            for cp in self._sends(i)[0:3]:
                cp.start()

    def forward(self):
        for i in range(self.n):
            self._copy(i, 1, self.xn, self.me).wait_recv()
            self._copy(i, 2, self.yn, self.me).wait_recv()
            for cp in self._sends(i)[3:7]:
                cp.start()

    def drain(self):
        x, y, c = self.me
        for i in range(self.n):
            self._copy(i, 3, self.dg, self.me, half=0).wait_recv()
            self._copy(i, 4, self.dg, self.me, half=1).wait_recv()
            self._sends(i)[7].start()
        for i in range(self.n):
            self._copy(i, 0, self.sibling, self.me).wait_recv()
            self._copy(i, 5, (1 - x, y, 1 - c), self.me).wait_recv()
            self._copy(i, 6, (x, 1 - y, 1 - c), self.me).wait_recv()
            self._copy(i, 7, (1 - x, 1 - y, 1 - c), self.me).wait_recv()
        for i in range(self.n):
            for cp in self._sends(i):
                cp.wait_send()


class _CarriedGather:
    def __init__(self, shards, padded_rows, zeros, forward_at, part=None, into=None):
        d = shards[0].shape[1]
        self.forward_at = forward_at
        self.n = len(shards)
        self.rows = [s.shape[0] for s in shards]
        self.pads = [p - N_DEV * r for r, p in zip(self.rows, padded_rows)]
        assert max(self.pads) <= zeros.shape[0] and zeros.shape[1] == d
        self.part = part if part is not None else (0, self.rows[0])
        assert (part is None and into is None) or self.n == 1
        assert self.part[0] % SUBLANE_BF16 == 0 and self.part[1] % SUBLANE_BF16 == 0
        self.arrays = list(shards) + [zeros] + ([into] if into is not None else [])
        self.out_shapes = [jax.ShapeDtypeStruct((p, d), s.dtype) for s, p in zip(shards, padded_rows)]
        self.aliases = {self.n + 1: 0} if into is not None else {}
        if into is not None:
            self.pads = [0] * self.n
        self.n_remote, self.n_local = 8 * self.n, 2 * self.n
        self.results = None

    def _rows_of(self, outs):
        def rows_of(i, px, py, pc, half):
            first, count = (self.part if self.n == 1 else (0, self.rows[i]))
            head = _round_up(count // 2, SUBLANE_BF16)
            if half == 0:
                count = head
            elif half == 1:
                first, count = first + head, count - head
            first = _block_of(px, py, pc) * self.rows[i] + first
            return outs[i].at[pl.ds(pl.multiple_of(first, SUBLANE_BF16), count), :]
        return rows_of

    def _own(self, ins, i):
        return ins[i].at[pl.ds(self.part[0], self.part[1]), :] if self.n == 1 else ins[i]

    def _gather(self, ins, outs, send_sems, recv_sems):
        return _RelayGather(self.n, self._rows_of(outs), functools.partial(self._own, ins), send_sems, recv_sems)

    def _local(self, ins, outs, local_sems):
        x, y, c = _position()
        rows_of = self._rows_of(outs)
        cps = []
        for i in range(self.n):
            cps.append(pltpu.make_async_copy(self._own(ins, i), rows_of(i, x, y, c, None), local_sems.at[2 * i]))
            if self.pads[i]:
                cps.append(pltpu.make_async_copy(ins[self.n].at[pl.ds(0, self.pads[i]), :],
                                                 outs[i].at[pl.ds(N_DEV * self.rows[i], self.pads[i]), :],
                                                 local_sems.at[2 * i + 1]))
        return cps

    def start(self, ins, outs, send_sems, recv_sems, local_sems):
        for cp in self._local(ins, outs, local_sems):
            cp.start()
        self._gather(ins, outs, send_sems, recv_sems).start()

    def forward(self, ins, outs, send_sems, recv_sems, local_sems):
        self._gather(ins, outs, send_sems, recv_sems).forward()

    def finish(self, ins, outs, send_sems, recv_sems, local_sems):
        self._gather(ins, outs, send_sems, recv_sems).drain()
        for cp in self._local(ins, outs, local_sems):
            cp.wait()


class _CarriedSwap:
    def __init__(self, grads, shard_rows):
        d = grads[0].shape[1]
        self.n, self.rows = len(grads), list(shard_rows)
        self.arrays = list(grads)
        self.out_shapes = [jax.ShapeDtypeStruct((4, s, d), g.dtype) for g, s in zip(grads, shard_rows)]
        self.aliases = {}
        self.n_remote, self.n_local = 4 * self.n, 0
        self.forward_at = 1.0
        self.results = None

    def _copies(self, ins, outs, send_sems, recv_sems):
        x, y, c = _position()
        cps = []
        for i in range(self.n):
            s = self.rows[i]
            for k in range(4):
                blk = _block_of(k >> 1, k & 1, 1 - c)
                cps.append(pltpu.make_async_remote_copy(
                    src_ref=ins[i].at[pl.ds(pl.multiple_of(blk * s, SUBLANE_BF16), s), :], dst_ref=outs[i].at[k],
                    send_sem=send_sems.at[4 * i + k], recv_sem=recv_sems.at[4 * i + k],
                    device_id=(x, y, 1 - c), device_id_type=MESH))
        return cps

    def start(self, ins, outs, send_sems, recv_sems, local_sems):
        for cp in self._copies(ins, outs, send_sems, recv_sems):
            cp.start()

    def forward(self, *_):
        pass

    def finish(self, ins, outs, send_sems, recv_sems, local_sems):
        for cp in self._copies(ins, outs, send_sems, recv_sems):
            cp.wait()


class _CarriedChipExchange:
    def __init__(self, presums, part=None, into=None):
        self.n = len(presums)
        assert (part is None and into is None) or self.n == 1
        self.part = part if part is not None else (0, presums[0].shape[1])
        assert self.part[0] % SUBLANE_BF16 == 0 and self.part[1] % SUBLANE_BF16 == 0
        self.arrays = list(presums) + ([into] if into is not None else [])
        self.out_shapes = [jax.ShapeDtypeStruct(p.shape, p.dtype) for p in presums]
        self.aliases = {self.n: 0} if into is not None else {}
        self.n_remote, self.n_local = 3 * self.n, 0
        self.forward_at = 1.0
        self.results = None

    def _copies(self, ins, outs, send_sems, recv_sems):
        x, y, c = _position()
        cps = []
        for i in range(self.n):
            rows = pl.ds(*self.part) if self.n == 1 else pl.ds(0, self.arrays[i].shape[1])
            for r in range(1, 4):
                cps.append(pltpu.make_async_remote_copy(
                    src_ref=ins[i].at[r - 1, rows, :], dst_ref=outs[i].at[r - 1, rows, :],
                    send_sem=send_sems.at[3 * i + r - 1], recv_sem=recv_sems.at[3 * i + r - 1],
                    device_id=(x ^ (r >> 1), y ^ (r & 1), c), device_id_type=MESH))
        return cps

    def start(self, ins, outs, send_sems, recv_sems, local_sems):
        for cp in self._copies(ins, outs, send_sems, recv_sems):
            cp.start()

    def forward(self, *_):
        pass

    def finish(self, ins, outs, send_sems, recv_sems, local_sems):
        for cp in self._copies(ins, outs, send_sems, recv_sems):
            cp.wait()


def _gather_small(block, reduce, name):
    rr, nn = block.shape

    def body(x_ref, out_ref, *rest):
        if reduce:
            stack_ref, send_sems, recv_sems, local_sem = rest
        else:
            send_sems, recv_sems, local_sem = rest
            stack_ref = out_ref
        x, y, c = _position()

        def rows_of(i, px, py, pc):
            return stack_ref.at[pl.ds(pl.multiple_of(_block_of(px, py, pc) * rr, 8), rr), :]

        own = pltpu.make_async_copy(x_ref, rows_of(0, x, y, c), local_sem)
        own.start()
        gather = _TwoLevelGather(1, rows_of, lambda i: x_ref, send_sems, recv_sems)
        gather.start()
        gather.forward()
        gather.drain()
        own.wait()
        if reduce:
            acc = stack_ref[0:rr, :]
            for k in range(1, N_DEV):
                acc = acc + stack_ref[k * rr:(k + 1) * rr, :]
            out_ref[...] = acc

    vmem = pl.BlockSpec(memory_space=pltpu.VMEM)
    scratch = [pltpu.SemaphoreType.DMA((7,)), pltpu.SemaphoreType.DMA((7,)), pltpu.SemaphoreType.DMA]
    if reduce:
        scratch = [pltpu.VMEM((N_DEV * rr, nn), F32)] + scratch
    out_rows = rr if reduce else N_DEV * rr
    return pl.pallas_call(
        body, in_specs=[vmem], out_specs=vmem, out_shape=jax.ShapeDtypeStruct((out_rows, nn), F32),
        scratch_shapes=scratch, name=name, compiler_params=_params())(block)


def _sum_stack(stack, name):
    rr = stack.shape[0] // N_DEV

    def body(s_ref, o_ref):
        acc = s_ref[0:rr, :]
        for k in range(1, N_DEV):
            acc = acc + s_ref[k * rr:(k + 1) * rr, :]
        o_ref[...] = acc

    vmem = pl.BlockSpec(memory_space=pltpu.VMEM)
    return pl.pallas_call(body, in_specs=[vmem], out_specs=vmem,
                          out_shape=jax.ShapeDtypeStruct((rr, stack.shape[1]), F32), name=name,
                          compiler_params=_params())(stack)


def _presum(where, grad, swapped, name):
    s, d = swapped.shape[1], swapped.shape[2]
    tc = _tile(d, 2048, LANE)

    def body(where_ref, g_ref, sw_ref, o_ref):
        o_ref[0] = (g_ref[...].astype(F32) + sw_ref[0].astype(F32)).astype(o_ref.dtype)

    return _call(
        body, grid=(3, d // tc),
        in_specs=[pl.BlockSpec((s, tc), lambda r, j, where: (where[1 + r], j)),
                  pl.BlockSpec((1, s, tc), lambda r, j, where: (where[5 + r], 0, j))],
        out_specs=pl.BlockSpec((1, s, tc), lambda r, j, where: (r, 0, j)),
        out_shape=jax.ShapeDtypeStruct((3, s, d), WIRE_DTYPE), name=name, args=(grad, swapped), prefetch=(where,))


def _final_sum(where, grad, swapped, received, name, carried=()):
    s, d = swapped.shape[1], swapped.shape[2]
    tc = _tile(d, 512, LANE)

    def body(where_ref, g_ref, sw_ref, r_ref, o_ref):
        acc = g_ref[...].astype(F32) + sw_ref[0].astype(F32)
        for k in range(3):
            acc = acc + r_ref[k].astype(F32)
        o_ref[...] = acc

    return _call(
        body, grid=(d // tc,),
        in_specs=[pl.BlockSpec((s, tc), lambda j, where: (where[0], j)),
                  pl.BlockSpec((1, s, tc), lambda j, where: (where[4], 0, j)),
                  pl.BlockSpec((3, s, tc), lambda j, where: (0, 0, j))],
        out_specs=pl.BlockSpec((s, tc), lambda j, where: (0, j)),
        out_shape=jax.ShapeDtypeStruct((s, d), F32), name=name, args=(grad, swapped, received),
        prefetch=(where,), carried=carried)


class _GradReduction:
    def __init__(self, key, grad, shard_rows, where):
        self.key, self.grad, self.rows, self.where = key, grad, shard_rows, where
        self._presum = self._exchange = None

    def swap(self):
        self._swap = _CarriedSwap([self.grad], [self.rows])
        return self._swap

    def exchange(self, part=None):
        if self._presum is None:
            self._presum = _presum(self.where, self.grad, self._swap.results[0], "presum_" + self.key)
        rows = None
        if part is not None:
            half = _round_up(self.rows // 2, SUBLANE_BF16)
            rows = (0, half) if part == 0 else (half, self.rows - half)
        into = self._exchange.results[0] if part == 1 else None
        self._exchange = _CarriedChipExchange([self._presum], rows, into)
        return self._exchange

    def total(self, carried=()):
        return _final_sum(self.where, self.grad, self._swap.results[0], self._exchange.results[0],
                          "sum_" + self.key, carried)

    def total_and_update(self, w, m, v):
        return _sum_adamw(self.where, self.grad, self._swap.results[0], self._exchange.results[0], w, m, v,
                          "update_" + self.key)


def _adamw_math(w, g, m, v):
    nm = ADAM_B1 * m + (1.0 - ADAM_B1) * g
    nv = ADAM_B2 * v + (1.0 - ADAM_B2) * (g * g)
    m_hat = nm / (1.0 - ADAM_B1 ** ADAM_STEP)
    v_hat = nv / (1.0 - ADAM_B2 ** ADAM_STEP)
    return -ADAM_LR * (m_hat / (jnp.sqrt(v_hat) + ADAM_EPS) + ADAM_WD * w), nm, nv


def _sum_adamw(where, grad, swapped, received, w, m, v, name):
    s, d = swapped.shape[1], swapped.shape[2]
    tc = _tile(d, 512, LANE)

    def body(where_ref, g_ref, sw_ref, r_ref, w_ref, m_ref, v_ref, gs_ref, d_ref, nm_ref, nv_ref):
        g = g_ref[...].astype(F32) + sw_ref[0].astype(F32)
        for k in range(3):
            g = g + r_ref[k].astype(F32)
        gs_ref[...] = g
        d_ref[...], nm_ref[...], nv_ref[...] = _adamw_math(w_ref[...], g, m_ref[...], v_ref[...])

    blk = pl.BlockSpec((s, tc), lambda j, where: (0, j))
    return _call(
        body, grid=(d // tc,),
        in_specs=[pl.BlockSpec((s, tc), lambda j, where: (where[0], j)),
                  pl.BlockSpec((1, s, tc), lambda j, where: (where[4], 0, j)),
                  pl.BlockSpec((3, s, tc), lambda j, where: (0, 0, j)), blk, blk, blk],
        out_specs=[blk] * 4, out_shape=[jax.ShapeDtypeStruct((s, d), F32)] * 4, name=name,
        args=(grad, swapped, received, w, m, v), prefetch=(where,))


def _adamw(w, g, m, v, name):
    rows, cols = w.shape
    tr = _tile(rows, 256, 8)

    def body(w_ref, g_ref, m_ref, v_ref, d_ref, nm_ref, nv_ref):
        d_ref[...], nm_ref[...], nv_ref[...] = _adamw_math(w_ref[...], g_ref[...], m_ref[...], v_ref[...])

    spec = pl.BlockSpec((tr, cols), lambda i: (i, 0))
    return pl.pallas_call(
        body, grid=(rows // tr,), in_specs=[spec] * 4, out_specs=[spec] * 3,
        out_shape=[jax.ShapeDtypeStruct((rows, cols), F32)] * 3, name=name, compiler_params=_params())(w, g, m, v)


def _pack_rows(arrays, width, row_quantum=8):
    flat = jnp.concatenate([a.reshape(-1) for a in arrays])
    total = _round_up(flat.shape[0], row_quantum * width)
    flat = jnp.pad(flat, (0, total - flat.shape[0]))
    return flat.reshape(-1, width)


def _unpack_rows(packed, shapes):
    flat = packed.reshape(-1)
    out = []
    off = 0
    for shp in shapes:
        size = 1
        for s in shp:
            size *= s
        out.append(flat[off:off + size].reshape(shp))
        off += size
    return out


def _block_diag(w):
    h, hb, _ = w.shape
    per = BD // hb
    w4 = w.reshape(h // per, per, hb, hb)
    eye = jnp.eye(per, dtype=w.dtype)
    return jnp.einsum('npij,pq->npiqj', w4, eye).reshape(h // per, BD, BD)


def _block_diag_extract(bd, hb):
    nbk = bd.shape[0]
    per = BD // hb
    b5 = bd.reshape(nbk, per, hb, per, hb)
    eye = jnp.eye(per, dtype=bd.dtype)
    return jnp.einsum('npiqj,pq->npij', b5, eye).reshape(nbk * per, hb, hb)


def kernel(x, meta_tokens, ffn1_pre_g, ffn1_w_gate, ffn1_w_up, ffn1_w_down, ffn1_post_g, mix_pre_g, w_in, lru_conv_w, lru_conv_b, lru_w_a, lru_b_a, lru_w_x, lru_b_x, lru_lambda, sconv_w, lru_out_g, sconv_out_g, w_out, mix_post_g, ffn2_pre_g, ffn2_w_gate, ffn2_w_up, ffn2_w_down, ffn2_post_g, loss_target, m_meta_tokens, m_ffn1_pre_g, m_ffn1_w_gate, m_ffn1_w_up, m_ffn1_w_down, m_ffn1_post_g, m_mix_pre_g, m_w_in, m_lru_conv_w, m_lru_conv_b, m_lru_w_a, m_lru_b_a, m_lru_w_x, m_lru_b_x, m_lru_lambda, m_sconv_w, m_lru_out_g, m_sconv_out_g, m_w_out, m_mix_post_g, m_ffn2_pre_g, m_ffn2_w_gate, m_ffn2_w_up, m_ffn2_w_down, m_ffn2_post_g, v_meta_tokens, v_ffn1_pre_g, v_ffn1_w_gate, v_ffn1_w_up, v_ffn1_w_down, v_ffn1_post_g, v_mix_pre_g, v_w_in, v_lru_conv_w, v_lru_conv_b, v_lru_w_a, v_lru_b_a, v_lru_w_x, v_lru_b_x, v_lru_lambda, v_sconv_w, v_lru_out_g, v_sconv_out_g, v_w_out, v_mix_post_g, v_ffn2_pre_g, v_ffn2_w_gate, v_ffn2_w_up, v_ffn2_w_down, v_ffn2_post_g):
    given = dict(locals())
    wts = {n: given[n] for n in WEIGHT_NAMES}
    mom = {n: given["m_" + n] for n in WEIGHT_NAMES}
    var = {n: given["v_" + n] for n in WEIGHT_NAMES}

    xi, yi, ci = _position()
    me = _block_of(xi, yi, ci)
    x2 = x[0]
    seq, d = x2.shape
    n_meta = meta_tokens.shape[0]
    m_rows = _round_up(n_meta + seq, ROW_ALIGN)
    pad = m_rows - n_meta - seq
    lead = pad + n_meta
    c = lru_conv_b.shape[1]
    hb = lru_w_a.shape[-1]
    dm = meta_tokens.shape[1]
    cs_ = lru_conv_w.shape[2]
    kw4, kw3 = lru_conv_w.shape[1], sconv_w.shape[1]
    assert d == 2 * c and c % BD == 0 and BD % hb == 0 and cs_ <= dm and kw4 == 4 and kw3 == 3

    small = jnp.zeros((_round_up(n_meta + kw4 + kw3, 8), dm), F32)
    small = small.at[0:n_meta].set(meta_tokens)
    small = small.at[n_meta:n_meta + kw4, 0:cs_].set(lru_conv_w[0])
    small = small.at[n_meta + kw4:n_meta + kw4 + kw3, 0:cs_].set(sconv_w[0])
    sr = small.shape[0]
    small_all = _gather_small(small, False, "gather_small").reshape(N_DEV, sr, dm)
    meta_full = small_all[:, 0:n_meta, :].transpose(1, 0, 2).reshape(n_meta, d)
    conv_w_full = small_all[:, n_meta:n_meta + kw4, 0:cs_].transpose(1, 0, 2).reshape(kw4, c)
    sconv_w_full = small_all[:, n_meta + kw4:n_meta + kw4 + kw3, 0:cs_].transpose(1, 0, 2).reshape(kw3, c)

    big = ['ffn1_w_gate', 'ffn1_w_up', 'ffn1_w_down', 'w_in', 'w_out', 'ffn2_w_gate', 'ffn2_w_up', 'ffn2_w_down']
    col_sharded = {'ffn1_w_gate', 'ffn1_w_up', 'w_in', 'ffn2_w_gate', 'ffn2_w_up'}
    shards = []
    for nme in big:
        w = wts[nme][0].astype(WIRE_DTYPE)
        shards.append(w.T if nme in col_sharded else w)
    shard_rows = dict(zip(big, [s.shape[0] for s in shards]))
    zeros = jnp.zeros((F_ALIGN, d), WIRE_DTYPE)

    def gather(forward_at, *names, part=None, into=None):
        sel = [shards[big.index(nme)] for nme in names]
        padded = [_round_up(N_DEV * shard_rows[nme], LANE if nme in ('w_in', 'w_out') else F_ALIGN) for nme in names]
        return _CarriedGather(sel, padded, zeros, forward_at, part, into)

    pv = jnp.zeros((16, c), F32)
    pv = pv.at[0:4].set(conv_w_full).at[4].set(lru_conv_b[0]).at[5].set(lru_b_a[0]).at[6].set(lru_b_x[0])
    pv = pv.at[7].set(lru_lambda[0]).at[8:11].set(sconv_w_full).at[11].set(lru_out_g[0]).at[12].set(sconv_out_g[0])
    wa_bd = _block_diag(lru_w_a[0]).astype(MXU_DTYPE)
    wx_bd = _block_diag(lru_w_x[0]).astype(MXU_DTYPE)
    gs = c // N_GROUPS
    gidx = jnp.arange(BD) // gs
    gm = jnp.where(gidx[:, None] == gidx[None, :], 1.0 / gs, 0.0).astype(MXU_DTYPE)

    ride = gather(0.3, 'ffn1_w_gate')
    h0, n1, target = _embed(x2, meta_full, loss_target[0], ffn1_pre_g, pad, "embed_prenorm", carried=[ride])
    (wg1,) = ride.results
    ride = gather(0.6, 'ffn1_w_up')
    g1 = _mm_nt(n1, wg1, "ffn1_gate", carried=[ride], out_dtype=MXU_DTYPE)
    (wu1,) = ride.results
    ride = gather(0.5, 'ffn1_w_down')
    u1, a1 = _ffn_up_act(n1, wu1, g1, "ffn1_up_act", carried=[ride])
    (wd1,) = ride.results
    ride = gather(0.6, 'w_in', 'w_out')
    fo1, h1, un = _mm_residual_norm(a1, wd1, h0, ffn1_post_g, 0.5, mix_pre_g, "ffn1_down", carried=[ride])
    win_t, wout = ride.results
    s2 = shard_rows['ffn2_w_gate']
    quarter = _round_up(s2 // 4, SUBLANE_BF16)
    ride_g = gather(0.5, 'ffn2_w_gate', part=(0, 3 * quarter))
    z = _mm_nt(un, win_t, "mix_in_proj", carried=[ride_g])
    ride_g = gather(0.5, 'ffn2_w_gate', part=(3 * quarter, s2 - 3 * quarter), into=ride_g.results[0])
    ride_u = gather(0.5, 'ffn2_w_up', part=(0, quarter))
    mixed, hs = _mixer_fwd(z, pv, wa_bd, wx_bd, gm, pad, "mixer_fwd", carried=[ride_g, ride_u])
    (wg2,) = ride_g.results
    ride_u = gather(0.5, 'ffn2_w_up', part=(quarter, s2 - quarter), into=ride_u.results[0])
    o_mix, h2, n2 = _mm_residual_norm(mixed, wout, h1, mix_post_g, 1.0, ffn2_pre_g, "mix_out_proj", carried=[ride_u])
    (wu2,) = ride_u.results
    ride = gather(0.75, 'ffn2_w_down')
    g2, u2, a2 = _ffn_gate_up(n2, wg2, wu2, "ffn2_gate_up", carried=[ride])
    (wd2,) = ride.results
    dh3, dfo2, d_post2, loss_part = _mm_residual_loss(a2, wd2, h2, ffn2_post_g, 0.5, target, lead, "ffn2_down_loss")

    chip_rel = [2 * (xi ^ (r >> 1)) + (yi ^ (r & 1)) for r in range(4)]
    where = jnp.stack([2 * k + ci for k in chip_rel] + chip_rel).astype(jnp.int32)
    red = {}

    def reduction(nme, grad):
        red[nme] = _GradReduction(nme, grad, shard_rows[nme], where)
        return red[nme]

    r_wd2 = reduction('ffn2_w_down', _mm_tn(a2, dfo2, "ffn2_dw_down"))
    dg2, du2 = _ffn_hidden_bwd(dfo2, wd2, g2, u2, "ffn2_hidden_bwd", carried=[r_wd2.swap()])
    r_wg2 = reduction('ffn2_w_gate', _mm_tn(dg2, n2, "ffn2_dw_gate", carried=[r_wd2.exchange(part=0)]))
    r_wu2 = reduction('ffn2_w_up', _mm_tn(du2, n2, "ffn2_dw_up", carried=[r_wd2.exchange(part=1), r_wg2.swap()]))
    dh2, d_pre2 = _mm_norm_bwd([(dg2, wg2), (du2, wu2)], h2, ffn2_pre_g, dh3, "ffn2_dx",
                               carried=[r_wg2.exchange(), r_wu2.swap()])
    do_mix, d_mix_post, dmixed = _norm_bwd_mm_nt(o_mix, mix_post_g, dh2, 1.0, wout, "mix_out_proj_bwd")
    r_wout = reduction('w_out', _mm_tn(mixed, do_mix, "mix_dw_out"))
    dz, dpv, dwa_bd, dwx_bd = _mixer_bwd(z, hs, dmixed, pv, wa_bd, wx_bd, gm, pad, "mixer_bwd",
                                         carried=[r_wu2.exchange(), r_wout.swap()])
    r_win = reduction('w_in', _mm_tn(dz, un, "mix_dw_in", carried=[r_wout.exchange()]))
    dh1, d_mix_pre, dfo1, d_post1 = _mm_norm_bwd([(dz, win_t)], h1, mix_pre_g, dh2, "mix_dx", carried=[r_win.swap()],
                                                 post=(fo1, ffn1_post_g, 0.5))
    r_wd1 = reduction('ffn1_w_down', _mm_tn(a1, dfo1, "ffn1_dw_down", carried=[r_win.exchange(part=0)]))
    early_names = ['mix_pre_g', 'mix_post_g', 'ffn2_pre_g', 'ffn2_post_g', 'ffn1_post_g',
                   'lru_conv_b', 'lru_b_a', 'lru_b_x', 'lru_lambda', 'lru_out_g', 'sconv_out_g',
                   'lru_conv_w', 'sconv_w', 'lru_w_a', 'lru_w_x']
    early_parts = [d_mix_pre, d_mix_post, d_pre2, d_post2, d_post1,
                   dpv[4:5], dpv[5:6], dpv[6:7], dpv[7:8], dpv[11:12], dpv[12:13],
                   dpv[0:4], dpv[8:11], _block_diag_extract(dwa_bd, hb), _block_diag_extract(dwx_bd, hb)]
    early_packed = _pack_rows(early_parts, d, SUBLANE_BF16)
    early_ride = _CarriedGather([early_packed], [N_DEV * early_packed.shape[0]], zeros, 0.75)
    dg1, du1 = _ffn_hidden_bwd(dfo1, wd1, g1, u1, "ffn1_hidden_bwd",
                               carried=[r_win.exchange(part=1), r_wd1.swap(), early_ride])
    early_sum = _sum_stack(early_ride.results[0], "sum_small_early")
    r_wg1 = reduction('ffn1_w_gate', _mm_tn(dg1, n1, "ffn1_dw_gate", carried=[r_wd1.exchange(part=0)]))
    r_wu1 = reduction('ffn1_w_up', _mm_tn(du1, n1, "ffn1_dw_up", carried=[r_wd1.exchange(part=1), r_wg1.swap()]))
    row_tile = _norm_bwd_row_tile(m_rows)
    n_tiles = m_rows // row_tile
    half = n_tiles // 2
    assert half >= 1 and half * row_tile >= lead
    dh0_a, d_pre1_a = _mm_norm_bwd([(dg1, wg1), (du1, wu1)], h0, ffn1_pre_g, dh1, "ffn1_dx_a",
                                   carried=[r_wg1.exchange(), r_wu1.swap()], row_tiles=(0, half))
    dh0_b, d_pre1 = _mm_norm_bwd([(dg1, wg1), (du1, wu1)], h0, ffn1_pre_g, dh1, "ffn1_dx_b",
                                 carried=[r_wu1.exchange()], row_tiles=(half, n_tiles - half), dg_init=d_pre1_a)
    grad_x = jnp.concatenate([dh0_a[lead:], dh0_b], axis=0)[None]
    d_meta = dh0_a[pad:lead]

    grads, delta, new_m, new_v = {}, {}, {}, {}
    for nme in big:
        in_shard_layout = nme not in col_sharded or shard_rows[nme] % LANE != 0
        if in_shard_layout:
            view = (lambda t: t[0].T) if nme in col_sharded else (lambda t: t[0])
            back = (lambda t: t.T[None]) if nme in col_sharded else (lambda t: t[None])
            outs = red[nme].total_and_update(view(wts[nme]), view(mom[nme]), view(var[nme]))
            grads[nme], delta[nme], new_m[nme], new_v[nme] = [back(t) for t in outs]
        else:
            grads[nme] = red[nme].total().T[None]
            outs = _adamw(wts[nme][0], grads[nme][0], mom[nme][0], var[nme][0], "adamw_" + nme)
            delta[nme], new_m[nme], new_v[nme] = [t[None] for t in outs]

    late_names = ['ffn1_pre_g', 'meta_tokens']
    late_parts = [d_pre1, d_meta, loss_part]
    late_sum = _gather_small(_pack_rows(late_parts, d), True, "reduce_small_late")
    small_sums = (_unpack_rows(early_sum, [p.shape for p in early_parts])
                  + _unpack_rows(late_sum, [p.shape for p in late_parts]))
    loss = small_sums.pop()[0, 0]
    for nme, gsm in zip(early_names + late_names, small_sums):
        if nme == 'meta_tokens':
            grads[nme] = lax.dynamic_slice_in_dim(gsm, me * dm, dm, axis=1)
        elif nme in ('lru_conv_w', 'sconv_w'):
            grads[nme] = lax.dynamic_slice_in_dim(gsm, me * cs_, cs_, axis=1)[None]
        else:
            grads[nme] = gsm.reshape(wts[nme].shape)

    rest = [n for n in WEIGHT_NAMES if n not in big]
    rest_shapes = [wts[n].shape for n in rest]
    packed = [_pack_rows([src[n] for n in rest], LANE, 256) for src in (wts, grads, mom, var)]
    for out, packed_out in zip((delta, new_m, new_v), _adamw(*packed, "adamw_small")):
        for nme, arr in zip(rest, _unpack_rows(packed_out, rest_shapes)):
            out[nme] = arr

    return (loss, grad_x, *[grads[n] for n in WEIGHT_NAMES], *[delta[n] for n in WEIGHT_NAMES],
            *[new_m[n] for n in WEIGHT_NAMES], *[new_v[n] for n in WEIGHT_NAMES])
```
